```python
import jax, jax.numpy as jnp
from jax import lax
import numpy as np

D_MODEL = 1024
BATCH = 2
SEQ = 8192
DEPTH = 2
DEC_BATCH = 128
DEC_SEQ = 8
PAST_LEN = 8192
PAGE_SIZE = 128

HEAD_DIM = 64
MIX_DIM = D_MODEL
M_HEADS = MIX_DIM // (4 * HEAD_DIM)
R_HEADS = MIX_DIM // (4 * HEAD_DIM)
A_HEADS = MIX_DIM // (2 * HEAD_DIM)
A_KV_HEADS = A_HEADS // 4
M_DIM = M_HEADS * HEAD_DIM
R_DIM = R_HEADS * HEAD_DIM
A_DIM = A_HEADS * HEAD_DIM
A_KV_DIM = A_KV_HEADS * HEAD_DIM
WINDOW = 128
CHUNK = 128
N_BUCKETS = 32
REL_MAX_DIST = 128
ROPE_BASE = 10000.0
NORM_EPS = 1e-6
IN_SPLITS = (M_DIM, M_DIM, M_DIM, M_DIM, M_DIM, M_HEADS, M_HEADS,
             R_DIM, R_DIM, R_DIM, R_DIM,
             A_DIM, A_KV_DIM, A_KV_DIM, A_DIM)
IN_COLS = 5 * M_DIM + 2 * M_HEADS + 4 * R_DIM + 2 * A_DIM + 2 * A_KV_DIM

kernel_name = "hybrid_mlstm_retention_swa_step"


def rms_norm(x, gain):
    xf = x.astype(jnp.float32)
    y = xf * lax.rsqrt(jnp.mean(xf * xf, axis=-1, keepdims=True) + NORM_EPS)
    return (y * gain.astype(jnp.float32)).astype(x.dtype)


def head_norm(h, gain, n_heads):
    B, T, _ = h.shape
    hh = h.reshape(B, T, n_heads, HEAD_DIM)
    hc = hh - jnp.mean(hh, axis=-1, keepdims=True)
    var = jnp.mean(hc * hc, axis=-1, keepdims=True)
    return (hc * lax.rsqrt(var + NORM_EPS)).reshape(B, T, n_heads * HEAD_DIM) * gain.astype(jnp.float32)


def rope(x, pos):
    half = HEAD_DIM // 2
    inv = ROPE_BASE ** (-jnp.arange(half, dtype=jnp.float32) / half)
    ang = pos.astype(jnp.float32)[:, None] * inv[None, :]
    cos = jnp.cos(ang)[None, :, None, :]
    sin = jnp.sin(ang)[None, :, None, :]
    x1, x2 = x[..., :half], x[..., half:]
    return jnp.concatenate([x1 * cos - x2 * sin, x1 * sin + x2 * cos], axis=-1)


def chunk_len(T):
    return CHUNK if T % CHUNK == 0 else T


def to_chunks(a, L):
    B, T, H = a.shape[:3]
    a = a.reshape((B, T // L, L, H) + a.shape[3:])
    return jnp.moveaxis(a, (1, 3), (0, 2))


def from_chunks(a):
    nc, B, H, L, D = a.shape
    return jnp.moveaxis(a, (0, 2), (1, 3)).reshape(B, nc * L, H * D)


def mlstm(q, k, v, i_pre, f_pre, C0, n0, m0):
    T = q.shape[1]
    L = chunk_len(T)
    k = k * HEAD_DIM ** -0.5
    lf = jax.nn.log_sigmoid(f_pre)
    causal = jnp.tril(jnp.ones((L, L), dtype=bool))

    def step(carry, xs):
        C, n, m = carry
        qc, kc, vc, ic, fc = xs
        b = jnp.cumsum(fc, axis=-1)
        d_log = jnp.where(causal, b[..., :, None] - b[..., None, :] + ic[..., None, :], -jnp.inf)
        g = b + m[..., None]
        m_t = jnp.maximum(g, jnp.max(d_log, axis=-1))
        w_intra = jnp.exp(d_log - m_t[..., None]) * jnp.einsum('bhtd,bhsd->bhts', qc, kc)
        w_inter = jnp.exp(g - m_t)
        num = (w_inter[..., None] * jnp.einsum('bhtd,bhde->bhte', qc, C)
               + jnp.einsum('bhts,bhse->bhte', w_intra, vc))
        nq = w_inter * jnp.einsum('bhtd,bhd->bht', qc, n) + jnp.sum(w_intra, axis=-1)
        h = num / jnp.maximum(jnp.abs(nq), jnp.exp(-m_t))[..., None]
        m_new = m_t[..., -1]
        decay = jnp.exp(b[..., -1] + m - m_new)
        w_s = jnp.exp(b[..., -1:] - b + ic - m_new[..., None])
        C_new = decay[..., None, None] * C + jnp.einsum('bhs,bhsd,bhse->bhde', w_s, kc, vc)
        n_new = decay[..., None] * n + jnp.einsum('bhs,bhsd->bhd', w_s, kc)
        return (C_new, n_new, m_new), h

    xs = (to_chunks(q, L), to_chunks(k, L), to_chunks(v, L), to_chunks(i_pre, L), to_chunks(lf, L))
    (C, n, m), h = lax.scan(step, (C0, n0, m0), xs)
    return from_chunks(h), C, n, m


def retention(q, k, v, S0):
    T = q.shape[1]
    L = chunk_len(T)
    k = k * HEAD_DIM ** -0.5
    log_g = jnp.log1p(-jnp.exp2(-5.0 - jnp.arange(R_HEADS, dtype=jnp.float32)))
    idx = jnp.arange(L, dtype=jnp.float32)
    diff = idx[:, None] - idx[None, :]
    d_mat = jnp.where(diff >= 0, jnp.exp(log_g[:, None, None] * jnp.maximum(diff, 0.0)), 0.0)
    inter = jnp.exp(log_g[:, None] * (idx + 1.0))
    tail = jnp.exp(log_g[:, None] * (L - 1.0 - idx))
    full = jnp.exp(log_g * L)

    def step(S, xs):
        qc, kc, vc = xs
        scores = jnp.einsum('bhtd,bhsd->bhts', qc, kc) * d_mat
        o = (jnp.einsum('bhts,bhse->bhte', scores, vc)
             + inter[..., None] * jnp.einsum('bhtd,bhde->bhte', qc, S))
        S_new = full[:, None, None] * S + jnp.einsum('hs,bhsd,bhse->bhde', tail, kc, vc)
        return S_new, o

    S, o = lax.scan(step, S0, (to_chunks(q, L), to_chunks(k, L), to_chunks(v, L)))
    return from_chunks(o), S


def t5_bucket(dist):
    max_exact = N_BUCKETS // 2
    d = np.maximum(dist, 1).astype(np.float32)
    large = max_exact + (np.log(d / max_exact) / np.log(REL_MAX_DIST / max_exact)
                         * (N_BUCKETS - max_exact)).astype(np.int32)
    large = np.minimum(large, N_BUCKETS - 1)
    return np.where(dist < max_exact, dist, large).astype(np.int32)


def window_attention(q, k, v, k_buf, v_buf, pos0, sinks, rel_table):
    B, T = q.shape[:2]
    G = A_HEADS // A_KV_HEADS
    Bq = WINDOW if T % WINDOW == 0 else T
    nb = T // Bq
    K = WINDOW + Bq
    k_all = jnp.concatenate([k_buf, k], axis=1)
    v_all = jnp.concatenate([v_buf, v], axis=1)

    def band(a):
        if nb == 1:
            return a[:, None]
        blocks = a.reshape(B, nb + 1, WINDOW, A_KV_HEADS, HEAD_DIM)
        return jnp.concatenate([blocks[:, :-1], blocks[:, 1:]], axis=2)

    kb, vb = band(k_all), band(v_all)
    qb = q.reshape(B, nb, Bq, A_KV_HEADS, G, HEAD_DIM)
    j = np.arange(Bq)[:, None]
    c = np.arange(K)[None, :]
    dist = WINDOW + j - c
    key_pos = pos0 - WINDOW + np.arange(nb)[:, None, None] * Bq + c[None]
    mask = (dist >= 0) & (dist < WINDOW) & (key_pos >= 0)
    bias = rel_table[t5_bucket(np.clip(dist, 0, None))]
    bias = jnp.transpose(bias, (2, 0, 1)).reshape(A_KV_HEADS, G, Bq, K).astype(jnp.float32)
    s = jnp.einsum('bnqhgd,bnkhd->bnhgqk', qb, kb) * HEAD_DIM ** -0.5 + bias
    s = jnp.where(jnp.asarray(mask)[:, None, None], s, -jnp.inf)
    sink = sinks.astype(jnp.float32).reshape(A_KV_HEADS, G)[None, None, :, :, None]
    m = jnp.maximum(jnp.max(s, axis=-1), sink)
    p = jnp.exp(s - m[..., None])
    denom = jnp.sum(p, axis=-1) + jnp.exp(sink - m)
    o = jnp.einsum('bnhgqk,bnkhd->bnqhgd', p / denom[..., None], vb)
    return o.reshape(B, T, A_DIM), k_all[:, -WINDOW:], v_all[:, -WINDOW:]


def hybrid_layer(x, C0, n0, m0, S0, k_buf, v_buf, pos0,
                 norm_gain, w_in, gate_bias, m_gain, r_gain, sinks, rel_table, w_out):
    B, T, _ = x.shape
    f32 = jnp.float32
    u = rms_norm(x, norm_gain)
    proj = jnp.einsum('btd,dc->btc', u, w_in).astype(f32)
    split_points = [int(s) for s in np.cumsum(IN_SPLITS)[:-1]]
    (mq, mk, mv, mo, mz, mi, mf, rq, rk, rv, rz, aq, ak, av, az) = jnp.split(proj, split_points, axis=-1)

    def heads(a, h):
        return a.reshape(B, T, h, HEAD_DIM)

    i_pre = mi + gate_bias[0].astype(f32)
    f_pre = mf + gate_bias[1].astype(f32)
    h_m, C, n, m = mlstm(heads(mq, M_HEADS), heads(mk, M_HEADS), heads(mv, M_HEADS), i_pre, f_pre,
                         C0.astype(f32), n0.astype(f32), m0.astype(f32))
    h_m = jax.nn.sigmoid(mo) * h_m
    out_m = head_norm(h_m, m_gain, M_HEADS) * jax.nn.silu(mz)

    pos = pos0 + jnp.arange(T, dtype=jnp.int32)
    h_r, S = retention(rope(heads(rq, R_HEADS), pos), rope(heads(rk, R_HEADS), pos),
                       heads(rv, R_HEADS), S0.astype(f32))
    out_r = head_norm(h_r, r_gain, R_HEADS) * jax.nn.silu(rz)

    h_a, kw, vw = window_attention(heads(aq, A_HEADS), heads(ak, A_KV_HEADS), heads(av, A_KV_HEADS),
                                   k_buf.astype(f32), v_buf.astype(f32), pos0, sinks, rel_table)
    out_a = h_a * jax.nn.silu(az)

    mix = jnp.concatenate([out_m, out_r, out_a], axis=-1).astype(x.dtype)
    x = x + jnp.einsum('btc,cd->btd', mix, w_out)
    dt = x.dtype
    return (x, C.astype(dt), n.astype(dt), m.astype(dt), S.astype(dt), kw.astype(dt), vw.astype(dt))


def setup_inputs(seed: int = 0) -> dict:
    key = jax.random.key(seed)
    ks = jax.random.split(key, 18)
    nrm = jax.random.normal
    f32 = jnp.float32
    gate_base = jnp.stack([jnp.zeros((M_HEADS,), f32), jnp.linspace(3.0, 6.0, M_HEADS, dtype=f32)])
    return {
        "x_prompt": nrm(ks[0], (BATCH, SEQ, D_MODEL), f32),
        "x_sample": nrm(ks[1], (DEC_BATCH, DEC_SEQ, D_MODEL), f32),
        "state_mlstm_C": 0.5 * nrm(ks[2], (DEPTH, DEC_BATCH, M_HEADS, HEAD_DIM, HEAD_DIM), f32),
        "state_mlstm_n": 0.5 * nrm(ks[3], (DEPTH, DEC_BATCH, M_HEADS, HEAD_DIM), f32),
        "state_mlstm_m": 0.5 * nrm(ks[4], (DEPTH, DEC_BATCH, M_HEADS), f32),
        "state_ret_S": 0.5 * nrm(ks[5], (DEPTH, DEC_BATCH, R_HEADS, HEAD_DIM, HEAD_DIM), f32),
        "cache_win_k": nrm(ks[6], (DEPTH, DEC_BATCH, WINDOW, A_KV_HEADS, HEAD_DIM), f32),
        "cache_win_v": nrm(ks[7], (DEPTH, DEC_BATCH, WINDOW, A_KV_HEADS, HEAD_DIM), f32),
        "norm_gain": 1.0 + 0.01 * nrm(ks[8], (DEPTH, D_MODEL), f32),
        "w_in": nrm(ks[9], (DEPTH, D_MODEL, IN_COLS), f32) * D_MODEL ** -0.5,
        "mlstm_gate_bias": gate_base[None] + 0.1 * nrm(ks[10], (DEPTH, 2, M_HEADS), f32),
        "mlstm_norm_gain": 1.0 + 0.01 * nrm(ks[11], (DEPTH, M_DIM), f32),
        "ret_norm_gain": 1.0 + 0.01 * nrm(ks[12], (DEPTH, R_DIM), f32),
        "attn_sinks": 0.5 * nrm(ks[13], (DEPTH, A_HEADS), f32),
        "rel_bias_table": 0.5 * nrm(ks[14], (N_BUCKETS, A_HEADS), f32),
        "w_out": nrm(ks[15], (DEPTH, MIX_DIM, D_MODEL), f32) * MIX_DIM ** -0.5,
        "final_norm_gain": 1.0 + 0.01 * nrm(ks[16], (D_MODEL,), f32),
    }


def reference(x_prompt, x_sample, state_mlstm_C, state_mlstm_n, state_mlstm_m, state_ret_S,
              cache_win_k, cache_win_v, norm_gain, w_in, mlstm_gate_bias, mlstm_norm_gain,
              ret_norm_gain, attn_sinks, rel_bias_table, w_out, final_norm_gain):
    f32 = jnp.float32
    Bp = x_prompt.shape[0]
    zC = jnp.zeros((Bp, M_HEADS, HEAD_DIM, HEAD_DIM), f32)
    zn = jnp.zeros((Bp, M_HEADS, HEAD_DIM), f32)
    zm = jnp.zeros((Bp, M_HEADS), f32)
    zS = jnp.zeros((Bp, R_HEADS, HEAD_DIM, HEAD_DIM), f32)
    zkv = jnp.zeros((Bp, WINDOW, A_KV_HEADS, HEAD_DIM), f32)
    xp, xs = x_prompt, x_sample
    p_states = [[] for _ in range(6)]
    s_states = [[] for _ in range(6)]
    for layer in range(DEPTH):
        params = (norm_gain[layer], w_in[layer], mlstm_gate_bias[layer], mlstm_norm_gain[layer],
                  ret_norm_gain[layer], attn_sinks[layer], rel_bias_table, w_out[layer])
        xp, *sp = hybrid_layer(xp, zC, zn, zm, zS, zkv, zkv, 0, *params)
        xs, *ss = hybrid_layer(xs, state_mlstm_C[layer], state_mlstm_n[layer], state_mlstm_m[layer],
                               state_ret_S[layer], cache_win_k[layer], cache_win_v[layer], PAST_LEN, *params)
        for lst, val in zip(p_states, sp):
            lst.append(val)
        for lst, val in zip(s_states, ss):
            lst.append(val)
    y_prompt = rms_norm(xp, final_norm_gain)
    y_sample = rms_norm(xs, final_norm_gain)
    C_p, n_p, m_p, S_p, k_p, v_p = [jnp.stack(l) for l in p_states]
    C_s, n_s, m_s, S_s, k_s, v_s = [jnp.stack(l) for l in s_states]
    return (y_prompt, y_sample, C_p, n_p, m_p, S_p, k_p, v_p, C_s, n_s, m_s, S_s, k_s, v_s)
```

```python
import functools

import numpy as np
import jax
import jax.numpy as jnp
from jax import lax
from jax.experimental import pallas as pl
from jax.experimental.pallas import tpu as pltpu

D_MODEL = 1024
HEAD_DIM = 64
M_HEADS = 4
R_HEADS = 4
A_HEADS = 8
A_KV_HEADS = 2
KV_GROUP = A_HEADS // A_KV_HEADS
M_DIM = M_HEADS * HEAD_DIM
R_DIM = R_HEADS * HEAD_DIM
A_DIM = A_HEADS * HEAD_DIM
A_KV_DIM = A_KV_HEADS * HEAD_DIM
WINDOW = 128
N_BUCKETS = 32
REL_MAX_DIST = 128
ROPE_BASE = 10000.0
NORM_EPS = 1e-6
QK_SCALE = HEAD_DIM ** -0.5

LANES = 128
ROWS = 128
GATE_PAD = LANES

OFF_MQ = 0
OFF_MK = OFF_MQ + M_DIM
OFF_MV = OFF_MK + M_DIM
OFF_MO = OFF_MV + M_DIM
OFF_MZ = OFF_MO + M_DIM
OFF_G = OFF_MZ + M_DIM
OFF_RQ = OFF_G + GATE_PAD
OFF_RK = OFF_RQ + R_DIM
OFF_RV = OFF_RK + R_DIM
OFF_RZ = OFF_RV + R_DIM
OFF_AQ = OFF_RZ + R_DIM
OFF_AK = OFF_AQ + A_DIM
OFF_AV = OFF_AK + A_KV_DIM
OFF_AZ = OFF_AV + A_KV_DIM
P_COLS = OFF_AZ + A_DIM
N_GATES = 2 * M_HEADS
PROJ_COL_BLOCK = 512

PROMPT_ROWS = 512
SAMPLE_SEQS = ROWS // 8
VMEM_LIMIT_BYTES = 56 * 1024 * 1024

f32 = jnp.float32
bf16 = jnp.bfloat16


def _dot(a, b):
    return jnp.dot(a, b, preferred_element_type=f32)


def _dot_nt(a, b):
    return lax.dot_general(a, b, (((1,), (1,)), ((), ())), preferred_element_type=f32)


def _sigmoid(x):
    return 1.0 / (1.0 + jnp.exp(-x))


def _silu(x):
    return x * _sigmoid(x)


def _head_norm(x, gain):
    mu = jnp.mean(x, axis=1, keepdims=True)
    xc = x - mu
    var = jnp.mean(xc * xc, axis=1, keepdims=True)
    return xc * lax.rsqrt(var + NORM_EPS) * gain


def _exact_tril_dot(tril16, x):
    hi = x.astype(bf16)
    r1 = x - hi.astype(f32)
    mid = r1.astype(bf16)
    lo = (r1 - mid.astype(f32)).astype(bf16)
    return _dot(tril16, hi) + _dot(tril16, mid) + _dot(tril16, lo)


def _group_last(x, groups):
    n = x.shape[1]
    glen = ROWS // groups
    x3 = x.reshape(groups, glen, n)
    return jnp.broadcast_to(x3[:, glen - 1:glen, :], (groups, glen, n)).reshape(ROWS, n)


def _state_rows(col, groups):
    glen = ROWS // groups
    wide = jnp.broadcast_to(col, (ROWS, HEAD_DIM)).reshape(groups, glen, HEAD_DIM)
    per_group = wide[:, 0:1, :]
    rows = jnp.broadcast_to(per_group, (groups, HEAD_DIM, HEAD_DIM)).reshape(groups * HEAD_DIM, HEAD_DIM)
    return rows, per_group.reshape(groups, HEAD_DIM)


def _cols_to_mat(cols, lane):
    acc = jnp.zeros((ROWS, LANES), f32)
    for h, c in enumerate(cols):
        acc = jnp.where(lane == h, c, acc)
    return acc


def _rope(x, cos_t, sin_t, lane):
    first_half = (lane % HEAD_DIM) < (HEAD_DIM // 2)
    out = []
    for s in range(x.shape[1] // LANES):
        xs = x[:, s * LANES:(s + 1) * LANES]
        up = pltpu.roll(xs, LANES - HEAD_DIM // 2, axis=1)
        down = pltpu.roll(xs, HEAD_DIM // 2, axis=1)
        out.append(xs * cos_t + jnp.where(first_half, up, down) * sin_t)
    return jnp.concatenate(out, axis=1)


def _layer_kernel(*refs, mode, groups, chunks, final, ret_full):
    sample = mode == "sample"
    it = iter(refs)
    x_ref = next(it)
    ngain_ref, win_ref, gbias_ref, mgain_ref, rgain_ref = next(it), next(it), next(it), next(it), next(it)
    sinks_ref, wout_ref, fgain_ref = next(it), next(it), next(it)
    tril_ref, maskadd_ref, dmat_ref, rc_ref = next(it), next(it), next(it), next(it)
    biasc_ref, biasp_ref, cos_ref, sin_ref = next(it), next(it), next(it), next(it)
    if sample:
        c_in, n_in, m_in, s_in, k_in, v_in = (next(it) for _ in range(6))
    y_ref = next(it)
    c_out, n_out, m_out, s_out, k_out, v_out = (next(it) for _ in range(6))
    proj_sc, mix_sc = next(it), next(it)
    if sample:
        qb_sc, sp_sc, pp_sc, ob_sc = (next(it) for _ in range(4))

    glen = ROWS // groups
    step = pl.program_id(1) if not sample else pl.program_id(0)

    xf = x_ref[...]
    u = xf * lax.rsqrt(jnp.mean(xf * xf, axis=1, keepdims=True) + NORM_EPS) * ngain_ref[...]
    u16 = u.astype(bf16)
    for c0 in range(0, P_COLS, PROJ_COL_BLOCK):
        c1 = min(c0 + PROJ_COL_BLOCK, P_COLS)
        proj_sc[:, c0:c1] = _dot(u16, win_ref[:, c0:c1])

    if not sample:
        @pl.when(step == 0)
        def _():
            c_out[...] = jnp.zeros_like(c_out)
            n_out[...] = jnp.zeros_like(n_out)
            m_out[...] = jnp.zeros_like(m_out)
            s_out[...] = jnp.zeros_like(s_out)
            k_out[...] = jnp.zeros_like(k_out)
            v_out[...] = jnp.zeros_like(v_out)
    else:
        @pl.when(step == 0)
        def _():
            qb_sc[...] = jnp.zeros_like(qb_sc)

    lane = lax.broadcasted_iota(jnp.int32, (ROWS, LANES), 1)

    def chunk_body(ci, carry):
        r0 = pl.multiple_of(ci * ROWS, ROWS)
        rows = pl.ds(r0, ROWS)

        def state_get(ref_in, ref_out, h):
            if sample:
                return ref_in[:, h].reshape(groups * HEAD_DIM, HEAD_DIM)
            return ref_out[0, h]

        def state_set(ref_out, h, val):
            if sample:
                ref_out[:, h] = val.reshape(groups, HEAD_DIM, HEAD_DIM)
            else:
                ref_out[0, h] = val

        if groups > 1:
            r_i = lax.broadcasted_iota(jnp.int32, (ROWS, groups * HEAD_DIM), 0)
            c_i = lax.broadcasted_iota(jnp.int32, (ROWS, groups * HEAD_DIM), 1)
            blk = (r_i // glen) == (c_i // HEAD_DIM)
            r_t = lax.broadcasted_iota(jnp.int32, (groups * HEAD_DIM, ROWS), 0)
            c_t = lax.broadcasted_iota(jnp.int32, (groups * HEAD_DIM, ROWS), 1)
            blk_t = (r_t // HEAD_DIM) == (c_t // glen)

        def q_times_state(qh, st):
            if groups == 1:
                return _dot(qh.astype(bf16), st.astype(bf16))
            qt = jnp.concatenate([qh] * groups, axis=1)
            qt = jnp.where(blk, qt, 0.0)
            return _dot(qt.astype(bf16), st.astype(bf16))

        def state_increment(kt_h, vh16):
            if groups == 1:
                return _dot(kt_h.astype(bf16), vh16)
            kt = jnp.concatenate([kt_h] * groups, axis=0)
            kt = jnp.where(blk_t, kt, 0.0)
            return _dot(kt.astype(bf16), vh16)

        gates = proj_sc[rows, OFF_G:OFF_G + GATE_PAD] + gbias_ref[...]
        lf = jnp.minimum(gates, 0.0) - jnp.log(1.0 + jnp.exp(-jnp.abs(gates)))
        bcum = _exact_tril_dot(tril_ref[...], lf)
        z = jnp.where(lane < M_HEADS, gates, bcum)
        zt = z.T
        zb = pltpu.roll(z, LANES - M_HEADS, axis=1)
        maskadd = maskadd_ref[...]

        dlogs, mintra = [], []
        for h in range(M_HEADS):
            bcol = zb[:, h:h + 1]
            dlog = (bcol - zt[M_HEADS + h:M_HEADS + h + 1, :]) + zt[h:h + 1, :] + maskadd
            dlogs.append(dlog)
            mintra.append(jnp.max(dlog, axis=1, keepdims=True))
        mprev = m_in[rows, :] if sample else m_out[0]
        g_all = zb + mprev
        mt_all = jnp.maximum(g_all, _cols_to_mat(mintra, lane))
        winter_all = jnp.exp(g_all - mt_all)
        emt_all = jnp.exp(-mt_all)
        mlast_all = _group_last(mt_all, groups)
        blast_all = _group_last(zb, groups)
        decay_all = jnp.exp(blast_all + mprev - mlast_all)
        ws_all = jnp.exp(blast_all - zb + z - mlast_all)

        mq = proj_sc[rows, OFF_MQ:OFF_MQ + M_DIM]
        mk = proj_sc[rows, OFF_MK:OFF_MK + M_DIM] * QK_SCALE
        mv = proj_sc[rows, OFF_MV:OFF_MV + M_DIM]
        kw_parts = []
        for h in range(M_HEADS):
            hs = slice(h * HEAD_DIM, (h + 1) * HEAD_DIM)
            kw_parts.append(mk[:, hs] * ws_all[:, h:h + 1])
        kwt = jnp.concatenate(kw_parts, axis=1).T

        hm = []
        for h in range(M_HEADS):
            hs = slice(h * HEAD_DIM, (h + 1) * HEAD_DIM)
            qh, kh, vh = mq[:, hs], mk[:, hs], mv[:, hs]
            vh16 = vh.astype(bf16)
            mt = mt_all[:, h:h + 1]
            winter = winter_all[:, h:h + 1]
            wintra = jnp.exp(dlogs[h] - mt) * _dot_nt(qh.astype(bf16), kh.astype(bf16))
            c_h = state_get(c_in if sample else None, c_out, h)
            num = winter * q_times_state(qh, c_h) + _dot(wintra.astype(bf16), vh16)
            if sample:
                n_g = n_in[h]
                n_rows = jnp.broadcast_to(n_g.reshape(groups, 1, HEAD_DIM),
                                          (groups, glen, HEAD_DIM)).reshape(ROWS, HEAD_DIM)
            else:
                n_g = n_out[0, h:h + 1, :]
                n_rows = n_g
            nq = (winter * jnp.sum(qh * n_rows, axis=1, keepdims=True)
                  + jnp.sum(wintra, axis=1, keepdims=True))
            hm.append(num / jnp.maximum(jnp.abs(nq), emt_all[:, h:h + 1]))
            dec_rows, dec_g = _state_rows(decay_all[:, h:h + 1], groups)
            c_new = dec_rows * c_h + state_increment(kwt[hs, :], vh16)
            state_set(c_out, h, c_new)
            n_new = dec_g * n_g + jnp.sum(kw_parts[h].reshape(groups, glen, HEAD_DIM), axis=1)
            if sample:
                n_out[h] = n_new
            else:
                n_out[0, h:h + 1, :] = n_new
        if sample:
            m_out[rows, :] = mlast_all
        else:
            m_out[0] = mlast_all
        hm = jnp.concatenate(hm, axis=1)
        hm = _sigmoid(proj_sc[rows, OFF_MO:OFF_MO + M_DIM]) * hm
        mgain = mgain_ref[...]
        out_m = jnp.concatenate(
            [_head_norm(hm[:, h * HEAD_DIM:(h + 1) * HEAD_DIM], mgain[:, h * HEAD_DIM:(h + 1) * HEAD_DIM])
             for h in range(M_HEADS)], axis=1)
        out_m = out_m * _silu(proj_sc[rows, OFF_MZ:OFF_MZ + M_DIM])
        mix_sc[rows, 0:M_DIM] = out_m.astype(bf16)

        cos_t = cos_ref[rows, :] if not sample else cos_ref[...]
        sin_t = sin_ref[rows, :] if not sample else sin_ref[...]
        rq = _rope(proj_sc[rows, OFF_RQ:OFF_RQ + R_DIM], cos_t, sin_t, lane)
        rk = _rope(proj_sc[rows, OFF_RK:OFF_RK + R_DIM], cos_t, sin_t, lane) * QK_SCALE
        rv = proj_sc[rows, OFF_RV:OFF_RV + R_DIM]
        rc = rc_ref[...]
        rkt = jnp.concatenate(
            [rk[:, h * HEAD_DIM:(h + 1) * HEAD_DIM] * rc[:, R_HEADS + h:R_HEADS + h + 1]
             for h in range(R_HEADS)], axis=1).T
        hr = []
        for h in range(R_HEADS):
            hs = slice(h * HEAD_DIM, (h + 1) * HEAD_DIM)
            qh, kh = rq[:, hs], rk[:, hs]
            vh16 = rv[:, hs].astype(bf16)
            scores = _dot_nt(qh.astype(bf16), kh.astype(bf16)) * dmat_ref[h]
            s_h = state_get(s_in if sample else None, s_out, h)
            hr.append(_dot(scores.astype(bf16), vh16) + rc[:, h:h + 1] * q_times_state(qh, s_h))
            state_set(s_out, h, ret_full[h] * s_h + state_increment(rkt[hs, :], vh16))
        rgain = rgain_ref[...]
        out_r = jnp.concatenate(
            [_head_norm(hr[h], rgain[:, h * HEAD_DIM:(h + 1) * HEAD_DIM]) for h in range(R_HEADS)], axis=1)
        out_r = out_r * _silu(proj_sc[rows, OFF_RZ:OFF_RZ + R_DIM])
        mix_sc[rows, M_DIM:M_DIM + R_DIM] = out_r.astype(bf16)

        kcur = proj_sc[rows, OFF_AK:OFF_AK + A_KV_DIM]
        vcur = proj_sc[rows, OFF_AV:OFF_AV + A_KV_DIM]
        kcur16, vcur16 = kcur.astype(bf16), vcur.astype(bf16)
        if sample:
            for h in range(A_HEADS):
                off = (h // KV_GROUP) * HEAD_DIM
                qh = proj_sc[rows, OFF_AQ + h * HEAD_DIM:OFF_AQ + (h + 1) * HEAD_DIM]
                qb_sc[:, h * glen:(h + 1) * glen, off:off + HEAD_DIM] = (
                    qh.reshape(groups, glen, HEAD_DIM).astype(bf16))

            def seq_scores(b, c):
                sp = _dot_nt(qb_sc[b], k_in[b].astype(bf16))
                sp_sc[:, pl.ds(pl.multiple_of(b * glen, glen), glen), :] = sp.reshape(A_HEADS, glen, WINDOW)
                return c
            lax.fori_loop(0, groups, seq_scores, 0)
            pen = 0.0
        else:
            kprev16 = k_out[0].astype(bf16)
            vprev16 = v_out[0].astype(bf16)
            first = jnp.logical_and(step == 0, ci == 0)
            pen = jnp.where(first, -jnp.inf, 0.0).astype(f32)

        pcs, dens, o_cur = [], [], []
        for h in range(A_HEADS):
            kv = h // KV_GROUP
            ks = slice(kv * HEAD_DIM, (kv + 1) * HEAD_DIM)
            qh16 = proj_sc[rows, OFF_AQ + h * HEAD_DIM:OFF_AQ + (h + 1) * HEAD_DIM].astype(bf16)
            sc = _dot_nt(qh16, kcur16[:, ks]) * QK_SCALE + biasc_ref[h]
            if sample:
                sp = sp_sc[h] * QK_SCALE + biasp_ref[h]
            else:
                sp = _dot_nt(qh16, kprev16[:, ks]) * QK_SCALE + biasp_ref[h] + pen
            sink = sinks_ref[h]
            m = jnp.maximum(jnp.maximum(jnp.max(sc, axis=1, keepdims=True),
                                        jnp.max(sp, axis=1, keepdims=True)), sink)
            pc = jnp.exp(sc - m)
            pp = jnp.exp(sp - m)
            den = (jnp.sum(pc, axis=1, keepdims=True) + jnp.sum(pp, axis=1, keepdims=True)
                   + jnp.exp(sink - m))
            oc = _dot(pc.astype(bf16), vcur16[:, ks])
            if sample:
                pp_sc[:, h * glen:(h + 1) * glen, :] = pp.reshape(groups, glen, WINDOW).astype(bf16)
                o_cur.append(oc)
                dens.append(den)
            else:
                o = (oc + _dot(pp.astype(bf16), vprev16[:, ks])) / den
                o_cur.append(o)
        if sample:
            def seq_out(b, c):
                ob = _dot(pp_sc[b], v_in[b].astype(bf16))
                ob_sc[:, pl.ds(pl.multiple_of(b * glen, glen), glen), :] = ob.reshape(A_HEADS, glen, A_KV_DIM)
                return c
            lax.fori_loop(0, groups, seq_out, 0)
            outs = []
            for h in range(A_HEADS):
                off = (h // KV_GROUP) * HEAD_DIM
                outs.append((o_cur[h] + ob_sc[h][:, off:off + HEAD_DIM]) / dens[h])
            o_cur = outs
            k_out[:, 0:WINDOW - glen, :] = k_in[:, glen:WINDOW, :]
            v_out[:, 0:WINDOW - glen, :] = v_in[:, glen:WINDOW, :]
            k_out[:, WINDOW - glen:WINDOW, :] = kcur.reshape(groups, glen, A_KV_DIM)
            v_out[:, WINDOW - glen:WINDOW, :] = vcur.reshape(groups, glen, A_KV_DIM)
        else:
            k_out[0] = kcur
            v_out[0] = vcur
        out_a = jnp.concatenate(o_cur, axis=1) * _silu(proj_sc[rows, OFF_AZ:OFF_AZ + A_DIM])
        mix_sc[rows, M_DIM + R_DIM:M_DIM + R_DIM + A_DIM] = out_a.astype(bf16)
        return carry

    if chunks == 1:
        chunk_body(0, 0)
    else:
        lax.fori_loop(0, chunks, chunk_body, 0)

    y = x_ref[...] + _dot(mix_sc[...], wout_ref[...])
    if final:
        y = y * lax.rsqrt(jnp.mean(y * y, axis=1, keepdims=True) + NORM_EPS) * fgain_ref[...]
    y_ref[...] = y


def _t5_bucket(dist):
    max_exact = N_BUCKETS // 2
    d = np.maximum(dist, 1).astype(np.float32)
    large = max_exact + (np.log(d / max_exact) / np.log(REL_MAX_DIST / max_exact)
                         * (N_BUCKETS - max_exact)).astype(np.int32)
    large = np.minimum(large, N_BUCKETS - 1)
    return np.where(dist < max_exact, dist, large).astype(np.int32)


def _static_tables(groups):
    glen = ROWS // groups
    r = np.arange(ROWS)
    grp, tau = r // glen, r % glen
    same = grp[:, None] == grp[None, :]
    causal = same & (tau[None, :] <= tau[:, None])
    tril = causal.astype(np.float32)
    maskadd = np.where(causal, 0.0, -np.inf).astype(np.float32)
    log_g = np.log1p(-np.exp2(-5.0 - np.arange(R_HEADS, dtype=np.float64)))
    diff = (tau[:, None] - tau[None, :]).astype(np.float64)
    dmat = np.where(causal[None], np.exp(log_g[:, None, None] * np.maximum(diff, 0.0)[None]), 0.0)
    rc = np.zeros((ROWS, LANES), np.float64)
    rc[:, 0:R_HEADS] = np.exp(log_g[None, :] * (tau[:, None] + 1.0))
    rc[:, R_HEADS:2 * R_HEADS] = np.exp(log_g[None, :] * (glen - 1.0 - tau[:, None]))
    full = tuple(float(v) for v in np.exp(log_g * glen))
    dist_cur = np.where(causal, tau[:, None] - tau[None, :], -1)
    c = np.arange(WINDOW)
    dist_prev = WINDOW + tau[:, None] - c[None, :]
    dist_prev = np.where(dist_prev < WINDOW, dist_prev, -1)
    return dict(tril=tril, maskadd=maskadd, dmat=dmat.astype(np.float32), rc=rc.astype(np.float32),
                full=full, dist_cur=dist_cur, dist_prev=dist_prev)


def _bias_from_table(rel_table, dist):
    bias = rel_table[_t5_bucket(np.clip(dist, 0, None))]
    bias = jnp.transpose(bias, (2, 0, 1)).astype(f32)
    return jnp.where(jnp.asarray(dist >= 0)[None], bias, -jnp.inf)


def _rope_tables(pos):
    half = HEAD_DIM // 2
    inv = ROPE_BASE ** (-jnp.arange(half, dtype=f32) / half)
    ang = pos.astype(f32)[:, None] * inv[None, :]
    cos, sin = jnp.cos(ang), jnp.sin(ang)
    reps = LANES // HEAD_DIM
    cos_t = jnp.tile(jnp.concatenate([cos, cos], axis=1), (1, reps))
    sin_t = jnp.tile(jnp.concatenate([-sin, sin], axis=1), (1, reps))
    return cos_t, sin_t


def _const_spec(shape, nargs):
    zeros = (0,) * len(shape)
    if nargs == 1:
        return pl.BlockSpec(shape, lambda i: zeros)
    return pl.BlockSpec(shape, lambda i, j: zeros)


def _layer_params(p, layer):
    return (p["norm_gain"][layer], p["w_in"][layer], p["gbias"][layer], p["m_gain"][layer],
            p["r_gain"][layer], p["sinks"][layer], p["w_out"][layer], p["fgain"])


def _prompt_layer(x, prm, tabs, bias_c, bias_p, cos_t, sin_t, final):
    B, T, _ = x.shape
    tb = min(PROMPT_ROWS, T)
    chunks = tb // ROWS
    nt = T // tb
    ngain, w_in, gbias, mgain, rgain, sinks, w_out, fgain = prm
    cs = functools.partial(_const_spec, nargs=2)
    in_specs = [
        pl.BlockSpec((None, tb, D_MODEL), lambda b, t: (b, t, 0)),
        cs((1, D_MODEL)), cs((D_MODEL, P_COLS)), cs((1, LANES)), cs((1, M_DIM)), cs((1, R_DIM)),
        pl.BlockSpec(memory_space=pltpu.SMEM), cs((D_MODEL, D_MODEL)), cs((1, D_MODEL)),
        cs((ROWS, ROWS)), cs((ROWS, ROWS)), cs((R_HEADS, ROWS, ROWS)), cs((ROWS, LANES)),
        cs((A_HEADS, ROWS, ROWS)), cs((A_HEADS, ROWS, WINDOW)),
        pl.BlockSpec((tb, LANES), lambda b, t: (t, 0)), pl.BlockSpec((tb, LANES), lambda b, t: (t, 0)),
    ]
    out_shape = (
        jax.ShapeDtypeStruct((B, T, D_MODEL), f32),
        jax.ShapeDtypeStruct((B, M_HEADS, HEAD_DIM, HEAD_DIM), f32),
        jax.ShapeDtypeStruct((B, M_HEADS, HEAD_DIM), f32),
        jax.ShapeDtypeStruct((B, ROWS, LANES), f32),
        jax.ShapeDtypeStruct((B, R_HEADS, HEAD_DIM, HEAD_DIM), f32),
        jax.ShapeDtypeStruct((B, WINDOW, A_KV_DIM), f32),
        jax.ShapeDtypeStruct((B, WINDOW, A_KV_DIM), f32),
    )
    out_specs = (
        pl.BlockSpec((None, tb, D_MODEL), lambda b, t: (b, t, 0)),
        pl.BlockSpec((1, M_HEADS, HEAD_DIM, HEAD_DIM), lambda b, t: (b, 0, 0, 0)),
        pl.BlockSpec((1, M_HEADS, HEAD_DIM), lambda b, t: (b, 0, 0)),
        pl.BlockSpec((1, ROWS, LANES), lambda b, t: (b, 0, 0)),
        pl.BlockSpec((1, R_HEADS, HEAD_DIM, HEAD_DIM), lambda b, t: (b, 0, 0, 0)),
        pl.BlockSpec((1, WINDOW, A_KV_DIM), lambda b, t: (b, 0, 0)),
        pl.BlockSpec((1, WINDOW, A_KV_DIM), lambda b, t: (b, 0, 0)),
    )
    kern = functools.partial(_layer_kernel, mode="prompt", groups=1, chunks=chunks, final=final,
                             ret_full=tabs["full"])
    y, c, n, m, s, k, v = pl.pallas_call(
        kern, grid=(B, nt), in_specs=in_specs, out_specs=out_specs, out_shape=out_shape,
        scratch_shapes=[pltpu.VMEM((tb, P_COLS), f32), pltpu.VMEM((tb, D_MODEL), bf16)],
        compiler_params=pltpu.CompilerParams(dimension_semantics=("arbitrary", "arbitrary"),
                                             vmem_limit_bytes=VMEM_LIMIT_BYTES),
        name="prompt_layer",
    )(x, ngain, w_in, gbias, mgain, rgain, sinks, w_out, fgain,
      tabs["tril"], tabs["maskadd"], tabs["dmat"], tabs["rc"], bias_c, bias_p, cos_t, sin_t)
    k = k.reshape(B, WINDOW, A_KV_HEADS, HEAD_DIM)
    v = v.reshape(B, WINDOW, A_KV_HEADS, HEAD_DIM)
    return y, c, n, m[:, 0, :M_HEADS], s, k, v


def _sample_layer(x, states, prm, tabs, bias_c, bias_p, cos_t, sin_t, final):
    B, T, _ = x.shape
    groups = ROWS // T
    nb = B // groups
    c0, n0, m0, s0, k0, v0 = states
    ngain, w_in, gbias, mgain, rgain, sinks, w_out, fgain = prm
    x2 = x.reshape(B * T, D_MODEL)
    n0t = jnp.transpose(n0, (1, 0, 2))
    m0r = jnp.pad(jnp.repeat(m0, T, axis=0), ((0, 0), (0, LANES - M_HEADS)))
    k0r = k0.reshape(B, WINDOW, A_KV_DIM)
    v0r = v0.reshape(B, WINDOW, A_KV_DIM)
    cs = functools.partial(_const_spec, nargs=1)
    st4 = pl.BlockSpec((groups, M_HEADS, HEAD_DIM, HEAD_DIM), lambda i: (i, 0, 0, 0))
    stn = pl.BlockSpec((M_HEADS, groups, HEAD_DIM), lambda i: (0, i, 0))
    stm = pl.BlockSpec((ROWS, LANES), lambda i: (i, 0))
    stk = pl.BlockSpec((groups, WINDOW, A_KV_DIM), lambda i: (i, 0, 0))
    in_specs = [
        pl.BlockSpec((ROWS, D_MODEL), lambda i: (i, 0)),
        cs((1, D_MODEL)), cs((D_MODEL, P_COLS)), cs((1, LANES)), cs((1, M_DIM)), cs((1, R_DIM)),
        pl.BlockSpec(memory_space=pltpu.SMEM), cs((D_MODEL, D_MODEL)), cs((1, D_MODEL)),
        cs((ROWS, ROWS)), cs((ROWS, ROWS)), cs((R_HEADS, ROWS, ROWS)), cs((ROWS, LANES)),
        cs((A_HEADS, ROWS, ROWS)), cs((A_HEADS, ROWS, WINDOW)),
        cs((ROWS, LANES)), cs((ROWS, LANES)),
        st4, stn, stm, st4, stk, stk,
    ]
    out_shape = (
        jax.ShapeDtypeStruct((B * T, D_MODEL), f32),
        jax.ShapeDtypeStruct((B, M_HEADS, HEAD_DIM, HEAD_DIM), f32),
        jax.ShapeDtypeStruct((M_HEADS, B, HEAD_DIM), f32),
        jax.ShapeDtypeStruct((B * T, LANES), f32),
        jax.ShapeDtypeStruct((B, R_HEADS, HEAD_DIM, HEAD_DIM), f32),
        jax.ShapeDtypeStruct((B, WINDOW, A_KV_DIM), f32),
        jax.ShapeDtypeStruct((B, WINDOW, A_KV_DIM), f32),
    )
    out_specs = (pl.BlockSpec((ROWS, D_MODEL), lambda i: (i, 0)), st4, stn, stm, st4, stk, stk)
    kern = functools.partial(_layer_kernel, mode="sample", groups=groups, chunks=1, final=final,
                             ret_full=tabs["full"])
    y, c, n, m, s, k, v = pl.pallas_call(
        kern, grid=(nb,), in_specs=in_specs, out_specs=out_specs, out_shape=out_shape,
        scratch_shapes=[pltpu.VMEM((ROWS, P_COLS), f32), pltpu.VMEM((ROWS, D_MODEL), bf16),
                        pltpu.VMEM((groups, A_HEADS * T, A_KV_DIM), bf16),
                        pltpu.VMEM((A_HEADS, ROWS, WINDOW), f32),
                        pltpu.VMEM((groups, A_HEADS * T, WINDOW), bf16),
                        pltpu.VMEM((A_HEADS, ROWS, A_KV_DIM), f32)],
        compiler_params=pltpu.CompilerParams(dimension_semantics=("arbitrary",),
                                             vmem_limit_bytes=VMEM_LIMIT_BYTES),
        name="sample_layer",
    )(x2, ngain, w_in, gbias, mgain, rgain, sinks, w_out, fgain,
      tabs["tril"], tabs["maskadd"], tabs["dmat"], tabs["rc"], bias_c, bias_p, cos_t, sin_t,
      c0, n0t, m0r, s0, k0r, v0r)
    y = y.reshape(B, T, D_MODEL)
    n = jnp.transpose(n, (1, 0, 2))
    m = m.reshape(B, T, LANES)[:, 0, :M_HEADS]
    k = k.reshape(B, WINDOW, A_KV_HEADS, HEAD_DIM)
    v = v.reshape(B, WINDOW, A_KV_HEADS, HEAD_DIM)
    return y, c, n, m, s, k, v


def kernel(x_prompt, x_sample, state_mlstm_C, state_mlstm_n, state_mlstm_m, state_ret_S, cache_win_k,
           cache_win_v, norm_gain, w_in, mlstm_gate_bias, mlstm_norm_gain, ret_norm_gain, attn_sinks,
           rel_bias_table, w_out, final_norm_gain):
    depth = w_in.shape[0]
    seq = x_prompt.shape[1]
    dec_seq = x_sample.shape[1]
    past_len = seq
    split = OFF_G + N_GATES
    w_in_p = jnp.concatenate(
        [w_in[:, :, :split], jnp.zeros((depth, D_MODEL, GATE_PAD - N_GATES), w_in.dtype), w_in[:, :, split:]],
        axis=2).astype(bf16)
    gbias = jnp.pad(mlstm_gate_bias.reshape(depth, 1, N_GATES), ((0, 0), (0, 0), (0, LANES - N_GATES)))
    p = dict(norm_gain=norm_gain.reshape(depth, 1, D_MODEL), w_in=w_in_p, gbias=gbias,
             m_gain=mlstm_norm_gain.reshape(depth, 1, M_DIM), r_gain=ret_norm_gain.reshape(depth, 1, R_DIM),
             sinks=attn_sinks, w_out=w_out.astype(bf16), fgain=final_norm_gain.reshape(1, D_MODEL))

    tabs_p = _static_tables(1)
    tabs_s = _static_tables(ROWS // dec_seq)
    bias_cp = _bias_from_table(rel_bias_table, tabs_p["dist_cur"])
    bias_pp = _bias_from_table(rel_bias_table, tabs_p["dist_prev"])
    bias_cs = _bias_from_table(rel_bias_table, tabs_s["dist_cur"])
    bias_ps = _bias_from_table(rel_bias_table, tabs_s["dist_prev"])
    cos_p, sin_p = _rope_tables(jnp.arange(seq, dtype=jnp.int32))
    cos_s, sin_s = _rope_tables(past_len + (jnp.arange(ROWS, dtype=jnp.int32) % dec_seq))

    xp, xs = x_prompt, x_sample
    p_states, s_states = [], []
    for layer in range(depth):
        prm = _layer_params(p, layer)
        final = layer == depth - 1
        xp, *sp = _prompt_layer(xp, prm, tabs_p, bias_cp, bias_pp, cos_p, sin_p, final)
        st = (state_mlstm_C[layer], state_mlstm_n[layer], state_mlstm_m[layer], state_ret_S[layer],
              cache_win_k[layer], cache_win_v[layer])
        xs, *ss = _sample_layer(xs, st, prm, tabs_s, bias_cs, bias_ps, cos_s, sin_s, final)
        p_states.append(sp)
        s_states.append(ss)
    outs_p = [jnp.stack([p_states[l][i] for l in range(depth)]) for i in range(6)]
    outs_s = [jnp.stack([s_states[l][i] for l in range(depth)]) for i in range(6)]
    return (xp, xs, *outs_p, *outs_s)
```

```python
import functools

import numpy as np
import jax
import jax.numpy as jnp
from jax import lax
from jax.experimental import pallas as pl
from jax.experimental.pallas import tpu as pltpu

D_MODEL = 1024
HEAD_DIM = 64
M_HEADS = 4
R_HEADS = 4
A_HEADS = 8
A_KV_HEADS = 2
KV_GROUP = A_HEADS // A_KV_HEADS
M_DIM = M_HEADS * HEAD_DIM
R_DIM = R_HEADS * HEAD_DIM
A_DIM = A_HEADS * HEAD_DIM
A_KV_DIM = A_KV_HEADS * HEAD_DIM
WINDOW = 128
N_BUCKETS = 32
REL_MAX_DIST = 128
ROPE_BASE = 10000.0
NORM_EPS = 1e-6
QK_SCALE = HEAD_DIM ** -0.5

LANES = 128
ROWS = 128
GATE_PAD = LANES
PAIRS = M_HEADS // 2
SPLIT_TERMS = 3

OFF_MQ = 0
OFF_MK = OFF_MQ + M_DIM
OFF_MV = OFF_MK + M_DIM
OFF_MO = OFF_MV + M_DIM
OFF_MZ = OFF_MO + M_DIM
OFF_G = OFF_MZ + M_DIM
OFF_RQ = OFF_G + GATE_PAD
OFF_RK = OFF_RQ + R_DIM
OFF_RV = OFF_RK + R_DIM
OFF_RZ = OFF_RV + R_DIM
OFF_AQ = OFF_RZ + R_DIM
OFF_AK = OFF_AQ + A_DIM
OFF_AV = OFF_AK + A_KV_DIM
OFF_AZ = OFF_AV + A_KV_DIM
P_COLS = OFF_AZ + A_DIM
N_GATES = 2 * M_HEADS
PROJ_COL_BLOCK = 512
ATTN_HEAD_ORDER = tuple(h for j in range(KV_GROUP) for h in (j, KV_GROUP + j))
ATTN_HEAD_POS = tuple(ATTN_HEAD_ORDER.index(h) for h in range(A_HEADS))

PROMPT_ROWS = 512
VMEM_LIMIT_BYTES = 56 * 1024 * 1024

f32 = jnp.float32
bf16 = jnp.bfloat16


def _dot(a, b):
    return jnp.dot(a, b, preferred_element_type=f32)


def _dot_nt(a, b):
    return lax.dot_general(a, b, (((1,), (1,)), ((), ())), preferred_element_type=f32)


def _sigmoid(x):
    return 1.0 / (1.0 + jnp.exp(-x))


def _silu(x):
    return x * _sigmoid(x)


def _log_sigmoid(x):
    return jnp.minimum(x, 0.0) - jnp.log(1.0 + jnp.exp(-jnp.abs(x)))


def _split_parts(x, terms):
    parts, r = [], x
    for i in range(terms):
        p = r.astype(bf16)
        parts.append(p)
        if i + 1 < terms:
            r = r - p.astype(f32)
    return parts


def _split_terms(x, terms=SPLIT_TERMS):
    return jnp.concatenate(_split_parts(x, terms), axis=1)


def _exact_tril_dot(tril3, x):
    return _dot(tril3, jnp.concatenate(_split_parts(x, SPLIT_TERMS), axis=0))


def _rope(x, cos_t, sin_t, first_half):
    up = pltpu.roll(x, LANES - HEAD_DIM // 2, axis=1)
    down = pltpu.roll(x, HEAD_DIM // 2, axis=1)
    return x * cos_t + jnp.where(first_half, up, down) * sin_t


def _rms_project(x_ref, ngain_ref, win_ref, proj_sc):
    xf = x_ref[...]
    u = xf * lax.rsqrt(jnp.mean(xf * xf, axis=1, keepdims=True) + NORM_EPS) * ngain_ref[...]
    u16 = u.astype(bf16)
    for c0 in range(0, P_COLS, PROJ_COL_BLOCK):
        c1 = min(c0 + PROJ_COL_BLOCK, P_COLS)
        proj_sc[:, c0:c1] = _dot(u16, win_ref[:, c0:c1])


def _out_project(x_ref, mix_sc, wout_ref, fgain_ref, y_ref, final):
    y = x_ref[...] + _dot(mix_sc[...], wout_ref[...])
    if final:
        y = y * lax.rsqrt(jnp.mean(y * y, axis=1, keepdims=True) + NORM_EPS) * fgain_ref[...]
    y_ref[...] = y


def _pair_norm(x, jj, gain):
    mean = _dot(_split_terms(x, 2), jj)
    xc = x - mean
    var = _dot(_split_terms(xc * xc, 2), jj)
    return xc * lax.rsqrt(var + NORM_EPS) * gain


def _prompt_kernel(x_ref, ngain_ref, win_ref, gbias_ref, mgain_ref, rgain_ref, sinks_ref, wout_ref,
                   fgain_ref, tril3_ref, maskadd_ref, dmat_ref, rslab_ref, biascp_ref, cos_ref, sin_ref,
                   selh_ref, selp_ref, jj_ref, eye_ref,
                   y_ref, c_out, n_out, m_out, s_out, k_out, v_out,
                   proj_sc, mix_sc, cn_sc, sb_sc, m_sc, kp_sc, vp_sc, *, chunks, final):
    step = pl.program_id(1)
    last_step = pl.num_programs(1) - 1

    _rms_project(x_ref, ngain_ref, win_ref, proj_sc)

    @pl.when(step == 0)
    def _():
        cn_sc[...] = jnp.zeros_like(cn_sc)
        sb_sc[...] = jnp.zeros_like(sb_sc)
        m_sc[...] = jnp.zeros_like(m_sc)
        kp_sc[...] = jnp.zeros_like(kp_sc)
        vp_sc[...] = jnp.zeros_like(vp_sc)

    lane = lax.broadcasted_iota(jnp.int32, (ROWS, LANES), 1)
    row = lax.broadcasted_iota(jnp.int32, (ROWS, LANES), 0)
    left = lane < HEAD_DIM
    first_half = (lane & (HEAD_DIM - 1)) < (HEAD_DIM // 2)
    blockdiag = (row < HEAD_DIM) == left
    row2 = lax.broadcasted_iota(jnp.int32, (ROWS, 2 * LANES), 0)
    lane2w = lax.broadcasted_iota(jnp.int32, (ROWS, 2 * LANES), 1)
    blockdiag2 = (row2 < HEAD_DIM) == ((lane2w & (LANES - 1)) < HEAD_DIM)
    ones_l = jnp.where(left, 1.0, 0.0).astype(bf16)
    ones_r = jnp.where(left, 0.0, 1.0).astype(bf16)
    ones16 = jnp.ones((ROWS, LANES), bf16)

    def half(x, side):
        keep = left if side == 0 else jnp.logical_not(left)
        return jnp.where(keep, x, 0.0).astype(bf16)

    def chunk_body(ci):
        rows = slice(ci * ROWS, (ci + 1) * ROWS)
        eye16 = eye_ref[...]
        jj = jj_ref[...]

        gates = proj_sc[rows, OFF_G:OFF_G + GATE_PAD] + gbias_ref[...]
        bcum = _exact_tril_dot(tril3_ref[...], _log_sigmoid(gates))
        zb = pltpu.roll(bcum, LANES - M_HEADS, axis=1)
        head_col = lane < M_HEADS
        r_mat = jnp.where(head_col, gates - zb, 0.0)
        cm = r_mat
        sh = 1
        while sh < ROWS:
            cm = jnp.where(row >= sh, jnp.maximum(cm, pltpu.roll(cm, sh, axis=0)), cm)
            sh *= 2
        mprev = m_sc[...]
        mx = jnp.maximum(mprev, cm)
        gm = mprev - mx
        em = jnp.where(head_col, -(zb + mx), 0.0)
        mx_last = jnp.broadcast_to(mx[ROWS - 1:ROWS, :], (ROWS, LANES))
        m_sc[...] = jnp.where(head_col, jnp.broadcast_to((zb + mx)[ROWS - 1:ROWS, :], (ROWS, LANES)), 0.0)
        mx_b = _dot_nt(_split_terms(mx), selh_ref[...])
        winter_b = jnp.exp(_dot_nt(_split_terms(gm), selp_ref[...]))
        emt_b = jnp.exp(_dot_nt(_split_terms(em), selp_ref[...]))
        ws_b = jnp.exp(_dot_nt(_split_terms(r_mat - mx_last), selp_ref[...]))
        r_t = r_mat.T
        maskadd = maskadd_ref[...]

        for p in range(PAIRS):
            ps = slice(p * LANES, (p + 1) * LANES)
            q = proj_sc[rows, OFF_MQ + p * LANES:OFF_MQ + (p + 1) * LANES]
            k = proj_sc[rows, OFF_MK + p * LANES:OFF_MK + (p + 1) * LANES] * QK_SCALE
            v = proj_sc[rows, OFF_MV + p * LANES:OFF_MV + (p + 1) * LANES]
            q16, k16, v16 = q.astype(bf16), k.astype(bf16), v.astype(bf16)
            cn = cn_sc[p]
            acc = jnp.concatenate([winter_b[:, ps]] * 2, axis=1) * _dot(q16, cn.astype(bf16))
            for side in range(2):
                h = 2 * p + side
                qk = _dot_nt(half(q, side), k16)
                w = jnp.exp((r_t[h:h + 1, :] + maskadd) - mx_b[:, h * ROWS:(h + 1) * ROWS]) * qk
                vn = jnp.concatenate([half(v, side), ones_l if side == 0 else ones_r], axis=1)
                acc = acc + _dot(w.astype(bf16), vn)
            hh = acc[:, :LANES] / jnp.maximum(jnp.abs(acc[:, LANES:]), emt_b[:, ps])
            kw16 = (k * ws_b[:, ps]).astype(bf16)
            kwt16 = _dot_nt(eye16, kw16).astype(bf16)
            dcn = _dot(kwt16, jnp.concatenate([v16, ones16], axis=1))
            decay = winter_b[ROWS - 1:ROWS, ps]
            cn_sc[p] = (jnp.concatenate([decay, decay], axis=1) * cn
                        + jnp.where(blockdiag2, dcn, 0.0))
            hm = _sigmoid(proj_sc[rows, OFF_MO + p * LANES:OFF_MO + (p + 1) * LANES]) * hh
            out = _pair_norm(hm, jj, mgain_ref[:, ps]) * _silu(
                proj_sc[rows, OFF_MZ + p * LANES:OFF_MZ + (p + 1) * LANES])
            mix_sc[rows, p * LANES:(p + 1) * LANES] = out.astype(bf16)

        cos_t = cos_ref[rows, :]
        sin_t = sin_ref[rows, :]
        for p in range(PAIRS):
            ps = slice(p * LANES, (p + 1) * LANES)
            q = _rope(proj_sc[rows, OFF_RQ + p * LANES:OFF_RQ + (p + 1) * LANES], cos_t, sin_t, first_half)
            k = _rope(proj_sc[rows, OFF_RK + p * LANES:OFF_RK + (p + 1) * LANES], cos_t, sin_t,
                      first_half) * QK_SCALE
            v = proj_sc[rows, OFF_RV + p * LANES:OFF_RV + (p + 1) * LANES]
            q16, k16, v16 = q.astype(bf16), k.astype(bf16), v.astype(bf16)
            sb = sb_sc[p]
            acc = rslab_ref[0, p] * _dot(q16, sb.astype(bf16))
            for side in range(2):
                h = 2 * p + side
                scores = _dot_nt(half(q, side), k16) * dmat_ref[h]
                acc = acc + _dot(scores.astype(bf16), half(v, side))
            kt16 = (k * rslab_ref[1, p]).astype(bf16)
            ktt16 = _dot_nt(eye16, kt16).astype(bf16)
            sb_sc[p] = rslab_ref[2, p] * sb + jnp.where(blockdiag, _dot(ktt16, v16), 0.0)
            out = _pair_norm(acc, jj, rgain_ref[:, ps]) * _silu(
                proj_sc[rows, OFF_RZ + p * LANES:OFF_RZ + (p + 1) * LANES])
            mix_sc[rows, M_DIM + p * LANES:M_DIM + (p + 1) * LANES] = out.astype(bf16)

        kcur = proj_sc[rows, OFF_AK:OFF_AK + A_KV_DIM]
        vcur = proj_sc[rows, OFF_AV:OFF_AV + A_KV_DIM]
        kprev, vprev = kp_sc[...], vp_sc[...]
        kk16 = jnp.concatenate([kcur, kprev], axis=0).astype(bf16)
        vv = [jnp.concatenate([jnp.concatenate([half(vcur, s), ones_l if s == 0 else ones_r], axis=1),
                               jnp.concatenate([half(vprev, s), ones_l if s == 0 else ones_r], axis=1)],
                              axis=0) for s in range(2)]
        if ci == 0:
            pen = jnp.where(step == 0, -jnp.inf, 0.0).astype(f32)
            lane2 = lax.broadcasted_iota(jnp.int32, (1, 2 * ROWS), 1)
            pen_row = jnp.where(lane2 >= ROWS, pen, 0.0)
        else:
            pen_row = None
        for j in range(KV_GROUP):
            q = proj_sc[rows, OFF_AQ + j * LANES:OFF_AQ + (j + 1) * LANES] * QK_SCALE
            acc = None
            esink = []
            for side in range(2):
                h = ATTN_HEAD_ORDER[2 * j + side]
                s = _dot_nt(half(q, side), kk16) + biascp_ref[h]
                if pen_row is not None:
                    s = s + pen_row
                sink = sinks_ref[h]
                m = jnp.maximum(jnp.max(jnp.maximum(s[:, :ROWS], s[:, ROWS:]), axis=1, keepdims=True), sink)
                part = _dot(jnp.exp(s - m).astype(bf16), vv[side])
                acc = part if acc is None else acc + part
                esink.append(jnp.exp(sink - m))
            den = acc[:, LANES:] + jnp.where(left, esink[0], esink[1])
            out = (acc[:, :LANES] / den) * _silu(proj_sc[rows, OFF_AZ + j * LANES:OFF_AZ + (j + 1) * LANES])
            mix_sc[rows, M_DIM + R_DIM + j * LANES:M_DIM + R_DIM + (j + 1) * LANES] = out.astype(bf16)
        kp_sc[...] = kcur
        vp_sc[...] = vcur

    for ci in range(chunks):
        chunk_body(ci)

    _out_project(x_ref, mix_sc, wout_ref, fgain_ref, y_ref, final)

    @pl.when(step == last_step)
    def _():
        for p in range(PAIRS):
            cn = cn_sc[p]
            sb = sb_sc[p]
            n_t = cn[:, LANES:].T
            for side in range(2):
                h = 2 * p + side
                blk = slice(side * HEAD_DIM, (side + 1) * HEAD_DIM)
                c_out[0, h] = cn[blk, blk]
                s_out[0, h] = sb[blk, blk]
                n_out[0, h:h + 1, :] = n_t[side * HEAD_DIM:side * HEAD_DIM + 1, blk]
        m_out[0] = m_sc[...]
        k_out[0] = kp_sc[...]
        v_out[0] = vp_sc[...]


def _staged_prompt_kernel(x_ref, ngain_ref, win_ref, gbias_ref, mgain_ref, rgain_ref, sinks_ref, wout_ref,
                          fgain_ref, tril3_ref, maskadd_ref, dmat2_ref, rslab_ref, biasall_ref, cos_ref,
                          sin_ref, selh_ref, selp_ref, jj_ref, eye_ref,
                          y_ref, c_out, n_out, m_out, s_out, k_out, v_out,
                          proj_sc, mix_sc, cn_sc, sb_sc, m_sc, kp_sc, vp_sc, *, chunks, final):
    step = pl.program_id(1)
    last_step = pl.num_programs(1) - 1

    _rms_project(x_ref, ngain_ref, win_ref, proj_sc)

    @pl.when(step == 0)
    def _():
        cn_sc[...] = jnp.zeros_like(cn_sc)
        sb_sc[...] = jnp.zeros_like(sb_sc)
        m_sc[...] = jnp.zeros_like(m_sc)
        kp_sc[...] = jnp.zeros_like(kp_sc)
        vp_sc[...] = jnp.zeros_like(vp_sc)

    lane = lax.broadcasted_iota(jnp.int32, (ROWS, LANES), 1)
    row = lax.broadcasted_iota(jnp.int32, (ROWS, LANES), 0)
    left = lane < HEAD_DIM
    first_half = (lane & (HEAD_DIM - 1)) < (HEAD_DIM // 2)
    head_col = lane < M_HEADS
    blockdiag = (row < HEAD_DIM) == left
    row2 = lax.broadcasted_iota(jnp.int32, (ROWS, 2 * LANES), 0)
    lane2w = lax.broadcasted_iota(jnp.int32, (ROWS, 2 * LANES), 1)
    left2 = (lane2w & (LANES - 1)) < HEAD_DIM
    blockdiag2 = (row2 < HEAD_DIM) == left2
    ones16 = jnp.ones((ROWS, LANES), bf16)

    def halves(x):
        return jnp.concatenate([jnp.where(left, x, 0.0), jnp.where(left, 0.0, x)], axis=0).astype(bf16)

    def pick(x, mask):
        return jnp.where(mask, x[:ROWS], x[ROWS:])

    def chunk_body(ci):
        rows = slice(ci * ROWS, (ci + 1) * ROWS)

        def proj(off, width=LANES):
            return proj_sc[rows, off:off + width]

        eye16 = eye_ref[...]

        gates = proj(OFF_G) + gbias_ref[...]
        bcum = _exact_tril_dot(tril3_ref[...], _log_sigmoid(gates))

        cos_t, sin_t = cos_ref[rows, :], sin_ref[rows, :]
        r_sc, r_inter, r_v16 = [], [], []
        for p in range(PAIRS):
            q = _rope(proj(OFF_RQ + p * LANES), cos_t, sin_t, first_half)
            k = _rope(proj(OFF_RK + p * LANES), cos_t, sin_t, first_half) * QK_SCALE
            v16 = proj(OFF_RV + p * LANES).astype(bf16)
            sb = sb_sc[p]
            r_sc.append(_dot_nt(halves(q), k.astype(bf16)))
            r_inter.append(_dot(q.astype(bf16), sb.astype(bf16)))
            ktt16 = _dot_nt(eye16, (k * rslab_ref[1, p]).astype(bf16)).astype(bf16)
            sb_sc[p] = rslab_ref[2, p] * sb + jnp.where(blockdiag, _dot(ktt16, v16), 0.0)
            r_v16.append(v16)

        kcur, vcur = proj(OFF_AK), proj(OFF_AV)
        kprev, vprev = kp_sc[...], vp_sc[...]
        kk16 = jnp.concatenate([kcur, kprev], axis=0).astype(bf16)
        vv16 = jnp.concatenate([jnp.concatenate([vcur.astype(bf16), ones16], axis=1),
                                jnp.concatenate([vprev.astype(bf16), ones16], axis=1)], axis=0)
        kp_sc[...] = kcur
        vp_sc[...] = vcur
        a_q = jnp.concatenate([halves(proj(OFF_AQ + j * LANES) * QK_SCALE) for j in range(KV_GROUP)], axis=0)
        a_s = _dot_nt(a_q, kk16) + biasall_ref[...]
        if ci == 0:
            pen = jnp.where(step == 0, -jnp.inf, 0.0).astype(f32)
            a_s = a_s + jnp.where(lax.broadcasted_iota(jnp.int32, (1, 2 * ROWS), 1) >= ROWS, pen, 0.0)

        m_qk, m_qcn, m_k, m_v16, m_cn = [], [], [], [], []
        for p in range(PAIRS):
            q = proj(OFF_MQ + p * LANES)
            k = proj(OFF_MK + p * LANES) * QK_SCALE
            cn = cn_sc[p]
            m_qk.append(_dot_nt(halves(q), k.astype(bf16)))
            m_qcn.append(_dot(q.astype(bf16), cn.astype(bf16)))
            m_k.append(k)
            m_v16.append(proj(OFF_MV + p * LANES).astype(bf16))
            m_cn.append(cn)

        zb = pltpu.roll(bcum, LANES - M_HEADS, axis=1)
        r_mat = jnp.where(head_col, gates - zb, 0.0)
        cm = r_mat
        sh = 1
        while sh < ROWS:
            cm = jnp.where(row >= sh, jnp.maximum(cm, pltpu.roll(cm, sh, axis=0)), cm)
            sh *= 2
        mprev = m_sc[...]
        mx = jnp.maximum(mprev, cm)
        gm = mprev - mx
        em = jnp.where(head_col, -(zb + mx), 0.0)
        mx_last = jnp.broadcast_to(mx[ROWS - 1:ROWS, :], (ROWS, LANES))
        m_sc[...] = jnp.where(head_col, jnp.broadcast_to((zb + mx)[ROWS - 1:ROWS, :], (ROWS, LANES)), 0.0)
        mx_b = _dot_nt(_split_terms(mx), selh_ref[...])
        slabs = jnp.exp(_dot_nt(_split_terms(jnp.concatenate([gm, em, r_mat - mx_last], axis=0)),
                                selp_ref[...]))
        winter_b, emt_b, ws_b = slabs[:ROWS], slabs[ROWS:2 * ROWS], slabs[2 * ROWS:]
        r_t = r_mat.T

        outs = []
        r_acc = []
        for p in range(PAIRS):
            o = _dot((r_sc[p] * dmat2_ref[p]).astype(bf16), r_v16[p])
            r_acc.append(pick(o, left) + rslab_ref[0, p] * r_inter[p])

        a_out = []
        a_p = []
        for blk in range(A_HEADS):
            s = a_s[blk * ROWS:(blk + 1) * ROWS]
            sink = sinks_ref[ATTN_HEAD_ORDER[blk]]
            m = jnp.maximum(jnp.max(jnp.maximum(s[:, :ROWS], s[:, ROWS:]), axis=1, keepdims=True), sink)
            a_p.append(jnp.exp(s - m).astype(bf16))
            a_out.append(jnp.exp(sink - m))
        a_pv = _dot(jnp.concatenate(a_p, axis=0), vv16)

        maskadd = maskadd_ref[...]
        for p in range(PAIRS):
            ps = slice(p * LANES, (p + 1) * LANES)
            w = jnp.concatenate(
                [jnp.exp((r_t[2 * p + side:2 * p + side + 1, :] + maskadd)
                         - mx_b[:, (2 * p + side) * ROWS:(2 * p + side + 1) * ROWS]) for side in range(2)],
                axis=0) * m_qk[p]
            acc = (pick(_dot(w.astype(bf16), jnp.concatenate([m_v16[p], ones16], axis=1)), left2)
                   + jnp.concatenate([winter_b[:, ps]] * 2, axis=1) * m_qcn[p])
            hh = acc[:, :LANES] / jnp.maximum(jnp.abs(acc[:, LANES:]), emt_b[:, ps])
            outs.append(_sigmoid(proj(OFF_MO + p * LANES)) * hh)
            kwt16 = _dot_nt(eye16, (m_k[p] * ws_b[:, ps]).astype(bf16)).astype(bf16)
            dcn = _dot(kwt16, jnp.concatenate([m_v16[p], ones16], axis=1))
            decay = winter_b[ROWS - 1:ROWS, ps]
            cn_sc[p] = jnp.concatenate([decay, decay], axis=1) * m_cn[p] + jnp.where(blockdiag2, dcn, 0.0)
        outs.extend(r_acc)

        x4 = jnp.concatenate(outs, axis=0)
        jj = jj_ref[...]
        xc = x4 - _dot(_split_terms(x4, 2), jj)
        var = _dot(_split_terms(xc * xc, 2), jj)
        y4 = xc * lax.rsqrt(var + NORM_EPS)
        for i in range(2 * PAIRS):
            gain = (mgain_ref if i < PAIRS else rgain_ref)[:, (i % PAIRS) * LANES:(i % PAIRS + 1) * LANES]
            zoff = (OFF_MZ if i < PAIRS else OFF_RZ) + (i % PAIRS) * LANES
            out = y4[i * ROWS:(i + 1) * ROWS] * gain * _silu(proj(zoff))
            mix_sc[rows, i * LANES:(i + 1) * LANES] = out.astype(bf16)
        for j in range(KV_GROUP):
            acc = pick(a_pv[2 * j * ROWS:(2 * j + 2) * ROWS], left2)
            den = acc[:, LANES:] + jnp.where(left, a_out[2 * j], a_out[2 * j + 1])
            out = (acc[:, :LANES] / den) * _silu(proj(OFF_AZ + j * LANES))
            mix_sc[rows, M_DIM + R_DIM + j * LANES:M_DIM + R_DIM + (j + 1) * LANES] = out.astype(bf16)

    for ci in range(chunks):
        chunk_body(ci)

    _out_project(x_ref, mix_sc, wout_ref, fgain_ref, y_ref, final)

    @pl.when(step == last_step)
    def _():
        for p in range(PAIRS):
            cn = cn_sc[p]
            sb = sb_sc[p]
            n_t = cn[:, LANES:].T
            for side in range(2):
                h = 2 * p + side
                blk = slice(side * HEAD_DIM, (side + 1) * HEAD_DIM)
                c_out[0, h] = cn[blk, blk]
                s_out[0, h] = sb[blk, blk]
                n_out[0, h:h + 1, :] = n_t[side * HEAD_DIM:side * HEAD_DIM + 1, blk]
        m_out[0] = m_sc[...]
        k_out[0] = kp_sc[...]
        v_out[0] = vp_sc[...]


def _head_norm(x, gain):
    mu = jnp.mean(x, axis=1, keepdims=True)
    xc = x - mu
    var = jnp.mean(xc * xc, axis=1, keepdims=True)
    return xc * lax.rsqrt(var + NORM_EPS) * gain


def _group_last(x, groups):
    n = x.shape[1]
    glen = ROWS // groups
    x3 = x.reshape(groups, glen, n)
    return jnp.broadcast_to(x3[:, glen - 1:glen, :], (groups, glen, n)).reshape(ROWS, n)


def _state_rows(col, groups):
    glen = ROWS // groups
    wide = jnp.broadcast_to(col, (ROWS, HEAD_DIM)).reshape(groups, glen, HEAD_DIM)
    per_group = wide[:, 0:1, :]
    rows = jnp.broadcast_to(per_group, (groups, HEAD_DIM, HEAD_DIM)).reshape(groups * HEAD_DIM, HEAD_DIM)
    return rows, per_group.reshape(groups, HEAD_DIM)


def _cols_to_mat(cols, lane):
    acc = jnp.zeros((ROWS, LANES), f32)
    for h, c in enumerate(cols):
        acc = jnp.where(lane == h, c, acc)
    return acc


def _sample_kernel(x_ref, ngain_ref, win_ref, gbias_ref, mgain_ref, rgain_ref, sinks_ref, wout_ref,
                   fgain_ref, tril3_ref, maskadd_ref, dmat_ref, rc_ref, biasc_ref, biasp_ref, cos_ref, sin_ref,
                   c_in, n_in, m_in, s_in, k_in, v_in,
                   y_ref, c_out, n_out, m_out, s_out, k_out, v_out,
                   proj_sc, mix_sc, qb_sc, sp_sc, pp_sc, ob_sc, *, groups, final, ret_full):
    glen = ROWS // groups
    glen_log2 = glen.bit_length() - 1
    hd_log2 = HEAD_DIM.bit_length() - 1
    step = pl.program_id(0)

    _rms_project(x_ref, ngain_ref, win_ref, proj_sc)

    @pl.when(step == 0)
    def _():
        qb_sc[...] = jnp.zeros_like(qb_sc)

    lane = lax.broadcasted_iota(jnp.int32, (ROWS, LANES), 1)
    first_half = (lane & (HEAD_DIM - 1)) < (HEAD_DIM // 2)
    rows = slice(0, ROWS)

    r_i = lax.broadcasted_iota(jnp.int32, (ROWS, groups * HEAD_DIM), 0)
    c_i = lax.broadcasted_iota(jnp.int32, (ROWS, groups * HEAD_DIM), 1)
    blk = (r_i >> glen_log2) == (c_i >> hd_log2)
    r_t = lax.broadcasted_iota(jnp.int32, (groups * HEAD_DIM, ROWS), 0)
    c_t = lax.broadcasted_iota(jnp.int32, (groups * HEAD_DIM, ROWS), 1)
    blk_t = (r_t >> hd_log2) == (c_t >> glen_log2)

    def q_times_state(qh, st):
        qt = jnp.where(blk, jnp.concatenate([qh] * groups, axis=1), 0.0)
        return _dot(qt.astype(bf16), st.astype(bf16))

    def state_increment(kt_h, vh16):
        kt = jnp.where(blk_t, jnp.concatenate([kt_h] * groups, axis=0), 0.0)
        return _dot(kt.astype(bf16), vh16)

    gates = proj_sc[rows, OFF_G:OFF_G + GATE_PAD] + gbias_ref[...]
    bcum = _exact_tril_dot(tril3_ref[...], _log_sigmoid(gates))
    z = jnp.where(lane < M_HEADS, gates, bcum)
    zt = z.T
    zb = pltpu.roll(z, LANES - M_HEADS, axis=1)
    maskadd = maskadd_ref[...]

    dlogs, mintra = [], []
    for h in range(M_HEADS):
        bcol = zb[:, h:h + 1]
        dlog = (bcol - zt[M_HEADS + h:M_HEADS + h + 1, :]) + zt[h:h + 1, :] + maskadd
        dlogs.append(dlog)
        mintra.append(jnp.max(dlog, axis=1, keepdims=True))
    mprev = m_in[...]
    g_all = zb + mprev
    mt_all = jnp.maximum(g_all, _cols_to_mat(mintra, lane))
    winter_all = jnp.exp(g_all - mt_all)
    emt_all = jnp.exp(-mt_all)
    mlast_all = _group_last(mt_all, groups)
    blast_all = _group_last(zb, groups)
    decay_all = jnp.exp(blast_all + mprev - mlast_all)
    ws_all = jnp.exp(blast_all - zb + z - mlast_all)

    mq = proj_sc[rows, OFF_MQ:OFF_MQ + M_DIM]
    mk = proj_sc[rows, OFF_MK:OFF_MK + M_DIM] * QK_SCALE
    mv = proj_sc[rows, OFF_MV:OFF_MV + M_DIM]
    kw_parts = []
    for h in range(M_HEADS):
        hs = slice(h * HEAD_DIM, (h + 1) * HEAD_DIM)
        kw_parts.append(mk[:, hs] * ws_all[:, h:h + 1])
    kwt = jnp.concatenate(kw_parts, axis=1).T

    hm = []
    for h in range(M_HEADS):
        hs = slice(h * HEAD_DIM, (h + 1) * HEAD_DIM)
        qh, kh, vh = mq[:, hs], mk[:, hs], mv[:, hs]
        vh16 = vh.astype(bf16)
        mt = mt_all[:, h:h + 1]
        winter = winter_all[:, h:h + 1]
        wintra = jnp.exp(dlogs[h] - mt) * _dot_nt(qh.astype(bf16), kh.astype(bf16))
        c_h = c_in[:, h].reshape(groups * HEAD_DIM, HEAD_DIM)
        num = winter * q_times_state(qh, c_h) + _dot(wintra.astype(bf16), vh16)
        n_g = n_in[h]
        n_rows = jnp.broadcast_to(n_g.reshape(groups, 1, HEAD_DIM),
                                  (groups, glen, HEAD_DIM)).reshape(ROWS, HEAD_DIM)
        nq = (winter * jnp.sum(qh * n_rows, axis=1, keepdims=True)
              + jnp.sum(wintra, axis=1, keepdims=True))
        hm.append(num / jnp.maximum(jnp.abs(nq), emt_all[:, h:h + 1]))
        dec_rows, dec_g = _state_rows(decay_all[:, h:h + 1], groups)
        c_new = dec_rows * c_h + state_increment(kwt[hs, :], vh16)
        c_out[:, h] = c_new.reshape(groups, HEAD_DIM, HEAD_DIM)
        n_out[h] = dec_g * n_g + jnp.sum(kw_parts[h].reshape(groups, glen, HEAD_DIM), axis=1)
    m_out[...] = mlast_all
    hm = jnp.concatenate(hm, axis=1)
    hm = _sigmoid(proj_sc[rows, OFF_MO:OFF_MO + M_DIM]) * hm
    mgain = mgain_ref[...]
    out_m = jnp.concatenate(
        [_head_norm(hm[:, h * HEAD_DIM:(h + 1) * HEAD_DIM], mgain[:, h * HEAD_DIM:(h + 1) * HEAD_DIM])
         for h in range(M_HEADS)], axis=1)
    out_m = out_m * _silu(proj_sc[rows, OFF_MZ:OFF_MZ + M_DIM])
    mix_sc[rows, 0:M_DIM] = out_m.astype(bf16)

    cos_t, sin_t = cos_ref[...], sin_ref[...]
    rq = jnp.concatenate([_rope(proj_sc[rows, OFF_RQ + p * LANES:OFF_RQ + (p + 1) * LANES], cos_t, sin_t,
                                first_half) for p in range(PAIRS)], axis=1)
    rk = jnp.concatenate([_rope(proj_sc[rows, OFF_RK + p * LANES:OFF_RK + (p + 1) * LANES], cos_t, sin_t,
                                first_half) for p in range(PAIRS)], axis=1) * QK_SCALE
    rv = proj_sc[rows, OFF_RV:OFF_RV + R_DIM]
    rc = rc_ref[...]
    rkt = jnp.concatenate(
        [rk[:, h * HEAD_DIM:(h + 1) * HEAD_DIM] * rc[:, R_HEADS + h:R_HEADS + h + 1]
         for h in range(R_HEADS)], axis=1).T
    hr = []
    for h in range(R_HEADS):
        hs = slice(h * HEAD_DIM, (h + 1) * HEAD_DIM)
        qh, kh = rq[:, hs], rk[:, hs]
        vh16 = rv[:, hs].astype(bf16)
        scores = _dot_nt(qh.astype(bf16), kh.astype(bf16)) * dmat_ref[h]
        s_h = s_in[:, h].reshape(groups * HEAD_DIM, HEAD_DIM)
        hr.append(_dot(scores.astype(bf16), vh16) + rc[:, h:h + 1] * q_times_state(qh, s_h))
        s_new = ret_full[h] * s_h + state_increment(rkt[hs, :], vh16)
        s_out[:, h] = s_new.reshape(groups, HEAD_DIM, HEAD_DIM)
    rgain = rgain_ref[...]
    out_r = jnp.concatenate(
        [_head_norm(hr[h], rgain[:, h * HEAD_DIM:(h + 1) * HEAD_DIM]) for h in range(R_HEADS)], axis=1)
    out_r = out_r * _silu(proj_sc[rows, OFF_RZ:OFF_RZ + R_DIM])
    mix_sc[rows, M_DIM:M_DIM + R_DIM] = out_r.astype(bf16)

    kcur = proj_sc[rows, OFF_AK:OFF_AK + A_KV_DIM]
    vcur = proj_sc[rows, OFF_AV:OFF_AV + A_KV_DIM]
    kcur16, vcur16 = kcur.astype(bf16), vcur.astype(bf16)

    def q_cols(h):
        c0 = OFF_AQ + ATTN_HEAD_POS[h] * HEAD_DIM
        return slice(c0, c0 + HEAD_DIM)

    for h in range(A_HEADS):
        off = (h // KV_GROUP) * HEAD_DIM
        qb_sc[:, h * glen:(h + 1) * glen, off:off + HEAD_DIM] = (
            proj_sc[rows, q_cols(h)].reshape(groups, glen, HEAD_DIM).astype(bf16))

    def seq_scores(b, c):
        sp = _dot_nt(qb_sc[b], k_in[b].astype(bf16))
        sp_sc[:, pl.ds(pl.multiple_of(b * glen, glen), glen), :] = sp.reshape(A_HEADS, glen, WINDOW)
        return c
    lax.fori_loop(0, groups, seq_scores, 0)

    dens, o_cur = [], []
    for h in range(A_HEADS):
        kv = h // KV_GROUP
        ks = slice(kv * HEAD_DIM, (kv + 1) * HEAD_DIM)
        qh16 = proj_sc[rows, q_cols(h)].astype(bf16)
        sc = _dot_nt(qh16, kcur16[:, ks]) * QK_SCALE + biasc_ref[h]
        sp = sp_sc[h] * QK_SCALE + biasp_ref[h]
        sink = sinks_ref[h]
        m = jnp.maximum(jnp.maximum(jnp.max(sc, axis=1, keepdims=True),
                                    jnp.max(sp, axis=1, keepdims=True)), sink)
        pc = jnp.exp(sc - m)
        pp = jnp.exp(sp - m)
        dens.append(jnp.sum(pc, axis=1, keepdims=True) + jnp.sum(pp, axis=1, keepdims=True)
                    + jnp.exp(sink - m))
        o_cur.append(_dot(pc.astype(bf16), vcur16[:, ks]))
        pp_sc[:, h * glen:(h + 1) * glen, :] = pp.reshape(groups, glen, WINDOW).astype(bf16)

    def seq_out(b, c):
        ob = _dot(pp_sc[b], v_in[b].astype(bf16))
        ob_sc[:, pl.ds(pl.multiple_of(b * glen, glen), glen), :] = ob.reshape(A_HEADS, glen, A_KV_DIM)
        return c
    lax.fori_loop(0, groups, seq_out, 0)
    outs = []
    for h in ATTN_HEAD_ORDER:
        off = (h // KV_GROUP) * HEAD_DIM
        outs.append((o_cur[h] + ob_sc[h][:, off:off + HEAD_DIM]) / dens[h])
    k_out[:, 0:WINDOW - glen, :] = k_in[:, glen:WINDOW, :]
    v_out[:, 0:WINDOW - glen, :] = v_in[:, glen:WINDOW, :]
    k_out[:, WINDOW - glen:WINDOW, :] = kcur.reshape(groups, glen, A_KV_DIM)
    v_out[:, WINDOW - glen:WINDOW, :] = vcur.reshape(groups, glen, A_KV_DIM)
    out_a = jnp.concatenate(outs, axis=1) * _silu(proj_sc[rows, OFF_AZ:OFF_AZ + A_DIM])
    mix_sc[rows, M_DIM + R_DIM:M_DIM + R_DIM + A_DIM] = out_a.astype(bf16)

    _out_project(x_ref, mix_sc, wout_ref, fgain_ref, y_ref, final)


def _t5_bucket(dist):
    max_exact = N_BUCKETS // 2
    d = np.maximum(dist, 1).astype(np.float32)
    large = max_exact + (np.log(d / max_exact) / np.log(REL_MAX_DIST / max_exact)
                         * (N_BUCKETS - max_exact)).astype(np.int32)
    large = np.minimum(large, N_BUCKETS - 1)
    return np.where(dist < max_exact, dist, large).astype(np.int32)


def _static_tables(groups):
    glen = ROWS // groups
    r = np.arange(ROWS)
    grp, tau = r // glen, r % glen
    causal = (grp[:, None] == grp[None, :]) & (tau[None, :] <= tau[:, None])
    tril = causal.astype(np.float32)
    maskadd = np.where(causal, 0.0, -np.inf).astype(np.float32)
    log_g = np.log1p(-np.exp2(-5.0 - np.arange(R_HEADS, dtype=np.float64)))
    diff = (tau[:, None] - tau[None, :]).astype(np.float64)
    dmat = np.where(causal[None], np.exp(log_g[:, None, None] * np.maximum(diff, 0.0)[None]), 0.0)
    inter = np.exp(log_g[None, :] * (tau[:, None] + 1.0))
    tail = np.exp(log_g[None, :] * (glen - 1.0 - tau[:, None]))
    full = np.exp(log_g * glen)
    rc = np.zeros((ROWS, LANES), np.float64)
    rc[:, 0:R_HEADS] = inter
    rc[:, R_HEADS:2 * R_HEADS] = tail
    lane_head = np.arange(LANES) // HEAD_DIM
    rslab = np.zeros((3, PAIRS, ROWS, LANES), np.float64)
    for p in range(PAIRS):
        rslab[0, p] = inter[:, 2 * p + lane_head]
        rslab[1, p] = tail[:, 2 * p + lane_head]
        rslab[2, p] = full[2 * p + lane_head][None, :]
    selh = np.zeros((M_HEADS * ROWS, SPLIT_TERMS * LANES), np.float32)
    selp = np.zeros((PAIRS * LANES, SPLIT_TERMS * LANES), np.float32)
    for t in range(SPLIT_TERMS):
        for h in range(M_HEADS):
            selh[h * ROWS:(h + 1) * ROWS, t * LANES + h] = 1.0
        for p in range(PAIRS):
            for side in range(2):
                selp[p * LANES + side * HEAD_DIM:p * LANES + (side + 1) * HEAD_DIM, t * LANES + 2 * p + side] = 1.0
    jj = np.zeros((2 * LANES, LANES), np.float32)
    for t in range(2):
        for side in range(2):
            jj[t * LANES + side * HEAD_DIM:t * LANES + (side + 1) * HEAD_DIM,
               side * HEAD_DIM:(side + 1) * HEAD_DIM] = 1.0 / HEAD_DIM
    return dict(tril3=jnp.asarray(np.concatenate([tril] * SPLIT_TERMS, axis=1), bf16),
                maskadd=maskadd, dmat=dmat.astype(np.float32),
                dmat2=dmat.astype(np.float32).reshape(PAIRS, 2 * ROWS, ROWS),
                rc=rc.astype(np.float32), rslab=rslab.astype(np.float32),
                full=tuple(float(v) for v in full), causal=causal, tau=tau,
                selh=jnp.asarray(selh, bf16), selp=jnp.asarray(selp, bf16), jj=jnp.asarray(jj, bf16),
                eye=jnp.asarray(np.eye(ROWS, dtype=np.float32), bf16))


def _bias_tables(rel_table, tabs_s):
    tb = jnp.transpose(rel_table[_t5_bucket(np.arange(WINDOW))]).astype(f32)
    ninf = jnp.full((A_HEADS, WINDOW + 1), -jnp.inf, f32)
    rev = tb[:, :0:-1]

    def skew(u):
        t = jnp.tile(u, (1, WINDOW))[:, :WINDOW * 2 * WINDOW]
        return t.reshape(A_HEADS, WINDOW, 2 * WINDOW)[:, :, :WINDOW]

    bias_cur = skew(jnp.concatenate([tb[:, :1], ninf, rev], axis=1))
    bias_prev = skew(jnp.concatenate([ninf[:, :1], rev, ninf], axis=1))
    bias_cp = jnp.concatenate([bias_cur, bias_prev], axis=2)
    bias_cp = jnp.concatenate([bias_cp[h] for h in ATTN_HEAD_ORDER], axis=0)
    glen = int(tabs_s["tau"].max()) + 1
    reps = ROWS // glen
    bias_cs = jnp.where(jnp.asarray(tabs_s["causal"])[None],
                        jnp.tile(bias_cur[:, :glen, :glen], (1, reps, reps)), -jnp.inf)
    bias_ps = jnp.tile(bias_prev[:, :glen, :], (1, reps, 1))
    return bias_cp, bias_cs, bias_ps


def _rope_tables(pos):
    half = HEAD_DIM // 2
    inv = ROPE_BASE ** (-jnp.arange(half, dtype=f32) / half)
    ang = pos.astype(f32)[:, None] * inv[None, :]
    cos, sin = jnp.cos(ang), jnp.sin(ang)
    reps = LANES // HEAD_DIM
    cos_t = jnp.tile(jnp.concatenate([cos, cos], axis=1), (1, reps))
    sin_t = jnp.tile(jnp.concatenate([-sin, sin], axis=1), (1, reps))
    return cos_t, sin_t


def _const_spec(shape, nargs):
    zeros = (0,) * len(shape)
    if nargs == 1:
        return pl.BlockSpec(shape, lambda i: zeros)
    return pl.BlockSpec(shape, lambda i, j: zeros)


def _layer_spec(shape, layer, nargs):
    idx = (layer,) + (0,) * len(shape)
    if nargs == 1:
        return pl.BlockSpec((None,) + shape, lambda i: idx)
    return pl.BlockSpec((None,) + shape, lambda i, j: idx)


def _param_specs(layer, nargs):
    ls = functools.partial(_layer_spec, layer=layer, nargs=nargs)
    return [ls((1, D_MODEL)), ls((D_MODEL, P_COLS)), ls((1, LANES)), ls((1, M_DIM)), ls((1, R_DIM)),
            pl.BlockSpec(memory_space=pltpu.SMEM), ls((D_MODEL, D_MODEL)), _const_spec((1, D_MODEL), nargs)]


def _param_args(p, layer):
    return (p["norm_gain"], p["w_in"], p["gbias"], p["m_gain"], p["r_gain"], p["sinks"][layer], p["w_out"],
            p["fgain"])


def _prompt_layer(x, p, layer, tabs, bias_cp, cos_t, sin_t, final):
    B, T, _ = x.shape
    tb = min(PROMPT_ROWS, T)
    chunks = tb // ROWS
    nt = T // tb
    cs = functools.partial(_const_spec, nargs=2)
    in_specs = [pl.BlockSpec((None, tb, D_MODEL), lambda b, t: (b, t, 0))] + _param_specs(layer, 2) + [
        cs((ROWS, SPLIT_TERMS * ROWS)), cs((ROWS, ROWS)), cs((PAIRS, 2 * ROWS, ROWS)),
        cs((3, PAIRS, ROWS, LANES)), cs((A_HEADS * ROWS, 2 * ROWS)),
        pl.BlockSpec((tb, LANES), lambda b, t: (t, 0)), pl.BlockSpec((tb, LANES), lambda b, t: (t, 0)),
        cs((M_HEADS * ROWS, SPLIT_TERMS * LANES)), cs((PAIRS * LANES, SPLIT_TERMS * LANES)),
        cs((2 * LANES, LANES)), cs((ROWS, ROWS)),
    ]
    out_shape = (
        jax.ShapeDtypeStruct((B, T, D_MODEL), f32),
        jax.ShapeDtypeStruct((B, M_HEADS, HEAD_DIM, HEAD_DIM), f32),
        jax.ShapeDtypeStruct((B, M_HEADS, HEAD_DIM), f32),
        jax.ShapeDtypeStruct((B, ROWS, LANES), f32),
        jax.ShapeDtypeStruct((B, R_HEADS, HEAD_DIM, HEAD_DIM), f32),
        jax.ShapeDtypeStruct((B, WINDOW, A_KV_DIM), f32),
        jax.ShapeDtypeStruct((B, WINDOW, A_KV_DIM), f32),
    )
    out_specs = (
        pl.BlockSpec((None, tb, D_MODEL), lambda b, t: (b, t, 0)),
        pl.BlockSpec((1, M_HEADS, HEAD_DIM, HEAD_DIM), lambda b, t: (b, 0, 0, 0)),
        pl.BlockSpec((1, M_HEADS, HEAD_DIM), lambda b, t: (b, 0, 0)),
        pl.BlockSpec((1, ROWS, LANES), lambda b, t: (b, 0, 0)),
        pl.BlockSpec((1, R_HEADS, HEAD_DIM, HEAD_DIM), lambda b, t: (b, 0, 0, 0)),
        pl.BlockSpec((1, WINDOW, A_KV_DIM), lambda b, t: (b, 0, 0)),
        pl.BlockSpec((1, WINDOW, A_KV_DIM), lambda b, t: (b, 0, 0)),
    )
    kern = functools.partial(_staged_prompt_kernel, chunks=chunks, final=final)
    y, c, n, m, s, k, v = pl.pallas_call(
        kern, grid=(B, nt), in_specs=in_specs, out_specs=out_specs, out_shape=out_shape,
        scratch_shapes=[pltpu.VMEM((tb, P_COLS), f32), pltpu.VMEM((tb, D_MODEL), bf16),
                        pltpu.VMEM((PAIRS, ROWS, 2 * LANES), f32), pltpu.VMEM((PAIRS, ROWS, LANES), f32),
                        pltpu.VMEM((ROWS, LANES), f32), pltpu.VMEM((ROWS, A_KV_DIM), f32),
                        pltpu.VMEM((ROWS, A_KV_DIM), f32)],
        compiler_params=pltpu.CompilerParams(dimension_semantics=("arbitrary", "arbitrary"),
                                             vmem_limit_bytes=VMEM_LIMIT_BYTES),
        name="prompt_layer",
    )(x, *_param_args(p, layer), tabs["tril3"], tabs["maskadd"], tabs["dmat2"], tabs["rslab"], bias_cp,
      cos_t, sin_t, tabs["selh"], tabs["selp"], tabs["jj"], tabs["eye"])
    k = k.reshape(B, WINDOW, A_KV_HEADS, HEAD_DIM)
    v = v.reshape(B, WINDOW, A_KV_HEADS, HEAD_DIM)
    return y, c, n, m[:, 0, :M_HEADS], s, k, v


def _sample_layer(x, states, p, layer, tabs, bias_c, bias_p, cos_t, sin_t, final):
    B, T, _ = x.shape
    groups = ROWS // T
    nb = B // groups
    c0, n0, m0, s0, k0, v0 = states
    depth = c0.shape[0]
    x2 = x.reshape(B * T, D_MODEL)
    n0t = jnp.transpose(n0[layer], (1, 0, 2))
    m0r = jnp.pad(jnp.repeat(m0[layer], T, axis=0), ((0, 0), (0, LANES - M_HEADS)))
    k0r = k0.reshape(depth, B, WINDOW, A_KV_DIM)
    v0r = v0.reshape(depth, B, WINDOW, A_KV_DIM)
    cs = functools.partial(_const_spec, nargs=1)
    st4_in = pl.BlockSpec((None, groups, M_HEADS, HEAD_DIM, HEAD_DIM), lambda i: (layer, i, 0, 0, 0))
    stk_in = pl.BlockSpec((None, groups, WINDOW, A_KV_DIM), lambda i: (layer, i, 0, 0))
    st4 = pl.BlockSpec((groups, M_HEADS, HEAD_DIM, HEAD_DIM), lambda i: (i, 0, 0, 0))
    stn = pl.BlockSpec((M_HEADS, groups, HEAD_DIM), lambda i: (0, i, 0))
    stm = pl.BlockSpec((ROWS, LANES), lambda i: (i, 0))
    stk = pl.BlockSpec((groups, WINDOW, A_KV_DIM), lambda i: (i, 0, 0))
    in_specs = [pl.BlockSpec((ROWS, D_MODEL), lambda i: (i, 0))] + _param_specs(layer, 1) + [
        cs((ROWS, SPLIT_TERMS * ROWS)), cs((ROWS, ROWS)), cs((R_HEADS, ROWS, ROWS)), cs((ROWS, LANES)),
        cs((A_HEADS, ROWS, ROWS)), cs((A_HEADS, ROWS, WINDOW)),
        cs((ROWS, LANES)), cs((ROWS, LANES)),
        st4_in, stn, stm, st4_in, stk_in, stk_in,
    ]
    out_shape = (
        jax.ShapeDtypeStruct((B * T, D_MODEL), f32),
        jax.ShapeDtypeStruct((B, M_HEADS, HEAD_DIM, HEAD_DIM), f32),
        jax.ShapeDtypeStruct((M_HEADS, B, HEAD_DIM), f32),
        jax.ShapeDtypeStruct((B * T, LANES), f32),
        jax.ShapeDtypeStruct((B, R_HEADS, HEAD_DIM, HEAD_DIM), f32),
        jax.ShapeDtypeStruct((B, WINDOW, A_KV_DIM), f32),
        jax.ShapeDtypeStruct((B, WINDOW, A_KV_DIM), f32),
    )
    out_specs = (pl.BlockSpec((ROWS, D_MODEL), lambda i: (i, 0)), st4, stn, stm, st4, stk, stk)
    kern = functools.partial(_sample_kernel, groups=groups, final=final, ret_full=tabs["full"])
    y, c, n, m, s, k, v = pl.pallas_call(
        kern, grid=(nb,), in_specs=in_specs, out_specs=out_specs, out_shape=out_shape,
        scratch_shapes=[pltpu.VMEM((ROWS, P_COLS), f32), pltpu.VMEM((ROWS, D_MODEL), bf16),
                        pltpu.VMEM((groups, A_HEADS * T, A_KV_DIM), bf16),
                        pltpu.VMEM((A_HEADS, ROWS, WINDOW), f32),
                        pltpu.VMEM((groups, A_HEADS * T, WINDOW), bf16),
                        pltpu.VMEM((A_HEADS, ROWS, A_KV_DIM), f32)],
        compiler_params=pltpu.CompilerParams(dimension_semantics=("arbitrary",),
                                             vmem_limit_bytes=VMEM_LIMIT_BYTES),
        name="sample_layer",
    )(x2, *_param_args(p, layer), tabs["tril3"], tabs["maskadd"], tabs["dmat"], tabs["rc"], bias_c, bias_p,
      cos_t, sin_t, c0, n0t, m0r, s0, k0r, v0r)
    y = y.reshape(B, T, D_MODEL)
    n = jnp.transpose(n, (1, 0, 2))
    m = m.reshape(B, T, LANES)[:, 0, :M_HEADS]
    k = k.reshape(B, WINDOW, A_KV_HEADS, HEAD_DIM)
    v = v.reshape(B, WINDOW, A_KV_HEADS, HEAD_DIM)
    return y, c, n, m, s, k, v


def _prepare_params(norm_gain, w_in, mlstm_gate_bias, mlstm_norm_gain, ret_norm_gain, attn_sinks, w_out,
                    final_norm_gain):
    depth = w_in.shape[0]
    w16 = w_in.astype(bf16)
    split = OFF_G + N_GATES
    aq0 = split + 4 * R_DIM
    akv0 = aq0 + A_DIM
    az0 = akv0 + 2 * A_KV_DIM

    def by_head(w):
        w = w.reshape(w.shape[:-1] + (A_HEADS, HEAD_DIM))
        return w[..., list(ATTN_HEAD_ORDER), :].reshape(w.shape[:-2] + (A_DIM,))

    w_in_p = jnp.concatenate(
        [w16[:, :, :split], jnp.zeros((depth, D_MODEL, GATE_PAD - N_GATES), bf16), w16[:, :, split:aq0],
         by_head(w16[:, :, aq0:akv0]), w16[:, :, akv0:az0], by_head(w16[:, :, az0:])], axis=2)
    wo16 = w_out.astype(bf16)
    a0 = M_DIM + R_DIM
    wo_a = wo16[:, a0:, :].reshape(depth, A_HEADS, HEAD_DIM, D_MODEL)[:, list(ATTN_HEAD_ORDER)]
    w_out_p = jnp.concatenate([wo16[:, :a0, :], wo_a.reshape(depth, A_DIM, D_MODEL)], axis=1)
    gbias = jnp.pad(mlstm_gate_bias.reshape(depth, 1, N_GATES), ((0, 0), (0, 0), (0, LANES - N_GATES)))
    return dict(norm_gain=norm_gain.reshape(depth, 1, D_MODEL), w_in=w_in_p, gbias=gbias,
                m_gain=mlstm_norm_gain.reshape(depth, 1, M_DIM), r_gain=ret_norm_gain.reshape(depth, 1, R_DIM),
                sinks=attn_sinks, w_out=w_out_p, fgain=final_norm_gain.reshape(1, D_MODEL))


def kernel(x_prompt, x_sample, state_mlstm_C, state_mlstm_n, state_mlstm_m, state_ret_S, cache_win_k,
           cache_win_v, norm_gain, w_in, mlstm_gate_bias, mlstm_norm_gain, ret_norm_gain, attn_sinks,
           rel_bias_table, w_out, final_norm_gain):
    depth = w_in.shape[0]
    seq = x_prompt.shape[1]
    dec_seq = x_sample.shape[1]
    past_len = seq
    p = _prepare_params(norm_gain, w_in, mlstm_gate_bias, mlstm_norm_gain, ret_norm_gain, attn_sinks, w_out,
                        final_norm_gain)
    tabs_p = _static_tables(1)
    tabs_s = _static_tables(ROWS // dec_seq)
    bias_cp, bias_cs, bias_ps = _bias_tables(rel_bias_table, tabs_s)
    cos_p, sin_p = _rope_tables(jnp.arange(seq, dtype=jnp.int32))
    cos_s, sin_s = _rope_tables(past_len + (jnp.arange(ROWS, dtype=jnp.int32) % dec_seq))
    states = (state_mlstm_C, state_mlstm_n, state_mlstm_m, state_ret_S, cache_win_k, cache_win_v)

    xp, xs = x_prompt, x_sample
    p_states, s_states = [], []
    for layer in range(depth):
        final = layer == depth - 1
        xp, *sp = _prompt_layer(xp, p, layer, tabs_p, bias_cp, cos_p, sin_p, final)
        xs, *ss = _sample_layer(xs, states, p, layer, tabs_s, bias_cs, bias_ps, cos_s, sin_s, final)
        p_states.append(sp)
        s_states.append(ss)
    outs_p = [jnp.stack([p_states[l][i] for l in range(depth)]) for i in range(6)]
    outs_s = [jnp.stack([s_states[l][i] for l in range(depth)]) for i in range(6)]
    return (xp, xs, *outs_p, *outs_s)
```

```python
import functools

import numpy as np
import jax
import jax.numpy as jnp
from jax import lax
from jax.experimental import pallas as pl
from jax.experimental.pallas import tpu as pltpu

D_MODEL = 1024
HEAD_DIM = 64
M_HEADS = 4
R_HEADS = 4
A_HEADS = 8
A_KV_HEADS = 2
KV_GROUP = A_HEADS // A_KV_HEADS
M_DIM = M_HEADS * HEAD_DIM
R_DIM = R_HEADS * HEAD_DIM
A_DIM = A_HEADS * HEAD_DIM
A_KV_DIM = A_KV_HEADS * HEAD_DIM
WINDOW = 128
N_BUCKETS = 32
REL_MAX_DIST = 128
ROPE_BASE = 10000.0
NORM_EPS = 1e-6
QK_SCALE = HEAD_DIM ** -0.5

LANES = 128
ROWS = 128
GATE_PAD = LANES
PAIRS = M_HEADS // 2
SPLIT_TERMS = 3

OFF_MQ = 0
OFF_MK = OFF_MQ + M_DIM
OFF_MV = OFF_MK + M_DIM
OFF_MO = OFF_MV + M_DIM
OFF_MZ = OFF_MO + M_DIM
OFF_G = OFF_MZ + M_DIM
OFF_RQ = OFF_G + GATE_PAD
OFF_RK = OFF_RQ + R_DIM
OFF_RV = OFF_RK + R_DIM
OFF_RZ = OFF_RV + R_DIM
OFF_AQ = OFF_RZ + R_DIM
OFF_AK = OFF_AQ + A_DIM
OFF_AV = OFF_AK + A_KV_DIM
OFF_AZ = OFF_AV + A_KV_DIM
P_COLS = OFF_AZ + A_DIM
N_GATES = 2 * M_HEADS
PROJ_COL_BLOCK = 512
ATTN_HEAD_ORDER = tuple(h for j in range(KV_GROUP) for h in (j, KV_GROUP + j))
ATTN_HEAD_POS = tuple(ATTN_HEAD_ORDER.index(h) for h in range(A_HEADS))

PROMPT_ROWS = 512
VMEM_LIMIT_BYTES = 56 * 1024 * 1024

f32 = jnp.float32
bf16 = jnp.bfloat16


def _dot(a, b):
    return jnp.dot(a, b, preferred_element_type=f32)


def _dot_nt(a, b):
    return lax.dot_general(a, b, (((1,), (1,)), ((), ())), preferred_element_type=f32)


def _sigmoid(x):
    return 1.0 / (1.0 + jnp.exp(-x))


def _silu(x):
    return x * _sigmoid(x)


def _log_sigmoid(x):
    return jnp.minimum(x, 0.0) - jnp.log(1.0 + jnp.exp(-jnp.abs(x)))


def _split_parts(x, terms):
    parts, r = [], x
    for i in range(terms):
        p = r.astype(bf16)
        parts.append(p)
        if i + 1 < terms:
            r = r - p.astype(f32)
    return parts


def _split_terms(x, terms=SPLIT_TERMS):
    return jnp.concatenate(_split_parts(x, terms), axis=1)


def _exact_tril_dot(tril3, x):
    return _dot(tril3, jnp.concatenate(_split_parts(x, SPLIT_TERMS), axis=0))


def _rope(x, cos_t, sin_t, first_half):
    up = pltpu.roll(x, LANES - HEAD_DIM // 2, axis=1)
    down = pltpu.roll(x, HEAD_DIM // 2, axis=1)
    return x * cos_t + jnp.where(first_half, up, down) * sin_t


def _rms_project(x_ref, ngain_ref, win_ref, proj_sc):
    xf = x_ref[...]
    u = xf * lax.rsqrt(jnp.mean(xf * xf, axis=1, keepdims=True) + NORM_EPS) * ngain_ref[...]
    u16 = u.astype(bf16)
    for c0 in range(0, P_COLS, PROJ_COL_BLOCK):
        c1 = min(c0 + PROJ_COL_BLOCK, P_COLS)
        proj_sc[:, c0:c1] = _dot_nt(u16, win_ref[c0:c1, :])


def _out_project(x_ref, mix_sc, wout_ref, fgain_ref, y_ref, final):
    y = x_ref[...] + _dot(mix_sc[...], wout_ref[...])
    if final:
        y = y * lax.rsqrt(jnp.mean(y * y, axis=1, keepdims=True) + NORM_EPS) * fgain_ref[...]
    y_ref[...] = y


def _pair_norm(x, jj, gain):
    mean = _dot(_split_terms(x, 2), jj)
    xc = x - mean
    var = _dot(_split_terms(xc * xc, 2), jj)
    return xc * lax.rsqrt(var + NORM_EPS) * gain


def _prompt_kernel(x_ref, ngain_ref, win_ref, gbias_ref, mgain_ref, rgain_ref, sinks_ref, wout_ref,
                   fgain_ref, tril3_ref, maskadd_ref, dmat_ref, rslab_ref, biascp_ref, cos_ref, sin_ref,
                   selh_ref, selp_ref, jj_ref, eye_ref,
                   y_ref, c_out, n_out, m_out, s_out, k_out, v_out,
                   proj_sc, mix_sc, cn_sc, sb_sc, m_sc, kp_sc, vp_sc, *, chunks, final):
    step = pl.program_id(1)
    last_step = pl.num_programs(1) - 1

    _rms_project(x_ref, ngain_ref, win_ref, proj_sc)

    @pl.when(step == 0)
    def _():
        cn_sc[...] = jnp.zeros_like(cn_sc)
        sb_sc[...] = jnp.zeros_like(sb_sc)
        m_sc[...] = jnp.zeros_like(m_sc)
        kp_sc[...] = jnp.zeros_like(kp_sc)
        vp_sc[...] = jnp.zeros_like(vp_sc)

    lane = lax.broadcasted_iota(jnp.int32, (ROWS, LANES), 1)
    row = lax.broadcasted_iota(jnp.int32, (ROWS, LANES), 0)
    left = lane < HEAD_DIM
    first_half = (lane & (HEAD_DIM - 1)) < (HEAD_DIM // 2)
    blockdiag = (row < HEAD_DIM) == left
    row2 = lax.broadcasted_iota(jnp.int32, (ROWS, 2 * LANES), 0)
    lane2w = lax.broadcasted_iota(jnp.int32, (ROWS, 2 * LANES), 1)
    blockdiag2 = (row2 < HEAD_DIM) == ((lane2w & (LANES - 1)) < HEAD_DIM)
    ones_l = jnp.where(left, 1.0, 0.0).astype(bf16)
    ones_r = jnp.where(left, 0.0, 1.0).astype(bf16)
    ones16 = jnp.ones((ROWS, LANES), bf16)

    def half(x, side):
        keep = left if side == 0 else jnp.logical_not(left)
        return jnp.where(keep, x, 0.0).astype(bf16)

    def chunk_body(ci):
        rows = slice(ci * ROWS, (ci + 1) * ROWS)
        eye16 = eye_ref[...]
        jj = jj_ref[...]

        gates = proj_sc[rows, OFF_G:OFF_G + GATE_PAD] + gbias_ref[...]
        bcum = _exact_tril_dot(tril3_ref[...], _log_sigmoid(gates))
        zb = pltpu.roll(bcum, LANES - M_HEADS, axis=1)
        head_col = lane < M_HEADS
        r_mat = jnp.where(head_col, gates - zb, 0.0)
        cm = r_mat
        sh = 1
        while sh < ROWS:
            cm = jnp.where(row >= sh, jnp.maximum(cm, pltpu.roll(cm, sh, axis=0)), cm)
            sh *= 2
        mprev = m_sc[...]
        mx = jnp.maximum(mprev, cm)
        gm = mprev - mx
        em = jnp.where(head_col, -(zb + mx), 0.0)
        mx_last = jnp.broadcast_to(mx[ROWS - 1:ROWS, :], (ROWS, LANES))
        m_sc[...] = jnp.where(head_col, jnp.broadcast_to((zb + mx)[ROWS - 1:ROWS, :], (ROWS, LANES)), 0.0)
        mx_b = _dot_nt(_split_terms(mx), selh_ref[...])
        winter_b = jnp.exp(_dot_nt(_split_terms(gm), selp_ref[...]))
        emt_b = jnp.exp(_dot_nt(_split_terms(em), selp_ref[...]))
        ws_b = jnp.exp(_dot_nt(_split_terms(r_mat - mx_last), selp_ref[...]))
        r_t = r_mat.T
        maskadd = maskadd_ref[...]

        for p in range(PAIRS):
            ps = slice(p * LANES, (p + 1) * LANES)
            q = proj_sc[rows, OFF_MQ + p * LANES:OFF_MQ + (p + 1) * LANES]
            k = proj_sc[rows, OFF_MK + p * LANES:OFF_MK + (p + 1) * LANES] * QK_SCALE
            v = proj_sc[rows, OFF_MV + p * LANES:OFF_MV + (p + 1) * LANES]
            q16, k16, v16 = q.astype(bf16), k.astype(bf16), v.astype(bf16)
            cn = cn_sc[p]
            acc = jnp.concatenate([winter_b[:, ps]] * 2, axis=1) * _dot(q16, cn.astype(bf16))
            for side in range(2):
                h = 2 * p + side
                qk = _dot_nt(half(q, side), k16)
                w = jnp.exp((r_t[h:h + 1, :] + maskadd) - mx_b[:, h * ROWS:(h + 1) * ROWS]) * qk
                vn = jnp.concatenate([half(v, side), ones_l if side == 0 else ones_r], axis=1)
                acc = acc + _dot(w.astype(bf16), vn)
            hh = acc[:, :LANES] / jnp.maximum(jnp.abs(acc[:, LANES:]), emt_b[:, ps])
            kw16 = (k * ws_b[:, ps]).astype(bf16)
            kwt16 = _dot_nt(eye16, kw16).astype(bf16)
            dcn = _dot(kwt16, jnp.concatenate([v16, ones16], axis=1))
            decay = winter_b[ROWS - 1:ROWS, ps]
            cn_sc[p] = (jnp.concatenate([decay, decay], axis=1) * cn
                        + jnp.where(blockdiag2, dcn, 0.0))
            hm = _sigmoid(proj_sc[rows, OFF_MO + p * LANES:OFF_MO + (p + 1) * LANES]) * hh
            out = _pair_norm(hm, jj, mgain_ref[:, ps]) * _silu(
                proj_sc[rows, OFF_MZ + p * LANES:OFF_MZ + (p + 1) * LANES])
            mix_sc[rows, p * LANES:(p + 1) * LANES] = out.astype(bf16)

        cos_t = cos_ref[rows, :]
        sin_t = sin_ref[rows, :]
        for p in range(PAIRS):
            ps = slice(p * LANES, (p + 1) * LANES)
            q = _rope(proj_sc[rows, OFF_RQ + p * LANES:OFF_RQ + (p + 1) * LANES], cos_t, sin_t, first_half)
            k = _rope(proj_sc[rows, OFF_RK + p * LANES:OFF_RK + (p + 1) * LANES], cos_t, sin_t,
                      first_half) * QK_SCALE
            v = proj_sc[rows, OFF_RV + p * LANES:OFF_RV + (p + 1) * LANES]
            q16, k16, v16 = q.astype(bf16), k.astype(bf16), v.astype(bf16)
            sb = sb_sc[p]
            acc = rslab_ref[0, p] * _dot(q16, sb.astype(bf16))
            for side in range(2):
                h = 2 * p + side
                scores = _dot_nt(half(q, side), k16) * dmat_ref[h]
                acc = acc + _dot(scores.astype(bf16), half(v, side))
            kt16 = (k * rslab_ref[1, p]).astype(bf16)
            ktt16 = _dot_nt(eye16, kt16).astype(bf16)
            sb_sc[p] = rslab_ref[2, p] * sb + jnp.where(blockdiag, _dot(ktt16, v16), 0.0)
            out = _pair_norm(acc, jj, rgain_ref[:, ps]) * _silu(
                proj_sc[rows, OFF_RZ + p * LANES:OFF_RZ + (p + 1) * LANES])
            mix_sc[rows, M_DIM + p * LANES:M_DIM + (p + 1) * LANES] = out.astype(bf16)

        kcur = proj_sc[rows, OFF_AK:OFF_AK + A_KV_DIM]
        vcur = proj_sc[rows, OFF_AV:OFF_AV + A_KV_DIM]
        kprev, vprev = kp_sc[...], vp_sc[...]
        kk16 = jnp.concatenate([kcur, kprev], axis=0).astype(bf16)
        vv = [jnp.concatenate([jnp.concatenate([half(vcur, s), ones_l if s == 0 else ones_r], axis=1),
                               jnp.concatenate([half(vprev, s), ones_l if s == 0 else ones_r], axis=1)],
                              axis=0) for s in range(2)]
        if ci == 0:
            pen = jnp.where(step == 0, -jnp.inf, 0.0).astype(f32)
            lane2 = lax.broadcasted_iota(jnp.int32, (1, 2 * ROWS), 1)
            pen_row = jnp.where(lane2 >= ROWS, pen, 0.0)
        else:
            pen_row = None
        for j in range(KV_GROUP):
            q = proj_sc[rows, OFF_AQ + j * LANES:OFF_AQ + (j + 1) * LANES] * QK_SCALE
            acc = None
            esink = []
            for side in range(2):
                h = ATTN_HEAD_ORDER[2 * j + side]
                s = _dot_nt(half(q, side), kk16) + biascp_ref[h]
                if pen_row is not None:
                    s = s + pen_row
                sink = sinks_ref[h]
                m = jnp.maximum(jnp.max(jnp.maximum(s[:, :ROWS], s[:, ROWS:]), axis=1, keepdims=True), sink)
                part = _dot(jnp.exp(s - m).astype(bf16), vv[side])
                acc = part if acc is None else acc + part
                esink.append(jnp.exp(sink - m))
            den = acc[:, LANES:] + jnp.where(left, esink[0], esink[1])
            out = (acc[:, :LANES] / den) * _silu(proj_sc[rows, OFF_AZ + j * LANES:OFF_AZ + (j + 1) * LANES])
            mix_sc[rows, M_DIM + R_DIM + j * LANES:M_DIM + R_DIM + (j + 1) * LANES] = out.astype(bf16)
        kp_sc[...] = kcur
        vp_sc[...] = vcur

    for ci in range(chunks):
        chunk_body(ci)

    _out_project(x_ref, mix_sc, wout_ref, fgain_ref, y_ref, final)

    @pl.when(step == last_step)
    def _():
        for p in range(PAIRS):
            cn = cn_sc[p]
            sb = sb_sc[p]
            n_t = cn[:, LANES:].T
            for side in range(2):
                h = 2 * p + side
                blk = slice(side * HEAD_DIM, (side + 1) * HEAD_DIM)
                c_out[0, h] = cn[blk, blk]
                s_out[0, h] = sb[blk, blk]
                n_out[0, h:h + 1, :] = n_t[side * HEAD_DIM:side * HEAD_DIM + 1, blk]
        m_out[0] = m_sc[...]
        k_out[0] = kp_sc[...]
        v_out[0] = vp_sc[...]


def _staged_prompt_kernel(x_ref, ngain_ref, win_ref, gbias_ref, mgain_ref, rgain_ref, sinks_ref, wout_ref,
                          fgain_ref, tril3_ref, maskadd_ref, dmat2_ref, rslab_ref, biasall_ref, cos_ref,
                          sin_ref, selh_ref, selp_ref, jj_ref, eye_ref,
                          y_ref, c_out, n_out, m_out, s_out, k_out, v_out,
                          proj_sc, mix_sc, cn_sc, sb_sc, m_sc, kp_sc, vp_sc, *, chunks, final):
    step = pl.program_id(1)
    last_step = pl.num_programs(1) - 1

    _rms_project(x_ref, ngain_ref, win_ref, proj_sc)

    @pl.when(step == 0)
    def _():
        cn_sc[...] = jnp.zeros_like(cn_sc)
        sb_sc[...] = jnp.zeros_like(sb_sc)
        m_sc[...] = jnp.zeros_like(m_sc)
        kp_sc[...] = jnp.zeros_like(kp_sc)
        vp_sc[...] = jnp.zeros_like(vp_sc)

    lane = lax.broadcasted_iota(jnp.int32, (ROWS, LANES), 1)
    row = lax.broadcasted_iota(jnp.int32, (ROWS, LANES), 0)
    left = lane < HEAD_DIM
    first_half = (lane & (HEAD_DIM - 1)) < (HEAD_DIM // 2)
    head_col = lane < M_HEADS
    blockdiag = (row < HEAD_DIM) == left
    row2 = lax.broadcasted_iota(jnp.int32, (ROWS, 2 * LANES), 0)
    lane2w = lax.broadcasted_iota(jnp.int32, (ROWS, 2 * LANES), 1)
    left2 = (lane2w & (LANES - 1)) < HEAD_DIM
    blockdiag2 = (row2 < HEAD_DIM) == left2
    ones16 = jnp.ones((ROWS, LANES), bf16)

    def halves(x):
        return jnp.concatenate([jnp.where(left, x, 0.0), jnp.where(left, 0.0, x)], axis=0).astype(bf16)

    def pick(x, mask):
        return jnp.where(mask, x[:ROWS], x[ROWS:])

    def chunk_body(ci):
        rows = slice(ci * ROWS, (ci + 1) * ROWS)

        def proj(off, width=LANES):
            return proj_sc[rows, off:off + width]

        eye16 = eye_ref[...]

        gates = proj(OFF_G) + gbias_ref[...]
        bcum = _exact_tril_dot(tril3_ref[...], _log_sigmoid(gates))

        cos_t, sin_t = cos_ref[rows, :], sin_ref[rows, :]
        r_sc, r_inter, r_v16 = [], [], []
        for p in range(PAIRS):
            q = _rope(proj(OFF_RQ + p * LANES), cos_t, sin_t, first_half)
            k = _rope(proj(OFF_RK + p * LANES), cos_t, sin_t, first_half) * QK_SCALE
            v16 = proj(OFF_RV + p * LANES).astype(bf16)
            sb = sb_sc[p]
            r_sc.append(_dot_nt(halves(q), k.astype(bf16)))
            r_inter.append(_dot(q.astype(bf16), sb.astype(bf16)))
            ktt16 = _dot_nt(eye16, (k * rslab_ref[1, p]).astype(bf16)).astype(bf16)
            sb_sc[p] = rslab_ref[2, p] * sb + jnp.where(blockdiag, _dot(ktt16, v16), 0.0)
            r_v16.append(v16)

        kcur, vcur = proj(OFF_AK), proj(OFF_AV)
        kprev, vprev = kp_sc[...], vp_sc[...]
        kk16 = jnp.concatenate([kcur, kprev], axis=0).astype(bf16)
        vv16 = jnp.concatenate([jnp.concatenate([vcur.astype(bf16), ones16], axis=1),
                                jnp.concatenate([vprev.astype(bf16), ones16], axis=1)], axis=0)
        kp_sc[...] = kcur
        vp_sc[...] = vcur
        a_q = jnp.concatenate([halves(proj(OFF_AQ + j * LANES) * QK_SCALE) for j in range(KV_GROUP)], axis=0)
        a_s = _dot_nt(a_q, kk16) + biasall_ref[...]
        if ci == 0:
            pen = jnp.where(step == 0, -jnp.inf, 0.0).astype(f32)
            a_s = a_s + jnp.where(lax.broadcasted_iota(jnp.int32, (1, 2 * ROWS), 1) >= ROWS, pen, 0.0)

        m_qk, m_qcn, m_k, m_v16, m_cn = [], [], [], [], []
        for p in range(PAIRS):
            q = proj(OFF_MQ + p * LANES)
            k = proj(OFF_MK + p * LANES) * QK_SCALE
            cn = cn_sc[p]
            m_qk.append(_dot_nt(halves(q), k.astype(bf16)))
            m_qcn.append(_dot(q.astype(bf16), cn.astype(bf16)))
            m_k.append(k)
            m_v16.append(proj(OFF_MV + p * LANES).astype(bf16))
            m_cn.append(cn)

        zb = pltpu.roll(bcum, LANES - M_HEADS, axis=1)
        r_mat = jnp.where(head_col, gates - zb, 0.0)
        cm = r_mat
        sh = 1
        while sh < ROWS:
            cm = jnp.where(row >= sh, jnp.maximum(cm, pltpu.roll(cm, sh, axis=0)), cm)
            sh *= 2
        mprev = m_sc[...]
        mx = jnp.maximum(mprev, cm)
        gm = mprev - mx
        em = jnp.where(head_col, -(zb + mx), 0.0)
        mx_last = jnp.broadcast_to(mx[ROWS - 1:ROWS, :], (ROWS, LANES))
        m_sc[...] = jnp.where(head_col, jnp.broadcast_to((zb + mx)[ROWS - 1:ROWS, :], (ROWS, LANES)), 0.0)
        mx_b = _dot_nt(_split_terms(mx), selh_ref[...])
        slabs = jnp.exp(_dot_nt(_split_terms(jnp.concatenate([gm, em, r_mat - mx_last], axis=0)),
                                selp_ref[...]))
        winter_b, emt_b, ws_b = slabs[:ROWS], slabs[ROWS:2 * ROWS], slabs[2 * ROWS:]
        r_t = r_mat.T

        outs = []
        r_acc = []
        for p in range(PAIRS):
            o = _dot((r_sc[p] * dmat2_ref[p]).astype(bf16), r_v16[p])
            r_acc.append(pick(o, left) + rslab_ref[0, p] * r_inter[p])

        a_out = []
        a_p = []
        for blk in range(A_HEADS):
            s = a_s[blk * ROWS:(blk + 1) * ROWS]
            sink = sinks_ref[ATTN_HEAD_ORDER[blk]]
            m = jnp.maximum(jnp.max(jnp.maximum(s[:, :ROWS], s[:, ROWS:]), axis=1, keepdims=True), sink)
            a_p.append(jnp.exp(s - m).astype(bf16))
            a_out.append(jnp.exp(sink - m))
        a_pv = _dot(jnp.concatenate(a_p, axis=0), vv16)

        maskadd = maskadd_ref[...]
        for p in range(PAIRS):
            ps = slice(p * LANES, (p + 1) * LANES)
            w = jnp.concatenate(
                [jnp.exp((r_t[2 * p + side:2 * p + side + 1, :] + maskadd)
                         - mx_b[:, (2 * p + side) * ROWS:(2 * p + side + 1) * ROWS]) for side in range(2)],
                axis=0) * m_qk[p]
            acc = (pick(_dot(w.astype(bf16), jnp.concatenate([m_v16[p], ones16], axis=1)), left2)
                   + jnp.concatenate([winter_b[:, ps]] * 2, axis=1) * m_qcn[p])
            hh = acc[:, :LANES] / jnp.maximum(jnp.abs(acc[:, LANES:]), emt_b[:, ps])
            outs.append(_sigmoid(proj(OFF_MO + p * LANES)) * hh)
            kwt16 = _dot_nt(eye16, (m_k[p] * ws_b[:, ps]).astype(bf16)).astype(bf16)
            dcn = _dot(kwt16, jnp.concatenate([m_v16[p], ones16], axis=1))
            decay = winter_b[ROWS - 1:ROWS, ps]
            cn_sc[p] = jnp.concatenate([decay, decay], axis=1) * m_cn[p] + jnp.where(blockdiag2, dcn, 0.0)
        outs.extend(r_acc)

        x4 = jnp.concatenate(outs, axis=0)
        jj = jj_ref[...]
        xc = x4 - _dot(_split_terms(x4, 2), jj)
        var = _dot(_split_terms(xc * xc, 2), jj)
        y4 = xc * lax.rsqrt(var + NORM_EPS)
        for i in range(2 * PAIRS):
            gain = (mgain_ref if i < PAIRS else rgain_ref)[:, (i % PAIRS) * LANES:(i % PAIRS + 1) * LANES]
            zoff = (OFF_MZ if i < PAIRS else OFF_RZ) + (i % PAIRS) * LANES
            out = y4[i * ROWS:(i + 1) * ROWS] * gain * _silu(proj(zoff))
            mix_sc[rows, i * LANES:(i + 1) * LANES] = out.astype(bf16)
        for j in range(KV_GROUP):
            acc = pick(a_pv[2 * j * ROWS:(2 * j + 2) * ROWS], left2)
            den = acc[:, LANES:] + jnp.where(left, a_out[2 * j], a_out[2 * j + 1])
            out = (acc[:, :LANES] / den) * _silu(proj(OFF_AZ + j * LANES))
            mix_sc[rows, M_DIM + R_DIM + j * LANES:M_DIM + R_DIM + (j + 1) * LANES] = out.astype(bf16)

    for ci in range(chunks):
        chunk_body(ci)

    _out_project(x_ref, mix_sc, wout_ref, fgain_ref, y_ref, final)

    @pl.when(step == last_step)
    def _():
        for p in range(PAIRS):
            cn = cn_sc[p]
            sb = sb_sc[p]
            n_t = cn[:, LANES:].T
            for side in range(2):
                h = 2 * p + side
                blk = slice(side * HEAD_DIM, (side + 1) * HEAD_DIM)
                c_out[0, h] = cn[blk, blk]
                s_out[0, h] = sb[blk, blk]
                n_out[0, h:h + 1, :] = n_t[side * HEAD_DIM:side * HEAD_DIM + 1, blk]
        m_out[0] = m_sc[...]
        k_out[0] = kp_sc[...]
        v_out[0] = vp_sc[...]


def _head_norm(x, gain):
    mu = jnp.mean(x, axis=1, keepdims=True)
    xc = x - mu
    var = jnp.mean(xc * xc, axis=1, keepdims=True)
    return xc * lax.rsqrt(var + NORM_EPS) * gain


def _group_last(x, groups):
    n = x.shape[1]
    glen = ROWS // groups
    x3 = x.reshape(groups, glen, n)
    return jnp.broadcast_to(x3[:, glen - 1:glen, :], (groups, glen, n)).reshape(ROWS, n)


def _state_rows(col, groups):
    glen = ROWS // groups
    wide = jnp.broadcast_to(col, (ROWS, HEAD_DIM)).reshape(groups, glen, HEAD_DIM)
    per_group = wide[:, 0:1, :]
    rows = jnp.broadcast_to(per_group, (groups, HEAD_DIM, HEAD_DIM)).reshape(groups * HEAD_DIM, HEAD_DIM)
    return rows, per_group.reshape(groups, HEAD_DIM)


def _cols_to_mat(cols, lane):
    acc = jnp.zeros((ROWS, LANES), f32)
    for h, c in enumerate(cols):
        acc = jnp.where(lane == h, c, acc)
    return acc


def _sample_kernel(x_ref, ngain_ref, win_ref, gbias_ref, mgain_ref, rgain_ref, sinks_ref, wout_ref,
                   fgain_ref, tril3_ref, maskadd_ref, dmat_ref, rc_ref, biasc_ref, biasp_ref, cos_ref, sin_ref,
                   c_in, n_in, m_in, s_in, k_in, v_in,
                   y_ref, c_out, n_out, m_out, s_out, k_out, v_out,
                   proj_sc, mix_sc, qb_sc, sp_sc, pp_sc, ob_sc, *, groups, final, ret_full):
    glen = ROWS // groups
    glen_log2 = glen.bit_length() - 1
    hd_log2 = HEAD_DIM.bit_length() - 1
    step = pl.program_id(0)

    _rms_project(x_ref, ngain_ref, win_ref, proj_sc)

    @pl.when(step == 0)
    def _():
        qb_sc[...] = jnp.zeros_like(qb_sc)

    lane = lax.broadcasted_iota(jnp.int32, (ROWS, LANES), 1)
    first_half = (lane & (HEAD_DIM - 1)) < (HEAD_DIM // 2)
    rows = slice(0, ROWS)

    r_i = lax.broadcasted_iota(jnp.int32, (ROWS, groups * HEAD_DIM), 0)
    c_i = lax.broadcasted_iota(jnp.int32, (ROWS, groups * HEAD_DIM), 1)
    blk = (r_i >> glen_log2) == (c_i >> hd_log2)
    r_t = lax.broadcasted_iota(jnp.int32, (groups * HEAD_DIM, ROWS), 0)
    c_t = lax.broadcasted_iota(jnp.int32, (groups * HEAD_DIM, ROWS), 1)
    blk_t = (r_t >> hd_log2) == (c_t >> glen_log2)

    def q_times_state(qh, st):
        qt = jnp.where(blk, jnp.concatenate([qh] * groups, axis=1), 0.0)
        return _dot(qt.astype(bf16), st.astype(bf16))

    def state_increment(kt_h, vh16):
        kt = jnp.where(blk_t, jnp.concatenate([kt_h] * groups, axis=0), 0.0)
        return _dot(kt.astype(bf16), vh16)

    gates = proj_sc[rows, OFF_G:OFF_G + GATE_PAD] + gbias_ref[...]
    bcum = _exact_tril_dot(tril3_ref[...], _log_sigmoid(gates))
    z = jnp.where(lane < M_HEADS, gates, bcum)
    zt = z.T
    zb = pltpu.roll(z, LANES - M_HEADS, axis=1)
    maskadd = maskadd_ref[...]

    dlogs, mintra = [], []
    for h in range(M_HEADS):
        bcol = zb[:, h:h + 1]
        dlog = (bcol - zt[M_HEADS + h:M_HEADS + h + 1, :]) + zt[h:h + 1, :] + maskadd
        dlogs.append(dlog)
        mintra.append(jnp.max(dlog, axis=1, keepdims=True))
    mprev = m_in[...]
    g_all = zb + mprev
    mt_all = jnp.maximum(g_all, _cols_to_mat(mintra, lane))
    winter_all = jnp.exp(g_all - mt_all)
    emt_all = jnp.exp(-mt_all)
    mlast_all = _group_last(mt_all, groups)
    blast_all = _group_last(zb, groups)
    decay_all = jnp.exp(blast_all + mprev - mlast_all)
    ws_all = jnp.exp(blast_all - zb + z - mlast_all)

    mq = proj_sc[rows, OFF_MQ:OFF_MQ + M_DIM]
    mk = proj_sc[rows, OFF_MK:OFF_MK + M_DIM] * QK_SCALE
    mv = proj_sc[rows, OFF_MV:OFF_MV + M_DIM]
    kw_parts = []
    for h in range(M_HEADS):
        hs = slice(h * HEAD_DIM, (h + 1) * HEAD_DIM)
        kw_parts.append(mk[:, hs] * ws_all[:, h:h + 1])
    kwt = jnp.concatenate(kw_parts, axis=1).T

    hm = []
    for h in range(M_HEADS):
        hs = slice(h * HEAD_DIM, (h + 1) * HEAD_DIM)
        qh, kh, vh = mq[:, hs], mk[:, hs], mv[:, hs]
        vh16 = vh.astype(bf16)
        mt = mt_all[:, h:h + 1]
        winter = winter_all[:, h:h + 1]
        wintra = jnp.exp(dlogs[h] - mt) * _dot_nt(qh.astype(bf16), kh.astype(bf16))
        c_h = c_in[:, h].reshape(groups * HEAD_DIM, HEAD_DIM)
        num = winter * q_times_state(qh, c_h) + _dot(wintra.astype(bf16), vh16)
        n_g = n_in[h]
        n_rows = jnp.broadcast_to(n_g.reshape(groups, 1, HEAD_DIM),
                                  (groups, glen, HEAD_DIM)).reshape(ROWS, HEAD_DIM)
        nq = (winter * jnp.sum(qh * n_rows, axis=1, keepdims=True)
              + jnp.sum(wintra, axis=1, keepdims=True))
        hm.append(num / jnp.maximum(jnp.abs(nq), emt_all[:, h:h + 1]))
        dec_rows, dec_g = _state_rows(decay_all[:, h:h + 1], groups)
        c_new = dec_rows * c_h + state_increment(kwt[hs, :], vh16)
        c_out[:, h] = c_new.reshape(groups, HEAD_DIM, HEAD_DIM)
        n_out[h] = dec_g * n_g + jnp.sum(kw_parts[h].reshape(groups, glen, HEAD_DIM), axis=1)
    m_out[...] = mlast_all
    hm = jnp.concatenate(hm, axis=1)
    hm = _sigmoid(proj_sc[rows, OFF_MO:OFF_MO + M_DIM]) * hm
    mgain = mgain_ref[...]
    out_m = jnp.concatenate(
        [_head_norm(hm[:, h * HEAD_DIM:(h + 1) * HEAD_DIM], mgain[:, h * HEAD_DIM:(h + 1) * HEAD_DIM])
         for h in range(M_HEADS)], axis=1)
    out_m = out_m * _silu(proj_sc[rows, OFF_MZ:OFF_MZ + M_DIM])
    mix_sc[rows, 0:M_DIM] = out_m.astype(bf16)

    cos_t, sin_t = cos_ref[...], sin_ref[...]
    rq = jnp.concatenate([_rope(proj_sc[rows, OFF_RQ + p * LANES:OFF_RQ + (p + 1) * LANES], cos_t, sin_t,
                                first_half) for p in range(PAIRS)], axis=1)
    rk = jnp.concatenate([_rope(proj_sc[rows, OFF_RK + p * LANES:OFF_RK + (p + 1) * LANES], cos_t, sin_t,
                                first_half) for p in range(PAIRS)], axis=1) * QK_SCALE
    rv = proj_sc[rows, OFF_RV:OFF_RV + R_DIM]
    rc = rc_ref[...]
    rkt = jnp.concatenate(
        [rk[:, h * HEAD_DIM:(h + 1) * HEAD_DIM] * rc[:, R_HEADS + h:R_HEADS + h + 1]
         for h in range(R_HEADS)], axis=1).T
    hr = []
    for h in range(R_HEADS):
        hs = slice(h * HEAD_DIM, (h + 1) * HEAD_DIM)
        qh, kh = rq[:, hs], rk[:, hs]
        vh16 = rv[:, hs].astype(bf16)
        scores = _dot_nt(qh.astype(bf16), kh.astype(bf16)) * dmat_ref[h]
        s_h = s_in[:, h].reshape(groups * HEAD_DIM, HEAD_DIM)
        hr.append(_dot(scores.astype(bf16), vh16) + rc[:, h:h + 1] * q_times_state(qh, s_h))
        s_new = ret_full[h] * s_h + state_increment(rkt[hs, :], vh16)
        s_out[:, h] = s_new.reshape(groups, HEAD_DIM, HEAD_DIM)
    rgain = rgain_ref[...]
    out_r = jnp.concatenate(
        [_head_norm(hr[h], rgain[:, h * HEAD_DIM:(h + 1) * HEAD_DIM]) for h in range(R_HEADS)], axis=1)
    out_r = out_r * _silu(proj_sc[rows, OFF_RZ:OFF_RZ + R_DIM])
    mix_sc[rows, M_DIM:M_DIM + R_DIM] = out_r.astype(bf16)

    kcur = proj_sc[rows, OFF_AK:OFF_AK + A_KV_DIM]
    vcur = proj_sc[rows, OFF_AV:OFF_AV + A_KV_DIM]
    kcur16, vcur16 = kcur.astype(bf16), vcur.astype(bf16)

    def q_cols(h):
        c0 = OFF_AQ + ATTN_HEAD_POS[h] * HEAD_DIM
        return slice(c0, c0 + HEAD_DIM)

    for h in range(A_HEADS):
        off = (h // KV_GROUP) * HEAD_DIM
        qb_sc[:, h * glen:(h + 1) * glen, off:off + HEAD_DIM] = (
            proj_sc[rows, q_cols(h)].reshape(groups, glen, HEAD_DIM).astype(bf16))

    def seq_scores(b, c):
        sp = _dot(qb_sc[b], k_in[b].astype(bf16))
        sp_sc[:, pl.ds(pl.multiple_of(b * glen, glen), glen), :] = sp.reshape(A_HEADS, glen, WINDOW)
        return c
    lax.fori_loop(0, groups, seq_scores, 0)

    dens, o_cur = [], []
    for h in range(A_HEADS):
        kv = h // KV_GROUP
        ks = slice(kv * HEAD_DIM, (kv + 1) * HEAD_DIM)
        qh16 = proj_sc[rows, q_cols(h)].astype(bf16)
        sc = _dot_nt(qh16, kcur16[:, ks]) * QK_SCALE + biasc_ref[h]
        sp = sp_sc[h] * QK_SCALE + biasp_ref[h]
        sink = sinks_ref[h]
        m = jnp.maximum(jnp.maximum(jnp.max(sc, axis=1, keepdims=True),
                                    jnp.max(sp, axis=1, keepdims=True)), sink)
        pc = jnp.exp(sc - m)
        pp = jnp.exp(sp - m)
        dens.append(jnp.sum(pc, axis=1, keepdims=True) + jnp.sum(pp, axis=1, keepdims=True)
                    + jnp.exp(sink - m))
        o_cur.append(_dot(pc.astype(bf16), vcur16[:, ks]))
        pp_sc[:, h * glen:(h + 1) * glen, :] = pp.reshape(groups, glen, WINDOW).astype(bf16)

    def seq_out(b, c):
        ob = _dot_nt(pp_sc[b], v_in[b].astype(bf16))
        ob_sc[:, pl.ds(pl.multiple_of(b * glen, glen), glen), :] = ob.reshape(A_HEADS, glen, A_KV_DIM)
        return c
    lax.fori_loop(0, groups, seq_out, 0)
    outs = []
    for h in ATTN_HEAD_ORDER:
        off = (h // KV_GROUP) * HEAD_DIM
        outs.append((o_cur[h] + ob_sc[h][:, off:off + HEAD_DIM]) / dens[h])
    kcur_t, vcur_t = kcur.T, vcur.T
    fresh = lane >= WINDOW - glen
    for b in range(groups):
        shift = (WINDOW - glen - b * glen) % LANES
        k_out[b] = jnp.where(fresh, pltpu.roll(kcur_t, shift, axis=1), pltpu.roll(k_in[b], WINDOW - glen, axis=1))
        v_out[b] = jnp.where(fresh, pltpu.roll(vcur_t, shift, axis=1), pltpu.roll(v_in[b], WINDOW - glen, axis=1))
    out_a = jnp.concatenate(outs, axis=1) * _silu(proj_sc[rows, OFF_AZ:OFF_AZ + A_DIM])
    mix_sc[rows, M_DIM + R_DIM:M_DIM + R_DIM + A_DIM] = out_a.astype(bf16)

    _out_project(x_ref, mix_sc, wout_ref, fgain_ref, y_ref, final)


def _t5_bucket(dist):
    max_exact = N_BUCKETS // 2
    d = np.maximum(dist, 1).astype(np.float32)
    large = max_exact + (np.log(d / max_exact) / np.log(REL_MAX_DIST / max_exact)
                         * (N_BUCKETS - max_exact)).astype(np.int32)
    large = np.minimum(large, N_BUCKETS - 1)
    return np.where(dist < max_exact, dist, large).astype(np.int32)


def _static_tables(groups):
    glen = ROWS // groups
    r = np.arange(ROWS)
    grp, tau = r // glen, r % glen
    causal = (grp[:, None] == grp[None, :]) & (tau[None, :] <= tau[:, None])
    tril = causal.astype(np.float32)
    maskadd = np.where(causal, 0.0, -np.inf).astype(np.float32)
    log_g = np.log1p(-np.exp2(-5.0 - np.arange(R_HEADS, dtype=np.float64)))
    diff = (tau[:, None] - tau[None, :]).astype(np.float64)
    dmat = np.where(causal[None], np.exp(log_g[:, None, None] * np.maximum(diff, 0.0)[None]), 0.0)
    inter = np.exp(log_g[None, :] * (tau[:, None] + 1.0))
    tail = np.exp(log_g[None, :] * (glen - 1.0 - tau[:, None]))
    full = np.exp(log_g * glen)
    rc = np.zeros((ROWS, LANES), np.float64)
    rc[:, 0:R_HEADS] = inter
    rc[:, R_HEADS:2 * R_HEADS] = tail
    lane_head = np.arange(LANES) // HEAD_DIM
    rslab = np.zeros((3, PAIRS, ROWS, LANES), np.float64)
    for p in range(PAIRS):
        rslab[0, p] = inter[:, 2 * p + lane_head]
        rslab[1, p] = tail[:, 2 * p + lane_head]
        rslab[2, p] = full[2 * p + lane_head][None, :]
    selh = np.zeros((M_HEADS * ROWS, SPLIT_TERMS * LANES), np.float32)
    selp = np.zeros((PAIRS * LANES, SPLIT_TERMS * LANES), np.float32)
    for t in range(SPLIT_TERMS):
        for h in range(M_HEADS):
            selh[h * ROWS:(h + 1) * ROWS, t * LANES + h] = 1.0
        for p in range(PAIRS):
            for side in range(2):
                selp[p * LANES + side * HEAD_DIM:p * LANES + (side + 1) * HEAD_DIM, t * LANES + 2 * p + side] = 1.0
    jj = np.zeros((2 * LANES, LANES), np.float32)
    for t in range(2):
        for side in range(2):
            jj[t * LANES + side * HEAD_DIM:t * LANES + (side + 1) * HEAD_DIM,
               side * HEAD_DIM:(side + 1) * HEAD_DIM] = 1.0 / HEAD_DIM
    return dict(tril3=jnp.asarray(np.concatenate([tril] * SPLIT_TERMS, axis=1), bf16),
                maskadd=maskadd, dmat=dmat.astype(np.float32),
                dmat2=dmat.astype(np.float32).reshape(PAIRS, 2 * ROWS, ROWS),
                rc=rc.astype(np.float32), rslab=rslab.astype(np.float32),
                full=tuple(float(v) for v in full), causal=causal, tau=tau,
                selh=jnp.asarray(selh, bf16), selp=jnp.asarray(selp, bf16), jj=jnp.asarray(jj, bf16),
                eye=jnp.asarray(np.eye(ROWS, dtype=np.float32), bf16))


def _bias_tables(rel_table, tabs_s):
    tb = jnp.transpose(rel_table[_t5_bucket(np.arange(WINDOW))]).astype(f32)
    ninf = jnp.full((A_HEADS, WINDOW + 1), -jnp.inf, f32)
    rev = tb[:, :0:-1]

    def skew(u):
        t = jnp.tile(u, (1, WINDOW))[:, :WINDOW * 2 * WINDOW]
        return t.reshape(A_HEADS, WINDOW, 2 * WINDOW)[:, :, :WINDOW]

    bias_cur = skew(jnp.concatenate([tb[:, :1], ninf, rev], axis=1))
    bias_prev = skew(jnp.concatenate([ninf[:, :1], rev, ninf], axis=1))
    bias_cp = jnp.concatenate([bias_cur, bias_prev], axis=2)
    bias_cp = jnp.concatenate([bias_cp[h] for h in ATTN_HEAD_ORDER], axis=0)
    glen = int(tabs_s["tau"].max()) + 1
    reps = ROWS // glen
    bias_cs = jnp.where(jnp.asarray(tabs_s["causal"])[None],
                        jnp.tile(bias_cur[:, :glen, :glen], (1, reps, reps)), -jnp.inf)
    bias_ps = jnp.tile(bias_prev[:, :glen, :], (1, reps, 1))
    return bias_cp, bias_cs, bias_ps


def _rope_tables(pos):
    half = HEAD_DIM // 2
    inv = ROPE_BASE ** (-jnp.arange(half, dtype=f32) / half)
    ang = pos.astype(f32)[:, None] * inv[None, :]
    cos, sin = jnp.cos(ang), jnp.sin(ang)
    reps = LANES // HEAD_DIM
    cos_t = jnp.tile(jnp.concatenate([cos, cos], axis=1), (1, reps))
    sin_t = jnp.tile(jnp.concatenate([-sin, sin], axis=1), (1, reps))
    return cos_t, sin_t


def _const_spec(shape, nargs):
    zeros = (0,) * len(shape)
    if nargs == 1:
        return pl.BlockSpec(shape, lambda i: zeros)
    return pl.BlockSpec(shape, lambda i, j: zeros)


def _layer_spec(shape, layer, nargs):
    idx = (layer,) + (0,) * len(shape)
    if nargs == 1:
        return pl.BlockSpec((None,) + shape, lambda i: idx)
    return pl.BlockSpec((None,) + shape, lambda i, j: idx)


def _param_specs(layer, nargs):
    ls = functools.partial(_layer_spec, layer=layer, nargs=nargs)
    return [ls((1, D_MODEL)), ls((P_COLS, D_MODEL)), ls((1, LANES)), ls((1, M_DIM)), ls((1, R_DIM)),
            pl.BlockSpec(memory_space=pltpu.SMEM), ls((D_MODEL, D_MODEL)), _const_spec((1, D_MODEL), nargs)]


def _param_args(p, layer):
    return (p["norm_gain"], p["w_in"], p["gbias"], p["m_gain"], p["r_gain"], p["sinks"][layer], p["w_out"],
            p["fgain"])


def _prompt_layer(x, p, layer, tabs, bias_cp, cos_t, sin_t, final):
    B, T, _ = x.shape
    tb = min(PROMPT_ROWS, T)
    chunks = tb // ROWS
    nt = T // tb
    cs = functools.partial(_const_spec, nargs=2)
    in_specs = [pl.BlockSpec((None, tb, D_MODEL), lambda b, t: (b, t, 0))] + _param_specs(layer, 2) + [
        cs((ROWS, SPLIT_TERMS * ROWS)), cs((ROWS, ROWS)), cs((PAIRS, 2 * ROWS, ROWS)),
        cs((3, PAIRS, ROWS, LANES)), cs((A_HEADS * ROWS, 2 * ROWS)),
        pl.BlockSpec((tb, LANES), lambda b, t: (t, 0)), pl.BlockSpec((tb, LANES), lambda b, t: (t, 0)),
        cs((M_HEADS * ROWS, SPLIT_TERMS * LANES)), cs((PAIRS * LANES, SPLIT_TERMS * LANES)),
        cs((2 * LANES, LANES)), cs((ROWS, ROWS)),
    ]
    out_shape = (
        jax.ShapeDtypeStruct((B, T, D_MODEL), f32),
        jax.ShapeDtypeStruct((B, M_HEADS, HEAD_DIM, HEAD_DIM), f32),
        jax.ShapeDtypeStruct((B, M_HEADS, HEAD_DIM), f32),
        jax.ShapeDtypeStruct((B, ROWS, LANES), f32),
        jax.ShapeDtypeStruct((B, R_HEADS, HEAD_DIM, HEAD_DIM), f32),
        jax.ShapeDtypeStruct((B, WINDOW, A_KV_DIM), f32),
        jax.ShapeDtypeStruct((B, WINDOW, A_KV_DIM), f32),
    )
    out_specs = (
        pl.BlockSpec((None, tb, D_MODEL), lambda b, t: (b, t, 0)),
        pl.BlockSpec((1, M_HEADS, HEAD_DIM, HEAD_DIM), lambda b, t: (b, 0, 0, 0)),
        pl.BlockSpec((1, M_HEADS, HEAD_DIM), lambda b, t: (b, 0, 0)),
        pl.BlockSpec((1, ROWS, LANES), lambda b, t: (b, 0, 0)),
        pl.BlockSpec((1, R_HEADS, HEAD_DIM, HEAD_DIM), lambda b, t: (b, 0, 0, 0)),
        pl.BlockSpec((1, WINDOW, A_KV_DIM), lambda b, t: (b, 0, 0)),
        pl.BlockSpec((1, WINDOW, A_KV_DIM), lambda b, t: (b, 0, 0)),
    )
    kern = functools.partial(_staged_prompt_kernel, chunks=chunks, final=final)
    y, c, n, m, s, k, v = pl.pallas_call(
        kern, grid=(B, nt), in_specs=in_specs, out_specs=out_specs, out_shape=out_shape,
        scratch_shapes=[pltpu.VMEM((tb, P_COLS), f32), pltpu.VMEM((tb, D_MODEL), bf16),
                        pltpu.VMEM((PAIRS, ROWS, 2 * LANES), f32), pltpu.VMEM((PAIRS, ROWS, LANES), f32),
                        pltpu.VMEM((ROWS, LANES), f32), pltpu.VMEM((ROWS, A_KV_DIM), f32),
                        pltpu.VMEM((ROWS, A_KV_DIM), f32)],
        compiler_params=pltpu.CompilerParams(dimension_semantics=("arbitrary", "arbitrary"),
                                             vmem_limit_bytes=VMEM_LIMIT_BYTES),
        name="prompt_layer",
    )(x, *_param_args(p, layer), tabs["tril3"], tabs["maskadd"], tabs["dmat2"], tabs["rslab"], bias_cp,
      cos_t, sin_t, tabs["selh"], tabs["selp"], tabs["jj"], tabs["eye"])
    k = k.reshape(B, WINDOW, A_KV_HEADS, HEAD_DIM)
    v = v.reshape(B, WINDOW, A_KV_HEADS, HEAD_DIM)
    return y, c, n, m[:, 0, :M_HEADS], s, k, v


def _sample_layer(x, states, p, layer, tabs, bias_c, bias_p, cos_t, sin_t, final):
    B, T, _ = x.shape
    groups = ROWS // T
    nb = B // groups
    c0, n0, m0, s0, k0, v0 = states
    depth = c0.shape[0]
    x2 = x.reshape(B * T, D_MODEL)
    n0t = jnp.transpose(n0[layer], (1, 0, 2))
    m0r = jnp.pad(jnp.repeat(m0[layer], T, axis=0), ((0, 0), (0, LANES - M_HEADS)))
    k0r = jnp.transpose(k0, (0, 1, 3, 4, 2)).reshape(depth, B, A_KV_DIM, WINDOW)
    v0r = jnp.transpose(v0, (0, 1, 3, 4, 2)).reshape(depth, B, A_KV_DIM, WINDOW)
    cs = functools.partial(_const_spec, nargs=1)
    st4_in = pl.BlockSpec((None, groups, M_HEADS, HEAD_DIM, HEAD_DIM), lambda i: (layer, i, 0, 0, 0))
    stk_in = pl.BlockSpec((None, groups, WINDOW, A_KV_DIM), lambda i: (layer, i, 0, 0))
    st4 = pl.BlockSpec((groups, M_HEADS, HEAD_DIM, HEAD_DIM), lambda i: (i, 0, 0, 0))
    stn = pl.BlockSpec((M_HEADS, groups, HEAD_DIM), lambda i: (0, i, 0))
    stm = pl.BlockSpec((ROWS, LANES), lambda i: (i, 0))
    stk = pl.BlockSpec((groups, WINDOW, A_KV_DIM), lambda i: (i, 0, 0))
    in_specs = [pl.BlockSpec((ROWS, D_MODEL), lambda i: (i, 0))] + _param_specs(layer, 1) + [
        cs((ROWS, SPLIT_TERMS * ROWS)), cs((ROWS, ROWS)), cs((R_HEADS, ROWS, ROWS)), cs((ROWS, LANES)),
        cs((A_HEADS, ROWS, ROWS)), cs((A_HEADS, ROWS, WINDOW)),
        cs((ROWS, LANES)), cs((ROWS, LANES)),
        st4_in, stn, stm, st4_in, stk_in, stk_in,
    ]
    out_shape = (
        jax.ShapeDtypeStruct((B * T, D_MODEL), f32),
        jax.ShapeDtypeStruct((B, M_HEADS, HEAD_DIM, HEAD_DIM), f32),
        jax.ShapeDtypeStruct((M_HEADS, B, HEAD_DIM), f32),
        jax.ShapeDtypeStruct((B * T, LANES), f32),
        jax.ShapeDtypeStruct((B, R_HEADS, HEAD_DIM, HEAD_DIM), f32),
        jax.ShapeDtypeStruct((B, WINDOW, A_KV_DIM), f32),
        jax.ShapeDtypeStruct((B, WINDOW, A_KV_DIM), f32),
    )
    out_specs = (pl.BlockSpec((ROWS, D_MODEL), lambda i: (i, 0)), st4, stn, stm, st4, stk, stk)
    kern = functools.partial(_sample_kernel, groups=groups, final=final, ret_full=tabs["full"])
    y, c, n, m, s, k, v = pl.pallas_call(
        kern, grid=(nb,), in_specs=in_specs, out_specs=out_specs, out_shape=out_shape,
        scratch_shapes=[pltpu.VMEM((ROWS, P_COLS), f32), pltpu.VMEM((ROWS, D_MODEL), bf16),
                        pltpu.VMEM((groups, A_HEADS * T, A_KV_DIM), bf16),
                        pltpu.VMEM((A_HEADS, ROWS, WINDOW), f32),
                        pltpu.VMEM((groups, A_HEADS * T, WINDOW), bf16),
                        pltpu.VMEM((A_HEADS, ROWS, A_KV_DIM), f32)],
        compiler_params=pltpu.CompilerParams(dimension_semantics=("arbitrary",),
                                             vmem_limit_bytes=VMEM_LIMIT_BYTES),
        name="sample_layer",
    )(x2, *_param_args(p, layer), tabs["tril3"], tabs["maskadd"], tabs["dmat"], tabs["rc"], bias_c, bias_p,
      cos_t, sin_t, c0, n0t, m0r, s0, k0r, v0r)
    y = y.reshape(B, T, D_MODEL)
    n = jnp.transpose(n, (1, 0, 2))
    m = m.reshape(B, T, LANES)[:, 0, :M_HEADS]
    k = jnp.transpose(k.reshape(B, A_KV_HEADS, HEAD_DIM, WINDOW), (0, 3, 1, 2))
    v = jnp.transpose(v.reshape(B, A_KV_HEADS, HEAD_DIM, WINDOW), (0, 3, 1, 2))
    return y, c, n, m, s, k, v


def _prepare_params(norm_gain, w_in, mlstm_gate_bias, mlstm_norm_gain, ret_norm_gain, attn_sinks, w_out,
                    final_norm_gain):
    depth = w_in.shape[0]
    w_t = jnp.swapaxes(w_in, 1, 2)
    split = OFF_G + N_GATES
    aq0 = split + 4 * R_DIM
    akv0 = aq0 + A_DIM
    az0 = akv0 + 2 * A_KV_DIM

    def by_head(w):
        w = w.reshape(depth, A_HEADS, HEAD_DIM, D_MODEL)
        return jnp.concatenate([w[:, h] for h in ATTN_HEAD_ORDER], axis=1)

    w_in_p = jnp.concatenate(
        [w_t[:, :split], jnp.zeros((depth, GATE_PAD - N_GATES, D_MODEL), w_t.dtype), w_t[:, split:aq0],
         by_head(w_t[:, aq0:akv0]), w_t[:, akv0:az0], by_head(w_t[:, az0:])], axis=1).astype(bf16)
    wo16 = w_out.astype(bf16)
    a0 = M_DIM + R_DIM
    w_out_p = jnp.concatenate([wo16[:, :a0, :], by_head(wo16[:, a0:, :])], axis=1)
    gbias = jnp.pad(mlstm_gate_bias.reshape(depth, 1, N_GATES), ((0, 0), (0, 0), (0, LANES - N_GATES)))
    return dict(norm_gain=norm_gain.reshape(depth, 1, D_MODEL), w_in=w_in_p, gbias=gbias,
                m_gain=mlstm_norm_gain.reshape(depth, 1, M_DIM), r_gain=ret_norm_gain.reshape(depth, 1, R_DIM),
                sinks=attn_sinks, w_out=w_out_p, fgain=final_norm_gain.reshape(1, D_MODEL))


def kernel(x_prompt, x_sample, state_mlstm_C, state_mlstm_n, state_mlstm_m, state_ret_S, cache_win_k,
           cache_win_v, norm_gain, w_in, mlstm_gate_bias, mlstm_norm_gain, ret_norm_gain, attn_sinks,
           rel_bias_table, w_out, final_norm_gain):
    depth = w_in.shape[0]
    seq = x_prompt.shape[1]
    dec_seq = x_sample.shape[1]
    past_len = seq
    p = _prepare_params(norm_gain, w_in, mlstm_gate_bias, mlstm_norm_gain, ret_norm_gain, attn_sinks, w_out,
                        final_norm_gain)
    tabs_p = _static_tables(1)
    tabs_s = _static_tables(ROWS // dec_seq)
    bias_cp, bias_cs, bias_ps = _bias_tables(rel_bias_table, tabs_s)
    cos_p, sin_p = _rope_tables(jnp.arange(seq, dtype=jnp.int32))
    cos_s, sin_s = _rope_tables(past_len + (jnp.arange(ROWS, dtype=jnp.int32) % dec_seq))
    states = (state_mlstm_C, state_mlstm_n, state_mlstm_m, state_ret_S, cache_win_k, cache_win_v)

    xp, xs = x_prompt, x_sample
    p_states, s_states = [], []
    for layer in range(depth):
        final = layer == depth - 1
        xp, *sp = _prompt_layer(xp, p, layer, tabs_p, bias_cp, cos_p, sin_p, final)
        xs, *ss = _sample_layer(xs, states, p, layer, tabs_s, bias_cs, bias_ps, cos_s, sin_s, final)
        p_states.append(sp)
        s_states.append(ss)
    outs_p = [jnp.stack([p_states[l][i] for l in range(depth)]) for i in range(6)]
    outs_s = [jnp.stack([s_states[l][i] for l in range(depth)]) for i in range(6)]
    return (xp, xs, *outs_p, *outs_s)
```

```python
import functools

import numpy as np
import jax
import jax.numpy as jnp
from jax import lax
from jax.experimental import pallas as pl
from jax.experimental.pallas import tpu as pltpu

D_MODEL = 1024
HEAD_DIM = 64
M_HEADS = 4
R_HEADS = 4
A_HEADS = 8
A_KV_HEADS = 2
KV_GROUP = A_HEADS // A_KV_HEADS
M_DIM = M_HEADS * HEAD_DIM
R_DIM = R_HEADS * HEAD_DIM
A_DIM = A_HEADS * HEAD_DIM
A_KV_DIM = A_KV_HEADS * HEAD_DIM
WINDOW = 128
N_BUCKETS = 32
REL_MAX_DIST = 128
ROPE_BASE = 10000.0
NORM_EPS = 1e-6
QK_SCALE = HEAD_DIM ** -0.5

LANES = 128
ROWS = 128
GATE_PAD = LANES
PAIRS = M_HEADS // 2
SPLIT_TERMS = 3

OFF_MQ = 0
OFF_MK = OFF_MQ + M_DIM
OFF_MV = OFF_MK + M_DIM
OFF_MO = OFF_MV + M_DIM
OFF_MZ = OFF_MO + M_DIM
OFF_G = OFF_MZ + M_DIM
OFF_RQ = OFF_G + GATE_PAD
OFF_RK = OFF_RQ + R_DIM
OFF_RV = OFF_RK + R_DIM
OFF_RZ = OFF_RV + R_DIM
OFF_AQ = OFF_RZ + R_DIM
OFF_AK = OFF_AQ + A_DIM
OFF_AV = OFF_AK + A_KV_DIM
OFF_AZ = OFF_AV + A_KV_DIM
P_COLS = OFF_AZ + A_DIM
N_GATES = 2 * M_HEADS
PROJ_COL_BLOCK = 512
ATTN_HEAD_ORDER = tuple(h for j in range(KV_GROUP) for h in (j, KV_GROUP + j))
ATTN_HEAD_POS = tuple(ATTN_HEAD_ORDER.index(h) for h in range(A_HEADS))

PROMPT_ROWS = 512
VMEM_LIMIT_BYTES = 56 * 1024 * 1024

f32 = jnp.float32
bf16 = jnp.bfloat16


def _dot(a, b):
    return jnp.dot(a, b, preferred_element_type=f32)


def _dot_nt(a, b):
    return lax.dot_general(a, b, (((1,), (1,)), ((), ())), preferred_element_type=f32)


def _sigmoid(x):
    return 1.0 / (1.0 + jnp.exp(-x))


def _silu(x):
    return x * _sigmoid(x)


def _log_sigmoid(x):
    return jnp.minimum(x, 0.0) - jnp.log(1.0 + jnp.exp(-jnp.abs(x)))


def _split_parts(x, terms):
    parts, r = [], x
    for i in range(terms):
        p = r.astype(bf16)
        parts.append(p)
        if i + 1 < terms:
            r = r - p.astype(f32)
    return parts


def _split_terms(x, terms=SPLIT_TERMS):
    return jnp.concatenate(_split_parts(x, terms), axis=1)


def _exact_tril_dot(tril3, x):
    return _dot(tril3, jnp.concatenate(_split_parts(x, SPLIT_TERMS), axis=0))


def _rope(x, cos_t, sin_t, first_half):
    up = pltpu.roll(x, LANES - HEAD_DIM // 2, axis=1)
    down = pltpu.roll(x, HEAD_DIM // 2, axis=1)
    return x * cos_t + jnp.where(first_half, up, down) * sin_t


def _rms_project(x_ref, ngain_ref, win_ref, proj_sc):
    xf = x_ref[...]
    u = xf * lax.rsqrt(jnp.mean(xf * xf, axis=1, keepdims=True) + NORM_EPS) * ngain_ref[...]
    u16 = u.astype(bf16)
    for c0 in range(0, P_COLS, PROJ_COL_BLOCK):
        c1 = min(c0 + PROJ_COL_BLOCK, P_COLS)
        proj_sc[:, c0:c1] = _dot_nt(u16, win_ref[c0:c1, :])


def _out_project(x_ref, mix_sc, wout_ref, fgain_ref, y_ref, final):
    y = x_ref[...] + _dot(mix_sc[...], wout_ref[...])
    if final:
        y = y * lax.rsqrt(jnp.mean(y * y, axis=1, keepdims=True) + NORM_EPS) * fgain_ref[...]
    y_ref[...] = y


def _pair_norm(x, jj, gain):
    mean = _dot(_split_terms(x, 2), jj)
    xc = x - mean
    var = _dot(_split_terms(xc * xc, 2), jj)
    return xc * lax.rsqrt(var + NORM_EPS) * gain


def _prompt_kernel(x_ref, ngain_ref, win_ref, gbias_ref, mgain_ref, rgain_ref, sinks_ref, wout_ref,
                   fgain_ref, tril3_ref, maskadd_ref, dmat_ref, rslab_ref, biascp_ref, cos_ref, sin_ref,
                   selh_ref, selp_ref, jj_ref, eye_ref,
                   y_ref, c_out, n_out, m_out, s_out, k_out, v_out,
                   proj_sc, mix_sc, cn_sc, sb_sc, m_sc, kp_sc, vp_sc, *, chunks, final):
    step = pl.program_id(1)
    last_step = pl.num_programs(1) - 1

    _rms_project(x_ref, ngain_ref, win_ref, proj_sc)

    @pl.when(step == 0)
    def _():
        cn_sc[...] = jnp.zeros_like(cn_sc)
        sb_sc[...] = jnp.zeros_like(sb_sc)
        m_sc[...] = jnp.zeros_like(m_sc)
        kp_sc[...] = jnp.zeros_like(kp_sc)
        vp_sc[...] = jnp.zeros_like(vp_sc)

    lane = lax.broadcasted_iota(jnp.int32, (ROWS, LANES), 1)
    row = lax.broadcasted_iota(jnp.int32, (ROWS, LANES), 0)
    left = lane < HEAD_DIM
    first_half = (lane & (HEAD_DIM - 1)) < (HEAD_DIM // 2)
    blockdiag = (row < HEAD_DIM) == left
    row2 = lax.broadcasted_iota(jnp.int32, (ROWS, 2 * LANES), 0)
    lane2w = lax.broadcasted_iota(jnp.int32, (ROWS, 2 * LANES), 1)
    blockdiag2 = (row2 < HEAD_DIM) == ((lane2w & (LANES - 1)) < HEAD_DIM)
    ones_l = jnp.where(left, 1.0, 0.0).astype(bf16)
    ones_r = jnp.where(left, 0.0, 1.0).astype(bf16)
    ones16 = jnp.ones((ROWS, LANES), bf16)

    def half(x, side):
        keep = left if side == 0 else jnp.logical_not(left)
        return jnp.where(keep, x, 0.0).astype(bf16)

    def chunk_body(ci):
        rows = slice(ci * ROWS, (ci + 1) * ROWS)
        eye16 = eye_ref[...]
        jj = jj_ref[...]

        gates = proj_sc[rows, OFF_G:OFF_G + GATE_PAD] + gbias_ref[...]
        bcum = _exact_tril_dot(tril3_ref[...], _log_sigmoid(gates))
        zb = pltpu.roll(bcum, LANES - M_HEADS, axis=1)
        head_col = lane < M_HEADS
        r_mat = jnp.where(head_col, gates - zb, 0.0)
        cm = r_mat
        sh = 1
        while sh < ROWS:
            cm = jnp.where(row >= sh, jnp.maximum(cm, pltpu.roll(cm, sh, axis=0)), cm)
            sh *= 2
        mprev = m_sc[...]
        mx = jnp.maximum(mprev, cm)
        gm = mprev - mx
        em = jnp.where(head_col, -(zb + mx), 0.0)
        mx_last = jnp.broadcast_to(mx[ROWS - 1:ROWS, :], (ROWS, LANES))
        m_sc[...] = jnp.where(head_col, jnp.broadcast_to((zb + mx)[ROWS - 1:ROWS, :], (ROWS, LANES)), 0.0)
        mx_b = _dot_nt(_split_terms(mx), selh_ref[...])
        winter_b = jnp.exp(_dot_nt(_split_terms(gm), selp_ref[...]))
        emt_b = jnp.exp(_dot_nt(_split_terms(em), selp_ref[...]))
        ws_b = jnp.exp(_dot_nt(_split_terms(r_mat - mx_last), selp_ref[...]))
        r_t = r_mat.T
        maskadd = maskadd_ref[...]

        for p in range(PAIRS):
            ps = slice(p * LANES, (p + 1) * LANES)
            q = proj_sc[rows, OFF_MQ + p * LANES:OFF_MQ + (p + 1) * LANES]
            k = proj_sc[rows, OFF_MK + p * LANES:OFF_MK + (p + 1) * LANES] * QK_SCALE
            v = proj_sc[rows, OFF_MV + p * LANES:OFF_MV + (p + 1) * LANES]
            q16, k16, v16 = q.astype(bf16), k.astype(bf16), v.astype(bf16)
            cn = cn_sc[p]
            acc = jnp.concatenate([winter_b[:, ps]] * 2, axis=1) * _dot(q16, cn.astype(bf16))
            for side in range(2):
                h = 2 * p + side
                qk = _dot_nt(half(q, side), k16)
                w = jnp.exp((r_t[h:h + 1, :] + maskadd) - mx_b[:, h * ROWS:(h + 1) * ROWS]) * qk
                vn = jnp.concatenate([half(v, side), ones_l if side == 0 else ones_r], axis=1)
                acc = acc + _dot(w.astype(bf16), vn)
            hh = acc[:, :LANES] / jnp.maximum(jnp.abs(acc[:, LANES:]), emt_b[:, ps])
            kw16 = (k * ws_b[:, ps]).astype(bf16)
            kwt16 = _dot_nt(eye16, kw16).astype(bf16)
            dcn = _dot(kwt16, jnp.concatenate([v16, ones16], axis=1))
            decay = winter_b[ROWS - 1:ROWS, ps]
            cn_sc[p] = (jnp.concatenate([decay, decay], axis=1) * cn
                        + jnp.where(blockdiag2, dcn, 0.0))
            hm = _sigmoid(proj_sc[rows, OFF_MO + p * LANES:OFF_MO + (p + 1) * LANES]) * hh
            out = _pair_norm(hm, jj, mgain_ref[:, ps]) * _silu(
                proj_sc[rows, OFF_MZ + p * LANES:OFF_MZ + (p + 1) * LANES])
            mix_sc[rows, p * LANES:(p + 1) * LANES] = out.astype(bf16)

        cos_t = cos_ref[rows, :]
        sin_t = sin_ref[rows, :]
        for p in range(PAIRS):
            ps = slice(p * LANES, (p + 1) * LANES)
            q = _rope(proj_sc[rows, OFF_RQ + p * LANES:OFF_RQ + (p + 1) * LANES], cos_t, sin_t, first_half)
            k = _rope(proj_sc[rows, OFF_RK + p * LANES:OFF_RK + (p + 1) * LANES], cos_t, sin_t,
                      first_half) * QK_SCALE
            v = proj_sc[rows, OFF_RV + p * LANES:OFF_RV + (p + 1) * LANES]
            q16, k16, v16 = q.astype(bf16), k.astype(bf16), v.astype(bf16)
            sb = sb_sc[p]
            acc = rslab_ref[0, p] * _dot(q16, sb.astype(bf16))
            for side in range(2):
                h = 2 * p + side
                scores = _dot_nt(half(q, side), k16) * dmat_ref[h]
                acc = acc + _dot(scores.astype(bf16), half(v, side))
            kt16 = (k * rslab_ref[1, p]).astype(bf16)
            ktt16 = _dot_nt(eye16, kt16).astype(bf16)
            sb_sc[p] = rslab_ref[2, p] * sb + jnp.where(blockdiag, _dot(ktt16, v16), 0.0)
            out = _pair_norm(acc, jj, rgain_ref[:, ps]) * _silu(
                proj_sc[rows, OFF_RZ + p * LANES:OFF_RZ + (p + 1) * LANES])
            mix_sc[rows, M_DIM + p * LANES:M_DIM + (p + 1) * LANES] = out.astype(bf16)

        kcur = proj_sc[rows, OFF_AK:OFF_AK + A_KV_DIM]
        vcur = proj_sc[rows, OFF_AV:OFF_AV + A_KV_DIM]
        kprev, vprev = kp_sc[...], vp_sc[...]
        kk16 = jnp.concatenate([kcur, kprev], axis=0).astype(bf16)
        vv = [jnp.concatenate([jnp.concatenate([half(vcur, s), ones_l if s == 0 else ones_r], axis=1),
                               jnp.concatenate([half(vprev, s), ones_l if s == 0 else ones_r], axis=1)],
                              axis=0) for s in range(2)]
        if ci == 0:
            pen = jnp.where(step == 0, -jnp.inf, 0.0).astype(f32)
            lane2 = lax.broadcasted_iota(jnp.int32, (1, 2 * ROWS), 1)
            pen_row = jnp.where(lane2 >= ROWS, pen, 0.0)
        else:
            pen_row = None
        for j in range(KV_GROUP):
            q = proj_sc[rows, OFF_AQ + j * LANES:OFF_AQ + (j + 1) * LANES] * QK_SCALE
            acc = None
            esink = []
            for side in range(2):
                h = ATTN_HEAD_ORDER[2 * j + side]
                s = _dot_nt(half(q, side), kk16) + biascp_ref[h]
                if pen_row is not None:
                    s = s + pen_row
                sink = sinks_ref[h]
                m = jnp.maximum(jnp.max(jnp.maximum(s[:, :ROWS], s[:, ROWS:]), axis=1, keepdims=True), sink)
                part = _dot(jnp.exp(s - m).astype(bf16), vv[side])
                acc = part if acc is None else acc + part
                esink.append(jnp.exp(sink - m))
            den = acc[:, LANES:] + jnp.where(left, esink[0], esink[1])
            out = (acc[:, :LANES] / den) * _silu(proj_sc[rows, OFF_AZ + j * LANES:OFF_AZ + (j + 1) * LANES])
            mix_sc[rows, M_DIM + R_DIM + j * LANES:M_DIM + R_DIM + (j + 1) * LANES] = out.astype(bf16)
        kp_sc[...] = kcur
        vp_sc[...] = vcur

    for ci in range(chunks):
        chunk_body(ci)

    _out_project(x_ref, mix_sc, wout_ref, fgain_ref, y_ref, final)

    @pl.when(step == last_step)
    def _():
        for p in range(PAIRS):
            cn = cn_sc[p]
            sb = sb_sc[p]
            n_t = cn[:, LANES:].T
            for side in range(2):
                h = 2 * p + side
                blk = slice(side * HEAD_DIM, (side + 1) * HEAD_DIM)
                c_out[0, h] = cn[blk, blk]
                s_out[0, h] = sb[blk, blk]
                n_out[0, h:h + 1, :] = n_t[side * HEAD_DIM:side * HEAD_DIM + 1, blk]
        m_out[0] = m_sc[...]
        k_out[0] = kp_sc[...]
        v_out[0] = vp_sc[...]


def _staged_prompt_kernel(x_ref, ngain_ref, win_ref, gbias_ref, mgain_ref, rgain_ref, sinks_ref, wout_ref,
                          fgain_ref, tril3_ref, maskadd_ref, dmat2_ref, rslab_ref, biasall_ref, cos_ref,
                          sin_ref, selh_ref, selp_ref, jj_ref, eye_ref,
                          y_ref, c_out, n_out, m_out, s_out, k_out, v_out,
                          proj_sc, mix_sc, cn_sc, sb_sc, m_sc, kp_sc, vp_sc, *, chunks, final):
    step = pl.program_id(1)
    last_step = pl.num_programs(1) - 1

    _rms_project(x_ref, ngain_ref, win_ref, proj_sc)

    @pl.when(step == 0)
    def _():
        cn_sc[...] = jnp.zeros_like(cn_sc)
        sb_sc[...] = jnp.zeros_like(sb_sc)
        m_sc[...] = jnp.zeros_like(m_sc)
        kp_sc[...] = jnp.zeros_like(kp_sc)
        vp_sc[...] = jnp.zeros_like(vp_sc)

    lane = lax.broadcasted_iota(jnp.int32, (ROWS, LANES), 1)
    row = lax.broadcasted_iota(jnp.int32, (ROWS, LANES), 0)
    left = lane < HEAD_DIM
    first_half = (lane & (HEAD_DIM - 1)) < (HEAD_DIM // 2)
    head_col = lane < M_HEADS
    blockdiag = (row < HEAD_DIM) == left
    row2 = lax.broadcasted_iota(jnp.int32, (ROWS, 2 * LANES), 0)
    lane2w = lax.broadcasted_iota(jnp.int32, (ROWS, 2 * LANES), 1)
    left2 = (lane2w & (LANES - 1)) < HEAD_DIM
    blockdiag2 = (row2 < HEAD_DIM) == left2
    ones16 = jnp.ones((ROWS, LANES), bf16)

    def halves(x):
        return jnp.concatenate([jnp.where(left, x, 0.0), jnp.where(left, 0.0, x)], axis=0).astype(bf16)

    def pick(x, mask):
        return jnp.where(mask, x[:ROWS], x[ROWS:])

    def chunk_body(ci):
        rows = slice(ci * ROWS, (ci + 1) * ROWS)

        def proj(off, width=LANES):
            return proj_sc[rows, off:off + width]

        eye16 = eye_ref[...]

        gates = proj(OFF_G) + gbias_ref[...]
        bcum = _exact_tril_dot(tril3_ref[...], _log_sigmoid(gates))

        cos_t, sin_t = cos_ref[rows, :], sin_ref[rows, :]
        r_sc, r_inter, r_v16 = [], [], []
        for p in range(PAIRS):
            q = _rope(proj(OFF_RQ + p * LANES), cos_t, sin_t, first_half)
            k = _rope(proj(OFF_RK + p * LANES), cos_t, sin_t, first_half) * QK_SCALE
            v16 = proj(OFF_RV + p * LANES).astype(bf16)
            sb = sb_sc[p]
            r_sc.append(_dot_nt(halves(q), k.astype(bf16)))
            r_inter.append(_dot(q.astype(bf16), sb.astype(bf16)))
            ktt16 = _dot_nt(eye16, (k * rslab_ref[1, p]).astype(bf16)).astype(bf16)
            sb_sc[p] = rslab_ref[2, p] * sb + jnp.where(blockdiag, _dot(ktt16, v16), 0.0)
            r_v16.append(v16)

        kcur, vcur = proj(OFF_AK), proj(OFF_AV)
        kprev, vprev = kp_sc[...], vp_sc[...]
        kk16 = jnp.concatenate([kcur, kprev], axis=0).astype(bf16)
        vv16 = jnp.concatenate([jnp.concatenate([vcur.astype(bf16), ones16], axis=1),
                                jnp.concatenate([vprev.astype(bf16), ones16], axis=1)], axis=0)
        kp_sc[...] = kcur
        vp_sc[...] = vcur
        a_q = jnp.concatenate([halves(proj(OFF_AQ + j * LANES) * QK_SCALE) for j in range(KV_GROUP)], axis=0)
        a_s = _dot_nt(a_q, kk16) + biasall_ref[...]
        if ci == 0:
            pen = jnp.where(step == 0, -jnp.inf, 0.0).astype(f32)
            a_s = a_s + jnp.where(lax.broadcasted_iota(jnp.int32, (1, 2 * ROWS), 1) >= ROWS, pen, 0.0)

        m_qk, m_qcn, m_k, m_v16, m_cn = [], [], [], [], []
        for p in range(PAIRS):
            q = proj(OFF_MQ + p * LANES)
            k = proj(OFF_MK + p * LANES) * QK_SCALE
            cn = cn_sc[p]
            m_qk.append(_dot_nt(halves(q), k.astype(bf16)))
            m_qcn.append(_dot(q.astype(bf16), cn.astype(bf16)))
            m_k.append(k)
            m_v16.append(proj(OFF_MV + p * LANES).astype(bf16))
            m_cn.append(cn)

        zb = pltpu.roll(bcum, LANES - M_HEADS, axis=1)
        r_mat = jnp.where(head_col, gates - zb, 0.0)
        cm = r_mat
        sh = 1
        while sh < ROWS:
            cm = jnp.where(row >= sh, jnp.maximum(cm, pltpu.roll(cm, sh, axis=0)), cm)
            sh *= 2
        mprev = m_sc[...]
        mx = jnp.maximum(mprev, cm)
        gm = mprev - mx
        em = jnp.where(head_col, -(zb + mx), 0.0)
        mx_last = jnp.broadcast_to(mx[ROWS - 1:ROWS, :], (ROWS, LANES))
        m_sc[...] = jnp.where(head_col, jnp.broadcast_to((zb + mx)[ROWS - 1:ROWS, :], (ROWS, LANES)), 0.0)
        mx_b = _dot_nt(_split_terms(mx), selh_ref[...])
        slabs = jnp.exp(_dot_nt(_split_terms(jnp.concatenate([gm, em, r_mat - mx_last], axis=0)),
                                selp_ref[...]))
        winter_b, emt_b, ws_b = slabs[:ROWS], slabs[ROWS:2 * ROWS], slabs[2 * ROWS:]
        r_t = r_mat.T

        outs = []
        r_acc = []
        for p in range(PAIRS):
            o = _dot((r_sc[p] * dmat2_ref[p]).astype(bf16), r_v16[p])
            r_acc.append(pick(o, left) + rslab_ref[0, p] * r_inter[p])

        a_out = []
        a_p = []
        for blk in range(A_HEADS):
            s = a_s[blk * ROWS:(blk + 1) * ROWS]
            sink = sinks_ref[ATTN_HEAD_ORDER[blk]]
            m = jnp.maximum(jnp.max(jnp.maximum(s[:, :ROWS], s[:, ROWS:]), axis=1, keepdims=True), sink)
            a_p.append(jnp.exp(s - m).astype(bf16))
            a_out.append(jnp.exp(sink - m))
        a_pv = _dot(jnp.concatenate(a_p, axis=0), vv16)

        maskadd = maskadd_ref[...]
        for p in range(PAIRS):
            ps = slice(p * LANES, (p + 1) * LANES)
            w = jnp.concatenate(
                [jnp.exp((r_t[2 * p + side:2 * p + side + 1, :] + maskadd)
                         - mx_b[:, (2 * p + side) * ROWS:(2 * p + side + 1) * ROWS]) for side in range(2)],
                axis=0) * m_qk[p]
            acc = (pick(_dot(w.astype(bf16), jnp.concatenate([m_v16[p], ones16], axis=1)), left2)
                   + jnp.concatenate([winter_b[:, ps]] * 2, axis=1) * m_qcn[p])
            hh = acc[:, :LANES] / jnp.maximum(jnp.abs(acc[:, LANES:]), emt_b[:, ps])
            outs.append(_sigmoid(proj(OFF_MO + p * LANES)) * hh)
            kwt16 = _dot_nt(eye16, (m_k[p] * ws_b[:, ps]).astype(bf16)).astype(bf16)
            dcn = _dot(kwt16, jnp.concatenate([m_v16[p], ones16], axis=1))
            decay = winter_b[ROWS - 1:ROWS, ps]
            cn_sc[p] = jnp.concatenate([decay, decay], axis=1) * m_cn[p] + jnp.where(blockdiag2, dcn, 0.0)
        outs.extend(r_acc)

        x4 = jnp.concatenate(outs, axis=0)
        jj = jj_ref[...]
        xc = x4 - _dot(_split_terms(x4, 2), jj)
        var = _dot(_split_terms(xc * xc, 2), jj)
        y4 = xc * lax.rsqrt(var + NORM_EPS)
        for i in range(2 * PAIRS):
            gain = (mgain_ref if i < PAIRS else rgain_ref)[:, (i % PAIRS) * LANES:(i % PAIRS + 1) * LANES]
            zoff = (OFF_MZ if i < PAIRS else OFF_RZ) + (i % PAIRS) * LANES
            out = y4[i * ROWS:(i + 1) * ROWS] * gain * _silu(proj(zoff))
            mix_sc[rows, i * LANES:(i + 1) * LANES] = out.astype(bf16)
        for j in range(KV_GROUP):
            acc = pick(a_pv[2 * j * ROWS:(2 * j + 2) * ROWS], left2)
            den = acc[:, LANES:] + jnp.where(left, a_out[2 * j], a_out[2 * j + 1])
            out = (acc[:, :LANES] / den) * _silu(proj(OFF_AZ + j * LANES))
            mix_sc[rows, M_DIM + R_DIM + j * LANES:M_DIM + R_DIM + (j + 1) * LANES] = out.astype(bf16)

    for ci in range(chunks):
        chunk_body(ci)

    _out_project(x_ref, mix_sc, wout_ref, fgain_ref, y_ref, final)

    @pl.when(step == last_step)
    def _():
        for p in range(PAIRS):
            cn = cn_sc[p]
            sb = sb_sc[p]
            n_t = cn[:, LANES:].T
            for side in range(2):
                h = 2 * p + side
                blk = slice(side * HEAD_DIM, (side + 1) * HEAD_DIM)
                c_out[0, h] = cn[blk, blk]
                s_out[0, h] = sb[blk, blk]
                n_out[0, h:h + 1, :] = n_t[side * HEAD_DIM:side * HEAD_DIM + 1, blk]
        m_out[0] = m_sc[...]
        k_out[0] = kp_sc[...]
        v_out[0] = vp_sc[...]


def _head_norm(x, gain):
    mu = jnp.mean(x, axis=1, keepdims=True)
    xc = x - mu
    var = jnp.mean(xc * xc, axis=1, keepdims=True)
    return xc * lax.rsqrt(var + NORM_EPS) * gain


def _group_last(x, groups):
    n = x.shape[1]
    glen = ROWS // groups
    x3 = x.reshape(groups, glen, n)
    return jnp.broadcast_to(x3[:, glen - 1:glen, :], (groups, glen, n)).reshape(ROWS, n)


def _state_rows(col, groups):
    glen = ROWS // groups
    wide = jnp.broadcast_to(col, (ROWS, HEAD_DIM)).reshape(groups, glen, HEAD_DIM)
    per_group = wide[:, 0:1, :]
    rows = jnp.broadcast_to(per_group, (groups, HEAD_DIM, HEAD_DIM)).reshape(groups * HEAD_DIM, HEAD_DIM)
    return rows, per_group.reshape(groups, HEAD_DIM)


def _cols_to_mat(cols, lane):
    acc = jnp.zeros((ROWS, LANES), f32)
    for h, c in enumerate(cols):
        acc = jnp.where(lane == h, c, acc)
    return acc


def _sample_kernel(x_ref, ngain_ref, win_ref, gbias_ref, mgain_ref, rgain_ref, sinks_ref, wout_ref,
                   fgain_ref, tril3_ref, maskadd_ref, dmat_ref, rc_ref, biasc_ref, biasp_ref, cos_ref, sin_ref,
                   c_in, n_in, m_in, s_in, k_in, v_in,
                   y_ref, c_out, n_out, m_out, s_out, k_out, v_out,
                   proj_sc, mix_sc, xcur_sc, qb_sc, sp_sc, pp_sc, ob_sc, *, groups, ret_full):
    glen = ROWS // groups
    glen_log2 = glen.bit_length() - 1
    hd_log2 = HEAD_DIM.bit_length() - 1
    step = pl.program_id(0)
    layer = pl.program_id(1)
    last_layer = pl.num_programs(1) - 1

    @pl.when(layer == 0)
    def _():
        xcur_sc[...] = x_ref[...]

    _rms_project(xcur_sc, ngain_ref, win_ref, proj_sc)

    @pl.when(jnp.logical_and(step == 0, layer == 0))
    def _():
        qb_sc[...] = jnp.zeros_like(qb_sc)

    lane = lax.broadcasted_iota(jnp.int32, (ROWS, LANES), 1)
    first_half = (lane & (HEAD_DIM - 1)) < (HEAD_DIM // 2)
    rows = slice(0, ROWS)

    r_i = lax.broadcasted_iota(jnp.int32, (ROWS, groups * HEAD_DIM), 0)
    c_i = lax.broadcasted_iota(jnp.int32, (ROWS, groups * HEAD_DIM), 1)
    blk = (r_i >> glen_log2) == (c_i >> hd_log2)
    r_t = lax.broadcasted_iota(jnp.int32, (groups * HEAD_DIM, ROWS), 0)
    c_t = lax.broadcasted_iota(jnp.int32, (groups * HEAD_DIM, ROWS), 1)
    blk_t = (r_t >> hd_log2) == (c_t >> glen_log2)

    def q_times_state(qh, st):
        qt = jnp.where(blk, jnp.concatenate([qh] * groups, axis=1), 0.0)
        return _dot(qt.astype(bf16), st.astype(bf16))

    def state_increment(kt_h, vh16):
        kt = jnp.where(blk_t, jnp.concatenate([kt_h] * groups, axis=0), 0.0)
        return _dot(kt.astype(bf16), vh16)

    gates = proj_sc[rows, OFF_G:OFF_G + GATE_PAD] + gbias_ref[...]
    bcum = _exact_tril_dot(tril3_ref[...], _log_sigmoid(gates))
    z = jnp.where(lane < M_HEADS, gates, bcum)
    zt = z.T
    zb = pltpu.roll(z, LANES - M_HEADS, axis=1)
    maskadd = maskadd_ref[...]

    dlogs, mintra = [], []
    for h in range(M_HEADS):
        bcol = zb[:, h:h + 1]
        dlog = (bcol - zt[M_HEADS + h:M_HEADS + h + 1, :]) + zt[h:h + 1, :] + maskadd
        dlogs.append(dlog)
        mintra.append(jnp.max(dlog, axis=1, keepdims=True))
    mprev = m_in[...]
    g_all = zb + mprev
    mt_all = jnp.maximum(g_all, _cols_to_mat(mintra, lane))
    winter_all = jnp.exp(g_all - mt_all)
    emt_all = jnp.exp(-mt_all)
    mlast_all = _group_last(mt_all, groups)
    blast_all = _group_last(zb, groups)
    decay_all = jnp.exp(blast_all + mprev - mlast_all)
    ws_all = jnp.exp(blast_all - zb + z - mlast_all)

    mq = proj_sc[rows, OFF_MQ:OFF_MQ + M_DIM]
    mk = proj_sc[rows, OFF_MK:OFF_MK + M_DIM] * QK_SCALE
    mv = proj_sc[rows, OFF_MV:OFF_MV + M_DIM]
    kw_parts = []
    for h in range(M_HEADS):
        hs = slice(h * HEAD_DIM, (h + 1) * HEAD_DIM)
        kw_parts.append(mk[:, hs] * ws_all[:, h:h + 1])
    kwt = jnp.concatenate(kw_parts, axis=1).T

    hm = []
    for h in range(M_HEADS):
        hs = slice(h * HEAD_DIM, (h + 1) * HEAD_DIM)
        qh, kh, vh = mq[:, hs], mk[:, hs], mv[:, hs]
        vh16 = vh.astype(bf16)
        mt = mt_all[:, h:h + 1]
        winter = winter_all[:, h:h + 1]
        wintra = jnp.exp(dlogs[h] - mt) * _dot_nt(qh.astype(bf16), kh.astype(bf16))
        c_h = c_in[:, h].reshape(groups * HEAD_DIM, HEAD_DIM)
        num = winter * q_times_state(qh, c_h) + _dot(wintra.astype(bf16), vh16)
        n_g = n_in[h]
        n_rows = jnp.broadcast_to(n_g.reshape(groups, 1, HEAD_DIM),
                                  (groups, glen, HEAD_DIM)).reshape(ROWS, HEAD_DIM)
        nq = (winter * jnp.sum(qh * n_rows, axis=1, keepdims=True)
              + jnp.sum(wintra, axis=1, keepdims=True))
        hm.append(num / jnp.maximum(jnp.abs(nq), emt_all[:, h:h + 1]))
        dec_rows, dec_g = _state_rows(decay_all[:, h:h + 1], groups)
        c_new = dec_rows * c_h + state_increment(kwt[hs, :], vh16)
        c_out[:, h] = c_new.reshape(groups, HEAD_DIM, HEAD_DIM)
        n_out[h] = dec_g * n_g + jnp.sum(kw_parts[h].reshape(groups, glen, HEAD_DIM), axis=1)
    m_out[...] = mlast_all
    hm = jnp.concatenate(hm, axis=1)
    hm = _sigmoid(proj_sc[rows, OFF_MO:OFF_MO + M_DIM]) * hm
    mgain = mgain_ref[...]
    out_m = jnp.concatenate(
        [_head_norm(hm[:, h * HEAD_DIM:(h + 1) * HEAD_DIM], mgain[:, h * HEAD_DIM:(h + 1) * HEAD_DIM])
         for h in range(M_HEADS)], axis=1)
    out_m = out_m * _silu(proj_sc[rows, OFF_MZ:OFF_MZ + M_DIM])
    mix_sc[rows, 0:M_DIM] = out_m.astype(bf16)

    cos_t, sin_t = cos_ref[...], sin_ref[...]
    rq = jnp.concatenate([_rope(proj_sc[rows, OFF_RQ + p * LANES:OFF_RQ + (p + 1) * LANES], cos_t, sin_t,
                                first_half) for p in range(PAIRS)], axis=1)
    rk = jnp.concatenate([_rope(proj_sc[rows, OFF_RK + p * LANES:OFF_RK + (p + 1) * LANES], cos_t, sin_t,
                                first_half) for p in range(PAIRS)], axis=1) * QK_SCALE
    rv = proj_sc[rows, OFF_RV:OFF_RV + R_DIM]
    rc = rc_ref[...]
    rkt = jnp.concatenate(
        [rk[:, h * HEAD_DIM:(h + 1) * HEAD_DIM] * rc[:, R_HEADS + h:R_HEADS + h + 1]
         for h in range(R_HEADS)], axis=1).T
    hr = []
    for h in range(R_HEADS):
        hs = slice(h * HEAD_DIM, (h + 1) * HEAD_DIM)
        qh, kh = rq[:, hs], rk[:, hs]
        vh16 = rv[:, hs].astype(bf16)
        scores = _dot_nt(qh.astype(bf16), kh.astype(bf16)) * dmat_ref[h]
        s_h = s_in[:, h].reshape(groups * HEAD_DIM, HEAD_DIM)
        hr.append(_dot(scores.astype(bf16), vh16) + rc[:, h:h + 1] * q_times_state(qh, s_h))
        s_new = ret_full[h] * s_h + state_increment(rkt[hs, :], vh16)
        s_out[:, h] = s_new.reshape(groups, HEAD_DIM, HEAD_DIM)
    rgain = rgain_ref[...]
    out_r = jnp.concatenate(
        [_head_norm(hr[h], rgain[:, h * HEAD_DIM:(h + 1) * HEAD_DIM]) for h in range(R_HEADS)], axis=1)
    out_r = out_r * _silu(proj_sc[rows, OFF_RZ:OFF_RZ + R_DIM])
    mix_sc[rows, M_DIM:M_DIM + R_DIM] = out_r.astype(bf16)

    kcur = proj_sc[rows, OFF_AK:OFF_AK + A_KV_DIM]
    vcur = proj_sc[rows, OFF_AV:OFF_AV + A_KV_DIM]
    kcur16, vcur16 = kcur.astype(bf16), vcur.astype(bf16)

    def q_cols(h):
        c0 = OFF_AQ + ATTN_HEAD_POS[h] * HEAD_DIM
        return slice(c0, c0 + HEAD_DIM)

    for h in range(A_HEADS):
        off = (h // KV_GROUP) * HEAD_DIM
        qb_sc[:, h * glen:(h + 1) * glen, off:off + HEAD_DIM] = (
            proj_sc[rows, q_cols(h)].reshape(groups, glen, HEAD_DIM).astype(bf16))

    for b in range(groups):
        sp = _dot(qb_sc[b], k_in[b].astype(bf16))
        sp_sc[:, b * glen:(b + 1) * glen, :] = sp.reshape(A_HEADS, glen, WINDOW)

    dens, o_cur = [], []
    for h in range(A_HEADS):
        kv = h // KV_GROUP
        ks = slice(kv * HEAD_DIM, (kv + 1) * HEAD_DIM)
        qh16 = proj_sc[rows, q_cols(h)].astype(bf16)
        sc = _dot_nt(qh16, kcur16[:, ks]) * QK_SCALE + biasc_ref[h]
        sp = sp_sc[h] * QK_SCALE + biasp_ref[h]
        sink = sinks_ref[layer, h]
        m = jnp.maximum(jnp.maximum(jnp.max(sc, axis=1, keepdims=True),
                                    jnp.max(sp, axis=1, keepdims=True)), sink)
        pc = jnp.exp(sc - m)
        pp = jnp.exp(sp - m)
        dens.append(jnp.sum(pc, axis=1, keepdims=True) + jnp.sum(pp, axis=1, keepdims=True)
                    + jnp.exp(sink - m))
        o_cur.append(_dot(pc.astype(bf16), vcur16[:, ks]))
        pp_sc[:, h * glen:(h + 1) * glen, :] = pp.reshape(groups, glen, WINDOW).astype(bf16)

    for b in range(groups):
        ob = _dot_nt(pp_sc[b], v_in[b].astype(bf16))
        ob_sc[:, b * glen:(b + 1) * glen, :] = ob.reshape(A_HEADS, glen, A_KV_DIM)
    outs = []
    for h in ATTN_HEAD_ORDER:
        off = (h // KV_GROUP) * HEAD_DIM
        outs.append((o_cur[h] + ob_sc[h][:, off:off + HEAD_DIM]) / dens[h])
    kcur_t, vcur_t = kcur.T, vcur.T
    fresh = lane >= WINDOW - glen
    for b in range(groups):
        shift = (WINDOW - glen - b * glen) % LANES
        k_out[b] = jnp.where(fresh, pltpu.roll(kcur_t, shift, axis=1), pltpu.roll(k_in[b], WINDOW - glen, axis=1))
        v_out[b] = jnp.where(fresh, pltpu.roll(vcur_t, shift, axis=1), pltpu.roll(v_in[b], WINDOW - glen, axis=1))
    out_a = jnp.concatenate(outs, axis=1) * _silu(proj_sc[rows, OFF_AZ:OFF_AZ + A_DIM])
    mix_sc[rows, M_DIM + R_DIM:M_DIM + R_DIM + A_DIM] = out_a.astype(bf16)

    y = xcur_sc[...] + _dot(mix_sc[...], wout_ref[...])
    xcur_sc[...] = y

    @pl.when(layer == last_layer)
    def _():
        y_ref[...] = y * lax.rsqrt(jnp.mean(y * y, axis=1, keepdims=True) + NORM_EPS) * fgain_ref[...]


def _t5_bucket(dist):
    max_exact = N_BUCKETS // 2
    d = np.maximum(dist, 1).astype(np.float32)
    large = max_exact + (np.log(d / max_exact) / np.log(REL_MAX_DIST / max_exact)
                         * (N_BUCKETS - max_exact)).astype(np.int32)
    large = np.minimum(large, N_BUCKETS - 1)
    return np.where(dist < max_exact, dist, large).astype(np.int32)


def _static_tables(groups):
    glen = ROWS // groups
    r = np.arange(ROWS)
    grp, tau = r // glen, r % glen
    causal = (grp[:, None] == grp[None, :]) & (tau[None, :] <= tau[:, None])
    tril = causal.astype(np.float32)
    maskadd = np.where(causal, 0.0, -np.inf).astype(np.float32)
    log_g = np.log1p(-np.exp2(-5.0 - np.arange(R_HEADS, dtype=np.float64)))
    diff = (tau[:, None] - tau[None, :]).astype(np.float64)
    dmat = np.where(causal[None], np.exp(log_g[:, None, None] * np.maximum(diff, 0.0)[None]), 0.0)
    inter = np.exp(log_g[None, :] * (tau[:, None] + 1.0))
    tail = np.exp(log_g[None, :] * (glen - 1.0 - tau[:, None]))
    full = np.exp(log_g * glen)
    rc = np.zeros((ROWS, LANES), np.float64)
    rc[:, 0:R_HEADS] = inter
    rc[:, R_HEADS:2 * R_HEADS] = tail
    lane_head = np.arange(LANES) // HEAD_DIM
    rslab = np.zeros((3, PAIRS, ROWS, LANES), np.float64)
    for p in range(PAIRS):
        rslab[0, p] = inter[:, 2 * p + lane_head]
        rslab[1, p] = tail[:, 2 * p + lane_head]
        rslab[2, p] = full[2 * p + lane_head][None, :]
    selh = np.zeros((M_HEADS * ROWS, SPLIT_TERMS * LANES), np.float32)
    selp = np.zeros((PAIRS * LANES, SPLIT_TERMS * LANES), np.float32)
    for t in range(SPLIT_TERMS):
        for h in range(M_HEADS):
            selh[h * ROWS:(h + 1) * ROWS, t * LANES + h] = 1.0
        for p in range(PAIRS):
            for side in range(2):
                selp[p * LANES + side * HEAD_DIM:p * LANES + (side + 1) * HEAD_DIM, t * LANES + 2 * p + side] = 1.0
    jj = np.zeros((2 * LANES, LANES), np.float32)
    for t in range(2):
        for side in range(2):
            jj[t * LANES + side * HEAD_DIM:t * LANES + (side + 1) * HEAD_DIM,
               side * HEAD_DIM:(side + 1) * HEAD_DIM] = 1.0 / HEAD_DIM
    return dict(tril3=jnp.asarray(np.concatenate([tril] * SPLIT_TERMS, axis=1), bf16),
                maskadd=maskadd, dmat=dmat.astype(np.float32),
                dmat2=dmat.astype(np.float32).reshape(PAIRS, 2 * ROWS, ROWS),
                rc=rc.astype(np.float32), rslab=rslab.astype(np.float32),
                full=tuple(float(v) for v in full), causal=causal, tau=tau,
                selh=jnp.asarray(selh, bf16), selp=jnp.asarray(selp, bf16), jj=jnp.asarray(jj, bf16),
                eye=jnp.asarray(np.eye(ROWS, dtype=np.float32), bf16))


def _bias_tables(rel_table, tabs_s):
    tb = jnp.transpose(rel_table[_t5_bucket(np.arange(WINDOW))]).astype(f32)
    ninf = jnp.full((A_HEADS, WINDOW + 1), -jnp.inf, f32)
    rev = tb[:, :0:-1]

    def skew(u):
        t = jnp.tile(u, (1, WINDOW))[:, :WINDOW * 2 * WINDOW]
        return t.reshape(A_HEADS, WINDOW, 2 * WINDOW)[:, :, :WINDOW]

    bias_cur = skew(jnp.concatenate([tb[:, :1], ninf, rev], axis=1))
    bias_prev = skew(jnp.concatenate([ninf[:, :1], rev, ninf], axis=1))
    bias_cp = jnp.concatenate([bias_cur, bias_prev], axis=2)
    bias_cp = jnp.concatenate([bias_cp[h] for h in ATTN_HEAD_ORDER], axis=0)
    glen = int(tabs_s["tau"].max()) + 1
    reps = ROWS // glen
    bias_cs = jnp.where(jnp.asarray(tabs_s["causal"])[None],
                        jnp.tile(bias_cur[:, :glen, :glen], (1, reps, reps)), -jnp.inf)
    bias_ps = jnp.tile(bias_prev[:, :glen, :], (1, reps, 1))
    return bias_cp, bias_cs, bias_ps


def _rope_tables(pos):
    half = HEAD_DIM // 2
    inv = ROPE_BASE ** (-jnp.arange(half, dtype=f32) / half)
    ang = pos.astype(f32)[:, None] * inv[None, :]
    cos, sin = jnp.cos(ang), jnp.sin(ang)
    reps = LANES // HEAD_DIM
    cos_t = jnp.tile(jnp.concatenate([cos, cos], axis=1), (1, reps))
    sin_t = jnp.tile(jnp.concatenate([-sin, sin], axis=1), (1, reps))
    return cos_t, sin_t


def _const_spec(shape, nargs):
    zeros = (0,) * len(shape)
    if nargs == 1:
        return pl.BlockSpec(shape, lambda i: zeros)
    return pl.BlockSpec(shape, lambda i, j: zeros)


def _layer_spec(shape, layer, nargs):
    idx = (layer,) + (0,) * len(shape)
    if nargs == 1:
        return pl.BlockSpec((None,) + shape, lambda i: idx)
    return pl.BlockSpec((None,) + shape, lambda i, j: idx)


def _param_specs(layer, nargs):
    ls = functools.partial(_layer_spec, layer=layer, nargs=nargs)
    return [ls((1, D_MODEL)), ls((P_COLS, D_MODEL)), ls((1, LANES)), ls((1, M_DIM)), ls((1, R_DIM)),
            pl.BlockSpec(memory_space=pltpu.SMEM), ls((D_MODEL, D_MODEL)), _const_spec((1, D_MODEL), nargs)]


def _param_args(p, layer):
    return (p["norm_gain"], p["w_in"], p["gbias"], p["m_gain"], p["r_gain"], p["sinks"][layer], p["w_out"],
            p["fgain"])


def _prompt_layer(x, p, layer, tabs, bias_cp, cos_t, sin_t, final):
    B, T, _ = x.shape
    tb = min(PROMPT_ROWS, T)
    chunks = tb // ROWS
    nt = T // tb
    cs = functools.partial(_const_spec, nargs=2)
    in_specs = [pl.BlockSpec((None, tb, D_MODEL), lambda b, t: (b, t, 0))] + _param_specs(layer, 2) + [
        cs((ROWS, SPLIT_TERMS * ROWS)), cs((ROWS, ROWS)), cs((PAIRS, 2 * ROWS, ROWS)),
        cs((3, PAIRS, ROWS, LANES)), cs((A_HEADS * ROWS, 2 * ROWS)),
        pl.BlockSpec((tb, LANES), lambda b, t: (t, 0)), pl.BlockSpec((tb, LANES), lambda b, t: (t, 0)),
        cs((M_HEADS * ROWS, SPLIT_TERMS * LANES)), cs((PAIRS * LANES, SPLIT_TERMS * LANES)),
        cs((2 * LANES, LANES)), cs((ROWS, ROWS)),
    ]
    out_shape = (
        jax.ShapeDtypeStruct((B, T, D_MODEL), f32),
        jax.ShapeDtypeStruct((B, M_HEADS, HEAD_DIM, HEAD_DIM), f32),
        jax.ShapeDtypeStruct((B, M_HEADS, HEAD_DIM), f32),
        jax.ShapeDtypeStruct((B, ROWS, LANES), f32),
        jax.ShapeDtypeStruct((B, R_HEADS, HEAD_DIM, HEAD_DIM), f32),
        jax.ShapeDtypeStruct((B, WINDOW, A_KV_DIM), f32),
        jax.ShapeDtypeStruct((B, WINDOW, A_KV_DIM), f32),
    )
    out_specs = (
        pl.BlockSpec((None, tb, D_MODEL), lambda b, t: (b, t, 0)),
        pl.BlockSpec((1, M_HEADS, HEAD_DIM, HEAD_DIM), lambda b, t: (b, 0, 0, 0)),
        pl.BlockSpec((1, M_HEADS, HEAD_DIM), lambda b, t: (b, 0, 0)),
        pl.BlockSpec((1, ROWS, LANES), lambda b, t: (b, 0, 0)),
        pl.BlockSpec((1, R_HEADS, HEAD_DIM, HEAD_DIM), lambda b, t: (b, 0, 0, 0)),
        pl.BlockSpec((1, WINDOW, A_KV_DIM), lambda b, t: (b, 0, 0)),
        pl.BlockSpec((1, WINDOW, A_KV_DIM), lambda b, t: (b, 0, 0)),
    )
    kern = functools.partial(_staged_prompt_kernel, chunks=chunks, final=final)
    y, c, n, m, s, k, v = pl.pallas_call(
        kern, grid=(B, nt), in_specs=in_specs, out_specs=out_specs, out_shape=out_shape,
        scratch_shapes=[pltpu.VMEM((tb, P_COLS), f32), pltpu.VMEM((tb, D_MODEL), bf16),
                        pltpu.VMEM((PAIRS, ROWS, 2 * LANES), f32), pltpu.VMEM((PAIRS, ROWS, LANES), f32),
                        pltpu.VMEM((ROWS, LANES), f32), pltpu.VMEM((ROWS, A_KV_DIM), f32),
                        pltpu.VMEM((ROWS, A_KV_DIM), f32)],
        compiler_params=pltpu.CompilerParams(dimension_semantics=("arbitrary", "arbitrary"),
                                             vmem_limit_bytes=VMEM_LIMIT_BYTES),
        name="prompt_layer",
    )(x, *_param_args(p, layer), tabs["tril3"], tabs["maskadd"], tabs["dmat2"], tabs["rslab"], bias_cp,
      cos_t, sin_t, tabs["selh"], tabs["selp"], tabs["jj"], tabs["eye"])
    k = k.reshape(B, WINDOW, A_KV_HEADS, HEAD_DIM)
    v = v.reshape(B, WINDOW, A_KV_HEADS, HEAD_DIM)
    return y, c, n, m[:, 0, :M_HEADS], s, k, v


def _sample_path(x, states, p, tabs, bias_c, bias_p, cos_t, sin_t):
    B, T, _ = x.shape
    groups = ROWS // T
    nb = B // groups
    c0, n0, m0, s0, k0, v0 = states
    depth = c0.shape[0]
    x2 = x.reshape(B * T, D_MODEL)
    n0t = jnp.transpose(n0, (0, 2, 1, 3))
    m0r = jnp.pad(jnp.repeat(m0, T, axis=1), ((0, 0), (0, 0), (0, LANES - M_HEADS)))
    k0r = jnp.transpose(k0, (0, 1, 3, 4, 2)).reshape(depth, B, A_KV_DIM, WINDOW)
    v0r = jnp.transpose(v0, (0, 1, 3, 4, 2)).reshape(depth, B, A_KV_DIM, WINDOW)

    def cs(shape):
        zeros = (0,) * len(shape)
        return pl.BlockSpec(shape, lambda i, l: zeros)

    def per_layer(shape):
        zeros = (0,) * len(shape)
        return pl.BlockSpec((None,) + shape, lambda i, l: (l,) + zeros)

    st4 = pl.BlockSpec((None, groups, M_HEADS, HEAD_DIM, HEAD_DIM), lambda i, l: (l, i, 0, 0, 0))
    stn = pl.BlockSpec((None, M_HEADS, groups, HEAD_DIM), lambda i, l: (l, 0, i, 0))
    stm = pl.BlockSpec((None, ROWS, LANES), lambda i, l: (l, i, 0))
    stk = pl.BlockSpec((None, groups, A_KV_DIM, WINDOW), lambda i, l: (l, i, 0, 0))
    rows_spec = pl.BlockSpec((ROWS, D_MODEL), lambda i, l: (i, 0))
    in_specs = [
        rows_spec,
        per_layer((1, D_MODEL)), per_layer((P_COLS, D_MODEL)), per_layer((1, LANES)), per_layer((1, M_DIM)),
        per_layer((1, R_DIM)), pl.BlockSpec(memory_space=pltpu.SMEM), per_layer((D_MODEL, D_MODEL)),
        cs((1, D_MODEL)),
        cs((ROWS, SPLIT_TERMS * ROWS)), cs((ROWS, ROWS)), cs((R_HEADS, ROWS, ROWS)), cs((ROWS, LANES)),
        cs((A_HEADS, ROWS, ROWS)), cs((A_HEADS, ROWS, WINDOW)),
        cs((ROWS, LANES)), cs((ROWS, LANES)),
        st4, stn, stm, st4, stk, stk,
    ]
    out_shape = (
        jax.ShapeDtypeStruct((B * T, D_MODEL), f32),
        jax.ShapeDtypeStruct((depth, B, M_HEADS, HEAD_DIM, HEAD_DIM), f32),
        jax.ShapeDtypeStruct((depth, M_HEADS, B, HEAD_DIM), f32),
        jax.ShapeDtypeStruct((depth, B * T, LANES), f32),
        jax.ShapeDtypeStruct((depth, B, R_HEADS, HEAD_DIM, HEAD_DIM), f32),
        jax.ShapeDtypeStruct((depth, B, A_KV_DIM, WINDOW), f32),
        jax.ShapeDtypeStruct((depth, B, A_KV_DIM, WINDOW), f32),
    )
    out_specs = (rows_spec, st4, stn, stm, st4, stk, stk)
    kern = functools.partial(_sample_kernel, groups=groups, ret_full=tabs["full"])
    y, c, n, m, s, k, v = pl.pallas_call(
        kern, grid=(nb, depth), in_specs=in_specs, out_specs=out_specs, out_shape=out_shape,
        scratch_shapes=[pltpu.VMEM((ROWS, P_COLS), f32), pltpu.VMEM((ROWS, D_MODEL), bf16),
                        pltpu.VMEM((ROWS, D_MODEL), f32),
                        pltpu.VMEM((groups, A_HEADS * T, A_KV_DIM), bf16),
                        pltpu.VMEM((A_HEADS, ROWS, WINDOW), f32),
                        pltpu.VMEM((groups, A_HEADS * T, WINDOW), bf16),
                        pltpu.VMEM((A_HEADS, ROWS, A_KV_DIM), f32)],
        compiler_params=pltpu.CompilerParams(dimension_semantics=("arbitrary", "arbitrary"),
                                             vmem_limit_bytes=VMEM_LIMIT_BYTES),
        name="sample_path",
    )(x2, p["norm_gain"], p["w_in"], p["gbias"], p["m_gain"], p["r_gain"], p["sinks"], p["w_out"], p["fgain"],
      tabs["tril3"], tabs["maskadd"], tabs["dmat"], tabs["rc"], bias_c, bias_p,
      cos_t, sin_t, c0, n0t, m0r, s0, k0r, v0r)
    y = y.reshape(B, T, D_MODEL)
    n = jnp.transpose(n, (0, 2, 1, 3))
    m = m.reshape(depth, B, T, LANES)[:, :, 0, :M_HEADS]
    k = jnp.transpose(k.reshape(depth, B, A_KV_HEADS, HEAD_DIM, WINDOW), (0, 1, 4, 2, 3))
    v = jnp.transpose(v.reshape(depth, B, A_KV_HEADS, HEAD_DIM, WINDOW), (0, 1, 4, 2, 3))
    return y, c, n, m, s, k, v


def _prepare_params(norm_gain, w_in, mlstm_gate_bias, mlstm_norm_gain, ret_norm_gain, attn_sinks, w_out,
                    final_norm_gain):
    depth = w_in.shape[0]
    w_t = jnp.swapaxes(w_in, 1, 2)
    split = OFF_G + N_GATES
    aq0 = split + 4 * R_DIM
    akv0 = aq0 + A_DIM
    az0 = akv0 + 2 * A_KV_DIM

    def by_head(w):
        w = w.reshape(depth, A_HEADS, HEAD_DIM, D_MODEL)
        return jnp.concatenate([w[:, h] for h in ATTN_HEAD_ORDER], axis=1)

    w_in_p = jnp.concatenate(
        [w_t[:, :split], jnp.zeros((depth, GATE_PAD - N_GATES, D_MODEL), w_t.dtype), w_t[:, split:aq0],
         by_head(w_t[:, aq0:akv0]), w_t[:, akv0:az0], by_head(w_t[:, az0:])], axis=1).astype(bf16)
    wo16 = w_out.astype(bf16)
    a0 = M_DIM + R_DIM
    w_out_p = jnp.concatenate([wo16[:, :a0, :], by_head(wo16[:, a0:, :])], axis=1)
    gbias = jnp.pad(mlstm_gate_bias.reshape(depth, 1, N_GATES), ((0, 0), (0, 0), (0, LANES - N_GATES)))
    return dict(norm_gain=norm_gain.reshape(depth, 1, D_MODEL), w_in=w_in_p, gbias=gbias,
                m_gain=mlstm_norm_gain.reshape(depth, 1, M_DIM), r_gain=ret_norm_gain.reshape(depth, 1, R_DIM),
                sinks=attn_sinks, w_out=w_out_p, fgain=final_norm_gain.reshape(1, D_MODEL))


def kernel(x_prompt, x_sample, state_mlstm_C, state_mlstm_n, state_mlstm_m, state_ret_S, cache_win_k,
           cache_win_v, norm_gain, w_in, mlstm_gate_bias, mlstm_norm_gain, ret_norm_gain, attn_sinks,
           rel_bias_table, w_out, final_norm_gain):
    depth = w_in.shape[0]
    seq = x_prompt.shape[1]
    dec_seq = x_sample.shape[1]
    past_len = seq
    p = _prepare_params(norm_gain, w_in, mlstm_gate_bias, mlstm_norm_gain, ret_norm_gain, attn_sinks, w_out,
                        final_norm_gain)
    tabs_p = _static_tables(1)
    tabs_s = _static_tables(ROWS // dec_seq)
    bias_cp, bias_cs, bias_ps = _bias_tables(rel_bias_table, tabs_s)
    cos_p, sin_p = _rope_tables(jnp.arange(seq, dtype=jnp.int32))
    cos_s, sin_s = _rope_tables(past_len + (jnp.arange(ROWS, dtype=jnp.int32) % dec_seq))
    states = (state_mlstm_C, state_mlstm_n, state_mlstm_m, state_ret_S, cache_win_k, cache_win_v)

    xp = x_prompt
    p_states = []
    for layer in range(depth):
        xp, *sp = _prompt_layer(xp, p, layer, tabs_p, bias_cp, cos_p, sin_p, layer == depth - 1)
        p_states.append(sp)
    outs_p = [jnp.stack([p_states[l][i] for l in range(depth)]) for i in range(6)]
    xs, *outs_s = _sample_path(x_sample, states, p, tabs_s, bias_cs, bias_ps, cos_s, sin_s)
    return (xp, xs, *outs_p, *outs_s)
```

```python
import functools

import numpy as np
import jax
import jax.numpy as jnp
from jax import lax
from jax.experimental import pallas as pl
from jax.experimental.pallas import tpu as pltpu

D_MODEL = 1024
HEAD_DIM = 64
M_HEADS = 4
R_HEADS = 4
A_HEADS = 8
A_KV_HEADS = 2
KV_GROUP = A_HEADS // A_KV_HEADS
M_DIM = M_HEADS * HEAD_DIM
R_DIM = R_HEADS * HEAD_DIM
A_DIM = A_HEADS * HEAD_DIM
A_KV_DIM = A_KV_HEADS * HEAD_DIM
WINDOW = 128
N_BUCKETS = 32
REL_MAX_DIST = 128
ROPE_BASE = 10000.0
NORM_EPS = 1e-6
QK_SCALE = HEAD_DIM ** -0.5

LANES = 128
ROWS = 128
GATE_PAD = LANES
PAIRS = M_HEADS // 2
SPLIT_TERMS = 3

OFF_MQ = 0
OFF_MK = OFF_MQ + M_DIM
OFF_MV = OFF_MK + M_DIM
OFF_MO = OFF_MV + M_DIM
OFF_MZ = OFF_MO + M_DIM
OFF_G = OFF_MZ + M_DIM
OFF_RQ = OFF_G + GATE_PAD
OFF_RK = OFF_RQ + R_DIM
OFF_RV = OFF_RK + R_DIM
OFF_RZ = OFF_RV + R_DIM
OFF_AQ = OFF_RZ + R_DIM
OFF_AK = OFF_AQ + A_DIM
OFF_AV = OFF_AK + A_KV_DIM
OFF_AZ = OFF_AV + A_KV_DIM
P_COLS = OFF_AZ + A_DIM
N_GATES = 2 * M_HEADS
PROJ_COL_BLOCK = 512
ATTN_HEAD_ORDER = tuple(h for j in range(KV_GROUP) for h in (j, KV_GROUP + j))
ATTN_HEAD_POS = tuple(ATTN_HEAD_ORDER.index(h) for h in range(A_HEADS))

PROMPT_ROWS = 512
VMEM_LIMIT_BYTES = 56 * 1024 * 1024

f32 = jnp.float32
bf16 = jnp.bfloat16


def _dot(a, b):
    return jnp.dot(a, b, preferred_element_type=f32)


def _dot_nt(a, b):
    return lax.dot_general(a, b, (((1,), (1,)), ((), ())), preferred_element_type=f32)


def _sigmoid(x):
    return 1.0 / (1.0 + jnp.exp(-x))


def _silu(x):
    return x * _sigmoid(x)


def _log_sigmoid(x):
    return jnp.minimum(x, 0.0) - jnp.log(1.0 + jnp.exp(-jnp.abs(x)))


def _split_parts(x, terms):
    parts, r = [], x
    for i in range(terms):
        p = r.astype(bf16)
        parts.append(p)
        if i + 1 < terms:
            r = r - p.astype(f32)
    return parts


def _split_terms(x, terms=SPLIT_TERMS):
    return jnp.concatenate(_split_parts(x, terms), axis=1)


def _exact_tril_dot(tril3, x):
    return _dot(tril3, jnp.concatenate(_split_parts(x, SPLIT_TERMS), axis=0))


def _rope(x, cos_t, sin_t, first_half):
    up = pltpu.roll(x, LANES - HEAD_DIM // 2, axis=1)
    down = pltpu.roll(x, HEAD_DIM // 2, axis=1)
    return x * cos_t + jnp.where(first_half, up, down) * sin_t


def _rms_project(x_ref, ngain_ref, win_ref, proj_sc):
    xf = x_ref[...]
    u = xf * lax.rsqrt(jnp.mean(xf * xf, axis=1, keepdims=True) + NORM_EPS) * ngain_ref[...]
    u16 = u.astype(bf16)
    for c0 in range(0, P_COLS, PROJ_COL_BLOCK):
        c1 = min(c0 + PROJ_COL_BLOCK, P_COLS)
        proj_sc[:, c0:c1] = _dot_nt(u16, win_ref[c0:c1, :])


def _out_project(x_ref, mix_sc, wout_ref, fgain_ref, y_ref, final):
    y = x_ref[...] + _dot(mix_sc[...], wout_ref[...])
    if final:
        y = y * lax.rsqrt(jnp.mean(y * y, axis=1, keepdims=True) + NORM_EPS) * fgain_ref[...]
    y_ref[...] = y


def _pair_norm(x, jj, gain):
    mean = _dot(_split_terms(x, 2), jj)
    xc = x - mean
    var = _dot(_split_terms(xc * xc, 2), jj)
    return xc * lax.rsqrt(var + NORM_EPS) * gain


def _prompt_kernel(x_ref, ngain_ref, win_ref, gbias_ref, mgain_ref, rgain_ref, sinks_ref, wout_ref,
                   fgain_ref, tril3_ref, maskadd_ref, dmat_ref, rslab_ref, biascp_ref, cos_ref, sin_ref,
                   selh_ref, selp_ref, jj_ref, eye_ref,
                   y_ref, c_out, n_out, m_out, s_out, k_out, v_out,
                   proj_sc, mix_sc, cn_sc, sb_sc, m_sc, kp_sc, vp_sc, *, chunks, final):
    step = pl.program_id(1)
    last_step = pl.num_programs(1) - 1

    _rms_project(x_ref, ngain_ref, win_ref, proj_sc)

    @pl.when(step == 0)
    def _():
        cn_sc[...] = jnp.zeros_like(cn_sc)
        sb_sc[...] = jnp.zeros_like(sb_sc)
        m_sc[...] = jnp.zeros_like(m_sc)
        kp_sc[...] = jnp.zeros_like(kp_sc)
        vp_sc[...] = jnp.zeros_like(vp_sc)

    lane = lax.broadcasted_iota(jnp.int32, (ROWS, LANES), 1)
    row = lax.broadcasted_iota(jnp.int32, (ROWS, LANES), 0)
    left = lane < HEAD_DIM
    first_half = (lane & (HEAD_DIM - 1)) < (HEAD_DIM // 2)
    blockdiag = (row < HEAD_DIM) == left
    row2 = lax.broadcasted_iota(jnp.int32, (ROWS, 2 * LANES), 0)
    lane2w = lax.broadcasted_iota(jnp.int32, (ROWS, 2 * LANES), 1)
    blockdiag2 = (row2 < HEAD_DIM) == ((lane2w & (LANES - 1)) < HEAD_DIM)
    ones_l = jnp.where(left, 1.0, 0.0).astype(bf16)
    ones_r = jnp.where(left, 0.0, 1.0).astype(bf16)
    ones16 = jnp.ones((ROWS, LANES), bf16)

    def half(x, side):
        keep = left if side == 0 else jnp.logical_not(left)
        return jnp.where(keep, x, 0.0).astype(bf16)

    def chunk_body(ci):
        rows = slice(ci * ROWS, (ci + 1) * ROWS)
        eye16 = eye_ref[...]
        jj = jj_ref[...]

        gates = proj_sc[rows, OFF_G:OFF_G + GATE_PAD] + gbias_ref[...]
        bcum = _exact_tril_dot(tril3_ref[...], _log_sigmoid(gates))
        zb = pltpu.roll(bcum, LANES - M_HEADS, axis=1)
        head_col = lane < M_HEADS
        r_mat = jnp.where(head_col, gates - zb, 0.0)
        cm = r_mat
        sh = 1
        while sh < ROWS:
            cm = jnp.where(row >= sh, jnp.maximum(cm, pltpu.roll(cm, sh, axis=0)), cm)
            sh *= 2
        mprev = m_sc[...]
        mx = jnp.maximum(mprev, cm)
        gm = mprev - mx
        em = jnp.where(head_col, -(zb + mx), 0.0)
        mx_last = jnp.broadcast_to(mx[ROWS - 1:ROWS, :], (ROWS, LANES))
        m_sc[...] = jnp.where(head_col, jnp.broadcast_to((zb + mx)[ROWS - 1:ROWS, :], (ROWS, LANES)), 0.0)
        mx_b = _dot_nt(_split_terms(mx), selh_ref[...])
        winter_b = jnp.exp(_dot_nt(_split_terms(gm), selp_ref[...]))
        emt_b = jnp.exp(_dot_nt(_split_terms(em), selp_ref[...]))
        ws_b = jnp.exp(_dot_nt(_split_terms(r_mat - mx_last), selp_ref[...]))
        r_t = r_mat.T
        maskadd = maskadd_ref[...]

        for p in range(PAIRS):
            ps = slice(p * LANES, (p + 1) * LANES)
            q = proj_sc[rows, OFF_MQ + p * LANES:OFF_MQ + (p + 1) * LANES]
            k = proj_sc[rows, OFF_MK + p * LANES:OFF_MK + (p + 1) * LANES] * QK_SCALE
            v = proj_sc[rows, OFF_MV + p * LANES:OFF_MV + (p + 1) * LANES]
            q16, k16, v16 = q.astype(bf16), k.astype(bf16), v.astype(bf16)
            cn = cn_sc[p]
            acc = jnp.concatenate([winter_b[:, ps]] * 2, axis=1) * _dot(q16, cn.astype(bf16))
            for side in range(2):
                h = 2 * p + side
                qk = _dot_nt(half(q, side), k16)
                w = jnp.exp((r_t[h:h + 1, :] + maskadd) - mx_b[:, h * ROWS:(h + 1) * ROWS]) * qk
                vn = jnp.concatenate([half(v, side), ones_l if side == 0 else ones_r], axis=1)
                acc = acc + _dot(w.astype(bf16), vn)
            hh = acc[:, :LANES] / jnp.maximum(jnp.abs(acc[:, LANES:]), emt_b[:, ps])
            kw16 = (k * ws_b[:, ps]).astype(bf16)
            kwt16 = _dot_nt(eye16, kw16).astype(bf16)
            dcn = _dot(kwt16, jnp.concatenate([v16, ones16], axis=1))
            decay = winter_b[ROWS - 1:ROWS, ps]
            cn_sc[p] = (jnp.concatenate([decay, decay], axis=1) * cn
                        + jnp.where(blockdiag2, dcn, 0.0))
            hm = _sigmoid(proj_sc[rows, OFF_MO + p * LANES:OFF_MO + (p + 1) * LANES]) * hh
            out = _pair_norm(hm, jj, mgain_ref[:, ps]) * _silu(
                proj_sc[rows, OFF_MZ + p * LANES:OFF_MZ + (p + 1) * LANES])
            mix_sc[rows, p * LANES:(p + 1) * LANES] = out.astype(bf16)

        cos_t = cos_ref[rows, :]
        sin_t = sin_ref[rows, :]
        for p in range(PAIRS):
            ps = slice(p * LANES, (p + 1) * LANES)
            q = _rope(proj_sc[rows, OFF_RQ + p * LANES:OFF_RQ + (p + 1) * LANES], cos_t, sin_t, first_half)
            k = _rope(proj_sc[rows, OFF_RK + p * LANES:OFF_RK + (p + 1) * LANES], cos_t, sin_t,
                      first_half) * QK_SCALE
            v = proj_sc[rows, OFF_RV + p * LANES:OFF_RV + (p + 1) * LANES]
            q16, k16, v16 = q.astype(bf16), k.astype(bf16), v.astype(bf16)
            sb = sb_sc[p]
            acc = rslab_ref[0, p] * _dot(q16, sb.astype(bf16))
            for side in range(2):
                h = 2 * p + side
                scores = _dot_nt(half(q, side), k16) * dmat_ref[h]
                acc = acc + _dot(scores.astype(bf16), half(v, side))
            kt16 = (k * rslab_ref[1, p]).astype(bf16)
            ktt16 = _dot_nt(eye16, kt16).astype(bf16)
            sb_sc[p] = rslab_ref[2, p] * sb + jnp.where(blockdiag, _dot(ktt16, v16), 0.0)
            out = _pair_norm(acc, jj, rgain_ref[:, ps]) * _silu(
                proj_sc[rows, OFF_RZ + p * LANES:OFF_RZ + (p + 1) * LANES])
            mix_sc[rows, M_DIM + p * LANES:M_DIM + (p + 1) * LANES] = out.astype(bf16)

        kcur = proj_sc[rows, OFF_AK:OFF_AK + A_KV_DIM]
        vcur = proj_sc[rows, OFF_AV:OFF_AV + A_KV_DIM]
        kprev, vprev = kp_sc[...], vp_sc[...]
        kk16 = jnp.concatenate([kcur, kprev], axis=0).astype(bf16)
        vv = [jnp.concatenate([jnp.concatenate([half(vcur, s), ones_l if s == 0 else ones_r], axis=1),
                               jnp.concatenate([half(vprev, s), ones_l if s == 0 else ones_r], axis=1)],
                              axis=0) for s in range(2)]
        if ci == 0:
            pen = jnp.where(step == 0, -jnp.inf, 0.0).astype(f32)
            lane2 = lax.broadcasted_iota(jnp.int32, (1, 2 * ROWS), 1)
            pen_row = jnp.where(lane2 >= ROWS, pen, 0.0)
        else:
            pen_row = None
        for j in range(KV_GROUP):
            q = proj_sc[rows, OFF_AQ + j * LANES:OFF_AQ + (j + 1) * LANES] * QK_SCALE
            acc = None
            esink = []
            for side in range(2):
                h = ATTN_HEAD_ORDER[2 * j + side]
                s = _dot_nt(half(q, side), kk16) + biascp_ref[h]
                if pen_row is not None:
                    s = s + pen_row
                sink = sinks_ref[h]
                m = jnp.maximum(jnp.max(jnp.maximum(s[:, :ROWS], s[:, ROWS:]), axis=1, keepdims=True), sink)
                part = _dot(jnp.exp(s - m).astype(bf16), vv[side])
                acc = part if acc is None else acc + part
                esink.append(jnp.exp(sink - m))
            den = acc[:, LANES:] + jnp.where(left, esink[0], esink[1])
            out = (acc[:, :LANES] / den) * _silu(proj_sc[rows, OFF_AZ + j * LANES:OFF_AZ + (j + 1) * LANES])
            mix_sc[rows, M_DIM + R_DIM + j * LANES:M_DIM + R_DIM + (j + 1) * LANES] = out.astype(bf16)
        kp_sc[...] = kcur
        vp_sc[...] = vcur

    for ci in range(chunks):
        chunk_body(ci)

    _out_project(x_ref, mix_sc, wout_ref, fgain_ref, y_ref, final)

    @pl.when(step == last_step)
    def _():
        for p in range(PAIRS):
            cn = cn_sc[p]
            sb = sb_sc[p]
            n_t = cn[:, LANES:].T
            for side in range(2):
                h = 2 * p + side
                blk = slice(side * HEAD_DIM, (side + 1) * HEAD_DIM)
                c_out[0, h] = cn[blk, blk]
                s_out[0, h] = sb[blk, blk]
                n_out[0, h:h + 1, :] = n_t[side * HEAD_DIM:side * HEAD_DIM + 1, blk]
        m_out[0] = m_sc[...]
        k_out[0] = kp_sc[...]
        v_out[0] = vp_sc[...]


def _staged_prompt_kernel(x_ref, ngain_ref, win_ref, gbias_ref, mgain_ref, rgain_ref, sinks_ref, wout_ref,
                          fgain_ref, tril3_ref, maskadd_ref, dmat2_ref, rslab_ref, biasall_ref, cos_ref,
                          sin_ref, selh_ref, selp_ref, jj_ref, eye_ref,
                          y_ref, c_out, n_out, m_out, s_out, k_out, v_out,
                          proj_sc, mix_sc, cn_sc, sb_sc, m_sc, kp_sc, vp_sc, *, chunks, final):
    step = pl.program_id(1)
    last_step = pl.num_programs(1) - 1

    xf = x_ref[...]
    u16 = (xf * lax.rsqrt(jnp.mean(xf * xf, axis=1, keepdims=True) + NORM_EPS) * ngain_ref[...]).astype(bf16)
    col_blocks = [(c0, min(c0 + PROJ_COL_BLOCK, P_COLS)) for c0 in range(0, P_COLS, PROJ_COL_BLOCK)]
    half_rows = (chunks // 2) * ROWS if chunks > 1 else chunks * ROWS

    def project(r0, r1, c0, c1):
        proj_sc[r0:r1, c0:c1] = _dot_nt(u16[r0:r1], win_ref[c0:c1, :])

    for c0, c1 in col_blocks:
        project(0, half_rows, c0, c1)
    late_pieces = [(half_rows, chunks * ROWS, c0, c1) for c0, c1 in col_blocks] if half_rows < chunks * ROWS else []

    @pl.when(step == 0)
    def _():
        cn_sc[...] = jnp.zeros_like(cn_sc)
        sb_sc[...] = jnp.zeros_like(sb_sc)
        m_sc[...] = jnp.zeros_like(m_sc)
        kp_sc[...] = jnp.zeros_like(kp_sc)
        vp_sc[...] = jnp.zeros_like(vp_sc)

    lane = lax.broadcasted_iota(jnp.int32, (ROWS, LANES), 1)
    row = lax.broadcasted_iota(jnp.int32, (ROWS, LANES), 0)
    left = lane < HEAD_DIM
    first_half = (lane & (HEAD_DIM - 1)) < (HEAD_DIM // 2)
    head_col = lane < M_HEADS
    blockdiag = (row < HEAD_DIM) == left
    row2 = lax.broadcasted_iota(jnp.int32, (ROWS, 2 * LANES), 0)
    lane2w = lax.broadcasted_iota(jnp.int32, (ROWS, 2 * LANES), 1)
    left2 = (lane2w & (LANES - 1)) < HEAD_DIM
    blockdiag2 = (row2 < HEAD_DIM) == left2
    ones16 = jnp.ones((ROWS, LANES), bf16)

    def halves(x):
        return jnp.concatenate([jnp.where(left, x, 0.0), jnp.where(left, 0.0, x)], axis=0).astype(bf16)

    def pick(x, mask):
        return jnp.where(mask, x[:ROWS], x[ROWS:])

    def chunk_body(ci):
        rows = slice(ci * ROWS, (ci + 1) * ROWS)

        def proj(off, width=LANES):
            return proj_sc[rows, off:off + width]

        eye16 = eye_ref[...]

        gates = proj(OFF_G) + gbias_ref[...]
        bcum = _exact_tril_dot(tril3_ref[...], _log_sigmoid(gates))

        cos_t, sin_t = cos_ref[rows, :], sin_ref[rows, :]
        r_sc, r_inter, r_v16 = [], [], []
        for p in range(PAIRS):
            q = _rope(proj(OFF_RQ + p * LANES), cos_t, sin_t, first_half)
            k = _rope(proj(OFF_RK + p * LANES), cos_t, sin_t, first_half) * QK_SCALE
            v16 = proj(OFF_RV + p * LANES).astype(bf16)
            sb = sb_sc[p]
            r_sc.append(_dot_nt(halves(q), k.astype(bf16)))
            r_inter.append(_dot(q.astype(bf16), sb.astype(bf16)))
            ktt16 = _dot_nt(eye16, (k * rslab_ref[1, p]).astype(bf16)).astype(bf16)
            sb_sc[p] = rslab_ref[2, p] * sb + jnp.where(blockdiag, _dot(ktt16, v16), 0.0)
            r_v16.append(v16)
        yield

        kcur, vcur = proj(OFF_AK), proj(OFF_AV)
        kprev, vprev = kp_sc[...], vp_sc[...]
        kk16 = jnp.concatenate([kcur, kprev], axis=0).astype(bf16)
        vv16 = jnp.concatenate([jnp.concatenate([vcur.astype(bf16), ones16], axis=1),
                                jnp.concatenate([vprev.astype(bf16), ones16], axis=1)], axis=0)
        kp_sc[...] = kcur
        vp_sc[...] = vcur
        a_q = jnp.concatenate([halves(proj(OFF_AQ + j * LANES) * QK_SCALE) for j in range(KV_GROUP)], axis=0)
        a_s = _dot_nt(a_q, kk16) + biasall_ref[...]
        if ci == 0:
            pen = jnp.where(step == 0, -jnp.inf, 0.0).astype(f32)
            a_s = a_s + jnp.where(lax.broadcasted_iota(jnp.int32, (1, 2 * ROWS), 1) >= ROWS, pen, 0.0)

        m_qk, m_q16, m_k, m_v16 = [], [], [], []
        for p in range(PAIRS):
            q = proj(OFF_MQ + p * LANES)
            k = proj(OFF_MK + p * LANES) * QK_SCALE
            m_qk.append(_dot_nt(halves(q), k.astype(bf16)))
            m_q16.append(q.astype(bf16))
            m_k.append(k)
            m_v16.append(proj(OFF_MV + p * LANES).astype(bf16))
        yield

        zb = pltpu.roll(bcum, LANES - M_HEADS, axis=1)
        r_mat = jnp.where(head_col, gates - zb, 0.0)
        cm = r_mat
        sh = 1
        while sh < ROWS:
            cm = jnp.where(row >= sh, jnp.maximum(cm, pltpu.roll(cm, sh, axis=0)), cm)
            sh *= 2
        mprev = m_sc[...]
        mx = jnp.maximum(mprev, cm)
        gm = mprev - mx
        em = jnp.where(head_col, -(zb + mx), 0.0)
        mx_last = jnp.broadcast_to(mx[ROWS - 1:ROWS, :], (ROWS, LANES))
        m_sc[...] = jnp.where(head_col, jnp.broadcast_to((zb + mx)[ROWS - 1:ROWS, :], (ROWS, LANES)), 0.0)
        mx_b = _dot_nt(_split_terms(mx), selh_ref[...])
        slabs = jnp.exp(_dot_nt(_split_terms(jnp.concatenate([gm, em, r_mat - mx_last], axis=0)),
                                selp_ref[...]))
        winter_b, emt_b, ws_b = slabs[:ROWS], slabs[ROWS:2 * ROWS], slabs[2 * ROWS:]
        r_t = r_mat.T
        yield

        outs = []
        r_acc = []
        for p in range(PAIRS):
            o = _dot((r_sc[p] * dmat2_ref[p]).astype(bf16), r_v16[p])
            r_acc.append(pick(o, left) + rslab_ref[0, p] * r_inter[p])

        a_out = []
        a_p = []
        for blk in range(A_HEADS):
            s = a_s[blk * ROWS:(blk + 1) * ROWS]
            sink = sinks_ref[ATTN_HEAD_ORDER[blk]]
            m = jnp.maximum(jnp.max(jnp.maximum(s[:, :ROWS], s[:, ROWS:]), axis=1, keepdims=True), sink)
            a_p.append(jnp.exp(s - m).astype(bf16))
            a_out.append(jnp.exp(sink - m))
        a_pv = _dot(jnp.concatenate(a_p, axis=0), vv16)
        yield

        maskadd = maskadd_ref[...]
        for p in range(PAIRS):
            ps = slice(p * LANES, (p + 1) * LANES)
            cn = cn_sc[p]
            w = jnp.concatenate(
                [jnp.exp((r_t[2 * p + side:2 * p + side + 1, :] + maskadd)
                         - mx_b[:, (2 * p + side) * ROWS:(2 * p + side + 1) * ROWS]) for side in range(2)],
                axis=0) * m_qk[p]
            acc = (pick(_dot(w.astype(bf16), jnp.concatenate([m_v16[p], ones16], axis=1)), left2)
                   + jnp.concatenate([winter_b[:, ps]] * 2, axis=1) * _dot(m_q16[p], cn.astype(bf16)))
            hh = acc[:, :LANES] / jnp.maximum(jnp.abs(acc[:, LANES:]), emt_b[:, ps])
            outs.append(_sigmoid(proj(OFF_MO + p * LANES)) * hh)
            kwt16 = _dot_nt(eye16, (m_k[p] * ws_b[:, ps]).astype(bf16)).astype(bf16)
            dcn = _dot(kwt16, jnp.concatenate([m_v16[p], ones16], axis=1))
            decay = winter_b[ROWS - 1:ROWS, ps]
            cn_sc[p] = jnp.concatenate([decay, decay], axis=1) * cn + jnp.where(blockdiag2, dcn, 0.0)
        outs.extend(r_acc)
        yield

        x4 = jnp.concatenate(outs, axis=0)
        jj = jj_ref[...]
        xc = x4 - _dot(_split_terms(x4, 2), jj)
        var = _dot(_split_terms(xc * xc, 2), jj)
        y4 = xc * lax.rsqrt(var + NORM_EPS)
        for i in range(2 * PAIRS):
            gain = (mgain_ref if i < PAIRS else rgain_ref)[:, (i % PAIRS) * LANES:(i % PAIRS + 1) * LANES]
            zoff = (OFF_MZ if i < PAIRS else OFF_RZ) + (i % PAIRS) * LANES
            out = y4[i * ROWS:(i + 1) * ROWS] * gain * _silu(proj(zoff))
            mix_sc[rows, i * LANES:(i + 1) * LANES] = out.astype(bf16)
        for j in range(KV_GROUP):
            acc = pick(a_pv[2 * j * ROWS:(2 * j + 2) * ROWS], left2)
            den = acc[:, LANES:] + jnp.where(left, a_out[2 * j], a_out[2 * j + 1])
            out = (acc[:, :LANES] / den) * _silu(proj(OFF_AZ + j * LANES))
            mix_sc[rows, M_DIM + R_DIM + j * LANES:M_DIM + R_DIM + (j + 1) * LANES] = out.astype(bf16)

    def out_project(r0, r1, c0, c1):
        y_ref[r0:r1, c0:c1] = x_ref[r0:r1, c0:c1] + _dot(mix_sc[r0:r1, :], wout_ref[:, c0:c1])

    out_blocks = [(c0, min(c0 + PROJ_COL_BLOCK, D_MODEL)) for c0 in range(0, D_MODEL, PROJ_COL_BLOCK)]
    early_out = [(0, half_rows, c0, c1) for c0, c1 in out_blocks] if half_rows < chunks * ROWS else []
    final_out = [(half_rows if early_out else 0, chunks * ROWS, c0, c1) for c0, c1 in out_blocks]

    def fill_mxu(ci):
        if late_pieces:
            project(*late_pieces.pop(0))
        elif early_out and (ci - 1) * ROWS >= half_rows:
            out_project(*early_out.pop(0))

    parts = [chunk_body(ci) for ci in range(chunks)]
    for ci in range(chunks + 1):
        if ci * ROWS >= half_rows:
            while late_pieces:
                project(*late_pieces.pop(0))
        for _ in range(3):
            if ci < chunks:
                next(parts[ci])
                fill_mxu(ci)
            if ci > 0:
                next(parts[ci - 1], None)
                fill_mxu(ci)
    for piece in early_out + final_out:
        out_project(*piece)
    if final:
        y = y_ref[...]
        y_ref[...] = y * lax.rsqrt(jnp.mean(y * y, axis=1, keepdims=True) + NORM_EPS) * fgain_ref[...]

    @pl.when(step == last_step)
    def _():
        for p in range(PAIRS):
            cn = cn_sc[p]
            sb = sb_sc[p]
            n_t = cn[:, LANES:].T
            for side in range(2):
                h = 2 * p + side
                blk = slice(side * HEAD_DIM, (side + 1) * HEAD_DIM)
                c_out[0, h] = cn[blk, blk]
                s_out[0, h] = sb[blk, blk]
                n_out[0, h:h + 1, :] = n_t[side * HEAD_DIM:side * HEAD_DIM + 1, blk]
        m_out[0] = m_sc[...]
        k_out[0] = kp_sc[...]
        v_out[0] = vp_sc[...]


def _head_norm(x, gain):
    mu = jnp.mean(x, axis=1, keepdims=True)
    xc = x - mu
    var = jnp.mean(xc * xc, axis=1, keepdims=True)
    return xc * lax.rsqrt(var + NORM_EPS) * gain


def _group_last(x, groups):
    n = x.shape[1]
    glen = ROWS // groups
    x3 = x.reshape(groups, glen, n)
    return jnp.broadcast_to(x3[:, glen - 1:glen, :], (groups, glen, n)).reshape(ROWS, n)


def _state_rows(col, groups):
    glen = ROWS // groups
    wide = jnp.broadcast_to(col, (ROWS, HEAD_DIM)).reshape(groups, glen, HEAD_DIM)
    per_group = wide[:, 0:1, :]
    rows = jnp.broadcast_to(per_group, (groups, HEAD_DIM, HEAD_DIM)).reshape(groups * HEAD_DIM, HEAD_DIM)
    return rows, per_group.reshape(groups, HEAD_DIM)


def _cols_to_mat(cols, lane):
    acc = jnp.zeros((ROWS, LANES), f32)
    for h, c in enumerate(cols):
        acc = jnp.where(lane == h, c, acc)
    return acc


def _sample_kernel(x_ref, ngain_ref, win_ref, gbias_ref, mgain_ref, rgain_ref, sinks_ref, wout_ref,
                   fgain_ref, tril3_ref, maskadd_ref, dmat_ref, rc_ref, biasc_ref, biasp_ref, cos_ref, sin_ref,
                   c_in, n_in, m_in, s_in, k_in, v_in,
                   y_ref, c_out, n_out, m_out, s_out, k_out, v_out,
                   proj_sc, mix_sc, xcur_sc, qb_sc, sp_sc, pp_sc, ob_sc, *, groups, ret_full):
    glen = ROWS // groups
    glen_log2 = glen.bit_length() - 1
    hd_log2 = HEAD_DIM.bit_length() - 1
    step = pl.program_id(0)
    layer = pl.program_id(1)
    last_layer = pl.num_programs(1) - 1

    @pl.when(layer == 0)
    def _():
        xcur_sc[...] = x_ref[...]

    _rms_project(xcur_sc, ngain_ref, win_ref, proj_sc)

    @pl.when(jnp.logical_and(step == 0, layer == 0))
    def _():
        qb_sc[...] = jnp.zeros_like(qb_sc)

    lane = lax.broadcasted_iota(jnp.int32, (ROWS, LANES), 1)
    first_half = (lane & (HEAD_DIM - 1)) < (HEAD_DIM // 2)
    rows = slice(0, ROWS)

    r_i = lax.broadcasted_iota(jnp.int32, (ROWS, groups * HEAD_DIM), 0)
    c_i = lax.broadcasted_iota(jnp.int32, (ROWS, groups * HEAD_DIM), 1)
    blk = (r_i >> glen_log2) == (c_i >> hd_log2)
    r_t = lax.broadcasted_iota(jnp.int32, (groups * HEAD_DIM, ROWS), 0)
    c_t = lax.broadcasted_iota(jnp.int32, (groups * HEAD_DIM, ROWS), 1)
    blk_t = (r_t >> hd_log2) == (c_t >> glen_log2)

    def q_times_state(qh, st):
        qt = jnp.where(blk, jnp.concatenate([qh] * groups, axis=1), 0.0)
        return _dot(qt.astype(bf16), st.astype(bf16))

    def state_increment(kt_h, vh16):
        kt = jnp.where(blk_t, jnp.concatenate([kt_h] * groups, axis=0), 0.0)
        return _dot(kt.astype(bf16), vh16)

    gates = proj_sc[rows, OFF_G:OFF_G + GATE_PAD] + gbias_ref[...]
    bcum = _exact_tril_dot(tril3_ref[...], _log_sigmoid(gates))
    z = jnp.where(lane < M_HEADS, gates, bcum)
    zt = z.T
    zb = pltpu.roll(z, LANES - M_HEADS, axis=1)
    maskadd = maskadd_ref[...]

    dlogs, mintra = [], []
    for h in range(M_HEADS):
        bcol = zb[:, h:h + 1]
        dlog = (bcol - zt[M_HEADS + h:M_HEADS + h + 1, :]) + zt[h:h + 1, :] + maskadd
        dlogs.append(dlog)
        mintra.append(jnp.max(dlog, axis=1, keepdims=True))
    mprev = m_in[...]
    g_all = zb + mprev
    mt_all = jnp.maximum(g_all, _cols_to_mat(mintra, lane))
    winter_all = jnp.exp(g_all - mt_all)
    emt_all = jnp.exp(-mt_all)
    mlast_all = _group_last(mt_all, groups)
    blast_all = _group_last(zb, groups)
    decay_all = jnp.exp(blast_all + mprev - mlast_all)
    ws_all = jnp.exp(blast_all - zb + z - mlast_all)

    mq = proj_sc[rows, OFF_MQ:OFF_MQ + M_DIM]
    mk = proj_sc[rows, OFF_MK:OFF_MK + M_DIM] * QK_SCALE
    mv = proj_sc[rows, OFF_MV:OFF_MV + M_DIM]
    kw_parts = []
    for h in range(M_HEADS):
        hs = slice(h * HEAD_DIM, (h + 1) * HEAD_DIM)
        kw_parts.append(mk[:, hs] * ws_all[:, h:h + 1])
    kwt = jnp.concatenate(kw_parts, axis=1).T

    hm = []
    for h in range(M_HEADS):
        hs = slice(h * HEAD_DIM, (h + 1) * HEAD_DIM)
        qh, kh, vh = mq[:, hs], mk[:, hs], mv[:, hs]
        vh16 = vh.astype(bf16)
        mt = mt_all[:, h:h + 1]
        winter = winter_all[:, h:h + 1]
        wintra = jnp.exp(dlogs[h] - mt) * _dot_nt(qh.astype(bf16), kh.astype(bf16))
        c_h = c_in[:, h].reshape(groups * HEAD_DIM, HEAD_DIM)
        num = winter * q_times_state(qh, c_h) + _dot(wintra.astype(bf16), vh16)
        n_g = n_in[h]
        n_rows = jnp.broadcast_to(n_g.reshape(groups, 1, HEAD_DIM),
                                  (groups, glen, HEAD_DIM)).reshape(ROWS, HEAD_DIM)
        nq = (winter * jnp.sum(qh * n_rows, axis=1, keepdims=True)
              + jnp.sum(wintra, axis=1, keepdims=True))
        hm.append(num / jnp.maximum(jnp.abs(nq), emt_all[:, h:h + 1]))
        dec_rows, dec_g = _state_rows(decay_all[:, h:h + 1], groups)
        c_new = dec_rows * c_h + state_increment(kwt[hs, :], vh16)
        c_out[:, h] = c_new.reshape(groups, HEAD_DIM, HEAD_DIM)
        n_out[h] = dec_g * n_g + jnp.sum(kw_parts[h].reshape(groups, glen, HEAD_DIM), axis=1)
    m_out[...] = mlast_all
    hm = jnp.concatenate(hm, axis=1)
    hm = _sigmoid(proj_sc[rows, OFF_MO:OFF_MO + M_DIM]) * hm
    mgain = mgain_ref[...]
    out_m = jnp.concatenate(
        [_head_norm(hm[:, h * HEAD_DIM:(h + 1) * HEAD_DIM], mgain[:, h * HEAD_DIM:(h + 1) * HEAD_DIM])
         for h in range(M_HEADS)], axis=1)
    out_m = out_m * _silu(proj_sc[rows, OFF_MZ:OFF_MZ + M_DIM])
    mix_sc[rows, 0:M_DIM] = out_m.astype(bf16)

    cos_t, sin_t = cos_ref[...], sin_ref[...]
    rq = jnp.concatenate([_rope(proj_sc[rows, OFF_RQ + p * LANES:OFF_RQ + (p + 1) * LANES], cos_t, sin_t,
                                first_half) for p in range(PAIRS)], axis=1)
    rk = jnp.concatenate([_rope(proj_sc[rows, OFF_RK + p * LANES:OFF_RK + (p + 1) * LANES], cos_t, sin_t,
                                first_half) for p in range(PAIRS)], axis=1) * QK_SCALE
    rv = proj_sc[rows, OFF_RV:OFF_RV + R_DIM]
    rc = rc_ref[...]
    rkt = jnp.concatenate(
        [rk[:, h * HEAD_DIM:(h + 1) * HEAD_DIM] * rc[:, R_HEADS + h:R_HEADS + h + 1]
         for h in range(R_HEADS)], axis=1).T
    hr = []
    for h in range(R_HEADS):
        hs = slice(h * HEAD_DIM, (h + 1) * HEAD_DIM)
        qh, kh = rq[:, hs], rk[:, hs]
        vh16 = rv[:, hs].astype(bf16)
        scores = _dot_nt(qh.astype(bf16), kh.astype(bf16)) * dmat_ref[h]
        s_h = s_in[:, h].reshape(groups * HEAD_DIM, HEAD_DIM)
        hr.append(_dot(scores.astype(bf16), vh16) + rc[:, h:h + 1] * q_times_state(qh, s_h))
        s_new = ret_full[h] * s_h + state_increment(rkt[hs, :], vh16)
        s_out[:, h] = s_new.reshape(groups, HEAD_DIM, HEAD_DIM)
    rgain = rgain_ref[...]
    out_r = jnp.concatenate(
        [_head_norm(hr[h], rgain[:, h * HEAD_DIM:(h + 1) * HEAD_DIM]) for h in range(R_HEADS)], axis=1)
    out_r = out_r * _silu(proj_sc[rows, OFF_RZ:OFF_RZ + R_DIM])
    mix_sc[rows, M_DIM:M_DIM + R_DIM] = out_r.astype(bf16)

    kcur = proj_sc[rows, OFF_AK:OFF_AK + A_KV_DIM]
    vcur = proj_sc[rows, OFF_AV:OFF_AV + A_KV_DIM]
    kcur16, vcur16 = kcur.astype(bf16), vcur.astype(bf16)

    def q_cols(h):
        c0 = OFF_AQ + ATTN_HEAD_POS[h] * HEAD_DIM
        return slice(c0, c0 + HEAD_DIM)

    for h in range(A_HEADS):
        off = (h // KV_GROUP) * HEAD_DIM
        qb_sc[:, h * glen:(h + 1) * glen, off:off + HEAD_DIM] = (
            proj_sc[rows, q_cols(h)].reshape(groups, glen, HEAD_DIM).astype(bf16))

    for b in range(groups):
        sp = _dot(qb_sc[b], k_in[b].astype(bf16))
        sp_sc[:, b * glen:(b + 1) * glen, :] = sp.reshape(A_HEADS, glen, WINDOW)

    dens, o_cur = [], []
    for h in range(A_HEADS):
        kv = h // KV_GROUP
        ks = slice(kv * HEAD_DIM, (kv + 1) * HEAD_DIM)
        qh16 = proj_sc[rows, q_cols(h)].astype(bf16)
        sc = _dot_nt(qh16, kcur16[:, ks]) * QK_SCALE + biasc_ref[h]
        sp = sp_sc[h] * QK_SCALE + biasp_ref[h]
        sink = sinks_ref[layer, h]
        m = jnp.maximum(jnp.maximum(jnp.max(sc, axis=1, keepdims=True),
                                    jnp.max(sp, axis=1, keepdims=True)), sink)
        pc = jnp.exp(sc - m)
        pp = jnp.exp(sp - m)
        dens.append(jnp.sum(pc, axis=1, keepdims=True) + jnp.sum(pp, axis=1, keepdims=True)
                    + jnp.exp(sink - m))
        o_cur.append(_dot(pc.astype(bf16), vcur16[:, ks]))
        pp_sc[:, h * glen:(h + 1) * glen, :] = pp.reshape(groups, glen, WINDOW).astype(bf16)

    for b in range(groups):
        ob = _dot_nt(pp_sc[b], v_in[b].astype(bf16))
        ob_sc[:, b * glen:(b + 1) * glen, :] = ob.reshape(A_HEADS, glen, A_KV_DIM)
    outs = []
    for h in ATTN_HEAD_ORDER:
        off = (h // KV_GROUP) * HEAD_DIM
        outs.append((o_cur[h] + ob_sc[h][:, off:off + HEAD_DIM]) / dens[h])
    kcur_t, vcur_t = kcur.T, vcur.T
    fresh = lane >= WINDOW - glen
    for b in range(groups):
        shift = (WINDOW - glen - b * glen) % LANES
        k_out[b] = jnp.where(fresh, pltpu.roll(kcur_t, shift, axis=1), pltpu.roll(k_in[b], WINDOW - glen, axis=1))
        v_out[b] = jnp.where(fresh, pltpu.roll(vcur_t, shift, axis=1), pltpu.roll(v_in[b], WINDOW - glen, axis=1))
    out_a = jnp.concatenate(outs, axis=1) * _silu(proj_sc[rows, OFF_AZ:OFF_AZ + A_DIM])
    mix_sc[rows, M_DIM + R_DIM:M_DIM + R_DIM + A_DIM] = out_a.astype(bf16)

    y = xcur_sc[...] + _dot(mix_sc[...], wout_ref[...])
    xcur_sc[...] = y

    @pl.when(layer == last_layer)
    def _():
        y_ref[...] = y * lax.rsqrt(jnp.mean(y * y, axis=1, keepdims=True) + NORM_EPS) * fgain_ref[...]


def _t5_bucket(dist):
    max_exact = N_BUCKETS // 2
    d = np.maximum(dist, 1).astype(np.float32)
    large = max_exact + (np.log(d / max_exact) / np.log(REL_MAX_DIST / max_exact)
                         * (N_BUCKETS - max_exact)).astype(np.int32)
    large = np.minimum(large, N_BUCKETS - 1)
    return np.where(dist < max_exact, dist, large).astype(np.int32)


def _static_tables(groups):
    glen = ROWS // groups
    r = np.arange(ROWS)
    grp, tau = r // glen, r % glen
    causal = (grp[:, None] == grp[None, :]) & (tau[None, :] <= tau[:, None])
    tril = causal.astype(np.float32)
    maskadd = np.where(causal, 0.0, -np.inf).astype(np.float32)
    log_g = np.log1p(-np.exp2(-5.0 - np.arange(R_HEADS, dtype=np.float64)))
    diff = (tau[:, None] - tau[None, :]).astype(np.float64)
    dmat = np.where(causal[None], np.exp(log_g[:, None, None] * np.maximum(diff, 0.0)[None]), 0.0)
    inter = np.exp(log_g[None, :] * (tau[:, None] + 1.0))
    tail = np.exp(log_g[None, :] * (glen - 1.0 - tau[:, None]))
    full = np.exp(log_g * glen)
    rc = np.zeros((ROWS, LANES), np.float64)
    rc[:, 0:R_HEADS] = inter
    rc[:, R_HEADS:2 * R_HEADS] = tail
    lane_head = np.arange(LANES) // HEAD_DIM
    rslab = np.zeros((3, PAIRS, ROWS, LANES), np.float64)
    for p in range(PAIRS):
        rslab[0, p] = inter[:, 2 * p + lane_head]
        rslab[1, p] = tail[:, 2 * p + lane_head]
        rslab[2, p] = full[2 * p + lane_head][None, :]
    selh = np.zeros((M_HEADS * ROWS, SPLIT_TERMS * LANES), np.float32)
    selp = np.zeros((PAIRS * LANES, SPLIT_TERMS * LANES), np.float32)
    for t in range(SPLIT_TERMS):
        for h in range(M_HEADS):
            selh[h * ROWS:(h + 1) * ROWS, t * LANES + h] = 1.0
        for p in range(PAIRS):
            for side in range(2):
                selp[p * LANES + side * HEAD_DIM:p * LANES + (side + 1) * HEAD_DIM, t * LANES + 2 * p + side] = 1.0
    jj = np.zeros((2 * LANES, LANES), np.float32)
    for t in range(2):
        for side in range(2):
            jj[t * LANES + side * HEAD_DIM:t * LANES + (side + 1) * HEAD_DIM,
               side * HEAD_DIM:(side + 1) * HEAD_DIM] = 1.0 / HEAD_DIM
    return dict(tril3=jnp.asarray(np.concatenate([tril] * SPLIT_TERMS, axis=1), bf16),
                maskadd=maskadd, dmat=dmat.astype(np.float32),
                dmat2=dmat.astype(np.float32).reshape(PAIRS, 2 * ROWS, ROWS),
                rc=rc.astype(np.float32), rslab=rslab.astype(np.float32),
                full=tuple(float(v) for v in full), causal=causal, tau=tau,
                selh=jnp.asarray(selh, bf16), selp=jnp.asarray(selp, bf16), jj=jnp.asarray(jj, bf16),
                eye=jnp.asarray(np.eye(ROWS, dtype=np.float32), bf16))


def _bias_tables(rel_table, tabs_s):
    tb = jnp.transpose(rel_table[_t5_bucket(np.arange(WINDOW))]).astype(f32)
    ninf = jnp.full((A_HEADS, WINDOW + 1), -jnp.inf, f32)
    rev = tb[:, :0:-1]

    def skew(u):
        t = jnp.tile(u, (1, WINDOW))[:, :WINDOW * 2 * WINDOW]
        return t.reshape(A_HEADS, WINDOW, 2 * WINDOW)[:, :, :WINDOW]

    bias_cur = skew(jnp.concatenate([tb[:, :1], ninf, rev], axis=1))
    bias_prev = skew(jnp.concatenate([ninf[:, :1], rev, ninf], axis=1))
    bias_cp = jnp.concatenate([bias_cur, bias_prev], axis=2)
    bias_cp = jnp.concatenate([bias_cp[h] for h in ATTN_HEAD_ORDER], axis=0)
    glen = int(tabs_s["tau"].max()) + 1
    reps = ROWS // glen
    bias_cs = jnp.where(jnp.asarray(tabs_s["causal"])[None],
                        jnp.tile(bias_cur[:, :glen, :glen], (1, reps, reps)), -jnp.inf)
    bias_ps = jnp.tile(bias_prev[:, :glen, :], (1, reps, 1))
    return bias_cp, bias_cs, bias_ps


def _rope_tables(pos):
    half = HEAD_DIM // 2
    inv = ROPE_BASE ** (-jnp.arange(half, dtype=f32) / half)
    ang = pos.astype(f32)[:, None] * inv[None, :]
    cos, sin = jnp.cos(ang), jnp.sin(ang)
    reps = LANES // HEAD_DIM
    cos_t = jnp.tile(jnp.concatenate([cos, cos], axis=1), (1, reps))
    sin_t = jnp.tile(jnp.concatenate([-sin, sin], axis=1), (1, reps))
    return cos_t, sin_t


def _const_spec(shape, nargs):
    zeros = (0,) * len(shape)
    if nargs == 1:
        return pl.BlockSpec(shape, lambda i: zeros)
    return pl.BlockSpec(shape, lambda i, j: zeros)


def _layer_spec(shape, layer, nargs):
    idx = (layer,) + (0,) * len(shape)
    if nargs == 1:
        return pl.BlockSpec((None,) + shape, lambda i: idx)
    return pl.BlockSpec((None,) + shape, lambda i, j: idx)


def _param_specs(layer, nargs):
    ls = functools.partial(_layer_spec, layer=layer, nargs=nargs)
    return [ls((1, D_MODEL)), ls((P_COLS, D_MODEL)), ls((1, LANES)), ls((1, M_DIM)), ls((1, R_DIM)),
            pl.BlockSpec(memory_space=pltpu.SMEM), ls((D_MODEL, D_MODEL)), _const_spec((1, D_MODEL), nargs)]


def _param_args(p, layer):
    return (p["norm_gain"], p["w_in"], p["gbias"], p["m_gain"], p["r_gain"], p["sinks"][layer], p["w_out"],
            p["fgain"])


def _prompt_layer(x, p, layer, tabs, bias_cp, cos_t, sin_t, final):
    B, T, _ = x.shape
    tb = min(PROMPT_ROWS, T)
    chunks = tb // ROWS
    nt = T // tb
    cs = functools.partial(_const_spec, nargs=2)
    in_specs = [pl.BlockSpec((None, tb, D_MODEL), lambda b, t: (b, t, 0))] + _param_specs(layer, 2) + [
        cs((ROWS, SPLIT_TERMS * ROWS)), cs((ROWS, ROWS)), cs((PAIRS, 2 * ROWS, ROWS)),
        cs((3, PAIRS, ROWS, LANES)), cs((A_HEADS * ROWS, 2 * ROWS)),
        pl.BlockSpec((tb, LANES), lambda b, t: (t, 0)), pl.BlockSpec((tb, LANES), lambda b, t: (t, 0)),
        cs((M_HEADS * ROWS, SPLIT_TERMS * LANES)), cs((PAIRS * LANES, SPLIT_TERMS * LANES)),
        cs((2 * LANES, LANES)), cs((ROWS, ROWS)),
    ]
    out_shape = (
        jax.ShapeDtypeStruct((B, T, D_MODEL), f32),
        jax.ShapeDtypeStruct((B, M_HEADS, HEAD_DIM, HEAD_DIM), f32),
        jax.ShapeDtypeStruct((B, M_HEADS, HEAD_DIM), f32),
        jax.ShapeDtypeStruct((B, ROWS, LANES), f32),
        jax.ShapeDtypeStruct((B, R_HEADS, HEAD_DIM, HEAD_DIM), f32),
        jax.ShapeDtypeStruct((B, WINDOW, A_KV_DIM), f32),
        jax.ShapeDtypeStruct((B, WINDOW, A_KV_DIM), f32),
    )
    out_specs = (
        pl.BlockSpec((None, tb, D_MODEL), lambda b, t: (b, t, 0)),
        pl.BlockSpec((1, M_HEADS, HEAD_DIM, HEAD_DIM), lambda b, t: (b, 0, 0, 0)),
        pl.BlockSpec((1, M_HEADS, HEAD_DIM), lambda b, t: (b, 0, 0)),
        pl.BlockSpec((1, ROWS, LANES), lambda b, t: (b, 0, 0)),
        pl.BlockSpec((1, R_HEADS, HEAD_DIM, HEAD_DIM), lambda b, t: (b, 0, 0, 0)),
        pl.BlockSpec((1, WINDOW, A_KV_DIM), lambda b, t: (b, 0, 0)),
        pl.BlockSpec((1, WINDOW, A_KV_DIM), lambda b, t: (b, 0, 0)),
    )
    kern = functools.partial(_staged_prompt_kernel, chunks=chunks, final=final)
    y, c, n, m, s, k, v = pl.pallas_call(
        kern, grid=(B, nt), in_specs=in_specs, out_specs=out_specs, out_shape=out_shape,
        scratch_shapes=[pltpu.VMEM((tb, P_COLS), f32), pltpu.VMEM((tb, D_MODEL), bf16),
                        pltpu.VMEM((PAIRS, ROWS, 2 * LANES), f32), pltpu.VMEM((PAIRS, ROWS, LANES), f32),
                        pltpu.VMEM((ROWS, LANES), f32), pltpu.VMEM((ROWS, A_KV_DIM), f32),
                        pltpu.VMEM((ROWS, A_KV_DIM), f32)],
        compiler_params=pltpu.CompilerParams(dimension_semantics=("arbitrary", "arbitrary"),
                                             vmem_limit_bytes=VMEM_LIMIT_BYTES),
        name="prompt_layer",
    )(x, *_param_args(p, layer), tabs["tril3"], tabs["maskadd"], tabs["dmat2"], tabs["rslab"], bias_cp,
      cos_t, sin_t, tabs["selh"], tabs["selp"], tabs["jj"], tabs["eye"])
    k = k.reshape(B, WINDOW, A_KV_HEADS, HEAD_DIM)
    v = v.reshape(B, WINDOW, A_KV_HEADS, HEAD_DIM)
    return y, c, n, m[:, 0, :M_HEADS], s, k, v


def _sample_path(x, states, p, tabs, bias_c, bias_p, cos_t, sin_t):
    B, T, _ = x.shape
    groups = ROWS // T
    nb = B // groups
    c0, n0, m0, s0, k0, v0 = states
    depth = c0.shape[0]
    x2 = x.reshape(B * T, D_MODEL)
    n0t = jnp.transpose(n0, (0, 2, 1, 3))
    m0r = jnp.pad(jnp.repeat(m0, T, axis=1), ((0, 0), (0, 0), (0, LANES - M_HEADS)))
    k0r = jnp.transpose(k0, (0, 1, 3, 4, 2)).reshape(depth, B, A_KV_DIM, WINDOW)
    v0r = jnp.transpose(v0, (0, 1, 3, 4, 2)).reshape(depth, B, A_KV_DIM, WINDOW)

    def cs(shape):
        zeros = (0,) * len(shape)
        return pl.BlockSpec(shape, lambda i, l: zeros)

    def per_layer(shape):
        zeros = (0,) * len(shape)
        return pl.BlockSpec((None,) + shape, lambda i, l: (l,) + zeros)

    st4 = pl.BlockSpec((None, groups, M_HEADS, HEAD_DIM, HEAD_DIM), lambda i, l: (l, i, 0, 0, 0))
    stn = pl.BlockSpec((None, M_HEADS, groups, HEAD_DIM), lambda i, l: (l, 0, i, 0))
    stm = pl.BlockSpec((None, ROWS, LANES), lambda i, l: (l, i, 0))
    stk = pl.BlockSpec((None, groups, A_KV_DIM, WINDOW), lambda i, l: (l, i, 0, 0))
    rows_spec = pl.BlockSpec((ROWS, D_MODEL), lambda i, l: (i, 0))
    in_specs = [
        rows_spec,
        per_layer((1, D_MODEL)), per_layer((P_COLS, D_MODEL)), per_layer((1, LANES)), per_layer((1, M_DIM)),
        per_layer((1, R_DIM)), pl.BlockSpec(memory_space=pltpu.SMEM), per_layer((D_MODEL, D_MODEL)),
        cs((1, D_MODEL)),
        cs((ROWS, SPLIT_TERMS * ROWS)), cs((ROWS, ROWS)), cs((R_HEADS, ROWS, ROWS)), cs((ROWS, LANES)),
        cs((A_HEADS, ROWS, ROWS)), cs((A_HEADS, ROWS, WINDOW)),
        cs((ROWS, LANES)), cs((ROWS, LANES)),
        st4, stn, stm, st4, stk, stk,
    ]
    out_shape = (
        jax.ShapeDtypeStruct((B * T, D_MODEL), f32),
        jax.ShapeDtypeStruct((depth, B, M_HEADS, HEAD_DIM, HEAD_DIM), f32),
        jax.ShapeDtypeStruct((depth, M_HEADS, B, HEAD_DIM), f32),
        jax.ShapeDtypeStruct((depth, B * T, LANES), f32),
        jax.ShapeDtypeStruct((depth, B, R_HEADS, HEAD_DIM, HEAD_DIM), f32),
        jax.ShapeDtypeStruct((depth, B, A_KV_DIM, WINDOW), f32),
        jax.ShapeDtypeStruct((depth, B, A_KV_DIM, WINDOW), f32),
    )
    out_specs = (rows_spec, st4, stn, stm, st4, stk, stk)
    kern = functools.partial(_sample_kernel, groups=groups, ret_full=tabs["full"])
    y, c, n, m, s, k, v = pl.pallas_call(
        kern, grid=(nb, depth), in_specs=in_specs, out_specs=out_specs, out_shape=out_shape,
        scratch_shapes=[pltpu.VMEM((ROWS, P_COLS), f32), pltpu.VMEM((ROWS, D_MODEL), bf16),
                        pltpu.VMEM((ROWS, D_MODEL), f32),
                        pltpu.VMEM((groups, A_HEADS * T, A_KV_DIM), bf16),
                        pltpu.VMEM((A_HEADS, ROWS, WINDOW), f32),
                        pltpu.VMEM((groups, A_HEADS * T, WINDOW), bf16),
                        pltpu.VMEM((A_HEADS, ROWS, A_KV_DIM), f32)],
        compiler_params=pltpu.CompilerParams(dimension_semantics=("arbitrary", "arbitrary"),
                                             vmem_limit_bytes=VMEM_LIMIT_BYTES),
        name="sample_path",
    )(x2, p["norm_gain"], p["w_in"], p["gbias"], p["m_gain"], p["r_gain"], p["sinks"], p["w_out"], p["fgain"],
      tabs["tril3"], tabs["maskadd"], tabs["dmat"], tabs["rc"], bias_c, bias_p,
      cos_t, sin_t, c0, n0t, m0r, s0, k0r, v0r)
    y = y.reshape(B, T, D_MODEL)
    n = jnp.transpose(n, (0, 2, 1, 3))
    m = m.reshape(depth, B, T, LANES)[:, :, 0, :M_HEADS]
    k = jnp.transpose(k.reshape(depth, B, A_KV_HEADS, HEAD_DIM, WINDOW), (0, 1, 4, 2, 3))
    v = jnp.transpose(v.reshape(depth, B, A_KV_HEADS, HEAD_DIM, WINDOW), (0, 1, 4, 2, 3))
    return y, c, n, m, s, k, v


def _prepare_params(norm_gain, w_in, mlstm_gate_bias, mlstm_norm_gain, ret_norm_gain, attn_sinks, w_out,
                    final_norm_gain):
    depth = w_in.shape[0]
    w_t = jnp.swapaxes(w_in, 1, 2)
    split = OFF_G + N_GATES
    aq0 = split + 4 * R_DIM
    akv0 = aq0 + A_DIM
    az0 = akv0 + 2 * A_KV_DIM

    def by_head(w):
        w = w.reshape(depth, A_HEADS, HEAD_DIM, D_MODEL)
        return jnp.concatenate([w[:, h] for h in ATTN_HEAD_ORDER], axis=1)

    w_in_p = jnp.concatenate(
        [w_t[:, :split], jnp.zeros((depth, GATE_PAD - N_GATES, D_MODEL), w_t.dtype), w_t[:, split:aq0],
         by_head(w_t[:, aq0:akv0]), w_t[:, akv0:az0], by_head(w_t[:, az0:])], axis=1).astype(bf16)
    wo16 = w_out.astype(bf16)
    a0 = M_DIM + R_DIM
    w_out_p = jnp.concatenate([wo16[:, :a0, :], by_head(wo16[:, a0:, :])], axis=1)
    gbias = jnp.pad(mlstm_gate_bias.reshape(depth, 1, N_GATES), ((0, 0), (0, 0), (0, LANES - N_GATES)))
    return dict(norm_gain=norm_gain.reshape(depth, 1, D_MODEL), w_in=w_in_p, gbias=gbias,
                m_gain=mlstm_norm_gain.reshape(depth, 1, M_DIM), r_gain=ret_norm_gain.reshape(depth, 1, R_DIM),
                sinks=attn_sinks, w_out=w_out_p, fgain=final_norm_gain.reshape(1, D_MODEL))


def kernel(x_prompt, x_sample, state_mlstm_C, state_mlstm_n, state_mlstm_m, state_ret_S, cache_win_k,
           cache_win_v, norm_gain, w_in, mlstm_gate_bias, mlstm_norm_gain, ret_norm_gain, attn_sinks,
           rel_bias_table, w_out, final_norm_gain):
    depth = w_in.shape[0]
    seq = x_prompt.shape[1]
    dec_seq = x_sample.shape[1]
    past_len = seq
    p = _prepare_params(norm_gain, w_in, mlstm_gate_bias, mlstm_norm_gain, ret_norm_gain, attn_sinks, w_out,
                        final_norm_gain)
    tabs_p = _static_tables(1)
    tabs_s = _static_tables(ROWS // dec_seq)
    bias_cp, bias_cs, bias_ps = _bias_tables(rel_bias_table, tabs_s)
    cos_p, sin_p = _rope_tables(jnp.arange(seq, dtype=jnp.int32))
    cos_s, sin_s = _rope_tables(past_len + (jnp.arange(ROWS, dtype=jnp.int32) % dec_seq))
    states = (state_mlstm_C, state_mlstm_n, state_mlstm_m, state_ret_S, cache_win_k, cache_win_v)

    xp = x_prompt
    p_states = []
    for layer in range(depth):
        xp, *sp = _prompt_layer(xp, p, layer, tabs_p, bias_cp, cos_p, sin_p, layer == depth - 1)
        p_states.append(sp)
    outs_p = [jnp.stack([p_states[l][i] for l in range(depth)]) for i in range(6)]
    xs, *outs_s = _sample_path(x_sample, states, p, tabs_s, bias_cs, bias_ps, cos_s, sin_s)
    return (xp, xs, *outs_p, *outs_s)
```

```python
import functools

import numpy as np
import jax
import jax.numpy as jnp
from jax import lax
from jax.experimental import pallas as pl
from jax.experimental.pallas import tpu as pltpu

D_MODEL = 1024
HEAD_DIM = 64
M_HEADS = 4
R_HEADS = 4
A_HEADS = 8
A_KV_HEADS = 2
KV_GROUP = A_HEADS // A_KV_HEADS
M_DIM = M_HEADS * HEAD_DIM
R_DIM = R_HEADS * HEAD_DIM
A_DIM = A_HEADS * HEAD_DIM
A_KV_DIM = A_KV_HEADS * HEAD_DIM
WINDOW = 128
N_BUCKETS = 32
REL_MAX_DIST = 128
ROPE_BASE = 10000.0
NORM_EPS = 1e-6
QK_SCALE = HEAD_DIM ** -0.5

LANES = 128
ROWS = 128
GATE_PAD = LANES
PAIRS = M_HEADS // 2
SPLIT_TERMS = 3

OFF_MQ = 0
OFF_MK = OFF_MQ + M_DIM
OFF_MV = OFF_MK + M_DIM
OFF_MO = OFF_MV + M_DIM
OFF_MZ = OFF_MO + M_DIM
OFF_G = OFF_MZ + M_DIM
OFF_RQ = OFF_G + GATE_PAD
OFF_RK = OFF_RQ + R_DIM
OFF_RV = OFF_RK + R_DIM
OFF_RZ = OFF_RV + R_DIM
OFF_AQ = OFF_RZ + R_DIM
OFF_AK = OFF_AQ + A_DIM
OFF_AV = OFF_AK + A_KV_DIM
OFF_AZ = OFF_AV + A_KV_DIM
P_COLS = OFF_AZ + A_DIM
N_GATES = 2 * M_HEADS
PROJ_COL_BLOCK = 512
ATTN_HEAD_ORDER = tuple(h for j in range(KV_GROUP) for h in (j, KV_GROUP + j))
ATTN_HEAD_POS = tuple(ATTN_HEAD_ORDER.index(h) for h in range(A_HEADS))

PROMPT_ROWS = 512
VMEM_LIMIT_BYTES = 56 * 1024 * 1024

f32 = jnp.float32
bf16 = jnp.bfloat16


def _dot(a, b):
    return jnp.dot(a, b, preferred_element_type=f32)


def _dot_nt(a, b):
    return lax.dot_general(a, b, (((1,), (1,)), ((), ())), preferred_element_type=f32)


def _sigmoid(x):
    return 1.0 / (1.0 + jnp.exp(-x))


def _silu(x):
    return x * _sigmoid(x)


def _log_sigmoid(x):
    return jnp.minimum(x, 0.0) - jnp.log(1.0 + jnp.exp(-jnp.abs(x)))


def _split_parts(x, terms):
    parts, r = [], x
    for i in range(terms):
        p = r.astype(bf16)
        parts.append(p)
        if i + 1 < terms:
            r = r - p.astype(f32)
    return parts


def _split_terms(x, terms=SPLIT_TERMS):
    return jnp.concatenate(_split_parts(x, terms), axis=1)


def _exact_tril_dot(tril3, x):
    return _dot(tril3, jnp.concatenate(_split_parts(x, SPLIT_TERMS), axis=0))


def _rope(x, cos_t, sin_t, first_half):
    up = pltpu.roll(x, LANES - HEAD_DIM // 2, axis=1)
    down = pltpu.roll(x, HEAD_DIM // 2, axis=1)
    return x * cos_t + jnp.where(first_half, up, down) * sin_t


def _rms_project(x_ref, ngain_ref, win_ref, proj_sc):
    xf = x_ref[...]
    u = xf * lax.rsqrt(jnp.mean(xf * xf, axis=1, keepdims=True) + NORM_EPS) * ngain_ref[...]
    u16 = u.astype(bf16)
    for c0 in range(0, P_COLS, PROJ_COL_BLOCK):
        c1 = min(c0 + PROJ_COL_BLOCK, P_COLS)
        proj_sc[:, c0:c1] = _dot_nt(u16, win_ref[c0:c1, :])


def _out_project(x_ref, mix_sc, wout_ref, fgain_ref, y_ref, final):
    y = x_ref[...] + _dot(mix_sc[...], wout_ref[...])
    if final:
        y = y * lax.rsqrt(jnp.mean(y * y, axis=1, keepdims=True) + NORM_EPS) * fgain_ref[...]
    y_ref[...] = y


def _pair_norm(x, jj, gain):
    mean = _dot(_split_terms(x, 2), jj)
    xc = x - mean
    var = _dot(_split_terms(xc * xc, 2), jj)
    return xc * lax.rsqrt(var + NORM_EPS) * gain


def _prompt_kernel(x_ref, ngain_ref, win_ref, gbias_ref, mgain_ref, rgain_ref, sinks_ref, wout_ref,
                   fgain_ref, tril3_ref, maskadd_ref, dmat_ref, rslab_ref, biascp_ref, cos_ref, sin_ref,
                   selh_ref, selp_ref, jj_ref, eye_ref,
                   y_ref, c_out, n_out, m_out, s_out, k_out, v_out,
                   proj_sc, mix_sc, cn_sc, sb_sc, m_sc, kp_sc, vp_sc, *, chunks, final):
    step = pl.program_id(1)
    last_step = pl.num_programs(1) - 1

    _rms_project(x_ref, ngain_ref, win_ref, proj_sc)

    @pl.when(step == 0)
    def _():
        cn_sc[...] = jnp.zeros_like(cn_sc)
        sb_sc[...] = jnp.zeros_like(sb_sc)
        m_sc[...] = jnp.zeros_like(m_sc)
        kp_sc[...] = jnp.zeros_like(kp_sc)
        vp_sc[...] = jnp.zeros_like(vp_sc)

    lane = lax.broadcasted_iota(jnp.int32, (ROWS, LANES), 1)
    row = lax.broadcasted_iota(jnp.int32, (ROWS, LANES), 0)
    left = lane < HEAD_DIM
    first_half = (lane & (HEAD_DIM - 1)) < (HEAD_DIM // 2)
    blockdiag = (row < HEAD_DIM) == left
    row2 = lax.broadcasted_iota(jnp.int32, (ROWS, 2 * LANES), 0)
    lane2w = lax.broadcasted_iota(jnp.int32, (ROWS, 2 * LANES), 1)
    blockdiag2 = (row2 < HEAD_DIM) == ((lane2w & (LANES - 1)) < HEAD_DIM)
    ones_l = jnp.where(left, 1.0, 0.0).astype(bf16)
    ones_r = jnp.where(left, 0.0, 1.0).astype(bf16)
    ones16 = jnp.ones((ROWS, LANES), bf16)

    def half(x, side):
        keep = left if side == 0 else jnp.logical_not(left)
        return jnp.where(keep, x, 0.0).astype(bf16)

    def chunk_body(ci):
        rows = slice(ci * ROWS, (ci + 1) * ROWS)
        eye16 = eye_ref[...]
        jj = jj_ref[...]

        gates = proj_sc[rows, OFF_G:OFF_G + GATE_PAD] + gbias_ref[...]
        bcum = _exact_tril_dot(tril3_ref[...], _log_sigmoid(gates))
        zb = pltpu.roll(bcum, LANES - M_HEADS, axis=1)
        head_col = lane < M_HEADS
        r_mat = jnp.where(head_col, gates - zb, 0.0)
        cm = r_mat
        sh = 1
        while sh < ROWS:
            cm = jnp.where(row >= sh, jnp.maximum(cm, pltpu.roll(cm, sh, axis=0)), cm)
            sh *= 2
        mprev = m_sc[...]
        mx = jnp.maximum(mprev, cm)
        gm = mprev - mx
        em = jnp.where(head_col, -(zb + mx), 0.0)
        mx_last = jnp.broadcast_to(mx[ROWS - 1:ROWS, :], (ROWS, LANES))
        m_sc[...] = jnp.where(head_col, jnp.broadcast_to((zb + mx)[ROWS - 1:ROWS, :], (ROWS, LANES)), 0.0)
        mx_b = _dot_nt(_split_terms(mx), selh_ref[...])
        winter_b = jnp.exp(_dot_nt(_split_terms(gm), selp_ref[...]))
        emt_b = jnp.exp(_dot_nt(_split_terms(em), selp_ref[...]))
        ws_b = jnp.exp(_dot_nt(_split_terms(r_mat - mx_last), selp_ref[...]))
        r_t = r_mat.T
        maskadd = maskadd_ref[...]

        for p in range(PAIRS):
            ps = slice(p * LANES, (p + 1) * LANES)
            q = proj_sc[rows, OFF_MQ + p * LANES:OFF_MQ + (p + 1) * LANES]
            k = proj_sc[rows, OFF_MK + p * LANES:OFF_MK + (p + 1) * LANES] * QK_SCALE
            v = proj_sc[rows, OFF_MV + p * LANES:OFF_MV + (p + 1) * LANES]
            q16, k16, v16 = q.astype(bf16), k.astype(bf16), v.astype(bf16)
            cn = cn_sc[p]
            acc = jnp.concatenate([winter_b[:, ps]] * 2, axis=1) * _dot(q16, cn.astype(bf16))
            for side in range(2):
                h = 2 * p + side
                qk = _dot_nt(half(q, side), k16)
                w = jnp.exp((r_t[h:h + 1, :] + maskadd) - mx_b[:, h * ROWS:(h + 1) * ROWS]) * qk
                vn = jnp.concatenate([half(v, side), ones_l if side == 0 else ones_r], axis=1)
                acc = acc + _dot(w.astype(bf16), vn)
            hh = acc[:, :LANES] / jnp.maximum(jnp.abs(acc[:, LANES:]), emt_b[:, ps])
            kw16 = (k * ws_b[:, ps]).astype(bf16)
            kwt16 = _dot_nt(eye16, kw16).astype(bf16)
            dcn = _dot(kwt16, jnp.concatenate([v16, ones16], axis=1))
            decay = winter_b[ROWS - 1:ROWS, ps]
            cn_sc[p] = (jnp.concatenate([decay, decay], axis=1) * cn
                        + jnp.where(blockdiag2, dcn, 0.0))
            hm = _sigmoid(proj_sc[rows, OFF_MO + p * LANES:OFF_MO + (p + 1) * LANES]) * hh
            out = _pair_norm(hm, jj, mgain_ref[:, ps]) * _silu(
                proj_sc[rows, OFF_MZ + p * LANES:OFF_MZ + (p + 1) * LANES])
            mix_sc[rows, p * LANES:(p + 1) * LANES] = out.astype(bf16)

        cos_t = cos_ref[rows, :]
        sin_t = sin_ref[rows, :]
        for p in range(PAIRS):
            ps = slice(p * LANES, (p + 1) * LANES)
            q = _rope(proj_sc[rows, OFF_RQ + p * LANES:OFF_RQ + (p + 1) * LANES], cos_t, sin_t, first_half)
            k = _rope(proj_sc[rows, OFF_RK + p * LANES:OFF_RK + (p + 1) * LANES], cos_t, sin_t,
                      first_half) * QK_SCALE
            v = proj_sc[rows, OFF_RV + p * LANES:OFF_RV + (p + 1) * LANES]
            q16, k16, v16 = q.astype(bf16), k.astype(bf16), v.astype(bf16)
            sb = sb_sc[p]
            acc = rslab_ref[0, p] * _dot(q16, sb.astype(bf16))
            for side in range(2):
                h = 2 * p + side
                scores = _dot_nt(half(q, side), k16) * dmat_ref[h]
                acc = acc + _dot(scores.astype(bf16), half(v, side))
            kt16 = (k * rslab_ref[1, p]).astype(bf16)
            ktt16 = _dot_nt(eye16, kt16).astype(bf16)
            sb_sc[p] = rslab_ref[2, p] * sb + jnp.where(blockdiag, _dot(ktt16, v16), 0.0)
            out = _pair_norm(acc, jj, rgain_ref[:, ps]) * _silu(
                proj_sc[rows, OFF_RZ + p * LANES:OFF_RZ + (p + 1) * LANES])
            mix_sc[rows, M_DIM + p * LANES:M_DIM + (p + 1) * LANES] = out.astype(bf16)

        kcur = proj_sc[rows, OFF_AK:OFF_AK + A_KV_DIM]
        vcur = proj_sc[rows, OFF_AV:OFF_AV + A_KV_DIM]
        kprev, vprev = kp_sc[...], vp_sc[...]
        kk16 = jnp.concatenate([kcur, kprev], axis=0).astype(bf16)
        vv = [jnp.concatenate([jnp.concatenate([half(vcur, s), ones_l if s == 0 else ones_r], axis=1),
                               jnp.concatenate([half(vprev, s), ones_l if s == 0 else ones_r], axis=1)],
                              axis=0) for s in range(2)]
        if ci == 0:
            pen = jnp.where(step == 0, -jnp.inf, 0.0).astype(f32)
            lane2 = lax.broadcasted_iota(jnp.int32, (1, 2 * ROWS), 1)
            pen_row = jnp.where(lane2 >= ROWS, pen, 0.0)
        else:
            pen_row = None
        for j in range(KV_GROUP):
            q = proj_sc[rows, OFF_AQ + j * LANES:OFF_AQ + (j + 1) * LANES] * QK_SCALE
            acc = None
            esink = []
            for side in range(2):
                h = ATTN_HEAD_ORDER[2 * j + side]
                s = _dot_nt(half(q, side), kk16) + biascp_ref[h]
                if pen_row is not None:
                    s = s + pen_row
                sink = sinks_ref[h]
                m = jnp.maximum(jnp.max(jnp.maximum(s[:, :ROWS], s[:, ROWS:]), axis=1, keepdims=True), sink)
                part = _dot(jnp.exp(s - m).astype(bf16), vv[side])
                acc = part if acc is None else acc + part
                esink.append(jnp.exp(sink - m))
            den = acc[:, LANES:] + jnp.where(left, esink[0], esink[1])
            out = (acc[:, :LANES] / den) * _silu(proj_sc[rows, OFF_AZ + j * LANES:OFF_AZ + (j + 1) * LANES])
            mix_sc[rows, M_DIM + R_DIM + j * LANES:M_DIM + R_DIM + (j + 1) * LANES] = out.astype(bf16)
        kp_sc[...] = kcur
        vp_sc[...] = vcur

    for ci in range(chunks):
        chunk_body(ci)

    _out_project(x_ref, mix_sc, wout_ref, fgain_ref, y_ref, final)

    @pl.when(step == last_step)
    def _():
        for p in range(PAIRS):
            cn = cn_sc[p]
            sb = sb_sc[p]
            n_t = cn[:, LANES:].T
            for side in range(2):
                h = 2 * p + side
                blk = slice(side * HEAD_DIM, (side + 1) * HEAD_DIM)
                c_out[0, h] = cn[blk, blk]
                s_out[0, h] = sb[blk, blk]
                n_out[0, h:h + 1, :] = n_t[side * HEAD_DIM:side * HEAD_DIM + 1, blk]
        m_out[0] = m_sc[...]
        k_out[0] = kp_sc[...]
        v_out[0] = vp_sc[...]


def _staged_prompt_kernel(x_ref, ngain_ref, win_ref, gbias_ref, mgain_ref, rgain_ref, sinks_ref, wout_ref,
                          fgain_ref, tril3_ref, maskadd_ref, dmat2_ref, rslab_ref, biasall_ref, cos_ref,
                          sin_ref, selh_ref, selp_ref, jj_ref, eye_ref,
                          y_ref, c_out, n_out, m_out, s_out, k_out, v_out,
                          proj_sc, mix_sc, cn_sc, sb_sc, m_sc, kp_sc, vp_sc, *, chunks, final):
    step = pl.program_id(1)
    last_step = pl.num_programs(1) - 1

    xf = x_ref[...]
    u16 = (xf * lax.rsqrt(jnp.mean(xf * xf, axis=1, keepdims=True) + NORM_EPS) * ngain_ref[...]).astype(bf16)
    col_blocks = [(c0, min(c0 + PROJ_COL_BLOCK, P_COLS)) for c0 in range(0, P_COLS, PROJ_COL_BLOCK)]
    half_rows = (chunks // 2) * ROWS if chunks > 1 else chunks * ROWS

    def project(r0, r1, c0, c1):
        proj_sc[r0:r1, c0:c1] = _dot_nt(u16[r0:r1], win_ref[c0:c1, :])

    for c0, c1 in col_blocks:
        project(0, half_rows, c0, c1)
    late_pieces = [(half_rows, chunks * ROWS, c0, c1) for c0, c1 in col_blocks] if half_rows < chunks * ROWS else []

    @pl.when(step == 0)
    def _():
        cn_sc[...] = jnp.zeros_like(cn_sc)
        sb_sc[...] = jnp.zeros_like(sb_sc)
        m_sc[...] = jnp.zeros_like(m_sc)
        kp_sc[...] = jnp.zeros_like(kp_sc)
        vp_sc[...] = jnp.zeros_like(vp_sc)

    lane = lax.broadcasted_iota(jnp.int32, (ROWS, LANES), 1)
    row = lax.broadcasted_iota(jnp.int32, (ROWS, LANES), 0)
    left = lane < HEAD_DIM
    first_half = (lane & (HEAD_DIM - 1)) < (HEAD_DIM // 2)
    head_col = lane < M_HEADS
    blockdiag = (row < HEAD_DIM) == left
    row2 = lax.broadcasted_iota(jnp.int32, (ROWS, 2 * LANES), 0)
    lane2w = lax.broadcasted_iota(jnp.int32, (ROWS, 2 * LANES), 1)
    left2 = (lane2w & (LANES - 1)) < HEAD_DIM
    blockdiag2 = (row2 < HEAD_DIM) == left2
    ones16 = jnp.ones((ROWS, LANES), bf16)

    def halves(x):
        return jnp.concatenate([jnp.where(left, x, 0.0), jnp.where(left, 0.0, x)], axis=0).astype(bf16)

    def pick(x, mask):
        return jnp.where(mask, x[:ROWS], x[ROWS:])

    def pair_blockdiag(blocks):
        z = jnp.zeros_like(blocks[0])
        return jnp.concatenate(
            [jnp.concatenate([blk if j == i else z for j in range(len(blocks))], axis=1)
             for i, blk in enumerate(blocks)], axis=0)

    def half_mean(x):
        lane_n = lax.broadcasted_iota(jnp.int32, x.shape, 1)
        is_left = lane_n < HEAD_DIM
        s_left = jnp.sum(jnp.where(is_left, x, 0.0), axis=1, keepdims=True)
        s_right = jnp.sum(jnp.where(is_left, 0.0, x), axis=1, keepdims=True)
        return jnp.where(is_left, s_left, s_right) * (1.0 / HEAD_DIM)

    def chunk_body(ci):
        rows = slice(ci * ROWS, (ci + 1) * ROWS)

        def proj(off, width=LANES):
            return proj_sc[rows, off:off + width]

        eye16 = eye_ref[...]

        gates = proj(OFF_G) + gbias_ref[...]
        bcum = _exact_tril_dot(tril3_ref[...], _log_sigmoid(gates))

        cos_t, sin_t = cos_ref[rows, :], sin_ref[rows, :]
        r_q = [_rope(proj(OFF_RQ + p * LANES), cos_t, sin_t, first_half) for p in range(PAIRS)]
        r_k = [_rope(proj(OFF_RK + p * LANES), cos_t, sin_t, first_half) * QK_SCALE for p in range(PAIRS)]
        r_vbd = pair_blockdiag([proj(OFF_RV + p * LANES).astype(bf16) for p in range(PAIRS)])
        r_sb = [sb_sc[p] for p in range(PAIRS)]
        r_sc = _dot_nt(jnp.concatenate([halves(q) for q in r_q], axis=1),
                       pair_blockdiag([k.astype(bf16) for k in r_k]))
        r_inter = _dot(jnp.concatenate([q.astype(bf16) for q in r_q], axis=1),
                       pair_blockdiag([sb.astype(bf16) for sb in r_sb]))
        r_upd = _dot(jnp.concatenate([(r_k[p] * rslab_ref[1, p]).T.astype(bf16) for p in range(PAIRS)], axis=1),
                     r_vbd)
        for p in range(PAIRS):
            sb_sc[p] = rslab_ref[2, p] * r_sb[p] + jnp.where(blockdiag, r_upd[:, p * LANES:(p + 1) * LANES], 0.0)
        yield

        kcur, vcur = proj(OFF_AK), proj(OFF_AV)
        kprev, vprev = kp_sc[...], vp_sc[...]
        kk16 = jnp.concatenate([kcur, kprev], axis=0).astype(bf16)
        vv16 = jnp.concatenate([jnp.concatenate([vcur.astype(bf16), ones16], axis=1),
                                jnp.concatenate([vprev.astype(bf16), ones16], axis=1)], axis=0)
        kp_sc[...] = kcur
        vp_sc[...] = vcur
        a_q = jnp.concatenate([halves(proj(OFF_AQ + j * LANES) * QK_SCALE) for j in range(KV_GROUP)], axis=0)
        a_s = _dot_nt(a_q, kk16) + biasall_ref[...]
        if ci == 0:
            pen = jnp.where(step == 0, -jnp.inf, 0.0).astype(f32)
            a_s = a_s + jnp.where(lax.broadcasted_iota(jnp.int32, (1, 2 * ROWS), 1) >= ROWS, pen, 0.0)

        m_q = [proj(OFF_MQ + p * LANES) for p in range(PAIRS)]
        m_k = [proj(OFF_MK + p * LANES) * QK_SCALE for p in range(PAIRS)]
        m_qk = _dot_nt(jnp.concatenate([halves(q) for q in m_q], axis=1),
                       pair_blockdiag([k.astype(bf16) for k in m_k]))
        m_q16 = [q.astype(bf16) for q in m_q]
        m_v16 = [proj(OFF_MV + p * LANES).astype(bf16) for p in range(PAIRS)]
        yield

        zb = pltpu.roll(bcum, LANES - M_HEADS, axis=1)
        r_mat = jnp.where(head_col, gates - zb, 0.0)
        cm = r_mat
        sh = 1
        while sh < ROWS:
            cm = jnp.where(row >= sh, jnp.maximum(cm, pltpu.roll(cm, sh, axis=0)), cm)
            sh *= 2
        mprev = m_sc[...]
        mx = jnp.maximum(mprev, cm)
        gm = mprev - mx
        em = jnp.where(head_col, -(zb + mx), 0.0)
        mx_last = jnp.broadcast_to(mx[ROWS - 1:ROWS, :], (ROWS, LANES))
        m_sc[...] = jnp.where(head_col, jnp.broadcast_to((zb + mx)[ROWS - 1:ROWS, :], (ROWS, LANES)), 0.0)
        mx_b = _dot_nt(_split_terms(mx), selh_ref[...])
        slabs = jnp.exp(_dot_nt(_split_terms(jnp.concatenate([gm, em, r_mat - mx_last], axis=0)),
                                selp_ref[...]))
        winter_b, emt_b, ws_b = slabs[:ROWS], slabs[ROWS:2 * ROWS], slabs[2 * ROWS:]
        r_t = r_mat.T
        yield

        outs = []
        r_acc = []
        r_o = _dot((r_sc * dmat2_ref[...]).astype(bf16), r_vbd)
        for p in range(PAIRS):
            ps = slice(p * LANES, (p + 1) * LANES)
            r_acc.append(pick(r_o[:, ps], left) + rslab_ref[0, p] * r_inter[:, ps])

        a_out = []
        a_p = []
        for blk in range(A_HEADS):
            s = a_s[blk * ROWS:(blk + 1) * ROWS]
            sink = sinks_ref[ATTN_HEAD_ORDER[blk]]
            m = jnp.maximum(jnp.max(jnp.maximum(s[:, :ROWS], s[:, ROWS:]), axis=1, keepdims=True), sink)
            a_p.append(jnp.exp(s - m).astype(bf16))
            a_out.append(jnp.exp(sink - m))
        a_pv = _dot(jnp.concatenate(a_p, axis=0), vv16)
        yield

        maskadd = maskadd_ref[...]
        for p in range(PAIRS):
            ps = slice(p * LANES, (p + 1) * LANES)
            cn = cn_sc[p]
            w = jnp.concatenate(
                [jnp.exp((r_t[2 * p + side:2 * p + side + 1, :] + maskadd)
                         - mx_b[:, (2 * p + side) * ROWS:(2 * p + side + 1) * ROWS]) for side in range(2)],
                axis=0) * m_qk[:, ps]
            acc = (pick(_dot(w.astype(bf16), jnp.concatenate([m_v16[p], ones16], axis=1)), left2)
                   + jnp.concatenate([winter_b[:, ps]] * 2, axis=1) * _dot(m_q16[p], cn.astype(bf16)))
            hh = acc[:, :LANES] / jnp.maximum(jnp.abs(acc[:, LANES:]), emt_b[:, ps])
            outs.append(_sigmoid(proj(OFF_MO + p * LANES)) * hh)
            kwt16 = (m_k[p] * ws_b[:, ps]).T.astype(bf16)
            dcn = _dot(kwt16, jnp.concatenate([m_v16[p], ones16], axis=1))
            decay = winter_b[ROWS - 1:ROWS, ps]
            cn_sc[p] = jnp.concatenate([decay, decay], axis=1) * cn + jnp.where(blockdiag2, dcn, 0.0)
        outs.extend(r_acc)
        yield

        x4 = jnp.concatenate(outs, axis=0)
        xc = x4 - half_mean(x4)
        y4 = xc * lax.rsqrt(half_mean(xc * xc) + NORM_EPS)
        for i in range(2 * PAIRS):
            gain = (mgain_ref if i < PAIRS else rgain_ref)[:, (i % PAIRS) * LANES:(i % PAIRS + 1) * LANES]
            zoff = (OFF_MZ if i < PAIRS else OFF_RZ) + (i % PAIRS) * LANES
            out = y4[i * ROWS:(i + 1) * ROWS] * gain * _silu(proj(zoff))
            mix_sc[rows, i * LANES:(i + 1) * LANES] = out.astype(bf16)
        for j in range(KV_GROUP):
            acc = pick(a_pv[2 * j * ROWS:(2 * j + 2) * ROWS], left2)
            den = acc[:, LANES:] + jnp.where(left, a_out[2 * j], a_out[2 * j + 1])
            out = (acc[:, :LANES] / den) * _silu(proj(OFF_AZ + j * LANES))
            mix_sc[rows, M_DIM + R_DIM + j * LANES:M_DIM + R_DIM + (j + 1) * LANES] = out.astype(bf16)

    def out_project(r0, r1, c0, c1):
        y_ref[r0:r1, c0:c1] = x_ref[r0:r1, c0:c1] + _dot(mix_sc[r0:r1, :], wout_ref[:, c0:c1])

    out_blocks = [(c0, min(c0 + PROJ_COL_BLOCK, D_MODEL)) for c0 in range(0, D_MODEL, PROJ_COL_BLOCK)]
    early_out = [(0, half_rows, c0, c1) for c0, c1 in out_blocks] if half_rows < chunks * ROWS else []
    final_out = [(half_rows if early_out else 0, chunks * ROWS, c0, c1) for c0, c1 in out_blocks]

    def fill_mxu(ci):
        if late_pieces:
            project(*late_pieces.pop(0))
        elif early_out and (ci - 1) * ROWS >= half_rows:
            out_project(*early_out.pop(0))

    parts = [chunk_body(ci) for ci in range(chunks)]
    for ci in range(chunks + 1):
        if ci * ROWS >= half_rows:
            while late_pieces:
                project(*late_pieces.pop(0))
        for _ in range(3):
            if ci < chunks:
                next(parts[ci])
                fill_mxu(ci)
            if ci > 0:
                next(parts[ci - 1], None)
                fill_mxu(ci)
    for piece in early_out + final_out:
        out_project(*piece)
    if final:
        y = y_ref[...]
        y_ref[...] = y * lax.rsqrt(jnp.mean(y * y, axis=1, keepdims=True) + NORM_EPS) * fgain_ref[...]

    @pl.when(step == last_step)
    def _():
        for p in range(PAIRS):
            cn = cn_sc[p]
            sb = sb_sc[p]
            n_t = cn[:, LANES:].T
            for side in range(2):
                h = 2 * p + side
                blk = slice(side * HEAD_DIM, (side + 1) * HEAD_DIM)
                c_out[0, h] = cn[blk, blk]
                s_out[0, h] = sb[blk, blk]
                n_out[0, h:h + 1, :] = n_t[side * HEAD_DIM:side * HEAD_DIM + 1, blk]
        m_out[0] = m_sc[...]
        k_out[0] = kp_sc[...]
        v_out[0] = vp_sc[...]


def _head_norm(x, gain):
    mu = jnp.mean(x, axis=1, keepdims=True)
    xc = x - mu
    var = jnp.mean(xc * xc, axis=1, keepdims=True)
    return xc * lax.rsqrt(var + NORM_EPS) * gain


def _group_last(x, groups):
    n = x.shape[1]
    glen = ROWS // groups
    x3 = x.reshape(groups, glen, n)
    return jnp.broadcast_to(x3[:, glen - 1:glen, :], (groups, glen, n)).reshape(ROWS, n)


def _state_rows(col, groups):
    glen = ROWS // groups
    wide = jnp.broadcast_to(col, (ROWS, HEAD_DIM)).reshape(groups, glen, HEAD_DIM)
    per_group = wide[:, 0:1, :]
    rows = jnp.broadcast_to(per_group, (groups, HEAD_DIM, HEAD_DIM)).reshape(groups * HEAD_DIM, HEAD_DIM)
    return rows, per_group.reshape(groups, HEAD_DIM)


def _cols_to_mat(cols, lane):
    acc = jnp.zeros((ROWS, LANES), f32)
    for h, c in enumerate(cols):
        acc = jnp.where(lane == h, c, acc)
    return acc


def _sample_kernel(x_ref, ngain_ref, win_ref, gbias_ref, mgain_ref, rgain_ref, sinks_ref, wout_ref,
                   fgain_ref, tril3_ref, maskadd_ref, dmat_ref, rc_ref, biasc_ref, biasp_ref, cos_ref, sin_ref,
                   c_in, n_in, m_in, s_in, k_in, v_in,
                   y_ref, c_out, n_out, m_out, s_out, k_out, v_out,
                   proj_sc, mix_sc, xcur_sc, qb_sc, sp_sc, pp_sc, ob_sc, *, groups, ret_full):
    glen = ROWS // groups
    glen_log2 = glen.bit_length() - 1
    hd_log2 = HEAD_DIM.bit_length() - 1
    step = pl.program_id(0)
    layer = pl.program_id(1)
    last_layer = pl.num_programs(1) - 1

    @pl.when(layer == 0)
    def _():
        xcur_sc[...] = x_ref[...]

    _rms_project(xcur_sc, ngain_ref, win_ref, proj_sc)

    @pl.when(jnp.logical_and(step == 0, layer == 0))
    def _():
        qb_sc[...] = jnp.zeros_like(qb_sc)

    lane = lax.broadcasted_iota(jnp.int32, (ROWS, LANES), 1)
    first_half = (lane & (HEAD_DIM - 1)) < (HEAD_DIM // 2)
    rows = slice(0, ROWS)

    r_i = lax.broadcasted_iota(jnp.int32, (ROWS, groups * HEAD_DIM), 0)
    c_i = lax.broadcasted_iota(jnp.int32, (ROWS, groups * HEAD_DIM), 1)
    blk = (r_i >> glen_log2) == (c_i >> hd_log2)
    r_t = lax.broadcasted_iota(jnp.int32, (groups * HEAD_DIM, ROWS), 0)
    c_t = lax.broadcasted_iota(jnp.int32, (groups * HEAD_DIM, ROWS), 1)
    blk_t = (r_t >> hd_log2) == (c_t >> glen_log2)

    def q_times_state(qh, st):
        qt = jnp.where(blk, jnp.concatenate([qh] * groups, axis=1), 0.0)
        return _dot(qt.astype(bf16), st.astype(bf16))

    def state_increment(kt_h, vh16):
        kt = jnp.where(blk_t, jnp.concatenate([kt_h] * groups, axis=0), 0.0)
        return _dot(kt.astype(bf16), vh16)

    gates = proj_sc[rows, OFF_G:OFF_G + GATE_PAD] + gbias_ref[...]
    bcum = _exact_tril_dot(tril3_ref[...], _log_sigmoid(gates))
    z = jnp.where(lane < M_HEADS, gates, bcum)
    zt = z.T
    zb = pltpu.roll(z, LANES - M_HEADS, axis=1)
    maskadd = maskadd_ref[...]

    dlogs, mintra = [], []
    for h in range(M_HEADS):
        bcol = zb[:, h:h + 1]
        dlog = (bcol - zt[M_HEADS + h:M_HEADS + h + 1, :]) + zt[h:h + 1, :] + maskadd
        dlogs.append(dlog)
        mintra.append(jnp.max(dlog, axis=1, keepdims=True))
    mprev = m_in[...]
    g_all = zb + mprev
    mt_all = jnp.maximum(g_all, _cols_to_mat(mintra, lane))
    winter_all = jnp.exp(g_all - mt_all)
    emt_all = jnp.exp(-mt_all)
    mlast_all = _group_last(mt_all, groups)
    blast_all = _group_last(zb, groups)
    decay_all = jnp.exp(blast_all + mprev - mlast_all)
    ws_all = jnp.exp(blast_all - zb + z - mlast_all)

    mq = proj_sc[rows, OFF_MQ:OFF_MQ + M_DIM]
    mk = proj_sc[rows, OFF_MK:OFF_MK + M_DIM] * QK_SCALE
    mv = proj_sc[rows, OFF_MV:OFF_MV + M_DIM]
    kw_parts = []
    for h in range(M_HEADS):
        hs = slice(h * HEAD_DIM, (h + 1) * HEAD_DIM)
        kw_parts.append(mk[:, hs] * ws_all[:, h:h + 1])
    kwt = jnp.concatenate(kw_parts, axis=1).T

    hm = []
    for h in range(M_HEADS):
        hs = slice(h * HEAD_DIM, (h + 1) * HEAD_DIM)
        qh, kh, vh = mq[:, hs], mk[:, hs], mv[:, hs]
        vh16 = vh.astype(bf16)
        mt = mt_all[:, h:h + 1]
        winter = winter_all[:, h:h + 1]
        wintra = jnp.exp(dlogs[h] - mt) * _dot_nt(qh.astype(bf16), kh.astype(bf16))
        c_h = c_in[:, h].reshape(groups * HEAD_DIM, HEAD_DIM)
        num = winter * q_times_state(qh, c_h) + _dot(wintra.astype(bf16), vh16)
        n_g = n_in[h]
        n_rows = jnp.broadcast_to(n_g.reshape(groups, 1, HEAD_DIM),
                                  (groups, glen, HEAD_DIM)).reshape(ROWS, HEAD_DIM)
        nq = (winter * jnp.sum(qh * n_rows, axis=1, keepdims=True)
              + jnp.sum(wintra, axis=1, keepdims=True))
        hm.append(num / jnp.maximum(jnp.abs(nq), emt_all[:, h:h + 1]))
        dec_rows, dec_g = _state_rows(decay_all[:, h:h + 1], groups)
        c_new = dec_rows * c_h + state_increment(kwt[hs, :], vh16)
        c_out[:, h] = c_new.reshape(groups, HEAD_DIM, HEAD_DIM)
        n_out[h] = dec_g * n_g + jnp.sum(kw_parts[h].reshape(groups, glen, HEAD_DIM), axis=1)
    m_out[...] = mlast_all
    hm = jnp.concatenate(hm, axis=1)
    hm = _sigmoid(proj_sc[rows, OFF_MO:OFF_MO + M_DIM]) * hm
    mgain = mgain_ref[...]
    out_m = jnp.concatenate(
        [_head_norm(hm[:, h * HEAD_DIM:(h + 1) * HEAD_DIM], mgain[:, h * HEAD_DIM:(h + 1) * HEAD_DIM])
         for h in range(M_HEADS)], axis=1)
    out_m = out_m * _silu(proj_sc[rows, OFF_MZ:OFF_MZ + M_DIM])
    mix_sc[rows, 0:M_DIM] = out_m.astype(bf16)

    cos_t, sin_t = cos_ref[...], sin_ref[...]
    rq = jnp.concatenate([_rope(proj_sc[rows, OFF_RQ + p * LANES:OFF_RQ + (p + 1) * LANES], cos_t, sin_t,
                                first_half) for p in range(PAIRS)], axis=1)
    rk = jnp.concatenate([_rope(proj_sc[rows, OFF_RK + p * LANES:OFF_RK + (p + 1) * LANES], cos_t, sin_t,
                                first_half) for p in range(PAIRS)], axis=1) * QK_SCALE
    rv = proj_sc[rows, OFF_RV:OFF_RV + R_DIM]
    rc = rc_ref[...]
    rkt = jnp.concatenate(
        [rk[:, h * HEAD_DIM:(h + 1) * HEAD_DIM] * rc[:, R_HEADS + h:R_HEADS + h + 1]
         for h in range(R_HEADS)], axis=1).T
    hr = []
    for h in range(R_HEADS):
        hs = slice(h * HEAD_DIM, (h + 1) * HEAD_DIM)
        qh, kh = rq[:, hs], rk[:, hs]
        vh16 = rv[:, hs].astype(bf16)
        scores = _dot_nt(qh.astype(bf16), kh.astype(bf16)) * dmat_ref[h]
        s_h = s_in[:, h].reshape(groups * HEAD_DIM, HEAD_DIM)
        hr.append(_dot(scores.astype(bf16), vh16) + rc[:, h:h + 1] * q_times_state(qh, s_h))
        s_new = ret_full[h] * s_h + state_increment(rkt[hs, :], vh16)
        s_out[:, h] = s_new.reshape(groups, HEAD_DIM, HEAD_DIM)
    rgain = rgain_ref[...]
    out_r = jnp.concatenate(
        [_head_norm(hr[h], rgain[:, h * HEAD_DIM:(h + 1) * HEAD_DIM]) for h in range(R_HEADS)], axis=1)
    out_r = out_r * _silu(proj_sc[rows, OFF_RZ:OFF_RZ + R_DIM])
    mix_sc[rows, M_DIM:M_DIM + R_DIM] = out_r.astype(bf16)

    kcur = proj_sc[rows, OFF_AK:OFF_AK + A_KV_DIM]
    vcur = proj_sc[rows, OFF_AV:OFF_AV + A_KV_DIM]
    kcur16, vcur16 = kcur.astype(bf16), vcur.astype(bf16)

    def q_cols(h):
        c0 = OFF_AQ + ATTN_HEAD_POS[h] * HEAD_DIM
        return slice(c0, c0 + HEAD_DIM)

    for h in range(A_HEADS):
        off = (h // KV_GROUP) * HEAD_DIM
        qb_sc[:, h * glen:(h + 1) * glen, off:off + HEAD_DIM] = (
            proj_sc[rows, q_cols(h)].reshape(groups, glen, HEAD_DIM).astype(bf16))

    for b in range(groups):
        sp = _dot(qb_sc[b], k_in[b].astype(bf16))
        sp_sc[:, b * glen:(b + 1) * glen, :] = sp.reshape(A_HEADS, glen, WINDOW)

    dens, o_cur = [], []
    for h in range(A_HEADS):
        kv = h // KV_GROUP
        ks = slice(kv * HEAD_DIM, (kv + 1) * HEAD_DIM)
        qh16 = proj_sc[rows, q_cols(h)].astype(bf16)
        sc = _dot_nt(qh16, kcur16[:, ks]) * QK_SCALE + biasc_ref[h]
        sp = sp_sc[h] * QK_SCALE + biasp_ref[h]
        sink = sinks_ref[layer, h]
        m = jnp.maximum(jnp.maximum(jnp.max(sc, axis=1, keepdims=True),
                                    jnp.max(sp, axis=1, keepdims=True)), sink)
        pc = jnp.exp(sc - m)
        pp = jnp.exp(sp - m)
        dens.append(jnp.sum(pc, axis=1, keepdims=True) + jnp.sum(pp, axis=1, keepdims=True)
                    + jnp.exp(sink - m))
        o_cur.append(_dot(pc.astype(bf16), vcur16[:, ks]))
        pp_sc[:, h * glen:(h + 1) * glen, :] = pp.reshape(groups, glen, WINDOW).astype(bf16)

    for b in range(groups):
        ob = _dot_nt(pp_sc[b], v_in[b].astype(bf16))
        ob_sc[:, b * glen:(b + 1) * glen, :] = ob.reshape(A_HEADS, glen, A_KV_DIM)
    outs = []
    for h in ATTN_HEAD_ORDER:
        off = (h // KV_GROUP) * HEAD_DIM
        outs.append((o_cur[h] + ob_sc[h][:, off:off + HEAD_DIM]) / dens[h])
    kcur_t, vcur_t = kcur.T, vcur.T
    fresh = lane >= WINDOW - glen
    for b in range(groups):
        shift = (WINDOW - glen - b * glen) % LANES
        k_out[b] = jnp.where(fresh, pltpu.roll(kcur_t, shift, axis=1), pltpu.roll(k_in[b], WINDOW - glen, axis=1))
        v_out[b] = jnp.where(fresh, pltpu.roll(vcur_t, shift, axis=1), pltpu.roll(v_in[b], WINDOW - glen, axis=1))
    out_a = jnp.concatenate(outs, axis=1) * _silu(proj_sc[rows, OFF_AZ:OFF_AZ + A_DIM])
    mix_sc[rows, M_DIM + R_DIM:M_DIM + R_DIM + A_DIM] = out_a.astype(bf16)

    y = xcur_sc[...] + _dot(mix_sc[...], wout_ref[...])
    xcur_sc[...] = y

    @pl.when(layer == last_layer)
    def _():
        y_ref[...] = y * lax.rsqrt(jnp.mean(y * y, axis=1, keepdims=True) + NORM_EPS) * fgain_ref[...]


def _t5_bucket(dist):
    max_exact = N_BUCKETS // 2
    d = np.maximum(dist, 1).astype(np.float32)
    large = max_exact + (np.log(d / max_exact) / np.log(REL_MAX_DIST / max_exact)
                         * (N_BUCKETS - max_exact)).astype(np.int32)
    large = np.minimum(large, N_BUCKETS - 1)
    return np.where(dist < max_exact, dist, large).astype(np.int32)


def _static_tables(groups):
    glen = ROWS // groups
    r = np.arange(ROWS)
    grp, tau = r // glen, r % glen
    causal = (grp[:, None] == grp[None, :]) & (tau[None, :] <= tau[:, None])
    tril = causal.astype(np.float32)
    maskadd = np.where(causal, 0.0, -np.inf).astype(np.float32)
    log_g = np.log1p(-np.exp2(-5.0 - np.arange(R_HEADS, dtype=np.float64)))
    diff = (tau[:, None] - tau[None, :]).astype(np.float64)
    dmat = np.where(causal[None], np.exp(log_g[:, None, None] * np.maximum(diff, 0.0)[None]), 0.0)
    inter = np.exp(log_g[None, :] * (tau[:, None] + 1.0))
    tail = np.exp(log_g[None, :] * (glen - 1.0 - tau[:, None]))
    full = np.exp(log_g * glen)
    rc = np.zeros((ROWS, LANES), np.float64)
    rc[:, 0:R_HEADS] = inter
    rc[:, R_HEADS:2 * R_HEADS] = tail
    lane_head = np.arange(LANES) // HEAD_DIM
    rslab = np.zeros((3, PAIRS, ROWS, LANES), np.float64)
    for p in range(PAIRS):
        rslab[0, p] = inter[:, 2 * p + lane_head]
        rslab[1, p] = tail[:, 2 * p + lane_head]
        rslab[2, p] = full[2 * p + lane_head][None, :]
    selh = np.zeros((M_HEADS * ROWS, SPLIT_TERMS * LANES), np.float32)
    selp = np.zeros((PAIRS * LANES, SPLIT_TERMS * LANES), np.float32)
    for t in range(SPLIT_TERMS):
        for h in range(M_HEADS):
            selh[h * ROWS:(h + 1) * ROWS, t * LANES + h] = 1.0
        for p in range(PAIRS):
            for side in range(2):
                selp[p * LANES + side * HEAD_DIM:p * LANES + (side + 1) * HEAD_DIM, t * LANES + 2 * p + side] = 1.0
    jj = np.zeros((2 * LANES, LANES), np.float32)
    for t in range(2):
        for side in range(2):
            jj[t * LANES + side * HEAD_DIM:t * LANES + (side + 1) * HEAD_DIM,
               side * HEAD_DIM:(side + 1) * HEAD_DIM] = 1.0 / HEAD_DIM
    return dict(tril3=jnp.asarray(np.concatenate([tril] * SPLIT_TERMS, axis=1), bf16),
                maskadd=maskadd, dmat=dmat.astype(np.float32),
                dmat2=np.concatenate(list(dmat.astype(np.float32).reshape(PAIRS, 2 * ROWS, ROWS)), axis=1),
                rc=rc.astype(np.float32), rslab=rslab.astype(np.float32),
                full=tuple(float(v) for v in full), causal=causal, tau=tau,
                selh=jnp.asarray(selh, bf16), selp=jnp.asarray(selp, bf16), jj=jnp.asarray(jj, bf16),
                eye=jnp.asarray(np.eye(ROWS, dtype=np.float32), bf16))


def _bias_tables(rel_table, tabs_s):
    tb = jnp.transpose(rel_table[_t5_bucket(np.arange(WINDOW))]).astype(f32)
    ninf = jnp.full((A_HEADS, WINDOW + 1), -jnp.inf, f32)
    rev = tb[:, :0:-1]

    def skew(u):
        t = jnp.tile(u, (1, WINDOW))[:, :WINDOW * 2 * WINDOW]
        return t.reshape(A_HEADS, WINDOW, 2 * WINDOW)[:, :, :WINDOW]

    bias_cur = skew(jnp.concatenate([tb[:, :1], ninf, rev], axis=1))
    bias_prev = skew(jnp.concatenate([ninf[:, :1], rev, ninf], axis=1))
    bias_cp = jnp.concatenate([bias_cur, bias_prev], axis=2)
    bias_cp = jnp.concatenate([bias_cp[h] for h in ATTN_HEAD_ORDER], axis=0)
    glen = int(tabs_s["tau"].max()) + 1
    reps = ROWS // glen
    bias_cs = jnp.where(jnp.asarray(tabs_s["causal"])[None],
                        jnp.tile(bias_cur[:, :glen, :glen], (1, reps, reps)), -jnp.inf)
    bias_ps = jnp.tile(bias_prev[:, :glen, :], (1, reps, 1))
    return bias_cp, bias_cs, bias_ps


def _rope_tables(pos):
    half = HEAD_DIM // 2
    inv = ROPE_BASE ** (-jnp.arange(half, dtype=f32) / half)
    ang = pos.astype(f32)[:, None] * inv[None, :]
    cos, sin = jnp.cos(ang), jnp.sin(ang)
    reps = LANES // HEAD_DIM
    cos_t = jnp.tile(jnp.concatenate([cos, cos], axis=1), (1, reps))
    sin_t = jnp.tile(jnp.concatenate([-sin, sin], axis=1), (1, reps))
    return cos_t, sin_t


def _const_spec(shape, nargs):
    zeros = (0,) * len(shape)
    if nargs == 1:
        return pl.BlockSpec(shape, lambda i: zeros)
    return pl.BlockSpec(shape, lambda i, j: zeros)


def _layer_spec(shape, layer, nargs):
    idx = (layer,) + (0,) * len(shape)
    if nargs == 1:
        return pl.BlockSpec((None,) + shape, lambda i: idx)
    return pl.BlockSpec((None,) + shape, lambda i, j: idx)


def _param_specs(layer, nargs):
    ls = functools.partial(_layer_spec, layer=layer, nargs=nargs)
    return [ls((1, D_MODEL)), ls((P_COLS, D_MODEL)), ls((1, LANES)), ls((1, M_DIM)), ls((1, R_DIM)),
            pl.BlockSpec(memory_space=pltpu.SMEM), ls((D_MODEL, D_MODEL)), _const_spec((1, D_MODEL), nargs)]


def _param_args(p, layer):
    return (p["norm_gain"], p["w_in"], p["gbias"], p["m_gain"], p["r_gain"], p["sinks"][layer], p["w_out"],
            p["fgain"])


def _prompt_layer(x, p, layer, tabs, bias_cp, cos_t, sin_t, final):
    B, T, _ = x.shape
    tb = min(PROMPT_ROWS, T)
    chunks = tb // ROWS
    nt = T // tb
    cs = functools.partial(_const_spec, nargs=2)
    in_specs = [pl.BlockSpec((None, tb, D_MODEL), lambda b, t: (b, t, 0))] + _param_specs(layer, 2) + [
        cs((ROWS, SPLIT_TERMS * ROWS)), cs((ROWS, ROWS)), cs((2 * ROWS, PAIRS * ROWS)),
        cs((3, PAIRS, ROWS, LANES)), cs((A_HEADS * ROWS, 2 * ROWS)),
        pl.BlockSpec((tb, LANES), lambda b, t: (t, 0)), pl.BlockSpec((tb, LANES), lambda b, t: (t, 0)),
        cs((M_HEADS * ROWS, SPLIT_TERMS * LANES)), cs((PAIRS * LANES, SPLIT_TERMS * LANES)),
        cs((2 * LANES, LANES)), cs((ROWS, ROWS)),
    ]
    out_shape = (
        jax.ShapeDtypeStruct((B, T, D_MODEL), f32),
        jax.ShapeDtypeStruct((B, M_HEADS, HEAD_DIM, HEAD_DIM), f32),
        jax.ShapeDtypeStruct((B, M_HEADS, HEAD_DIM), f32),
        jax.ShapeDtypeStruct((B, ROWS, LANES), f32),
        jax.ShapeDtypeStruct((B, R_HEADS, HEAD_DIM, HEAD_DIM), f32),
        jax.ShapeDtypeStruct((B, WINDOW, A_KV_DIM), f32),
        jax.ShapeDtypeStruct((B, WINDOW, A_KV_DIM), f32),
    )
    out_specs = (
        pl.BlockSpec((None, tb, D_MODEL), lambda b, t: (b, t, 0)),
        pl.BlockSpec((1, M_HEADS, HEAD_DIM, HEAD_DIM), lambda b, t: (b, 0, 0, 0)),
        pl.BlockSpec((1, M_HEADS, HEAD_DIM), lambda b, t: (b, 0, 0)),
        pl.BlockSpec((1, ROWS, LANES), lambda b, t: (b, 0, 0)),
        pl.BlockSpec((1, R_HEADS, HEAD_DIM, HEAD_DIM), lambda b, t: (b, 0, 0, 0)),
        pl.BlockSpec((1, WINDOW, A_KV_DIM), lambda b, t: (b, 0, 0)),
        pl.BlockSpec((1, WINDOW, A_KV_DIM), lambda b, t: (b, 0, 0)),
    )
    kern = functools.partial(_staged_prompt_kernel, chunks=chunks, final=final)
    y, c, n, m, s, k, v = pl.pallas_call(
        kern, grid=(B, nt), in_specs=in_specs, out_specs=out_specs, out_shape=out_shape,
        scratch_shapes=[pltpu.VMEM((tb, P_COLS), f32), pltpu.VMEM((tb, D_MODEL), bf16),
                        pltpu.VMEM((PAIRS, ROWS, 2 * LANES), f32), pltpu.VMEM((PAIRS, ROWS, LANES), f32),
                        pltpu.VMEM((ROWS, LANES), f32), pltpu.VMEM((ROWS, A_KV_DIM), f32),
                        pltpu.VMEM((ROWS, A_KV_DIM), f32)],
        compiler_params=pltpu.CompilerParams(dimension_semantics=("arbitrary", "arbitrary"),
                                             vmem_limit_bytes=VMEM_LIMIT_BYTES),
        name="prompt_layer",
    )(x, *_param_args(p, layer), tabs["tril3"], tabs["maskadd"], tabs["dmat2"], tabs["rslab"], bias_cp,
      cos_t, sin_t, tabs["selh"], tabs["selp"], tabs["jj"], tabs["eye"])
    k = k.reshape(B, WINDOW, A_KV_HEADS, HEAD_DIM)
    v = v.reshape(B, WINDOW, A_KV_HEADS, HEAD_DIM)
    return y, c, n, m[:, 0, :M_HEADS], s, k, v


def _sample_path(x, states, p, tabs, bias_c, bias_p, cos_t, sin_t):
    B, T, _ = x.shape
    groups = ROWS // T
    nb = B // groups
    c0, n0, m0, s0, k0, v0 = states
    depth = c0.shape[0]
    x2 = x.reshape(B * T, D_MODEL)
    n0t = jnp.transpose(n0, (0, 2, 1, 3))
    m0r = jnp.pad(jnp.repeat(m0, T, axis=1), ((0, 0), (0, 0), (0, LANES - M_HEADS)))
    k0r = jnp.transpose(k0, (0, 1, 3, 4, 2)).reshape(depth, B, A_KV_DIM, WINDOW)
    v0r = jnp.transpose(v0, (0, 1, 3, 4, 2)).reshape(depth, B, A_KV_DIM, WINDOW)

    def cs(shape):
        zeros = (0,) * len(shape)
        return pl.BlockSpec(shape, lambda i, l: zeros)

    def per_layer(shape):
        zeros = (0,) * len(shape)
        return pl.BlockSpec((None,) + shape, lambda i, l: (l,) + zeros)

    st4 = pl.BlockSpec((None, groups, M_HEADS, HEAD_DIM, HEAD_DIM), lambda i, l: (l, i, 0, 0, 0))
    stn = pl.BlockSpec((None, M_HEADS, groups, HEAD_DIM), lambda i, l: (l, 0, i, 0))
    stm = pl.BlockSpec((None, ROWS, LANES), lambda i, l: (l, i, 0))
    stk = pl.BlockSpec((None, groups, A_KV_DIM, WINDOW), lambda i, l: (l, i, 0, 0))
    rows_spec = pl.BlockSpec((ROWS, D_MODEL), lambda i, l: (i, 0))
    in_specs = [
        rows_spec,
        per_layer((1, D_MODEL)), per_layer((P_COLS, D_MODEL)), per_layer((1, LANES)), per_layer((1, M_DIM)),
        per_layer((1, R_DIM)), pl.BlockSpec(memory_space=pltpu.SMEM), per_layer((D_MODEL, D_MODEL)),
        cs((1, D_MODEL)),
        cs((ROWS, SPLIT_TERMS * ROWS)), cs((ROWS, ROWS)), cs((R_HEADS, ROWS, ROWS)), cs((ROWS, LANES)),
        cs((A_HEADS, ROWS, ROWS)), cs((A_HEADS, ROWS, WINDOW)),
        cs((ROWS, LANES)), cs((ROWS, LANES)),
        st4, stn, stm, st4, stk, stk,
    ]
    out_shape = (
        jax.ShapeDtypeStruct((B * T, D_MODEL), f32),
        jax.ShapeDtypeStruct((depth, B, M_HEADS, HEAD_DIM, HEAD_DIM), f32),
        jax.ShapeDtypeStruct((depth, M_HEADS, B, HEAD_DIM), f32),
        jax.ShapeDtypeStruct((depth, B * T, LANES), f32),
        jax.ShapeDtypeStruct((depth, B, R_HEADS, HEAD_DIM, HEAD_DIM), f32),
        jax.ShapeDtypeStruct((depth, B, A_KV_DIM, WINDOW), f32),
        jax.ShapeDtypeStruct((depth, B, A_KV_DIM, WINDOW), f32),
    )
    out_specs = (rows_spec, st4, stn, stm, st4, stk, stk)
    kern = functools.partial(_sample_kernel, groups=groups, ret_full=tabs["full"])
    y, c, n, m, s, k, v = pl.pallas_call(
        kern, grid=(nb, depth), in_specs=in_specs, out_specs=out_specs, out_shape=out_shape,
        scratch_shapes=[pltpu.VMEM((ROWS, P_COLS), f32), pltpu.VMEM((ROWS, D_MODEL), bf16),
                        pltpu.VMEM((ROWS, D_MODEL), f32),
                        pltpu.VMEM((groups, A_HEADS * T, A_KV_DIM), bf16),
                        pltpu.VMEM((A_HEADS, ROWS, WINDOW), f32),
                        pltpu.VMEM((groups, A_HEADS * T, WINDOW), bf16),
                        pltpu.VMEM((A_HEADS, ROWS, A_KV_DIM), f32)],
        compiler_params=pltpu.CompilerParams(dimension_semantics=("arbitrary", "arbitrary"),
                                             vmem_limit_bytes=VMEM_LIMIT_BYTES),
        name="sample_path",
    )(x2, p["norm_gain"], p["w_in"], p["gbias"], p["m_gain"], p["r_gain"], p["sinks"], p["w_out"], p["fgain"],
      tabs["tril3"], tabs["maskadd"], tabs["dmat"], tabs["rc"], bias_c, bias_p,
      cos_t, sin_t, c0, n0t, m0r, s0, k0r, v0r)
    y = y.reshape(B, T, D_MODEL)
    n = jnp.transpose(n, (0, 2, 1, 3))
    m = m.reshape(depth, B, T, LANES)[:, :, 0, :M_HEADS]
    k = jnp.transpose(k.reshape(depth, B, A_KV_HEADS, HEAD_DIM, WINDOW), (0, 1, 4, 2, 3))
    v = jnp.transpose(v.reshape(depth, B, A_KV_HEADS, HEAD_DIM, WINDOW), (0, 1, 4, 2, 3))
    return y, c, n, m, s, k, v


def _prepare_params(norm_gain, w_in, mlstm_gate_bias, mlstm_norm_gain, ret_norm_gain, attn_sinks, w_out,
                    final_norm_gain):
    depth = w_in.shape[0]
    w_t = jnp.swapaxes(w_in, 1, 2)
    split = OFF_G + N_GATES
    aq0 = split + 4 * R_DIM
    akv0 = aq0 + A_DIM
    az0 = akv0 + 2 * A_KV_DIM

    def by_head(w):
        w = w.reshape(depth, A_HEADS, HEAD_DIM, D_MODEL)
        return jnp.concatenate([w[:, h] for h in ATTN_HEAD_ORDER], axis=1)

    w_in_p = jnp.concatenate(
        [w_t[:, :split], jnp.zeros((depth, GATE_PAD - N_GATES, D_MODEL), w_t.dtype), w_t[:, split:aq0],
         by_head(w_t[:, aq0:akv0]), w_t[:, akv0:az0], by_head(w_t[:, az0:])], axis=1).astype(bf16)
    wo16 = w_out.astype(bf16)
    a0 = M_DIM + R_DIM
    w_out_p = jnp.concatenate([wo16[:, :a0, :], by_head(wo16[:, a0:, :])], axis=1)
    gbias = jnp.pad(mlstm_gate_bias.reshape(depth, 1, N_GATES), ((0, 0), (0, 0), (0, LANES - N_GATES)))
    return dict(norm_gain=norm_gain.reshape(depth, 1, D_MODEL), w_in=w_in_p, gbias=gbias,
                m_gain=mlstm_norm_gain.reshape(depth, 1, M_DIM), r_gain=ret_norm_gain.reshape(depth, 1, R_DIM),
                sinks=attn_sinks, w_out=w_out_p, fgain=final_norm_gain.reshape(1, D_MODEL))


def kernel(x_prompt, x_sample, state_mlstm_C, state_mlstm_n, state_mlstm_m, state_ret_S, cache_win_k,
           cache_win_v, norm_gain, w_in, mlstm_gate_bias, mlstm_norm_gain, ret_norm_gain, attn_sinks,
           rel_bias_table, w_out, final_norm_gain):
    depth = w_in.shape[0]
    seq = x_prompt.shape[1]
    dec_seq = x_sample.shape[1]
    past_len = seq
    p = _prepare_params(norm_gain, w_in, mlstm_gate_bias, mlstm_norm_gain, ret_norm_gain, attn_sinks, w_out,
                        final_norm_gain)
    tabs_p = _static_tables(1)
    tabs_s = _static_tables(ROWS // dec_seq)
    bias_cp, bias_cs, bias_ps = _bias_tables(rel_bias_table, tabs_s)
    cos_p, sin_p = _rope_tables(jnp.arange(seq, dtype=jnp.int32))
    cos_s, sin_s = _rope_tables(past_len + (jnp.arange(ROWS, dtype=jnp.int32) % dec_seq))
    states = (state_mlstm_C, state_mlstm_n, state_mlstm_m, state_ret_S, cache_win_k, cache_win_v)

    xp = x_prompt
    p_states = []
    for layer in range(depth):
        xp, *sp = _prompt_layer(xp, p, layer, tabs_p, bias_cp, cos_p, sin_p, layer == depth - 1)
        p_states.append(sp)
    outs_p = [jnp.stack([p_states[l][i] for l in range(depth)]) for i in range(6)]
    xs, *outs_s = _sample_path(x_sample, states, p, tabs_s, bias_cs, bias_ps, cos_s, sin_s)
    return (xp, xs, *outs_p, *outs_s)
```

```python
import functools

import numpy as np
import jax
import jax.numpy as jnp
from jax import lax
from jax.experimental import pallas as pl
from jax.experimental.pallas import tpu as pltpu

D_MODEL = 1024
HEAD_DIM = 64
M_HEADS = 4
R_HEADS = 4
A_HEADS = 8
A_KV_HEADS = 2
KV_GROUP = A_HEADS // A_KV_HEADS
M_DIM = M_HEADS * HEAD_DIM
R_DIM = R_HEADS * HEAD_DIM
A_DIM = A_HEADS * HEAD_DIM
A_KV_DIM = A_KV_HEADS * HEAD_DIM
WINDOW = 128
N_BUCKETS = 32
REL_MAX_DIST = 128
ROPE_BASE = 10000.0
NORM_EPS = 1e-6
QK_SCALE = HEAD_DIM ** -0.5

LANES = 128
ROWS = 128
GATE_PAD = LANES
PAIRS = M_HEADS // 2
SPLIT_TERMS = 3

OFF_MQ = 0
OFF_MK = OFF_MQ + M_DIM
OFF_MV = OFF_MK + M_DIM
OFF_MO = OFF_MV + M_DIM
OFF_MZ = OFF_MO + M_DIM
OFF_G = OFF_MZ + M_DIM
OFF_RQ = OFF_G + GATE_PAD
OFF_RK = OFF_RQ + R_DIM
OFF_RV = OFF_RK + R_DIM
OFF_RZ = OFF_RV + R_DIM
OFF_AQ = OFF_RZ + R_DIM
OFF_AK = OFF_AQ + A_DIM
OFF_AV = OFF_AK + A_KV_DIM
OFF_AZ = OFF_AV + A_KV_DIM
P_COLS = OFF_AZ + A_DIM
N_GATES = 2 * M_HEADS
PROJ_COL_BLOCK = 512
ATTN_HEAD_ORDER = tuple(h for j in range(KV_GROUP) for h in (j, KV_GROUP + j))
ATTN_HEAD_POS = tuple(ATTN_HEAD_ORDER.index(h) for h in range(A_HEADS))

PROMPT_ROWS = 512
VMEM_LIMIT_BYTES = 56 * 1024 * 1024

f32 = jnp.float32
bf16 = jnp.bfloat16


def _dot(a, b):
    return jnp.dot(a, b, preferred_element_type=f32)


def _dot_nt(a, b):
    return lax.dot_general(a, b, (((1,), (1,)), ((), ())), preferred_element_type=f32)


def _sigmoid(x):
    return 1.0 / (1.0 + jnp.exp(-x))


def _silu(x):
    return x * _sigmoid(x)


def _log_sigmoid(x):
    return jnp.minimum(x, 0.0) - jnp.log(1.0 + jnp.exp(-jnp.abs(x)))


def _split_parts(x, terms):
    parts, r = [], x
    for i in range(terms):
        p = r.astype(bf16)
        parts.append(p)
        if i + 1 < terms:
            r = r - p.astype(f32)
    return parts


def _split_terms(x, terms=SPLIT_TERMS):
    return jnp.concatenate(_split_parts(x, terms), axis=1)


def _exact_tril_dot(tril3, x):
    return _dot(tril3, jnp.concatenate(_split_parts(x, SPLIT_TERMS), axis=0))


def _rope(x, cos_t, sin_t, first_half):
    up = pltpu.roll(x, LANES - HEAD_DIM // 2, axis=1)
    down = pltpu.roll(x, HEAD_DIM // 2, axis=1)
    return x * cos_t + jnp.where(first_half, up, down) * sin_t


def _rms_project(x_ref, ngain_ref, win_ref, proj_sc):
    xf = x_ref[...]
    u = xf * lax.rsqrt(jnp.mean(xf * xf, axis=1, keepdims=True) + NORM_EPS) * ngain_ref[...]
    u16 = u.astype(bf16)
    for c0 in range(0, P_COLS, PROJ_COL_BLOCK):
        c1 = min(c0 + PROJ_COL_BLOCK, P_COLS)
        proj_sc[:, c0:c1] = _dot_nt(u16, win_ref[c0:c1, :])


def _out_project(x_ref, mix_sc, wout_ref, fgain_ref, y_ref, final):
    y = x_ref[...] + _dot(mix_sc[...], wout_ref[...])
    if final:
        y = y * lax.rsqrt(jnp.mean(y * y, axis=1, keepdims=True) + NORM_EPS) * fgain_ref[...]
    y_ref[...] = y


def _pair_norm(x, jj, gain):
    mean = _dot(_split_terms(x, 2), jj)
    xc = x - mean
    var = _dot(_split_terms(xc * xc, 2), jj)
    return xc * lax.rsqrt(var + NORM_EPS) * gain


def _prompt_kernel(x_ref, ngain_ref, win_ref, gbias_ref, mgain_ref, rgain_ref, sinks_ref, wout_ref,
                   fgain_ref, tril3_ref, maskadd_ref, dmat_ref, rslab_ref, biascp_ref, cos_ref, sin_ref,
                   selh_ref, selp_ref, jj_ref, eye_ref,
                   y_ref, c_out, n_out, m_out, s_out, k_out, v_out,
                   proj_sc, mix_sc, cn_sc, sb_sc, m_sc, kp_sc, vp_sc, *, chunks, final):
    step = pl.program_id(1)
    last_step = pl.num_programs(1) - 1

    _rms_project(x_ref, ngain_ref, win_ref, proj_sc)

    @pl.when(step == 0)
    def _():
        cn_sc[...] = jnp.zeros_like(cn_sc)
        sb_sc[...] = jnp.zeros_like(sb_sc)
        m_sc[...] = jnp.zeros_like(m_sc)
        kp_sc[...] = jnp.zeros_like(kp_sc)
        vp_sc[...] = jnp.zeros_like(vp_sc)

    lane = lax.broadcasted_iota(jnp.int32, (ROWS, LANES), 1)
    row = lax.broadcasted_iota(jnp.int32, (ROWS, LANES), 0)
    left = lane < HEAD_DIM
    first_half = (lane & (HEAD_DIM - 1)) < (HEAD_DIM // 2)
    blockdiag = (row < HEAD_DIM) == left
    row2 = lax.broadcasted_iota(jnp.int32, (ROWS, 2 * LANES), 0)
    lane2w = lax.broadcasted_iota(jnp.int32, (ROWS, 2 * LANES), 1)
    blockdiag2 = (row2 < HEAD_DIM) == ((lane2w & (LANES - 1)) < HEAD_DIM)
    ones_l = jnp.where(left, 1.0, 0.0).astype(bf16)
    ones_r = jnp.where(left, 0.0, 1.0).astype(bf16)
    ones16 = jnp.ones((ROWS, LANES), bf16)

    def half(x, side):
        keep = left if side == 0 else jnp.logical_not(left)
        return jnp.where(keep, x, 0.0).astype(bf16)

    def chunk_body(ci):
        rows = slice(ci * ROWS, (ci + 1) * ROWS)
        eye16 = eye_ref[...]
        jj = jj_ref[...]

        gates = proj_sc[rows, OFF_G:OFF_G + GATE_PAD] + gbias_ref[...]
        bcum = _exact_tril_dot(tril3_ref[...], _log_sigmoid(gates))
        zb = pltpu.roll(bcum, LANES - M_HEADS, axis=1)
        head_col = lane < M_HEADS
        r_mat = jnp.where(head_col, gates - zb, 0.0)
        cm = r_mat
        sh = 1
        while sh < ROWS:
            cm = jnp.where(row >= sh, jnp.maximum(cm, pltpu.roll(cm, sh, axis=0)), cm)
            sh *= 2
        mprev = m_sc[...]
        mx = jnp.maximum(mprev, cm)
        gm = mprev - mx
        em = jnp.where(head_col, -(zb + mx), 0.0)
        mx_last = jnp.broadcast_to(mx[ROWS - 1:ROWS, :], (ROWS, LANES))
        m_sc[...] = jnp.where(head_col, jnp.broadcast_to((zb + mx)[ROWS - 1:ROWS, :], (ROWS, LANES)), 0.0)
        mx_b = _dot_nt(_split_terms(mx), selh_ref[...])
        winter_b = jnp.exp(_dot_nt(_split_terms(gm), selp_ref[...]))
        emt_b = jnp.exp(_dot_nt(_split_terms(em), selp_ref[...]))
        ws_b = jnp.exp(_dot_nt(_split_terms(r_mat - mx_last), selp_ref[...]))
        r_t = r_mat.T
        maskadd = maskadd_ref[...]

        for p in range(PAIRS):
            ps = slice(p * LANES, (p + 1) * LANES)
            q = proj_sc[rows, OFF_MQ + p * LANES:OFF_MQ + (p + 1) * LANES]
            k = proj_sc[rows, OFF_MK + p * LANES:OFF_MK + (p + 1) * LANES] * QK_SCALE
            v = proj_sc[rows, OFF_MV + p * LANES:OFF_MV + (p + 1) * LANES]
            q16, k16, v16 = q.astype(bf16), k.astype(bf16), v.astype(bf16)
            cn = cn_sc[p]
            acc = jnp.concatenate([winter_b[:, ps]] * 2, axis=1) * _dot(q16, cn.astype(bf16))
            for side in range(2):
                h = 2 * p + side
                qk = _dot_nt(half(q, side), k16)
                w = jnp.exp((r_t[h:h + 1, :] + maskadd) - mx_b[:, h * ROWS:(h + 1) * ROWS]) * qk
                vn = jnp.concatenate([half(v, side), ones_l if side == 0 else ones_r], axis=1)
                acc = acc + _dot(w.astype(bf16), vn)
            hh = acc[:, :LANES] / jnp.maximum(jnp.abs(acc[:, LANES:]), emt_b[:, ps])
            kw16 = (k * ws_b[:, ps]).astype(bf16)
            kwt16 = _dot_nt(eye16, kw16).astype(bf16)
            dcn = _dot(kwt16, jnp.concatenate([v16, ones16], axis=1))
            decay = winter_b[ROWS - 1:ROWS, ps]
            cn_sc[p] = (jnp.concatenate([decay, decay], axis=1) * cn
                        + jnp.where(blockdiag2, dcn, 0.0))
            hm = _sigmoid(proj_sc[rows, OFF_MO + p * LANES:OFF_MO + (p + 1) * LANES]) * hh
            out = _pair_norm(hm, jj, mgain_ref[:, ps]) * _silu(
                proj_sc[rows, OFF_MZ + p * LANES:OFF_MZ + (p + 1) * LANES])
            mix_sc[rows, p * LANES:(p + 1) * LANES] = out.astype(bf16)

        cos_t = cos_ref[rows, :]
        sin_t = sin_ref[rows, :]
        for p in range(PAIRS):
            ps = slice(p * LANES, (p + 1) * LANES)
            q = _rope(proj_sc[rows, OFF_RQ + p * LANES:OFF_RQ + (p + 1) * LANES], cos_t, sin_t, first_half)
            k = _rope(proj_sc[rows, OFF_RK + p * LANES:OFF_RK + (p + 1) * LANES], cos_t, sin_t,
                      first_half) * QK_SCALE
            v = proj_sc[rows, OFF_RV + p * LANES:OFF_RV + (p + 1) * LANES]
            q16, k16, v16 = q.astype(bf16), k.astype(bf16), v.astype(bf16)
            sb = sb_sc[p]
            acc = rslab_ref[0, p] * _dot(q16, sb.astype(bf16))
            for side in range(2):
                h = 2 * p + side
                scores = _dot_nt(half(q, side), k16) * dmat_ref[h]
                acc = acc + _dot(scores.astype(bf16), half(v, side))
            kt16 = (k * rslab_ref[1, p]).astype(bf16)
            ktt16 = _dot_nt(eye16, kt16).astype(bf16)
            sb_sc[p] = rslab_ref[2, p] * sb + jnp.where(blockdiag, _dot(ktt16, v16), 0.0)
            out = _pair_norm(acc, jj, rgain_ref[:, ps]) * _silu(
                proj_sc[rows, OFF_RZ + p * LANES:OFF_RZ + (p + 1) * LANES])
            mix_sc[rows, M_DIM + p * LANES:M_DIM + (p + 1) * LANES] = out.astype(bf16)

        kcur = proj_sc[rows, OFF_AK:OFF_AK + A_KV_DIM]
        vcur = proj_sc[rows, OFF_AV:OFF_AV + A_KV_DIM]
        kprev, vprev = kp_sc[...], vp_sc[...]
        kk16 = jnp.concatenate([kcur, kprev], axis=0).astype(bf16)
        vv = [jnp.concatenate([jnp.concatenate([half(vcur, s), ones_l if s == 0 else ones_r], axis=1),
                               jnp.concatenate([half(vprev, s), ones_l if s == 0 else ones_r], axis=1)],
                              axis=0) for s in range(2)]
        if ci == 0:
            pen = jnp.where(step == 0, -jnp.inf, 0.0).astype(f32)
            lane2 = lax.broadcasted_iota(jnp.int32, (1, 2 * ROWS), 1)
            pen_row = jnp.where(lane2 >= ROWS, pen, 0.0)
        else:
            pen_row = None
        for j in range(KV_GROUP):
            q = proj_sc[rows, OFF_AQ + j * LANES:OFF_AQ + (j + 1) * LANES] * QK_SCALE
            acc = None
            esink = []
            for side in range(2):
                h = ATTN_HEAD_ORDER[2 * j + side]
                s = _dot_nt(half(q, side), kk16) + biascp_ref[h]
                if pen_row is not None:
                    s = s + pen_row
                sink = sinks_ref[h]
                m = jnp.maximum(jnp.max(jnp.maximum(s[:, :ROWS], s[:, ROWS:]), axis=1, keepdims=True), sink)
                part = _dot(jnp.exp(s - m).astype(bf16), vv[side])
                acc = part if acc is None else acc + part
                esink.append(jnp.exp(sink - m))
            den = acc[:, LANES:] + jnp.where(left, esink[0], esink[1])
            out = (acc[:, :LANES] / den) * _silu(proj_sc[rows, OFF_AZ + j * LANES:OFF_AZ + (j + 1) * LANES])
            mix_sc[rows, M_DIM + R_DIM + j * LANES:M_DIM + R_DIM + (j + 1) * LANES] = out.astype(bf16)
        kp_sc[...] = kcur
        vp_sc[...] = vcur

    for ci in range(chunks):
        chunk_body(ci)

    _out_project(x_ref, mix_sc, wout_ref, fgain_ref, y_ref, final)

    @pl.when(step == last_step)
    def _():
        for p in range(PAIRS):
            cn = cn_sc[p]
            sb = sb_sc[p]
            n_t = cn[:, LANES:].T
            for side in range(2):
                h = 2 * p + side
                blk = slice(side * HEAD_DIM, (side + 1) * HEAD_DIM)
                c_out[0, h] = cn[blk, blk]
                s_out[0, h] = sb[blk, blk]
                n_out[0, h:h + 1, :] = n_t[side * HEAD_DIM:side * HEAD_DIM + 1, blk]
        m_out[0] = m_sc[...]
        k_out[0] = kp_sc[...]
        v_out[0] = vp_sc[...]


def _staged_prompt_kernel(x_ref, ngain_ref, win_ref, gbias_ref, mgain_ref, rgain_ref, sinks_ref, wout_ref,
                          fgain_ref, tril3_ref, maskadd_ref, dmat2_ref, rslab_ref, ubias_ref, cos_ref,
                          sin_ref, selh_ref, selp_ref,
                          y_ref, c_out, n_out, m_out, s_out, k_out, v_out,
                          proj_sc, mix_sc, cn_sc, sb_sc, m_sc, kp_sc, vp_sc, bias_sc, *, chunks, final):
    step = pl.program_id(1)
    last_step = pl.num_programs(1) - 1

    @pl.when(jnp.logical_and(pl.program_id(0) == 0, step == 0))
    def _():
        for blk in range(A_HEADS):
            u = ubias_ref[ATTN_HEAD_ORDER[blk]:ATTN_HEAD_ORDER[blk] + 1, :]
            bias_sc[blk * ROWS:(blk + 1) * ROWS, :] = jnp.concatenate(
                [_skew(u[:, :2 * WINDOW], ROWS), _skew(u[:, 2 * WINDOW:], ROWS)], axis=1)

    xf = x_ref[...]
    u16 = (xf * lax.rsqrt(jnp.mean(xf * xf, axis=1, keepdims=True) + NORM_EPS) * ngain_ref[...]).astype(bf16)
    col_blocks = [(c0, min(c0 + PROJ_COL_BLOCK, P_COLS)) for c0 in range(0, P_COLS, PROJ_COL_BLOCK)]
    half_rows = (chunks // 2) * ROWS if chunks > 1 else chunks * ROWS

    def project(r0, r1, c0, c1):
        proj_sc[r0:r1, c0:c1] = _dot_nt(u16[r0:r1], win_ref[c0:c1, :])

    for c0, c1 in col_blocks:
        project(0, half_rows, c0, c1)
    late_pieces = [(half_rows, chunks * ROWS, c0, c1) for c0, c1 in col_blocks] if half_rows < chunks * ROWS else []

    @pl.when(step == 0)
    def _():
        cn_sc[...] = jnp.zeros_like(cn_sc)
        sb_sc[...] = jnp.zeros_like(sb_sc)
        m_sc[...] = jnp.zeros_like(m_sc)
        kp_sc[...] = jnp.zeros_like(kp_sc)
        vp_sc[...] = jnp.zeros_like(vp_sc)

    lane = lax.broadcasted_iota(jnp.int32, (ROWS, LANES), 1)
    row = lax.broadcasted_iota(jnp.int32, (ROWS, LANES), 0)
    left = lane < HEAD_DIM
    first_half = (lane & (HEAD_DIM - 1)) < (HEAD_DIM // 2)
    head_col = lane < M_HEADS
    blockdiag = (row < HEAD_DIM) == left
    row2 = lax.broadcasted_iota(jnp.int32, (ROWS, 2 * LANES), 0)
    lane2w = lax.broadcasted_iota(jnp.int32, (ROWS, 2 * LANES), 1)
    left2 = (lane2w & (LANES - 1)) < HEAD_DIM
    blockdiag2 = (row2 < HEAD_DIM) == left2
    ones16 = jnp.ones((ROWS, LANES), bf16)

    def halves(x):
        return jnp.concatenate([jnp.where(left, x, 0.0), jnp.where(left, 0.0, x)], axis=0).astype(bf16)

    def pick(x, mask):
        return jnp.where(mask, x[:ROWS], x[ROWS:])

    def pair_blockdiag(blocks):
        z = jnp.zeros_like(blocks[0])
        return jnp.concatenate(
            [jnp.concatenate([blk if j == i else z for j in range(len(blocks))], axis=1)
             for i, blk in enumerate(blocks)], axis=0)

    def half_mean(x):
        lane_n = lax.broadcasted_iota(jnp.int32, x.shape, 1)
        is_left = lane_n < HEAD_DIM
        s_left = jnp.sum(jnp.where(is_left, x, 0.0), axis=1, keepdims=True)
        s_right = jnp.sum(jnp.where(is_left, 0.0, x), axis=1, keepdims=True)
        return jnp.where(is_left, s_left, s_right) * (1.0 / HEAD_DIM)

    def chunk_body(ci):
        rows = slice(ci * ROWS, (ci + 1) * ROWS)

        def proj(off, width=LANES):
            return proj_sc[rows, off:off + width]

        gates = proj(OFF_G) + gbias_ref[...]
        bcum = _exact_tril_dot(tril3_ref[...], _log_sigmoid(gates))

        cos_t, sin_t = cos_ref[rows, :], sin_ref[rows, :]
        r_q = [_rope(proj(OFF_RQ + p * LANES), cos_t, sin_t, first_half) for p in range(PAIRS)]
        r_k = [_rope(proj(OFF_RK + p * LANES), cos_t, sin_t, first_half) * QK_SCALE for p in range(PAIRS)]
        r_vbd = pair_blockdiag([proj(OFF_RV + p * LANES).astype(bf16) for p in range(PAIRS)])
        r_sb = [sb_sc[p] for p in range(PAIRS)]
        r_sc = _dot_nt(jnp.concatenate([halves(q) for q in r_q], axis=1),
                       pair_blockdiag([k.astype(bf16) for k in r_k]))
        r_inter = _dot(jnp.concatenate([q.astype(bf16) for q in r_q], axis=1),
                       pair_blockdiag([sb.astype(bf16) for sb in r_sb]))
        r_upd = _dot(jnp.concatenate([(r_k[p] * rslab_ref[1, p]).T.astype(bf16) for p in range(PAIRS)], axis=1),
                     r_vbd)
        for p in range(PAIRS):
            sb_sc[p] = rslab_ref[2, p] * r_sb[p] + jnp.where(blockdiag, r_upd[:, p * LANES:(p + 1) * LANES], 0.0)
        yield

        kcur, vcur = proj(OFF_AK), proj(OFF_AV)
        kprev, vprev = kp_sc[...], vp_sc[...]
        kk16 = jnp.concatenate([kcur, kprev], axis=0).astype(bf16)
        vv16 = jnp.concatenate([jnp.concatenate([vcur.astype(bf16), ones16], axis=1),
                                jnp.concatenate([vprev.astype(bf16), ones16], axis=1)], axis=0)
        kp_sc[...] = kcur
        vp_sc[...] = vcur
        a_q = jnp.concatenate([halves(proj(OFF_AQ + j * LANES) * QK_SCALE) for j in range(KV_GROUP)], axis=0)
        a_s = _dot_nt(a_q, kk16) + bias_sc[...]
        if ci == 0:
            pen = jnp.where(step == 0, -jnp.inf, 0.0).astype(f32)
            a_s = a_s + jnp.where(lax.broadcasted_iota(jnp.int32, (1, 2 * ROWS), 1) >= ROWS, pen, 0.0)

        m_q = [proj(OFF_MQ + p * LANES) for p in range(PAIRS)]
        m_k = [proj(OFF_MK + p * LANES) * QK_SCALE for p in range(PAIRS)]
        m_qk = _dot_nt(jnp.concatenate([halves(q) for q in m_q], axis=1),
                       pair_blockdiag([k.astype(bf16) for k in m_k]))
        m_q16 = [q.astype(bf16) for q in m_q]
        m_v16 = [proj(OFF_MV + p * LANES).astype(bf16) for p in range(PAIRS)]
        yield

        zb = pltpu.roll(bcum, LANES - M_HEADS, axis=1)
        r_mat = jnp.where(head_col, gates - zb, 0.0)
        cm = r_mat
        sh = 1
        while sh < ROWS:
            cm = jnp.where(row >= sh, jnp.maximum(cm, pltpu.roll(cm, sh, axis=0)), cm)
            sh *= 2
        mprev = m_sc[...]
        mx = jnp.maximum(mprev, cm)
        gm = mprev - mx
        em = jnp.where(head_col, -(zb + mx), 0.0)
        mx_last = jnp.broadcast_to(mx[ROWS - 1:ROWS, :], (ROWS, LANES))
        m_sc[...] = jnp.where(head_col, jnp.broadcast_to((zb + mx)[ROWS - 1:ROWS, :], (ROWS, LANES)), 0.0)
        mx_b = _dot_nt(_split_terms(mx), selh_ref[...])
        slabs = jnp.exp(_dot_nt(_split_terms(jnp.concatenate([gm, em, r_mat - mx_last], axis=0)),
                                selp_ref[...]))
        winter_b, emt_b, ws_b = slabs[:ROWS], slabs[ROWS:2 * ROWS], slabs[2 * ROWS:]
        r_t = r_mat.T
        yield

        outs = []
        r_acc = []
        r_o = _dot((r_sc * dmat2_ref[...]).astype(bf16), r_vbd)
        for p in range(PAIRS):
            ps = slice(p * LANES, (p + 1) * LANES)
            r_acc.append(pick(r_o[:, ps], left) + rslab_ref[0, p] * r_inter[:, ps])

        a_out = []
        a_p = []
        for blk in range(A_HEADS):
            s = a_s[blk * ROWS:(blk + 1) * ROWS]
            sink = sinks_ref[ATTN_HEAD_ORDER[blk]]
            m = jnp.maximum(jnp.max(jnp.maximum(s[:, :ROWS], s[:, ROWS:]), axis=1, keepdims=True), sink)
            a_p.append(jnp.exp(s - m).astype(bf16))
            a_out.append(jnp.exp(sink - m))
        a_pv = _dot(jnp.concatenate(a_p, axis=0), vv16)
        yield

        maskadd = maskadd_ref[...]
        for p in range(PAIRS):
            ps = slice(p * LANES, (p + 1) * LANES)
            cn = cn_sc[p]
            w = jnp.concatenate(
                [jnp.exp((r_t[2 * p + side:2 * p + side + 1, :] + maskadd)
                         - mx_b[:, (2 * p + side) * ROWS:(2 * p + side + 1) * ROWS]) for side in range(2)],
                axis=0) * m_qk[:, ps]
            acc = (pick(_dot(w.astype(bf16), jnp.concatenate([m_v16[p], ones16], axis=1)), left2)
                   + jnp.concatenate([winter_b[:, ps]] * 2, axis=1) * _dot(m_q16[p], cn.astype(bf16)))
            hh = acc[:, :LANES] / jnp.maximum(jnp.abs(acc[:, LANES:]), emt_b[:, ps])
            outs.append(_sigmoid(proj(OFF_MO + p * LANES)) * hh)
            kwt16 = (m_k[p] * ws_b[:, ps]).T.astype(bf16)
            dcn = _dot(kwt16, jnp.concatenate([m_v16[p], ones16], axis=1))
            decay = winter_b[ROWS - 1:ROWS, ps]
            cn_sc[p] = jnp.concatenate([decay, decay], axis=1) * cn + jnp.where(blockdiag2, dcn, 0.0)
        outs.extend(r_acc)
        yield

        x4 = jnp.concatenate(outs, axis=0)
        xc = x4 - half_mean(x4)
        y4 = xc * lax.rsqrt(half_mean(xc * xc) + NORM_EPS)
        for i in range(2 * PAIRS):
            gain = (mgain_ref if i < PAIRS else rgain_ref)[:, (i % PAIRS) * LANES:(i % PAIRS + 1) * LANES]
            zoff = (OFF_MZ if i < PAIRS else OFF_RZ) + (i % PAIRS) * LANES
            out = y4[i * ROWS:(i + 1) * ROWS] * gain * _silu(proj(zoff))
            mix_sc[rows, i * LANES:(i + 1) * LANES] = out.astype(bf16)
        for j in range(KV_GROUP):
            acc = pick(a_pv[2 * j * ROWS:(2 * j + 2) * ROWS], left2)
            den = acc[:, LANES:] + jnp.where(left, a_out[2 * j], a_out[2 * j + 1])
            out = (acc[:, :LANES] / den) * _silu(proj(OFF_AZ + j * LANES))
            mix_sc[rows, M_DIM + R_DIM + j * LANES:M_DIM + R_DIM + (j + 1) * LANES] = out.astype(bf16)

    def out_project(r0, r1, c0, c1):
        y_ref[r0:r1, c0:c1] = x_ref[r0:r1, c0:c1] + _dot(mix_sc[r0:r1, :], wout_ref[:, c0:c1])

    out_blocks = [(c0, min(c0 + PROJ_COL_BLOCK, D_MODEL)) for c0 in range(0, D_MODEL, PROJ_COL_BLOCK)]
    early_out = [(0, half_rows, c0, c1) for c0, c1 in out_blocks] if half_rows < chunks * ROWS else []
    final_out = [(half_rows if early_out else 0, chunks * ROWS, c0, c1) for c0, c1 in out_blocks]

    def fill_mxu(ci):
        if late_pieces:
            project(*late_pieces.pop(0))
        elif early_out and (ci - 1) * ROWS >= half_rows:
            out_project(*early_out.pop(0))

    parts = [chunk_body(ci) for ci in range(chunks)]
    for ci in range(chunks + 1):
        if ci * ROWS >= half_rows:
            while late_pieces:
                project(*late_pieces.pop(0))
        for _ in range(3):
            if ci < chunks:
                next(parts[ci])
                fill_mxu(ci)
            if ci > 0:
                next(parts[ci - 1], None)
                fill_mxu(ci)
    for piece in early_out + final_out:
        out_project(*piece)
    if final:
        y = y_ref[...]
        y_ref[...] = y * lax.rsqrt(jnp.mean(y * y, axis=1, keepdims=True) + NORM_EPS) * fgain_ref[...]

    @pl.when(step == last_step)
    def _():
        for p in range(PAIRS):
            cn = cn_sc[p]
            sb = sb_sc[p]
            n_t = cn[:, LANES:].T
            for side in range(2):
                h = 2 * p + side
                blk = slice(side * HEAD_DIM, (side + 1) * HEAD_DIM)
                c_out[0, h] = cn[blk, blk]
                s_out[0, h] = sb[blk, blk]
                n_out[0, h:h + 1, :] = n_t[side * HEAD_DIM:side * HEAD_DIM + 1, blk]
        m_out[0] = m_sc[...]
        k_out[0] = kp_sc[...]
        v_out[0] = vp_sc[...]


def _head_norm(x, gain):
    mu = jnp.mean(x, axis=1, keepdims=True)
    xc = x - mu
    var = jnp.mean(xc * xc, axis=1, keepdims=True)
    return xc * lax.rsqrt(var + NORM_EPS) * gain


def _group_last(x, groups):
    n = x.shape[1]
    glen = ROWS // groups
    x3 = x.reshape(groups, glen, n)
    return jnp.broadcast_to(x3[:, glen - 1:glen, :], (groups, glen, n)).reshape(ROWS, n)


def _state_rows(col, groups):
    glen = ROWS // groups
    wide = jnp.broadcast_to(col, (ROWS, HEAD_DIM)).reshape(groups, glen, HEAD_DIM)
    per_group = wide[:, 0:1, :]
    rows = jnp.broadcast_to(per_group, (groups, HEAD_DIM, HEAD_DIM)).reshape(groups * HEAD_DIM, HEAD_DIM)
    return rows, per_group.reshape(groups, HEAD_DIM)


def _cols_to_mat(cols, lane):
    acc = jnp.zeros((ROWS, LANES), f32)
    for h, c in enumerate(cols):
        acc = jnp.where(lane == h, c, acc)
    return acc


def _sample_kernel(x_ref, ngain_ref, win_ref, gbias_ref, mgain_ref, rgain_ref, sinks_ref, wout_ref,
                   fgain_ref, tril3_ref, maskadd_ref, dmat_ref, rc_ref, ubias_ref, cos_ref, sin_ref,
                   c_in, n_in, m_in, s_in, k_in, v_in,
                   y_ref, c_out, n_out, m_out, s_out, k_out, v_out,
                   proj_sc, mix_sc, xcur_sc, qb_sc, sp_sc, pp_sc, ob_sc, biasc_ref, biasp_ref, *, groups,
                   ret_full):
    glen = ROWS // groups
    glen_log2 = glen.bit_length() - 1
    hd_log2 = HEAD_DIM.bit_length() - 1
    step = pl.program_id(0)
    layer = pl.program_id(1)
    last_layer = pl.num_programs(1) - 1

    @pl.when(layer == 0)
    def _():
        xcur_sc[...] = x_ref[...]

    _rms_project(xcur_sc, ngain_ref, win_ref, proj_sc)

    @pl.when(jnp.logical_and(step == 0, layer == 0))
    def _():
        qb_sc[...] = jnp.zeros_like(qb_sc)
        for h in range(A_HEADS):
            u = ubias_ref[h:h + 1, :]
            biasc_ref[h] = _skew(u[:, :2 * WINDOW], ROWS) + maskadd_ref[...]
            biasp_ref[h] = jnp.concatenate([_skew(u[:, 2 * WINDOW:], glen)] * groups, axis=0)

    lane = lax.broadcasted_iota(jnp.int32, (ROWS, LANES), 1)
    first_half = (lane & (HEAD_DIM - 1)) < (HEAD_DIM // 2)
    rows = slice(0, ROWS)

    r_i = lax.broadcasted_iota(jnp.int32, (ROWS, groups * HEAD_DIM), 0)
    c_i = lax.broadcasted_iota(jnp.int32, (ROWS, groups * HEAD_DIM), 1)
    blk = (r_i >> glen_log2) == (c_i >> hd_log2)
    r_t = lax.broadcasted_iota(jnp.int32, (groups * HEAD_DIM, ROWS), 0)
    c_t = lax.broadcasted_iota(jnp.int32, (groups * HEAD_DIM, ROWS), 1)
    blk_t = (r_t >> hd_log2) == (c_t >> glen_log2)

    def q_times_state(qh, st):
        qt = jnp.where(blk, jnp.concatenate([qh] * groups, axis=1), 0.0)
        return _dot(qt.astype(bf16), st.astype(bf16))

    def state_increment(kt_h, vh16):
        kt = jnp.where(blk_t, jnp.concatenate([kt_h] * groups, axis=0), 0.0)
        return _dot(kt.astype(bf16), vh16)

    gates = proj_sc[rows, OFF_G:OFF_G + GATE_PAD] + gbias_ref[...]
    bcum = _exact_tril_dot(tril3_ref[...], _log_sigmoid(gates))
    z = jnp.where(lane < M_HEADS, gates, bcum)
    zt = z.T
    zb = pltpu.roll(z, LANES - M_HEADS, axis=1)
    maskadd = maskadd_ref[...]

    dlogs, mintra = [], []
    for h in range(M_HEADS):
        bcol = zb[:, h:h + 1]
        dlog = (bcol - zt[M_HEADS + h:M_HEADS + h + 1, :]) + zt[h:h + 1, :] + maskadd
        dlogs.append(dlog)
        mintra.append(jnp.max(dlog, axis=1, keepdims=True))
    mprev = m_in[...]
    g_all = zb + mprev
    mt_all = jnp.maximum(g_all, _cols_to_mat(mintra, lane))
    winter_all = jnp.exp(g_all - mt_all)
    emt_all = jnp.exp(-mt_all)
    mlast_all = _group_last(mt_all, groups)
    blast_all = _group_last(zb, groups)
    decay_all = jnp.exp(blast_all + mprev - mlast_all)
    ws_all = jnp.exp(blast_all - zb + z - mlast_all)

    mq = proj_sc[rows, OFF_MQ:OFF_MQ + M_DIM]
    mk = proj_sc[rows, OFF_MK:OFF_MK + M_DIM] * QK_SCALE
    mv = proj_sc[rows, OFF_MV:OFF_MV + M_DIM]
    kw_parts = []
    for h in range(M_HEADS):
        hs = slice(h * HEAD_DIM, (h + 1) * HEAD_DIM)
        kw_parts.append(mk[:, hs] * ws_all[:, h:h + 1])
    kwt = jnp.concatenate(kw_parts, axis=1).T

    hm = []
    for h in range(M_HEADS):
        hs = slice(h * HEAD_DIM, (h + 1) * HEAD_DIM)
        qh, kh, vh = mq[:, hs], mk[:, hs], mv[:, hs]
        vh16 = vh.astype(bf16)
        mt = mt_all[:, h:h + 1]
        winter = winter_all[:, h:h + 1]
        wintra = jnp.exp(dlogs[h] - mt) * _dot_nt(qh.astype(bf16), kh.astype(bf16))
        c_h = c_in[:, h].reshape(groups * HEAD_DIM, HEAD_DIM)
        num = winter * q_times_state(qh, c_h) + _dot(wintra.astype(bf16), vh16)
        n_g = n_in[h]
        n_rows = jnp.broadcast_to(n_g.reshape(groups, 1, HEAD_DIM),
                                  (groups, glen, HEAD_DIM)).reshape(ROWS, HEAD_DIM)
        nq = (winter * jnp.sum(qh * n_rows, axis=1, keepdims=True)
              + jnp.sum(wintra, axis=1, keepdims=True))
        hm.append(num / jnp.maximum(jnp.abs(nq), emt_all[:, h:h + 1]))
        dec_rows, dec_g = _state_rows(decay_all[:, h:h + 1], groups)
        c_new = dec_rows * c_h + state_increment(kwt[hs, :], vh16)
        c_out[:, h] = c_new.reshape(groups, HEAD_DIM, HEAD_DIM)
        n_out[h] = dec_g * n_g + jnp.sum(kw_parts[h].reshape(groups, glen, HEAD_DIM), axis=1)
    m_out[...] = mlast_all
    hm = jnp.concatenate(hm, axis=1)
    hm = _sigmoid(proj_sc[rows, OFF_MO:OFF_MO + M_DIM]) * hm
    mgain = mgain_ref[...]
    out_m = jnp.concatenate(
        [_head_norm(hm[:, h * HEAD_DIM:(h + 1) * HEAD_DIM], mgain[:, h * HEAD_DIM:(h + 1) * HEAD_DIM])
         for h in range(M_HEADS)], axis=1)
    out_m = out_m * _silu(proj_sc[rows, OFF_MZ:OFF_MZ + M_DIM])
    mix_sc[rows, 0:M_DIM] = out_m.astype(bf16)

    cos_t, sin_t = cos_ref[...], sin_ref[...]
    rq = jnp.concatenate([_rope(proj_sc[rows, OFF_RQ + p * LANES:OFF_RQ + (p + 1) * LANES], cos_t, sin_t,
                                first_half) for p in range(PAIRS)], axis=1)
    rk = jnp.concatenate([_rope(proj_sc[rows, OFF_RK + p * LANES:OFF_RK + (p + 1) * LANES], cos_t, sin_t,
                                first_half) for p in range(PAIRS)], axis=1) * QK_SCALE
    rv = proj_sc[rows, OFF_RV:OFF_RV + R_DIM]
    rc = rc_ref[...]
    rkt = jnp.concatenate(
        [rk[:, h * HEAD_DIM:(h + 1) * HEAD_DIM] * rc[:, R_HEADS + h:R_HEADS + h + 1]
         for h in range(R_HEADS)], axis=1).T
    hr = []
    for h in range(R_HEADS):
        hs = slice(h * HEAD_DIM, (h + 1) * HEAD_DIM)
        qh, kh = rq[:, hs], rk[:, hs]
        vh16 = rv[:, hs].astype(bf16)
        scores = _dot_nt(qh.astype(bf16), kh.astype(bf16)) * dmat_ref[h]
        s_h = s_in[:, h].reshape(groups * HEAD_DIM, HEAD_DIM)
        hr.append(_dot(scores.astype(bf16), vh16) + rc[:, h:h + 1] * q_times_state(qh, s_h))
        s_new = ret_full[h] * s_h + state_increment(rkt[hs, :], vh16)
        s_out[:, h] = s_new.reshape(groups, HEAD_DIM, HEAD_DIM)
    rgain = rgain_ref[...]
    out_r = jnp.concatenate(
        [_head_norm(hr[h], rgain[:, h * HEAD_DIM:(h + 1) * HEAD_DIM]) for h in range(R_HEADS)], axis=1)
    out_r = out_r * _silu(proj_sc[rows, OFF_RZ:OFF_RZ + R_DIM])
    mix_sc[rows, M_DIM:M_DIM + R_DIM] = out_r.astype(bf16)

    kcur = proj_sc[rows, OFF_AK:OFF_AK + A_KV_DIM]
    vcur = proj_sc[rows, OFF_AV:OFF_AV + A_KV_DIM]
    kcur16, vcur16 = kcur.astype(bf16), vcur.astype(bf16)

    def q_cols(h):
        c0 = OFF_AQ + ATTN_HEAD_POS[h] * HEAD_DIM
        return slice(c0, c0 + HEAD_DIM)

    for h in range(A_HEADS):
        off = (h // KV_GROUP) * HEAD_DIM
        qb_sc[:, h * glen:(h + 1) * glen, off:off + HEAD_DIM] = (
            proj_sc[rows, q_cols(h)].reshape(groups, glen, HEAD_DIM).astype(bf16))

    for b in range(groups):
        sp = _dot(qb_sc[b], k_in[b].astype(bf16))
        sp_sc[:, b * glen:(b + 1) * glen, :] = sp.reshape(A_HEADS, glen, WINDOW)

    dens, o_cur = [], []
    for h in range(A_HEADS):
        kv = h // KV_GROUP
        ks = slice(kv * HEAD_DIM, (kv + 1) * HEAD_DIM)
        qh16 = proj_sc[rows, q_cols(h)].astype(bf16)
        sc = _dot_nt(qh16, kcur16[:, ks]) * QK_SCALE + biasc_ref[h]
        sp = sp_sc[h] * QK_SCALE + biasp_ref[h]
        sink = sinks_ref[layer, h]
        m = jnp.maximum(jnp.maximum(jnp.max(sc, axis=1, keepdims=True),
                                    jnp.max(sp, axis=1, keepdims=True)), sink)
        pc = jnp.exp(sc - m)
        pp = jnp.exp(sp - m)
        dens.append(jnp.sum(pc, axis=1, keepdims=True) + jnp.sum(pp, axis=1, keepdims=True)
                    + jnp.exp(sink - m))
        o_cur.append(_dot(pc.astype(bf16), vcur16[:, ks]))
        pp_sc[:, h * glen:(h + 1) * glen, :] = pp.reshape(groups, glen, WINDOW).astype(bf16)

    for b in range(groups):
        ob = _dot_nt(pp_sc[b], v_in[b].astype(bf16))
        ob_sc[:, b * glen:(b + 1) * glen, :] = ob.reshape(A_HEADS, glen, A_KV_DIM)
    outs = []
    for h in ATTN_HEAD_ORDER:
        off = (h // KV_GROUP) * HEAD_DIM
        outs.append((o_cur[h] + ob_sc[h][:, off:off + HEAD_DIM]) / dens[h])
    kcur_t, vcur_t = kcur.T, vcur.T
    fresh = lane >= WINDOW - glen
    for b in range(groups):
        shift = (WINDOW - glen - b * glen) % LANES
        k_out[b] = jnp.where(fresh, pltpu.roll(kcur_t, shift, axis=1), pltpu.roll(k_in[b], WINDOW - glen, axis=1))
        v_out[b] = jnp.where(fresh, pltpu.roll(vcur_t, shift, axis=1), pltpu.roll(v_in[b], WINDOW - glen, axis=1))
    out_a = jnp.concatenate(outs, axis=1) * _silu(proj_sc[rows, OFF_AZ:OFF_AZ + A_DIM])
    mix_sc[rows, M_DIM + R_DIM:M_DIM + R_DIM + A_DIM] = out_a.astype(bf16)

    y = xcur_sc[...] + _dot(mix_sc[...], wout_ref[...])
    xcur_sc[...] = y

    @pl.when(layer == last_layer)
    def _():
        y_ref[...] = y * lax.rsqrt(jnp.mean(y * y, axis=1, keepdims=True) + NORM_EPS) * fgain_ref[...]


def _t5_bucket(dist):
    max_exact = N_BUCKETS // 2
    d = np.maximum(dist, 1).astype(np.float32)
    large = max_exact + (np.log(d / max_exact) / np.log(REL_MAX_DIST / max_exact)
                         * (N_BUCKETS - max_exact)).astype(np.int32)
    large = np.minimum(large, N_BUCKETS - 1)
    return np.where(dist < max_exact, dist, large).astype(np.int32)


def _static_tables(groups):
    glen = ROWS // groups
    r = np.arange(ROWS)
    grp, tau = r // glen, r % glen
    causal = (grp[:, None] == grp[None, :]) & (tau[None, :] <= tau[:, None])
    tril = causal.astype(np.float32)
    maskadd = np.where(causal, 0.0, -np.inf).astype(np.float32)
    log_g = np.log1p(-np.exp2(-5.0 - np.arange(R_HEADS, dtype=np.float64)))
    diff = (tau[:, None] - tau[None, :]).astype(np.float64)
    dmat = np.where(causal[None], np.exp(log_g[:, None, None] * np.maximum(diff, 0.0)[None]), 0.0)
    inter = np.exp(log_g[None, :] * (tau[:, None] + 1.0))
    tail = np.exp(log_g[None, :] * (glen - 1.0 - tau[:, None]))
    full = np.exp(log_g * glen)
    rc = np.zeros((ROWS, LANES), np.float64)
    rc[:, 0:R_HEADS] = inter
    rc[:, R_HEADS:2 * R_HEADS] = tail
    lane_head = np.arange(LANES) // HEAD_DIM
    rslab = np.zeros((3, PAIRS, ROWS, LANES), np.float64)
    for p in range(PAIRS):
        rslab[0, p] = inter[:, 2 * p + lane_head]
        rslab[1, p] = tail[:, 2 * p + lane_head]
        rslab[2, p] = full[2 * p + lane_head][None, :]
    selh = np.zeros((M_HEADS * ROWS, SPLIT_TERMS * LANES), np.float32)
    selp = np.zeros((PAIRS * LANES, SPLIT_TERMS * LANES), np.float32)
    for t in range(SPLIT_TERMS):
        for h in range(M_HEADS):
            selh[h * ROWS:(h + 1) * ROWS, t * LANES + h] = 1.0
        for p in range(PAIRS):
            for side in range(2):
                selp[p * LANES + side * HEAD_DIM:p * LANES + (side + 1) * HEAD_DIM, t * LANES + 2 * p + side] = 1.0
    jj = np.zeros((2 * LANES, LANES), np.float32)
    for t in range(2):
        for side in range(2):
            jj[t * LANES + side * HEAD_DIM:t * LANES + (side + 1) * HEAD_DIM,
               side * HEAD_DIM:(side + 1) * HEAD_DIM] = 1.0 / HEAD_DIM
    return dict(tril3=jnp.asarray(np.concatenate([tril] * SPLIT_TERMS, axis=1), bf16),
                maskadd=maskadd, dmat=dmat.astype(np.float32),
                dmat2=np.concatenate(list(dmat.astype(np.float32).reshape(PAIRS, 2 * ROWS, ROWS)), axis=1),
                rc=rc.astype(np.float32), rslab=rslab.astype(np.float32),
                full=tuple(float(v) for v in full), causal=causal, tau=tau,
                selh=jnp.asarray(selh, bf16), selp=jnp.asarray(selp, bf16), jj=jnp.asarray(jj, bf16),
                eye=jnp.asarray(np.eye(ROWS, dtype=np.float32), bf16))


def _bias_vectors(rel_table):
    tb = jnp.transpose(rel_table[_t5_bucket(np.arange(WINDOW))]).astype(f32)
    ninf = jnp.full((A_HEADS, WINDOW), -jnp.inf, f32)
    rev = tb[:, :0:-1]
    return jnp.concatenate([tb[:, :1], ninf, rev, ninf[:, :1], rev, ninf], axis=1)


def _skew(u_row, rows):
    x = jnp.broadcast_to(u_row, (rows, 2 * WINDOW))
    return pltpu.roll(x, 0, 1, stride=1, stride_axis=0)[:, :WINDOW]


def _rope_tables(pos):
    half = HEAD_DIM // 2
    inv = ROPE_BASE ** (-jnp.arange(half, dtype=f32) / half)
    ang = pos.astype(f32)[:, None] * inv[None, :]
    cos, sin = jnp.cos(ang), jnp.sin(ang)
    reps = LANES // HEAD_DIM
    cos_t = jnp.tile(jnp.concatenate([cos, cos], axis=1), (1, reps))
    sin_t = jnp.tile(jnp.concatenate([-sin, sin], axis=1), (1, reps))
    return cos_t, sin_t


def _const_spec(shape, nargs):
    zeros = (0,) * len(shape)
    if nargs == 1:
        return pl.BlockSpec(shape, lambda i: zeros)
    return pl.BlockSpec(shape, lambda i, j: zeros)


def _layer_spec(shape, layer, nargs):
    idx = (layer,) + (0,) * len(shape)
    if nargs == 1:
        return pl.BlockSpec((None,) + shape, lambda i: idx)
    return pl.BlockSpec((None,) + shape, lambda i, j: idx)


def _param_specs(layer, nargs):
    ls = functools.partial(_layer_spec, layer=layer, nargs=nargs)
    return [ls((1, D_MODEL)), ls((P_COLS, D_MODEL)), ls((1, LANES)), ls((1, M_DIM)), ls((1, R_DIM)),
            pl.BlockSpec(memory_space=pltpu.SMEM), ls((D_MODEL, D_MODEL)), _const_spec((1, D_MODEL), nargs)]


def _param_args(p, layer):
    return (p["norm_gain"], p["w_in"], p["gbias"], p["m_gain"], p["r_gain"], p["sinks"][layer], p["w_out"],
            p["fgain"])


def _prompt_layer(x, p, layer, tabs, ubias, cos_t, sin_t, final):
    B, T, _ = x.shape
    tb = min(PROMPT_ROWS, T)
    chunks = tb // ROWS
    nt = T // tb
    cs = functools.partial(_const_spec, nargs=2)
    in_specs = [pl.BlockSpec((None, tb, D_MODEL), lambda b, t: (b, t, 0))] + _param_specs(layer, 2) + [
        cs((ROWS, SPLIT_TERMS * ROWS)), cs((ROWS, ROWS)), cs((2 * ROWS, PAIRS * ROWS)),
        cs((3, PAIRS, ROWS, LANES)), cs((A_HEADS, 4 * WINDOW)),
        pl.BlockSpec((tb, LANES), lambda b, t: (t, 0)), pl.BlockSpec((tb, LANES), lambda b, t: (t, 0)),
        cs((M_HEADS * ROWS, SPLIT_TERMS * LANES)), cs((PAIRS * LANES, SPLIT_TERMS * LANES)),
    ]
    out_shape = (
        jax.ShapeDtypeStruct((B, T, D_MODEL), f32),
        jax.ShapeDtypeStruct((B, M_HEADS, HEAD_DIM, HEAD_DIM), f32),
        jax.ShapeDtypeStruct((B, M_HEADS, HEAD_DIM), f32),
        jax.ShapeDtypeStruct((B, ROWS, LANES), f32),
        jax.ShapeDtypeStruct((B, R_HEADS, HEAD_DIM, HEAD_DIM), f32),
        jax.ShapeDtypeStruct((B, WINDOW, A_KV_DIM), f32),
        jax.ShapeDtypeStruct((B, WINDOW, A_KV_DIM), f32),
    )
    out_specs = (
        pl.BlockSpec((None, tb, D_MODEL), lambda b, t: (b, t, 0)),
        pl.BlockSpec((1, M_HEADS, HEAD_DIM, HEAD_DIM), lambda b, t: (b, 0, 0, 0)),
        pl.BlockSpec((1, M_HEADS, HEAD_DIM), lambda b, t: (b, 0, 0)),
        pl.BlockSpec((1, ROWS, LANES), lambda b, t: (b, 0, 0)),
        pl.BlockSpec((1, R_HEADS, HEAD_DIM, HEAD_DIM), lambda b, t: (b, 0, 0, 0)),
        pl.BlockSpec((1, WINDOW, A_KV_DIM), lambda b, t: (b, 0, 0)),
        pl.BlockSpec((1, WINDOW, A_KV_DIM), lambda b, t: (b, 0, 0)),
    )
    kern = functools.partial(_staged_prompt_kernel, chunks=chunks, final=final)
    y, c, n, m, s, k, v = pl.pallas_call(
        kern, grid=(B, nt), in_specs=in_specs, out_specs=out_specs, out_shape=out_shape,
        scratch_shapes=[pltpu.VMEM((tb, P_COLS), f32), pltpu.VMEM((tb, D_MODEL), bf16),
                        pltpu.VMEM((PAIRS, ROWS, 2 * LANES), f32), pltpu.VMEM((PAIRS, ROWS, LANES), f32),
                        pltpu.VMEM((ROWS, LANES), f32), pltpu.VMEM((ROWS, A_KV_DIM), f32),
                        pltpu.VMEM((ROWS, A_KV_DIM), f32), pltpu.VMEM((A_HEADS * ROWS, 2 * WINDOW), f32)],
        compiler_params=pltpu.CompilerParams(dimension_semantics=("arbitrary", "arbitrary"),
                                             vmem_limit_bytes=VMEM_LIMIT_BYTES),
        name="prompt_layer",
    )(x, *_param_args(p, layer), tabs["tril3"], tabs["maskadd"], tabs["dmat2"], tabs["rslab"], ubias,
      cos_t, sin_t, tabs["selh"], tabs["selp"])
    k = k.reshape(B, WINDOW, A_KV_HEADS, HEAD_DIM)
    v = v.reshape(B, WINDOW, A_KV_HEADS, HEAD_DIM)
    return y, c, n, m[:, 0, :M_HEADS], s, k, v


def _sample_path(x, states, p, tabs, ubias, cos_t, sin_t):
    B, T, _ = x.shape
    groups = ROWS // T
    nb = B // groups
    c0, n0, m0, s0, k0, v0 = states
    depth = c0.shape[0]
    x2 = x.reshape(B * T, D_MODEL)
    n0t = jnp.transpose(n0, (0, 2, 1, 3))
    m0r = jnp.pad(jnp.repeat(m0, T, axis=1), ((0, 0), (0, 0), (0, LANES - M_HEADS)))
    k0r = jnp.transpose(k0, (0, 1, 3, 4, 2)).reshape(depth, B, A_KV_DIM, WINDOW)
    v0r = jnp.transpose(v0, (0, 1, 3, 4, 2)).reshape(depth, B, A_KV_DIM, WINDOW)

    def cs(shape):
        zeros = (0,) * len(shape)
        return pl.BlockSpec(shape, lambda i, l: zeros)

    def per_layer(shape):
        zeros = (0,) * len(shape)
        return pl.BlockSpec((None,) + shape, lambda i, l: (l,) + zeros)

    st4 = pl.BlockSpec((None, groups, M_HEADS, HEAD_DIM, HEAD_DIM), lambda i, l: (l, i, 0, 0, 0))
    stn = pl.BlockSpec((None, M_HEADS, groups, HEAD_DIM), lambda i, l: (l, 0, i, 0))
    stm = pl.BlockSpec((None, ROWS, LANES), lambda i, l: (l, i, 0))
    stk = pl.BlockSpec((None, groups, A_KV_DIM, WINDOW), lambda i, l: (l, i, 0, 0))
    rows_spec = pl.BlockSpec((ROWS, D_MODEL), lambda i, l: (i, 0))
    in_specs = [
        rows_spec,
        per_layer((1, D_MODEL)), per_layer((P_COLS, D_MODEL)), per_layer((1, LANES)), per_layer((1, M_DIM)),
        per_layer((1, R_DIM)), pl.BlockSpec(memory_space=pltpu.SMEM), per_layer((D_MODEL, D_MODEL)),
        cs((1, D_MODEL)),
        cs((ROWS, SPLIT_TERMS * ROWS)), cs((ROWS, ROWS)), cs((R_HEADS, ROWS, ROWS)), cs((ROWS, LANES)),
        cs((A_HEADS, 4 * WINDOW)),
        cs((ROWS, LANES)), cs((ROWS, LANES)),
        st4, stn, stm, st4, stk, stk,
    ]
    out_shape = (
        jax.ShapeDtypeStruct((B * T, D_MODEL), f32),
        jax.ShapeDtypeStruct((depth, B, M_HEADS, HEAD_DIM, HEAD_DIM), f32),
        jax.ShapeDtypeStruct((depth, M_HEADS, B, HEAD_DIM), f32),
        jax.ShapeDtypeStruct((depth, B * T, LANES), f32),
        jax.ShapeDtypeStruct((depth, B, R_HEADS, HEAD_DIM, HEAD_DIM), f32),
        jax.ShapeDtypeStruct((depth, B, A_KV_DIM, WINDOW), f32),
        jax.ShapeDtypeStruct((depth, B, A_KV_DIM, WINDOW), f32),
    )
    out_specs = (rows_spec, st4, stn, stm, st4, stk, stk)
    kern = functools.partial(_sample_kernel, groups=groups, ret_full=tabs["full"])
    y, c, n, m, s, k, v = pl.pallas_call(
        kern, grid=(nb, depth), in_specs=in_specs, out_specs=out_specs, out_shape=out_shape,
        scratch_shapes=[pltpu.VMEM((ROWS, P_COLS), f32), pltpu.VMEM((ROWS, D_MODEL), bf16),
                        pltpu.VMEM((ROWS, D_MODEL), f32),
                        pltpu.VMEM((groups, A_HEADS * T, A_KV_DIM), bf16),
                        pltpu.VMEM((A_HEADS, ROWS, WINDOW), f32),
                        pltpu.VMEM((groups, A_HEADS * T, WINDOW), bf16),
                        pltpu.VMEM((A_HEADS, ROWS, A_KV_DIM), f32),
                        pltpu.VMEM((A_HEADS, ROWS, ROWS), f32), pltpu.VMEM((A_HEADS, ROWS, WINDOW), f32)],
        compiler_params=pltpu.CompilerParams(dimension_semantics=("arbitrary", "arbitrary"),
                                             vmem_limit_bytes=VMEM_LIMIT_BYTES),
        name="sample_path",
    )(x2, p["norm_gain"], p["w_in"], p["gbias"], p["m_gain"], p["r_gain"], p["sinks"], p["w_out"], p["fgain"],
      tabs["tril3"], tabs["maskadd"], tabs["dmat"], tabs["rc"], ubias,
      cos_t, sin_t, c0, n0t, m0r, s0, k0r, v0r)
    y = y.reshape(B, T, D_MODEL)
    n = jnp.transpose(n, (0, 2, 1, 3))
    m = m.reshape(depth, B, T, LANES)[:, :, 0, :M_HEADS]
    k = jnp.transpose(k.reshape(depth, B, A_KV_HEADS, HEAD_DIM, WINDOW), (0, 1, 4, 2, 3))
    v = jnp.transpose(v.reshape(depth, B, A_KV_HEADS, HEAD_DIM, WINDOW), (0, 1, 4, 2, 3))
    return y, c, n, m, s, k, v


def _prepare_params(norm_gain, w_in, mlstm_gate_bias, mlstm_norm_gain, ret_norm_gain, attn_sinks, w_out,
                    final_norm_gain):
    depth = w_in.shape[0]
    w_t = jnp.swapaxes(w_in, 1, 2)
    split = OFF_G + N_GATES
    aq0 = split + 4 * R_DIM
    akv0 = aq0 + A_DIM
    az0 = akv0 + 2 * A_KV_DIM

    def by_head(w):
        w = w.reshape(depth, A_HEADS, HEAD_DIM, D_MODEL)
        return jnp.concatenate([w[:, h] for h in ATTN_HEAD_ORDER], axis=1)

    w_in_p = jnp.concatenate(
        [w_t[:, :split], jnp.zeros((depth, GATE_PAD - N_GATES, D_MODEL), w_t.dtype), w_t[:, split:aq0],
         by_head(w_t[:, aq0:akv0]), w_t[:, akv0:az0], by_head(w_t[:, az0:])], axis=1).astype(bf16)
    wo16 = w_out.astype(bf16)
    a0 = M_DIM + R_DIM
    w_out_p = jnp.concatenate([wo16[:, :a0, :], by_head(wo16[:, a0:, :])], axis=1)
    gbias = jnp.pad(mlstm_gate_bias.reshape(depth, 1, N_GATES), ((0, 0), (0, 0), (0, LANES - N_GATES)))
    return dict(norm_gain=norm_gain.reshape(depth, 1, D_MODEL), w_in=w_in_p, gbias=gbias,
                m_gain=mlstm_norm_gain.reshape(depth, 1, M_DIM), r_gain=ret_norm_gain.reshape(depth, 1, R_DIM),
                sinks=attn_sinks, w_out=w_out_p, fgain=final_norm_gain.reshape(1, D_MODEL))


def kernel(x_prompt, x_sample, state_mlstm_C, state_mlstm_n, state_mlstm_m, state_ret_S, cache_win_k,
           cache_win_v, norm_gain, w_in, mlstm_gate_bias, mlstm_norm_gain, ret_norm_gain, attn_sinks,
           rel_bias_table, w_out, final_norm_gain):
    depth = w_in.shape[0]
    seq = x_prompt.shape[1]
    dec_seq = x_sample.shape[1]
    past_len = seq
    p = _prepare_params(norm_gain, w_in, mlstm_gate_bias, mlstm_norm_gain, ret_norm_gain, attn_sinks, w_out,
                        final_norm_gain)
    tabs_p = _static_tables(1)
    tabs_s = _static_tables(ROWS // dec_seq)
    ubias = _bias_vectors(rel_bias_table)
    cos_p, sin_p = _rope_tables(jnp.arange(seq, dtype=jnp.int32))
    cos_s, sin_s = _rope_tables(past_len + (jnp.arange(ROWS, dtype=jnp.int32) % dec_seq))
    states = (state_mlstm_C, state_mlstm_n, state_mlstm_m, state_ret_S, cache_win_k, cache_win_v)

    xp = x_prompt
    p_states = []
    for layer in range(depth):
        xp, *sp = _prompt_layer(xp, p, layer, tabs_p, ubias, cos_p, sin_p, layer == depth - 1)
        p_states.append(sp)
    outs_p = [jnp.stack([p_states[l][i] for l in range(depth)]) for i in range(6)]
    xs, *outs_s = _sample_path(x_sample, states, p, tabs_s, ubias, cos_s, sin_s)
    return (xp, xs, *outs_p, *outs_s)
```

```python
import functools

import numpy as np
import jax
import jax.numpy as jnp
from jax import lax
from jax.experimental import pallas as pl
from jax.experimental.pallas import tpu as pltpu

D_MODEL = 1024
HEAD_DIM = 64
M_HEADS = 4
R_HEADS = 4
A_HEADS = 8
A_KV_HEADS = 2
KV_GROUP = A_HEADS // A_KV_HEADS
M_DIM = M_HEADS * HEAD_DIM
R_DIM = R_HEADS * HEAD_DIM
A_DIM = A_HEADS * HEAD_DIM
A_KV_DIM = A_KV_HEADS * HEAD_DIM
WINDOW = 128
N_BUCKETS = 32
REL_MAX_DIST = 128
ROPE_BASE = 10000.0
NORM_EPS = 1e-6
QK_SCALE = HEAD_DIM ** -0.5

LANES = 128
ROWS = 128
GATE_PAD = LANES
PAIRS = M_HEADS // 2
SPLIT_TERMS = 3

OFF_MQ = 0
OFF_MK = OFF_MQ + M_DIM
OFF_MV = OFF_MK + M_DIM
OFF_MO = OFF_MV + M_DIM
OFF_MZ = OFF_MO + M_DIM
OFF_G = OFF_MZ + M_DIM
OFF_RQ = OFF_G + GATE_PAD
OFF_RK = OFF_RQ + R_DIM
OFF_RV = OFF_RK + R_DIM
OFF_RZ = OFF_RV + R_DIM
OFF_AQ = OFF_RZ + R_DIM
OFF_AK = OFF_AQ + A_DIM
OFF_AV = OFF_AK + A_KV_DIM
OFF_AZ = OFF_AV + A_KV_DIM
P_COLS = OFF_AZ + A_DIM
N_GATES = 2 * M_HEADS
PROJ_COL_BLOCK = 512
SCHEDULE = "FBFBFB"
ATTN_HEAD_ORDER = tuple(h for j in range(KV_GROUP) for h in (j, KV_GROUP + j))
ATTN_HEAD_POS = tuple(ATTN_HEAD_ORDER.index(h) for h in range(A_HEADS))

PROMPT_ROWS = 512
VMEM_LIMIT_BYTES = 56 * 1024 * 1024

f32 = jnp.float32
bf16 = jnp.bfloat16


def _dot(a, b):
    return jnp.dot(a, b, preferred_element_type=f32)


def _dot_nt(a, b):
    return lax.dot_general(a, b, (((1,), (1,)), ((), ())), preferred_element_type=f32)


def _sigmoid(x):
    return 1.0 / (1.0 + jnp.exp(-x))


def _silu(x):
    return x * _sigmoid(x)


def _log_sigmoid(x):
    return jnp.minimum(x, 0.0) - jnp.log(1.0 + jnp.exp(-jnp.abs(x)))


def _split_parts(x, terms):
    parts, r = [], x
    for i in range(terms):
        p = r.astype(bf16)
        parts.append(p)
        if i + 1 < terms:
            r = r - p.astype(f32)
    return parts


def _split_terms(x, terms=SPLIT_TERMS):
    return jnp.concatenate(_split_parts(x, terms), axis=1)


def _exact_tril_dot(tril3, x):
    return _dot(tril3, jnp.concatenate(_split_parts(x, SPLIT_TERMS), axis=0))


def _rope(x, cos_t, sin_t, first_half):
    up = pltpu.roll(x, LANES - HEAD_DIM // 2, axis=1)
    down = pltpu.roll(x, HEAD_DIM // 2, axis=1)
    return x * cos_t + jnp.where(first_half, up, down) * sin_t


def _rms_project(x_ref, ngain_ref, win_ref, proj_sc):
    xf = x_ref[...]
    u = xf * lax.rsqrt(jnp.mean(xf * xf, axis=1, keepdims=True) + NORM_EPS) * ngain_ref[...]
    u16 = u.astype(bf16)
    for c0 in range(0, P_COLS, PROJ_COL_BLOCK):
        c1 = min(c0 + PROJ_COL_BLOCK, P_COLS)
        proj_sc[:, c0:c1] = _dot_nt(u16, win_ref[c0:c1, :])


def _prompt_kernel(x_ref, ngain_ref, win_ref, gbias_ref, mgain_ref, rgain_ref, sinks_ref, wout_ref,
                          fgain_ref, tril3_ref, maskadd_ref, dmat2_ref, rslab_ref, ubias_ref, cos_ref,
                          sin_ref, selh_ref, selp_ref,
                          y_ref, c_out, n_out, m_out, s_out, k_out, v_out,
                          proj_sc, mix_sc, cn_sc, sb_sc, m_sc, kp_sc, vp_sc, bias_sc, *, chunks, final):
    step = pl.program_id(1)
    last_step = pl.num_programs(1) - 1

    @pl.when(jnp.logical_and(pl.program_id(0) == 0, step == 0))
    def _():
        for blk in range(A_HEADS):
            u = ubias_ref[ATTN_HEAD_ORDER[blk]:ATTN_HEAD_ORDER[blk] + 1, :]
            bias_sc[blk * ROWS:(blk + 1) * ROWS, :] = jnp.concatenate(
                [_skew(u[:, :2 * WINDOW], ROWS), _skew(u[:, 2 * WINDOW:], ROWS)], axis=1)

    xf = x_ref[...]
    u16 = (xf * lax.rsqrt(jnp.mean(xf * xf, axis=1, keepdims=True) + NORM_EPS) * ngain_ref[...]).astype(bf16)
    col_blocks = [(c0, min(c0 + PROJ_COL_BLOCK, P_COLS)) for c0 in range(0, P_COLS, PROJ_COL_BLOCK)]
    half_rows = (chunks // 2) * ROWS if chunks > 1 else chunks * ROWS

    def project(r0, r1, c0, c1):
        proj_sc[r0:r1, c0:c1] = _dot_nt(u16[r0:r1], win_ref[c0:c1, :])

    for c0, c1 in col_blocks:
        project(0, half_rows, c0, c1)
    late_pieces = [(half_rows, chunks * ROWS, c0, c1) for c0, c1 in col_blocks] if half_rows < chunks * ROWS else []

    @pl.when(step == 0)
    def _():
        cn_sc[...] = jnp.zeros_like(cn_sc)
        sb_sc[...] = jnp.zeros_like(sb_sc)
        m_sc[...] = jnp.zeros_like(m_sc)
        kp_sc[...] = jnp.zeros_like(kp_sc)
        vp_sc[...] = jnp.zeros_like(vp_sc)

    lane = lax.broadcasted_iota(jnp.int32, (ROWS, LANES), 1)
    row = lax.broadcasted_iota(jnp.int32, (ROWS, LANES), 0)
    left = lane < HEAD_DIM
    first_half = (lane & (HEAD_DIM - 1)) < (HEAD_DIM // 2)
    head_col = lane < M_HEADS
    blockdiag = (row < HEAD_DIM) == left
    row2 = lax.broadcasted_iota(jnp.int32, (ROWS, 2 * LANES), 0)
    lane2w = lax.broadcasted_iota(jnp.int32, (ROWS, 2 * LANES), 1)
    left2 = (lane2w & (LANES - 1)) < HEAD_DIM
    blockdiag2 = (row2 < HEAD_DIM) == left2
    ones16 = jnp.ones((ROWS, LANES), bf16)

    def halves(x):
        return jnp.concatenate([jnp.where(left, x, 0.0), jnp.where(left, 0.0, x)], axis=0).astype(bf16)

    def pick(x, mask):
        return jnp.where(mask, x[:ROWS], x[ROWS:])

    def pair_blockdiag(blocks):
        z = jnp.zeros_like(blocks[0])
        return jnp.concatenate(
            [jnp.concatenate([blk if j == i else z for j in range(len(blocks))], axis=1)
             for i, blk in enumerate(blocks)], axis=0)

    def half_mean(x):
        lane_n = lax.broadcasted_iota(jnp.int32, x.shape, 1)
        is_left = lane_n < HEAD_DIM
        s_left = jnp.sum(jnp.where(is_left, x, 0.0), axis=1, keepdims=True)
        s_right = jnp.sum(jnp.where(is_left, 0.0, x), axis=1, keepdims=True)
        return jnp.where(is_left, s_left, s_right) * (1.0 / HEAD_DIM)

    def chunk_body(ci):
        rows = slice(ci * ROWS, (ci + 1) * ROWS)

        def proj(off, width=LANES):
            return proj_sc[rows, off:off + width]

        gates = proj(OFF_G) + gbias_ref[...]
        bcum = _exact_tril_dot(tril3_ref[...], _log_sigmoid(gates))

        cos_t, sin_t = cos_ref[rows, :], sin_ref[rows, :]
        r_q = [_rope(proj(OFF_RQ + p * LANES), cos_t, sin_t, first_half) for p in range(PAIRS)]
        r_k = [_rope(proj(OFF_RK + p * LANES), cos_t, sin_t, first_half) * QK_SCALE for p in range(PAIRS)]
        r_vbd = pair_blockdiag([proj(OFF_RV + p * LANES).astype(bf16) for p in range(PAIRS)])
        r_sb = [sb_sc[p] for p in range(PAIRS)]
        r_sc = _dot_nt(jnp.concatenate([halves(q) for q in r_q], axis=1),
                       pair_blockdiag([k.astype(bf16) for k in r_k]))
        r_inter = _dot(jnp.concatenate([q.astype(bf16) for q in r_q], axis=1),
                       pair_blockdiag([sb.astype(bf16) for sb in r_sb]))
        r_upd = _dot(jnp.concatenate([(r_k[p] * rslab_ref[1, p]).T.astype(bf16) for p in range(PAIRS)], axis=1),
                     r_vbd)
        for p in range(PAIRS):
            sb_sc[p] = rslab_ref[2, p] * r_sb[p] + jnp.where(blockdiag, r_upd[:, p * LANES:(p + 1) * LANES], 0.0)
        yield

        kcur, vcur = proj(OFF_AK), proj(OFF_AV)
        kprev, vprev = kp_sc[...], vp_sc[...]
        kk16 = jnp.concatenate([kcur, kprev], axis=0).astype(bf16)
        vv16 = jnp.concatenate([jnp.concatenate([vcur.astype(bf16), ones16], axis=1),
                                jnp.concatenate([vprev.astype(bf16), ones16], axis=1)], axis=0)
        kp_sc[...] = kcur
        vp_sc[...] = vcur
        a_q = jnp.concatenate([halves(proj(OFF_AQ + j * LANES) * QK_SCALE) for j in range(KV_GROUP)], axis=0)
        a_s = _dot_nt(a_q, kk16) + bias_sc[...]
        if ci == 0:
            pen = jnp.where(step == 0, -jnp.inf, 0.0).astype(f32)
            a_s = a_s + jnp.where(lax.broadcasted_iota(jnp.int32, (1, 2 * ROWS), 1) >= ROWS, pen, 0.0)

        m_q = [proj(OFF_MQ + p * LANES) for p in range(PAIRS)]
        m_k = [proj(OFF_MK + p * LANES) * QK_SCALE for p in range(PAIRS)]
        m_qk = _dot_nt(jnp.concatenate([halves(q) for q in m_q], axis=1),
                       pair_blockdiag([k.astype(bf16) for k in m_k]))
        m_q16 = [q.astype(bf16) for q in m_q]
        m_v16 = [proj(OFF_MV + p * LANES).astype(bf16) for p in range(PAIRS)]
        yield

        zb = pltpu.roll(bcum, LANES - M_HEADS, axis=1)
        r_mat = jnp.where(head_col, gates - zb, 0.0)
        cm = r_mat
        sh = 1
        while sh < ROWS:
            cm = jnp.where(row >= sh, jnp.maximum(cm, pltpu.roll(cm, sh, axis=0)), cm)
            sh *= 2
        mprev = m_sc[...]
        mx = jnp.maximum(mprev, cm)
        gm = mprev - mx
        em = jnp.where(head_col, -(zb + mx), 0.0)
        mx_last = jnp.broadcast_to(mx[ROWS - 1:ROWS, :], (ROWS, LANES))
        m_sc[...] = jnp.where(head_col, jnp.broadcast_to((zb + mx)[ROWS - 1:ROWS, :], (ROWS, LANES)), 0.0)
        mx_b = _dot_nt(_split_terms(mx), selh_ref[...])
        slabs = jnp.exp(_dot_nt(_split_terms(jnp.concatenate([gm, em, r_mat - mx_last], axis=0)),
                                selp_ref[...]))
        winter_b, emt_b, ws_b = slabs[:ROWS], slabs[ROWS:2 * ROWS], slabs[2 * ROWS:]
        r_t = r_mat.T
        yield

        outs = []
        r_acc = []
        r_o = _dot((r_sc * dmat2_ref[...]).astype(bf16), r_vbd)
        for p in range(PAIRS):
            ps = slice(p * LANES, (p + 1) * LANES)
            r_acc.append(pick(r_o[:, ps], left) + rslab_ref[0, p] * r_inter[:, ps])

        a_out = []
        a_p = []
        for blk in range(A_HEADS):
            s = a_s[blk * ROWS:(blk + 1) * ROWS]
            sink = sinks_ref[ATTN_HEAD_ORDER[blk]]
            m = jnp.maximum(jnp.max(jnp.maximum(s[:, :ROWS], s[:, ROWS:]), axis=1, keepdims=True), sink)
            a_p.append(jnp.exp(s - m).astype(bf16))
            a_out.append(jnp.exp(sink - m))
        a_pv = _dot(jnp.concatenate(a_p, axis=0), vv16)
        yield

        maskadd = maskadd_ref[...]
        for p in range(PAIRS):
            ps = slice(p * LANES, (p + 1) * LANES)
            cn = cn_sc[p]
            w = jnp.concatenate(
                [jnp.exp((r_t[2 * p + side:2 * p + side + 1, :] + maskadd)
                         - mx_b[:, (2 * p + side) * ROWS:(2 * p + side + 1) * ROWS]) for side in range(2)],
                axis=0) * m_qk[:, ps]
            acc = (pick(_dot(w.astype(bf16), jnp.concatenate([m_v16[p], ones16], axis=1)), left2)
                   + jnp.concatenate([winter_b[:, ps]] * 2, axis=1) * _dot(m_q16[p], cn.astype(bf16)))
            hh = acc[:, :LANES] / jnp.maximum(jnp.abs(acc[:, LANES:]), emt_b[:, ps])
            outs.append(_sigmoid(proj(OFF_MO + p * LANES)) * hh)
            kwt16 = (m_k[p] * ws_b[:, ps]).T.astype(bf16)
            dcn = _dot(kwt16, jnp.concatenate([m_v16[p], ones16], axis=1))
            decay = winter_b[ROWS - 1:ROWS, ps]
            cn_sc[p] = jnp.concatenate([decay, decay], axis=1) * cn + jnp.where(blockdiag2, dcn, 0.0)
        outs.extend(r_acc)
        yield

        x4 = jnp.concatenate(outs, axis=0)
        xc = x4 - half_mean(x4)
        y4 = xc * lax.rsqrt(half_mean(xc * xc) + NORM_EPS)
        for i in range(2 * PAIRS):
            gain = (mgain_ref if i < PAIRS else rgain_ref)[:, (i % PAIRS) * LANES:(i % PAIRS + 1) * LANES]
            zoff = (OFF_MZ if i < PAIRS else OFF_RZ) + (i % PAIRS) * LANES
            out = y4[i * ROWS:(i + 1) * ROWS] * gain * _silu(proj(zoff))
            mix_sc[rows, i * LANES:(i + 1) * LANES] = out.astype(bf16)
        for j in range(KV_GROUP):
            acc = pick(a_pv[2 * j * ROWS:(2 * j + 2) * ROWS], left2)
            den = acc[:, LANES:] + jnp.where(left, a_out[2 * j], a_out[2 * j + 1])
            out = (acc[:, :LANES] / den) * _silu(proj(OFF_AZ + j * LANES))
            mix_sc[rows, M_DIM + R_DIM + j * LANES:M_DIM + R_DIM + (j + 1) * LANES] = out.astype(bf16)

    def out_project(r0, r1, c0, c1):
        y_ref[r0:r1, c0:c1] = x_ref[r0:r1, c0:c1] + _dot(mix_sc[r0:r1, :], wout_ref[:, c0:c1])

    out_blocks = [(c0, min(c0 + PROJ_COL_BLOCK, D_MODEL)) for c0 in range(0, D_MODEL, PROJ_COL_BLOCK)]
    early_out = [(0, half_rows, c0, c1) for c0, c1 in out_blocks] if half_rows < chunks * ROWS else []
    final_out = [(half_rows if early_out else 0, chunks * ROWS, c0, c1) for c0, c1 in out_blocks]

    def fill_mxu(ci):
        if late_pieces:
            project(*late_pieces.pop(0))
        elif early_out and (ci - 1) * ROWS >= half_rows:
            out_project(*early_out.pop(0))

    parts = [chunk_body(ci) for ci in range(chunks)]
    for ci in range(chunks + 1):
        if ci * ROWS >= half_rows:
            while late_pieces:
                project(*late_pieces.pop(0))
        for which in SCHEDULE:
            if which == "F" and ci < chunks:
                next(parts[ci])
                fill_mxu(ci)
            if which == "B" and ci > 0:
                next(parts[ci - 1], None)
                fill_mxu(ci)
    for piece in early_out + final_out:
        out_project(*piece)
    if final:
        y = y_ref[...]
        y_ref[...] = y * lax.rsqrt(jnp.mean(y * y, axis=1, keepdims=True) + NORM_EPS) * fgain_ref[...]

    @pl.when(step == last_step)
    def _():
        for p in range(PAIRS):
            cn = cn_sc[p]
            sb = sb_sc[p]
            n_t = cn[:, LANES:].T
            for side in range(2):
                h = 2 * p + side
                blk = slice(side * HEAD_DIM, (side + 1) * HEAD_DIM)
                c_out[0, h] = cn[blk, blk]
                s_out[0, h] = sb[blk, blk]
                n_out[0, h:h + 1, :] = n_t[side * HEAD_DIM:side * HEAD_DIM + 1, blk]
        m_out[0] = m_sc[...]
        k_out[0] = kp_sc[...]
        v_out[0] = vp_sc[...]


def _lane_is_left(shape):
    return (lax.broadcasted_iota(jnp.int32, shape, 1) & (LANES - 1)) < HEAD_DIM


def _halves(x):
    left = _lane_is_left(x.shape)
    return jnp.concatenate([jnp.where(left, x, 0.0), jnp.where(left, 0.0, x)], axis=0).astype(bf16)


def _pick(x):
    return jnp.where(_lane_is_left((ROWS, x.shape[1])), x[:ROWS], x[ROWS:])


def _pair_blockdiag(blocks):
    z = jnp.zeros_like(blocks[0])
    return jnp.concatenate(
        [jnp.concatenate([blk if j == i else z for j in range(len(blocks))], axis=1)
         for i, blk in enumerate(blocks)], axis=0)


def _half_mean(x):
    left = _lane_is_left(x.shape)
    s_left = jnp.sum(jnp.where(left, x, 0.0), axis=1, keepdims=True)
    s_right = jnp.sum(jnp.where(left, 0.0, x), axis=1, keepdims=True)
    return jnp.where(left, s_left, s_right) * (1.0 / HEAD_DIM)


def _group_last(x, groups):
    n = x.shape[1]
    glen = ROWS // groups
    x3 = x.reshape(groups, glen, n)
    return jnp.broadcast_to(x3[:, glen - 1:glen, :], (groups, glen, n)).reshape(ROWS, n)


def _sample_kernel(x_ref, ngain_ref, win_ref, gbias_ref, mgain_ref, rgain_ref, sinks_ref, wout_ref,
                   fgain_ref, tril3_ref, maskadd_ref, dmat2_ref, rslab_ref, ubias_ref, cos_ref, sin_ref,
                   selh_ref, selp_ref,
                   c_in, n_in, m_in, s_in, k_in, v_in,
                   y_ref, c_out, n_out, m_out, s_out, k_out, v_out,
                   proj_sc, mix_sc, xcur_sc, qb_sc, sp_sc, pp_sc, ob_sc, biasc_ref, biasp_ref, *, groups,
                   ret_full):
    glen = ROWS // groups
    glen_log2 = glen.bit_length() - 1
    hd_log2 = HEAD_DIM.bit_length() - 1
    step = pl.program_id(0)
    layer = pl.program_id(1)
    last_layer = pl.num_programs(1) - 1

    @pl.when(layer == 0)
    def _():
        xcur_sc[...] = x_ref[...]

    _rms_project(xcur_sc, ngain_ref, win_ref, proj_sc)

    @pl.when(jnp.logical_and(step == 0, layer == 0))
    def _():
        qb_sc[...] = jnp.zeros_like(qb_sc)
        for h in range(A_HEADS):
            u = ubias_ref[h:h + 1, :]
            biasc_ref[h] = _skew(u[:, :2 * WINDOW], ROWS) + maskadd_ref[...]
            biasp_ref[h] = jnp.concatenate([_skew(u[:, 2 * WINDOW:], glen)] * groups, axis=0)

    lane = lax.broadcasted_iota(jnp.int32, (ROWS, LANES), 1)
    first_half = (lane & (HEAD_DIM - 1)) < (HEAD_DIM // 2)
    rows = slice(0, ROWS)

    r_i = lax.broadcasted_iota(jnp.int32, (ROWS, groups * HEAD_DIM), 0)
    c_i = lax.broadcasted_iota(jnp.int32, (ROWS, groups * HEAD_DIM), 1)
    blk = (r_i >> glen_log2) == (c_i >> hd_log2)
    r_t = lax.broadcasted_iota(jnp.int32, (groups * HEAD_DIM, ROWS), 0)
    c_t = lax.broadcasted_iota(jnp.int32, (groups * HEAD_DIM, ROWS), 1)
    blk_t = (r_t >> hd_log2) == (c_t >> glen_log2)

    def q_times_state(qh, st):
        qt = jnp.where(blk, jnp.concatenate([qh] * groups, axis=1), 0.0)
        return _dot(qt.astype(bf16), st.astype(bf16))

    def state_increment(kt_h, vh16):
        kt = jnp.where(blk_t, jnp.concatenate([kt_h] * groups, axis=0), 0.0)
        return _dot(kt.astype(bf16), vh16)

    def proj(off, width=LANES):
        return proj_sc[:, off:off + width]

    def head_state_rows(slab, side):
        wide = slab[:, side * HEAD_DIM:(side + 1) * HEAD_DIM].reshape(groups, glen, HEAD_DIM)[:, 0:1, :]
        rows_ = jnp.broadcast_to(wide, (groups, HEAD_DIM, HEAD_DIM)).reshape(groups * HEAD_DIM, HEAD_DIM)
        return rows_, wide.reshape(groups, HEAD_DIM)

    row = lax.broadcasted_iota(jnp.int32, (ROWS, LANES), 0)
    tau = row & (glen - 1)
    head_col = lane < M_HEADS
    left = lane < HEAD_DIM
    ones16 = jnp.ones((ROWS, LANES), bf16)
    gates = proj(OFF_G) + gbias_ref[...]
    bcum = _exact_tril_dot(tril3_ref[...], _log_sigmoid(gates))
    zb = pltpu.roll(bcum, LANES - M_HEADS, axis=1)
    r_mat = jnp.where(head_col, gates - zb, 0.0)
    cm = r_mat
    sh = 1
    while sh < glen:
        cm = jnp.where(tau >= sh, jnp.maximum(cm, pltpu.roll(cm, sh, axis=0)), cm)
        sh *= 2
    mprev = m_in[...]
    mx = jnp.maximum(mprev, cm)
    gm = mprev - mx
    em = jnp.where(head_col, -(zb + mx), 0.0)
    mx_last = _group_last(mx, groups)
    m_out[...] = jnp.where(head_col, _group_last(zb + mx, groups), 0.0)
    mx_b = _dot_nt(_split_terms(mx), selh_ref[...])
    slabs = jnp.exp(_dot_nt(_split_terms(jnp.concatenate([gm, em, r_mat - mx_last], axis=0)), selp_ref[...]))
    winter_b, emt_b, ws_b = slabs[:ROWS], slabs[ROWS:2 * ROWS], slabs[2 * ROWS:]
    decay_b = _group_last(winter_b, groups)
    r_t = r_mat.T
    maskadd = maskadd_ref[...]

    m_q = [proj(OFF_MQ + p * LANES) for p in range(PAIRS)]
    m_k = [proj(OFF_MK + p * LANES) * QK_SCALE for p in range(PAIRS)]
    m_v = [proj(OFF_MV + p * LANES) for p in range(PAIRS)]
    m_qk = _dot_nt(jnp.concatenate([_halves(q) for q in m_q], axis=1),
                   _pair_blockdiag([k.astype(bf16) for k in m_k]))
    outs = []
    for p in range(PAIRS):
        ps = slice(p * LANES, (p + 1) * LANES)
        w = jnp.concatenate(
            [jnp.exp((r_t[2 * p + side:2 * p + side + 1, :] + maskadd)
                     - mx_b[:, (2 * p + side) * ROWS:(2 * p + side + 1) * ROWS]) for side in range(2)],
            axis=0) * m_qk[:, ps]
        intra = _pick(_dot(w.astype(bf16), jnp.concatenate([m_v[p].astype(bf16), ones16], axis=1)))
        kw = m_k[p] * ws_b[:, ps]
        kwt = kw.T
        q_c, q_n = [], []
        for side in range(2):
            h = 2 * p + side
            hs = slice(side * HEAD_DIM, (side + 1) * HEAD_DIM)
            qh = m_q[p][:, hs]
            c_h = c_in[:, h].reshape(groups * HEAD_DIM, HEAD_DIM)
            n_g = n_in[h]
            n_rows = jnp.broadcast_to(n_g.reshape(groups, 1, HEAD_DIM),
                                      (groups, glen, HEAD_DIM)).reshape(ROWS, HEAD_DIM)
            q_c.append(q_times_state(qh, c_h))
            q_n.append(jnp.sum(qh * n_rows, axis=1, keepdims=True))
            dec_rows, dec_g = head_state_rows(decay_b[:, ps], side)
            c_new = dec_rows * c_h + state_increment(kwt[hs, :], m_v[p][:, hs].astype(bf16))
            c_out[:, h] = c_new.reshape(groups, HEAD_DIM, HEAD_DIM)
            n_out[h] = dec_g * n_g + jnp.sum(kw[:, hs].reshape(groups, glen, HEAD_DIM), axis=1)
        wb = winter_b[:, ps]
        num = intra[:, :LANES] + wb * jnp.concatenate(q_c, axis=1)
        nq = intra[:, LANES:] + wb * jnp.where(left, q_n[0], q_n[1])
        outs.append(_sigmoid(proj(OFF_MO + p * LANES)) * (num / jnp.maximum(jnp.abs(nq), emt_b[:, ps])))

    cos_t, sin_t = cos_ref[...], sin_ref[...]
    r_q = [_rope(proj(OFF_RQ + p * LANES), cos_t, sin_t, first_half) for p in range(PAIRS)]
    r_k = [_rope(proj(OFF_RK + p * LANES), cos_t, sin_t, first_half) * QK_SCALE for p in range(PAIRS)]
    r_v = [proj(OFF_RV + p * LANES) for p in range(PAIRS)]
    r_sc = _dot_nt(jnp.concatenate([_halves(q) for q in r_q], axis=1),
                   _pair_blockdiag([k.astype(bf16) for k in r_k]))
    r_o = _dot((r_sc * dmat2_ref[...]).astype(bf16), _pair_blockdiag([v.astype(bf16) for v in r_v]))
    for p in range(PAIRS):
        ps = slice(p * LANES, (p + 1) * LANES)
        rkt = (r_k[p] * rslab_ref[1, p]).T
        q_s = []
        for side in range(2):
            h = 2 * p + side
            hs = slice(side * HEAD_DIM, (side + 1) * HEAD_DIM)
            s_h = s_in[:, h].reshape(groups * HEAD_DIM, HEAD_DIM)
            q_s.append(q_times_state(r_q[p][:, hs], s_h))
            s_new = ret_full[h] * s_h + state_increment(rkt[hs, :], r_v[p][:, hs].astype(bf16))
            s_out[:, h] = s_new.reshape(groups, HEAD_DIM, HEAD_DIM)
        outs.append(_pick(r_o[:, ps]) + rslab_ref[0, p] * jnp.concatenate(q_s, axis=1))

    x4 = jnp.concatenate(outs, axis=0)
    xc = x4 - _half_mean(x4)
    y4 = xc * lax.rsqrt(_half_mean(xc * xc) + NORM_EPS)
    for i in range(2 * PAIRS):
        gain = (mgain_ref if i < PAIRS else rgain_ref)[:, (i % PAIRS) * LANES:(i % PAIRS + 1) * LANES]
        zoff = (OFF_MZ if i < PAIRS else OFF_RZ) + (i % PAIRS) * LANES
        mix_sc[:, i * LANES:(i + 1) * LANES] = (y4[i * ROWS:(i + 1) * ROWS] * gain * _silu(proj(zoff))).astype(bf16)

    kcur = proj_sc[rows, OFF_AK:OFF_AK + A_KV_DIM]
    vcur = proj_sc[rows, OFF_AV:OFF_AV + A_KV_DIM]
    kcur16, vcur16 = kcur.astype(bf16), vcur.astype(bf16)

    def q_cols(h):
        c0 = OFF_AQ + ATTN_HEAD_POS[h] * HEAD_DIM
        return slice(c0, c0 + HEAD_DIM)

    for h in range(A_HEADS):
        off = (h // KV_GROUP) * HEAD_DIM
        qb_sc[:, h * glen:(h + 1) * glen, off:off + HEAD_DIM] = (
            proj_sc[rows, q_cols(h)].reshape(groups, glen, HEAD_DIM).astype(bf16))

    for b in range(groups):
        sp = _dot(qb_sc[b], k_in[b].astype(bf16))
        sp_sc[:, b * glen:(b + 1) * glen, :] = sp.reshape(A_HEADS, glen, WINDOW)

    dens, o_cur = [], []
    for h in range(A_HEADS):
        kv = h // KV_GROUP
        ks = slice(kv * HEAD_DIM, (kv + 1) * HEAD_DIM)
        qh16 = proj_sc[rows, q_cols(h)].astype(bf16)
        sc = _dot_nt(qh16, kcur16[:, ks]) * QK_SCALE + biasc_ref[h]
        sp = sp_sc[h] * QK_SCALE + biasp_ref[h]
        sink = sinks_ref[layer, h]
        m = jnp.maximum(jnp.maximum(jnp.max(sc, axis=1, keepdims=True),
                                    jnp.max(sp, axis=1, keepdims=True)), sink)
        pc = jnp.exp(sc - m)
        pp = jnp.exp(sp - m)
        dens.append(jnp.sum(pc, axis=1, keepdims=True) + jnp.sum(pp, axis=1, keepdims=True)
                    + jnp.exp(sink - m))
        o_cur.append(_dot(pc.astype(bf16), vcur16[:, ks]))
        pp_sc[:, h * glen:(h + 1) * glen, :] = pp.reshape(groups, glen, WINDOW).astype(bf16)

    for b in range(groups):
        ob = _dot_nt(pp_sc[b], v_in[b].astype(bf16))
        ob_sc[:, b * glen:(b + 1) * glen, :] = ob.reshape(A_HEADS, glen, A_KV_DIM)
    outs = []
    for h in ATTN_HEAD_ORDER:
        off = (h // KV_GROUP) * HEAD_DIM
        outs.append((o_cur[h] + ob_sc[h][:, off:off + HEAD_DIM]) / dens[h])
    kcur_t, vcur_t = kcur.T, vcur.T
    fresh = lane >= WINDOW - glen
    for b in range(groups):
        shift = (WINDOW - glen - b * glen) % LANES
        k_out[b] = jnp.where(fresh, pltpu.roll(kcur_t, shift, axis=1), pltpu.roll(k_in[b], WINDOW - glen, axis=1))
        v_out[b] = jnp.where(fresh, pltpu.roll(vcur_t, shift, axis=1), pltpu.roll(v_in[b], WINDOW - glen, axis=1))
    out_a = jnp.concatenate(outs, axis=1) * _silu(proj_sc[rows, OFF_AZ:OFF_AZ + A_DIM])
    mix_sc[rows, M_DIM + R_DIM:M_DIM + R_DIM + A_DIM] = out_a.astype(bf16)

    y = xcur_sc[...] + _dot(mix_sc[...], wout_ref[...])
    xcur_sc[...] = y

    @pl.when(layer == last_layer)
    def _():
        y_ref[...] = y * lax.rsqrt(jnp.mean(y * y, axis=1, keepdims=True) + NORM_EPS) * fgain_ref[...]


def _t5_bucket(dist):
    max_exact = N_BUCKETS // 2
    d = np.maximum(dist, 1).astype(np.float32)
    large = max_exact + (np.log(d / max_exact) / np.log(REL_MAX_DIST / max_exact)
                         * (N_BUCKETS - max_exact)).astype(np.int32)
    large = np.minimum(large, N_BUCKETS - 1)
    return np.where(dist < max_exact, dist, large).astype(np.int32)


def _static_tables(groups):
    glen = ROWS // groups
    r = np.arange(ROWS)
    grp, tau = r // glen, r % glen
    causal = (grp[:, None] == grp[None, :]) & (tau[None, :] <= tau[:, None])
    tril = causal.astype(np.float32)
    maskadd = np.where(causal, 0.0, -np.inf).astype(np.float32)
    log_g = np.log1p(-np.exp2(-5.0 - np.arange(R_HEADS, dtype=np.float64)))
    diff = (tau[:, None] - tau[None, :]).astype(np.float64)
    dmat = np.where(causal[None], np.exp(log_g[:, None, None] * np.maximum(diff, 0.0)[None]), 0.0)
    inter = np.exp(log_g[None, :] * (tau[:, None] + 1.0))
    tail = np.exp(log_g[None, :] * (glen - 1.0 - tau[:, None]))
    full = np.exp(log_g * glen)
    lane_head = np.arange(LANES) // HEAD_DIM
    rslab = np.zeros((3, PAIRS, ROWS, LANES), np.float64)
    for p in range(PAIRS):
        rslab[0, p] = inter[:, 2 * p + lane_head]
        rslab[1, p] = tail[:, 2 * p + lane_head]
        rslab[2, p] = full[2 * p + lane_head][None, :]
    selh = np.zeros((M_HEADS * ROWS, SPLIT_TERMS * LANES), np.float32)
    selp = np.zeros((PAIRS * LANES, SPLIT_TERMS * LANES), np.float32)
    for t in range(SPLIT_TERMS):
        for h in range(M_HEADS):
            selh[h * ROWS:(h + 1) * ROWS, t * LANES + h] = 1.0
        for p in range(PAIRS):
            for side in range(2):
                selp[p * LANES + side * HEAD_DIM:p * LANES + (side + 1) * HEAD_DIM, t * LANES + 2 * p + side] = 1.0
    return dict(tril3=jnp.asarray(np.concatenate([tril] * SPLIT_TERMS, axis=1), bf16),
                maskadd=maskadd,
                dmat2=np.concatenate(list(dmat.astype(np.float32).reshape(PAIRS, 2 * ROWS, ROWS)), axis=1),
                rslab=rslab.astype(np.float32),
                full=tuple(float(v) for v in full),
                selh=jnp.asarray(selh, bf16), selp=jnp.asarray(selp, bf16))


def _bias_vectors(rel_table):
    tb = jnp.transpose(rel_table[_t5_bucket(np.arange(WINDOW))]).astype(f32)
    ninf = jnp.full((A_HEADS, WINDOW), -jnp.inf, f32)
    rev = tb[:, :0:-1]
    return jnp.concatenate([tb[:, :1], ninf, rev, ninf[:, :1], rev, ninf], axis=1)


def _skew(u_row, rows):
    x = jnp.broadcast_to(u_row, (rows, 2 * WINDOW))
    return pltpu.roll(x, 0, 1, stride=1, stride_axis=0)[:, :WINDOW]


def _rope_tables(pos):
    half = HEAD_DIM // 2
    inv = ROPE_BASE ** (-jnp.arange(half, dtype=f32) / half)
    ang = pos.astype(f32)[:, None] * inv[None, :]
    cos, sin = jnp.cos(ang), jnp.sin(ang)
    reps = LANES // HEAD_DIM
    cos_t = jnp.tile(jnp.concatenate([cos, cos], axis=1), (1, reps))
    sin_t = jnp.tile(jnp.concatenate([-sin, sin], axis=1), (1, reps))
    return cos_t, sin_t


def _const_spec(shape, nargs):
    zeros = (0,) * len(shape)
    if nargs == 1:
        return pl.BlockSpec(shape, lambda i: zeros)
    return pl.BlockSpec(shape, lambda i, j: zeros)


def _layer_spec(shape, layer, nargs):
    idx = (layer,) + (0,) * len(shape)
    if nargs == 1:
        return pl.BlockSpec((None,) + shape, lambda i: idx)
    return pl.BlockSpec((None,) + shape, lambda i, j: idx)


def _param_specs(layer, nargs):
    ls = functools.partial(_layer_spec, layer=layer, nargs=nargs)
    return [ls((1, D_MODEL)), ls((P_COLS, D_MODEL)), ls((1, LANES)), ls((1, M_DIM)), ls((1, R_DIM)),
            pl.BlockSpec(memory_space=pltpu.SMEM), ls((D_MODEL, D_MODEL)), _const_spec((1, D_MODEL), nargs)]


def _param_args(p, layer):
    return (p["norm_gain"], p["w_in"], p["gbias"], p["m_gain"], p["r_gain"], p["sinks"][layer], p["w_out"],
            p["fgain"])


def _prompt_layer(x, p, layer, tabs, ubias, cos_t, sin_t, final):
    B, T, _ = x.shape
    tb = min(PROMPT_ROWS, T)
    chunks = tb // ROWS
    nt = T // tb
    cs = functools.partial(_const_spec, nargs=2)
    in_specs = [pl.BlockSpec((None, tb, D_MODEL), lambda b, t: (b, t, 0))] + _param_specs(layer, 2) + [
        cs((ROWS, SPLIT_TERMS * ROWS)), cs((ROWS, ROWS)), cs((2 * ROWS, PAIRS * ROWS)),
        cs((3, PAIRS, ROWS, LANES)), cs((A_HEADS, 4 * WINDOW)),
        pl.BlockSpec((tb, LANES), lambda b, t: (t, 0)), pl.BlockSpec((tb, LANES), lambda b, t: (t, 0)),
        cs((M_HEADS * ROWS, SPLIT_TERMS * LANES)), cs((PAIRS * LANES, SPLIT_TERMS * LANES)),
    ]
    out_shape = (
        jax.ShapeDtypeStruct((B, T, D_MODEL), f32),
        jax.ShapeDtypeStruct((B, M_HEADS, HEAD_DIM, HEAD_DIM), f32),
        jax.ShapeDtypeStruct((B, M_HEADS, HEAD_DIM), f32),
        jax.ShapeDtypeStruct((B, ROWS, LANES), f32),
        jax.ShapeDtypeStruct((B, R_HEADS, HEAD_DIM, HEAD_DIM), f32),
        jax.ShapeDtypeStruct((B, WINDOW, A_KV_DIM), f32),
        jax.ShapeDtypeStruct((B, WINDOW, A_KV_DIM), f32),
    )
    out_specs = (
        pl.BlockSpec((None, tb, D_MODEL), lambda b, t: (b, t, 0)),
        pl.BlockSpec((1, M_HEADS, HEAD_DIM, HEAD_DIM), lambda b, t: (b, 0, 0, 0)),
        pl.BlockSpec((1, M_HEADS, HEAD_DIM), lambda b, t: (b, 0, 0)),
        pl.BlockSpec((1, ROWS, LANES), lambda b, t: (b, 0, 0)),
        pl.BlockSpec((1, R_HEADS, HEAD_DIM, HEAD_DIM), lambda b, t: (b, 0, 0, 0)),
        pl.BlockSpec((1, WINDOW, A_KV_DIM), lambda b, t: (b, 0, 0)),
        pl.BlockSpec((1, WINDOW, A_KV_DIM), lambda b, t: (b, 0, 0)),
    )
    kern = functools.partial(_prompt_kernel, chunks=chunks, final=final)
    y, c, n, m, s, k, v = pl.pallas_call(
        kern, grid=(B, nt), in_specs=in_specs, out_specs=out_specs, out_shape=out_shape,
        scratch_shapes=[pltpu.VMEM((tb, P_COLS), f32), pltpu.VMEM((tb, D_MODEL), bf16),
                        pltpu.VMEM((PAIRS, ROWS, 2 * LANES), f32), pltpu.VMEM((PAIRS, ROWS, LANES), f32),
                        pltpu.VMEM((ROWS, LANES), f32), pltpu.VMEM((ROWS, A_KV_DIM), f32),
                        pltpu.VMEM((ROWS, A_KV_DIM), f32), pltpu.VMEM((A_HEADS * ROWS, 2 * WINDOW), f32)],
        compiler_params=pltpu.CompilerParams(dimension_semantics=("arbitrary", "arbitrary"),
                                             vmem_limit_bytes=VMEM_LIMIT_BYTES),
        name="prompt_layer",
    )(x, *_param_args(p, layer), tabs["tril3"], tabs["maskadd"], tabs["dmat2"], tabs["rslab"], ubias,
      cos_t, sin_t, tabs["selh"], tabs["selp"])
    k = k.reshape(B, WINDOW, A_KV_HEADS, HEAD_DIM)
    v = v.reshape(B, WINDOW, A_KV_HEADS, HEAD_DIM)
    return y, c, n, m[:, 0, :M_HEADS], s, k, v


def _sample_path(x, states, p, tabs, ubias, cos_t, sin_t):
    B, T, _ = x.shape
    groups = ROWS // T
    nb = B // groups
    c0, n0, m0, s0, k0, v0 = states
    depth = c0.shape[0]
    x2 = x.reshape(B * T, D_MODEL)
    n0t = jnp.transpose(n0, (0, 2, 1, 3))
    m0r = jnp.pad(jnp.repeat(m0, T, axis=1), ((0, 0), (0, 0), (0, LANES - M_HEADS)))
    k0r = jnp.transpose(k0, (0, 1, 3, 4, 2)).reshape(depth, B, A_KV_DIM, WINDOW)
    v0r = jnp.transpose(v0, (0, 1, 3, 4, 2)).reshape(depth, B, A_KV_DIM, WINDOW)

    def cs(shape):
        zeros = (0,) * len(shape)
        return pl.BlockSpec(shape, lambda i, l: zeros)

    def per_layer(shape):
        zeros = (0,) * len(shape)
        return pl.BlockSpec((None,) + shape, lambda i, l: (l,) + zeros)

    st4 = pl.BlockSpec((None, groups, M_HEADS, HEAD_DIM, HEAD_DIM), lambda i, l: (l, i, 0, 0, 0))
    stn = pl.BlockSpec((None, M_HEADS, groups, HEAD_DIM), lambda i, l: (l, 0, i, 0))
    stm = pl.BlockSpec((None, ROWS, LANES), lambda i, l: (l, i, 0))
    stk = pl.BlockSpec((None, groups, A_KV_DIM, WINDOW), lambda i, l: (l, i, 0, 0))
    rows_spec = pl.BlockSpec((ROWS, D_MODEL), lambda i, l: (i, 0))
    in_specs = [
        rows_spec,
        per_layer((1, D_MODEL)), per_layer((P_COLS, D_MODEL)), per_layer((1, LANES)), per_layer((1, M_DIM)),
        per_layer((1, R_DIM)), pl.BlockSpec(memory_space=pltpu.SMEM), per_layer((D_MODEL, D_MODEL)),
        cs((1, D_MODEL)),
        cs((ROWS, SPLIT_TERMS * ROWS)), cs((ROWS, ROWS)), cs((2 * ROWS, PAIRS * ROWS)),
        cs((3, PAIRS, ROWS, LANES)), cs((A_HEADS, 4 * WINDOW)),
        cs((ROWS, LANES)), cs((ROWS, LANES)),
        cs((M_HEADS * ROWS, SPLIT_TERMS * LANES)), cs((PAIRS * LANES, SPLIT_TERMS * LANES)),
        st4, stn, stm, st4, stk, stk,
    ]
    out_shape = (
        jax.ShapeDtypeStruct((B * T, D_MODEL), f32),
        jax.ShapeDtypeStruct((depth, B, M_HEADS, HEAD_DIM, HEAD_DIM), f32),
        jax.ShapeDtypeStruct((depth, M_HEADS, B, HEAD_DIM), f32),
        jax.ShapeDtypeStruct((depth, B * T, LANES), f32),
        jax.ShapeDtypeStruct((depth, B, R_HEADS, HEAD_DIM, HEAD_DIM), f32),
        jax.ShapeDtypeStruct((depth, B, A_KV_DIM, WINDOW), f32),
        jax.ShapeDtypeStruct((depth, B, A_KV_DIM, WINDOW), f32),
    )
    out_specs = (rows_spec, st4, stn, stm, st4, stk, stk)
    kern = functools.partial(_sample_kernel, groups=groups, ret_full=tabs["full"])
    y, c, n, m, s, k, v = pl.pallas_call(
        kern, grid=(nb, depth), in_specs=in_specs, out_specs=out_specs, out_shape=out_shape,
        scratch_shapes=[pltpu.VMEM((ROWS, P_COLS), f32), pltpu.VMEM((ROWS, D_MODEL), bf16),
                        pltpu.VMEM((ROWS, D_MODEL), f32),
                        pltpu.VMEM((groups, A_HEADS * T, A_KV_DIM), bf16),
                        pltpu.VMEM((A_HEADS, ROWS, WINDOW), f32),
                        pltpu.VMEM((groups, A_HEADS * T, WINDOW), bf16),
                        pltpu.VMEM((A_HEADS, ROWS, A_KV_DIM), f32),
                        pltpu.VMEM((A_HEADS, ROWS, ROWS), f32), pltpu.VMEM((A_HEADS, ROWS, WINDOW), f32)],
        compiler_params=pltpu.CompilerParams(dimension_semantics=("arbitrary", "arbitrary"),
                                             vmem_limit_bytes=VMEM_LIMIT_BYTES),
        name="sample_path",
    )(x2, p["norm_gain"], p["w_in"], p["gbias"], p["m_gain"], p["r_gain"], p["sinks"], p["w_out"], p["fgain"],
      tabs["tril3"], tabs["maskadd"], tabs["dmat2"], tabs["rslab"], ubias,
      cos_t, sin_t, tabs["selh"], tabs["selp"], c0, n0t, m0r, s0, k0r, v0r)
    y = y.reshape(B, T, D_MODEL)
    n = jnp.transpose(n, (0, 2, 1, 3))
    m = m.reshape(depth, B, T, LANES)[:, :, 0, :M_HEADS]
    k = jnp.transpose(k.reshape(depth, B, A_KV_HEADS, HEAD_DIM, WINDOW), (0, 1, 4, 2, 3))
    v = jnp.transpose(v.reshape(depth, B, A_KV_HEADS, HEAD_DIM, WINDOW), (0, 1, 4, 2, 3))
    return y, c, n, m, s, k, v


def _prepare_params(norm_gain, w_in, mlstm_gate_bias, mlstm_norm_gain, ret_norm_gain, attn_sinks, w_out,
                    final_norm_gain):
    depth = w_in.shape[0]
    w_t = jnp.swapaxes(w_in, 1, 2)
    split = OFF_G + N_GATES
    aq0 = split + 4 * R_DIM
    akv0 = aq0 + A_DIM
    az0 = akv0 + 2 * A_KV_DIM

    def by_head(w):
        w = w.reshape(depth, A_HEADS, HEAD_DIM, D_MODEL)
        return jnp.concatenate([w[:, h] for h in ATTN_HEAD_ORDER], axis=1)

    w_in_p = jnp.concatenate(
        [w_t[:, :split], jnp.zeros((depth, GATE_PAD - N_GATES, D_MODEL), w_t.dtype), w_t[:, split:aq0],
         by_head(w_t[:, aq0:akv0]), w_t[:, akv0:az0], by_head(w_t[:, az0:])], axis=1).astype(bf16)
    wo16 = w_out.astype(bf16)
    a0 = M_DIM + R_DIM
    w_out_p = jnp.concatenate([wo16[:, :a0, :], by_head(wo16[:, a0:, :])], axis=1)
    gbias = jnp.pad(mlstm_gate_bias.reshape(depth, 1, N_GATES), ((0, 0), (0, 0), (0, LANES - N_GATES)))
    return dict(norm_gain=norm_gain.reshape(depth, 1, D_MODEL), w_in=w_in_p, gbias=gbias,
                m_gain=mlstm_norm_gain.reshape(depth, 1, M_DIM), r_gain=ret_norm_gain.reshape(depth, 1, R_DIM),
                sinks=attn_sinks, w_out=w_out_p, fgain=final_norm_gain.reshape(1, D_MODEL))


def kernel(x_prompt, x_sample, state_mlstm_C, state_mlstm_n, state_mlstm_m, state_ret_S, cache_win_k,
           cache_win_v, norm_gain, w_in, mlstm_gate_bias, mlstm_norm_gain, ret_norm_gain, attn_sinks,
           rel_bias_table, w_out, final_norm_gain):
    depth = w_in.shape[0]
    seq = x_prompt.shape[1]
    dec_seq = x_sample.shape[1]
    past_len = seq
    p = _prepare_params(norm_gain, w_in, mlstm_gate_bias, mlstm_norm_gain, ret_norm_gain, attn_sinks, w_out,
                        final_norm_gain)
    tabs_p = _static_tables(1)
    tabs_s = _static_tables(ROWS // dec_seq)
    ubias = _bias_vectors(rel_bias_table)
    cos_p, sin_p = _rope_tables(jnp.arange(seq, dtype=jnp.int32))
    cos_s, sin_s = _rope_tables(past_len + (jnp.arange(ROWS, dtype=jnp.int32) % dec_seq))
    states = (state_mlstm_C, state_mlstm_n, state_mlstm_m, state_ret_S, cache_win_k, cache_win_v)

    xp = x_prompt
    p_states = []
    for layer in range(depth):
        xp, *sp = _prompt_layer(xp, p, layer, tabs_p, ubias, cos_p, sin_p, layer == depth - 1)
        p_states.append(sp)
    outs_p = [jnp.stack([p_states[l][i] for l in range(depth)]) for i in range(6)]
    xs, *outs_s = _sample_path(x_sample, states, p, tabs_s, ubias, cos_s, sin_s)
    return (xp, xs, *outs_p, *outs_s)
```

```python
import functools

import numpy as np
import jax
import jax.numpy as jnp
from jax import lax
from jax.experimental import pallas as pl
from jax.experimental.pallas import tpu as pltpu

D_MODEL = 1024
HEAD_DIM = 64
M_HEADS = 4
R_HEADS = 4
A_HEADS = 8
A_KV_HEADS = 2
KV_GROUP = A_HEADS // A_KV_HEADS
M_DIM = M_HEADS * HEAD_DIM
R_DIM = R_HEADS * HEAD_DIM
A_DIM = A_HEADS * HEAD_DIM
A_KV_DIM = A_KV_HEADS * HEAD_DIM
WINDOW = 128
N_BUCKETS = 32
REL_MAX_DIST = 128
ROPE_BASE = 10000.0
NORM_EPS = 1e-6
QK_SCALE = HEAD_DIM ** -0.5

LANES = 128
ROWS = 128
GATE_PAD = LANES
PAIRS = M_HEADS // 2
SPLIT_TERMS = 3

OFF_MQ = 0
OFF_MK = OFF_MQ + M_DIM
OFF_MV = OFF_MK + M_DIM
OFF_MO = OFF_MV + M_DIM
OFF_MZ = OFF_MO + M_DIM
OFF_G = OFF_MZ + M_DIM
OFF_RQ = OFF_G + GATE_PAD
OFF_RK = OFF_RQ + R_DIM
OFF_RV = OFF_RK + R_DIM
OFF_RZ = OFF_RV + R_DIM
OFF_AQ = OFF_RZ + R_DIM
OFF_AK = OFF_AQ + A_DIM
OFF_AV = OFF_AK + A_KV_DIM
OFF_AZ = OFF_AV + A_KV_DIM
P_COLS = OFF_AZ + A_DIM
N_GATES = 2 * M_HEADS
PROJ_COL_BLOCK = 512
SCHEDULE = "FBFBFB"
ATTN_HEAD_ORDER = tuple(h for j in range(KV_GROUP) for h in (j, KV_GROUP + j))
ATTN_HEAD_POS = tuple(ATTN_HEAD_ORDER.index(h) for h in range(A_HEADS))

PROMPT_ROWS = 512
VMEM_LIMIT_BYTES = 56 * 1024 * 1024

f32 = jnp.float32
bf16 = jnp.bfloat16


def _dot(a, b):
    return jnp.dot(a, b, preferred_element_type=f32)


def _dot_nt(a, b):
    return lax.dot_general(a, b, (((1,), (1,)), ((), ())), preferred_element_type=f32)


def _sigmoid(x):
    return 1.0 / (1.0 + jnp.exp(-x))


def _silu(x):
    return x * _sigmoid(x)


def _log_sigmoid(x):
    return jnp.minimum(x, 0.0) - jnp.log(1.0 + jnp.exp(-jnp.abs(x)))


def _split_parts(x, terms):
    parts, r = [], x
    for i in range(terms):
        p = r.astype(bf16)
        parts.append(p)
        if i + 1 < terms:
            r = r - p.astype(f32)
    return parts


def _split_terms(x, terms=SPLIT_TERMS):
    return jnp.concatenate(_split_parts(x, terms), axis=1)


def _exact_tril_dot(tril3, x):
    return _dot(tril3, jnp.concatenate(_split_parts(x, SPLIT_TERMS), axis=0))


def _rope(x, cos_t, sin_t, first_half):
    up = pltpu.roll(x, LANES - HEAD_DIM // 2, axis=1)
    down = pltpu.roll(x, HEAD_DIM // 2, axis=1)
    return x * cos_t + jnp.where(first_half, up, down) * sin_t


def _rms_project(x_ref, ngain_ref, win_ref, proj_sc):
    xf = x_ref[...]
    u = xf * lax.rsqrt(jnp.mean(xf * xf, axis=1, keepdims=True) + NORM_EPS) * ngain_ref[...]
    u16 = u.astype(bf16)
    for c0 in range(0, P_COLS, PROJ_COL_BLOCK):
        c1 = min(c0 + PROJ_COL_BLOCK, P_COLS)
        proj_sc[:, c0:c1] = _dot_nt(u16, win_ref[c0:c1, :])


def _prompt_kernel(x_ref, ngain_ref, win_ref, gbias_ref, mgain_ref, rgain_ref, sinks_ref, wout_ref,
                   fgain_ref, tril3_ref, maskadd_ref, dmat2_ref, rslab_ref, ubias_ref, cos_ref,
                   sin_ref, cosoff_ref, sinoff_ref, selh_ref, selp_ref,
                   y_ref, c_out, n_out, m_out, s_out, k_out, v_out,
                   proj_sc, mix_sc, cn_sc, sb_sc, m_sc, kp_sc, vp_sc, bias_sc, *, chunks, final):
    step = pl.program_id(1)
    last_step = pl.num_programs(1) - 1

    @pl.when(jnp.logical_and(pl.program_id(0) == 0, step == 0))
    def _():
        for blk in range(A_HEADS):
            u = ubias_ref[ATTN_HEAD_ORDER[blk]:ATTN_HEAD_ORDER[blk] + 1, :]
            bias_sc[blk * ROWS:(blk + 1) * ROWS, :] = jnp.concatenate(
                [_skew(u[:, :2 * WINDOW], ROWS), _skew(u[:, 2 * WINDOW:], ROWS)], axis=1)

    xf = x_ref[...]
    u16 = (xf * lax.rsqrt(jnp.mean(xf * xf, axis=1, keepdims=True) + NORM_EPS) * ngain_ref[...]).astype(bf16)
    col_blocks = [(c0, min(c0 + PROJ_COL_BLOCK, P_COLS)) for c0 in range(0, P_COLS, PROJ_COL_BLOCK)]
    half_rows = (chunks // 2) * ROWS if chunks > 1 else chunks * ROWS

    def project(r0, r1, c0, c1):
        proj_sc[r0:r1, c0:c1] = _dot_nt(u16[r0:r1], win_ref[c0:c1, :])

    for c0, c1 in col_blocks:
        project(0, half_rows, c0, c1)
    late_pieces = [(half_rows, chunks * ROWS, c0, c1) for c0, c1 in col_blocks] if half_rows < chunks * ROWS else []

    @pl.when(step == 0)
    def _():
        cn_sc[...] = jnp.zeros_like(cn_sc)
        sb_sc[...] = jnp.zeros_like(sb_sc)
        m_sc[...] = jnp.zeros_like(m_sc)
        kp_sc[...] = jnp.zeros_like(kp_sc)
        vp_sc[...] = jnp.zeros_like(vp_sc)

    lane = lax.broadcasted_iota(jnp.int32, (ROWS, LANES), 1)
    row = lax.broadcasted_iota(jnp.int32, (ROWS, LANES), 0)
    left = lane < HEAD_DIM
    first_half = (lane & (HEAD_DIM - 1)) < (HEAD_DIM // 2)
    head_col = lane < M_HEADS
    blockdiag = (row < HEAD_DIM) == left
    row2 = lax.broadcasted_iota(jnp.int32, (ROWS, 2 * LANES), 0)
    lane2w = lax.broadcasted_iota(jnp.int32, (ROWS, 2 * LANES), 1)
    left2 = (lane2w & (LANES - 1)) < HEAD_DIM
    blockdiag2 = (row2 < HEAD_DIM) == left2
    ones16 = jnp.ones((ROWS, LANES), bf16)
    halves, pick, pair_blockdiag, half_mean = _halves, _pick, _pair_blockdiag, _half_mean

    def chunk_body(ci):
        rows = slice(ci * ROWS, (ci + 1) * ROWS)

        def proj(off, width=LANES):
            return proj_sc[rows, off:off + width]

        gates = proj(OFF_G) + gbias_ref[...]
        bcum = _exact_tril_dot(tril3_ref[...], _log_sigmoid(gates))

        base = pl.ds(step * chunks + ci, 1)
        cos_a, sin_a = cos_ref[base, :], sin_ref[base, :]
        cos_b, sin_b = cosoff_ref[...], sinoff_ref[...]
        cos_t = cos_a * cos_b - sin_a * sin_b
        sin_t = sin_a * cos_b + cos_a * sin_b
        sin_t = jnp.where(first_half, -sin_t, sin_t)
        r_q = [_rope(proj(OFF_RQ + p * LANES), cos_t, sin_t, first_half) for p in range(PAIRS)]
        r_k = [_rope(proj(OFF_RK + p * LANES), cos_t, sin_t, first_half) * QK_SCALE for p in range(PAIRS)]
        r_vbd = pair_blockdiag([proj(OFF_RV + p * LANES).astype(bf16) for p in range(PAIRS)])
        r_sb = [sb_sc[p] for p in range(PAIRS)]
        r_sc = _dot_nt(jnp.concatenate([halves(q) for q in r_q], axis=1),
                       pair_blockdiag([k.astype(bf16) for k in r_k]))
        r_inter = _dot(jnp.concatenate([q.astype(bf16) for q in r_q], axis=1),
                       pair_blockdiag([sb.astype(bf16) for sb in r_sb]))
        r_upd = _dot(jnp.concatenate([(r_k[p] * rslab_ref[1, p]).T.astype(bf16) for p in range(PAIRS)], axis=1),
                     r_vbd)
        for p in range(PAIRS):
            sb_sc[p] = rslab_ref[2, p] * r_sb[p] + jnp.where(blockdiag, r_upd[:, p * LANES:(p + 1) * LANES], 0.0)
        yield

        kcur, vcur = proj(OFF_AK), proj(OFF_AV)
        kprev, vprev = kp_sc[...], vp_sc[...]
        kk16 = jnp.concatenate([kcur, kprev], axis=0).astype(bf16)
        vv16 = jnp.concatenate([jnp.concatenate([vcur.astype(bf16), ones16], axis=1),
                                jnp.concatenate([vprev.astype(bf16), ones16], axis=1)], axis=0)
        kp_sc[...] = kcur
        vp_sc[...] = vcur
        a_q = jnp.concatenate([halves(proj(OFF_AQ + j * LANES) * QK_SCALE) for j in range(KV_GROUP)], axis=0)
        a_s = _dot_nt(a_q, kk16) + bias_sc[...]
        if ci == 0:
            pen = jnp.where(step == 0, -jnp.inf, 0.0).astype(f32)
            a_s = a_s + jnp.where(lax.broadcasted_iota(jnp.int32, (1, 2 * ROWS), 1) >= ROWS, pen, 0.0)

        m_q = [proj(OFF_MQ + p * LANES) for p in range(PAIRS)]
        m_k = [proj(OFF_MK + p * LANES) * QK_SCALE for p in range(PAIRS)]
        m_qk = _dot_nt(jnp.concatenate([halves(q) for q in m_q], axis=1),
                       pair_blockdiag([k.astype(bf16) for k in m_k]))
        m_q16 = [q.astype(bf16) for q in m_q]
        m_v16 = [proj(OFF_MV + p * LANES).astype(bf16) for p in range(PAIRS)]
        yield

        zb = pltpu.roll(bcum, LANES - M_HEADS, axis=1)
        r_mat = jnp.where(head_col, gates - zb, 0.0)
        cm = r_mat
        sh = 1
        while sh < ROWS:
            cm = jnp.where(row >= sh, jnp.maximum(cm, pltpu.roll(cm, sh, axis=0)), cm)
            sh *= 2
        mprev = m_sc[...]
        mx = jnp.maximum(mprev, cm)
        gm = mprev - mx
        em = jnp.where(head_col, -(zb + mx), 0.0)
        mx_last = jnp.broadcast_to(mx[ROWS - 1:ROWS, :], (ROWS, LANES))
        m_sc[...] = jnp.where(head_col, jnp.broadcast_to((zb + mx)[ROWS - 1:ROWS, :], (ROWS, LANES)), 0.0)
        mx_b = _dot_nt(_split_terms(mx), selh_ref[...])
        slabs = jnp.exp(_dot_nt(_split_terms(jnp.concatenate([gm, em, r_mat - mx_last], axis=0)),
                                selp_ref[...]))
        winter_b, emt_b, ws_b = slabs[:ROWS], slabs[ROWS:2 * ROWS], slabs[2 * ROWS:]
        r_t = r_mat.T
        yield

        outs = []
        r_acc = []
        r_o = _dot((r_sc * dmat2_ref[...]).astype(bf16), r_vbd)
        for p in range(PAIRS):
            ps = slice(p * LANES, (p + 1) * LANES)
            r_acc.append(pick(r_o[:, ps]) + rslab_ref[0, p] * r_inter[:, ps])

        a_out = []
        a_p = []
        for blk in range(A_HEADS):
            s = a_s[blk * ROWS:(blk + 1) * ROWS]
            sink = sinks_ref[ATTN_HEAD_ORDER[blk]]
            m = jnp.maximum(jnp.max(jnp.maximum(s[:, :ROWS], s[:, ROWS:]), axis=1, keepdims=True), sink)
            a_p.append(jnp.exp(s - m).astype(bf16))
            a_out.append(jnp.exp(sink - m))
        a_pv = _dot(jnp.concatenate(a_p, axis=0), vv16)
        yield

        maskadd = maskadd_ref[...]
        for p in range(PAIRS):
            ps = slice(p * LANES, (p + 1) * LANES)
            cn = cn_sc[p]
            w = jnp.concatenate(
                [jnp.exp((r_t[2 * p + side:2 * p + side + 1, :] + maskadd)
                         - mx_b[:, (2 * p + side) * ROWS:(2 * p + side + 1) * ROWS]) for side in range(2)],
                axis=0) * m_qk[:, ps]
            acc = (pick(_dot(w.astype(bf16), jnp.concatenate([m_v16[p], ones16], axis=1)))
                   + jnp.concatenate([winter_b[:, ps]] * 2, axis=1) * _dot(m_q16[p], cn.astype(bf16)))
            hh = acc[:, :LANES] / jnp.maximum(jnp.abs(acc[:, LANES:]), emt_b[:, ps])
            outs.append(_sigmoid(proj(OFF_MO + p * LANES)) * hh)
            kwt16 = (m_k[p] * ws_b[:, ps]).T.astype(bf16)
            dcn = _dot(kwt16, jnp.concatenate([m_v16[p], ones16], axis=1))
            decay = winter_b[ROWS - 1:ROWS, ps]
            cn_sc[p] = jnp.concatenate([decay, decay], axis=1) * cn + jnp.where(blockdiag2, dcn, 0.0)
        outs.extend(r_acc)
        yield

        x4 = jnp.concatenate(outs, axis=0)
        xc = x4 - half_mean(x4)
        y4 = xc * lax.rsqrt(half_mean(xc * xc) + NORM_EPS)
        for i in range(2 * PAIRS):
            gain = (mgain_ref if i < PAIRS else rgain_ref)[:, (i % PAIRS) * LANES:(i % PAIRS + 1) * LANES]
            zoff = (OFF_MZ if i < PAIRS else OFF_RZ) + (i % PAIRS) * LANES
            out = y4[i * ROWS:(i + 1) * ROWS] * gain * _silu(proj(zoff))
            mix_sc[rows, i * LANES:(i + 1) * LANES] = out.astype(bf16)
        for j in range(KV_GROUP):
            acc = pick(a_pv[2 * j * ROWS:(2 * j + 2) * ROWS])
            den = acc[:, LANES:] + jnp.where(left, a_out[2 * j], a_out[2 * j + 1])
            out = (acc[:, :LANES] / den) * _silu(proj(OFF_AZ + j * LANES))
            mix_sc[rows, M_DIM + R_DIM + j * LANES:M_DIM + R_DIM + (j + 1) * LANES] = out.astype(bf16)

    def out_project(r0, r1, c0, c1):
        y_ref[r0:r1, c0:c1] = x_ref[r0:r1, c0:c1] + _dot(mix_sc[r0:r1, :], wout_ref[:, c0:c1])

    out_blocks = [(c0, min(c0 + PROJ_COL_BLOCK, D_MODEL)) for c0 in range(0, D_MODEL, PROJ_COL_BLOCK)]
    early_out = [(0, half_rows, c0, c1) for c0, c1 in out_blocks] if half_rows < chunks * ROWS else []
    final_out = [(half_rows if early_out else 0, chunks * ROWS, c0, c1) for c0, c1 in out_blocks]

    def fill_mxu(ci):
        if late_pieces:
            project(*late_pieces.pop(0))
        elif early_out and (ci - 1) * ROWS >= half_rows:
            out_project(*early_out.pop(0))

    parts = [chunk_body(ci) for ci in range(chunks)]
    for ci in range(chunks + 1):
        if ci * ROWS >= half_rows:
            while late_pieces:
                project(*late_pieces.pop(0))
        for which in SCHEDULE:
            if which == "F" and ci < chunks:
                next(parts[ci])
                fill_mxu(ci)
            if which == "B" and ci > 0:
                next(parts[ci - 1], None)
                fill_mxu(ci)
    for piece in early_out + final_out:
        out_project(*piece)
    if final:
        y = y_ref[...]
        y_ref[...] = y * lax.rsqrt(jnp.mean(y * y, axis=1, keepdims=True) + NORM_EPS) * fgain_ref[...]

    @pl.when(step == last_step)
    def _():
        for p in range(PAIRS):
            cn = cn_sc[p]
            sb = sb_sc[p]
            n_t = cn[:, LANES:].T
            for side in range(2):
                h = 2 * p + side
                blk = slice(side * HEAD_DIM, (side + 1) * HEAD_DIM)
                c_out[0, h] = cn[blk, blk]
                s_out[0, h] = sb[blk, blk]
                n_out[0, h:h + 1, :] = n_t[side * HEAD_DIM:side * HEAD_DIM + 1, blk]
        m_out[0] = m_sc[...]
        k_out[0] = kp_sc[...]
        v_out[0] = vp_sc[...]


def _lane_is_left(shape):
    return (lax.broadcasted_iota(jnp.int32, shape, 1) & (LANES - 1)) < HEAD_DIM


def _halves(x):
    left = _lane_is_left(x.shape)
    return jnp.concatenate([jnp.where(left, x, 0.0), jnp.where(left, 0.0, x)], axis=0).astype(bf16)


def _pick(x):
    return jnp.where(_lane_is_left((ROWS, x.shape[1])), x[:ROWS], x[ROWS:])


def _pair_blockdiag(blocks):
    z = jnp.zeros_like(blocks[0])
    return jnp.concatenate(
        [jnp.concatenate([blk if j == i else z for j in range(len(blocks))], axis=1)
         for i, blk in enumerate(blocks)], axis=0)


def _half_mean(x):
    left = _lane_is_left(x.shape)
    s_left = jnp.sum(jnp.where(left, x, 0.0), axis=1, keepdims=True)
    s_right = jnp.sum(jnp.where(left, 0.0, x), axis=1, keepdims=True)
    return jnp.where(left, s_left, s_right) * (1.0 / HEAD_DIM)


def _group_last(x, groups):
    n = x.shape[1]
    glen = ROWS // groups
    x3 = x.reshape(groups, glen, n)
    return jnp.broadcast_to(x3[:, glen - 1:glen, :], (groups, glen, n)).reshape(ROWS, n)


def _sample_kernel(x_ref, ngain_ref, win_ref, gbias_ref, mgain_ref, rgain_ref, sinks_ref, wout_ref,
                   fgain_ref, tril3_ref, maskadd_ref, dmat2_ref, rslab_ref, ubias_ref, cos_ref, sin_ref,
                   selh_ref, selp_ref,
                   c_in, n_in, m_in, s_in, k_in, v_in,
                   y_ref, c_out, n_out, m_out, s_out, k_out, v_out,
                   proj_sc, mix_sc, xcur_sc, qb_sc, sp_sc, pp_sc, ob_sc, biasc_ref, biasp_ref, *, groups,
                   ret_full):
    glen = ROWS // groups
    glen_log2 = glen.bit_length() - 1
    hd_log2 = HEAD_DIM.bit_length() - 1
    step = pl.program_id(0)
    layer = pl.program_id(1)
    last_layer = pl.num_programs(1) - 1

    @pl.when(layer == 0)
    def _():
        xcur_sc[...] = x_ref[...]

    _rms_project(xcur_sc, ngain_ref, win_ref, proj_sc)

    @pl.when(jnp.logical_and(step == 0, layer == 0))
    def _():
        qb_sc[...] = jnp.zeros_like(qb_sc)
        for h in range(A_HEADS):
            u = ubias_ref[h:h + 1, :]
            biasc_ref[h] = _skew(u[:, :2 * WINDOW], ROWS) + maskadd_ref[...]
            biasp_ref[h] = jnp.concatenate([_skew(u[:, 2 * WINDOW:], glen)] * groups, axis=0)

    lane = lax.broadcasted_iota(jnp.int32, (ROWS, LANES), 1)
    first_half = (lane & (HEAD_DIM - 1)) < (HEAD_DIM // 2)
    rows = slice(0, ROWS)

    r_i = lax.broadcasted_iota(jnp.int32, (ROWS, groups * HEAD_DIM), 0)
    c_i = lax.broadcasted_iota(jnp.int32, (ROWS, groups * HEAD_DIM), 1)
    blk = (r_i >> glen_log2) == (c_i >> hd_log2)
    r_t = lax.broadcasted_iota(jnp.int32, (groups * HEAD_DIM, ROWS), 0)
    c_t = lax.broadcasted_iota(jnp.int32, (groups * HEAD_DIM, ROWS), 1)
    blk_t = (r_t >> hd_log2) == (c_t >> glen_log2)

    def q_times_state(qh, st):
        qt = jnp.where(blk, jnp.concatenate([qh] * groups, axis=1), 0.0)
        return _dot(qt.astype(bf16), st.astype(bf16))

    def state_increment(kt_h, vh16):
        kt = jnp.where(blk_t, jnp.concatenate([kt_h] * groups, axis=0), 0.0)
        return _dot(kt.astype(bf16), vh16)

    def proj(off, width=LANES):
        return proj_sc[:, off:off + width]

    def head_state_rows(slab, side):
        wide = slab[:, side * HEAD_DIM:(side + 1) * HEAD_DIM].reshape(groups, glen, HEAD_DIM)[:, 0:1, :]
        rows_ = jnp.broadcast_to(wide, (groups, HEAD_DIM, HEAD_DIM)).reshape(groups * HEAD_DIM, HEAD_DIM)
        return rows_, wide.reshape(groups, HEAD_DIM)

    row = lax.broadcasted_iota(jnp.int32, (ROWS, LANES), 0)
    tau = row & (glen - 1)
    head_col = lane < M_HEADS
    left = lane < HEAD_DIM
    ones16 = jnp.ones((ROWS, LANES), bf16)
    gates = proj(OFF_G) + gbias_ref[...]
    bcum = _exact_tril_dot(tril3_ref[...], _log_sigmoid(gates))
    zb = pltpu.roll(bcum, LANES - M_HEADS, axis=1)
    r_mat = jnp.where(head_col, gates - zb, 0.0)
    cm = r_mat
    sh = 1
    while sh < glen:
        cm = jnp.where(tau >= sh, jnp.maximum(cm, pltpu.roll(cm, sh, axis=0)), cm)
        sh *= 2
    mprev = m_in[...]
    mx = jnp.maximum(mprev, cm)
    gm = mprev - mx
    em = jnp.where(head_col, -(zb + mx), 0.0)
    mx_last = _group_last(mx, groups)
    m_out[...] = jnp.where(head_col, _group_last(zb + mx, groups), 0.0)
    mx_b = _dot_nt(_split_terms(mx), selh_ref[...])
    slabs = jnp.exp(_dot_nt(_split_terms(jnp.concatenate([gm, em, r_mat - mx_last], axis=0)), selp_ref[...]))
    winter_b, emt_b, ws_b = slabs[:ROWS], slabs[ROWS:2 * ROWS], slabs[2 * ROWS:]
    decay_b = _group_last(winter_b, groups)
    r_t = r_mat.T
    maskadd = maskadd_ref[...]

    m_q = [proj(OFF_MQ + p * LANES) for p in range(PAIRS)]
    m_k = [proj(OFF_MK + p * LANES) * QK_SCALE for p in range(PAIRS)]
    m_v = [proj(OFF_MV + p * LANES) for p in range(PAIRS)]
    m_qk = _dot_nt(jnp.concatenate([_halves(q) for q in m_q], axis=1),
                   _pair_blockdiag([k.astype(bf16) for k in m_k]))
    outs = []
    for p in range(PAIRS):
        ps = slice(p * LANES, (p + 1) * LANES)
        w = jnp.concatenate(
            [jnp.exp((r_t[2 * p + side:2 * p + side + 1, :] + maskadd)
                     - mx_b[:, (2 * p + side) * ROWS:(2 * p + side + 1) * ROWS]) for side in range(2)],
            axis=0) * m_qk[:, ps]
        intra = _pick(_dot(w.astype(bf16), jnp.concatenate([m_v[p].astype(bf16), ones16], axis=1)))
        kw = m_k[p] * ws_b[:, ps]
        kwt = kw.T
        q_c, q_n = [], []
        for side in range(2):
            h = 2 * p + side
            hs = slice(side * HEAD_DIM, (side + 1) * HEAD_DIM)
            qh = m_q[p][:, hs]
            c_h = c_in[:, h].reshape(groups * HEAD_DIM, HEAD_DIM)
            n_g = n_in[h]
            n_rows = jnp.broadcast_to(n_g.reshape(groups, 1, HEAD_DIM),
                                      (groups, glen, HEAD_DIM)).reshape(ROWS, HEAD_DIM)
            q_c.append(q_times_state(qh, c_h))
            q_n.append(jnp.sum(qh * n_rows, axis=1, keepdims=True))
            dec_rows, dec_g = head_state_rows(decay_b[:, ps], side)
            c_new = dec_rows * c_h + state_increment(kwt[hs, :], m_v[p][:, hs].astype(bf16))
            c_out[:, h] = c_new.reshape(groups, HEAD_DIM, HEAD_DIM)
            n_out[h] = dec_g * n_g + jnp.sum(kw[:, hs].reshape(groups, glen, HEAD_DIM), axis=1)
        wb = winter_b[:, ps]
        num = intra[:, :LANES] + wb * jnp.concatenate(q_c, axis=1)
        nq = intra[:, LANES:] + wb * jnp.where(left, q_n[0], q_n[1])
        outs.append(_sigmoid(proj(OFF_MO + p * LANES)) * (num / jnp.maximum(jnp.abs(nq), emt_b[:, ps])))

    cos_t, sin_t = cos_ref[...], sin_ref[...]
    r_q = [_rope(proj(OFF_RQ + p * LANES), cos_t, sin_t, first_half) for p in range(PAIRS)]
    r_k = [_rope(proj(OFF_RK + p * LANES), cos_t, sin_t, first_half) * QK_SCALE for p in range(PAIRS)]
    r_v = [proj(OFF_RV + p * LANES) for p in range(PAIRS)]
    r_sc = _dot_nt(jnp.concatenate([_halves(q) for q in r_q], axis=1),
                   _pair_blockdiag([k.astype(bf16) for k in r_k]))
    r_o = _dot((r_sc * dmat2_ref[...]).astype(bf16), _pair_blockdiag([v.astype(bf16) for v in r_v]))
    for p in range(PAIRS):
        ps = slice(p * LANES, (p + 1) * LANES)
        rkt = (r_k[p] * rslab_ref[1, p]).T
        q_s = []
        for side in range(2):
            h = 2 * p + side
            hs = slice(side * HEAD_DIM, (side + 1) * HEAD_DIM)
            s_h = s_in[:, h].reshape(groups * HEAD_DIM, HEAD_DIM)
            q_s.append(q_times_state(r_q[p][:, hs], s_h))
            s_new = ret_full[h] * s_h + state_increment(rkt[hs, :], r_v[p][:, hs].astype(bf16))
            s_out[:, h] = s_new.reshape(groups, HEAD_DIM, HEAD_DIM)
        outs.append(_pick(r_o[:, ps]) + rslab_ref[0, p] * jnp.concatenate(q_s, axis=1))

    x4 = jnp.concatenate(outs, axis=0)
    xc = x4 - _half_mean(x4)
    y4 = xc * lax.rsqrt(_half_mean(xc * xc) + NORM_EPS)
    for i in range(2 * PAIRS):
        gain = (mgain_ref if i < PAIRS else rgain_ref)[:, (i % PAIRS) * LANES:(i % PAIRS + 1) * LANES]
        zoff = (OFF_MZ if i < PAIRS else OFF_RZ) + (i % PAIRS) * LANES
        mix_sc[:, i * LANES:(i + 1) * LANES] = (y4[i * ROWS:(i + 1) * ROWS] * gain * _silu(proj(zoff))).astype(bf16)

    kcur = proj_sc[rows, OFF_AK:OFF_AK + A_KV_DIM]
    vcur = proj_sc[rows, OFF_AV:OFF_AV + A_KV_DIM]
    kcur16, vcur16 = kcur.astype(bf16), vcur.astype(bf16)

    def q_cols(h):
        c0 = OFF_AQ + ATTN_HEAD_POS[h] * HEAD_DIM
        return slice(c0, c0 + HEAD_DIM)

    for h in range(A_HEADS):
        off = (h // KV_GROUP) * HEAD_DIM
        qb_sc[:, h * glen:(h + 1) * glen, off:off + HEAD_DIM] = (
            proj_sc[rows, q_cols(h)].reshape(groups, glen, HEAD_DIM).astype(bf16))

    for b in range(groups):
        sp = _dot(qb_sc[b], k_in[b].astype(bf16))
        sp_sc[:, b * glen:(b + 1) * glen, :] = sp.reshape(A_HEADS, glen, WINDOW)

    dens, o_cur = [], []
    for h in range(A_HEADS):
        kv = h // KV_GROUP
        ks = slice(kv * HEAD_DIM, (kv + 1) * HEAD_DIM)
        qh16 = proj_sc[rows, q_cols(h)].astype(bf16)
        sc = _dot_nt(qh16, kcur16[:, ks]) * QK_SCALE + biasc_ref[h]
        sp = sp_sc[h] * QK_SCALE + biasp_ref[h]
        sink = sinks_ref[layer, h]
        m = jnp.maximum(jnp.maximum(jnp.max(sc, axis=1, keepdims=True),
                                    jnp.max(sp, axis=1, keepdims=True)), sink)
        pc = jnp.exp(sc - m)
        pp = jnp.exp(sp - m)
        dens.append(jnp.sum(pc, axis=1, keepdims=True) + jnp.sum(pp, axis=1, keepdims=True)
                    + jnp.exp(sink - m))
        o_cur.append(_dot(pc.astype(bf16), vcur16[:, ks]))
        pp_sc[:, h * glen:(h + 1) * glen, :] = pp.reshape(groups, glen, WINDOW).astype(bf16)

    for b in range(groups):
        ob = _dot_nt(pp_sc[b], v_in[b].astype(bf16))
        ob_sc[:, b * glen:(b + 1) * glen, :] = ob.reshape(A_HEADS, glen, A_KV_DIM)
    outs = []
    for h in ATTN_HEAD_ORDER:
        off = (h // KV_GROUP) * HEAD_DIM
        outs.append((o_cur[h] + ob_sc[h][:, off:off + HEAD_DIM]) / dens[h])
    kcur_t, vcur_t = kcur.T, vcur.T
    fresh = lane >= WINDOW - glen
    for b in range(groups):
        shift = (WINDOW - glen - b * glen) % LANES
        k_out[b] = jnp.where(fresh, pltpu.roll(kcur_t, shift, axis=1), pltpu.roll(k_in[b], WINDOW - glen, axis=1))
        v_out[b] = jnp.where(fresh, pltpu.roll(vcur_t, shift, axis=1), pltpu.roll(v_in[b], WINDOW - glen, axis=1))
    out_a = jnp.concatenate(outs, axis=1) * _silu(proj_sc[rows, OFF_AZ:OFF_AZ + A_DIM])
    mix_sc[rows, M_DIM + R_DIM:M_DIM + R_DIM + A_DIM] = out_a.astype(bf16)

    y = xcur_sc[...] + _dot(mix_sc[...], wout_ref[...])
    xcur_sc[...] = y

    @pl.when(layer == last_layer)
    def _():
        y_ref[...] = y * lax.rsqrt(jnp.mean(y * y, axis=1, keepdims=True) + NORM_EPS) * fgain_ref[...]


def _t5_bucket(dist):
    max_exact = N_BUCKETS // 2
    d = np.maximum(dist, 1).astype(np.float32)
    large = max_exact + (np.log(d / max_exact) / np.log(REL_MAX_DIST / max_exact)
                         * (N_BUCKETS - max_exact)).astype(np.int32)
    large = np.minimum(large, N_BUCKETS - 1)
    return np.where(dist < max_exact, dist, large).astype(np.int32)


def _static_tables(groups):
    glen = ROWS // groups
    r = np.arange(ROWS)
    grp, tau = r // glen, r % glen
    causal = (grp[:, None] == grp[None, :]) & (tau[None, :] <= tau[:, None])
    tril = causal.astype(np.float32)
    maskadd = np.where(causal, 0.0, -np.inf).astype(np.float32)
    log_g = np.log1p(-np.exp2(-5.0 - np.arange(R_HEADS, dtype=np.float64)))
    diff = (tau[:, None] - tau[None, :]).astype(np.float64)
    dmat = np.where(causal[None], np.exp(log_g[:, None, None] * np.maximum(diff, 0.0)[None]), 0.0)
    inter = np.exp(log_g[None, :] * (tau[:, None] + 1.0))
    tail = np.exp(log_g[None, :] * (glen - 1.0 - tau[:, None]))
    full = np.exp(log_g * glen)
    lane_head = np.arange(LANES) // HEAD_DIM
    rslab = np.zeros((3, PAIRS, ROWS, LANES), np.float64)
    for p in range(PAIRS):
        rslab[0, p] = inter[:, 2 * p + lane_head]
        rslab[1, p] = tail[:, 2 * p + lane_head]
        rslab[2, p] = full[2 * p + lane_head][None, :]
    selh = np.zeros((M_HEADS * ROWS, SPLIT_TERMS * LANES), np.float32)
    selp = np.zeros((PAIRS * LANES, SPLIT_TERMS * LANES), np.float32)
    for t in range(SPLIT_TERMS):
        for h in range(M_HEADS):
            selh[h * ROWS:(h + 1) * ROWS, t * LANES + h] = 1.0
        for p in range(PAIRS):
            for side in range(2):
                selp[p * LANES + side * HEAD_DIM:p * LANES + (side + 1) * HEAD_DIM, t * LANES + 2 * p + side] = 1.0
    return dict(tril3=jnp.asarray(np.concatenate([tril] * SPLIT_TERMS, axis=1), bf16),
                maskadd=maskadd,
                dmat2=np.concatenate(list(dmat.astype(np.float32).reshape(PAIRS, 2 * ROWS, ROWS)), axis=1),
                rslab=rslab.astype(np.float32),
                full=tuple(float(v) for v in full),
                selh=jnp.asarray(selh, bf16), selp=jnp.asarray(selp, bf16))


def _bias_vectors(rel_table):
    tb = jnp.transpose(rel_table[_t5_bucket(np.arange(WINDOW))]).astype(f32)
    ninf = jnp.full((A_HEADS, WINDOW), -jnp.inf, f32)
    rev = tb[:, :0:-1]
    return jnp.concatenate([tb[:, :1], ninf, rev, ninf[:, :1], rev, ninf], axis=1)


def _skew(u_row, rows):
    x = jnp.broadcast_to(u_row, (rows, 2 * WINDOW))
    return pltpu.roll(x, 0, 1, stride=1, stride_axis=0)[:, :WINDOW]


def _rope_tables(pos, signed=True):
    half = HEAD_DIM // 2
    inv = ROPE_BASE ** (-jnp.arange(half, dtype=f32) / half)
    ang = pos.astype(f32)[:, None] * inv[None, :]
    cos, sin = jnp.cos(ang), jnp.sin(ang)
    reps = LANES // HEAD_DIM
    cos_t = jnp.tile(jnp.concatenate([cos, cos], axis=1), (1, reps))
    sin_t = jnp.tile(jnp.concatenate([-sin if signed else sin, sin], axis=1), (1, reps))
    return cos_t, sin_t


def _const_spec(shape, nargs):
    zeros = (0,) * len(shape)
    if nargs == 1:
        return pl.BlockSpec(shape, lambda i: zeros)
    return pl.BlockSpec(shape, lambda i, j: zeros)


def _layer_spec(shape, layer, nargs):
    idx = (layer,) + (0,) * len(shape)
    if nargs == 1:
        return pl.BlockSpec((None,) + shape, lambda i: idx)
    return pl.BlockSpec((None,) + shape, lambda i, j: idx)


def _param_specs(layer, nargs):
    ls = functools.partial(_layer_spec, layer=layer, nargs=nargs)
    return [ls((1, D_MODEL)), ls((P_COLS, D_MODEL)), ls((1, LANES)), ls((1, M_DIM)), ls((1, R_DIM)),
            pl.BlockSpec(memory_space=pltpu.SMEM), ls((D_MODEL, D_MODEL)), _const_spec((1, D_MODEL), nargs)]


def _param_args(p, layer):
    return (p["norm_gain"], p["w_in"], p["gbias"], p["m_gain"], p["r_gain"], p["sinks"][layer], p["w_out"],
            p["fgain"])


def _prompt_layer(x, p, layer, tabs, ubias, rope, final):
    B, T, _ = x.shape
    tb = min(PROMPT_ROWS, T)
    chunks = tb // ROWS
    nt = T // tb
    cs = functools.partial(_const_spec, nargs=2)
    in_specs = [pl.BlockSpec((None, tb, D_MODEL), lambda b, t: (b, t, 0))] + _param_specs(layer, 2) + [
        cs((ROWS, SPLIT_TERMS * ROWS)), cs((ROWS, ROWS)), cs((2 * ROWS, PAIRS * ROWS)),
        cs((3, PAIRS, ROWS, LANES)), cs((A_HEADS, 4 * WINDOW)),
        cs((T // ROWS, LANES)), cs((T // ROWS, LANES)), cs((ROWS, LANES)), cs((ROWS, LANES)),
        cs((M_HEADS * ROWS, SPLIT_TERMS * LANES)), cs((PAIRS * LANES, SPLIT_TERMS * LANES)),
    ]
    out_shape = (
        jax.ShapeDtypeStruct((B, T, D_MODEL), f32),
        jax.ShapeDtypeStruct((B, M_HEADS, HEAD_DIM, HEAD_DIM), f32),
        jax.ShapeDtypeStruct((B, M_HEADS, HEAD_DIM), f32),
        jax.ShapeDtypeStruct((B, ROWS, LANES), f32),
        jax.ShapeDtypeStruct((B, R_HEADS, HEAD_DIM, HEAD_DIM), f32),
        jax.ShapeDtypeStruct((B, WINDOW, A_KV_DIM), f32),
        jax.ShapeDtypeStruct((B, WINDOW, A_KV_DIM), f32),
    )
    out_specs = (
        pl.BlockSpec((None, tb, D_MODEL), lambda b, t: (b, t, 0)),
        pl.BlockSpec((1, M_HEADS, HEAD_DIM, HEAD_DIM), lambda b, t: (b, 0, 0, 0)),
        pl.BlockSpec((1, M_HEADS, HEAD_DIM), lambda b, t: (b, 0, 0)),
        pl.BlockSpec((1, ROWS, LANES), lambda b, t: (b, 0, 0)),
        pl.BlockSpec((1, R_HEADS, HEAD_DIM, HEAD_DIM), lambda b, t: (b, 0, 0, 0)),
        pl.BlockSpec((1, WINDOW, A_KV_DIM), lambda b, t: (b, 0, 0)),
        pl.BlockSpec((1, WINDOW, A_KV_DIM), lambda b, t: (b, 0, 0)),
    )
    kern = functools.partial(_prompt_kernel, chunks=chunks, final=final)
    y, c, n, m, s, k, v = pl.pallas_call(
        kern, grid=(B, nt), in_specs=in_specs, out_specs=out_specs, out_shape=out_shape,
        scratch_shapes=[pltpu.VMEM((tb, P_COLS), f32), pltpu.VMEM((tb, D_MODEL), bf16),
                        pltpu.VMEM((PAIRS, ROWS, 2 * LANES), f32), pltpu.VMEM((PAIRS, ROWS, LANES), f32),
                        pltpu.VMEM((ROWS, LANES), f32), pltpu.VMEM((ROWS, A_KV_DIM), f32),
                        pltpu.VMEM((ROWS, A_KV_DIM), f32), pltpu.VMEM((A_HEADS * ROWS, 2 * WINDOW), f32)],
        compiler_params=pltpu.CompilerParams(dimension_semantics=("arbitrary", "arbitrary"),
                                             vmem_limit_bytes=VMEM_LIMIT_BYTES),
        name="prompt_layer",
    )(x, *_param_args(p, layer), tabs["tril3"], tabs["maskadd"], tabs["dmat2"], tabs["rslab"], ubias,
      *rope, tabs["selh"], tabs["selp"])
    k = k.reshape(B, WINDOW, A_KV_HEADS, HEAD_DIM)
    v = v.reshape(B, WINDOW, A_KV_HEADS, HEAD_DIM)
    return y, c, n, m[:, 0, :M_HEADS], s, k, v


def _sample_path(x, states, p, tabs, ubias, cos_t, sin_t):
    B, T, _ = x.shape
    groups = ROWS // T
    nb = B // groups
    c0, n0, m0, s0, k0, v0 = states
    depth = c0.shape[0]
    x2 = x.reshape(B * T, D_MODEL)
    n0t = jnp.transpose(n0, (0, 2, 1, 3))
    m0r = jnp.pad(jnp.repeat(m0, T, axis=1), ((0, 0), (0, 0), (0, LANES - M_HEADS)))
    k0r = jnp.transpose(k0, (0, 1, 3, 4, 2)).reshape(depth, B, A_KV_DIM, WINDOW)
    v0r = jnp.transpose(v0, (0, 1, 3, 4, 2)).reshape(depth, B, A_KV_DIM, WINDOW)

    def cs(shape):
        zeros = (0,) * len(shape)
        return pl.BlockSpec(shape, lambda i, l: zeros)

    def per_layer(shape):
        zeros = (0,) * len(shape)
        return pl.BlockSpec((None,) + shape, lambda i, l: (l,) + zeros)

    st4 = pl.BlockSpec((None, groups, M_HEADS, HEAD_DIM, HEAD_DIM), lambda i, l: (l, i, 0, 0, 0))
    stn = pl.BlockSpec((None, M_HEADS, groups, HEAD_DIM), lambda i, l: (l, 0, i, 0))
    stm = pl.BlockSpec((None, ROWS, LANES), lambda i, l: (l, i, 0))
    stk = pl.BlockSpec((None, groups, A_KV_DIM, WINDOW), lambda i, l: (l, i, 0, 0))
    rows_spec = pl.BlockSpec((ROWS, D_MODEL), lambda i, l: (i, 0))
    in_specs = [
        rows_spec,
        per_layer((1, D_MODEL)), per_layer((P_COLS, D_MODEL)), per_layer((1, LANES)), per_layer((1, M_DIM)),
        per_layer((1, R_DIM)), pl.BlockSpec(memory_space=pltpu.SMEM), per_layer((D_MODEL, D_MODEL)),
        cs((1, D_MODEL)),
        cs((ROWS, SPLIT_TERMS * ROWS)), cs((ROWS, ROWS)), cs((2 * ROWS, PAIRS * ROWS)),
        cs((3, PAIRS, ROWS, LANES)), cs((A_HEADS, 4 * WINDOW)),
        cs((ROWS, LANES)), cs((ROWS, LANES)),
        cs((M_HEADS * ROWS, SPLIT_TERMS * LANES)), cs((PAIRS * LANES, SPLIT_TERMS * LANES)),
        st4, stn, stm, st4, stk, stk,
    ]
    out_shape = (
        jax.ShapeDtypeStruct((B * T, D_MODEL), f32),
        jax.ShapeDtypeStruct((depth, B, M_HEADS, HEAD_DIM, HEAD_DIM), f32),
        jax.ShapeDtypeStruct((depth, M_HEADS, B, HEAD_DIM), f32),
        jax.ShapeDtypeStruct((depth, B * T, LANES), f32),
        jax.ShapeDtypeStruct((depth, B, R_HEADS, HEAD_DIM, HEAD_DIM), f32),
        jax.ShapeDtypeStruct((depth, B, A_KV_DIM, WINDOW), f32),
        jax.ShapeDtypeStruct((depth, B, A_KV_DIM, WINDOW), f32),
    )
    out_specs = (rows_spec, st4, stn, stm, st4, stk, stk)
    kern = functools.partial(_sample_kernel, groups=groups, ret_full=tabs["full"])
    y, c, n, m, s, k, v = pl.pallas_call(
        kern, grid=(nb, depth), in_specs=in_specs, out_specs=out_specs, out_shape=out_shape,
        scratch_shapes=[pltpu.VMEM((ROWS, P_COLS), f32), pltpu.VMEM((ROWS, D_MODEL), bf16),
                        pltpu.VMEM((ROWS, D_MODEL), f32),
                        pltpu.VMEM((groups, A_HEADS * T, A_KV_DIM), bf16),
                        pltpu.VMEM((A_HEADS, ROWS, WINDOW), f32),
                        pltpu.VMEM((groups, A_HEADS * T, WINDOW), bf16),
                        pltpu.VMEM((A_HEADS, ROWS, A_KV_DIM), f32),
                        pltpu.VMEM((A_HEADS, ROWS, ROWS), f32), pltpu.VMEM((A_HEADS, ROWS, WINDOW), f32)],
        compiler_params=pltpu.CompilerParams(dimension_semantics=("arbitrary", "arbitrary"),
                                             vmem_limit_bytes=VMEM_LIMIT_BYTES),
        name="sample_path",
    )(x2, p["norm_gain"], p["w_in"], p["gbias"], p["m_gain"], p["r_gain"], p["sinks"], p["w_out"], p["fgain"],
      tabs["tril3"], tabs["maskadd"], tabs["dmat2"], tabs["rslab"], ubias,
      cos_t, sin_t, tabs["selh"], tabs["selp"], c0, n0t, m0r, s0, k0r, v0r)
    y = y.reshape(B, T, D_MODEL)
    n = jnp.transpose(n, (0, 2, 1, 3))
    m = m.reshape(depth, B, T, LANES)[:, :, 0, :M_HEADS]
    k = jnp.transpose(k.reshape(depth, B, A_KV_HEADS, HEAD_DIM, WINDOW), (0, 1, 4, 2, 3))
    v = jnp.transpose(v.reshape(depth, B, A_KV_HEADS, HEAD_DIM, WINDOW), (0, 1, 4, 2, 3))
    return y, c, n, m, s, k, v


def _prepare_params(norm_gain, w_in, mlstm_gate_bias, mlstm_norm_gain, ret_norm_gain, attn_sinks, w_out,
                    final_norm_gain):
    depth = w_in.shape[0]
    w_t = jnp.swapaxes(w_in, 1, 2)
    split = OFF_G + N_GATES
    aq0 = split + 4 * R_DIM
    akv0 = aq0 + A_DIM
    az0 = akv0 + 2 * A_KV_DIM

    def by_head(w):
        w = w.reshape(depth, A_HEADS, HEAD_DIM, D_MODEL)
        return jnp.concatenate([w[:, h] for h in ATTN_HEAD_ORDER], axis=1)

    w_in_p = jnp.concatenate(
        [w_t[:, :split], jnp.zeros((depth, GATE_PAD - N_GATES, D_MODEL), w_t.dtype), w_t[:, split:aq0],
         by_head(w_t[:, aq0:akv0]), w_t[:, akv0:az0], by_head(w_t[:, az0:])], axis=1).astype(bf16)
    wo16 = w_out.astype(bf16)
    a0 = M_DIM + R_DIM
    w_out_p = jnp.concatenate([wo16[:, :a0, :], by_head(wo16[:, a0:, :])], axis=1)
    gbias = jnp.pad(mlstm_gate_bias.reshape(depth, 1, N_GATES), ((0, 0), (0, 0), (0, LANES - N_GATES)))
    return dict(norm_gain=norm_gain.reshape(depth, 1, D_MODEL), w_in=w_in_p, gbias=gbias,
                m_gain=mlstm_norm_gain.reshape(depth, 1, M_DIM), r_gain=ret_norm_gain.reshape(depth, 1, R_DIM),
                sinks=attn_sinks, w_out=w_out_p, fgain=final_norm_gain.reshape(1, D_MODEL))


def kernel(x_prompt, x_sample, state_mlstm_C, state_mlstm_n, state_mlstm_m, state_ret_S, cache_win_k,
           cache_win_v, norm_gain, w_in, mlstm_gate_bias, mlstm_norm_gain, ret_norm_gain, attn_sinks,
           rel_bias_table, w_out, final_norm_gain):
    depth = w_in.shape[0]
    seq = x_prompt.shape[1]
    dec_seq = x_sample.shape[1]
    past_len = seq
    p = _prepare_params(norm_gain, w_in, mlstm_gate_bias, mlstm_norm_gain, ret_norm_gain, attn_sinks, w_out,
                        final_norm_gain)
    tabs_p = _static_tables(1)
    tabs_s = _static_tables(ROWS // dec_seq)
    ubias = _bias_vectors(rel_bias_table)
    rope_p = (*_rope_tables(jnp.arange(0, seq, ROWS, dtype=jnp.int32), signed=False),
              *_rope_tables(jnp.arange(ROWS, dtype=jnp.int32), signed=False))
    cos_s, sin_s = _rope_tables(past_len + (jnp.arange(ROWS, dtype=jnp.int32) % dec_seq))
    states = (state_mlstm_C, state_mlstm_n, state_mlstm_m, state_ret_S, cache_win_k, cache_win_v)

    xp = x_prompt
    p_states = []
    for layer in range(depth):
        xp, *sp = _prompt_layer(xp, p, layer, tabs_p, ubias, rope_p, layer == depth - 1)
        p_states.append(sp)
    outs_p = [jnp.stack([p_states[l][i] for l in range(depth)]) for i in range(6)]
    xs, *outs_s = _sample_path(x_sample, states, p, tabs_s, ubias, cos_s, sin_s)
    return (xp, xs, *outs_p, *outs_s)
```

```python
import functools

import numpy as np
import jax
import jax.numpy as jnp
from jax import lax
from jax.experimental import pallas as pl
from jax.experimental.pallas import tpu as pltpu

D_MODEL = 1024
HEAD_DIM = 64
M_HEADS = 4
R_HEADS = 4
A_HEADS = 8
A_KV_HEADS = 2
KV_GROUP = A_HEADS // A_KV_HEADS
M_DIM = M_HEADS * HEAD_DIM
R_DIM = R_HEADS * HEAD_DIM
A_DIM = A_HEADS * HEAD_DIM
A_KV_DIM = A_KV_HEADS * HEAD_DIM
WINDOW = 128
N_BUCKETS = 32
REL_MAX_DIST = 128
ROPE_BASE = 10000.0
NORM_EPS = 1e-6
QK_SCALE = HEAD_DIM ** -0.5

LANES = 128
ROWS = 128
GATE_PAD = LANES
PAIRS = M_HEADS // 2
SPLIT_TERMS = 3

OFF_MQ = 0
OFF_MK = OFF_MQ + M_DIM
OFF_MV = OFF_MK + M_DIM
OFF_MO = OFF_MV + M_DIM
OFF_MZ = OFF_MO + M_DIM
OFF_G = OFF_MZ + M_DIM
OFF_RQ = OFF_G + GATE_PAD
OFF_RK = OFF_RQ + R_DIM
OFF_RV = OFF_RK + R_DIM
OFF_RZ = OFF_RV + R_DIM
OFF_AQ = OFF_RZ + R_DIM
OFF_AK = OFF_AQ + A_DIM
OFF_AV = OFF_AK + A_KV_DIM
OFF_AZ = OFF_AV + A_KV_DIM
P_COLS = OFF_AZ + A_DIM
N_GATES = 2 * M_HEADS
PROJ_COL_BLOCK = 512
SCHEDULE = "FBFBFB"
ATTN_HEAD_ORDER = tuple(h for j in range(KV_GROUP) for h in (j, KV_GROUP + j))

PROMPT_ROWS = 512
VMEM_LIMIT_BYTES = 56 * 1024 * 1024

f32 = jnp.float32
bf16 = jnp.bfloat16


def _dot(a, b):
    return jnp.dot(a, b, preferred_element_type=f32)


def _dot_nt(a, b):
    return lax.dot_general(a, b, (((1,), (1,)), ((), ())), preferred_element_type=f32)


def _sigmoid(x):
    return 1.0 / (1.0 + jnp.exp(-x))


def _silu(x):
    return x * _sigmoid(x)


def _log_sigmoid(x):
    return jnp.minimum(x, 0.0) - jnp.log(1.0 + jnp.exp(-jnp.abs(x)))


def _split_parts(x, terms):
    parts, r = [], x
    for i in range(terms):
        p = r.astype(bf16)
        parts.append(p)
        if i + 1 < terms:
            r = r - p.astype(f32)
    return parts


def _split_terms(x, terms=SPLIT_TERMS):
    return jnp.concatenate(_split_parts(x, terms), axis=1)


def _exact_tril_dot(tril3, x):
    return _dot(tril3, jnp.concatenate(_split_parts(x, SPLIT_TERMS), axis=0))


def _rope(x, cos_t, sin_t, first_half):
    up = pltpu.roll(x, LANES - HEAD_DIM // 2, axis=1)
    down = pltpu.roll(x, HEAD_DIM // 2, axis=1)
    return x * cos_t + jnp.where(first_half, up, down) * sin_t


def _rms_project(x_ref, ngain_ref, win_ref, proj_sc):
    xf = x_ref[...]
    u = xf * lax.rsqrt(jnp.mean(xf * xf, axis=1, keepdims=True) + NORM_EPS) * ngain_ref[...]
    u16 = u.astype(bf16)
    for c0 in range(0, P_COLS, PROJ_COL_BLOCK):
        c1 = min(c0 + PROJ_COL_BLOCK, P_COLS)
        proj_sc[:, c0:c1] = _dot_nt(u16, win_ref[c0:c1, :])


def _prompt_kernel(x_ref, ngain_ref, win_ref, gbias_ref, mgain_ref, rgain_ref, sinks_ref, wout_ref,
                   fgain_ref, tril3_ref, maskadd_ref, dmat2_ref, rslab_ref, ubias_ref, cos_ref,
                   sin_ref, cosoff_ref, sinoff_ref, selh_ref, selp_ref,
                   y_ref, c_out, n_out, m_out, s_out, k_out, v_out,
                   proj_sc, mix_sc, cn_sc, sb_sc, m_sc, kp_sc, vp_sc, bias_sc, *, chunks, final):
    step = pl.program_id(1)
    last_step = pl.num_programs(1) - 1

    @pl.when(jnp.logical_and(pl.program_id(0) == 0, step == 0))
    def _():
        for blk in range(A_HEADS):
            u = ubias_ref[ATTN_HEAD_ORDER[blk]:ATTN_HEAD_ORDER[blk] + 1, :]
            bias_sc[blk * ROWS:(blk + 1) * ROWS, :] = jnp.concatenate(
                [_skew(u[:, :2 * WINDOW], ROWS), _skew(u[:, 2 * WINDOW:], ROWS)], axis=1)

    xf = x_ref[...]
    u16 = (xf * lax.rsqrt(jnp.mean(xf * xf, axis=1, keepdims=True) + NORM_EPS) * ngain_ref[...]).astype(bf16)
    col_blocks = [(c0, min(c0 + PROJ_COL_BLOCK, P_COLS)) for c0 in range(0, P_COLS, PROJ_COL_BLOCK)]
    half_rows = (chunks // 2) * ROWS if chunks > 1 else chunks * ROWS

    def project(r0, r1, c0, c1):
        proj_sc[r0:r1, c0:c1] = _dot_nt(u16[r0:r1], win_ref[c0:c1, :])

    for c0, c1 in col_blocks:
        project(0, half_rows, c0, c1)
    late_pieces = [(half_rows, chunks * ROWS, c0, c1) for c0, c1 in col_blocks] if half_rows < chunks * ROWS else []

    @pl.when(step == 0)
    def _():
        cn_sc[...] = jnp.zeros_like(cn_sc)
        sb_sc[...] = jnp.zeros_like(sb_sc)
        m_sc[...] = jnp.zeros_like(m_sc)
        kp_sc[...] = jnp.zeros_like(kp_sc)
        vp_sc[...] = jnp.zeros_like(vp_sc)

    lane = lax.broadcasted_iota(jnp.int32, (ROWS, LANES), 1)
    row = lax.broadcasted_iota(jnp.int32, (ROWS, LANES), 0)
    left = lane < HEAD_DIM
    first_half = (lane & (HEAD_DIM - 1)) < (HEAD_DIM // 2)
    head_col = lane < M_HEADS
    blockdiag = (row < HEAD_DIM) == left
    row2 = lax.broadcasted_iota(jnp.int32, (ROWS, 2 * LANES), 0)
    lane2w = lax.broadcasted_iota(jnp.int32, (ROWS, 2 * LANES), 1)
    left2 = (lane2w & (LANES - 1)) < HEAD_DIM
    blockdiag2 = (row2 < HEAD_DIM) == left2
    ones16 = jnp.ones((ROWS, LANES), bf16)
    halves, pick, pair_blockdiag, half_mean = _halves, _pick, _pair_blockdiag, _half_mean

    def chunk_body(ci):
        rows = slice(ci * ROWS, (ci + 1) * ROWS)

        def proj(off, width=LANES):
            return proj_sc[rows, off:off + width]

        gates = proj(OFF_G) + gbias_ref[...]
        bcum = _exact_tril_dot(tril3_ref[...], _log_sigmoid(gates))

        base = pl.ds(step * chunks + ci, 1)
        cos_a, sin_a = cos_ref[base, :], sin_ref[base, :]
        cos_b, sin_b = cosoff_ref[...], sinoff_ref[...]
        cos_t = cos_a * cos_b - sin_a * sin_b
        sin_t = sin_a * cos_b + cos_a * sin_b
        sin_t = jnp.where(first_half, -sin_t, sin_t)
        r_q = [_rope(proj(OFF_RQ + p * LANES), cos_t, sin_t, first_half) for p in range(PAIRS)]
        r_k = [_rope(proj(OFF_RK + p * LANES), cos_t, sin_t, first_half) * QK_SCALE for p in range(PAIRS)]
        r_vbd = pair_blockdiag([proj(OFF_RV + p * LANES).astype(bf16) for p in range(PAIRS)])
        r_sb = [sb_sc[p] for p in range(PAIRS)]
        r_sc = _dot_nt(jnp.concatenate([halves(q) for q in r_q], axis=1),
                       pair_blockdiag([k.astype(bf16) for k in r_k]))
        r_inter = _dot(jnp.concatenate([q.astype(bf16) for q in r_q], axis=1),
                       pair_blockdiag([sb.astype(bf16) for sb in r_sb]))
        r_upd = _dot(jnp.concatenate([(r_k[p] * rslab_ref[1, p]).T.astype(bf16) for p in range(PAIRS)], axis=1),
                     r_vbd)
        for p in range(PAIRS):
            sb_sc[p] = rslab_ref[2, p] * r_sb[p] + jnp.where(blockdiag, r_upd[:, p * LANES:(p + 1) * LANES], 0.0)
        yield

        kcur, vcur = proj(OFF_AK), proj(OFF_AV)
        kprev, vprev = kp_sc[...], vp_sc[...]
        kk16 = jnp.concatenate([kcur, kprev], axis=0).astype(bf16)
        vv16 = jnp.concatenate([jnp.concatenate([vcur.astype(bf16), ones16], axis=1),
                                jnp.concatenate([vprev.astype(bf16), ones16], axis=1)], axis=0)
        kp_sc[...] = kcur
        vp_sc[...] = vcur
        a_q = jnp.concatenate([halves(proj(OFF_AQ + j * LANES) * QK_SCALE) for j in range(KV_GROUP)], axis=0)
        a_s = _dot_nt(a_q, kk16) + bias_sc[...]
        if ci == 0:
            pen = jnp.where(step == 0, -jnp.inf, 0.0).astype(f32)
            a_s = a_s + jnp.where(lax.broadcasted_iota(jnp.int32, (1, 2 * ROWS), 1) >= ROWS, pen, 0.0)

        m_q = [proj(OFF_MQ + p * LANES) for p in range(PAIRS)]
        m_k = [proj(OFF_MK + p * LANES) * QK_SCALE for p in range(PAIRS)]
        m_qk = _dot_nt(jnp.concatenate([halves(q) for q in m_q], axis=1),
                       pair_blockdiag([k.astype(bf16) for k in m_k]))
        m_q16 = [q.astype(bf16) for q in m_q]
        m_v16 = [proj(OFF_MV + p * LANES).astype(bf16) for p in range(PAIRS)]
        yield

        zb = pltpu.roll(bcum, LANES - M_HEADS, axis=1)
        r_mat = jnp.where(head_col, gates - zb, 0.0)
        cm = r_mat
        sh = 1
        while sh < ROWS:
            cm = jnp.where(row >= sh, jnp.maximum(cm, pltpu.roll(cm, sh, axis=0)), cm)
            sh *= 2
        mprev = m_sc[...]
        mx = jnp.maximum(mprev, cm)
        gm = mprev - mx
        em = jnp.where(head_col, -(zb + mx), 0.0)
        mx_last = jnp.broadcast_to(mx[ROWS - 1:ROWS, :], (ROWS, LANES))
        m_sc[...] = jnp.where(head_col, jnp.broadcast_to((zb + mx)[ROWS - 1:ROWS, :], (ROWS, LANES)), 0.0)
        mx_b = _dot_nt(_split_terms(mx), selh_ref[...])
        slabs = jnp.exp(_dot_nt(_split_terms(jnp.concatenate([gm, em, r_mat - mx_last], axis=0)),
                                selp_ref[...]))
        winter_b, emt_b, ws_b = slabs[:ROWS], slabs[ROWS:2 * ROWS], slabs[2 * ROWS:]
        r_t = r_mat.T
        yield

        outs = []
        r_acc = []
        r_o = _dot((r_sc * dmat2_ref[...]).astype(bf16), r_vbd)
        for p in range(PAIRS):
            ps = slice(p * LANES, (p + 1) * LANES)
            r_acc.append(pick(r_o[:, ps]) + rslab_ref[0, p] * r_inter[:, ps])

        a_out = []
        a_p = []
        for blk in range(A_HEADS):
            s = a_s[blk * ROWS:(blk + 1) * ROWS]
            sink = sinks_ref[ATTN_HEAD_ORDER[blk]]
            m = jnp.maximum(jnp.max(jnp.maximum(s[:, :ROWS], s[:, ROWS:]), axis=1, keepdims=True), sink)
            a_p.append(jnp.exp(s - m).astype(bf16))
            a_out.append(jnp.exp(sink - m))
        a_pv = _dot(jnp.concatenate(a_p, axis=0), vv16)
        yield

        maskadd = maskadd_ref[...]
        for p in range(PAIRS):
            ps = slice(p * LANES, (p + 1) * LANES)
            cn = cn_sc[p]
            w = jnp.concatenate(
                [jnp.exp((r_t[2 * p + side:2 * p + side + 1, :] + maskadd)
                         - mx_b[:, (2 * p + side) * ROWS:(2 * p + side + 1) * ROWS]) for side in range(2)],
                axis=0) * m_qk[:, ps]
            acc = (pick(_dot(w.astype(bf16), jnp.concatenate([m_v16[p], ones16], axis=1)))
                   + jnp.concatenate([winter_b[:, ps]] * 2, axis=1) * _dot(m_q16[p], cn.astype(bf16)))
            hh = acc[:, :LANES] / jnp.maximum(jnp.abs(acc[:, LANES:]), emt_b[:, ps])
            outs.append(_sigmoid(proj(OFF_MO + p * LANES)) * hh)
            kwt16 = (m_k[p] * ws_b[:, ps]).T.astype(bf16)
            dcn = _dot(kwt16, jnp.concatenate([m_v16[p], ones16], axis=1))
            decay = winter_b[ROWS - 1:ROWS, ps]
            cn_sc[p] = jnp.concatenate([decay, decay], axis=1) * cn + jnp.where(blockdiag2, dcn, 0.0)
        outs.extend(r_acc)
        yield

        x4 = jnp.concatenate(outs, axis=0)
        xc = x4 - half_mean(x4)
        y4 = xc * lax.rsqrt(half_mean(xc * xc) + NORM_EPS)
        for i in range(2 * PAIRS):
            gain = (mgain_ref if i < PAIRS else rgain_ref)[:, (i % PAIRS) * LANES:(i % PAIRS + 1) * LANES]
            zoff = (OFF_MZ if i < PAIRS else OFF_RZ) + (i % PAIRS) * LANES
            out = y4[i * ROWS:(i + 1) * ROWS] * gain * _silu(proj(zoff))
            mix_sc[rows, i * LANES:(i + 1) * LANES] = out.astype(bf16)
        for j in range(KV_GROUP):
            acc = pick(a_pv[2 * j * ROWS:(2 * j + 2) * ROWS])
            den = acc[:, LANES:] + jnp.where(left, a_out[2 * j], a_out[2 * j + 1])
            out = (acc[:, :LANES] / den) * _silu(proj(OFF_AZ + j * LANES))
            mix_sc[rows, M_DIM + R_DIM + j * LANES:M_DIM + R_DIM + (j + 1) * LANES] = out.astype(bf16)

    def out_project(r0, r1, c0, c1):
        y_ref[r0:r1, c0:c1] = x_ref[r0:r1, c0:c1] + _dot(mix_sc[r0:r1, :], wout_ref[:, c0:c1])

    out_blocks = [(c0, min(c0 + PROJ_COL_BLOCK, D_MODEL)) for c0 in range(0, D_MODEL, PROJ_COL_BLOCK)]
    early_out = [(0, half_rows, c0, c1) for c0, c1 in out_blocks] if half_rows < chunks * ROWS else []
    final_out = [(half_rows if early_out else 0, chunks * ROWS, c0, c1) for c0, c1 in out_blocks]

    def fill_mxu(ci):
        if late_pieces:
            project(*late_pieces.pop(0))
        elif early_out and (ci - 1) * ROWS >= half_rows:
            out_project(*early_out.pop(0))

    parts = [chunk_body(ci) for ci in range(chunks)]
    for ci in range(chunks + 1):
        if ci * ROWS >= half_rows:
            while late_pieces:
                project(*late_pieces.pop(0))
        for which in SCHEDULE:
            if which == "F" and ci < chunks:
                next(parts[ci])
                fill_mxu(ci)
            if which == "B" and ci > 0:
                next(parts[ci - 1], None)
                fill_mxu(ci)
    for piece in early_out + final_out:
        out_project(*piece)
    if final:
        y = y_ref[...]
        y_ref[...] = y * lax.rsqrt(jnp.mean(y * y, axis=1, keepdims=True) + NORM_EPS) * fgain_ref[...]

    @pl.when(step == last_step)
    def _():
        for p in range(PAIRS):
            cn = cn_sc[p]
            sb = sb_sc[p]
            n_t = cn[:, LANES:].T
            for side in range(2):
                h = 2 * p + side
                blk = slice(side * HEAD_DIM, (side + 1) * HEAD_DIM)
                c_out[0, h] = cn[blk, blk]
                s_out[0, h] = sb[blk, blk]
                n_out[0, h:h + 1, :] = n_t[side * HEAD_DIM:side * HEAD_DIM + 1, blk]
        m_out[0] = m_sc[...]
        k_out[0] = kp_sc[...]
        v_out[0] = vp_sc[...]


def _lane_is_left(shape):
    return (lax.broadcasted_iota(jnp.int32, shape, 1) & (LANES - 1)) < HEAD_DIM


def _halves(x):
    left = _lane_is_left(x.shape)
    return jnp.concatenate([jnp.where(left, x, 0.0), jnp.where(left, 0.0, x)], axis=0).astype(bf16)


def _pick(x):
    return jnp.where(_lane_is_left((ROWS, x.shape[1])), x[:ROWS], x[ROWS:])


def _pair_blockdiag(blocks):
    z = jnp.zeros_like(blocks[0])
    return jnp.concatenate(
        [jnp.concatenate([blk if j == i else z for j in range(len(blocks))], axis=1)
         for i, blk in enumerate(blocks)], axis=0)


def _half_mean(x):
    left = _lane_is_left(x.shape)
    s_left = jnp.sum(jnp.where(left, x, 0.0), axis=1, keepdims=True)
    s_right = jnp.sum(jnp.where(left, 0.0, x), axis=1, keepdims=True)
    return jnp.where(left, s_left, s_right) * (1.0 / HEAD_DIM)


def _group_last(x, groups):
    n = x.shape[1]
    glen = ROWS // groups
    x3 = x.reshape(groups, glen, n)
    return jnp.broadcast_to(x3[:, glen - 1:glen, :], (groups, glen, n)).reshape(ROWS, n)


def _sample_kernel(x_ref, ngain_ref, win_ref, gbias_ref, mgain_ref, rgain_ref, sinks_ref, wout_ref,
                   fgain_ref, tril3_ref, maskadd_ref, dmat2_ref, rslab_ref, ubias_ref, cos_ref, sin_ref,
                   selh_ref, selp_ref,
                   c_in, n_in, m_in, s_in, k_in, v_in,
                   y_ref, c_out, n_out, m_out, s_out, k_out, v_out,
                   proj_sc, mix_sc, xcur_sc, qb_sc, sp_sc, pp_sc, ob_sc, biasc_ref, biasp_ref, *, groups,
                   ret_full):
    glen = ROWS // groups
    glen_log2 = glen.bit_length() - 1
    hd_log2 = HEAD_DIM.bit_length() - 1
    step = pl.program_id(0)
    layer = pl.program_id(1)
    last_layer = pl.num_programs(1) - 1

    @pl.when(layer == 0)
    def _():
        xcur_sc[...] = x_ref[...]

    for b in range(groups):
        k_out[b] = pltpu.roll(k_in[b], WINDOW - glen, axis=1)
        v_out[b] = pltpu.roll(v_in[b], WINDOW - glen, axis=1)

    _rms_project(xcur_sc, ngain_ref, win_ref, proj_sc)

    @pl.when(jnp.logical_and(step == 0, layer == 0))
    def _():
        for h in range(A_HEADS):
            u = ubias_ref[h:h + 1, :]
            biasc_ref[h] = _skew(u[:, :2 * WINDOW], ROWS) + maskadd_ref[...]
            biasp_ref[h] = jnp.concatenate([_skew(u[:, 2 * WINDOW:], glen)] * groups, axis=0)

    lane = lax.broadcasted_iota(jnp.int32, (ROWS, LANES), 1)
    first_half = (lane & (HEAD_DIM - 1)) < (HEAD_DIM // 2)
    rows = slice(0, ROWS)

    r_i = lax.broadcasted_iota(jnp.int32, (ROWS, groups * HEAD_DIM), 0)
    c_i = lax.broadcasted_iota(jnp.int32, (ROWS, groups * HEAD_DIM), 1)
    blk = (r_i >> glen_log2) == (c_i >> hd_log2)
    r_t = lax.broadcasted_iota(jnp.int32, (groups * HEAD_DIM, ROWS), 0)
    c_t = lax.broadcasted_iota(jnp.int32, (groups * HEAD_DIM, ROWS), 1)
    blk_t = (r_t >> hd_log2) == (c_t >> glen_log2)

    def q_times_state(qh, st):
        qt = jnp.where(blk, jnp.concatenate([qh] * groups, axis=1), 0.0)
        return _dot(qt.astype(bf16), st.astype(bf16))

    def state_increment(kt_h, vh16):
        kt = jnp.where(blk_t, jnp.concatenate([kt_h] * groups, axis=0), 0.0)
        return _dot(kt.astype(bf16), vh16)

    def proj(off, width=LANES):
        return proj_sc[:, off:off + width]

    def head_state_rows(slab, side):
        wide = slab[:, side * HEAD_DIM:(side + 1) * HEAD_DIM].reshape(groups, glen, HEAD_DIM)[:, 0:1, :]
        rows_ = jnp.broadcast_to(wide, (groups, HEAD_DIM, HEAD_DIM)).reshape(groups * HEAD_DIM, HEAD_DIM)
        return rows_, wide.reshape(groups, HEAD_DIM)

    row = lax.broadcasted_iota(jnp.int32, (ROWS, LANES), 0)
    tau = row & (glen - 1)
    head_col = lane < M_HEADS
    left = lane < HEAD_DIM
    ones16 = jnp.ones((ROWS, LANES), bf16)
    gates = proj(OFF_G) + gbias_ref[...]
    bcum = _exact_tril_dot(tril3_ref[...], _log_sigmoid(gates))
    zb = pltpu.roll(bcum, LANES - M_HEADS, axis=1)
    r_mat = jnp.where(head_col, gates - zb, 0.0)
    cm = r_mat
    sh = 1
    while sh < glen:
        cm = jnp.where(tau >= sh, jnp.maximum(cm, pltpu.roll(cm, sh, axis=0)), cm)
        sh *= 2
    mprev = m_in[...]
    mx = jnp.maximum(mprev, cm)
    gm = mprev - mx
    em = jnp.where(head_col, -(zb + mx), 0.0)
    mx_last = _group_last(mx, groups)
    m_out[...] = jnp.where(head_col, _group_last(zb + mx, groups), 0.0)
    mx_b = _dot_nt(_split_terms(mx), selh_ref[...])
    slabs = jnp.exp(_dot_nt(_split_terms(jnp.concatenate([gm, em, r_mat - mx_last], axis=0)), selp_ref[...]))
    winter_b, emt_b, ws_b = slabs[:ROWS], slabs[ROWS:2 * ROWS], slabs[2 * ROWS:]
    decay_b = _group_last(winter_b, groups)
    r_t = r_mat.T
    maskadd = maskadd_ref[...]

    m_q = [proj(OFF_MQ + p * LANES) for p in range(PAIRS)]
    m_k = [proj(OFF_MK + p * LANES) * QK_SCALE for p in range(PAIRS)]
    m_v = [proj(OFF_MV + p * LANES) for p in range(PAIRS)]
    m_qk = _dot_nt(jnp.concatenate([_halves(q) for q in m_q], axis=1),
                   _pair_blockdiag([k.astype(bf16) for k in m_k]))
    outs = []
    for p in range(PAIRS):
        ps = slice(p * LANES, (p + 1) * LANES)
        w = jnp.concatenate(
            [jnp.exp((r_t[2 * p + side:2 * p + side + 1, :] + maskadd)
                     - mx_b[:, (2 * p + side) * ROWS:(2 * p + side + 1) * ROWS]) for side in range(2)],
            axis=0) * m_qk[:, ps]
        intra = _pick(_dot(w.astype(bf16), jnp.concatenate([m_v[p].astype(bf16), ones16], axis=1)))
        kw = m_k[p] * ws_b[:, ps]
        kwt = kw.T
        q_c, q_n = [], []
        for side in range(2):
            h = 2 * p + side
            hs = slice(side * HEAD_DIM, (side + 1) * HEAD_DIM)
            qh = m_q[p][:, hs]
            c_h = c_in[:, h].reshape(groups * HEAD_DIM, HEAD_DIM)
            n_g = n_in[h]
            n_rows = jnp.broadcast_to(n_g.reshape(groups, 1, HEAD_DIM),
                                      (groups, glen, HEAD_DIM)).reshape(ROWS, HEAD_DIM)
            q_c.append(q_times_state(qh, c_h))
            q_n.append(jnp.sum(qh * n_rows, axis=1, keepdims=True))
            dec_rows, dec_g = head_state_rows(decay_b[:, ps], side)
            c_new = dec_rows * c_h + state_increment(kwt[hs, :], m_v[p][:, hs].astype(bf16))
            c_out[:, h] = c_new.reshape(groups, HEAD_DIM, HEAD_DIM)
            n_out[h] = dec_g * n_g + jnp.sum(kw[:, hs].reshape(groups, glen, HEAD_DIM), axis=1)
        wb = winter_b[:, ps]
        num = intra[:, :LANES] + wb * jnp.concatenate(q_c, axis=1)
        nq = intra[:, LANES:] + wb * jnp.where(left, q_n[0], q_n[1])
        outs.append(_sigmoid(proj(OFF_MO + p * LANES)) * (num / jnp.maximum(jnp.abs(nq), emt_b[:, ps])))

    cos_t, sin_t = cos_ref[...], sin_ref[...]
    r_q = [_rope(proj(OFF_RQ + p * LANES), cos_t, sin_t, first_half) for p in range(PAIRS)]
    r_k = [_rope(proj(OFF_RK + p * LANES), cos_t, sin_t, first_half) * QK_SCALE for p in range(PAIRS)]
    r_v = [proj(OFF_RV + p * LANES) for p in range(PAIRS)]
    r_sc = _dot_nt(jnp.concatenate([_halves(q) for q in r_q], axis=1),
                   _pair_blockdiag([k.astype(bf16) for k in r_k]))
    r_o = _dot((r_sc * dmat2_ref[...]).astype(bf16), _pair_blockdiag([v.astype(bf16) for v in r_v]))
    for p in range(PAIRS):
        ps = slice(p * LANES, (p + 1) * LANES)
        rkt = (r_k[p] * rslab_ref[1, p]).T
        q_s = []
        for side in range(2):
            h = 2 * p + side
            hs = slice(side * HEAD_DIM, (side + 1) * HEAD_DIM)
            s_h = s_in[:, h].reshape(groups * HEAD_DIM, HEAD_DIM)
            q_s.append(q_times_state(r_q[p][:, hs], s_h))
            s_new = ret_full[h] * s_h + state_increment(rkt[hs, :], r_v[p][:, hs].astype(bf16))
            s_out[:, h] = s_new.reshape(groups, HEAD_DIM, HEAD_DIM)
        outs.append(_pick(r_o[:, ps]) + rslab_ref[0, p] * jnp.concatenate(q_s, axis=1))

    x4 = jnp.concatenate(outs, axis=0)
    xc = x4 - _half_mean(x4)
    y4 = xc * lax.rsqrt(_half_mean(xc * xc) + NORM_EPS)
    for i in range(2 * PAIRS):
        gain = (mgain_ref if i < PAIRS else rgain_ref)[:, (i % PAIRS) * LANES:(i % PAIRS + 1) * LANES]
        zoff = (OFF_MZ if i < PAIRS else OFF_RZ) + (i % PAIRS) * LANES
        mix_sc[:, i * LANES:(i + 1) * LANES] = (y4[i * ROWS:(i + 1) * ROWS] * gain * _silu(proj(zoff))).astype(bf16)

    kcur = proj_sc[rows, OFF_AK:OFF_AK + A_KV_DIM]
    vcur = proj_sc[rows, OFF_AV:OFF_AV + A_KV_DIM]
    kcur16, vcur16 = kcur.astype(bf16), vcur.astype(bf16)

    a_q = {}
    for j in range(KV_GROUP):
        hq = _halves(proj(OFF_AQ + j * LANES) * QK_SCALE)
        a_q[j], a_q[KV_GROUP + j] = hq[:ROWS], hq[ROWS:]
    for h in range(A_HEADS):
        qb_sc[:, h * glen:(h + 1) * glen, :] = a_q[h].reshape(groups, glen, LANES)
    sc_all = _dot_nt(jnp.concatenate([a_q[h] for h in range(A_HEADS)], axis=0), kcur16)

    for b in range(groups):
        sp = _dot(qb_sc[b], k_in[b].astype(bf16))
        sp_sc[:, b * glen:(b + 1) * glen, :] = sp.reshape(A_HEADS, glen, WINDOW)

    vcur1 = jnp.concatenate([vcur16, ones16], axis=1)
    esink, o_cur = [], []
    for h in range(A_HEADS):
        sc = sc_all[h * ROWS:(h + 1) * ROWS] + biasc_ref[h]
        sp = sp_sc[h] + biasp_ref[h]
        sink = sinks_ref[layer, h]
        m = jnp.maximum(jnp.max(jnp.maximum(sc, sp), axis=1, keepdims=True), sink)
        esink.append(jnp.exp(sink - m))
        o_cur.append(_dot(jnp.exp(sc - m).astype(bf16), vcur1))
        pp_sc[:, h * glen:(h + 1) * glen, :] = jnp.exp(sp - m).reshape(groups, glen, WINDOW).astype(bf16)

    for b in range(groups):
        v1 = jnp.concatenate([v_in[b].astype(bf16), ones16], axis=0)
        ob = _dot_nt(pp_sc[b], v1)
        ob_sc[:, b * glen:(b + 1) * glen, :] = ob.reshape(A_HEADS, glen, 2 * LANES)
    norm = []
    for h in range(A_HEADS):
        acc = o_cur[h] + ob_sc[h]
        norm.append(acc[:, :LANES] / (acc[:, LANES:] + esink[h]))
    outs = [jnp.where(left, norm[j], norm[KV_GROUP + j]) for j in range(KV_GROUP)]
    kcur_t, vcur_t = kcur.T, vcur.T
    new = slice(WINDOW - glen, WINDOW)
    for b in range(groups):
        shift = (WINDOW - glen - b * glen) % LANES
        k_out[b, :, new] = pltpu.roll(kcur_t, shift, axis=1)[:, new]
        v_out[b, :, new] = pltpu.roll(vcur_t, shift, axis=1)[:, new]
    out_a = jnp.concatenate(outs, axis=1) * _silu(proj_sc[rows, OFF_AZ:OFF_AZ + A_DIM])
    mix_sc[rows, M_DIM + R_DIM:M_DIM + R_DIM + A_DIM] = out_a.astype(bf16)

    y = xcur_sc[...] + _dot(mix_sc[...], wout_ref[...])
    xcur_sc[...] = y

    @pl.when(layer == last_layer)
    def _():
        y_ref[...] = y * lax.rsqrt(jnp.mean(y * y, axis=1, keepdims=True) + NORM_EPS) * fgain_ref[...]


def _t5_bucket(dist):
    max_exact = N_BUCKETS // 2
    d = np.maximum(dist, 1).astype(np.float32)
    large = max_exact + (np.log(d / max_exact) / np.log(REL_MAX_DIST / max_exact)
                         * (N_BUCKETS - max_exact)).astype(np.int32)
    large = np.minimum(large, N_BUCKETS - 1)
    return np.where(dist < max_exact, dist, large).astype(np.int32)


def _static_tables(groups):
    glen = ROWS // groups
    r = np.arange(ROWS)
    grp, tau = r // glen, r % glen
    causal = (grp[:, None] == grp[None, :]) & (tau[None, :] <= tau[:, None])
    tril = causal.astype(np.float32)
    maskadd = np.where(causal, 0.0, -np.inf).astype(np.float32)
    log_g = np.log1p(-np.exp2(-5.0 - np.arange(R_HEADS, dtype=np.float64)))
    diff = (tau[:, None] - tau[None, :]).astype(np.float64)
    dmat = np.where(causal[None], np.exp(log_g[:, None, None] * np.maximum(diff, 0.0)[None]), 0.0)
    inter = np.exp(log_g[None, :] * (tau[:, None] + 1.0))
    tail = np.exp(log_g[None, :] * (glen - 1.0 - tau[:, None]))
    full = np.exp(log_g * glen)
    lane_head = np.arange(LANES) // HEAD_DIM
    rslab = np.zeros((3, PAIRS, ROWS, LANES), np.float64)
    for p in range(PAIRS):
        rslab[0, p] = inter[:, 2 * p + lane_head]
        rslab[1, p] = tail[:, 2 * p + lane_head]
        rslab[2, p] = full[2 * p + lane_head][None, :]
    selh = np.zeros((M_HEADS * ROWS, SPLIT_TERMS * LANES), np.float32)
    selp = np.zeros((PAIRS * LANES, SPLIT_TERMS * LANES), np.float32)
    for t in range(SPLIT_TERMS):
        for h in range(M_HEADS):
            selh[h * ROWS:(h + 1) * ROWS, t * LANES + h] = 1.0
        for p in range(PAIRS):
            for side in range(2):
                selp[p * LANES + side * HEAD_DIM:p * LANES + (side + 1) * HEAD_DIM, t * LANES + 2 * p + side] = 1.0
    return dict(tril3=jnp.asarray(np.concatenate([tril] * SPLIT_TERMS, axis=1), bf16),
                maskadd=maskadd,
                dmat2=np.concatenate(list(dmat.astype(np.float32).reshape(PAIRS, 2 * ROWS, ROWS)), axis=1),
                rslab=rslab.astype(np.float32),
                full=tuple(float(v) for v in full),
                selh=jnp.asarray(selh, bf16), selp=jnp.asarray(selp, bf16))


def _bias_vectors(rel_table):
    tb = jnp.transpose(rel_table[_t5_bucket(np.arange(WINDOW))]).astype(f32)
    ninf = jnp.full((A_HEADS, WINDOW), -jnp.inf, f32)
    rev = tb[:, :0:-1]
    return jnp.concatenate([tb[:, :1], ninf, rev, ninf[:, :1], rev, ninf], axis=1)


def _skew(u_row, rows):
    x = jnp.broadcast_to(u_row, (rows, 2 * WINDOW))
    return pltpu.roll(x, 0, 1, stride=1, stride_axis=0)[:, :WINDOW]


def _rope_tables(pos, signed=True):
    half = HEAD_DIM // 2
    inv = ROPE_BASE ** (-jnp.arange(half, dtype=f32) / half)
    ang = pos.astype(f32)[:, None] * inv[None, :]
    cos, sin = jnp.cos(ang), jnp.sin(ang)
    reps = LANES // HEAD_DIM
    cos_t = jnp.tile(jnp.concatenate([cos, cos], axis=1), (1, reps))
    sin_t = jnp.tile(jnp.concatenate([-sin if signed else sin, sin], axis=1), (1, reps))
    return cos_t, sin_t


def _const_spec(shape, nargs):
    zeros = (0,) * len(shape)
    if nargs == 1:
        return pl.BlockSpec(shape, lambda i: zeros)
    return pl.BlockSpec(shape, lambda i, j: zeros)


def _layer_spec(shape, layer, nargs):
    idx = (layer,) + (0,) * len(shape)
    if nargs == 1:
        return pl.BlockSpec((None,) + shape, lambda i: idx)
    return pl.BlockSpec((None,) + shape, lambda i, j: idx)


def _param_specs(layer, nargs):
    ls = functools.partial(_layer_spec, layer=layer, nargs=nargs)
    return [ls((1, D_MODEL)), ls((P_COLS, D_MODEL)), ls((1, LANES)), ls((1, M_DIM)), ls((1, R_DIM)),
            pl.BlockSpec(memory_space=pltpu.SMEM), ls((D_MODEL, D_MODEL)), _const_spec((1, D_MODEL), nargs)]


def _param_args(p, layer):
    return (p["norm_gain"], p["w_in"], p["gbias"], p["m_gain"], p["r_gain"], p["sinks"][layer], p["w_out"],
            p["fgain"])


def _prompt_layer(x, p, layer, tabs, ubias, rope, final):
    B, T, _ = x.shape
    tb = min(PROMPT_ROWS, T)
    chunks = tb // ROWS
    nt = T // tb
    cs = functools.partial(_const_spec, nargs=2)
    in_specs = [pl.BlockSpec((None, tb, D_MODEL), lambda b, t: (b, t, 0))] + _param_specs(layer, 2) + [
        cs((ROWS, SPLIT_TERMS * ROWS)), cs((ROWS, ROWS)), cs((2 * ROWS, PAIRS * ROWS)),
        cs((3, PAIRS, ROWS, LANES)), cs((A_HEADS, 4 * WINDOW)),
        cs((T // ROWS, LANES)), cs((T // ROWS, LANES)), cs((ROWS, LANES)), cs((ROWS, LANES)),
        cs((M_HEADS * ROWS, SPLIT_TERMS * LANES)), cs((PAIRS * LANES, SPLIT_TERMS * LANES)),
    ]
    out_shape = (
        jax.ShapeDtypeStruct((B, T, D_MODEL), f32),
        jax.ShapeDtypeStruct((B, M_HEADS, HEAD_DIM, HEAD_DIM), f32),
        jax.ShapeDtypeStruct((B, M_HEADS, HEAD_DIM), f32),
        jax.ShapeDtypeStruct((B, ROWS, LANES), f32),
        jax.ShapeDtypeStruct((B, R_HEADS, HEAD_DIM, HEAD_DIM), f32),
        jax.ShapeDtypeStruct((B, WINDOW, A_KV_DIM), f32),
        jax.ShapeDtypeStruct((B, WINDOW, A_KV_DIM), f32),
    )
    out_specs = (
        pl.BlockSpec((None, tb, D_MODEL), lambda b, t: (b, t, 0)),
        pl.BlockSpec((1, M_HEADS, HEAD_DIM, HEAD_DIM), lambda b, t: (b, 0, 0, 0)),
        pl.BlockSpec((1, M_HEADS, HEAD_DIM), lambda b, t: (b, 0, 0)),
        pl.BlockSpec((1, ROWS, LANES), lambda b, t: (b, 0, 0)),
        pl.BlockSpec((1, R_HEADS, HEAD_DIM, HEAD_DIM), lambda b, t: (b, 0, 0, 0)),
        pl.BlockSpec((1, WINDOW, A_KV_DIM), lambda b, t: (b, 0, 0)),
        pl.BlockSpec((1, WINDOW, A_KV_DIM), lambda b, t: (b, 0, 0)),
    )
    kern = functools.partial(_prompt_kernel, chunks=chunks, final=final)
    y, c, n, m, s, k, v = pl.pallas_call(
        kern, grid=(B, nt), in_specs=in_specs, out_specs=out_specs, out_shape=out_shape,
        scratch_shapes=[pltpu.VMEM((tb, P_COLS), f32), pltpu.VMEM((tb, D_MODEL), bf16),
                        pltpu.VMEM((PAIRS, ROWS, 2 * LANES), f32), pltpu.VMEM((PAIRS, ROWS, LANES), f32),
                        pltpu.VMEM((ROWS, LANES), f32), pltpu.VMEM((ROWS, A_KV_DIM), f32),
                        pltpu.VMEM((ROWS, A_KV_DIM), f32), pltpu.VMEM((A_HEADS * ROWS, 2 * WINDOW), f32)],
        compiler_params=pltpu.CompilerParams(dimension_semantics=("arbitrary", "arbitrary"),
                                             vmem_limit_bytes=VMEM_LIMIT_BYTES),
        name="prompt_layer",
    )(x, *_param_args(p, layer), tabs["tril3"], tabs["maskadd"], tabs["dmat2"], tabs["rslab"], ubias,
      *rope, tabs["selh"], tabs["selp"])
    k = k.reshape(B, WINDOW, A_KV_HEADS, HEAD_DIM)
    v = v.reshape(B, WINDOW, A_KV_HEADS, HEAD_DIM)
    return y, c, n, m[:, 0, :M_HEADS], s, k, v


def _sample_path(x, states, p, tabs, ubias, cos_t, sin_t):
    B, T, _ = x.shape
    groups = ROWS // T
    nb = B // groups
    c0, n0, m0, s0, k0, v0 = states
    depth = c0.shape[0]
    x2 = x.reshape(B * T, D_MODEL)
    n0t = jnp.transpose(n0, (0, 2, 1, 3))
    m0r = jnp.pad(jnp.repeat(m0, T, axis=1), ((0, 0), (0, 0), (0, LANES - M_HEADS)))
    k0r = jnp.transpose(k0, (0, 1, 3, 4, 2)).reshape(depth, B, A_KV_DIM, WINDOW)
    v0r = jnp.transpose(v0, (0, 1, 3, 4, 2)).reshape(depth, B, A_KV_DIM, WINDOW)

    def cs(shape):
        zeros = (0,) * len(shape)
        return pl.BlockSpec(shape, lambda i, l: zeros)

    def per_layer(shape):
        zeros = (0,) * len(shape)
        return pl.BlockSpec((None,) + shape, lambda i, l: (l,) + zeros)

    st4 = pl.BlockSpec((None, groups, M_HEADS, HEAD_DIM, HEAD_DIM), lambda i, l: (l, i, 0, 0, 0))
    stn = pl.BlockSpec((None, M_HEADS, groups, HEAD_DIM), lambda i, l: (l, 0, i, 0))
    stm = pl.BlockSpec((None, ROWS, LANES), lambda i, l: (l, i, 0))
    stk = pl.BlockSpec((None, groups, A_KV_DIM, WINDOW), lambda i, l: (l, i, 0, 0))
    rows_spec = pl.BlockSpec((ROWS, D_MODEL), lambda i, l: (i, 0))
    in_specs = [
        rows_spec,
        per_layer((1, D_MODEL)), per_layer((P_COLS, D_MODEL)), per_layer((1, LANES)), per_layer((1, M_DIM)),
        per_layer((1, R_DIM)), pl.BlockSpec(memory_space=pltpu.SMEM), per_layer((D_MODEL, D_MODEL)),
        cs((1, D_MODEL)),
        cs((ROWS, SPLIT_TERMS * ROWS)), cs((ROWS, ROWS)), cs((2 * ROWS, PAIRS * ROWS)),
        cs((3, PAIRS, ROWS, LANES)), cs((A_HEADS, 4 * WINDOW)),
        cs((ROWS, LANES)), cs((ROWS, LANES)),
        cs((M_HEADS * ROWS, SPLIT_TERMS * LANES)), cs((PAIRS * LANES, SPLIT_TERMS * LANES)),
        st4, stn, stm, st4, stk, stk,
    ]
    out_shape = (
        jax.ShapeDtypeStruct((B * T, D_MODEL), f32),
        jax.ShapeDtypeStruct((depth, B, M_HEADS, HEAD_DIM, HEAD_DIM), f32),
        jax.ShapeDtypeStruct((depth, M_HEADS, B, HEAD_DIM), f32),
        jax.ShapeDtypeStruct((depth, B * T, LANES), f32),
        jax.ShapeDtypeStruct((depth, B, R_HEADS, HEAD_DIM, HEAD_DIM), f32),
        jax.ShapeDtypeStruct((depth, B, A_KV_DIM, WINDOW), f32),
        jax.ShapeDtypeStruct((depth, B, A_KV_DIM, WINDOW), f32),
    )
    out_specs = (rows_spec, st4, stn, stm, st4, stk, stk)
    kern = functools.partial(_sample_kernel, groups=groups, ret_full=tabs["full"])
    y, c, n, m, s, k, v = pl.pallas_call(
        kern, grid=(nb, depth), in_specs=in_specs, out_specs=out_specs, out_shape=out_shape,
        scratch_shapes=[pltpu.VMEM((ROWS, P_COLS), f32), pltpu.VMEM((ROWS, D_MODEL), bf16),
                        pltpu.VMEM((ROWS, D_MODEL), f32),
                        pltpu.VMEM((groups, A_HEADS * T, A_KV_DIM), bf16),
                        pltpu.VMEM((A_HEADS, ROWS, WINDOW), f32),
                        pltpu.VMEM((groups, A_HEADS * T, WINDOW), bf16),
                        pltpu.VMEM((A_HEADS, ROWS, 2 * LANES), f32),
                        pltpu.VMEM((A_HEADS, ROWS, ROWS), f32), pltpu.VMEM((A_HEADS, ROWS, WINDOW), f32)],
        compiler_params=pltpu.CompilerParams(dimension_semantics=("arbitrary", "arbitrary"),
                                             vmem_limit_bytes=VMEM_LIMIT_BYTES),
        name="sample_path",
    )(x2, p["norm_gain"], p["w_in"], p["gbias"], p["m_gain"], p["r_gain"], p["sinks"], p["w_out"], p["fgain"],
      tabs["tril3"], tabs["maskadd"], tabs["dmat2"], tabs["rslab"], ubias,
      cos_t, sin_t, tabs["selh"], tabs["selp"], c0, n0t, m0r, s0, k0r, v0r)
    y = y.reshape(B, T, D_MODEL)
    n = jnp.transpose(n, (0, 2, 1, 3))
    m = m.reshape(depth, B, T, LANES)[:, :, 0, :M_HEADS]
    k = jnp.transpose(k.reshape(depth, B, A_KV_HEADS, HEAD_DIM, WINDOW), (0, 1, 4, 2, 3))
    v = jnp.transpose(v.reshape(depth, B, A_KV_HEADS, HEAD_DIM, WINDOW), (0, 1, 4, 2, 3))
    return y, c, n, m, s, k, v


def _prepare_params(norm_gain, w_in, mlstm_gate_bias, mlstm_norm_gain, ret_norm_gain, attn_sinks, w_out,
                    final_norm_gain):
    depth = w_in.shape[0]
    w_t = jnp.swapaxes(w_in, 1, 2)
    split = OFF_G + N_GATES
    aq0 = split + 4 * R_DIM
    akv0 = aq0 + A_DIM
    az0 = akv0 + 2 * A_KV_DIM

    def by_head(w):
        w = w.reshape(depth, A_HEADS, HEAD_DIM, D_MODEL)
        return jnp.concatenate([w[:, h] for h in ATTN_HEAD_ORDER], axis=1)

    w_in_p = jnp.concatenate(
        [w_t[:, :split], jnp.zeros((depth, GATE_PAD - N_GATES, D_MODEL), w_t.dtype), w_t[:, split:aq0],
         by_head(w_t[:, aq0:akv0]), w_t[:, akv0:az0], by_head(w_t[:, az0:])], axis=1).astype(bf16)
    wo16 = w_out.astype(bf16)
    a0 = M_DIM + R_DIM
    w_out_p = jnp.concatenate([wo16[:, :a0, :], by_head(wo16[:, a0:, :])], axis=1)
    gbias = jnp.pad(mlstm_gate_bias.reshape(depth, 1, N_GATES), ((0, 0), (0, 0), (0, LANES - N_GATES)))
    return dict(norm_gain=norm_gain.reshape(depth, 1, D_MODEL), w_in=w_in_p, gbias=gbias,
                m_gain=mlstm_norm_gain.reshape(depth, 1, M_DIM), r_gain=ret_norm_gain.reshape(depth, 1, R_DIM),
                sinks=attn_sinks, w_out=w_out_p, fgain=final_norm_gain.reshape(1, D_MODEL))


def kernel(x_prompt, x_sample, state_mlstm_C, state_mlstm_n, state_mlstm_m, state_ret_S, cache_win_k,
           cache_win_v, norm_gain, w_in, mlstm_gate_bias, mlstm_norm_gain, ret_norm_gain, attn_sinks,
           rel_bias_table, w_out, final_norm_gain):
    depth = w_in.shape[0]
    seq = x_prompt.shape[1]
    dec_seq = x_sample.shape[1]
    past_len = seq
    p = _prepare_params(norm_gain, w_in, mlstm_gate_bias, mlstm_norm_gain, ret_norm_gain, attn_sinks, w_out,
                        final_norm_gain)
    tabs_p = _static_tables(1)
    tabs_s = _static_tables(ROWS // dec_seq)
    ubias = _bias_vectors(rel_bias_table)
    rope_p = (*_rope_tables(jnp.arange(0, seq, ROWS, dtype=jnp.int32), signed=False),
              *_rope_tables(jnp.arange(ROWS, dtype=jnp.int32), signed=False))
    cos_s, sin_s = _rope_tables(past_len + (jnp.arange(ROWS, dtype=jnp.int32) % dec_seq))
    states = (state_mlstm_C, state_mlstm_n, state_mlstm_m, state_ret_S, cache_win_k, cache_win_v)

    xp = x_prompt
    p_states = []
    for layer in range(depth):
        xp, *sp = _prompt_layer(xp, p, layer, tabs_p, ubias, rope_p, layer == depth - 1)
        p_states.append(sp)
    outs_p = [jnp.stack([p_states[l][i] for l in range(depth)]) for i in range(6)]
    xs, *outs_s = _sample_path(x_sample, states, p, tabs_s, ubias, cos_s, sin_s)
    return (xp, xs, *outs_p, *outs_s)
```

```python
import functools

import numpy as np
import jax
import jax.numpy as jnp
from jax import lax
from jax.experimental import pallas as pl
from jax.experimental.pallas import tpu as pltpu

D_MODEL = 1024
HEAD_DIM = 64
M_HEADS = 4
R_HEADS = 4
A_HEADS = 8
A_KV_HEADS = 2
KV_GROUP = A_HEADS // A_KV_HEADS
M_DIM = M_HEADS * HEAD_DIM
R_DIM = R_HEADS * HEAD_DIM
A_DIM = A_HEADS * HEAD_DIM
A_KV_DIM = A_KV_HEADS * HEAD_DIM
WINDOW = 128
N_BUCKETS = 32
REL_MAX_DIST = 128
ROPE_BASE = 10000.0
NORM_EPS = 1e-6
QK_SCALE = HEAD_DIM ** -0.5

LANES = 128
ROWS = 128
GATE_PAD = LANES
PAIRS = M_HEADS // 2
SPLIT_TERMS = 3

OFF_MQ = 0
OFF_MK = OFF_MQ + M_DIM
OFF_MV = OFF_MK + M_DIM
OFF_MO = OFF_MV + M_DIM
OFF_MZ = OFF_MO + M_DIM
OFF_G = OFF_MZ + M_DIM
OFF_RQ = OFF_G + GATE_PAD
OFF_RK = OFF_RQ + R_DIM
OFF_RV = OFF_RK + R_DIM
OFF_RZ = OFF_RV + R_DIM
OFF_AQ = OFF_RZ + R_DIM
OFF_AK = OFF_AQ + A_DIM
OFF_AV = OFF_AK + A_KV_DIM
OFF_AZ = OFF_AV + A_KV_DIM
P_COLS = OFF_AZ + A_DIM
N_GATES = 2 * M_HEADS
PROJ_COL_BLOCK = 512
SCHEDULE = "FBFBFB"
ATTN_HEAD_ORDER = tuple(h for j in range(KV_GROUP) for h in (j, KV_GROUP + j))

PROMPT_ROWS = 512
VMEM_LIMIT_BYTES = 56 * 1024 * 1024
DECODE_VMEM_LIMIT_BYTES = 58 * 1024 * 1024

f32 = jnp.float32
bf16 = jnp.bfloat16


def _dot(a, b):
    return jnp.dot(a, b, preferred_element_type=f32)


def _dot_nt(a, b):
    return lax.dot_general(a, b, (((1,), (1,)), ((), ())), preferred_element_type=f32)


def _sigmoid(x):
    return 1.0 / (1.0 + jnp.exp(-x))


def _silu(x):
    return x * _sigmoid(x)


def _log_sigmoid(x):
    return jnp.minimum(x, 0.0) - jnp.log(1.0 + jnp.exp(-jnp.abs(x)))


def _split_parts(x, terms):
    parts, r = [], x
    for i in range(terms):
        p = r.astype(bf16)
        parts.append(p)
        if i + 1 < terms:
            r = r - p.astype(f32)
    return parts


def _split_terms(x, terms=SPLIT_TERMS):
    return jnp.concatenate(_split_parts(x, terms), axis=1)


def _exact_tril_dot(tril3, x):
    return _dot(tril3, jnp.concatenate(_split_parts(x, SPLIT_TERMS), axis=0))


def _rope(x, cos_t, sin_t, first_half):
    up = pltpu.roll(x, LANES - HEAD_DIM // 2, axis=1)
    down = pltpu.roll(x, HEAD_DIM // 2, axis=1)
    return x * cos_t + jnp.where(first_half, up, down) * sin_t


def _rms_project(x_ref, ngain_ref, win_ref, proj_sc):
    xf = x_ref[...]
    u = xf * lax.rsqrt(jnp.mean(xf * xf, axis=1, keepdims=True) + NORM_EPS) * ngain_ref[...]
    u16 = u.astype(bf16)
    for c0 in range(0, P_COLS, PROJ_COL_BLOCK):
        c1 = min(c0 + PROJ_COL_BLOCK, P_COLS)
        proj_sc[:, c0:c1] = _dot_nt(u16, win_ref[c0:c1, :])


def _prompt_kernel(x_ref, ngain_ref, win_ref, gbias_ref, mgain_ref, rgain_ref, sinks_ref, wout_ref,
                   fgain_ref, tril3_ref, maskadd_ref, dmat2_ref, rslab_ref, ubias_ref, cos_ref,
                   sin_ref, cosoff_ref, sinoff_ref, selh_ref, selp_ref,
                   y_ref, c_out, n_out, m_out, s_out, k_out, v_out,
                   proj_sc, mix_sc, cn_sc, sb_sc, m_sc, kp_sc, vp_sc, bias_sc, *, chunks, final):
    step = pl.program_id(1)
    last_step = pl.num_programs(1) - 1

    @pl.when(jnp.logical_and(pl.program_id(0) == 0, step == 0))
    def _():
        for blk in range(A_HEADS):
            u = ubias_ref[ATTN_HEAD_ORDER[blk]:ATTN_HEAD_ORDER[blk] + 1, :]
            bias_sc[blk * ROWS:(blk + 1) * ROWS, :] = jnp.concatenate(
                [_skew(u[:, :2 * WINDOW], ROWS), _skew(u[:, 2 * WINDOW:], ROWS)], axis=1)

    xf = x_ref[...]
    u16 = (xf * lax.rsqrt(jnp.mean(xf * xf, axis=1, keepdims=True) + NORM_EPS) * ngain_ref[...]).astype(bf16)
    col_blocks = [(c0, min(c0 + PROJ_COL_BLOCK, P_COLS)) for c0 in range(0, P_COLS, PROJ_COL_BLOCK)]
    half_rows = (chunks // 2) * ROWS if chunks > 1 else chunks * ROWS

    def project(r0, r1, c0, c1):
        proj_sc[r0:r1, c0:c1] = _dot_nt(u16[r0:r1], win_ref[c0:c1, :])

    for c0, c1 in col_blocks:
        project(0, half_rows, c0, c1)
    late_pieces = [(half_rows, chunks * ROWS, c0, c1) for c0, c1 in col_blocks] if half_rows < chunks * ROWS else []

    @pl.when(step == 0)
    def _():
        cn_sc[...] = jnp.zeros_like(cn_sc)
        sb_sc[...] = jnp.zeros_like(sb_sc)
        m_sc[...] = jnp.zeros_like(m_sc)
        kp_sc[...] = jnp.zeros_like(kp_sc)
        vp_sc[...] = jnp.zeros_like(vp_sc)

    lane = lax.broadcasted_iota(jnp.int32, (ROWS, LANES), 1)
    row = lax.broadcasted_iota(jnp.int32, (ROWS, LANES), 0)
    left = lane < HEAD_DIM
    first_half = (lane & (HEAD_DIM - 1)) < (HEAD_DIM // 2)
    head_col = lane < M_HEADS
    blockdiag = (row < HEAD_DIM) == left
    row2 = lax.broadcasted_iota(jnp.int32, (ROWS, 2 * LANES), 0)
    lane2w = lax.broadcasted_iota(jnp.int32, (ROWS, 2 * LANES), 1)
    left2 = (lane2w & (LANES - 1)) < HEAD_DIM
    blockdiag2 = (row2 < HEAD_DIM) == left2
    ones16 = jnp.ones((ROWS, LANES), bf16)
    halves, pick, pair_blockdiag, half_mean = _halves, _pick, _pair_blockdiag, _half_mean

    def chunk_body(ci):
        rows = slice(ci * ROWS, (ci + 1) * ROWS)

        def proj(off, width=LANES):
            return proj_sc[rows, off:off + width]

        gates = proj(OFF_G) + gbias_ref[...]
        bcum = _exact_tril_dot(tril3_ref[...], _log_sigmoid(gates))

        base = pl.ds(step * chunks + ci, 1)
        cos_a, sin_a = cos_ref[base, :], sin_ref[base, :]
        cos_b, sin_b = cosoff_ref[...], sinoff_ref[...]
        cos_t = cos_a * cos_b - sin_a * sin_b
        sin_t = sin_a * cos_b + cos_a * sin_b
        sin_t = jnp.where(first_half, -sin_t, sin_t)
        r_q = [_rope(proj(OFF_RQ + p * LANES), cos_t, sin_t, first_half) for p in range(PAIRS)]
        r_k = [_rope(proj(OFF_RK + p * LANES), cos_t, sin_t, first_half) * QK_SCALE for p in range(PAIRS)]
        r_vbd = pair_blockdiag([proj(OFF_RV + p * LANES).astype(bf16) for p in range(PAIRS)])
        r_sb = [sb_sc[p] for p in range(PAIRS)]
        r_sc = _dot_nt(jnp.concatenate([halves(q) for q in r_q], axis=1),
                       pair_blockdiag([k.astype(bf16) for k in r_k]))
        r_inter = _dot(jnp.concatenate([q.astype(bf16) for q in r_q], axis=1),
                       pair_blockdiag([sb.astype(bf16) for sb in r_sb]))
        r_upd = _dot(jnp.concatenate([(r_k[p] * rslab_ref[1, p]).T.astype(bf16) for p in range(PAIRS)], axis=1),
                     r_vbd)
        for p in range(PAIRS):
            sb_sc[p] = rslab_ref[2, p] * r_sb[p] + jnp.where(blockdiag, r_upd[:, p * LANES:(p + 1) * LANES], 0.0)
        yield

        kcur, vcur = proj(OFF_AK), proj(OFF_AV)
        kprev, vprev = kp_sc[...], vp_sc[...]
        kk16 = jnp.concatenate([kcur, kprev], axis=0).astype(bf16)
        vv16 = jnp.concatenate([jnp.concatenate([vcur.astype(bf16), ones16], axis=1),
                                jnp.concatenate([vprev.astype(bf16), ones16], axis=1)], axis=0)
        kp_sc[...] = kcur
        vp_sc[...] = vcur
        a_q = jnp.concatenate([halves(proj(OFF_AQ + j * LANES) * QK_SCALE) for j in range(KV_GROUP)], axis=0)
        a_s = _dot_nt(a_q, kk16) + bias_sc[...]
        if ci == 0:
            pen = jnp.where(step == 0, -jnp.inf, 0.0).astype(f32)
            a_s = a_s + jnp.where(lax.broadcasted_iota(jnp.int32, (1, 2 * ROWS), 1) >= ROWS, pen, 0.0)

        m_q = [proj(OFF_MQ + p * LANES) for p in range(PAIRS)]
        m_k = [proj(OFF_MK + p * LANES) * QK_SCALE for p in range(PAIRS)]
        m_qk = _dot_nt(jnp.concatenate([halves(q) for q in m_q], axis=1),
                       pair_blockdiag([k.astype(bf16) for k in m_k]))
        m_q16 = [q.astype(bf16) for q in m_q]
        m_v16 = [proj(OFF_MV + p * LANES).astype(bf16) for p in range(PAIRS)]
        yield

        zb = pltpu.roll(bcum, LANES - M_HEADS, axis=1)
        r_mat = jnp.where(head_col, gates - zb, 0.0)
        cm = r_mat
        sh = 1
        while sh < ROWS:
            cm = jnp.where(row >= sh, jnp.maximum(cm, pltpu.roll(cm, sh, axis=0)), cm)
            sh *= 2
        mprev = m_sc[...]
        mx = jnp.maximum(mprev, cm)
        gm = mprev - mx
        em = jnp.where(head_col, -(zb + mx), 0.0)
        mx_last = jnp.broadcast_to(mx[ROWS - 1:ROWS, :], (ROWS, LANES))
        m_sc[...] = jnp.where(head_col, jnp.broadcast_to((zb + mx)[ROWS - 1:ROWS, :], (ROWS, LANES)), 0.0)
        mx_b = _dot_nt(_split_terms(mx), selh_ref[...])
        slabs = jnp.exp(_dot_nt(_split_terms(jnp.concatenate([gm, em, r_mat - mx_last], axis=0)),
                                selp_ref[...]))
        winter_b, emt_b, ws_b = slabs[:ROWS], slabs[ROWS:2 * ROWS], slabs[2 * ROWS:]
        r_t = r_mat.T
        yield

        outs = []
        r_acc = []
        r_o = _dot((r_sc * dmat2_ref[...]).astype(bf16), r_vbd)
        for p in range(PAIRS):
            ps = slice(p * LANES, (p + 1) * LANES)
            r_acc.append(pick(r_o[:, ps]) + rslab_ref[0, p] * r_inter[:, ps])

        a_out = []
        a_p = []
        for blk in range(A_HEADS):
            s = a_s[blk * ROWS:(blk + 1) * ROWS]
            sink = sinks_ref[ATTN_HEAD_ORDER[blk]]
            m = jnp.maximum(jnp.max(jnp.maximum(s[:, :ROWS], s[:, ROWS:]), axis=1, keepdims=True), sink)
            a_p.append(jnp.exp(s - m).astype(bf16))
            a_out.append(jnp.exp(sink - m))
        a_pv = _dot(jnp.concatenate(a_p, axis=0), vv16)
        yield

        maskadd = maskadd_ref[...]
        for p in range(PAIRS):
            ps = slice(p * LANES, (p + 1) * LANES)
            cn = cn_sc[p]
            w = jnp.concatenate(
                [jnp.exp((r_t[2 * p + side:2 * p + side + 1, :] + maskadd)
                         - mx_b[:, (2 * p + side) * ROWS:(2 * p + side + 1) * ROWS]) for side in range(2)],
                axis=0) * m_qk[:, ps]
            acc = (pick(_dot(w.astype(bf16), jnp.concatenate([m_v16[p], ones16], axis=1)))
                   + jnp.concatenate([winter_b[:, ps]] * 2, axis=1) * _dot(m_q16[p], cn.astype(bf16)))
            hh = acc[:, :LANES] / jnp.maximum(jnp.abs(acc[:, LANES:]), emt_b[:, ps])
            outs.append(_sigmoid(proj(OFF_MO + p * LANES)) * hh)
            kwt16 = (m_k[p] * ws_b[:, ps]).T.astype(bf16)
            dcn = _dot(kwt16, jnp.concatenate([m_v16[p], ones16], axis=1))
            decay = winter_b[ROWS - 1:ROWS, ps]
            cn_sc[p] = jnp.concatenate([decay, decay], axis=1) * cn + jnp.where(blockdiag2, dcn, 0.0)
        outs.extend(r_acc)
        yield

        x4 = jnp.concatenate(outs, axis=0)
        xc = x4 - half_mean(x4)
        y4 = xc * lax.rsqrt(half_mean(xc * xc) + NORM_EPS)
        for i in range(2 * PAIRS):
            gain = (mgain_ref if i < PAIRS else rgain_ref)[:, (i % PAIRS) * LANES:(i % PAIRS + 1) * LANES]
            zoff = (OFF_MZ if i < PAIRS else OFF_RZ) + (i % PAIRS) * LANES
            out = y4[i * ROWS:(i + 1) * ROWS] * gain * _silu(proj(zoff))
            mix_sc[rows, i * LANES:(i + 1) * LANES] = out.astype(bf16)
        for j in range(KV_GROUP):
            acc = pick(a_pv[2 * j * ROWS:(2 * j + 2) * ROWS])
            den = acc[:, LANES:] + jnp.where(left, a_out[2 * j], a_out[2 * j + 1])
            out = (acc[:, :LANES] / den) * _silu(proj(OFF_AZ + j * LANES))
            mix_sc[rows, M_DIM + R_DIM + j * LANES:M_DIM + R_DIM + (j + 1) * LANES] = out.astype(bf16)

    def out_project(r0, r1, c0, c1):
        y_ref[r0:r1, c0:c1] = x_ref[r0:r1, c0:c1] + _dot(mix_sc[r0:r1, :], wout_ref[:, c0:c1])

    out_blocks = [(c0, min(c0 + PROJ_COL_BLOCK, D_MODEL)) for c0 in range(0, D_MODEL, PROJ_COL_BLOCK)]
    early_out = [(0, half_rows, c0, c1) for c0, c1 in out_blocks] if half_rows < chunks * ROWS else []
    final_out = [(half_rows if early_out else 0, chunks * ROWS, c0, c1) for c0, c1 in out_blocks]

    def fill_mxu(ci):
        if late_pieces:
            project(*late_pieces.pop(0))
        elif early_out and (ci - 1) * ROWS >= half_rows:
            out_project(*early_out.pop(0))

    parts = [chunk_body(ci) for ci in range(chunks)]
    for ci in range(chunks + 1):
        if ci * ROWS >= half_rows:
            while late_pieces:
                project(*late_pieces.pop(0))
        for which in SCHEDULE:
            if which == "F" and ci < chunks:
                next(parts[ci])
                fill_mxu(ci)
            if which == "B" and ci > 0:
                next(parts[ci - 1], None)
                fill_mxu(ci)
    for piece in early_out + final_out:
        out_project(*piece)
    if final:
        y = y_ref[...]
        y_ref[...] = y * lax.rsqrt(jnp.mean(y * y, axis=1, keepdims=True) + NORM_EPS) * fgain_ref[...]

    @pl.when(step == last_step)
    def _():
        for p in range(PAIRS):
            cn = cn_sc[p]
            sb = sb_sc[p]
            n_t = cn[:, LANES:].T
            for side in range(2):
                h = 2 * p + side
                blk = slice(side * HEAD_DIM, (side + 1) * HEAD_DIM)
                c_out[0, h] = cn[blk, blk]
                s_out[0, h] = sb[blk, blk]
                n_out[0, h:h + 1, :] = n_t[side * HEAD_DIM:side * HEAD_DIM + 1, blk]
        m_out[0] = m_sc[...]
        k_out[0] = kp_sc[...]
        v_out[0] = vp_sc[...]


def _lane_is_left(shape):
    return (lax.broadcasted_iota(jnp.int32, shape, 1) & (LANES - 1)) < HEAD_DIM


def _halves(x):
    left = _lane_is_left(x.shape)
    return jnp.concatenate([jnp.where(left, x, 0.0), jnp.where(left, 0.0, x)], axis=0).astype(bf16)


def _pick(x):
    return jnp.where(_lane_is_left((ROWS, x.shape[1])), x[:ROWS], x[ROWS:])


def _pair_blockdiag(blocks):
    z = jnp.zeros_like(blocks[0])
    return jnp.concatenate(
        [jnp.concatenate([blk if j == i else z for j in range(len(blocks))], axis=1)
         for i, blk in enumerate(blocks)], axis=0)


def _half_mean(x):
    left = _lane_is_left(x.shape)
    s_left = jnp.sum(jnp.where(left, x, 0.0), axis=1, keepdims=True)
    s_right = jnp.sum(jnp.where(left, 0.0, x), axis=1, keepdims=True)
    return jnp.where(left, s_left, s_right) * (1.0 / HEAD_DIM)


def _group_last(x, groups):
    n = x.shape[1]
    glen = ROWS // groups
    x3 = x.reshape(groups, glen, n)
    return jnp.broadcast_to(x3[:, glen - 1:glen, :], (groups, glen, n)).reshape(ROWS, n)


def _sample_kernel(x_ref, ngain_ref, win_ref, gbias_ref, mgain_ref, rgain_ref, sinks_ref, wout_ref,
                   fgain_ref, tril3_ref, maskadd_ref, dmat2_ref, rslab_ref, ubias_ref, cos_ref, sin_ref,
                   selh_ref, selp_ref,
                   c_in, n_in, m_in, s_in, k_in, v_in,
                   y_ref, c_out, n_out, m_out, s_out, k_out, v_out,
                   proj_sc, mix_sc, xcur_sc, xall_sc, qb_sc, sp_sc, pp_sc, ob_sc, biasc_ref, biasp_ref, *, groups,
                   ret_full):
    glen = ROWS // groups
    glen_log2 = glen.bit_length() - 1
    hd_log2 = HEAD_DIM.bit_length() - 1
    layer = pl.program_id(0)
    step = pl.program_id(1)
    last_layer = pl.num_programs(0) - 1
    xrows = pl.ds(pl.multiple_of(step * ROWS, ROWS), ROWS)

    @pl.when(layer == 0)
    def _():
        xcur_sc[...] = x_ref[...]

    @pl.when(layer != 0)
    def _():
        xcur_sc[...] = xall_sc[xrows, :]

    for b in range(groups):
        k_out[b] = pltpu.roll(k_in[b], WINDOW - glen, axis=1)
        v_out[b] = pltpu.roll(v_in[b], WINDOW - glen, axis=1)

    _rms_project(xcur_sc, ngain_ref, win_ref, proj_sc)

    @pl.when(jnp.logical_and(step == 0, layer == 0))
    def _():
        for h in range(A_HEADS):
            u = ubias_ref[h:h + 1, :]
            biasc_ref[h] = _skew(u[:, :2 * WINDOW], ROWS) + maskadd_ref[...]
            biasp_ref[h] = jnp.concatenate([_skew(u[:, 2 * WINDOW:], glen)] * groups, axis=0)

    lane = lax.broadcasted_iota(jnp.int32, (ROWS, LANES), 1)
    first_half = (lane & (HEAD_DIM - 1)) < (HEAD_DIM // 2)
    rows = slice(0, ROWS)

    r_i = lax.broadcasted_iota(jnp.int32, (ROWS, groups * HEAD_DIM), 0)
    c_i = lax.broadcasted_iota(jnp.int32, (ROWS, groups * HEAD_DIM), 1)
    blk = (r_i >> glen_log2) == (c_i >> hd_log2)
    r_t = lax.broadcasted_iota(jnp.int32, (groups * HEAD_DIM, ROWS), 0)
    c_t = lax.broadcasted_iota(jnp.int32, (groups * HEAD_DIM, ROWS), 1)
    blk_t = (r_t >> hd_log2) == (c_t >> glen_log2)

    def q_times_state(qh, st):
        qt = jnp.where(blk, jnp.concatenate([qh] * groups, axis=1), 0.0)
        return _dot(qt.astype(bf16), st.astype(bf16))

    def state_increment(kt_h, vh16):
        kt = jnp.where(blk_t, jnp.concatenate([kt_h] * groups, axis=0), 0.0)
        return _dot(kt.astype(bf16), vh16)

    def proj(off, width=LANES):
        return proj_sc[:, off:off + width]

    def head_state_rows(slab, side):
        wide = slab[:, side * HEAD_DIM:(side + 1) * HEAD_DIM].reshape(groups, glen, HEAD_DIM)[:, 0:1, :]
        rows_ = jnp.broadcast_to(wide, (groups, HEAD_DIM, HEAD_DIM)).reshape(groups * HEAD_DIM, HEAD_DIM)
        return rows_, wide.reshape(groups, HEAD_DIM)

    row = lax.broadcasted_iota(jnp.int32, (ROWS, LANES), 0)
    tau = row & (glen - 1)
    head_col = lane < M_HEADS
    left = lane < HEAD_DIM
    ones16 = jnp.ones((ROWS, LANES), bf16)
    gates = proj(OFF_G) + gbias_ref[...]
    bcum = _exact_tril_dot(tril3_ref[...], _log_sigmoid(gates))
    zb = pltpu.roll(bcum, LANES - M_HEADS, axis=1)
    r_mat = jnp.where(head_col, gates - zb, 0.0)
    cm = r_mat
    sh = 1
    while sh < glen:
        cm = jnp.where(tau >= sh, jnp.maximum(cm, pltpu.roll(cm, sh, axis=0)), cm)
        sh *= 2
    mprev = m_in[...]
    mx = jnp.maximum(mprev, cm)
    gm = mprev - mx
    em = jnp.where(head_col, -(zb + mx), 0.0)
    mx_last = _group_last(mx, groups)
    m_out[...] = jnp.where(head_col, _group_last(zb + mx, groups), 0.0)
    mx_b = _dot_nt(_split_terms(mx), selh_ref[...])
    slabs = jnp.exp(_dot_nt(_split_terms(jnp.concatenate([gm, em, r_mat - mx_last], axis=0)), selp_ref[...]))
    winter_b, emt_b, ws_b = slabs[:ROWS], slabs[ROWS:2 * ROWS], slabs[2 * ROWS:]
    decay_b = _group_last(winter_b, groups)
    r_t = r_mat.T
    maskadd = maskadd_ref[...]

    m_q = [proj(OFF_MQ + p * LANES) for p in range(PAIRS)]
    m_k = [proj(OFF_MK + p * LANES) * QK_SCALE for p in range(PAIRS)]
    m_v = [proj(OFF_MV + p * LANES) for p in range(PAIRS)]
    m_qk = _dot_nt(jnp.concatenate([_halves(q) for q in m_q], axis=1),
                   _pair_blockdiag([k.astype(bf16) for k in m_k]))
    outs = []
    for p in range(PAIRS):
        ps = slice(p * LANES, (p + 1) * LANES)
        w = jnp.concatenate(
            [jnp.exp((r_t[2 * p + side:2 * p + side + 1, :] + maskadd)
                     - mx_b[:, (2 * p + side) * ROWS:(2 * p + side + 1) * ROWS]) for side in range(2)],
            axis=0) * m_qk[:, ps]
        intra = _pick(_dot(w.astype(bf16), jnp.concatenate([m_v[p].astype(bf16), ones16], axis=1)))
        kw = m_k[p] * ws_b[:, ps]
        kwt = kw.T
        q_c, q_n = [], []
        for side in range(2):
            h = 2 * p + side
            hs = slice(side * HEAD_DIM, (side + 1) * HEAD_DIM)
            qh = m_q[p][:, hs]
            c_h = c_in[:, h].reshape(groups * HEAD_DIM, HEAD_DIM)
            n_g = n_in[h]
            n_rows = jnp.broadcast_to(n_g.reshape(groups, 1, HEAD_DIM),
                                      (groups, glen, HEAD_DIM)).reshape(ROWS, HEAD_DIM)
            q_c.append(q_times_state(qh, c_h))
            q_n.append(jnp.sum(qh * n_rows, axis=1, keepdims=True))
            dec_rows, dec_g = head_state_rows(decay_b[:, ps], side)
            c_new = dec_rows * c_h + state_increment(kwt[hs, :], m_v[p][:, hs].astype(bf16))
            c_out[:, h] = c_new.reshape(groups, HEAD_DIM, HEAD_DIM)
            n_out[h] = dec_g * n_g + jnp.sum(kw[:, hs].reshape(groups, glen, HEAD_DIM), axis=1)
        wb = winter_b[:, ps]
        num = intra[:, :LANES] + wb * jnp.concatenate(q_c, axis=1)
        nq = intra[:, LANES:] + wb * jnp.where(left, q_n[0], q_n[1])
        outs.append(_sigmoid(proj(OFF_MO + p * LANES)) * (num / jnp.maximum(jnp.abs(nq), emt_b[:, ps])))

    cos_t, sin_t = cos_ref[...], sin_ref[...]
    r_q = [_rope(proj(OFF_RQ + p * LANES), cos_t, sin_t, first_half) for p in range(PAIRS)]
    r_k = [_rope(proj(OFF_RK + p * LANES), cos_t, sin_t, first_half) * QK_SCALE for p in range(PAIRS)]
    r_v = [proj(OFF_RV + p * LANES) for p in range(PAIRS)]
    r_sc = _dot_nt(jnp.concatenate([_halves(q) for q in r_q], axis=1),
                   _pair_blockdiag([k.astype(bf16) for k in r_k]))
    r_o = _dot((r_sc * dmat2_ref[...]).astype(bf16), _pair_blockdiag([v.astype(bf16) for v in r_v]))
    for p in range(PAIRS):
        ps = slice(p * LANES, (p + 1) * LANES)
        rkt = (r_k[p] * rslab_ref[1, p]).T
        q_s = []
        for side in range(2):
            h = 2 * p + side
            hs = slice(side * HEAD_DIM, (side + 1) * HEAD_DIM)
            s_h = s_in[:, h].reshape(groups * HEAD_DIM, HEAD_DIM)
            q_s.append(q_times_state(r_q[p][:, hs], s_h))
            s_new = ret_full[h] * s_h + state_increment(rkt[hs, :], r_v[p][:, hs].astype(bf16))
            s_out[:, h] = s_new.reshape(groups, HEAD_DIM, HEAD_DIM)
        outs.append(_pick(r_o[:, ps]) + rslab_ref[0, p] * jnp.concatenate(q_s, axis=1))

    x4 = jnp.concatenate(outs, axis=0)
    xc = x4 - _half_mean(x4)
    y4 = xc * lax.rsqrt(_half_mean(xc * xc) + NORM_EPS)
    for i in range(2 * PAIRS):
        gain = (mgain_ref if i < PAIRS else rgain_ref)[:, (i % PAIRS) * LANES:(i % PAIRS + 1) * LANES]
        zoff = (OFF_MZ if i < PAIRS else OFF_RZ) + (i % PAIRS) * LANES
        mix_sc[:, i * LANES:(i + 1) * LANES] = (y4[i * ROWS:(i + 1) * ROWS] * gain * _silu(proj(zoff))).astype(bf16)

    kcur = proj_sc[rows, OFF_AK:OFF_AK + A_KV_DIM]
    vcur = proj_sc[rows, OFF_AV:OFF_AV + A_KV_DIM]
    kcur16, vcur16 = kcur.astype(bf16), vcur.astype(bf16)

    a_q = {}
    for j in range(KV_GROUP):
        hq = _halves(proj(OFF_AQ + j * LANES) * QK_SCALE)
        a_q[j], a_q[KV_GROUP + j] = hq[:ROWS], hq[ROWS:]
    for h in range(A_HEADS):
        qb_sc[:, h * glen:(h + 1) * glen, :] = a_q[h].reshape(groups, glen, LANES)
    sc_all = _dot_nt(jnp.concatenate([a_q[h] for h in range(A_HEADS)], axis=0), kcur16)

    for b in range(groups):
        sp = _dot(qb_sc[b], k_in[b].astype(bf16))
        sp_sc[:, b * glen:(b + 1) * glen, :] = sp.reshape(A_HEADS, glen, WINDOW)

    vcur1 = jnp.concatenate([vcur16, ones16], axis=1)
    esink, o_cur = [], []
    for h in range(A_HEADS):
        sc = sc_all[h * ROWS:(h + 1) * ROWS] + biasc_ref[h]
        sp = sp_sc[h] + biasp_ref[h]
        sink = sinks_ref[layer, h]
        m = jnp.maximum(jnp.max(jnp.maximum(sc, sp), axis=1, keepdims=True), sink)
        esink.append(jnp.exp(sink - m))
        o_cur.append(_dot(jnp.exp(sc - m).astype(bf16), vcur1))
        pp_sc[:, h * glen:(h + 1) * glen, :] = jnp.exp(sp - m).reshape(groups, glen, WINDOW).astype(bf16)

    for b in range(groups):
        v1 = jnp.concatenate([v_in[b].astype(bf16), ones16], axis=0)
        ob = _dot_nt(pp_sc[b], v1)
        ob_sc[:, b * glen:(b + 1) * glen, :] = ob.reshape(A_HEADS, glen, 2 * LANES)
    norm = []
    for h in range(A_HEADS):
        acc = o_cur[h] + ob_sc[h]
        norm.append(acc[:, :LANES] / (acc[:, LANES:] + esink[h]))
    outs = [jnp.where(left, norm[j], norm[KV_GROUP + j]) for j in range(KV_GROUP)]
    kcur_t, vcur_t = kcur.T, vcur.T
    new = slice(WINDOW - glen, WINDOW)
    for b in range(groups):
        shift = (WINDOW - glen - b * glen) % LANES
        k_out[b, :, new] = pltpu.roll(kcur_t, shift, axis=1)[:, new]
        v_out[b, :, new] = pltpu.roll(vcur_t, shift, axis=1)[:, new]
    out_a = jnp.concatenate(outs, axis=1) * _silu(proj_sc[rows, OFF_AZ:OFF_AZ + A_DIM])
    mix_sc[rows, M_DIM + R_DIM:M_DIM + R_DIM + A_DIM] = out_a.astype(bf16)

    y = xcur_sc[...] + _dot(mix_sc[...], wout_ref[...])

    @pl.when(layer != last_layer)
    def _():
        xall_sc[xrows, :] = y
        y_ref[...] = y

    @pl.when(layer == last_layer)
    def _():
        y_ref[...] = y * lax.rsqrt(jnp.mean(y * y, axis=1, keepdims=True) + NORM_EPS) * fgain_ref[...]


def _t5_bucket(dist):
    max_exact = N_BUCKETS // 2
    d = np.maximum(dist, 1).astype(np.float32)
    large = max_exact + (np.log(d / max_exact) / np.log(REL_MAX_DIST / max_exact)
                         * (N_BUCKETS - max_exact)).astype(np.int32)
    large = np.minimum(large, N_BUCKETS - 1)
    return np.where(dist < max_exact, dist, large).astype(np.int32)


def _static_tables(groups):
    glen = ROWS // groups
    r = np.arange(ROWS)
    grp, tau = r // glen, r % glen
    causal = (grp[:, None] == grp[None, :]) & (tau[None, :] <= tau[:, None])
    tril = causal.astype(np.float32)
    maskadd = np.where(causal, 0.0, -np.inf).astype(np.float32)
    log_g = np.log1p(-np.exp2(-5.0 - np.arange(R_HEADS, dtype=np.float64)))
    diff = (tau[:, None] - tau[None, :]).astype(np.float64)
    dmat = np.where(causal[None], np.exp(log_g[:, None, None] * np.maximum(diff, 0.0)[None]), 0.0)
    inter = np.exp(log_g[None, :] * (tau[:, None] + 1.0))
    tail = np.exp(log_g[None, :] * (glen - 1.0 - tau[:, None]))
    full = np.exp(log_g * glen)
    lane_head = np.arange(LANES) // HEAD_DIM
    rslab = np.zeros((3, PAIRS, ROWS, LANES), np.float64)
    for p in range(PAIRS):
        rslab[0, p] = inter[:, 2 * p + lane_head]
        rslab[1, p] = tail[:, 2 * p + lane_head]
        rslab[2, p] = full[2 * p + lane_head][None, :]
    selh = np.zeros((M_HEADS * ROWS, SPLIT_TERMS * LANES), np.float32)
    selp = np.zeros((PAIRS * LANES, SPLIT_TERMS * LANES), np.float32)
    for t in range(SPLIT_TERMS):
        for h in range(M_HEADS):
            selh[h * ROWS:(h + 1) * ROWS, t * LANES + h] = 1.0
        for p in range(PAIRS):
            for side in range(2):
                selp[p * LANES + side * HEAD_DIM:p * LANES + (side + 1) * HEAD_DIM, t * LANES + 2 * p + side] = 1.0
    return dict(tril3=jnp.asarray(np.concatenate([tril] * SPLIT_TERMS, axis=1), bf16),
                maskadd=maskadd,
                dmat2=np.concatenate(list(dmat.astype(np.float32).reshape(PAIRS, 2 * ROWS, ROWS)), axis=1),
                rslab=rslab.astype(np.float32),
                full=tuple(float(v) for v in full),
                selh=jnp.asarray(selh, bf16), selp=jnp.asarray(selp, bf16))


def _bias_vectors(rel_table):
    tb = jnp.transpose(rel_table[_t5_bucket(np.arange(WINDOW))]).astype(f32)
    ninf = jnp.full((A_HEADS, WINDOW), -jnp.inf, f32)
    rev = tb[:, :0:-1]
    return jnp.concatenate([tb[:, :1], ninf, rev, ninf[:, :1], rev, ninf], axis=1)


def _skew(u_row, rows):
    x = jnp.broadcast_to(u_row, (rows, 2 * WINDOW))
    return pltpu.roll(x, 0, 1, stride=1, stride_axis=0)[:, :WINDOW]


def _rope_tables(pos, signed=True):
    half = HEAD_DIM // 2
    inv = ROPE_BASE ** (-jnp.arange(half, dtype=f32) / half)
    ang = pos.astype(f32)[:, None] * inv[None, :]
    cos, sin = jnp.cos(ang), jnp.sin(ang)
    reps = LANES // HEAD_DIM
    cos_t = jnp.tile(jnp.concatenate([cos, cos], axis=1), (1, reps))
    sin_t = jnp.tile(jnp.concatenate([-sin if signed else sin, sin], axis=1), (1, reps))
    return cos_t, sin_t


def _const_spec(shape, nargs):
    zeros = (0,) * len(shape)
    if nargs == 1:
        return pl.BlockSpec(shape, lambda i: zeros)
    return pl.BlockSpec(shape, lambda i, j: zeros)


def _layer_spec(shape, layer, nargs):
    idx = (layer,) + (0,) * len(shape)
    if nargs == 1:
        return pl.BlockSpec((None,) + shape, lambda i: idx)
    return pl.BlockSpec((None,) + shape, lambda i, j: idx)


def _param_specs(layer, nargs):
    ls = functools.partial(_layer_spec, layer=layer, nargs=nargs)
    return [ls((1, D_MODEL)), ls((P_COLS, D_MODEL)), ls((1, LANES)), ls((1, M_DIM)), ls((1, R_DIM)),
            pl.BlockSpec(memory_space=pltpu.SMEM), ls((D_MODEL, D_MODEL)), _const_spec((1, D_MODEL), nargs)]


def _param_args(p, layer):
    return (p["norm_gain"], p["w_in"], p["gbias"], p["m_gain"], p["r_gain"], p["sinks"][layer], p["w_out"],
            p["fgain"])


def _prompt_layer(x, p, layer, tabs, ubias, rope, final):
    B, T, _ = x.shape
    tb = min(PROMPT_ROWS, T)
    chunks = tb // ROWS
    nt = T // tb
    cs = functools.partial(_const_spec, nargs=2)
    in_specs = [pl.BlockSpec((None, tb, D_MODEL), lambda b, t: (b, t, 0))] + _param_specs(layer, 2) + [
        cs((ROWS, SPLIT_TERMS * ROWS)), cs((ROWS, ROWS)), cs((2 * ROWS, PAIRS * ROWS)),
        cs((3, PAIRS, ROWS, LANES)), cs((A_HEADS, 4 * WINDOW)),
        cs((T // ROWS, LANES)), cs((T // ROWS, LANES)), cs((ROWS, LANES)), cs((ROWS, LANES)),
        cs((M_HEADS * ROWS, SPLIT_TERMS * LANES)), cs((PAIRS * LANES, SPLIT_TERMS * LANES)),
    ]
    out_shape = (
        jax.ShapeDtypeStruct((B, T, D_MODEL), f32),
        jax.ShapeDtypeStruct((B, M_HEADS, HEAD_DIM, HEAD_DIM), f32),
        jax.ShapeDtypeStruct((B, M_HEADS, HEAD_DIM), f32),
        jax.ShapeDtypeStruct((B, ROWS, LANES), f32),
        jax.ShapeDtypeStruct((B, R_HEADS, HEAD_DIM, HEAD_DIM), f32),
        jax.ShapeDtypeStruct((B, WINDOW, A_KV_DIM), f32),
        jax.ShapeDtypeStruct((B, WINDOW, A_KV_DIM), f32),
    )
    out_specs = (
        pl.BlockSpec((None, tb, D_MODEL), lambda b, t: (b, t, 0)),
        pl.BlockSpec((1, M_HEADS, HEAD_DIM, HEAD_DIM), lambda b, t: (b, 0, 0, 0)),
        pl.BlockSpec((1, M_HEADS, HEAD_DIM), lambda b, t: (b, 0, 0)),
        pl.BlockSpec((1, ROWS, LANES), lambda b, t: (b, 0, 0)),
        pl.BlockSpec((1, R_HEADS, HEAD_DIM, HEAD_DIM), lambda b, t: (b, 0, 0, 0)),
        pl.BlockSpec((1, WINDOW, A_KV_DIM), lambda b, t: (b, 0, 0)),
        pl.BlockSpec((1, WINDOW, A_KV_DIM), lambda b, t: (b, 0, 0)),
    )
    kern = functools.partial(_prompt_kernel, chunks=chunks, final=final)
    y, c, n, m, s, k, v = pl.pallas_call(
        kern, grid=(B, nt), in_specs=in_specs, out_specs=out_specs, out_shape=out_shape,
        scratch_shapes=[pltpu.VMEM((tb, P_COLS), f32), pltpu.VMEM((tb, D_MODEL), bf16),
                        pltpu.VMEM((PAIRS, ROWS, 2 * LANES), f32), pltpu.VMEM((PAIRS, ROWS, LANES), f32),
                        pltpu.VMEM((ROWS, LANES), f32), pltpu.VMEM((ROWS, A_KV_DIM), f32),
                        pltpu.VMEM((ROWS, A_KV_DIM), f32), pltpu.VMEM((A_HEADS * ROWS, 2 * WINDOW), f32)],
        compiler_params=pltpu.CompilerParams(dimension_semantics=("arbitrary", "arbitrary"),
                                             vmem_limit_bytes=VMEM_LIMIT_BYTES),
        name="prompt_layer",
    )(x, *_param_args(p, layer), tabs["tril3"], tabs["maskadd"], tabs["dmat2"], tabs["rslab"], ubias,
      *rope, tabs["selh"], tabs["selp"])
    k = k.reshape(B, WINDOW, A_KV_HEADS, HEAD_DIM)
    v = v.reshape(B, WINDOW, A_KV_HEADS, HEAD_DIM)
    return y, c, n, m[:, 0, :M_HEADS], s, k, v


def _sample_path(x, states, p, tabs, ubias, cos_t, sin_t):
    B, T, _ = x.shape
    groups = ROWS // T
    nb = B // groups
    c0, n0, m0, s0, k0, v0 = states
    depth = c0.shape[0]
    x2 = x.reshape(B * T, D_MODEL)
    n0t = jnp.transpose(n0, (0, 2, 1, 3))
    m0r = jnp.pad(jnp.repeat(m0, T, axis=1), ((0, 0), (0, 0), (0, LANES - M_HEADS)))
    k0r = jnp.transpose(k0, (0, 1, 3, 4, 2)).reshape(depth, B, A_KV_DIM, WINDOW)
    v0r = jnp.transpose(v0, (0, 1, 3, 4, 2)).reshape(depth, B, A_KV_DIM, WINDOW)

    def cs(shape):
        zeros = (0,) * len(shape)
        return pl.BlockSpec(shape, lambda l, i: zeros)

    def per_layer(shape):
        zeros = (0,) * len(shape)
        return pl.BlockSpec((None,) + shape, lambda l, i: (l,) + zeros)

    st4 = pl.BlockSpec((None, groups, M_HEADS, HEAD_DIM, HEAD_DIM), lambda l, i: (l, i, 0, 0, 0))
    stn = pl.BlockSpec((None, M_HEADS, groups, HEAD_DIM), lambda l, i: (l, 0, i, 0))
    stm = pl.BlockSpec((None, ROWS, LANES), lambda l, i: (l, i, 0))
    stk = pl.BlockSpec((None, groups, A_KV_DIM, WINDOW), lambda l, i: (l, i, 0, 0))
    rows_spec = pl.BlockSpec((ROWS, D_MODEL), lambda l, i: (i, 0))
    in_specs = [
        rows_spec,
        per_layer((1, D_MODEL)), per_layer((P_COLS, D_MODEL)), per_layer((1, LANES)), per_layer((1, M_DIM)),
        per_layer((1, R_DIM)), pl.BlockSpec(memory_space=pltpu.SMEM), per_layer((D_MODEL, D_MODEL)),
        cs((1, D_MODEL)),
        cs((ROWS, SPLIT_TERMS * ROWS)), cs((ROWS, ROWS)), cs((2 * ROWS, PAIRS * ROWS)),
        cs((3, PAIRS, ROWS, LANES)), cs((A_HEADS, 4 * WINDOW)),
        cs((ROWS, LANES)), cs((ROWS, LANES)),
        cs((M_HEADS * ROWS, SPLIT_TERMS * LANES)), cs((PAIRS * LANES, SPLIT_TERMS * LANES)),
        st4, stn, stm, st4, stk, stk,
    ]
    out_shape = (
        jax.ShapeDtypeStruct((B * T + ROWS, D_MODEL), f32),
        jax.ShapeDtypeStruct((depth, B, M_HEADS, HEAD_DIM, HEAD_DIM), f32),
        jax.ShapeDtypeStruct((depth, M_HEADS, B, HEAD_DIM), f32),
        jax.ShapeDtypeStruct((depth, B * T, LANES), f32),
        jax.ShapeDtypeStruct((depth, B, R_HEADS, HEAD_DIM, HEAD_DIM), f32),
        jax.ShapeDtypeStruct((depth, B, A_KV_DIM, WINDOW), f32),
        jax.ShapeDtypeStruct((depth, B, A_KV_DIM, WINDOW), f32),
    )
    y_spec = pl.BlockSpec((ROWS, D_MODEL), lambda l, i: (jnp.where(l == depth - 1, i, nb), 0))
    out_specs = (y_spec, st4, stn, stm, st4, stk, stk)
    kern = functools.partial(_sample_kernel, groups=groups, ret_full=tabs["full"])
    y, c, n, m, s, k, v = pl.pallas_call(
        kern, grid=(depth, nb), in_specs=in_specs, out_specs=out_specs, out_shape=out_shape,
        scratch_shapes=[pltpu.VMEM((ROWS, P_COLS), f32), pltpu.VMEM((ROWS, D_MODEL), bf16),
                        pltpu.VMEM((ROWS, D_MODEL), f32), pltpu.VMEM((B * T, D_MODEL), f32),
                        pltpu.VMEM((groups, A_HEADS * T, A_KV_DIM), bf16),
                        pltpu.VMEM((A_HEADS, ROWS, WINDOW), f32),
                        pltpu.VMEM((groups, A_HEADS * T, WINDOW), bf16),
                        pltpu.VMEM((A_HEADS, ROWS, 2 * LANES), f32),
                        pltpu.VMEM((A_HEADS, ROWS, ROWS), f32), pltpu.VMEM((A_HEADS, ROWS, WINDOW), f32)],
        compiler_params=pltpu.CompilerParams(dimension_semantics=("arbitrary", "arbitrary"),
                                             vmem_limit_bytes=DECODE_VMEM_LIMIT_BYTES),
        name="sample_path",
    )(x2, p["norm_gain"], p["w_in"], p["gbias"], p["m_gain"], p["r_gain"], p["sinks"], p["w_out"], p["fgain"],
      tabs["tril3"], tabs["maskadd"], tabs["dmat2"], tabs["rslab"], ubias,
      cos_t, sin_t, tabs["selh"], tabs["selp"], c0, n0t, m0r, s0, k0r, v0r)
    y = y[:B * T].reshape(B, T, D_MODEL)
    n = jnp.transpose(n, (0, 2, 1, 3))
    m = m.reshape(depth, B, T, LANES)[:, :, 0, :M_HEADS]
    k = jnp.transpose(k.reshape(depth, B, A_KV_HEADS, HEAD_DIM, WINDOW), (0, 1, 4, 2, 3))
    v = jnp.transpose(v.reshape(depth, B, A_KV_HEADS, HEAD_DIM, WINDOW), (0, 1, 4, 2, 3))
    return y, c, n, m, s, k, v


def _prepare_params(norm_gain, w_in, mlstm_gate_bias, mlstm_norm_gain, ret_norm_gain, attn_sinks, w_out,
                    final_norm_gain):
    depth = w_in.shape[0]
    w_t = jnp.swapaxes(w_in, 1, 2)
    split = OFF_G + N_GATES
    aq0 = split + 4 * R_DIM
    akv0 = aq0 + A_DIM
    az0 = akv0 + 2 * A_KV_DIM

    def by_head(w):
        w = w.reshape(depth, A_HEADS, HEAD_DIM, D_MODEL)
        return jnp.concatenate([w[:, h] for h in ATTN_HEAD_ORDER], axis=1)

    w_in_p = jnp.concatenate(
        [w_t[:, :split], jnp.zeros((depth, GATE_PAD - N_GATES, D_MODEL), w_t.dtype), w_t[:, split:aq0],
         by_head(w_t[:, aq0:akv0]), w_t[:, akv0:az0], by_head(w_t[:, az0:])], axis=1).astype(bf16)
    wo16 = w_out.astype(bf16)
    a0 = M_DIM + R_DIM
    w_out_p = jnp.concatenate([wo16[:, :a0, :], by_head(wo16[:, a0:, :])], axis=1)
    gbias = jnp.pad(mlstm_gate_bias.reshape(depth, 1, N_GATES), ((0, 0), (0, 0), (0, LANES - N_GATES)))
    return dict(norm_gain=norm_gain.reshape(depth, 1, D_MODEL), w_in=w_in_p, gbias=gbias,
                m_gain=mlstm_norm_gain.reshape(depth, 1, M_DIM), r_gain=ret_norm_gain.reshape(depth, 1, R_DIM),
                sinks=attn_sinks, w_out=w_out_p, fgain=final_norm_gain.reshape(1, D_MODEL))


def kernel(x_prompt, x_sample, state_mlstm_C, state_mlstm_n, state_mlstm_m, state_ret_S, cache_win_k,
           cache_win_v, norm_gain, w_in, mlstm_gate_bias, mlstm_norm_gain, ret_norm_gain, attn_sinks,
           rel_bias_table, w_out, final_norm_gain):
    depth = w_in.shape[0]
    seq = x_prompt.shape[1]
    dec_seq = x_sample.shape[1]
    past_len = seq
    p = _prepare_params(norm_gain, w_in, mlstm_gate_bias, mlstm_norm_gain, ret_norm_gain, attn_sinks, w_out,
                        final_norm_gain)
    tabs_p = _static_tables(1)
    tabs_s = _static_tables(ROWS // dec_seq)
    ubias = _bias_vectors(rel_bias_table)
    rope_p = (*_rope_tables(jnp.arange(0, seq, ROWS, dtype=jnp.int32), signed=False),
              *_rope_tables(jnp.arange(ROWS, dtype=jnp.int32), signed=False))
    cos_s, sin_s = _rope_tables(past_len + (jnp.arange(ROWS, dtype=jnp.int32) % dec_seq))
    states = (state_mlstm_C, state_mlstm_n, state_mlstm_m, state_ret_S, cache_win_k, cache_win_v)

    xp = x_prompt
    p_states = []
    for layer in range(depth):
        xp, *sp = _prompt_layer(xp, p, layer, tabs_p, ubias, rope_p, layer == depth - 1)
        p_states.append(sp)
    outs_p = [jnp.stack([p_states[l][i] for l in range(depth)]) for i in range(6)]
    xs, *outs_s = _sample_path(x_sample, states, p, tabs_s, ubias, cos_s, sin_s)
    return (xp, xs, *outs_p, *outs_s)
```

```python
import functools

import numpy as np
import jax
import jax.numpy as jnp
from jax import lax
from jax.experimental import pallas as pl
from jax.experimental.pallas import tpu as pltpu

D_MODEL = 1024
HEAD_DIM = 64
M_HEADS = 4
R_HEADS = 4
A_HEADS = 8
A_KV_HEADS = 2
KV_GROUP = A_HEADS // A_KV_HEADS
M_DIM = M_HEADS * HEAD_DIM
R_DIM = R_HEADS * HEAD_DIM
A_DIM = A_HEADS * HEAD_DIM
A_KV_DIM = A_KV_HEADS * HEAD_DIM
WINDOW = 128
N_BUCKETS = 32
REL_MAX_DIST = 128
ROPE_BASE = 10000.0
NORM_EPS = 1e-6
QK_SCALE = HEAD_DIM ** -0.5

LANES = 128
ROWS = 128
GATE_PAD = LANES
PAIRS = M_HEADS // 2
SPLIT_TERMS = 3

OFF_MQ = 0
OFF_MK = OFF_MQ + M_DIM
OFF_MV = OFF_MK + M_DIM
OFF_MO = OFF_MV + M_DIM
OFF_MZ = OFF_MO + M_DIM
OFF_G = OFF_MZ + M_DIM
OFF_RQ = OFF_G + GATE_PAD
OFF_RK = OFF_RQ + R_DIM
OFF_RV = OFF_RK + R_DIM
OFF_RZ = OFF_RV + R_DIM
OFF_AQ = OFF_RZ + R_DIM
OFF_AK = OFF_AQ + A_DIM
OFF_AV = OFF_AK + A_KV_DIM
OFF_AZ = OFF_AV + A_KV_DIM
P_COLS = OFF_AZ + A_DIM
N_GATES = 2 * M_HEADS
PROJ_COL_BLOCK = 512
SCHEDULE = "FBFBFB"
ATTN_HEAD_ORDER = tuple(h for j in range(KV_GROUP) for h in (j, KV_GROUP + j))

PROMPT_ROWS = 512
VMEM_LIMIT_BYTES = 56 * 1024 * 1024
DECODE_VMEM_LIMIT_BYTES = 58 * 1024 * 1024

f32 = jnp.float32
bf16 = jnp.bfloat16


def _dot(a, b):
    return jnp.dot(a, b, preferred_element_type=f32)


def _dot_nt(a, b):
    return lax.dot_general(a, b, (((1,), (1,)), ((), ())), preferred_element_type=f32)


def _sigmoid(x):
    return 1.0 / (1.0 + jnp.exp(-x))


def _silu(x):
    return x * _sigmoid(x)


def _log_sigmoid(x):
    return jnp.minimum(x, 0.0) - jnp.log(1.0 + jnp.exp(-jnp.abs(x)))


def _split_parts(x, terms):
    parts, r = [], x
    for i in range(terms):
        p = r.astype(bf16)
        parts.append(p)
        if i + 1 < terms:
            r = r - p.astype(f32)
    return parts


def _split_terms(x, terms=SPLIT_TERMS):
    return jnp.concatenate(_split_parts(x, terms), axis=1)


def _exact_tril_dot(tril3, x):
    return _dot(tril3, jnp.concatenate(_split_parts(x, SPLIT_TERMS), axis=0))


def _rope(x, cos_t, sin_t, first_half):
    up = pltpu.roll(x, LANES - HEAD_DIM // 2, axis=1)
    down = pltpu.roll(x, HEAD_DIM // 2, axis=1)
    return x * cos_t + jnp.where(first_half, up, down) * sin_t


def _rms_project(x_ref, ngain_ref, win_ref, proj_sc):
    xf = x_ref[...]
    u = xf * lax.rsqrt(jnp.mean(xf * xf, axis=1, keepdims=True) + NORM_EPS) * ngain_ref[...]
    u16 = u.astype(bf16)
    for c0 in range(0, P_COLS, PROJ_COL_BLOCK):
        c1 = min(c0 + PROJ_COL_BLOCK, P_COLS)
        proj_sc[:, c0:c1] = _dot_nt(u16, win_ref[c0:c1, :])


def _prompt_kernel(x_ref, ngain_ref, win_ref, gbias_ref, mgain_ref, rgain_ref, sinks_ref, wout_ref,
                   fgain_ref, tril3_ref, maskadd_ref, dmat2_ref, rslab_ref, ubias_ref, cos_ref,
                   sin_ref, cosoff_ref, sinoff_ref, selh_ref, selp_ref,
                   y_ref, c_out, n_out, m_out, s_out, k_out, v_out,
                   proj_sc, mix_sc, cn_sc, sb_sc, m_sc, kp_sc, vp_sc, bias_sc, *, chunks, final):
    step = pl.program_id(1)
    last_step = pl.num_programs(1) - 1

    @pl.when(jnp.logical_and(pl.program_id(0) == 0, step == 0))
    def _():
        for blk in range(A_HEADS):
            u = ubias_ref[ATTN_HEAD_ORDER[blk]:ATTN_HEAD_ORDER[blk] + 1, :]
            bias_sc[blk * ROWS:(blk + 1) * ROWS, :] = jnp.concatenate(
                [_skew(u[:, :2 * WINDOW], ROWS), _skew(u[:, 2 * WINDOW:], ROWS)], axis=1)

    xf = x_ref[...]
    u16 = (xf * lax.rsqrt(jnp.mean(xf * xf, axis=1, keepdims=True) + NORM_EPS) * ngain_ref[...]).astype(bf16)
    col_blocks = [(c0, min(c0 + PROJ_COL_BLOCK, P_COLS)) for c0 in range(0, P_COLS, PROJ_COL_BLOCK)]
    half_rows = (chunks // 2) * ROWS if chunks > 1 else chunks * ROWS

    def project(r0, r1, c0, c1):
        proj_sc[r0:r1, c0:c1] = _dot_nt(u16[r0:r1], win_ref[c0:c1, :])

    for c0, c1 in col_blocks:
        project(0, half_rows, c0, c1)
    late_pieces = [(half_rows, chunks * ROWS, c0, c1) for c0, c1 in col_blocks] if half_rows < chunks * ROWS else []

    @pl.when(step == 0)
    def _():
        cn_sc[...] = jnp.zeros_like(cn_sc)
        sb_sc[...] = jnp.zeros_like(sb_sc)
        m_sc[...] = jnp.zeros_like(m_sc)
        kp_sc[...] = jnp.zeros_like(kp_sc)
        vp_sc[...] = jnp.zeros_like(vp_sc)

    lane = lax.broadcasted_iota(jnp.int32, (ROWS, LANES), 1)
    row = lax.broadcasted_iota(jnp.int32, (ROWS, LANES), 0)
    left = lane < HEAD_DIM
    first_half = (lane & (HEAD_DIM - 1)) < (HEAD_DIM // 2)
    head_col = lane < M_HEADS
    blockdiag = (row < HEAD_DIM) == left
    row2 = lax.broadcasted_iota(jnp.int32, (ROWS, 2 * LANES), 0)
    lane2w = lax.broadcasted_iota(jnp.int32, (ROWS, 2 * LANES), 1)
    left2 = (lane2w & (LANES - 1)) < HEAD_DIM
    blockdiag2 = (row2 < HEAD_DIM) == left2
    ones16 = jnp.ones((ROWS, LANES), bf16)
    halves, pick, pair_blockdiag, half_mean = _halves, _pick, _pair_blockdiag, _half_mean

    def chunk_body(ci):
        rows = slice(ci * ROWS, (ci + 1) * ROWS)

        def proj(off, width=LANES):
            return proj_sc[rows, off:off + width]

        gates = proj(OFF_G) + gbias_ref[...]
        bcum = _exact_tril_dot(tril3_ref[...], _log_sigmoid(gates))

        base = pl.ds(step * chunks + ci, 1)
        cos_a, sin_a = cos_ref[base, :], sin_ref[base, :]
        cos_b, sin_b = cosoff_ref[...], sinoff_ref[...]
        cos_t = cos_a * cos_b - sin_a * sin_b
        sin_t = sin_a * cos_b + cos_a * sin_b
        sin_t = jnp.where(first_half, -sin_t, sin_t)
        r_q = [_rope(proj(OFF_RQ + p * LANES), cos_t, sin_t, first_half) for p in range(PAIRS)]
        r_k = [_rope(proj(OFF_RK + p * LANES), cos_t, sin_t, first_half) * QK_SCALE for p in range(PAIRS)]
        r_vbd = pair_blockdiag([proj(OFF_RV + p * LANES).astype(bf16) for p in range(PAIRS)])
        r_sb = [sb_sc[p] for p in range(PAIRS)]
        r_sc = _dot_nt(jnp.concatenate([halves(q) for q in r_q], axis=1),
                       pair_blockdiag([k.astype(bf16) for k in r_k]))
        r_inter = _dot(jnp.concatenate([q.astype(bf16) for q in r_q], axis=1),
                       pair_blockdiag([sb.astype(bf16) for sb in r_sb]))
        r_upd = _dot(jnp.concatenate([(r_k[p] * rslab_ref[1, p]).T.astype(bf16) for p in range(PAIRS)], axis=1),
                     r_vbd)
        for p in range(PAIRS):
            sb_sc[p] = rslab_ref[2, p] * r_sb[p] + jnp.where(blockdiag, r_upd[:, p * LANES:(p + 1) * LANES], 0.0)
        yield

        kcur, vcur = proj(OFF_AK), proj(OFF_AV)
        kprev, vprev = kp_sc[...], vp_sc[...]
        kk16 = jnp.concatenate([kcur, kprev], axis=0).astype(bf16)
        vv16 = jnp.concatenate([jnp.concatenate([vcur.astype(bf16), ones16], axis=1),
                                jnp.concatenate([vprev.astype(bf16), ones16], axis=1)], axis=0)
        kp_sc[...] = kcur
        vp_sc[...] = vcur
        a_q = jnp.concatenate([halves(proj(OFF_AQ + j * LANES) * QK_SCALE) for j in range(KV_GROUP)], axis=0)
        a_s = _dot_nt(a_q, kk16) + bias_sc[...]
        if ci == 0:
            pen = jnp.where(step == 0, -jnp.inf, 0.0).astype(f32)
            a_s = a_s + jnp.where(lax.broadcasted_iota(jnp.int32, (1, 2 * ROWS), 1) >= ROWS, pen, 0.0)

        m_q = [proj(OFF_MQ + p * LANES) for p in range(PAIRS)]
        m_k = [proj(OFF_MK + p * LANES) * QK_SCALE for p in range(PAIRS)]
        m_qk = _dot_nt(jnp.concatenate([halves(q) for q in m_q], axis=1),
                       pair_blockdiag([k.astype(bf16) for k in m_k]))
        m_q16 = [q.astype(bf16) for q in m_q]
        m_v16 = [proj(OFF_MV + p * LANES).astype(bf16) for p in range(PAIRS)]
        yield

        zb = pltpu.roll(bcum, LANES - M_HEADS, axis=1)
        r_mat = jnp.where(head_col, gates - zb, 0.0)
        cm = r_mat
        sh = 1
        while sh < ROWS:
            cm = jnp.where(row >= sh, jnp.maximum(cm, pltpu.roll(cm, sh, axis=0)), cm)
            sh *= 2
        mprev = m_sc[...]
        mx = jnp.maximum(mprev, cm)
        gm = mprev - mx
        em = jnp.where(head_col, -(zb + mx), 0.0)
        mx_last = jnp.broadcast_to(mx[ROWS - 1:ROWS, :], (ROWS, LANES))
        m_sc[...] = jnp.where(head_col, jnp.broadcast_to((zb + mx)[ROWS - 1:ROWS, :], (ROWS, LANES)), 0.0)
        mx_b = _dot_nt(_split_terms(mx), selh_ref[...])
        slabs = jnp.exp(_dot_nt(_split_terms(jnp.concatenate([gm, em, r_mat - mx_last], axis=0)),
                                selp_ref[...]))
        winter_b, emt_b, ws_b = slabs[:ROWS], slabs[ROWS:2 * ROWS], slabs[2 * ROWS:]
        r_t = r_mat.T
        yield

        outs = []
        r_acc = []
        r_o = _dot((r_sc * dmat2_ref[...]).astype(bf16), r_vbd)
        for p in range(PAIRS):
            ps = slice(p * LANES, (p + 1) * LANES)
            r_acc.append(pick(r_o[:, ps]) + rslab_ref[0, p] * r_inter[:, ps])

        a_out = []
        a_p = []
        for blk in range(A_HEADS):
            s = a_s[blk * ROWS:(blk + 1) * ROWS]
            sink = sinks_ref[ATTN_HEAD_ORDER[blk]]
            m = jnp.maximum(jnp.max(jnp.maximum(s[:, :ROWS], s[:, ROWS:]), axis=1, keepdims=True), sink)
            a_p.append(jnp.exp(s - m).astype(bf16))
            a_out.append(jnp.exp(sink - m))
        a_pv = _dot(jnp.concatenate(a_p, axis=0), vv16)
        yield

        maskadd = maskadd_ref[...]
        for p in range(PAIRS):
            ps = slice(p * LANES, (p + 1) * LANES)
            cn = cn_sc[p]
            w = jnp.concatenate(
                [jnp.exp((r_t[2 * p + side:2 * p + side + 1, :] + maskadd)
                         - mx_b[:, (2 * p + side) * ROWS:(2 * p + side + 1) * ROWS]) for side in range(2)],
                axis=0) * m_qk[:, ps]
            acc = (pick(_dot(w.astype(bf16), jnp.concatenate([m_v16[p], ones16], axis=1)))
                   + jnp.concatenate([winter_b[:, ps]] * 2, axis=1) * _dot(m_q16[p], cn.astype(bf16)))
            hh = acc[:, :LANES] / jnp.maximum(jnp.abs(acc[:, LANES:]), emt_b[:, ps])
            outs.append(_sigmoid(proj(OFF_MO + p * LANES)) * hh)
            kwt16 = (m_k[p] * ws_b[:, ps]).T.astype(bf16)
            dcn = _dot(kwt16, jnp.concatenate([m_v16[p], ones16], axis=1))
            decay = winter_b[ROWS - 1:ROWS, ps]
            cn_sc[p] = jnp.concatenate([decay, decay], axis=1) * cn + jnp.where(blockdiag2, dcn, 0.0)
        outs.extend(r_acc)
        yield

        x4 = jnp.concatenate(outs, axis=0)
        xc = x4 - half_mean(x4)
        y4 = xc * lax.rsqrt(half_mean(xc * xc) + NORM_EPS)
        for i in range(2 * PAIRS):
            gain = (mgain_ref if i < PAIRS else rgain_ref)[:, (i % PAIRS) * LANES:(i % PAIRS + 1) * LANES]
            zoff = (OFF_MZ if i < PAIRS else OFF_RZ) + (i % PAIRS) * LANES
            out = y4[i * ROWS:(i + 1) * ROWS] * gain * _silu(proj(zoff))
            mix_sc[rows, i * LANES:(i + 1) * LANES] = out.astype(bf16)
        for j in range(KV_GROUP):
            acc = pick(a_pv[2 * j * ROWS:(2 * j + 2) * ROWS])
            den = acc[:, LANES:] + jnp.where(left, a_out[2 * j], a_out[2 * j + 1])
            out = (acc[:, :LANES] / den) * _silu(proj(OFF_AZ + j * LANES))
            mix_sc[rows, M_DIM + R_DIM + j * LANES:M_DIM + R_DIM + (j + 1) * LANES] = out.astype(bf16)

    def out_project(r0, r1, c0, c1):
        y_ref[r0:r1, c0:c1] = x_ref[r0:r1, c0:c1] + _dot(mix_sc[r0:r1, :], wout_ref[:, c0:c1])

    out_blocks = [(c0, min(c0 + PROJ_COL_BLOCK, D_MODEL)) for c0 in range(0, D_MODEL, PROJ_COL_BLOCK)]
    early_out = [(0, half_rows, c0, c1) for c0, c1 in out_blocks] if half_rows < chunks * ROWS else []
    final_out = [(half_rows if early_out else 0, chunks * ROWS, c0, c1) for c0, c1 in out_blocks]

    def fill_mxu(ci):
        if late_pieces:
            project(*late_pieces.pop(0))
        elif early_out and (ci - 1) * ROWS >= half_rows:
            out_project(*early_out.pop(0))

    parts = [chunk_body(ci) for ci in range(chunks)]
    for ci in range(chunks + 1):
        if ci * ROWS >= half_rows:
            while late_pieces:
                project(*late_pieces.pop(0))
        for which in SCHEDULE:
            if which == "F" and ci < chunks:
                next(parts[ci])
                fill_mxu(ci)
            if which == "B" and ci > 0:
                next(parts[ci - 1], None)
                fill_mxu(ci)
    for piece in early_out + final_out:
        out_project(*piece)
    if final:
        y = y_ref[...]
        y_ref[...] = y * lax.rsqrt(jnp.mean(y * y, axis=1, keepdims=True) + NORM_EPS) * fgain_ref[...]

    @pl.when(step == last_step)
    def _():
        for p in range(PAIRS):
            cn = cn_sc[p]
            sb = sb_sc[p]
            n_t = cn[:, LANES:].T
            for side in range(2):
                h = 2 * p + side
                blk = slice(side * HEAD_DIM, (side + 1) * HEAD_DIM)
                c_out[0, h] = cn[blk, blk]
                s_out[0, h] = sb[blk, blk]
                n_out[0, h:h + 1, :] = n_t[side * HEAD_DIM:side * HEAD_DIM + 1, blk]
        m_out[0] = m_sc[...]
        k_out[0] = kp_sc[...]
        v_out[0] = vp_sc[...]


def _lane_is_left(shape):
    return (lax.broadcasted_iota(jnp.int32, shape, 1) & (LANES - 1)) < HEAD_DIM


def _halves(x):
    left = _lane_is_left(x.shape)
    return jnp.concatenate([jnp.where(left, x, 0.0), jnp.where(left, 0.0, x)], axis=0).astype(bf16)


def _pick(x):
    return jnp.where(_lane_is_left((ROWS, x.shape[1])), x[:ROWS], x[ROWS:])


def _pair_blockdiag(blocks):
    z = jnp.zeros_like(blocks[0])
    return jnp.concatenate(
        [jnp.concatenate([blk if j == i else z for j in range(len(blocks))], axis=1)
         for i, blk in enumerate(blocks)], axis=0)


def _half_mean(x):
    left = _lane_is_left(x.shape)
    s_left = jnp.sum(jnp.where(left, x, 0.0), axis=1, keepdims=True)
    s_right = jnp.sum(jnp.where(left, 0.0, x), axis=1, keepdims=True)
    return jnp.where(left, s_left, s_right) * (1.0 / HEAD_DIM)


def _group_last(x, groups):
    n = x.shape[1]
    glen = ROWS // groups
    x3 = x.reshape(groups, glen, n)
    return jnp.broadcast_to(x3[:, glen - 1:glen, :], (groups, glen, n)).reshape(ROWS, n)


def _sample_kernel(x_ref, ngain_ref, win_ref, gbias_ref, mgain_ref, rgain_ref, sinks_ref, wout_ref,
                   fgain_ref, tril3_ref, maskadd_ref, dmat2_ref, rslab_ref, ubias_ref, cos_ref, sin_ref,
                   selh_ref, selp_ref,
                   c_in, n_in, m_in, s_in, k_in, v_in,
                   y_ref, c_out, n_out, m_out, s_out, k_out, v_out,
                   proj_sc, mix_sc, xcur_sc, xall_sc, qb_sc, sp_sc, pp_sc, ob_sc, biasc_ref, biasp_ref, *, groups,
                   ret_full):
    glen = ROWS // groups
    glen_log2 = glen.bit_length() - 1
    hd_log2 = HEAD_DIM.bit_length() - 1
    layer = pl.program_id(0)
    step = pl.program_id(1)
    last_layer = pl.num_programs(0) - 1
    xrows = pl.ds(pl.multiple_of(step * ROWS, ROWS), ROWS)

    @pl.when(layer == 0)
    def _():
        xcur_sc[...] = x_ref[...]

    @pl.when(layer != 0)
    def _():
        xcur_sc[...] = xall_sc[xrows, :]

    for b in range(groups):
        k_out[b] = pltpu.roll(k_in[b], WINDOW - glen, axis=1)
        v_out[b] = pltpu.roll(v_in[b], WINDOW - glen, axis=1)

    _rms_project(xcur_sc, ngain_ref, win_ref, proj_sc)

    @pl.when(jnp.logical_and(step == 0, layer == 0))
    def _():
        for h in range(A_HEADS):
            u = ubias_ref[h:h + 1, :]
            biasc_ref[h] = _skew(u[:, :2 * WINDOW], ROWS) + maskadd_ref[...]
            biasp_ref[h] = jnp.concatenate([_skew(u[:, 2 * WINDOW:], glen)] * groups, axis=0)

    lane = lax.broadcasted_iota(jnp.int32, (ROWS, LANES), 1)
    first_half = (lane & (HEAD_DIM - 1)) < (HEAD_DIM // 2)
    rows = slice(0, ROWS)

    r_i = lax.broadcasted_iota(jnp.int32, (ROWS, groups * HEAD_DIM), 0)
    c_i = lax.broadcasted_iota(jnp.int32, (ROWS, groups * HEAD_DIM), 1)
    blk = (r_i >> glen_log2) == (c_i >> hd_log2)
    r_t = lax.broadcasted_iota(jnp.int32, (groups * HEAD_DIM, ROWS), 0)
    c_t = lax.broadcasted_iota(jnp.int32, (groups * HEAD_DIM, ROWS), 1)
    blk_t = (r_t >> hd_log2) == (c_t >> glen_log2)

    def q_times_state(qh, st):
        qt = jnp.where(blk, jnp.concatenate([qh] * groups, axis=1), 0.0)
        return _dot(qt.astype(bf16), st.astype(bf16))

    def state_increment(kt_h, vh16):
        kt = jnp.where(blk_t, jnp.concatenate([kt_h] * groups, axis=0), 0.0)
        return _dot(kt.astype(bf16), vh16)

    def proj(off, width=LANES):
        return proj_sc[:, off:off + width]

    def head_state_rows(slab, side):
        wide = slab[:, side * HEAD_DIM:(side + 1) * HEAD_DIM].reshape(groups, glen, HEAD_DIM)[:, 0:1, :]
        rows_ = jnp.broadcast_to(wide, (groups, HEAD_DIM, HEAD_DIM)).reshape(groups * HEAD_DIM, HEAD_DIM)
        return rows_, wide.reshape(groups, HEAD_DIM)

    row = lax.broadcasted_iota(jnp.int32, (ROWS, LANES), 0)
    tau = row & (glen - 1)
    head_col = lane < M_HEADS
    left = lane < HEAD_DIM
    ones16 = jnp.ones((ROWS, LANES), bf16)
    gates = proj(OFF_G) + gbias_ref[...]
    bcum = _exact_tril_dot(tril3_ref[...], _log_sigmoid(gates))
    zb = pltpu.roll(bcum, LANES - M_HEADS, axis=1)
    r_mat = jnp.where(head_col, gates - zb, 0.0)
    cm = r_mat
    sh = 1
    while sh < glen:
        cm = jnp.where(tau >= sh, jnp.maximum(cm, pltpu.roll(cm, sh, axis=0)), cm)
        sh *= 2
    mprev = m_in[...]
    mx = jnp.maximum(mprev, cm)
    gm = mprev - mx
    em = jnp.where(head_col, -(zb + mx), 0.0)
    mx_last = _group_last(mx, groups)
    m_out[...] = jnp.where(head_col, _group_last(zb + mx, groups), 0.0)
    mx_b = _dot_nt(_split_terms(mx), selh_ref[...])
    slabs = jnp.exp(_dot_nt(_split_terms(jnp.concatenate([gm, em, r_mat - mx_last], axis=0)), selp_ref[...]))
    winter_b, emt_b, ws_b = slabs[:ROWS], slabs[ROWS:2 * ROWS], slabs[2 * ROWS:]
    decay_b = _group_last(winter_b, groups)
    r_t = r_mat.T
    maskadd = maskadd_ref[...]

    m_q = [proj(OFF_MQ + p * LANES) for p in range(PAIRS)]
    m_k = [proj(OFF_MK + p * LANES) * QK_SCALE for p in range(PAIRS)]
    m_v = [proj(OFF_MV + p * LANES) for p in range(PAIRS)]
    m_qk = _dot_nt(jnp.concatenate([_halves(q) for q in m_q], axis=1),
                   _pair_blockdiag([k.astype(bf16) for k in m_k]))
    outs = []
    for p in range(PAIRS):
        ps = slice(p * LANES, (p + 1) * LANES)
        w = jnp.concatenate(
            [jnp.exp((r_t[2 * p + side:2 * p + side + 1, :] + maskadd)
                     - mx_b[:, (2 * p + side) * ROWS:(2 * p + side + 1) * ROWS]) for side in range(2)],
            axis=0) * m_qk[:, ps]
        intra = _pick(_dot(w.astype(bf16), jnp.concatenate([m_v[p].astype(bf16), ones16], axis=1)))
        kw = m_k[p] * ws_b[:, ps]
        kwt = kw.T
        q_c, q_n = [], []
        for side in range(2):
            h = 2 * p + side
            hs = slice(side * HEAD_DIM, (side + 1) * HEAD_DIM)
            qh = m_q[p][:, hs]
            c_h = c_in[:, h].reshape(groups * HEAD_DIM, HEAD_DIM)
            n_g = n_in[h]
            n_rows = jnp.broadcast_to(n_g.reshape(groups, 1, HEAD_DIM),
                                      (groups, glen, HEAD_DIM)).reshape(ROWS, HEAD_DIM)
            q_c.append(q_times_state(qh, c_h))
            q_n.append(jnp.sum(qh * n_rows, axis=1, keepdims=True))
            dec_rows, dec_g = head_state_rows(decay_b[:, ps], side)
            c_new = dec_rows * c_h + state_increment(kwt[hs, :], m_v[p][:, hs].astype(bf16))
            c_out[:, h] = c_new.reshape(groups, HEAD_DIM, HEAD_DIM)
            n_out[h] = dec_g * n_g + jnp.sum(kw[:, hs].reshape(groups, glen, HEAD_DIM), axis=1)
        wb = winter_b[:, ps]
        num = intra[:, :LANES] + wb * jnp.concatenate(q_c, axis=1)
        nq = intra[:, LANES:] + wb * jnp.where(left, q_n[0], q_n[1])
        outs.append(_sigmoid(proj(OFF_MO + p * LANES)) * (num / jnp.maximum(jnp.abs(nq), emt_b[:, ps])))

    cos_t, sin_t = cos_ref[...], sin_ref[...]
    r_q = [_rope(proj(OFF_RQ + p * LANES), cos_t, sin_t, first_half) for p in range(PAIRS)]
    r_k = [_rope(proj(OFF_RK + p * LANES), cos_t, sin_t, first_half) * QK_SCALE for p in range(PAIRS)]
    r_v = [proj(OFF_RV + p * LANES) for p in range(PAIRS)]
    r_sc = _dot_nt(jnp.concatenate([_halves(q) for q in r_q], axis=1),
                   _pair_blockdiag([k.astype(bf16) for k in r_k]))
    r_o = _dot((r_sc * dmat2_ref[...]).astype(bf16), _pair_blockdiag([v.astype(bf16) for v in r_v]))
    for p in range(PAIRS):
        ps = slice(p * LANES, (p + 1) * LANES)
        rkt = (r_k[p] * rslab_ref[1, p]).T
        q_s = []
        for side in range(2):
            h = 2 * p + side
            hs = slice(side * HEAD_DIM, (side + 1) * HEAD_DIM)
            s_h = s_in[:, h].reshape(groups * HEAD_DIM, HEAD_DIM)
            q_s.append(q_times_state(r_q[p][:, hs], s_h))
            s_new = ret_full[h] * s_h + state_increment(rkt[hs, :], r_v[p][:, hs].astype(bf16))
            s_out[:, h] = s_new.reshape(groups, HEAD_DIM, HEAD_DIM)
        outs.append(_pick(r_o[:, ps]) + rslab_ref[0, p] * jnp.concatenate(q_s, axis=1))

    x4 = jnp.concatenate(outs, axis=0)
    xc = x4 - _half_mean(x4)
    y4 = xc * lax.rsqrt(_half_mean(xc * xc) + NORM_EPS)
    for i in range(2 * PAIRS):
        gain = (mgain_ref if i < PAIRS else rgain_ref)[:, (i % PAIRS) * LANES:(i % PAIRS + 1) * LANES]
        zoff = (OFF_MZ if i < PAIRS else OFF_RZ) + (i % PAIRS) * LANES
        mix_sc[:, i * LANES:(i + 1) * LANES] = (y4[i * ROWS:(i + 1) * ROWS] * gain * _silu(proj(zoff))).astype(bf16)

    kcur = proj_sc[rows, OFF_AK:OFF_AK + A_KV_DIM]
    vcur = proj_sc[rows, OFF_AV:OFF_AV + A_KV_DIM]
    kcur16, vcur16 = kcur.astype(bf16), vcur.astype(bf16)

    a_q = {}
    for j in range(KV_GROUP):
        hq = _halves(proj(OFF_AQ + j * LANES) * QK_SCALE)
        a_q[j], a_q[KV_GROUP + j] = hq[:ROWS], hq[ROWS:]
    for h in range(A_HEADS):
        qb_sc[:, h * glen:(h + 1) * glen, :] = a_q[h].reshape(groups, glen, LANES)
    sc_all = _dot_nt(jnp.concatenate([a_q[h] for h in range(A_HEADS)], axis=0), kcur16)

    for b in range(groups):
        sp = _dot(qb_sc[b], k_in[b].astype(bf16))
        sp_sc[:, b * glen:(b + 1) * glen, :] = sp.reshape(A_HEADS, glen, WINDOW)

    vcur1 = jnp.concatenate([vcur16, ones16], axis=1)
    esink, o_cur = [], []
    for h in range(A_HEADS):
        sc = sc_all[h * ROWS:(h + 1) * ROWS] + biasc_ref[h]
        sp = sp_sc[h] + biasp_ref[h]
        sink = sinks_ref[layer, h]
        m = jnp.maximum(jnp.max(jnp.maximum(sc, sp), axis=1, keepdims=True), sink)
        esink.append(jnp.exp(sink - m))
        o_cur.append(_dot(jnp.exp(sc - m).astype(bf16), vcur1))
        pp_sc[:, h * glen:(h + 1) * glen, :] = jnp.exp(sp - m).reshape(groups, glen, WINDOW).astype(bf16)

    for b in range(groups):
        v1 = jnp.concatenate([v_in[b].astype(bf16), ones16], axis=0)
        ob = _dot_nt(pp_sc[b], v1)
        ob_sc[:, b * glen:(b + 1) * glen, :] = ob.reshape(A_HEADS, glen, 2 * LANES)
    norm = []
    for h in range(A_HEADS):
        acc = o_cur[h] + ob_sc[h]
        norm.append(acc[:, :LANES] / (acc[:, LANES:] + esink[h]))
    outs = [jnp.where(left, norm[j], norm[KV_GROUP + j]) for j in range(KV_GROUP)]
    kcur_t, vcur_t = kcur.T, vcur.T
    new = slice(WINDOW - glen, WINDOW)
    for b in range(groups):
        shift = (WINDOW - glen - b * glen) % LANES
        k_out[b, :, new] = pltpu.roll(kcur_t, shift, axis=1)[:, new]
        v_out[b, :, new] = pltpu.roll(vcur_t, shift, axis=1)[:, new]
    out_a = jnp.concatenate(outs, axis=1) * _silu(proj_sc[rows, OFF_AZ:OFF_AZ + A_DIM])
    mix_sc[rows, M_DIM + R_DIM:M_DIM + R_DIM + A_DIM] = out_a.astype(bf16)

    y = xcur_sc[...] + _dot(mix_sc[...], wout_ref[...])

    @pl.when(layer != last_layer)
    def _():
        xall_sc[xrows, :] = y
        y_ref[...] = y

    @pl.when(layer == last_layer)
    def _():
        y_ref[...] = y * lax.rsqrt(jnp.mean(y * y, axis=1, keepdims=True) + NORM_EPS) * fgain_ref[...]


def _t5_bucket(dist):
    max_exact = N_BUCKETS // 2
    d = np.maximum(dist, 1).astype(np.float32)
    large = max_exact + (np.log(d / max_exact) / np.log(REL_MAX_DIST / max_exact)
                         * (N_BUCKETS - max_exact)).astype(np.int32)
    large = np.minimum(large, N_BUCKETS - 1)
    return np.where(dist < max_exact, dist, large).astype(np.int32)


def _static_tables(groups):
    glen = ROWS // groups
    r = np.arange(ROWS)
    grp, tau = r // glen, r % glen
    causal = (grp[:, None] == grp[None, :]) & (tau[None, :] <= tau[:, None])
    tril = causal.astype(np.float32)
    maskadd = np.where(causal, 0.0, -np.inf).astype(np.float32)
    log_g = np.log1p(-np.exp2(-5.0 - np.arange(R_HEADS, dtype=np.float64)))
    diff = (tau[:, None] - tau[None, :]).astype(np.float64)
    dmat = np.where(causal[None], np.exp(log_g[:, None, None] * np.maximum(diff, 0.0)[None]), 0.0)
    inter = np.exp(log_g[None, :] * (tau[:, None] + 1.0))
    tail = np.exp(log_g[None, :] * (glen - 1.0 - tau[:, None]))
    full = np.exp(log_g * glen)
    lane_head = np.arange(LANES) // HEAD_DIM
    rslab = np.zeros((3, PAIRS, ROWS, LANES), np.float64)
    for p in range(PAIRS):
        rslab[0, p] = inter[:, 2 * p + lane_head]
        rslab[1, p] = tail[:, 2 * p + lane_head]
        rslab[2, p] = full[2 * p + lane_head][None, :]
    selh = np.zeros((M_HEADS * ROWS, SPLIT_TERMS * LANES), np.float32)
    selp = np.zeros((PAIRS * LANES, SPLIT_TERMS * LANES), np.float32)
    for t in range(SPLIT_TERMS):
        for h in range(M_HEADS):
            selh[h * ROWS:(h + 1) * ROWS, t * LANES + h] = 1.0
        for p in range(PAIRS):
            for side in range(2):
                selp[p * LANES + side * HEAD_DIM:p * LANES + (side + 1) * HEAD_DIM, t * LANES + 2 * p + side] = 1.0
    return dict(tril3=jnp.asarray(np.concatenate([tril] * SPLIT_TERMS, axis=1), bf16),
                maskadd=maskadd,
                dmat2=np.concatenate(list(dmat.astype(np.float32).reshape(PAIRS, 2 * ROWS, ROWS)), axis=1),
                rslab=rslab.astype(np.float32),
                full=tuple(float(v) for v in full),
                selh=jnp.asarray(selh, bf16), selp=jnp.asarray(selp, bf16))


def _bias_vectors(rel_table):
    tb = jnp.transpose(rel_table[_t5_bucket(np.arange(WINDOW))]).astype(f32)
    ninf = jnp.full((A_HEADS, WINDOW), -jnp.inf, f32)
    rev = tb[:, :0:-1]
    return jnp.concatenate([tb[:, :1], ninf, rev, ninf[:, :1], rev, ninf], axis=1)


def _skew(u_row, rows):
    x = jnp.broadcast_to(u_row, (rows, 2 * WINDOW))
    return pltpu.roll(x, 0, 1, stride=1, stride_axis=0)[:, :WINDOW]


def _rope_tables(pos, signed=True):
    half = HEAD_DIM // 2
    inv = ROPE_BASE ** (-jnp.arange(half, dtype=f32) / half)
    ang = pos.astype(f32)[:, None] * inv[None, :]
    cos, sin = jnp.cos(ang), jnp.sin(ang)
    reps = LANES // HEAD_DIM
    cos_t = jnp.tile(jnp.concatenate([cos, cos], axis=1), (1, reps))
    sin_t = jnp.tile(jnp.concatenate([-sin if signed else sin, sin], axis=1), (1, reps))
    return cos_t, sin_t


def _const_spec(shape, nargs):
    zeros = (0,) * len(shape)
    if nargs == 1:
        return pl.BlockSpec(shape, lambda i: zeros)
    return pl.BlockSpec(shape, lambda i, j: zeros)


def _layer_spec(shape, layer, nargs):
    idx = (layer,) + (0,) * len(shape)
    if nargs == 1:
        return pl.BlockSpec((None,) + shape, lambda i: idx)
    return pl.BlockSpec((None,) + shape, lambda i, j: idx)


def _param_specs(layer, nargs):
    ls = functools.partial(_layer_spec, layer=layer, nargs=nargs)
    return [ls((1, D_MODEL)), ls((P_COLS, D_MODEL)), ls((1, LANES)), ls((1, M_DIM)), ls((1, R_DIM)),
            pl.BlockSpec(memory_space=pltpu.SMEM), ls((D_MODEL, D_MODEL)), _const_spec((1, D_MODEL), nargs)]


def _param_args(p, layer):
    return (p["norm_gain"], p["w_in"], p["gbias"], p["m_gain"], p["r_gain"], p["sinks"][layer], p["w_out"],
            p["fgain"])


def _prompt_layer(x, p, layer, tabs, ubias, rope, final):
    B, T, _ = x.shape
    tb = min(PROMPT_ROWS, T)
    chunks = tb // ROWS
    nt = T // tb
    cs = functools.partial(_const_spec, nargs=2)
    in_specs = [pl.BlockSpec((None, tb, D_MODEL), lambda b, t: (b, t, 0))] + _param_specs(layer, 2) + [
        cs((ROWS, SPLIT_TERMS * ROWS)), cs((ROWS, ROWS)), cs((2 * ROWS, PAIRS * ROWS)),
        cs((3, PAIRS, ROWS, LANES)), cs((A_HEADS, 4 * WINDOW)),
        cs((T // ROWS, LANES)), cs((T // ROWS, LANES)), cs((ROWS, LANES)), cs((ROWS, LANES)),
        cs((M_HEADS * ROWS, SPLIT_TERMS * LANES)), cs((PAIRS * LANES, SPLIT_TERMS * LANES)),
    ]
    out_shape = (
        jax.ShapeDtypeStruct((B, T, D_MODEL), f32),
        jax.ShapeDtypeStruct((B, M_HEADS, HEAD_DIM, HEAD_DIM), f32),
        jax.ShapeDtypeStruct((B, M_HEADS, HEAD_DIM), f32),
        jax.ShapeDtypeStruct((B, ROWS, LANES), f32),
        jax.ShapeDtypeStruct((B, R_HEADS, HEAD_DIM, HEAD_DIM), f32),
        jax.ShapeDtypeStruct((B, WINDOW, A_KV_DIM), f32),
        jax.ShapeDtypeStruct((B, WINDOW, A_KV_DIM), f32),
    )
    out_specs = (
        pl.BlockSpec((None, tb, D_MODEL), lambda b, t: (b, t, 0)),
        pl.BlockSpec((1, M_HEADS, HEAD_DIM, HEAD_DIM), lambda b, t: (b, 0, 0, 0)),
        pl.BlockSpec((1, M_HEADS, HEAD_DIM), lambda b, t: (b, 0, 0)),
        pl.BlockSpec((1, ROWS, LANES), lambda b, t: (b, 0, 0)),
        pl.BlockSpec((1, R_HEADS, HEAD_DIM, HEAD_DIM), lambda b, t: (b, 0, 0, 0)),
        pl.BlockSpec((1, WINDOW, A_KV_DIM), lambda b, t: (b, 0, 0)),
        pl.BlockSpec((1, WINDOW, A_KV_DIM), lambda b, t: (b, 0, 0)),
    )
    kern = functools.partial(_prompt_kernel, chunks=chunks, final=final)
    y, c, n, m, s, k, v = pl.pallas_call(
        kern, grid=(B, nt), in_specs=in_specs, out_specs=out_specs, out_shape=out_shape,
        scratch_shapes=[pltpu.VMEM((tb, P_COLS), f32), pltpu.VMEM((tb, D_MODEL), bf16),
                        pltpu.VMEM((PAIRS, ROWS, 2 * LANES), f32), pltpu.VMEM((PAIRS, ROWS, LANES), f32),
                        pltpu.VMEM((ROWS, LANES), f32), pltpu.VMEM((ROWS, A_KV_DIM), f32),
                        pltpu.VMEM((ROWS, A_KV_DIM), f32), pltpu.VMEM((A_HEADS * ROWS, 2 * WINDOW), f32)],
        compiler_params=pltpu.CompilerParams(dimension_semantics=("arbitrary", "arbitrary"),
                                             vmem_limit_bytes=VMEM_LIMIT_BYTES),
        name="prompt_layer",
    )(x, *_param_args(p, layer), tabs["tril3"], tabs["maskadd"], tabs["dmat2"], tabs["rslab"], ubias,
      *rope, tabs["selh"], tabs["selp"])
    k = k.reshape(B, WINDOW, A_KV_HEADS, HEAD_DIM)
    v = v.reshape(B, WINDOW, A_KV_HEADS, HEAD_DIM)
    return y, c, n, m[:, 0, :M_HEADS], s, k, v


def _sample_path(x, states, p, tabs, ubias, cos_t, sin_t):
    B, T, _ = x.shape
    groups = ROWS // T
    nb = B // groups
    c0, n0, m0, s0, k0, v0 = states
    depth = c0.shape[0]
    x2 = x.reshape(B * T, D_MODEL)
    n0t = jnp.transpose(n0, (0, 2, 1, 3))
    m0r = jnp.pad(jnp.repeat(m0, T, axis=1), ((0, 0), (0, 0), (0, LANES - M_HEADS)))
    k0r = jnp.transpose(k0, (0, 1, 3, 4, 2)).reshape(depth, B, A_KV_DIM, WINDOW)
    v0r = jnp.transpose(v0, (0, 1, 3, 4, 2)).reshape(depth, B, A_KV_DIM, WINDOW)

    def cs(shape):
        zeros = (0,) * len(shape)
        return pl.BlockSpec(shape, lambda l, i: zeros)

    def per_layer(shape):
        zeros = (0,) * len(shape)
        return pl.BlockSpec((None,) + shape, lambda l, i: (l,) + zeros)

    st4 = pl.BlockSpec((None, groups, M_HEADS, HEAD_DIM, HEAD_DIM), lambda l, i: (l, i, 0, 0, 0))
    stn = pl.BlockSpec((None, M_HEADS, groups, HEAD_DIM), lambda l, i: (l, 0, i, 0))
    stm = pl.BlockSpec((None, ROWS, LANES), lambda l, i: (l, i, 0))
    stk = pl.BlockSpec((None, groups, A_KV_DIM, WINDOW), lambda l, i: (l, i, 0, 0))
    rows_spec = pl.BlockSpec((ROWS, D_MODEL), lambda l, i: (i, 0))
    in_specs = [
        rows_spec,
        per_layer((1, D_MODEL)), per_layer((P_COLS, D_MODEL)), per_layer((1, LANES)), per_layer((1, M_DIM)),
        per_layer((1, R_DIM)), pl.BlockSpec(memory_space=pltpu.SMEM), per_layer((D_MODEL, D_MODEL)),
        cs((1, D_MODEL)),
        cs((ROWS, SPLIT_TERMS * ROWS)), cs((ROWS, ROWS)), cs((2 * ROWS, PAIRS * ROWS)),
        cs((3, PAIRS, ROWS, LANES)), cs((A_HEADS, 4 * WINDOW)),
        cs((ROWS, LANES)), cs((ROWS, LANES)),
        cs((M_HEADS * ROWS, SPLIT_TERMS * LANES)), cs((PAIRS * LANES, SPLIT_TERMS * LANES)),
        st4, stn, stm, st4, stk, stk,
    ]
    out_shape = (
        jax.ShapeDtypeStruct((B * T, D_MODEL), f32),
        jax.ShapeDtypeStruct((depth, B, M_HEADS, HEAD_DIM, HEAD_DIM), f32),
        jax.ShapeDtypeStruct((depth, M_HEADS, B, HEAD_DIM), f32),
        jax.ShapeDtypeStruct((depth, B * T, LANES), f32),
        jax.ShapeDtypeStruct((depth, B, R_HEADS, HEAD_DIM, HEAD_DIM), f32),
        jax.ShapeDtypeStruct((depth, B, A_KV_DIM, WINDOW), f32),
        jax.ShapeDtypeStruct((depth, B, A_KV_DIM, WINDOW), f32),
    )
    y_spec = pl.BlockSpec((ROWS, D_MODEL), lambda l, i: (jnp.where(l == depth - 1, i, 0), 0))
    out_specs = (y_spec, st4, stn, stm, st4, stk, stk)
    kern = functools.partial(_sample_kernel, groups=groups, ret_full=tabs["full"])
    y, c, n, m, s, k, v = pl.pallas_call(
        kern, grid=(depth, nb), in_specs=in_specs, out_specs=out_specs, out_shape=out_shape,
        scratch_shapes=[pltpu.VMEM((ROWS, P_COLS), f32), pltpu.VMEM((ROWS, D_MODEL), bf16),
                        pltpu.VMEM((ROWS, D_MODEL), f32), pltpu.VMEM((B * T, D_MODEL), f32),
                        pltpu.VMEM((groups, A_HEADS * T, A_KV_DIM), bf16),
                        pltpu.VMEM((A_HEADS, ROWS, WINDOW), f32),
                        pltpu.VMEM((groups, A_HEADS * T, WINDOW), bf16),
                        pltpu.VMEM((A_HEADS, ROWS, 2 * LANES), f32),
                        pltpu.VMEM((A_HEADS, ROWS, ROWS), f32), pltpu.VMEM((A_HEADS, ROWS, WINDOW), f32)],
        compiler_params=pltpu.CompilerParams(dimension_semantics=("arbitrary", "arbitrary"),
                                             vmem_limit_bytes=DECODE_VMEM_LIMIT_BYTES),
        name="sample_path",
    )(x2, p["norm_gain"], p["w_in"], p["gbias"], p["m_gain"], p["r_gain"], p["sinks"], p["w_out"], p["fgain"],
      tabs["tril3"], tabs["maskadd"], tabs["dmat2"], tabs["rslab"], ubias,
      cos_t, sin_t, tabs["selh"], tabs["selp"], c0, n0t, m0r, s0, k0r, v0r)
    y = y.reshape(B, T, D_MODEL)
    n = jnp.transpose(n, (0, 2, 1, 3))
    m = m.reshape(depth, B, T, LANES)[:, :, 0, :M_HEADS]
    k = jnp.transpose(k.reshape(depth, B, A_KV_HEADS, HEAD_DIM, WINDOW), (0, 1, 4, 2, 3))
    v = jnp.transpose(v.reshape(depth, B, A_KV_HEADS, HEAD_DIM, WINDOW), (0, 1, 4, 2, 3))
    return y, c, n, m, s, k, v


def _prepare_params(norm_gain, w_in, mlstm_gate_bias, mlstm_norm_gain, ret_norm_gain, attn_sinks, w_out,
                    final_norm_gain):
    depth = w_in.shape[0]
    w_t = jnp.swapaxes(w_in, 1, 2)
    split = OFF_G + N_GATES
    aq0 = split + 4 * R_DIM
    akv0 = aq0 + A_DIM
    az0 = akv0 + 2 * A_KV_DIM

    def by_head(w):
        w = w.reshape(depth, A_HEADS, HEAD_DIM, D_MODEL)
        return jnp.concatenate([w[:, h] for h in ATTN_HEAD_ORDER], axis=1)

    w_in_p = jnp.concatenate(
        [w_t[:, :split], jnp.zeros((depth, GATE_PAD - N_GATES, D_MODEL), w_t.dtype), w_t[:, split:aq0],
         by_head(w_t[:, aq0:akv0]), w_t[:, akv0:az0], by_head(w_t[:, az0:])], axis=1).astype(bf16)
    wo16 = w_out.astype(bf16)
    a0 = M_DIM + R_DIM
    w_out_p = jnp.concatenate([wo16[:, :a0, :], by_head(wo16[:, a0:, :])], axis=1)
    gbias = jnp.pad(mlstm_gate_bias.reshape(depth, 1, N_GATES), ((0, 0), (0, 0), (0, LANES - N_GATES)))
    return dict(norm_gain=norm_gain.reshape(depth, 1, D_MODEL), w_in=w_in_p, gbias=gbias,
                m_gain=mlstm_norm_gain.reshape(depth, 1, M_DIM), r_gain=ret_norm_gain.reshape(depth, 1, R_DIM),
                sinks=attn_sinks, w_out=w_out_p, fgain=final_norm_gain.reshape(1, D_MODEL))


def kernel(x_prompt, x_sample, state_mlstm_C, state_mlstm_n, state_mlstm_m, state_ret_S, cache_win_k,
           cache_win_v, norm_gain, w_in, mlstm_gate_bias, mlstm_norm_gain, ret_norm_gain, attn_sinks,
           rel_bias_table, w_out, final_norm_gain):
    depth = w_in.shape[0]
    seq = x_prompt.shape[1]
    dec_seq = x_sample.shape[1]
    past_len = seq
    p = _prepare_params(norm_gain, w_in, mlstm_gate_bias, mlstm_norm_gain, ret_norm_gain, attn_sinks, w_out,
                        final_norm_gain)
    tabs_p = _static_tables(1)
    tabs_s = _static_tables(ROWS // dec_seq)
    ubias = _bias_vectors(rel_bias_table)
    rope_p = (*_rope_tables(jnp.arange(0, seq, ROWS, dtype=jnp.int32), signed=False),
              *_rope_tables(jnp.arange(ROWS, dtype=jnp.int32), signed=False))
    cos_s, sin_s = _rope_tables(past_len + (jnp.arange(ROWS, dtype=jnp.int32) % dec_seq))
    states = (state_mlstm_C, state_mlstm_n, state_mlstm_m, state_ret_S, cache_win_k, cache_win_v)

    xp = x_prompt
    p_states = []
    for layer in range(depth):
        xp, *sp = _prompt_layer(xp, p, layer, tabs_p, ubias, rope_p, layer == depth - 1)
        p_states.append(sp)
    outs_p = [jnp.stack([p_states[l][i] for l in range(depth)]) for i in range(6)]
    xs, *outs_s = _sample_path(x_sample, states, p, tabs_s, ubias, cos_s, sin_s)
    return (xp, xs, *outs_p, *outs_s)
```

```python
import functools
import math

import numpy as np
import jax
import jax.numpy as jnp
from jax import lax
from jax.experimental import pallas as pl
from jax.experimental.pallas import tpu as pltpu

D_MODEL = 1024
HEAD_DIM = 64
M_HEADS = 4
R_HEADS = 4
A_HEADS = 8
A_KV_HEADS = 2
KV_GROUP = A_HEADS // A_KV_HEADS
M_DIM = M_HEADS * HEAD_DIM
R_DIM = R_HEADS * HEAD_DIM
A_DIM = A_HEADS * HEAD_DIM
A_KV_DIM = A_KV_HEADS * HEAD_DIM
WINDOW = 128
N_BUCKETS = 32
REL_MAX_DIST = 128
ROPE_BASE = 10000.0
NORM_EPS = 1e-6
QK_SCALE = HEAD_DIM ** -0.5

LANES = 128
ROWS = 128
GATE_PAD = LANES
PAIRS = M_HEADS // 2
SPLIT_TERMS = 3

OFF_MQ = 0
OFF_MK = OFF_MQ + M_DIM
OFF_MV = OFF_MK + M_DIM
OFF_MO = OFF_MV + M_DIM
OFF_MZ = OFF_MO + M_DIM
OFF_G = OFF_MZ + M_DIM
OFF_RQ = OFF_G + GATE_PAD
OFF_RK = OFF_RQ + R_DIM
OFF_RV = OFF_RK + R_DIM
OFF_RZ = OFF_RV + R_DIM
OFF_AQ = OFF_RZ + R_DIM
OFF_AK = OFF_AQ + A_DIM
OFF_AV = OFF_AK + A_KV_DIM
OFF_AZ = OFF_AV + A_KV_DIM
P_COLS = OFF_AZ + A_DIM
N_GATES = 2 * M_HEADS
PROJ_COL_BLOCK = 512
SCHEDULE = "FBFBFB"
ATTN_HEAD_ORDER = tuple(h for j in range(KV_GROUP) for h in (j, KV_GROUP + j))

PROMPT_ROWS = 512
DECODE_PROJ_GROUPS = 4
VMEM_LIMIT_BYTES = 56 * 1024 * 1024
DECODE_VMEM_LIMIT_BYTES = 58 * 1024 * 1024

f32 = jnp.float32
bf16 = jnp.bfloat16


def _dot(a, b):
    return jnp.dot(a, b, preferred_element_type=f32)


def _dot_nt(a, b):
    return lax.dot_general(a, b, (((1,), (1,)), ((), ())), preferred_element_type=f32)


def _sigmoid(x):
    return 1.0 / (1.0 + jnp.exp(-x))


def _silu(x):
    return x * _sigmoid(x)


def _log_sigmoid(x):
    return jnp.minimum(x, 0.0) - jnp.log(1.0 + jnp.exp(-jnp.abs(x)))


def _split_parts(x, terms):
    parts, r = [], x
    for i in range(terms):
        p = r.astype(bf16)
        parts.append(p)
        if i + 1 < terms:
            r = r - p.astype(f32)
    return parts


def _split_terms(x, terms=SPLIT_TERMS):
    return jnp.concatenate(_split_parts(x, terms), axis=1)


def _exact_tril_dot(tril3, x):
    return _dot(tril3, jnp.concatenate(_split_parts(x, SPLIT_TERMS), axis=0))


def _rope(x, cos_t, sin_t, first_half):
    up = pltpu.roll(x, LANES - HEAD_DIM // 2, axis=1)
    down = pltpu.roll(x, HEAD_DIM // 2, axis=1)
    return x * cos_t + jnp.where(first_half, up, down) * sin_t


def _rms_project(x_ref, ngain_ref, win_ref, proj_sc):
    xf = x_ref[...]
    u = xf * lax.rsqrt(jnp.mean(xf * xf, axis=1, keepdims=True) + NORM_EPS) * ngain_ref[...]
    u16 = u.astype(bf16)
    for c0 in range(0, P_COLS, PROJ_COL_BLOCK):
        c1 = min(c0 + PROJ_COL_BLOCK, P_COLS)
        proj_sc[:, c0:c1] = _dot_nt(u16, win_ref[c0:c1, :])


def _prompt_kernel(x_ref, ngain_ref, win_ref, gbias_ref, mgain_ref, rgain_ref, sinks_ref, wout_ref,
                   fgain_ref, tril3_ref, maskadd_ref, dmat2_ref, rslab_ref, ubias_ref, cos_ref,
                   sin_ref, cosoff_ref, sinoff_ref, selh_ref, selp_ref,
                   y_ref, c_out, n_out, m_out, s_out, k_out, v_out,
                   proj_sc, mix_sc, cn_sc, sb_sc, m_sc, kp_sc, vp_sc, bias_sc, *, chunks, final):
    step = pl.program_id(1)
    last_step = pl.num_programs(1) - 1

    @pl.when(jnp.logical_and(pl.program_id(0) == 0, step == 0))
    def _():
        for blk in range(A_HEADS):
            u = ubias_ref[ATTN_HEAD_ORDER[blk]:ATTN_HEAD_ORDER[blk] + 1, :]
            bias_sc[blk * ROWS:(blk + 1) * ROWS, :] = jnp.concatenate(
                [_skew(u[:, :2 * WINDOW], ROWS), _skew(u[:, 2 * WINDOW:], ROWS)], axis=1)

    xf = x_ref[...]
    u16 = (xf * lax.rsqrt(jnp.mean(xf * xf, axis=1, keepdims=True) + NORM_EPS) * ngain_ref[...]).astype(bf16)
    col_blocks = [(c0, min(c0 + PROJ_COL_BLOCK, P_COLS)) for c0 in range(0, P_COLS, PROJ_COL_BLOCK)]
    half_rows = (chunks // 2) * ROWS if chunks > 1 else chunks * ROWS

    def project(r0, r1, c0, c1):
        proj_sc[r0:r1, c0:c1] = _dot_nt(u16[r0:r1], win_ref[c0:c1, :])

    for c0, c1 in col_blocks:
        project(0, half_rows, c0, c1)
    late_pieces = [(half_rows, chunks * ROWS, c0, c1) for c0, c1 in col_blocks] if half_rows < chunks * ROWS else []

    @pl.when(step == 0)
    def _():
        cn_sc[...] = jnp.zeros_like(cn_sc)
        sb_sc[...] = jnp.zeros_like(sb_sc)
        m_sc[...] = jnp.zeros_like(m_sc)
        kp_sc[...] = jnp.zeros_like(kp_sc)
        vp_sc[...] = jnp.zeros_like(vp_sc)

    lane = lax.broadcasted_iota(jnp.int32, (ROWS, LANES), 1)
    row = lax.broadcasted_iota(jnp.int32, (ROWS, LANES), 0)
    left = lane < HEAD_DIM
    first_half = (lane & (HEAD_DIM - 1)) < (HEAD_DIM // 2)
    head_col = lane < M_HEADS
    blockdiag = (row < HEAD_DIM) == left
    row2 = lax.broadcasted_iota(jnp.int32, (ROWS, 2 * LANES), 0)
    lane2w = lax.broadcasted_iota(jnp.int32, (ROWS, 2 * LANES), 1)
    left2 = (lane2w & (LANES - 1)) < HEAD_DIM
    blockdiag2 = (row2 < HEAD_DIM) == left2
    ones16 = jnp.ones((ROWS, LANES), bf16)
    halves, pick, pair_blockdiag, half_mean = _halves, _pick, _pair_blockdiag, _half_mean

    def chunk_body(ci):
        rows = slice(ci * ROWS, (ci + 1) * ROWS)

        def proj(off, width=LANES):
            return proj_sc[rows, off:off + width]

        gates = proj(OFF_G) + gbias_ref[...]
        bcum = _exact_tril_dot(tril3_ref[...], _log_sigmoid(gates))

        base = pl.ds(step * chunks + ci, 1)
        cos_a, sin_a = cos_ref[base, :], sin_ref[base, :]
        cos_b, sin_b = cosoff_ref[...], sinoff_ref[...]
        cos_t = cos_a * cos_b - sin_a * sin_b
        sin_t = sin_a * cos_b + cos_a * sin_b
        sin_t = jnp.where(first_half, -sin_t, sin_t)
        r_q = [_rope(proj(OFF_RQ + p * LANES), cos_t, sin_t, first_half) for p in range(PAIRS)]
        r_k = [_rope(proj(OFF_RK + p * LANES), cos_t, sin_t, first_half) * QK_SCALE for p in range(PAIRS)]
        r_vbd = pair_blockdiag([proj(OFF_RV + p * LANES).astype(bf16) for p in range(PAIRS)])
        r_sb = [sb_sc[p] for p in range(PAIRS)]
        r_sc = _dot_nt(jnp.concatenate([halves(q) for q in r_q], axis=1),
                       pair_blockdiag([k.astype(bf16) for k in r_k]))
        r_inter = _dot(jnp.concatenate([q.astype(bf16) for q in r_q], axis=1),
                       pair_blockdiag([sb.astype(bf16) for sb in r_sb]))
        r_upd = _dot(jnp.concatenate([(r_k[p] * rslab_ref[1, p]).T.astype(bf16) for p in range(PAIRS)], axis=1),
                     r_vbd)
        for p in range(PAIRS):
            sb_sc[p] = rslab_ref[2, p] * r_sb[p] + jnp.where(blockdiag, r_upd[:, p * LANES:(p + 1) * LANES], 0.0)
        yield

        kcur, vcur = proj(OFF_AK), proj(OFF_AV)
        kprev, vprev = kp_sc[...], vp_sc[...]
        kk16 = jnp.concatenate([kcur, kprev], axis=0).astype(bf16)
        vv16 = jnp.concatenate([jnp.concatenate([vcur.astype(bf16), ones16], axis=1),
                                jnp.concatenate([vprev.astype(bf16), ones16], axis=1)], axis=0)
        kp_sc[...] = kcur
        vp_sc[...] = vcur
        a_q = jnp.concatenate([halves(proj(OFF_AQ + j * LANES) * QK_SCALE) for j in range(KV_GROUP)], axis=0)
        a_s = _dot_nt(a_q, kk16) + bias_sc[...]
        if ci == 0:
            pen = jnp.where(step == 0, -jnp.inf, 0.0).astype(f32)
            a_s = a_s + jnp.where(lax.broadcasted_iota(jnp.int32, (1, 2 * ROWS), 1) >= ROWS, pen, 0.0)

        m_q = [proj(OFF_MQ + p * LANES) for p in range(PAIRS)]
        m_k = [proj(OFF_MK + p * LANES) * QK_SCALE for p in range(PAIRS)]
        m_qk = _dot_nt(jnp.concatenate([halves(q) for q in m_q], axis=1),
                       pair_blockdiag([k.astype(bf16) for k in m_k]))
        m_q16 = [q.astype(bf16) for q in m_q]
        m_v16 = [proj(OFF_MV + p * LANES).astype(bf16) for p in range(PAIRS)]
        yield

        zb = pltpu.roll(bcum, LANES - M_HEADS, axis=1)
        r_mat = jnp.where(head_col, gates - zb, 0.0)
        cm = r_mat
        sh = 1
        while sh < ROWS:
            cm = jnp.where(row >= sh, jnp.maximum(cm, pltpu.roll(cm, sh, axis=0)), cm)
            sh *= 2
        mprev = m_sc[...]
        mx = jnp.maximum(mprev, cm)
        gm = mprev - mx
        em = jnp.where(head_col, -(zb + mx), 0.0)
        mx_last = jnp.broadcast_to(mx[ROWS - 1:ROWS, :], (ROWS, LANES))
        m_sc[...] = jnp.where(head_col, jnp.broadcast_to((zb + mx)[ROWS - 1:ROWS, :], (ROWS, LANES)), 0.0)
        mx_b = _dot_nt(_split_terms(mx), selh_ref[...])
        slabs = jnp.exp(_dot_nt(_split_terms(jnp.concatenate([gm, em, r_mat - mx_last], axis=0)),
                                selp_ref[...]))
        winter_b, emt_b, ws_b = slabs[:ROWS], slabs[ROWS:2 * ROWS], slabs[2 * ROWS:]
        r_t = r_mat.T
        yield

        outs = []
        r_acc = []
        r_o = _dot((r_sc * dmat2_ref[...]).astype(bf16), r_vbd)
        for p in range(PAIRS):
            ps = slice(p * LANES, (p + 1) * LANES)
            r_acc.append(pick(r_o[:, ps]) + rslab_ref[0, p] * r_inter[:, ps])

        a_out = []
        a_p = []
        for blk in range(A_HEADS):
            s = a_s[blk * ROWS:(blk + 1) * ROWS]
            sink = sinks_ref[ATTN_HEAD_ORDER[blk]]
            m = jnp.maximum(jnp.max(jnp.maximum(s[:, :ROWS], s[:, ROWS:]), axis=1, keepdims=True), sink)
            a_p.append(jnp.exp(s - m).astype(bf16))
            a_out.append(jnp.exp(sink - m))
        a_pv = _dot(jnp.concatenate(a_p, axis=0), vv16)
        yield

        maskadd = maskadd_ref[...]
        for p in range(PAIRS):
            ps = slice(p * LANES, (p + 1) * LANES)
            cn = cn_sc[p]
            w = jnp.concatenate(
                [jnp.exp((r_t[2 * p + side:2 * p + side + 1, :] + maskadd)
                         - mx_b[:, (2 * p + side) * ROWS:(2 * p + side + 1) * ROWS]) for side in range(2)],
                axis=0) * m_qk[:, ps]
            acc = (pick(_dot(w.astype(bf16), jnp.concatenate([m_v16[p], ones16], axis=1)))
                   + jnp.concatenate([winter_b[:, ps]] * 2, axis=1) * _dot(m_q16[p], cn.astype(bf16)))
            hh = acc[:, :LANES] / jnp.maximum(jnp.abs(acc[:, LANES:]), emt_b[:, ps])
            outs.append(_sigmoid(proj(OFF_MO + p * LANES)) * hh)
            kwt16 = (m_k[p] * ws_b[:, ps]).T.astype(bf16)
            dcn = _dot(kwt16, jnp.concatenate([m_v16[p], ones16], axis=1))
            decay = winter_b[ROWS - 1:ROWS, ps]
            cn_sc[p] = jnp.concatenate([decay, decay], axis=1) * cn + jnp.where(blockdiag2, dcn, 0.0)
        outs.extend(r_acc)
        yield

        x4 = jnp.concatenate(outs, axis=0)
        xc = x4 - half_mean(x4)
        y4 = xc * lax.rsqrt(half_mean(xc * xc) + NORM_EPS)
        for i in range(2 * PAIRS):
            gain = (mgain_ref if i < PAIRS else rgain_ref)[:, (i % PAIRS) * LANES:(i % PAIRS + 1) * LANES]
            zoff = (OFF_MZ if i < PAIRS else OFF_RZ) + (i % PAIRS) * LANES
            out = y4[i * ROWS:(i + 1) * ROWS] * gain * _silu(proj(zoff))
            mix_sc[rows, i * LANES:(i + 1) * LANES] = out.astype(bf16)
        for j in range(KV_GROUP):
            acc = pick(a_pv[2 * j * ROWS:(2 * j + 2) * ROWS])
            den = acc[:, LANES:] + jnp.where(left, a_out[2 * j], a_out[2 * j + 1])
            out = (acc[:, :LANES] / den) * _silu(proj(OFF_AZ + j * LANES))
            mix_sc[rows, M_DIM + R_DIM + j * LANES:M_DIM + R_DIM + (j + 1) * LANES] = out.astype(bf16)

    def out_project(r0, r1, c0, c1):
        y_ref[r0:r1, c0:c1] = x_ref[r0:r1, c0:c1] + _dot(mix_sc[r0:r1, :], wout_ref[:, c0:c1])

    out_blocks = [(c0, min(c0 + PROJ_COL_BLOCK, D_MODEL)) for c0 in range(0, D_MODEL, PROJ_COL_BLOCK)]
    early_out = [(0, half_rows, c0, c1) for c0, c1 in out_blocks] if half_rows < chunks * ROWS else []
    final_out = [(half_rows if early_out else 0, chunks * ROWS, c0, c1) for c0, c1 in out_blocks]

    def fill_mxu(ci):
        if late_pieces:
            project(*late_pieces.pop(0))
        elif early_out and (ci - 1) * ROWS >= half_rows:
            out_project(*early_out.pop(0))

    parts = [chunk_body(ci) for ci in range(chunks)]
    for ci in range(chunks + 1):
        if ci * ROWS >= half_rows:
            while late_pieces:
                project(*late_pieces.pop(0))
        for which in SCHEDULE:
            if which == "F" and ci < chunks:
                next(parts[ci])
                fill_mxu(ci)
            if which == "B" and ci > 0:
                next(parts[ci - 1], None)
                fill_mxu(ci)
    for piece in early_out + final_out:
        out_project(*piece)
    if final:
        y = y_ref[...]
        y_ref[...] = y * lax.rsqrt(jnp.mean(y * y, axis=1, keepdims=True) + NORM_EPS) * fgain_ref[...]

    @pl.when(step == last_step)
    def _():
        for p in range(PAIRS):
            cn = cn_sc[p]
            sb = sb_sc[p]
            n_t = cn[:, LANES:].T
            for side in range(2):
                h = 2 * p + side
                blk = slice(side * HEAD_DIM, (side + 1) * HEAD_DIM)
                c_out[0, h] = cn[blk, blk]
                s_out[0, h] = sb[blk, blk]
                n_out[0, h:h + 1, :] = n_t[side * HEAD_DIM:side * HEAD_DIM + 1, blk]
        m_out[0] = m_sc[...]
        k_out[0] = kp_sc[...]
        v_out[0] = vp_sc[...]


def _lane_is_left(shape):
    return (lax.broadcasted_iota(jnp.int32, shape, 1) & (LANES - 1)) < HEAD_DIM


def _halves(x):
    left = _lane_is_left(x.shape)
    return jnp.concatenate([jnp.where(left, x, 0.0), jnp.where(left, 0.0, x)], axis=0).astype(bf16)


def _pick(x):
    return jnp.where(_lane_is_left((ROWS, x.shape[1])), x[:ROWS], x[ROWS:])


def _pair_blockdiag(blocks):
    z = jnp.zeros_like(blocks[0])
    return jnp.concatenate(
        [jnp.concatenate([blk if j == i else z for j in range(len(blocks))], axis=1)
         for i, blk in enumerate(blocks)], axis=0)


def _half_mean(x):
    left = _lane_is_left(x.shape)
    s_left = jnp.sum(jnp.where(left, x, 0.0), axis=1, keepdims=True)
    s_right = jnp.sum(jnp.where(left, 0.0, x), axis=1, keepdims=True)
    return jnp.where(left, s_left, s_right) * (1.0 / HEAD_DIM)


def _group_last(x, groups):
    n = x.shape[1]
    glen = ROWS // groups
    x3 = x.reshape(groups, glen, n)
    return jnp.broadcast_to(x3[:, glen - 1:glen, :], (groups, glen, n)).reshape(ROWS, n)


def _sample_kernel(x_ref, ngain_ref, win_ref, gbias_ref, mgain_ref, rgain_ref, sinks_ref, wout_ref,
                   fgain_ref, tril3_ref, maskadd_ref, dmat2_ref, rslab_ref, ubias_ref, cos_ref, sin_ref,
                   selh_ref, selp_ref,
                   c_in, n_in, m_in, s_in, k_in, v_in,
                   y_ref, c_out, n_out, m_out, s_out, k_out, v_out,
                   proj_sc, mix_sc, xcur_sc, xall_sc, qb_sc, sp_sc, pp_sc, ob_sc, biasc_ref, biasp_ref, *, groups,
                   proj_groups, ret_full):
    glen = ROWS // groups
    glen_log2 = glen.bit_length() - 1
    hd_log2 = HEAD_DIM.bit_length() - 1
    layer = pl.program_id(0)
    step = pl.program_id(1)
    last_layer = pl.num_programs(0) - 1
    xrows = pl.ds(pl.multiple_of(step * ROWS, ROWS), ROWS)
    sub = lax.rem(step, proj_groups)
    rows = pl.ds(pl.multiple_of(sub * ROWS, ROWS), ROWS)
    wide = pl.ds(pl.multiple_of((step - sub) * ROWS, ROWS), proj_groups * ROWS)

    @pl.when(jnp.logical_and(sub == 0, layer == 0))
    def _():
        _rms_project(x_ref, ngain_ref, win_ref, proj_sc)

    @pl.when(jnp.logical_and(sub == 0, layer != 0))
    def _():
        _rms_project(xall_sc.at[wide, :], ngain_ref, win_ref, proj_sc)

    @pl.when(layer == 0)
    def _():
        xcur_sc[...] = x_ref[rows, :]

    @pl.when(layer != 0)
    def _():
        xcur_sc[...] = xall_sc[xrows, :]

    for b in range(groups):
        k_out[b] = pltpu.roll(k_in[b], WINDOW - glen, axis=1)
        v_out[b] = pltpu.roll(v_in[b], WINDOW - glen, axis=1)

    @pl.when(jnp.logical_and(step == 0, layer == 0))
    def _():
        for h in range(A_HEADS):
            u = ubias_ref[h:h + 1, :]
            biasc_ref[h] = _skew(u[:, :2 * WINDOW], ROWS) + maskadd_ref[...]
            biasp_ref[h] = jnp.concatenate([_skew(u[:, 2 * WINDOW:], glen)] * groups, axis=0)

    lane = lax.broadcasted_iota(jnp.int32, (ROWS, LANES), 1)
    first_half = (lane & (HEAD_DIM - 1)) < (HEAD_DIM // 2)

    r_i = lax.broadcasted_iota(jnp.int32, (ROWS, groups * HEAD_DIM), 0)
    c_i = lax.broadcasted_iota(jnp.int32, (ROWS, groups * HEAD_DIM), 1)
    blk = (r_i >> glen_log2) == (c_i >> hd_log2)
    r_t = lax.broadcasted_iota(jnp.int32, (groups * HEAD_DIM, ROWS), 0)
    c_t = lax.broadcasted_iota(jnp.int32, (groups * HEAD_DIM, ROWS), 1)
    blk_t = (r_t >> hd_log2) == (c_t >> glen_log2)

    def q_times_state(qh, st):
        qt = jnp.where(blk, jnp.concatenate([qh] * groups, axis=1), 0.0)
        return _dot(qt.astype(bf16), st.astype(bf16))

    def state_increment(kt_h, vh16):
        kt = jnp.where(blk_t, jnp.concatenate([kt_h] * groups, axis=0), 0.0)
        return _dot(kt.astype(bf16), vh16)

    def proj(off, width=LANES):
        return proj_sc[rows, off:off + width]

    def head_state_rows(slab, side):
        wide = slab[:, side * HEAD_DIM:(side + 1) * HEAD_DIM].reshape(groups, glen, HEAD_DIM)[:, 0:1, :]
        rows_ = jnp.broadcast_to(wide, (groups, HEAD_DIM, HEAD_DIM)).reshape(groups * HEAD_DIM, HEAD_DIM)
        return rows_, wide.reshape(groups, HEAD_DIM)

    row = lax.broadcasted_iota(jnp.int32, (ROWS, LANES), 0)
    tau = row & (glen - 1)
    head_col = lane < M_HEADS
    left = lane < HEAD_DIM
    ones16 = jnp.ones((ROWS, LANES), bf16)
    gates = proj(OFF_G) + gbias_ref[...]
    bcum = _exact_tril_dot(tril3_ref[...], _log_sigmoid(gates))
    zb = pltpu.roll(bcum, LANES - M_HEADS, axis=1)
    r_mat = jnp.where(head_col, gates - zb, 0.0)
    cm = r_mat
    sh = 1
    while sh < glen:
        cm = jnp.where(tau >= sh, jnp.maximum(cm, pltpu.roll(cm, sh, axis=0)), cm)
        sh *= 2
    mprev = m_in[...]
    mx = jnp.maximum(mprev, cm)
    gm = mprev - mx
    em = jnp.where(head_col, -(zb + mx), 0.0)
    mx_last = _group_last(mx, groups)
    m_out[...] = jnp.where(head_col, _group_last(zb + mx, groups), 0.0)
    mx_b = _dot_nt(_split_terms(mx), selh_ref[...])
    slabs = jnp.exp(_dot_nt(_split_terms(jnp.concatenate([gm, em, r_mat - mx_last], axis=0)), selp_ref[...]))
    winter_b, emt_b, ws_b = slabs[:ROWS], slabs[ROWS:2 * ROWS], slabs[2 * ROWS:]
    decay_b = _group_last(winter_b, groups)
    r_t = r_mat.T
    maskadd = maskadd_ref[...]

    m_q = [proj(OFF_MQ + p * LANES) for p in range(PAIRS)]
    m_k = [proj(OFF_MK + p * LANES) * QK_SCALE for p in range(PAIRS)]
    m_v = [proj(OFF_MV + p * LANES) for p in range(PAIRS)]
    m_qk = _dot_nt(jnp.concatenate([_halves(q) for q in m_q], axis=1),
                   _pair_blockdiag([k.astype(bf16) for k in m_k]))
    outs = []
    for p in range(PAIRS):
        ps = slice(p * LANES, (p + 1) * LANES)
        w = jnp.concatenate(
            [jnp.exp((r_t[2 * p + side:2 * p + side + 1, :] + maskadd)
                     - mx_b[:, (2 * p + side) * ROWS:(2 * p + side + 1) * ROWS]) for side in range(2)],
            axis=0) * m_qk[:, ps]
        intra = _pick(_dot(w.astype(bf16), jnp.concatenate([m_v[p].astype(bf16), ones16], axis=1)))
        kw = m_k[p] * ws_b[:, ps]
        kwt = kw.T
        q_c, q_n = [], []
        for side in range(2):
            h = 2 * p + side
            hs = slice(side * HEAD_DIM, (side + 1) * HEAD_DIM)
            qh = m_q[p][:, hs]
            c_h = c_in[:, h].reshape(groups * HEAD_DIM, HEAD_DIM)
            n_g = n_in[h]
            n_rows = jnp.broadcast_to(n_g.reshape(groups, 1, HEAD_DIM),
                                      (groups, glen, HEAD_DIM)).reshape(ROWS, HEAD_DIM)
            q_c.append(q_times_state(qh, c_h))
            q_n.append(jnp.sum(qh * n_rows, axis=1, keepdims=True))
            dec_rows, dec_g = head_state_rows(decay_b[:, ps], side)
            c_new = dec_rows * c_h + state_increment(kwt[hs, :], m_v[p][:, hs].astype(bf16))
            c_out[:, h] = c_new.reshape(groups, HEAD_DIM, HEAD_DIM)
            n_out[h] = dec_g * n_g + jnp.sum(kw[:, hs].reshape(groups, glen, HEAD_DIM), axis=1)
        wb = winter_b[:, ps]
        num = intra[:, :LANES] + wb * jnp.concatenate(q_c, axis=1)
        nq = intra[:, LANES:] + wb * jnp.where(left, q_n[0], q_n[1])
        outs.append(_sigmoid(proj(OFF_MO + p * LANES)) * (num / jnp.maximum(jnp.abs(nq), emt_b[:, ps])))

    cos_t, sin_t = cos_ref[...], sin_ref[...]
    r_q = [_rope(proj(OFF_RQ + p * LANES), cos_t, sin_t, first_half) for p in range(PAIRS)]
    r_k = [_rope(proj(OFF_RK + p * LANES), cos_t, sin_t, first_half) * QK_SCALE for p in range(PAIRS)]
    r_v = [proj(OFF_RV + p * LANES) for p in range(PAIRS)]
    r_sc = _dot_nt(jnp.concatenate([_halves(q) for q in r_q], axis=1),
                   _pair_blockdiag([k.astype(bf16) for k in r_k]))
    r_o = _dot((r_sc * dmat2_ref[...]).astype(bf16), _pair_blockdiag([v.astype(bf16) for v in r_v]))
    for p in range(PAIRS):
        ps = slice(p * LANES, (p + 1) * LANES)
        rkt = (r_k[p] * rslab_ref[1, p]).T
        q_s = []
        for side in range(2):
            h = 2 * p + side
            hs = slice(side * HEAD_DIM, (side + 1) * HEAD_DIM)
            s_h = s_in[:, h].reshape(groups * HEAD_DIM, HEAD_DIM)
            q_s.append(q_times_state(r_q[p][:, hs], s_h))
            s_new = ret_full[h] * s_h + state_increment(rkt[hs, :], r_v[p][:, hs].astype(bf16))
            s_out[:, h] = s_new.reshape(groups, HEAD_DIM, HEAD_DIM)
        outs.append(_pick(r_o[:, ps]) + rslab_ref[0, p] * jnp.concatenate(q_s, axis=1))

    x4 = jnp.concatenate(outs, axis=0)
    xc = x4 - _half_mean(x4)
    y4 = xc * lax.rsqrt(_half_mean(xc * xc) + NORM_EPS)
    for i in range(2 * PAIRS):
        gain = (mgain_ref if i < PAIRS else rgain_ref)[:, (i % PAIRS) * LANES:(i % PAIRS + 1) * LANES]
        zoff = (OFF_MZ if i < PAIRS else OFF_RZ) + (i % PAIRS) * LANES
        mix_sc[:, i * LANES:(i + 1) * LANES] = (y4[i * ROWS:(i + 1) * ROWS] * gain * _silu(proj(zoff))).astype(bf16)

    kcur = proj_sc[rows, OFF_AK:OFF_AK + A_KV_DIM]
    vcur = proj_sc[rows, OFF_AV:OFF_AV + A_KV_DIM]
    kcur16, vcur16 = kcur.astype(bf16), vcur.astype(bf16)

    a_q = {}
    for j in range(KV_GROUP):
        hq = _halves(proj(OFF_AQ + j * LANES) * QK_SCALE)
        a_q[j], a_q[KV_GROUP + j] = hq[:ROWS], hq[ROWS:]
    for h in range(A_HEADS):
        qb_sc[:, h * glen:(h + 1) * glen, :] = a_q[h].reshape(groups, glen, LANES)
    sc_all = _dot_nt(jnp.concatenate([a_q[h] for h in range(A_HEADS)], axis=0), kcur16)

    for b in range(groups):
        sp = _dot(qb_sc[b], k_in[b].astype(bf16))
        sp_sc[:, b * glen:(b + 1) * glen, :] = sp.reshape(A_HEADS, glen, WINDOW)

    vcur1 = jnp.concatenate([vcur16, ones16], axis=1)
    esink, o_cur = [], []
    for h in range(A_HEADS):
        sc = sc_all[h * ROWS:(h + 1) * ROWS] + biasc_ref[h]
        sp = sp_sc[h] + biasp_ref[h]
        sink = sinks_ref[layer, h]
        m = jnp.maximum(jnp.max(jnp.maximum(sc, sp), axis=1, keepdims=True), sink)
        esink.append(jnp.exp(sink - m))
        o_cur.append(_dot(jnp.exp(sc - m).astype(bf16), vcur1))
        pp_sc[:, h * glen:(h + 1) * glen, :] = jnp.exp(sp - m).reshape(groups, glen, WINDOW).astype(bf16)

    for b in range(groups):
        v1 = jnp.concatenate([v_in[b].astype(bf16), ones16], axis=0)
        ob = _dot_nt(pp_sc[b], v1)
        ob_sc[:, b * glen:(b + 1) * glen, :] = ob.reshape(A_HEADS, glen, 2 * LANES)
    norm = []
    for h in range(A_HEADS):
        acc = o_cur[h] + ob_sc[h]
        norm.append(acc[:, :LANES] / (acc[:, LANES:] + esink[h]))
    outs = [jnp.where(left, norm[j], norm[KV_GROUP + j]) for j in range(KV_GROUP)]
    kcur_t, vcur_t = kcur.T, vcur.T
    new = slice(WINDOW - glen, WINDOW)
    for b in range(groups):
        shift = (WINDOW - glen - b * glen) % LANES
        k_out[b, :, new] = pltpu.roll(kcur_t, shift, axis=1)[:, new]
        v_out[b, :, new] = pltpu.roll(vcur_t, shift, axis=1)[:, new]
    out_a = jnp.concatenate(outs, axis=1) * _silu(proj_sc[rows, OFF_AZ:OFF_AZ + A_DIM])
    mix_sc[:, M_DIM + R_DIM:M_DIM + R_DIM + A_DIM] = out_a.astype(bf16)

    y = xcur_sc[...] + _dot(mix_sc[...], wout_ref[...])

    @pl.when(layer != last_layer)
    def _():
        xall_sc[xrows, :] = y
        y_ref[...] = y

    @pl.when(layer == last_layer)
    def _():
        y_ref[...] = y * lax.rsqrt(jnp.mean(y * y, axis=1, keepdims=True) + NORM_EPS) * fgain_ref[...]


def _t5_bucket(dist):
    max_exact = N_BUCKETS // 2
    d = np.maximum(dist, 1).astype(np.float32)
    large = max_exact + (np.log(d / max_exact) / np.log(REL_MAX_DIST / max_exact)
                         * (N_BUCKETS - max_exact)).astype(np.int32)
    large = np.minimum(large, N_BUCKETS - 1)
    return np.where(dist < max_exact, dist, large).astype(np.int32)


def _static_tables(groups):
    glen = ROWS // groups
    r = np.arange(ROWS)
    grp, tau = r // glen, r % glen
    causal = (grp[:, None] == grp[None, :]) & (tau[None, :] <= tau[:, None])
    tril = causal.astype(np.float32)
    maskadd = np.where(causal, 0.0, -np.inf).astype(np.float32)
    log_g = np.log1p(-np.exp2(-5.0 - np.arange(R_HEADS, dtype=np.float64)))
    diff = (tau[:, None] - tau[None, :]).astype(np.float64)
    dmat = np.where(causal[None], np.exp(log_g[:, None, None] * np.maximum(diff, 0.0)[None]), 0.0)
    inter = np.exp(log_g[None, :] * (tau[:, None] + 1.0))
    tail = np.exp(log_g[None, :] * (glen - 1.0 - tau[:, None]))
    full = np.exp(log_g * glen)
    lane_head = np.arange(LANES) // HEAD_DIM
    rslab = np.zeros((3, PAIRS, ROWS, LANES), np.float64)
    for p in range(PAIRS):
        rslab[0, p] = inter[:, 2 * p + lane_head]
        rslab[1, p] = tail[:, 2 * p + lane_head]
        rslab[2, p] = full[2 * p + lane_head][None, :]
    selh = np.zeros((M_HEADS * ROWS, SPLIT_TERMS * LANES), np.float32)
    selp = np.zeros((PAIRS * LANES, SPLIT_TERMS * LANES), np.float32)
    for t in range(SPLIT_TERMS):
        for h in range(M_HEADS):
            selh[h * ROWS:(h + 1) * ROWS, t * LANES + h] = 1.0
        for p in range(PAIRS):
            for side in range(2):
                selp[p * LANES + side * HEAD_DIM:p * LANES + (side + 1) * HEAD_DIM, t * LANES + 2 * p + side] = 1.0
    return dict(tril3=jnp.asarray(np.concatenate([tril] * SPLIT_TERMS, axis=1), bf16),
                maskadd=maskadd,
                dmat2=np.concatenate(list(dmat.astype(np.float32).reshape(PAIRS, 2 * ROWS, ROWS)), axis=1),
                rslab=rslab.astype(np.float32),
                full=tuple(float(v) for v in full),
                selh=jnp.asarray(selh, bf16), selp=jnp.asarray(selp, bf16))


def _bias_vectors(rel_table):
    tb = jnp.transpose(rel_table[_t5_bucket(np.arange(WINDOW))]).astype(f32)
    ninf = jnp.full((A_HEADS, WINDOW), -jnp.inf, f32)
    rev = tb[:, :0:-1]
    return jnp.concatenate([tb[:, :1], ninf, rev, ninf[:, :1], rev, ninf], axis=1)


def _skew(u_row, rows):
    x = jnp.broadcast_to(u_row, (rows, 2 * WINDOW))
    return pltpu.roll(x, 0, 1, stride=1, stride_axis=0)[:, :WINDOW]


def _rope_tables(pos, signed=True):
    half = HEAD_DIM // 2
    inv = ROPE_BASE ** (-jnp.arange(half, dtype=f32) / half)
    ang = pos.astype(f32)[:, None] * inv[None, :]
    cos, sin = jnp.cos(ang), jnp.sin(ang)
    reps = LANES // HEAD_DIM
    cos_t = jnp.tile(jnp.concatenate([cos, cos], axis=1), (1, reps))
    sin_t = jnp.tile(jnp.concatenate([-sin if signed else sin, sin], axis=1), (1, reps))
    return cos_t, sin_t


def _const_spec(shape, nargs):
    zeros = (0,) * len(shape)
    if nargs == 1:
        return pl.BlockSpec(shape, lambda i: zeros)
    return pl.BlockSpec(shape, lambda i, j: zeros)


def _layer_spec(shape, layer, nargs):
    idx = (layer,) + (0,) * len(shape)
    if nargs == 1:
        return pl.BlockSpec((None,) + shape, lambda i: idx)
    return pl.BlockSpec((None,) + shape, lambda i, j: idx)


def _param_specs(layer, nargs):
    ls = functools.partial(_layer_spec, layer=layer, nargs=nargs)
    return [ls((1, D_MODEL)), ls((P_COLS, D_MODEL)), ls((1, LANES)), ls((1, M_DIM)), ls((1, R_DIM)),
            pl.BlockSpec(memory_space=pltpu.SMEM), ls((D_MODEL, D_MODEL)), _const_spec((1, D_MODEL), nargs)]


def _param_args(p, layer):
    return (p["norm_gain"], p["w_in"], p["gbias"], p["m_gain"], p["r_gain"], p["sinks"][layer], p["w_out"],
            p["fgain"])


def _prompt_layer(x, p, layer, tabs, ubias, rope, final):
    B, T, _ = x.shape
    tb = min(PROMPT_ROWS, T)
    chunks = tb // ROWS
    nt = T // tb
    cs = functools.partial(_const_spec, nargs=2)
    in_specs = [pl.BlockSpec((None, tb, D_MODEL), lambda b, t: (b, t, 0))] + _param_specs(layer, 2) + [
        cs((ROWS, SPLIT_TERMS * ROWS)), cs((ROWS, ROWS)), cs((2 * ROWS, PAIRS * ROWS)),
        cs((3, PAIRS, ROWS, LANES)), cs((A_HEADS, 4 * WINDOW)),
        cs((T // ROWS, LANES)), cs((T // ROWS, LANES)), cs((ROWS, LANES)), cs((ROWS, LANES)),
        cs((M_HEADS * ROWS, SPLIT_TERMS * LANES)), cs((PAIRS * LANES, SPLIT_TERMS * LANES)),
    ]
    out_shape = (
        jax.ShapeDtypeStruct((B, T, D_MODEL), f32),
        jax.ShapeDtypeStruct((B, M_HEADS, HEAD_DIM, HEAD_DIM), f32),
        jax.ShapeDtypeStruct((B, M_HEADS, HEAD_DIM), f32),
        jax.ShapeDtypeStruct((B, ROWS, LANES), f32),
        jax.ShapeDtypeStruct((B, R_HEADS, HEAD_DIM, HEAD_DIM), f32),
        jax.ShapeDtypeStruct((B, WINDOW, A_KV_DIM), f32),
        jax.ShapeDtypeStruct((B, WINDOW, A_KV_DIM), f32),
    )
    out_specs = (
        pl.BlockSpec((None, tb, D_MODEL), lambda b, t: (b, t, 0)),
        pl.BlockSpec((1, M_HEADS, HEAD_DIM, HEAD_DIM), lambda b, t: (b, 0, 0, 0)),
        pl.BlockSpec((1, M_HEADS, HEAD_DIM), lambda b, t: (b, 0, 0)),
        pl.BlockSpec((1, ROWS, LANES), lambda b, t: (b, 0, 0)),
        pl.BlockSpec((1, R_HEADS, HEAD_DIM, HEAD_DIM), lambda b, t: (b, 0, 0, 0)),
        pl.BlockSpec((1, WINDOW, A_KV_DIM), lambda b, t: (b, 0, 0)),
        pl.BlockSpec((1, WINDOW, A_KV_DIM), lambda b, t: (b, 0, 0)),
    )
    kern = functools.partial(_prompt_kernel, chunks=chunks, final=final)
    y, c, n, m, s, k, v = pl.pallas_call(
        kern, grid=(B, nt), in_specs=in_specs, out_specs=out_specs, out_shape=out_shape,
        scratch_shapes=[pltpu.VMEM((tb, P_COLS), f32), pltpu.VMEM((tb, D_MODEL), bf16),
                        pltpu.VMEM((PAIRS, ROWS, 2 * LANES), f32), pltpu.VMEM((PAIRS, ROWS, LANES), f32),
                        pltpu.VMEM((ROWS, LANES), f32), pltpu.VMEM((ROWS, A_KV_DIM), f32),
                        pltpu.VMEM((ROWS, A_KV_DIM), f32), pltpu.VMEM((A_HEADS * ROWS, 2 * WINDOW), f32)],
        compiler_params=pltpu.CompilerParams(dimension_semantics=("arbitrary", "arbitrary"),
                                             vmem_limit_bytes=VMEM_LIMIT_BYTES),
        name="prompt_layer",
    )(x, *_param_args(p, layer), tabs["tril3"], tabs["maskadd"], tabs["dmat2"], tabs["rslab"], ubias,
      *rope, tabs["selh"], tabs["selp"])
    k = k.reshape(B, WINDOW, A_KV_HEADS, HEAD_DIM)
    v = v.reshape(B, WINDOW, A_KV_HEADS, HEAD_DIM)
    return y, c, n, m[:, 0, :M_HEADS], s, k, v


def _sample_path(x, states, p, tabs, ubias, cos_t, sin_t):
    B, T, _ = x.shape
    groups = ROWS // T
    nb = B // groups
    pg = math.gcd(nb, DECODE_PROJ_GROUPS)
    c0, n0, m0, s0, k0, v0 = states
    depth = c0.shape[0]
    x2 = x.reshape(B * T, D_MODEL)
    n0t = jnp.transpose(n0, (0, 2, 1, 3))
    m0r = jnp.pad(jnp.repeat(m0, T, axis=1), ((0, 0), (0, 0), (0, LANES - M_HEADS)))
    k0r = jnp.transpose(k0, (0, 1, 3, 4, 2)).reshape(depth, B, A_KV_DIM, WINDOW)
    v0r = jnp.transpose(v0, (0, 1, 3, 4, 2)).reshape(depth, B, A_KV_DIM, WINDOW)

    def cs(shape):
        zeros = (0,) * len(shape)
        return pl.BlockSpec(shape, lambda l, i: zeros)

    def per_layer(shape, buffers=None):
        zeros = (0,) * len(shape)
        mode = {} if buffers is None else dict(pipeline_mode=pl.Buffered(buffers))
        return pl.BlockSpec((None,) + shape, lambda l, i: (l,) + zeros, **mode)

    st4 = pl.BlockSpec((None, groups, M_HEADS, HEAD_DIM, HEAD_DIM), lambda l, i: (l, i, 0, 0, 0))
    stn = pl.BlockSpec((None, M_HEADS, groups, HEAD_DIM), lambda l, i: (l, 0, i, 0))
    stm = pl.BlockSpec((None, ROWS, LANES), lambda l, i: (l, i, 0))
    stk = pl.BlockSpec((None, groups, A_KV_DIM, WINDOW), lambda l, i: (l, i, 0, 0))
    in_specs = [
        pl.BlockSpec((pg * ROWS, D_MODEL), lambda l, i: (i // pg, 0)),
        per_layer((1, D_MODEL)), per_layer((P_COLS, D_MODEL), 1), per_layer((1, LANES)), per_layer((1, M_DIM)),
        per_layer((1, R_DIM)), pl.BlockSpec(memory_space=pltpu.SMEM), per_layer((D_MODEL, D_MODEL), 1),
        cs((1, D_MODEL)),
        cs((ROWS, SPLIT_TERMS * ROWS)), cs((ROWS, ROWS)), cs((2 * ROWS, PAIRS * ROWS)),
        cs((3, PAIRS, ROWS, LANES)), cs((A_HEADS, 4 * WINDOW)),
        cs((ROWS, LANES)), cs((ROWS, LANES)),
        cs((M_HEADS * ROWS, SPLIT_TERMS * LANES)), cs((PAIRS * LANES, SPLIT_TERMS * LANES)),
        st4, stn, stm, st4, stk, stk,
    ]
    out_shape = (
        jax.ShapeDtypeStruct((B * T, D_MODEL), f32),
        jax.ShapeDtypeStruct((depth, B, M_HEADS, HEAD_DIM, HEAD_DIM), f32),
        jax.ShapeDtypeStruct((depth, M_HEADS, B, HEAD_DIM), f32),
        jax.ShapeDtypeStruct((depth, B * T, LANES), f32),
        jax.ShapeDtypeStruct((depth, B, R_HEADS, HEAD_DIM, HEAD_DIM), f32),
        jax.ShapeDtypeStruct((depth, B, A_KV_DIM, WINDOW), f32),
        jax.ShapeDtypeStruct((depth, B, A_KV_DIM, WINDOW), f32),
    )
    y_spec = pl.BlockSpec((ROWS, D_MODEL), lambda l, i: (jnp.where(l == depth - 1, i, 0), 0))
    out_specs = (y_spec, st4, stn, stm, st4, stk, stk)
    kern = functools.partial(_sample_kernel, groups=groups, proj_groups=pg, ret_full=tabs["full"])
    y, c, n, m, s, k, v = pl.pallas_call(
        kern, grid=(depth, nb), in_specs=in_specs, out_specs=out_specs, out_shape=out_shape,
        scratch_shapes=[pltpu.VMEM((pg * ROWS, P_COLS), f32), pltpu.VMEM((ROWS, D_MODEL), bf16),
                        pltpu.VMEM((ROWS, D_MODEL), f32), pltpu.VMEM((B * T, D_MODEL), f32),
                        pltpu.VMEM((groups, A_HEADS * T, A_KV_DIM), bf16),
                        pltpu.VMEM((A_HEADS, ROWS, WINDOW), f32),
                        pltpu.VMEM((groups, A_HEADS * T, WINDOW), bf16),
                        pltpu.VMEM((A_HEADS, ROWS, 2 * LANES), f32),
                        pltpu.VMEM((A_HEADS, ROWS, ROWS), f32), pltpu.VMEM((A_HEADS, ROWS, WINDOW), f32)],
        compiler_params=pltpu.CompilerParams(dimension_semantics=("arbitrary", "arbitrary"),
                                             vmem_limit_bytes=DECODE_VMEM_LIMIT_BYTES),
        name="sample_path",
    )(x2, p["norm_gain"], p["w_in"], p["gbias"], p["m_gain"], p["r_gain"], p["sinks"], p["w_out"], p["fgain"],
      tabs["tril3"], tabs["maskadd"], tabs["dmat2"], tabs["rslab"], ubias,
      cos_t, sin_t, tabs["selh"], tabs["selp"], c0, n0t, m0r, s0, k0r, v0r)
    y = y.reshape(B, T, D_MODEL)
    n = jnp.transpose(n, (0, 2, 1, 3))
    m = m.reshape(depth, B, T, LANES)[:, :, 0, :M_HEADS]
    k = jnp.transpose(k.reshape(depth, B, A_KV_HEADS, HEAD_DIM, WINDOW), (0, 1, 4, 2, 3))
    v = jnp.transpose(v.reshape(depth, B, A_KV_HEADS, HEAD_DIM, WINDOW), (0, 1, 4, 2, 3))
    return y, c, n, m, s, k, v


def _prepare_params(norm_gain, w_in, mlstm_gate_bias, mlstm_norm_gain, ret_norm_gain, attn_sinks, w_out,
                    final_norm_gain):
    depth = w_in.shape[0]
    w_t = jnp.swapaxes(w_in, 1, 2)
    split = OFF_G + N_GATES
    aq0 = split + 4 * R_DIM
    akv0 = aq0 + A_DIM
    az0 = akv0 + 2 * A_KV_DIM

    def by_head(w):
        w = w.reshape(depth, A_HEADS, HEAD_DIM, D_MODEL)
        return jnp.concatenate([w[:, h] for h in ATTN_HEAD_ORDER], axis=1)

    w_in_p = jnp.concatenate(
        [w_t[:, :split], jnp.zeros((depth, GATE_PAD - N_GATES, D_MODEL), w_t.dtype), w_t[:, split:aq0],
         by_head(w_t[:, aq0:akv0]), w_t[:, akv0:az0], by_head(w_t[:, az0:])], axis=1).astype(bf16)
    wo16 = w_out.astype(bf16)
    a0 = M_DIM + R_DIM
    w_out_p = jnp.concatenate([wo16[:, :a0, :], by_head(wo16[:, a0:, :])], axis=1)
    gbias = jnp.pad(mlstm_gate_bias.reshape(depth, 1, N_GATES), ((0, 0), (0, 0), (0, LANES - N_GATES)))
    return dict(norm_gain=norm_gain.reshape(depth, 1, D_MODEL), w_in=w_in_p, gbias=gbias,
                m_gain=mlstm_norm_gain.reshape(depth, 1, M_DIM), r_gain=ret_norm_gain.reshape(depth, 1, R_DIM),
                sinks=attn_sinks, w_out=w_out_p, fgain=final_norm_gain.reshape(1, D_MODEL))


def kernel(x_prompt, x_sample, state_mlstm_C, state_mlstm_n, state_mlstm_m, state_ret_S, cache_win_k,
           cache_win_v, norm_gain, w_in, mlstm_gate_bias, mlstm_norm_gain, ret_norm_gain, attn_sinks,
           rel_bias_table, w_out, final_norm_gain):
    depth = w_in.shape[0]
    seq = x_prompt.shape[1]
    dec_seq = x_sample.shape[1]
    past_len = seq
    p = _prepare_params(norm_gain, w_in, mlstm_gate_bias, mlstm_norm_gain, ret_norm_gain, attn_sinks, w_out,
                        final_norm_gain)
    tabs_p = _static_tables(1)
    tabs_s = _static_tables(ROWS // dec_seq)
    ubias = _bias_vectors(rel_bias_table)
    rope_p = (*_rope_tables(jnp.arange(0, seq, ROWS, dtype=jnp.int32), signed=False),
              *_rope_tables(jnp.arange(ROWS, dtype=jnp.int32), signed=False))
    cos_s, sin_s = _rope_tables(past_len + (jnp.arange(ROWS, dtype=jnp.int32) % dec_seq))
    states = (state_mlstm_C, state_mlstm_n, state_mlstm_m, state_ret_S, cache_win_k, cache_win_v)

    xp = x_prompt
    p_states = []
    for layer in range(depth):
        xp, *sp = _prompt_layer(xp, p, layer, tabs_p, ubias, rope_p, layer == depth - 1)
        p_states.append(sp)
    outs_p = [jnp.stack([p_states[l][i] for l in range(depth)]) for i in range(6)]
    xs, *outs_s = _sample_path(x_sample, states, p, tabs_s, ubias, cos_s, sin_s)
    return (xp, xs, *outs_p, *outs_s)
```

```python
import functools
import math

import numpy as np
import jax
import jax.numpy as jnp
from jax import lax
from jax.experimental import pallas as pl
from jax.experimental.pallas import tpu as pltpu

D_MODEL = 1024
HEAD_DIM = 64
M_HEADS = 4
R_HEADS = 4
A_HEADS = 8
A_KV_HEADS = 2
KV_GROUP = A_HEADS // A_KV_HEADS
M_DIM = M_HEADS * HEAD_DIM
R_DIM = R_HEADS * HEAD_DIM
A_DIM = A_HEADS * HEAD_DIM
A_KV_DIM = A_KV_HEADS * HEAD_DIM
WINDOW = 128
N_BUCKETS = 32
REL_MAX_DIST = 128
ROPE_BASE = 10000.0
NORM_EPS = 1e-6
QK_SCALE = HEAD_DIM ** -0.5

LANES = 128
ROWS = 128
GATE_PAD = LANES
PAIRS = M_HEADS // 2
SPLIT_TERMS = 3
SEL_TERMS = 2

OFF_MQ = 0
OFF_MK = OFF_MQ + M_DIM
OFF_MV = OFF_MK + M_DIM
OFF_MO = OFF_MV + M_DIM
OFF_MZ = OFF_MO + M_DIM
OFF_G = OFF_MZ + M_DIM
OFF_RQ = OFF_G + GATE_PAD
OFF_RK = OFF_RQ + R_DIM
OFF_RV = OFF_RK + R_DIM
OFF_RZ = OFF_RV + R_DIM
OFF_AQ = OFF_RZ + R_DIM
OFF_AK = OFF_AQ + A_DIM
OFF_AV = OFF_AK + A_KV_DIM
OFF_AZ = OFF_AV + A_KV_DIM
P_COLS = OFF_AZ + A_DIM
N_GATES = 2 * M_HEADS
PROJ_COL_BLOCK = 512
SCHEDULE = "FBFBFB"
ATTN_HEAD_ORDER = tuple(h for j in range(KV_GROUP) for h in (j, KV_GROUP + j))

PROMPT_ROWS = 512
DECODE_PROJ_GROUPS = 4
VMEM_LIMIT_BYTES = 56 * 1024 * 1024
DECODE_VMEM_LIMIT_BYTES = 58 * 1024 * 1024

f32 = jnp.float32
bf16 = jnp.bfloat16


def _dot(a, b):
    return jnp.dot(a, b, preferred_element_type=f32)


def _dot_nt(a, b):
    return lax.dot_general(a, b, (((1,), (1,)), ((), ())), preferred_element_type=f32)


def _sigmoid(x):
    return 1.0 / (1.0 + jnp.exp(-x))


def _silu(x):
    return x * _sigmoid(x)


def _log_sigmoid(x):
    return jnp.minimum(x, 0.0) - jnp.log(1.0 + jnp.exp(-jnp.abs(x)))


def _split_parts(x, terms):
    parts, r = [], x
    for i in range(terms):
        p = r.astype(bf16)
        parts.append(p)
        if i + 1 < terms:
            r = r - p.astype(f32)
    return parts


def _split_terms(x, terms=SPLIT_TERMS):
    return jnp.concatenate(_split_parts(x, terms), axis=1)


def _exact_tril_dot(tril3, x):
    return _dot(tril3, jnp.concatenate(_split_parts(x, SPLIT_TERMS), axis=0))


def _rope(x, cos_t, sin_t, first_half):
    up = pltpu.roll(x, LANES - HEAD_DIM // 2, axis=1)
    down = pltpu.roll(x, HEAD_DIM // 2, axis=1)
    return x * cos_t + jnp.where(first_half, up, down) * sin_t


def _rms_project(x_ref, ngain_ref, win_ref, proj_sc):
    xf = x_ref[...]
    u = xf * lax.rsqrt(jnp.mean(xf * xf, axis=1, keepdims=True) + NORM_EPS) * ngain_ref[...]
    u16 = u.astype(bf16)
    for c0 in range(0, P_COLS, PROJ_COL_BLOCK):
        c1 = min(c0 + PROJ_COL_BLOCK, P_COLS)
        proj_sc[:, c0:c1] = _dot_nt(u16, win_ref[c0:c1, :])


def _prompt_kernel(x_ref, ngain_ref, win_ref, gbias_ref, mgain_ref, rgain_ref, sinks_ref, wout_ref,
                   fgain_ref, tril3_ref, maskadd_ref, dmat2_ref, rslab_ref, ubias_ref, cos_ref,
                   sin_ref, cosoff_ref, sinoff_ref, selh_ref, selp_ref,
                   y_ref, c_out, n_out, m_out, s_out, k_out, v_out,
                   proj_sc, mix_sc, cn_sc, sb_sc, m_sc, kp_sc, vp_sc, bias_sc, *, chunks, final):
    step = pl.program_id(1)
    last_step = pl.num_programs(1) - 1

    @pl.when(jnp.logical_and(pl.program_id(0) == 0, step == 0))
    def _():
        for blk in range(A_HEADS):
            u = ubias_ref[ATTN_HEAD_ORDER[blk]:ATTN_HEAD_ORDER[blk] + 1, :]
            bias_sc[blk * ROWS:(blk + 1) * ROWS, :] = jnp.concatenate(
                [_skew(u[:, :2 * WINDOW], ROWS), _skew(u[:, 2 * WINDOW:], ROWS)], axis=1)

    xf = x_ref[...]
    u16 = (xf * lax.rsqrt(jnp.mean(xf * xf, axis=1, keepdims=True) + NORM_EPS) * ngain_ref[...]).astype(bf16)
    col_blocks = [(c0, min(c0 + PROJ_COL_BLOCK, P_COLS)) for c0 in range(0, P_COLS, PROJ_COL_BLOCK)]
    half_rows = (chunks // 2) * ROWS if chunks > 1 else chunks * ROWS

    def project(r0, r1, c0, c1):
        proj_sc[r0:r1, c0:c1] = _dot_nt(u16[r0:r1], win_ref[c0:c1, :])

    for c0, c1 in col_blocks:
        project(0, half_rows, c0, c1)
    late_pieces = [(half_rows, chunks * ROWS, c0, c1) for c0, c1 in col_blocks] if half_rows < chunks * ROWS else []

    @pl.when(step == 0)
    def _():
        cn_sc[...] = jnp.zeros_like(cn_sc)
        sb_sc[...] = jnp.zeros_like(sb_sc)
        m_sc[...] = jnp.zeros_like(m_sc)
        kp_sc[...] = jnp.zeros_like(kp_sc)
        vp_sc[...] = jnp.zeros_like(vp_sc)

    lane = lax.broadcasted_iota(jnp.int32, (ROWS, LANES), 1)
    row = lax.broadcasted_iota(jnp.int32, (ROWS, LANES), 0)
    left = lane < HEAD_DIM
    first_half = (lane & (HEAD_DIM - 1)) < (HEAD_DIM // 2)
    head_col = lane < M_HEADS
    blockdiag = (row < HEAD_DIM) == left
    row2 = lax.broadcasted_iota(jnp.int32, (ROWS, 2 * LANES), 0)
    lane2w = lax.broadcasted_iota(jnp.int32, (ROWS, 2 * LANES), 1)
    left2 = (lane2w & (LANES - 1)) < HEAD_DIM
    blockdiag2 = (row2 < HEAD_DIM) == left2
    ones16 = jnp.ones((ROWS, LANES), bf16)
    halves, pick, pair_blockdiag, half_mean = _halves, _pick, _pair_blockdiag, _half_mean

    def chunk_body(ci):
        rows = slice(ci * ROWS, (ci + 1) * ROWS)

        def proj(off, width=LANES):
            return proj_sc[rows, off:off + width]

        gates = proj(OFF_G) + gbias_ref[...]
        bcum = _exact_tril_dot(tril3_ref[...], _log_sigmoid(gates))

        base = pl.ds(step * chunks + ci, 1)
        cos_a, sin_a = cos_ref[base, :], sin_ref[base, :]
        cos_b, sin_b = cosoff_ref[...], sinoff_ref[...]
        cos_t = cos_a * cos_b - sin_a * sin_b
        sin_t = sin_a * cos_b + cos_a * sin_b
        sin_t = jnp.where(first_half, -sin_t, sin_t)
        r_q = [_rope(proj(OFF_RQ + p * LANES), cos_t, sin_t, first_half) for p in range(PAIRS)]
        r_k = [_rope(proj(OFF_RK + p * LANES), cos_t, sin_t, first_half) * QK_SCALE for p in range(PAIRS)]
        r_vbd = pair_blockdiag([proj(OFF_RV + p * LANES).astype(bf16) for p in range(PAIRS)])
        r_sb = [sb_sc[p] for p in range(PAIRS)]
        r_sc = _dot_nt(jnp.concatenate([halves(q) for q in r_q], axis=1),
                       pair_blockdiag([k.astype(bf16) for k in r_k]))
        r_inter = _dot(jnp.concatenate([q.astype(bf16) for q in r_q], axis=1),
                       pair_blockdiag([sb.astype(bf16) for sb in r_sb]))
        r_upd = _dot(jnp.concatenate([(r_k[p] * rslab_ref[1, p]).T.astype(bf16) for p in range(PAIRS)], axis=1),
                     r_vbd)
        for p in range(PAIRS):
            sb_sc[p] = rslab_ref[2, p] * r_sb[p] + jnp.where(blockdiag, r_upd[:, p * LANES:(p + 1) * LANES], 0.0)
        yield

        kcur, vcur = proj(OFF_AK), proj(OFF_AV)
        kprev, vprev = kp_sc[...], vp_sc[...]
        kk16 = jnp.concatenate([kcur, kprev], axis=0).astype(bf16)
        vv16 = jnp.concatenate([jnp.concatenate([vcur.astype(bf16), ones16], axis=1),
                                jnp.concatenate([vprev.astype(bf16), ones16], axis=1)], axis=0)
        kp_sc[...] = kcur
        vp_sc[...] = vcur
        a_q = jnp.concatenate([halves(proj(OFF_AQ + j * LANES) * QK_SCALE) for j in range(KV_GROUP)], axis=0)
        a_s = _dot_nt(a_q, kk16) + bias_sc[...]
        if ci == 0:
            pen = jnp.where(step == 0, -jnp.inf, 0.0).astype(f32)
            a_s = a_s + jnp.where(lax.broadcasted_iota(jnp.int32, (1, 2 * ROWS), 1) >= ROWS, pen, 0.0)

        m_q = [proj(OFF_MQ + p * LANES) for p in range(PAIRS)]
        m_k = [proj(OFF_MK + p * LANES) * QK_SCALE for p in range(PAIRS)]
        m_qk = _dot_nt(jnp.concatenate([halves(q) for q in m_q], axis=1),
                       pair_blockdiag([k.astype(bf16) for k in m_k]))
        m_q16 = [q.astype(bf16) for q in m_q]
        m_v16 = [proj(OFF_MV + p * LANES).astype(bf16) for p in range(PAIRS)]
        yield

        zb = pltpu.roll(bcum, LANES - M_HEADS, axis=1)
        r_mat = jnp.where(head_col, gates - zb, 0.0)
        cm = r_mat
        sh = 1
        while sh < ROWS:
            cm = jnp.where(row >= sh, jnp.maximum(cm, pltpu.roll(cm, sh, axis=0)), cm)
            sh *= 2
        mprev = m_sc[...]
        mx = jnp.maximum(mprev, cm)
        gm = mprev - mx
        em = jnp.where(head_col, -(zb + mx), 0.0)
        mx_last = jnp.broadcast_to(mx[ROWS - 1:ROWS, :], (ROWS, LANES))
        m_sc[...] = jnp.where(head_col, jnp.broadcast_to((zb + mx)[ROWS - 1:ROWS, :], (ROWS, LANES)), 0.0)
        mx_b = _dot_nt(_split_terms(mx, SEL_TERMS), selh_ref[...])
        slabs = jnp.exp(_dot_nt(_split_terms(jnp.concatenate([gm, em, r_mat - mx_last], axis=0), SEL_TERMS),
                                selp_ref[...]))
        winter_b, emt_b, ws_b = slabs[:ROWS], slabs[ROWS:2 * ROWS], slabs[2 * ROWS:]
        r_t = r_mat.T
        yield

        outs = []
        r_acc = []
        r_o = _dot((r_sc * dmat2_ref[...]).astype(bf16), r_vbd)
        for p in range(PAIRS):
            ps = slice(p * LANES, (p + 1) * LANES)
            r_acc.append(pick(r_o[:, ps]) + rslab_ref[0, p] * r_inter[:, ps])

        a_out = []
        a_p = []
        for blk in range(A_HEADS):
            s = a_s[blk * ROWS:(blk + 1) * ROWS]
            sink = sinks_ref[ATTN_HEAD_ORDER[blk]]
            m = jnp.maximum(jnp.max(jnp.maximum(s[:, :ROWS], s[:, ROWS:]), axis=1, keepdims=True), sink)
            a_p.append(jnp.exp(s - m).astype(bf16))
            a_out.append(jnp.exp(sink - m))
        a_pv = _dot(jnp.concatenate(a_p, axis=0), vv16)
        yield

        maskadd = maskadd_ref[...]
        for p in range(PAIRS):
            ps = slice(p * LANES, (p + 1) * LANES)
            cn = cn_sc[p]
            w = jnp.concatenate(
                [jnp.exp((r_t[2 * p + side:2 * p + side + 1, :] + maskadd)
                         - mx_b[:, (2 * p + side) * ROWS:(2 * p + side + 1) * ROWS]) for side in range(2)],
                axis=0) * m_qk[:, ps]
            acc = (pick(_dot(w.astype(bf16), jnp.concatenate([m_v16[p], ones16], axis=1)))
                   + jnp.concatenate([winter_b[:, ps]] * 2, axis=1) * _dot(m_q16[p], cn.astype(bf16)))
            hh = acc[:, :LANES] / jnp.maximum(jnp.abs(acc[:, LANES:]), emt_b[:, ps])
            outs.append(_sigmoid(proj(OFF_MO + p * LANES)) * hh)
            kwt16 = (m_k[p] * ws_b[:, ps]).T.astype(bf16)
            dcn = _dot(kwt16, jnp.concatenate([m_v16[p], ones16], axis=1))
            decay = winter_b[ROWS - 1:ROWS, ps]
            cn_sc[p] = jnp.concatenate([decay, decay], axis=1) * cn + jnp.where(blockdiag2, dcn, 0.0)
        outs.extend(r_acc)
        yield

        x4 = jnp.concatenate(outs, axis=0)
        xc = x4 - half_mean(x4)
        y4 = xc * lax.rsqrt(half_mean(xc * xc) + NORM_EPS)
        for i in range(2 * PAIRS):
            gain = (mgain_ref if i < PAIRS else rgain_ref)[:, (i % PAIRS) * LANES:(i % PAIRS + 1) * LANES]
            zoff = (OFF_MZ if i < PAIRS else OFF_RZ) + (i % PAIRS) * LANES
            out = y4[i * ROWS:(i + 1) * ROWS] * gain * _silu(proj(zoff))
            mix_sc[rows, i * LANES:(i + 1) * LANES] = out.astype(bf16)
        for j in range(KV_GROUP):
            acc = pick(a_pv[2 * j * ROWS:(2 * j + 2) * ROWS])
            den = acc[:, LANES:] + jnp.where(left, a_out[2 * j], a_out[2 * j + 1])
            out = (acc[:, :LANES] / den) * _silu(proj(OFF_AZ + j * LANES))
            mix_sc[rows, M_DIM + R_DIM + j * LANES:M_DIM + R_DIM + (j + 1) * LANES] = out.astype(bf16)

    def out_project(r0, r1, c0, c1):
        y_ref[r0:r1, c0:c1] = x_ref[r0:r1, c0:c1] + _dot(mix_sc[r0:r1, :], wout_ref[:, c0:c1])

    out_blocks = [(c0, min(c0 + PROJ_COL_BLOCK, D_MODEL)) for c0 in range(0, D_MODEL, PROJ_COL_BLOCK)]
    early_out = [(0, half_rows, c0, c1) for c0, c1 in out_blocks] if half_rows < chunks * ROWS else []
    final_out = [(half_rows if early_out else 0, chunks * ROWS, c0, c1) for c0, c1 in out_blocks]

    def fill_mxu(ci):
        if late_pieces:
            project(*late_pieces.pop(0))
        elif early_out and (ci - 1) * ROWS >= half_rows:
            out_project(*early_out.pop(0))

    parts = [chunk_body(ci) for ci in range(chunks)]
    for ci in range(chunks + 1):
        if ci * ROWS >= half_rows:
            while late_pieces:
                project(*late_pieces.pop(0))
        for which in SCHEDULE:
            if which == "F" and ci < chunks:
                next(parts[ci])
                fill_mxu(ci)
            if which == "B" and ci > 0:
                next(parts[ci - 1], None)
                fill_mxu(ci)
    for piece in early_out + final_out:
        out_project(*piece)
    if final:
        y = y_ref[...]
        y_ref[...] = y * lax.rsqrt(jnp.mean(y * y, axis=1, keepdims=True) + NORM_EPS) * fgain_ref[...]

    @pl.when(step == last_step)
    def _():
        for p in range(PAIRS):
            cn = cn_sc[p]
            sb = sb_sc[p]
            n_t = cn[:, LANES:].T
            for side in range(2):
                h = 2 * p + side
                blk = slice(side * HEAD_DIM, (side + 1) * HEAD_DIM)
                c_out[0, h] = cn[blk, blk]
                s_out[0, h] = sb[blk, blk]
                n_out[0, h:h + 1, :] = n_t[side * HEAD_DIM:side * HEAD_DIM + 1, blk]
        m_out[0] = m_sc[...]
        k_out[0] = kp_sc[...]
        v_out[0] = vp_sc[...]


def _lane_is_left(shape):
    return (lax.broadcasted_iota(jnp.int32, shape, 1) & (LANES - 1)) < HEAD_DIM


def _halves(x):
    left = _lane_is_left(x.shape)
    return jnp.concatenate([jnp.where(left, x, 0.0), jnp.where(left, 0.0, x)], axis=0).astype(bf16)


def _pick(x):
    return jnp.where(_lane_is_left((ROWS, x.shape[1])), x[:ROWS], x[ROWS:])


def _pair_blockdiag(blocks):
    z = jnp.zeros_like(blocks[0])
    return jnp.concatenate(
        [jnp.concatenate([blk if j == i else z for j in range(len(blocks))], axis=1)
         for i, blk in enumerate(blocks)], axis=0)


def _half_mean(x):
    left = _lane_is_left(x.shape)
    s_left = jnp.sum(jnp.where(left, x, 0.0), axis=1, keepdims=True)
    s_right = jnp.sum(jnp.where(left, 0.0, x), axis=1, keepdims=True)
    return jnp.where(left, s_left, s_right) * (1.0 / HEAD_DIM)


def _group_last(x, groups):
    n = x.shape[1]
    glen = ROWS // groups
    x3 = x.reshape(groups, glen, n)
    return jnp.broadcast_to(x3[:, glen - 1:glen, :], (groups, glen, n)).reshape(ROWS, n)


def _sample_kernel(x_ref, ngain_ref, win_ref, gbias_ref, mgain_ref, rgain_ref, sinks_ref, wout_ref,
                   fgain_ref, tril3_ref, maskadd_ref, dmat2_ref, rslab_ref, ubias_ref, cos_ref, sin_ref,
                   selh_ref, selp_ref,
                   c_in, n_in, m_in, s_in, k_in, v_in,
                   y_ref, c_out, n_out, m_out, s_out, k_out, v_out,
                   proj_sc, mix_sc, xcur_sc, xall_sc, qb_sc, sp_sc, pp_sc, ob_sc, biasc_ref, biasp_ref, *, groups,
                   proj_groups, ret_full):
    glen = ROWS // groups
    glen_log2 = glen.bit_length() - 1
    hd_log2 = HEAD_DIM.bit_length() - 1
    layer = pl.program_id(0)
    step = pl.program_id(1)
    last_layer = pl.num_programs(0) - 1
    xrows = pl.ds(pl.multiple_of(step * ROWS, ROWS), ROWS)
    sub = lax.rem(step, proj_groups)
    rows = pl.ds(pl.multiple_of(sub * ROWS, ROWS), ROWS)
    wide = pl.ds(pl.multiple_of((step - sub) * ROWS, ROWS), proj_groups * ROWS)

    @pl.when(jnp.logical_and(sub == 0, layer == 0))
    def _():
        _rms_project(x_ref, ngain_ref, win_ref, proj_sc)

    @pl.when(jnp.logical_and(sub == 0, layer != 0))
    def _():
        _rms_project(xall_sc.at[wide, :], ngain_ref, win_ref, proj_sc)

    @pl.when(layer == 0)
    def _():
        xcur_sc[...] = x_ref[rows, :]

    @pl.when(layer != 0)
    def _():
        xcur_sc[...] = xall_sc[xrows, :]

    for b in range(groups):
        k_out[b] = pltpu.roll(k_in[b], WINDOW - glen, axis=1)
        v_out[b] = pltpu.roll(v_in[b], WINDOW - glen, axis=1)

    @pl.when(jnp.logical_and(step == 0, layer == 0))
    def _():
        for h in range(A_HEADS):
            u = ubias_ref[h:h + 1, :]
            biasc_ref[h] = _skew(u[:, :2 * WINDOW], ROWS) + maskadd_ref[...]
            biasp_ref[h] = jnp.concatenate([_skew(u[:, 2 * WINDOW:], glen)] * groups, axis=0)

    lane = lax.broadcasted_iota(jnp.int32, (ROWS, LANES), 1)
    first_half = (lane & (HEAD_DIM - 1)) < (HEAD_DIM // 2)

    r_i = lax.broadcasted_iota(jnp.int32, (ROWS, groups * HEAD_DIM), 0)
    c_i = lax.broadcasted_iota(jnp.int32, (ROWS, groups * HEAD_DIM), 1)
    blk = (r_i >> glen_log2) == (c_i >> hd_log2)
    r_t = lax.broadcasted_iota(jnp.int32, (groups * HEAD_DIM, ROWS), 0)
    c_t = lax.broadcasted_iota(jnp.int32, (groups * HEAD_DIM, ROWS), 1)
    blk_t = (r_t >> hd_log2) == (c_t >> glen_log2)

    def q_times_state(qh, st):
        qt = jnp.where(blk, jnp.concatenate([qh] * groups, axis=1), 0.0)
        return _dot(qt.astype(bf16), st.astype(bf16))

    def state_increment(kt_h, vh16):
        kt = jnp.where(blk_t, jnp.concatenate([kt_h] * groups, axis=0), 0.0)
        return _dot(kt.astype(bf16), vh16)

    def proj(off, width=LANES):
        return proj_sc[rows, off:off + width]

    def head_state_rows(slab, side):
        wide = slab[:, side * HEAD_DIM:(side + 1) * HEAD_DIM].reshape(groups, glen, HEAD_DIM)[:, 0:1, :]
        rows_ = jnp.broadcast_to(wide, (groups, HEAD_DIM, HEAD_DIM)).reshape(groups * HEAD_DIM, HEAD_DIM)
        return rows_, wide.reshape(groups, HEAD_DIM)

    row = lax.broadcasted_iota(jnp.int32, (ROWS, LANES), 0)
    tau = row & (glen - 1)
    head_col = lane < M_HEADS
    left = lane < HEAD_DIM
    ones16 = jnp.ones((ROWS, LANES), bf16)
    gates = proj(OFF_G) + gbias_ref[...]
    bcum = _exact_tril_dot(tril3_ref[...], _log_sigmoid(gates))
    zb = pltpu.roll(bcum, LANES - M_HEADS, axis=1)
    r_mat = jnp.where(head_col, gates - zb, 0.0)
    cm = r_mat
    sh = 1
    while sh < glen:
        cm = jnp.where(tau >= sh, jnp.maximum(cm, pltpu.roll(cm, sh, axis=0)), cm)
        sh *= 2
    mprev = m_in[...]
    mx = jnp.maximum(mprev, cm)
    gm = mprev - mx
    em = jnp.where(head_col, -(zb + mx), 0.0)
    mx_last = _group_last(mx, groups)
    m_out[...] = jnp.where(head_col, _group_last(zb + mx, groups), 0.0)
    mx_b = _dot_nt(_split_terms(mx, SEL_TERMS), selh_ref[...])
    slabs = jnp.exp(_dot_nt(_split_terms(jnp.concatenate([gm, em, r_mat - mx_last], axis=0), SEL_TERMS), selp_ref[...]))
    winter_b, emt_b, ws_b = slabs[:ROWS], slabs[ROWS:2 * ROWS], slabs[2 * ROWS:]
    decay_b = _group_last(winter_b, groups)
    r_t = r_mat.T
    maskadd = maskadd_ref[...]

    m_q = [proj(OFF_MQ + p * LANES) for p in range(PAIRS)]
    m_k = [proj(OFF_MK + p * LANES) * QK_SCALE for p in range(PAIRS)]
    m_v = [proj(OFF_MV + p * LANES) for p in range(PAIRS)]
    m_qk = _dot_nt(jnp.concatenate([_halves(q) for q in m_q], axis=1),
                   _pair_blockdiag([k.astype(bf16) for k in m_k]))
    outs = []
    for p in range(PAIRS):
        ps = slice(p * LANES, (p + 1) * LANES)
        w = jnp.concatenate(
            [jnp.exp((r_t[2 * p + side:2 * p + side + 1, :] + maskadd)
                     - mx_b[:, (2 * p + side) * ROWS:(2 * p + side + 1) * ROWS]) for side in range(2)],
            axis=0) * m_qk[:, ps]
        intra = _pick(_dot(w.astype(bf16), jnp.concatenate([m_v[p].astype(bf16), ones16], axis=1)))
        kw = m_k[p] * ws_b[:, ps]
        kwt = kw.T
        q_c, q_n = [], []
        for side in range(2):
            h = 2 * p + side
            hs = slice(side * HEAD_DIM, (side + 1) * HEAD_DIM)
            qh = m_q[p][:, hs]
            c_h = c_in[:, h].reshape(groups * HEAD_DIM, HEAD_DIM)
            n_g = n_in[h]
            n_rows = jnp.broadcast_to(n_g.reshape(groups, 1, HEAD_DIM),
                                      (groups, glen, HEAD_DIM)).reshape(ROWS, HEAD_DIM)
            q_c.append(q_times_state(qh, c_h))
            q_n.append(jnp.sum(qh * n_rows, axis=1, keepdims=True))
            dec_rows, dec_g = head_state_rows(decay_b[:, ps], side)
            c_new = dec_rows * c_h + state_increment(kwt[hs, :], m_v[p][:, hs].astype(bf16))
            c_out[:, h] = c_new.reshape(groups, HEAD_DIM, HEAD_DIM)
            n_out[h] = dec_g * n_g + jnp.sum(kw[:, hs].reshape(groups, glen, HEAD_DIM), axis=1)
        wb = winter_b[:, ps]
        num = intra[:, :LANES] + wb * jnp.concatenate(q_c, axis=1)
        nq = intra[:, LANES:] + wb * jnp.where(left, q_n[0], q_n[1])
        outs.append(_sigmoid(proj(OFF_MO + p * LANES)) * (num / jnp.maximum(jnp.abs(nq), emt_b[:, ps])))

    cos_t, sin_t = cos_ref[...], sin_ref[...]
    r_q = [_rope(proj(OFF_RQ + p * LANES), cos_t, sin_t, first_half) for p in range(PAIRS)]
    r_k = [_rope(proj(OFF_RK + p * LANES), cos_t, sin_t, first_half) * QK_SCALE for p in range(PAIRS)]
    r_v = [proj(OFF_RV + p * LANES) for p in range(PAIRS)]
    r_sc = _dot_nt(jnp.concatenate([_halves(q) for q in r_q], axis=1),
                   _pair_blockdiag([k.astype(bf16) for k in r_k]))
    r_o = _dot((r_sc * dmat2_ref[...]).astype(bf16), _pair_blockdiag([v.astype(bf16) for v in r_v]))
    for p in range(PAIRS):
        ps = slice(p * LANES, (p + 1) * LANES)
        rkt = (r_k[p] * rslab_ref[1, p]).T
        q_s = []
        for side in range(2):
            h = 2 * p + side
            hs = slice(side * HEAD_DIM, (side + 1) * HEAD_DIM)
            s_h = s_in[:, h].reshape(groups * HEAD_DIM, HEAD_DIM)
            q_s.append(q_times_state(r_q[p][:, hs], s_h))
            s_new = ret_full[h] * s_h + state_increment(rkt[hs, :], r_v[p][:, hs].astype(bf16))
            s_out[:, h] = s_new.reshape(groups, HEAD_DIM, HEAD_DIM)
        outs.append(_pick(r_o[:, ps]) + rslab_ref[0, p] * jnp.concatenate(q_s, axis=1))

    x4 = jnp.concatenate(outs, axis=0)
    xc = x4 - _half_mean(x4)
    y4 = xc * lax.rsqrt(_half_mean(xc * xc) + NORM_EPS)
    for i in range(2 * PAIRS):
        gain = (mgain_ref if i < PAIRS else rgain_ref)[:, (i % PAIRS) * LANES:(i % PAIRS + 1) * LANES]
        zoff = (OFF_MZ if i < PAIRS else OFF_RZ) + (i % PAIRS) * LANES
        mix_sc[:, i * LANES:(i + 1) * LANES] = (y4[i * ROWS:(i + 1) * ROWS] * gain * _silu(proj(zoff))).astype(bf16)

    kcur = proj_sc[rows, OFF_AK:OFF_AK + A_KV_DIM]
    vcur = proj_sc[rows, OFF_AV:OFF_AV + A_KV_DIM]
    kcur16, vcur16 = kcur.astype(bf16), vcur.astype(bf16)

    a_q = {}
    for j in range(KV_GROUP):
        hq = _halves(proj(OFF_AQ + j * LANES) * QK_SCALE)
        a_q[j], a_q[KV_GROUP + j] = hq[:ROWS], hq[ROWS:]
    for h in range(A_HEADS):
        qb_sc[:, h * glen:(h + 1) * glen, :] = a_q[h].reshape(groups, glen, LANES)
    sc_all = _dot_nt(jnp.concatenate([a_q[h] for h in range(A_HEADS)], axis=0), kcur16)

    for b in range(groups):
        sp = _dot(qb_sc[b], k_in[b].astype(bf16))
        sp_sc[:, b * glen:(b + 1) * glen, :] = sp.reshape(A_HEADS, glen, WINDOW)

    vcur1 = jnp.concatenate([vcur16, ones16], axis=1)
    esink, o_cur = [], []
    for h in range(A_HEADS):
        sc = sc_all[h * ROWS:(h + 1) * ROWS] + biasc_ref[h]
        sp = sp_sc[h] + biasp_ref[h]
        sink = sinks_ref[layer, h]
        m = jnp.maximum(jnp.max(jnp.maximum(sc, sp), axis=1, keepdims=True), sink)
        esink.append(jnp.exp(sink - m))
        o_cur.append(_dot(jnp.exp(sc - m).astype(bf16), vcur1))
        pp_sc[:, h * glen:(h + 1) * glen, :] = jnp.exp(sp - m).reshape(groups, glen, WINDOW).astype(bf16)

    for b in range(groups):
        v1 = jnp.concatenate([v_in[b].astype(bf16), ones16], axis=0)
        ob = _dot_nt(pp_sc[b], v1)
        ob_sc[:, b * glen:(b + 1) * glen, :] = ob.reshape(A_HEADS, glen, 2 * LANES)
    norm = []
    for h in range(A_HEADS):
        acc = o_cur[h] + ob_sc[h]
        norm.append(acc[:, :LANES] / (acc[:, LANES:] + esink[h]))
    outs = [jnp.where(left, norm[j], norm[KV_GROUP + j]) for j in range(KV_GROUP)]
    kcur_t, vcur_t = kcur.T, vcur.T
    new = slice(WINDOW - glen, WINDOW)
    for b in range(groups):
        shift = (WINDOW - glen - b * glen) % LANES
        k_out[b, :, new] = pltpu.roll(kcur_t, shift, axis=1)[:, new]
        v_out[b, :, new] = pltpu.roll(vcur_t, shift, axis=1)[:, new]
    out_a = jnp.concatenate(outs, axis=1) * _silu(proj_sc[rows, OFF_AZ:OFF_AZ + A_DIM])
    mix_sc[:, M_DIM + R_DIM:M_DIM + R_DIM + A_DIM] = out_a.astype(bf16)

    y = xcur_sc[...] + _dot(mix_sc[...], wout_ref[...])

    @pl.when(layer != last_layer)
    def _():
        xall_sc[xrows, :] = y
        y_ref[...] = y

    @pl.when(layer == last_layer)
    def _():
        y_ref[...] = y * lax.rsqrt(jnp.mean(y * y, axis=1, keepdims=True) + NORM_EPS) * fgain_ref[...]


def _t5_bucket(dist):
    max_exact = N_BUCKETS // 2
    d = np.maximum(dist, 1).astype(np.float32)
    large = max_exact + (np.log(d / max_exact) / np.log(REL_MAX_DIST / max_exact)
                         * (N_BUCKETS - max_exact)).astype(np.int32)
    large = np.minimum(large, N_BUCKETS - 1)
    return np.where(dist < max_exact, dist, large).astype(np.int32)


def _static_tables(groups):
    glen = ROWS // groups
    r = np.arange(ROWS)
    grp, tau = r // glen, r % glen
    causal = (grp[:, None] == grp[None, :]) & (tau[None, :] <= tau[:, None])
    tril = causal.astype(np.float32)
    maskadd = np.where(causal, 0.0, -np.inf).astype(np.float32)
    log_g = np.log1p(-np.exp2(-5.0 - np.arange(R_HEADS, dtype=np.float64)))
    diff = (tau[:, None] - tau[None, :]).astype(np.float64)
    dmat = np.where(causal[None], np.exp(log_g[:, None, None] * np.maximum(diff, 0.0)[None]), 0.0)
    inter = np.exp(log_g[None, :] * (tau[:, None] + 1.0))
    tail = np.exp(log_g[None, :] * (glen - 1.0 - tau[:, None]))
    full = np.exp(log_g * glen)
    lane_head = np.arange(LANES) // HEAD_DIM
    rslab = np.zeros((3, PAIRS, ROWS, LANES), np.float64)
    for p in range(PAIRS):
        rslab[0, p] = inter[:, 2 * p + lane_head]
        rslab[1, p] = tail[:, 2 * p + lane_head]
        rslab[2, p] = full[2 * p + lane_head][None, :]
    selh = np.zeros((M_HEADS * ROWS, SEL_TERMS * LANES), np.float32)
    selp = np.zeros((PAIRS * LANES, SEL_TERMS * LANES), np.float32)
    for t in range(SEL_TERMS):
        for h in range(M_HEADS):
            selh[h * ROWS:(h + 1) * ROWS, t * LANES + h] = 1.0
        for p in range(PAIRS):
            for side in range(2):
                selp[p * LANES + side * HEAD_DIM:p * LANES + (side + 1) * HEAD_DIM, t * LANES + 2 * p + side] = 1.0
    return dict(tril3=jnp.asarray(np.concatenate([tril] * SPLIT_TERMS, axis=1), bf16),
                maskadd=maskadd,
                dmat2=np.concatenate(list(dmat.astype(np.float32).reshape(PAIRS, 2 * ROWS, ROWS)), axis=1),
                rslab=rslab.astype(np.float32),
                full=tuple(float(v) for v in full),
                selh=jnp.asarray(selh, bf16), selp=jnp.asarray(selp, bf16))


def _bias_vectors(rel_table):
    tb = jnp.transpose(rel_table[_t5_bucket(np.arange(WINDOW))]).astype(f32)
    ninf = jnp.full((A_HEADS, WINDOW), -jnp.inf, f32)
    rev = tb[:, :0:-1]
    return jnp.concatenate([tb[:, :1], ninf, rev, ninf[:, :1], rev, ninf], axis=1)


def _skew(u_row, rows):
    x = jnp.broadcast_to(u_row, (rows, 2 * WINDOW))
    return pltpu.roll(x, 0, 1, stride=1, stride_axis=0)[:, :WINDOW]


def _rope_tables(pos, signed=True):
    half = HEAD_DIM // 2
    inv = ROPE_BASE ** (-jnp.arange(half, dtype=f32) / half)
    ang = pos.astype(f32)[:, None] * inv[None, :]
    cos, sin = jnp.cos(ang), jnp.sin(ang)
    reps = LANES // HEAD_DIM
    cos_t = jnp.tile(jnp.concatenate([cos, cos], axis=1), (1, reps))
    sin_t = jnp.tile(jnp.concatenate([-sin if signed else sin, sin], axis=1), (1, reps))
    return cos_t, sin_t


def _const_spec(shape, nargs):
    zeros = (0,) * len(shape)
    if nargs == 1:
        return pl.BlockSpec(shape, lambda i: zeros)
    return pl.BlockSpec(shape, lambda i, j: zeros)


def _layer_spec(shape, layer, nargs):
    idx = (layer,) + (0,) * len(shape)
    if nargs == 1:
        return pl.BlockSpec((None,) + shape, lambda i: idx)
    return pl.BlockSpec((None,) + shape, lambda i, j: idx)


def _param_specs(layer, nargs):
    ls = functools.partial(_layer_spec, layer=layer, nargs=nargs)
    return [ls((1, D_MODEL)), ls((P_COLS, D_MODEL)), ls((1, LANES)), ls((1, M_DIM)), ls((1, R_DIM)),
            pl.BlockSpec(memory_space=pltpu.SMEM), ls((D_MODEL, D_MODEL)), _const_spec((1, D_MODEL), nargs)]


def _param_args(p, layer):
    return (p["norm_gain"], p["w_in"], p["gbias"], p["m_gain"], p["r_gain"], p["sinks"][layer], p["w_out"],
            p["fgain"])


def _prompt_layer(x, p, layer, tabs, ubias, rope, final):
    B, T, _ = x.shape
    tb = min(PROMPT_ROWS, T)
    chunks = tb // ROWS
    nt = T // tb
    cs = functools.partial(_const_spec, nargs=2)
    in_specs = [pl.BlockSpec((None, tb, D_MODEL), lambda b, t: (b, t, 0))] + _param_specs(layer, 2) + [
        cs((ROWS, SPLIT_TERMS * ROWS)), cs((ROWS, ROWS)), cs((2 * ROWS, PAIRS * ROWS)),
        cs((3, PAIRS, ROWS, LANES)), cs((A_HEADS, 4 * WINDOW)),
        cs((T // ROWS, LANES)), cs((T // ROWS, LANES)), cs((ROWS, LANES)), cs((ROWS, LANES)),
        cs((M_HEADS * ROWS, SEL_TERMS * LANES)), cs((PAIRS * LANES, SEL_TERMS * LANES)),
    ]
    out_shape = (
        jax.ShapeDtypeStruct((B, T, D_MODEL), f32),
        jax.ShapeDtypeStruct((B, M_HEADS, HEAD_DIM, HEAD_DIM), f32),
        jax.ShapeDtypeStruct((B, M_HEADS, HEAD_DIM), f32),
        jax.ShapeDtypeStruct((B, ROWS, LANES), f32),
        jax.ShapeDtypeStruct((B, R_HEADS, HEAD_DIM, HEAD_DIM), f32),
        jax.ShapeDtypeStruct((B, WINDOW, A_KV_DIM), f32),
        jax.ShapeDtypeStruct((B, WINDOW, A_KV_DIM), f32),
    )
    out_specs = (
        pl.BlockSpec((None, tb, D_MODEL), lambda b, t: (b, t, 0)),
        pl.BlockSpec((1, M_HEADS, HEAD_DIM, HEAD_DIM), lambda b, t: (b, 0, 0, 0)),
        pl.BlockSpec((1, M_HEADS, HEAD_DIM), lambda b, t: (b, 0, 0)),
        pl.BlockSpec((1, ROWS, LANES), lambda b, t: (b, 0, 0)),
        pl.BlockSpec((1, R_HEADS, HEAD_DIM, HEAD_DIM), lambda b, t: (b, 0, 0, 0)),
        pl.BlockSpec((1, WINDOW, A_KV_DIM), lambda b, t: (b, 0, 0)),
        pl.BlockSpec((1, WINDOW, A_KV_DIM), lambda b, t: (b, 0, 0)),
    )
    kern = functools.partial(_prompt_kernel, chunks=chunks, final=final)
    y, c, n, m, s, k, v = pl.pallas_call(
        kern, grid=(B, nt), in_specs=in_specs, out_specs=out_specs, out_shape=out_shape,
        scratch_shapes=[pltpu.VMEM((tb, P_COLS), f32), pltpu.VMEM((tb, D_MODEL), bf16),
                        pltpu.VMEM((PAIRS, ROWS, 2 * LANES), f32), pltpu.VMEM((PAIRS, ROWS, LANES), f32),
                        pltpu.VMEM((ROWS, LANES), f32), pltpu.VMEM((ROWS, A_KV_DIM), f32),
                        pltpu.VMEM((ROWS, A_KV_DIM), f32), pltpu.VMEM((A_HEADS * ROWS, 2 * WINDOW), f32)],
        compiler_params=pltpu.CompilerParams(dimension_semantics=("arbitrary", "arbitrary"),
                                             vmem_limit_bytes=VMEM_LIMIT_BYTES),
        name="prompt_layer",
    )(x, *_param_args(p, layer), tabs["tril3"], tabs["maskadd"], tabs["dmat2"], tabs["rslab"], ubias,
      *rope, tabs["selh"], tabs["selp"])
    k = k.reshape(B, WINDOW, A_KV_HEADS, HEAD_DIM)
    v = v.reshape(B, WINDOW, A_KV_HEADS, HEAD_DIM)
    return y, c, n, m[:, 0, :M_HEADS], s, k, v


def _sample_path(x, states, p, tabs, ubias, cos_t, sin_t):
    B, T, _ = x.shape
    groups = ROWS // T
    nb = B // groups
    pg = math.gcd(nb, DECODE_PROJ_GROUPS)
    c0, n0, m0, s0, k0, v0 = states
    depth = c0.shape[0]
    x2 = x.reshape(B * T, D_MODEL)
    n0t = jnp.transpose(n0, (0, 2, 1, 3))
    m0r = jnp.pad(jnp.repeat(m0, T, axis=1), ((0, 0), (0, 0), (0, LANES - M_HEADS)))
    k0r = jnp.transpose(k0, (0, 1, 3, 4, 2)).reshape(depth, B, A_KV_DIM, WINDOW)
    v0r = jnp.transpose(v0, (0, 1, 3, 4, 2)).reshape(depth, B, A_KV_DIM, WINDOW)

    def cs(shape):
        zeros = (0,) * len(shape)
        return pl.BlockSpec(shape, lambda l, i: zeros)

    def per_layer(shape, buffers=None):
        zeros = (0,) * len(shape)
        mode = {} if buffers is None else dict(pipeline_mode=pl.Buffered(buffers))
        return pl.BlockSpec((None,) + shape, lambda l, i: (l,) + zeros, **mode)

    st4 = pl.BlockSpec((None, groups, M_HEADS, HEAD_DIM, HEAD_DIM), lambda l, i: (l, i, 0, 0, 0))
    stn = pl.BlockSpec((None, M_HEADS, groups, HEAD_DIM), lambda l, i: (l, 0, i, 0))
    stm = pl.BlockSpec((None, ROWS, LANES), lambda l, i: (l, i, 0))
    stk = pl.BlockSpec((None, groups, A_KV_DIM, WINDOW), lambda l, i: (l, i, 0, 0))
    in_specs = [
        pl.BlockSpec((pg * ROWS, D_MODEL), lambda l, i: (i // pg, 0)),
        per_layer((1, D_MODEL)), per_layer((P_COLS, D_MODEL), 1), per_layer((1, LANES)), per_layer((1, M_DIM)),
        per_layer((1, R_DIM)), pl.BlockSpec(memory_space=pltpu.SMEM), per_layer((D_MODEL, D_MODEL), 1),
        cs((1, D_MODEL)),
        cs((ROWS, SPLIT_TERMS * ROWS)), cs((ROWS, ROWS)), cs((2 * ROWS, PAIRS * ROWS)),
        cs((3, PAIRS, ROWS, LANES)), cs((A_HEADS, 4 * WINDOW)),
        cs((ROWS, LANES)), cs((ROWS, LANES)),
        cs((M_HEADS * ROWS, SEL_TERMS * LANES)), cs((PAIRS * LANES, SEL_TERMS * LANES)),
        st4, stn, stm, st4, stk, stk,
    ]
    out_shape = (
        jax.ShapeDtypeStruct((B * T, D_MODEL), f32),
        jax.ShapeDtypeStruct((depth, B, M_HEADS, HEAD_DIM, HEAD_DIM), f32),
        jax.ShapeDtypeStruct((depth, M_HEADS, B, HEAD_DIM), f32),
        jax.ShapeDtypeStruct((depth, B * T, LANES), f32),
        jax.ShapeDtypeStruct((depth, B, R_HEADS, HEAD_DIM, HEAD_DIM), f32),
        jax.ShapeDtypeStruct((depth, B, A_KV_DIM, WINDOW), f32),
        jax.ShapeDtypeStruct((depth, B, A_KV_DIM, WINDOW), f32),
    )
    y_spec = pl.BlockSpec((ROWS, D_MODEL), lambda l, i: (jnp.where(l == depth - 1, i, 0), 0))
    out_specs = (y_spec, st4, stn, stm, st4, stk, stk)
    kern = functools.partial(_sample_kernel, groups=groups, proj_groups=pg, ret_full=tabs["full"])
    y, c, n, m, s, k, v = pl.pallas_call(
        kern, grid=(depth, nb), in_specs=in_specs, out_specs=out_specs, out_shape=out_shape,
        scratch_shapes=[pltpu.VMEM((pg * ROWS, P_COLS), f32), pltpu.VMEM((ROWS, D_MODEL), bf16),
                        pltpu.VMEM((ROWS, D_MODEL), f32), pltpu.VMEM((B * T, D_MODEL), f32),
                        pltpu.VMEM((groups, A_HEADS * T, A_KV_DIM), bf16),
                        pltpu.VMEM((A_HEADS, ROWS, WINDOW), f32),
                        pltpu.VMEM((groups, A_HEADS * T, WINDOW), bf16),
                        pltpu.VMEM((A_HEADS, ROWS, 2 * LANES), f32),
                        pltpu.VMEM((A_HEADS, ROWS, ROWS), f32), pltpu.VMEM((A_HEADS, ROWS, WINDOW), f32)],
        compiler_params=pltpu.CompilerParams(dimension_semantics=("arbitrary", "arbitrary"),
                                             vmem_limit_bytes=DECODE_VMEM_LIMIT_BYTES),
        name="sample_path",
    )(x2, p["norm_gain"], p["w_in"], p["gbias"], p["m_gain"], p["r_gain"], p["sinks"], p["w_out"], p["fgain"],
      tabs["tril3"], tabs["maskadd"], tabs["dmat2"], tabs["rslab"], ubias,
      cos_t, sin_t, tabs["selh"], tabs["selp"], c0, n0t, m0r, s0, k0r, v0r)
    y = y.reshape(B, T, D_MODEL)
    n = jnp.transpose(n, (0, 2, 1, 3))
    m = m.reshape(depth, B, T, LANES)[:, :, 0, :M_HEADS]
    k = jnp.transpose(k.reshape(depth, B, A_KV_HEADS, HEAD_DIM, WINDOW), (0, 1, 4, 2, 3))
    v = jnp.transpose(v.reshape(depth, B, A_KV_HEADS, HEAD_DIM, WINDOW), (0, 1, 4, 2, 3))
    return y, c, n, m, s, k, v


def _prepare_params(norm_gain, w_in, mlstm_gate_bias, mlstm_norm_gain, ret_norm_gain, attn_sinks, w_out,
                    final_norm_gain):
    depth = w_in.shape[0]
    w_t = jnp.swapaxes(w_in, 1, 2)
    split = OFF_G + N_GATES
    aq0 = split + 4 * R_DIM
    akv0 = aq0 + A_DIM
    az0 = akv0 + 2 * A_KV_DIM

    def by_head(w):
        w = w.reshape(depth, A_KV_HEADS, KV_GROUP, HEAD_DIM, D_MODEL)
        return jnp.transpose(w, (0, 2, 1, 3, 4)).reshape(depth, A_DIM, D_MODEL)

    w_in_p = jnp.concatenate(
        [w_t[:, :split], jnp.zeros((depth, GATE_PAD - N_GATES, D_MODEL), w_t.dtype), w_t[:, split:aq0],
         by_head(w_t[:, aq0:akv0]), w_t[:, akv0:az0], by_head(w_t[:, az0:])], axis=1).astype(bf16)
    wo16 = w_out.astype(bf16)
    a0 = M_DIM + R_DIM
    w_out_p = jnp.concatenate([wo16[:, :a0, :], by_head(wo16[:, a0:, :])], axis=1)
    gbias = jnp.pad(mlstm_gate_bias.reshape(depth, 1, N_GATES), ((0, 0), (0, 0), (0, LANES - N_GATES)))
    return dict(norm_gain=norm_gain.reshape(depth, 1, D_MODEL), w_in=w_in_p, gbias=gbias,
                m_gain=mlstm_norm_gain.reshape(depth, 1, M_DIM), r_gain=ret_norm_gain.reshape(depth, 1, R_DIM),
                sinks=attn_sinks, w_out=w_out_p, fgain=final_norm_gain.reshape(1, D_MODEL))


def kernel(x_prompt, x_sample, state_mlstm_C, state_mlstm_n, state_mlstm_m, state_ret_S, cache_win_k,
           cache_win_v, norm_gain, w_in, mlstm_gate_bias, mlstm_norm_gain, ret_norm_gain, attn_sinks,
           rel_bias_table, w_out, final_norm_gain):
    depth = w_in.shape[0]
    seq = x_prompt.shape[1]
    dec_seq = x_sample.shape[1]
    past_len = seq
    p = _prepare_params(norm_gain, w_in, mlstm_gate_bias, mlstm_norm_gain, ret_norm_gain, attn_sinks, w_out,
                        final_norm_gain)
    tabs_p = _static_tables(1)
    tabs_s = _static_tables(ROWS // dec_seq)
    ubias = _bias_vectors(rel_bias_table)
    rope_p = (*_rope_tables(jnp.arange(0, seq, ROWS, dtype=jnp.int32), signed=False),
              *_rope_tables(jnp.arange(ROWS, dtype=jnp.int32), signed=False))
    cos_s, sin_s = _rope_tables(past_len + (jnp.arange(ROWS, dtype=jnp.int32) % dec_seq))
    states = (state_mlstm_C, state_mlstm_n, state_mlstm_m, state_ret_S, cache_win_k, cache_win_v)

    xp = x_prompt
    p_states = []
    for layer in range(depth):
        xp, *sp = _prompt_layer(xp, p, layer, tabs_p, ubias, rope_p, layer == depth - 1)
        p_states.append(sp)
    outs_p = [jnp.stack([p_states[l][i] for l in range(depth)]) for i in range(6)]
    xs, *outs_s = _sample_path(x_sample, states, p, tabs_s, ubias, cos_s, sin_s)
    return (xp, xs, *outs_p, *outs_s)
```

```python
import functools
import math

import numpy as np
import jax
import jax.numpy as jnp
from jax import lax
from jax.experimental import pallas as pl
from jax.experimental.pallas import tpu as pltpu

D_MODEL = 1024
HEAD_DIM = 64
M_HEADS = 4
R_HEADS = 4
A_HEADS = 8
A_KV_HEADS = 2
KV_GROUP = A_HEADS // A_KV_HEADS
M_DIM = M_HEADS * HEAD_DIM
R_DIM = R_HEADS * HEAD_DIM
A_DIM = A_HEADS * HEAD_DIM
A_KV_DIM = A_KV_HEADS * HEAD_DIM
WINDOW = 128
N_BUCKETS = 32
REL_MAX_DIST = 128
ROPE_BASE = 10000.0
NORM_EPS = 1e-6
QK_SCALE = HEAD_DIM ** -0.5

LANES = 128
ROWS = 128
GATE_PAD = LANES
PAIRS = M_HEADS // 2
SPLIT_TERMS = 3
SEL_TERMS = 2

OFF_MQ = 0
OFF_MK = OFF_MQ + M_DIM
OFF_MV = OFF_MK + M_DIM
OFF_MO = OFF_MV + M_DIM
OFF_MZ = OFF_MO + M_DIM
OFF_G = OFF_MZ + M_DIM
OFF_RQ = OFF_G + GATE_PAD
OFF_RK = OFF_RQ + R_DIM
OFF_RV = OFF_RK + R_DIM
OFF_RZ = OFF_RV + R_DIM
OFF_AQ = OFF_RZ + R_DIM
OFF_AK = OFF_AQ + A_DIM
OFF_AV = OFF_AK + A_KV_DIM
OFF_AZ = OFF_AV + A_KV_DIM
P_COLS = OFF_AZ + A_DIM
N_GATES = 2 * M_HEADS
PROJ_COL_BLOCK = 512
SCHEDULE = "FBFBFB"
ATTN_HEAD_ORDER = tuple(h for j in range(KV_GROUP) for h in (j, KV_GROUP + j))

PROMPT_ROWS = 512
DECODE_PROJ_GROUPS = 4
VMEM_LIMIT_BYTES = 56 * 1024 * 1024
DECODE_VMEM_LIMIT_BYTES = 58 * 1024 * 1024

f32 = jnp.float32
bf16 = jnp.bfloat16


def _dot(a, b):
    return jnp.dot(a, b, preferred_element_type=f32)


def _dot_nt(a, b):
    return lax.dot_general(a, b, (((1,), (1,)), ((), ())), preferred_element_type=f32)


def _sigmoid(x):
    return 1.0 / (1.0 + jnp.exp(-x))


def _silu(x):
    return x * _sigmoid(x)


def _log_sigmoid(x):
    return jnp.minimum(x, 0.0) - jnp.log(1.0 + jnp.exp(-jnp.abs(x)))


def _split_parts(x, terms):
    parts, r = [], x
    for i in range(terms):
        p = r.astype(bf16)
        parts.append(p)
        if i + 1 < terms:
            r = r - p.astype(f32)
    return parts


def _split_terms(x, terms=SPLIT_TERMS):
    return jnp.concatenate(_split_parts(x, terms), axis=1)


def _exact_tril_dot(tril3, x):
    return _dot(tril3, jnp.concatenate(_split_parts(x, SPLIT_TERMS), axis=0))


def _rope(x, cos_t, sin_t, first_half):
    up = pltpu.roll(x, LANES - HEAD_DIM // 2, axis=1)
    down = pltpu.roll(x, HEAD_DIM // 2, axis=1)
    return x * cos_t + jnp.where(first_half, up, down) * sin_t


def _rms_project(x_ref, ngain_ref, win_ref, proj_sc):
    xf = x_ref[...]
    u = xf * lax.rsqrt(jnp.mean(xf * xf, axis=1, keepdims=True) + NORM_EPS) * ngain_ref[...]
    u16 = u.astype(bf16)
    for c0 in range(0, P_COLS, PROJ_COL_BLOCK):
        c1 = min(c0 + PROJ_COL_BLOCK, P_COLS)
        proj_sc[:, c0:c1] = _dot_nt(u16, win_ref[c0:c1, :])


def _prompt_kernel(x_ref, ngain_ref, win_ref, gbias_ref, mgain_ref, rgain_ref, sinks_ref, wout_ref,
                   fgain_ref, tril3_ref, maskadd_ref, dmat2_ref, rslab_ref, ubias_ref, cos_ref,
                   sin_ref, cosoff_ref, sinoff_ref, selh_ref, selp_ref,
                   y_ref, c_out, n_out, m_out, s_out, k_out, v_out,
                   proj_sc, mix_sc, cn_sc, sb_sc, m_sc, kp_sc, vp_sc, bias_sc, *, chunks, final):
    step = pl.program_id(1)
    last_step = pl.num_programs(1) - 1

    @pl.when(jnp.logical_and(pl.program_id(0) == 0, step == 0))
    def _():
        for blk in range(A_HEADS):
            u = ubias_ref[ATTN_HEAD_ORDER[blk]:ATTN_HEAD_ORDER[blk] + 1, :]
            bias_sc[blk * ROWS:(blk + 1) * ROWS, :] = jnp.concatenate(
                [_skew(u[:, :2 * WINDOW], ROWS), _skew(u[:, 2 * WINDOW:], ROWS)], axis=1)

    xf = x_ref[...]
    u16 = (xf * lax.rsqrt(jnp.mean(xf * xf, axis=1, keepdims=True) + NORM_EPS) * ngain_ref[...]).astype(bf16)
    col_blocks = [(c0, min(c0 + PROJ_COL_BLOCK, P_COLS)) for c0 in range(0, P_COLS, PROJ_COL_BLOCK)]
    half_rows = (chunks // 2) * ROWS if chunks > 1 else chunks * ROWS

    def project(r0, r1, c0, c1):
        proj_sc[r0:r1, c0:c1] = _dot_nt(u16[r0:r1], win_ref[c0:c1, :])

    for c0, c1 in col_blocks:
        project(0, half_rows, c0, c1)
    late_pieces = [(half_rows, chunks * ROWS, c0, c1) for c0, c1 in col_blocks] if half_rows < chunks * ROWS else []

    @pl.when(step == 0)
    def _():
        cn_sc[...] = jnp.zeros_like(cn_sc)
        sb_sc[...] = jnp.zeros_like(sb_sc)
        m_sc[...] = jnp.zeros_like(m_sc)
        kp_sc[...] = jnp.zeros_like(kp_sc)
        vp_sc[...] = jnp.zeros_like(vp_sc)

    lane = lax.broadcasted_iota(jnp.int32, (ROWS, LANES), 1)
    row = lax.broadcasted_iota(jnp.int32, (ROWS, LANES), 0)
    left = lane < HEAD_DIM
    first_half = (lane & (HEAD_DIM - 1)) < (HEAD_DIM // 2)
    head_col = lane < M_HEADS
    blockdiag = (row < HEAD_DIM) == left
    row2 = lax.broadcasted_iota(jnp.int32, (ROWS, 2 * LANES), 0)
    lane2w = lax.broadcasted_iota(jnp.int32, (ROWS, 2 * LANES), 1)
    left2 = (lane2w & (LANES - 1)) < HEAD_DIM
    blockdiag2 = (row2 < HEAD_DIM) == left2
    ones16 = jnp.ones((ROWS, LANES), bf16)
    halves, pick, pair_blockdiag, half_mean = _halves, _pick, _pair_blockdiag, _half_mean

    def chunk_body(ci):
        rows = slice(ci * ROWS, (ci + 1) * ROWS)

        def proj(off, width=LANES):
            return proj_sc[rows, off:off + width]

        gates = proj(OFF_G) + gbias_ref[...]
        bcum = _exact_tril_dot(tril3_ref[...], _log_sigmoid(gates))

        base = pl.ds(step * chunks + ci, 1)
        cos_a, sin_a = cos_ref[base, :], sin_ref[base, :]
        cos_b, sin_b = cosoff_ref[...], sinoff_ref[...]
        cos_t = cos_a * cos_b - sin_a * sin_b
        sin_t = sin_a * cos_b + cos_a * sin_b
        sin_t = jnp.where(first_half, -sin_t, sin_t)
        r_q = [_rope(proj(OFF_RQ + p * LANES), cos_t, sin_t, first_half) for p in range(PAIRS)]
        r_k = [_rope(proj(OFF_RK + p * LANES), cos_t, sin_t, first_half) * QK_SCALE for p in range(PAIRS)]
        r_vbd = pair_blockdiag([proj(OFF_RV + p * LANES).astype(bf16) for p in range(PAIRS)])
        r_sb = [sb_sc[p] for p in range(PAIRS)]
        r_sc = _dot_nt(jnp.concatenate([halves(q) for q in r_q], axis=1),
                       pair_blockdiag([k.astype(bf16) for k in r_k]))
        r_inter = _dot(jnp.concatenate([q.astype(bf16) for q in r_q], axis=1),
                       pair_blockdiag([sb.astype(bf16) for sb in r_sb]))
        r_upd = _dot(jnp.concatenate([(r_k[p] * rslab_ref[1, p]).T.astype(bf16) for p in range(PAIRS)], axis=1),
                     r_vbd)
        for p in range(PAIRS):
            sb_sc[p] = rslab_ref[2, p] * r_sb[p] + jnp.where(blockdiag, r_upd[:, p * LANES:(p + 1) * LANES], 0.0)
        yield

        kcur, vcur = proj(OFF_AK), proj(OFF_AV)
        kprev, vprev = kp_sc[...], vp_sc[...]
        kk16 = jnp.concatenate([kcur, kprev], axis=0).astype(bf16)
        vv16 = jnp.concatenate([jnp.concatenate([vcur.astype(bf16), ones16], axis=1),
                                jnp.concatenate([vprev.astype(bf16), ones16], axis=1)], axis=0)
        kp_sc[...] = kcur
        vp_sc[...] = vcur
        a_q = jnp.concatenate([halves(proj(OFF_AQ + j * LANES) * QK_SCALE) for j in range(KV_GROUP)], axis=0)
        a_s = _dot_nt(a_q, kk16) + bias_sc[...]
        if ci == 0:
            pen = jnp.where(step == 0, -jnp.inf, 0.0).astype(f32)
            a_s = a_s + jnp.where(lax.broadcasted_iota(jnp.int32, (1, 2 * ROWS), 1) >= ROWS, pen, 0.0)

        m_q = [proj(OFF_MQ + p * LANES) for p in range(PAIRS)]
        m_k = [proj(OFF_MK + p * LANES) * QK_SCALE for p in range(PAIRS)]
        m_qk = _dot_nt(jnp.concatenate([halves(q) for q in m_q], axis=1),
                       pair_blockdiag([k.astype(bf16) for k in m_k]))
        m_q16 = [q.astype(bf16) for q in m_q]
        m_v16 = [proj(OFF_MV + p * LANES).astype(bf16) for p in range(PAIRS)]
        yield

        zb = pltpu.roll(bcum, LANES - M_HEADS, axis=1)
        r_mat = jnp.where(head_col, gates - zb, 0.0)
        cm = r_mat
        sh = 1
        while sh < ROWS:
            cm = jnp.where(row >= sh, jnp.maximum(cm, pltpu.roll(cm, sh, axis=0)), cm)
            sh *= 2
        mprev = m_sc[...]
        mx = jnp.maximum(mprev, cm)
        gm = mprev - mx
        em = jnp.where(head_col, -(zb + mx), 0.0)
        mx_last = jnp.broadcast_to(mx[ROWS - 1:ROWS, :], (ROWS, LANES))
        m_sc[...] = jnp.where(head_col, jnp.broadcast_to((zb + mx)[ROWS - 1:ROWS, :], (ROWS, LANES)), 0.0)
        mx_b = _dot_nt(_split_terms(mx, SEL_TERMS), selh_ref[...])
        slabs = jnp.exp(_dot_nt(_split_terms(jnp.concatenate([gm, em, r_mat - mx_last], axis=0), SEL_TERMS),
                                selp_ref[...]))
        winter_b, emt_b, ws_b = slabs[:ROWS], slabs[ROWS:2 * ROWS], slabs[2 * ROWS:]
        r_t = r_mat.T
        yield

        outs = []
        r_acc = []
        r_o = _dot((r_sc * dmat2_ref[...]).astype(bf16), r_vbd)
        for p in range(PAIRS):
            ps = slice(p * LANES, (p + 1) * LANES)
            r_acc.append(pick(r_o[:, ps]) + rslab_ref[0, p] * r_inter[:, ps])

        a_out = []
        a_p = []
        for blk in range(A_HEADS):
            s = a_s[blk * ROWS:(blk + 1) * ROWS]
            sink = sinks_ref[ATTN_HEAD_ORDER[blk]]
            m = jnp.maximum(jnp.max(jnp.maximum(s[:, :ROWS], s[:, ROWS:]), axis=1, keepdims=True), sink)
            a_p.append(jnp.exp(s - m).astype(bf16))
            a_out.append(jnp.exp(sink - m))
        a_pv = _dot(jnp.concatenate(a_p, axis=0), vv16)
        yield

        maskadd = maskadd_ref[...]
        for p in range(PAIRS):
            ps = slice(p * LANES, (p + 1) * LANES)
            cn = cn_sc[p]
            w = jnp.concatenate(
                [jnp.exp((r_t[2 * p + side:2 * p + side + 1, :] + maskadd)
                         - mx_b[:, (2 * p + side) * ROWS:(2 * p + side + 1) * ROWS]) for side in range(2)],
                axis=0) * m_qk[:, ps]
            acc = (pick(_dot(w.astype(bf16), jnp.concatenate([m_v16[p], ones16], axis=1)))
                   + jnp.concatenate([winter_b[:, ps]] * 2, axis=1) * _dot(m_q16[p], cn.astype(bf16)))
            hh = acc[:, :LANES] / jnp.maximum(jnp.abs(acc[:, LANES:]), emt_b[:, ps])
            outs.append(_sigmoid(proj(OFF_MO + p * LANES)) * hh)
            kwt16 = (m_k[p] * ws_b[:, ps]).T.astype(bf16)
            dcn = _dot(kwt16, jnp.concatenate([m_v16[p], ones16], axis=1))
            decay = winter_b[ROWS - 1:ROWS, ps]
            cn_sc[p] = jnp.concatenate([decay, decay], axis=1) * cn + jnp.where(blockdiag2, dcn, 0.0)
        outs.extend(r_acc)
        yield

        x4 = jnp.concatenate(outs, axis=0)
        xc = x4 - half_mean(x4)
        y4 = xc * lax.rsqrt(half_mean(xc * xc) + NORM_EPS)
        for i in range(2 * PAIRS):
            gain = (mgain_ref if i < PAIRS else rgain_ref)[:, (i % PAIRS) * LANES:(i % PAIRS + 1) * LANES]
            zoff = (OFF_MZ if i < PAIRS else OFF_RZ) + (i % PAIRS) * LANES
            out = y4[i * ROWS:(i + 1) * ROWS] * gain * _silu(proj(zoff))
            mix_sc[rows, i * LANES:(i + 1) * LANES] = out.astype(bf16)
        for j in range(KV_GROUP):
            acc = pick(a_pv[2 * j * ROWS:(2 * j + 2) * ROWS])
            den = acc[:, LANES:] + jnp.where(left, a_out[2 * j], a_out[2 * j + 1])
            out = (acc[:, :LANES] / den) * _silu(proj(OFF_AZ + j * LANES))
            mix_sc[rows, M_DIM + R_DIM + j * LANES:M_DIM + R_DIM + (j + 1) * LANES] = out.astype(bf16)

    def out_project(r0, r1, c0, c1):
        y_ref[r0:r1, c0:c1] = x_ref[r0:r1, c0:c1] + _dot(mix_sc[r0:r1, :], wout_ref[:, c0:c1])

    out_blocks = [(c0, min(c0 + PROJ_COL_BLOCK, D_MODEL)) for c0 in range(0, D_MODEL, PROJ_COL_BLOCK)]
    early_out = [(0, half_rows, c0, c1) for c0, c1 in out_blocks] if half_rows < chunks * ROWS else []
    final_out = [(half_rows if early_out else 0, chunks * ROWS, c0, c1) for c0, c1 in out_blocks]

    def fill_mxu(ci):
        if late_pieces:
            project(*late_pieces.pop(0))
        elif early_out and (ci - 1) * ROWS >= half_rows:
            out_project(*early_out.pop(0))

    parts = [chunk_body(ci) for ci in range(chunks)]
    for ci in range(chunks + 1):
        if ci * ROWS >= half_rows:
            while late_pieces:
                project(*late_pieces.pop(0))
        for which in SCHEDULE:
            if which == "F" and ci < chunks:
                next(parts[ci])
                fill_mxu(ci)
            if which == "B" and ci > 0:
                next(parts[ci - 1], None)
                fill_mxu(ci)
    for piece in early_out + final_out:
        out_project(*piece)
    if final:
        y = y_ref[...]
        y_ref[...] = y * lax.rsqrt(jnp.mean(y * y, axis=1, keepdims=True) + NORM_EPS) * fgain_ref[...]

    @pl.when(step == last_step)
    def _():
        for p in range(PAIRS):
            cn = cn_sc[p]
            sb = sb_sc[p]
            n_t = cn[:, LANES:].T
            for side in range(2):
                h = 2 * p + side
                blk = slice(side * HEAD_DIM, (side + 1) * HEAD_DIM)
                c_out[0, h] = cn[blk, blk]
                s_out[0, h] = sb[blk, blk]
                n_out[0, h:h + 1, :] = n_t[side * HEAD_DIM:side * HEAD_DIM + 1, blk]
        m_out[0] = m_sc[...]
        k_out[0] = kp_sc[...]
        v_out[0] = vp_sc[...]


def _lane_is_left(shape):
    return (lax.broadcasted_iota(jnp.int32, shape, 1) & (LANES - 1)) < HEAD_DIM


def _halves(x):
    left = _lane_is_left(x.shape)
    return jnp.concatenate([jnp.where(left, x, 0.0), jnp.where(left, 0.0, x)], axis=0).astype(bf16)


def _pick(x):
    return jnp.where(_lane_is_left((ROWS, x.shape[1])), x[:ROWS], x[ROWS:])


def _pair_blockdiag(blocks):
    z = jnp.zeros_like(blocks[0])
    return jnp.concatenate(
        [jnp.concatenate([blk if j == i else z for j in range(len(blocks))], axis=1)
         for i, blk in enumerate(blocks)], axis=0)


def _half_mean(x):
    left = _lane_is_left(x.shape)
    s_left = jnp.sum(jnp.where(left, x, 0.0), axis=1, keepdims=True)
    s_right = jnp.sum(jnp.where(left, 0.0, x), axis=1, keepdims=True)
    return jnp.where(left, s_left, s_right) * (1.0 / HEAD_DIM)


def _group_last(x, groups):
    n = x.shape[1]
    glen = ROWS // groups
    x3 = x.reshape(groups, glen, n)
    return jnp.broadcast_to(x3[:, glen - 1:glen, :], (groups, glen, n)).reshape(ROWS, n)


def _sample_kernel(x_ref, ngain_ref, win_ref, gbias_ref, mgain_ref, rgain_ref, sinks_ref, wout_ref,
                   fgain_ref, tril3_ref, maskadd_ref, dmat2_ref, rslab_ref, ubias_ref, cos_ref, sin_ref,
                   selh_ref, selp_ref,
                   c_in, n_in, m_in, s_in, k_in, v_in,
                   y_ref, c_out, n_out, m_out, s_out, k_out, v_out,
                   proj_sc, mix_sc, xcur_sc, xall_sc, qb_sc, sp_sc, pp_sc, ob_sc, biasc_ref, biasp_ref, *, groups,
                   proj_groups):
    glen = ROWS // groups
    glen_log2 = glen.bit_length() - 1
    hd_log2 = HEAD_DIM.bit_length() - 1
    layer = pl.program_id(0)
    step = pl.program_id(1)
    last_layer = pl.num_programs(0) - 1
    xrows = pl.ds(pl.multiple_of(step * ROWS, ROWS), ROWS)
    sub = lax.rem(step, proj_groups)
    rows = pl.ds(pl.multiple_of(sub * ROWS, ROWS), ROWS)
    wide = pl.ds(pl.multiple_of((step - sub) * ROWS, ROWS), proj_groups * ROWS)

    @pl.when(jnp.logical_and(sub == 0, layer == 0))
    def _():
        _rms_project(x_ref, ngain_ref, win_ref, proj_sc)

    @pl.when(jnp.logical_and(sub == 0, layer != 0))
    def _():
        _rms_project(xall_sc.at[wide, :], ngain_ref, win_ref, proj_sc)

    @pl.when(layer == 0)
    def _():
        xcur_sc[...] = x_ref[rows, :]

    @pl.when(layer != 0)
    def _():
        xcur_sc[...] = xall_sc[xrows, :]

    for b in range(groups):
        k_out[b] = pltpu.roll(k_in[b], WINDOW - glen, axis=1)
        v_out[b] = pltpu.roll(v_in[b], WINDOW - glen, axis=1)

    @pl.when(jnp.logical_and(step == 0, layer == 0))
    def _():
        for h in range(A_HEADS):
            u = ubias_ref[h:h + 1, :]
            biasc_ref[h] = _skew(u[:, :2 * WINDOW], ROWS) + maskadd_ref[...]
            biasp_ref[h] = jnp.concatenate([_skew(u[:, 2 * WINDOW:], glen)] * groups, axis=0)

    lane = lax.broadcasted_iota(jnp.int32, (ROWS, LANES), 1)
    first_half = (lane & (HEAD_DIM - 1)) < (HEAD_DIM // 2)

    r_i = lax.broadcasted_iota(jnp.int32, (ROWS, groups * HEAD_DIM), 0)
    c_i = lax.broadcasted_iota(jnp.int32, (ROWS, groups * HEAD_DIM), 1)
    blk = (r_i >> glen_log2) == (c_i >> hd_log2)
    r_t = lax.broadcasted_iota(jnp.int32, (groups * HEAD_DIM, ROWS), 0)
    c_t = lax.broadcasted_iota(jnp.int32, (groups * HEAD_DIM, ROWS), 1)
    blk_t = (r_t >> hd_log2) == (c_t >> glen_log2)

    def q_times_state(qh, st):
        qt = jnp.where(blk, jnp.concatenate([qh] * groups, axis=1), 0.0)
        return _dot(qt.astype(bf16), st.astype(bf16))

    def state_increment(kt_h, vh16):
        kt = jnp.where(blk_t, jnp.concatenate([kt_h] * groups, axis=0), 0.0)
        return _dot(kt.astype(bf16), vh16)

    def proj(off, width=LANES):
        return proj_sc[rows, off:off + width]

    def pair_state_rows(slab):
        per_seq = slab.reshape(groups, glen, LANES)[:, 0:1, :]
        rows_ = jnp.broadcast_to(per_seq, (groups, HEAD_DIM, LANES)).reshape(groups * HEAD_DIM, LANES)
        return rows_, per_seq.reshape(groups, LANES)

    def half_lanes(v, side):
        keep = left if side == 0 else jnp.logical_not(left)
        return jnp.where(keep, v, 0.0).astype(bf16)

    row = lax.broadcasted_iota(jnp.int32, (ROWS, LANES), 0)
    tau = row & (glen - 1)
    head_col = lane < M_HEADS
    left = lane < HEAD_DIM
    ones16 = jnp.ones((ROWS, LANES), bf16)
    gates = proj(OFF_G) + gbias_ref[...]
    bcum = _exact_tril_dot(tril3_ref[...], _log_sigmoid(gates))
    zb = pltpu.roll(bcum, LANES - M_HEADS, axis=1)
    r_mat = jnp.where(head_col, gates - zb, 0.0)
    cm = r_mat
    sh = 1
    while sh < glen:
        cm = jnp.where(tau >= sh, jnp.maximum(cm, pltpu.roll(cm, sh, axis=0)), cm)
        sh *= 2
    mprev = m_in[...]
    mx = jnp.maximum(mprev, cm)
    gm = mprev - mx
    em = jnp.where(head_col, -(zb + mx), 0.0)
    mx_last = _group_last(mx, groups)
    m_out[...] = jnp.where(head_col, _group_last(zb + mx, groups), 0.0)
    mx_b = _dot_nt(_split_terms(mx, SEL_TERMS), selh_ref[...])
    slabs = jnp.exp(_dot_nt(_split_terms(jnp.concatenate([gm, em, r_mat - mx_last], axis=0), SEL_TERMS), selp_ref[...]))
    winter_b, emt_b, ws_b = slabs[:ROWS], slabs[ROWS:2 * ROWS], slabs[2 * ROWS:]
    decay_b = _group_last(winter_b, groups)
    r_t = r_mat.T
    maskadd = maskadd_ref[...]

    m_q = [proj(OFF_MQ + p * LANES) for p in range(PAIRS)]
    m_k = [proj(OFF_MK + p * LANES) * QK_SCALE for p in range(PAIRS)]
    m_v = [proj(OFF_MV + p * LANES) for p in range(PAIRS)]
    m_qk = _dot_nt(jnp.concatenate([_halves(q) for q in m_q], axis=1),
                   _pair_blockdiag([k.astype(bf16) for k in m_k]))
    outs = []
    for p in range(PAIRS):
        ps = slice(p * LANES, (p + 1) * LANES)
        w = jnp.concatenate(
            [jnp.exp((r_t[2 * p + side:2 * p + side + 1, :] + maskadd)
                     - mx_b[:, (2 * p + side) * ROWS:(2 * p + side + 1) * ROWS]) for side in range(2)],
            axis=0) * m_qk[:, ps]
        intra = _pick(_dot(w.astype(bf16), jnp.concatenate([m_v[p].astype(bf16), ones16], axis=1)))
        kw = m_k[p] * ws_b[:, ps]
        kwt = kw.T
        c_p = c_in[:, p].reshape(groups * HEAD_DIM, LANES)
        dec_rows, dec_seq = pair_state_rows(decay_b[:, ps])
        c_new = dec_rows * c_p
        q_c, q_n = [], []
        for side in range(2):
            h = 2 * p + side
            hs = slice(side * HEAD_DIM, (side + 1) * HEAD_DIM)
            qh = m_q[p][:, hs]
            n_g = n_in[h]
            n_rows = jnp.broadcast_to(n_g.reshape(groups, 1, HEAD_DIM),
                                      (groups, glen, HEAD_DIM)).reshape(ROWS, HEAD_DIM)
            q_c.append(q_times_state(qh, c_p))
            q_n.append(jnp.sum(qh * n_rows, axis=1, keepdims=True))
            c_new = c_new + state_increment(kwt[hs, :], half_lanes(m_v[p], side))
            n_out[h] = dec_seq[:, hs] * n_g + jnp.sum(kw[:, hs].reshape(groups, glen, HEAD_DIM), axis=1)
        c_out[:, p] = c_new.reshape(groups, HEAD_DIM, LANES)
        wb = winter_b[:, ps]
        num = intra[:, :LANES] + wb * jnp.where(left, q_c[0], q_c[1])
        nq = intra[:, LANES:] + wb * jnp.where(left, q_n[0], q_n[1])
        outs.append(_sigmoid(proj(OFF_MO + p * LANES)) * (num / jnp.maximum(jnp.abs(nq), emt_b[:, ps])))

    cos_t, sin_t = cos_ref[...], sin_ref[...]
    r_q = [_rope(proj(OFF_RQ + p * LANES), cos_t, sin_t, first_half) for p in range(PAIRS)]
    r_k = [_rope(proj(OFF_RK + p * LANES), cos_t, sin_t, first_half) * QK_SCALE for p in range(PAIRS)]
    r_v = [proj(OFF_RV + p * LANES) for p in range(PAIRS)]
    r_sc = _dot_nt(jnp.concatenate([_halves(q) for q in r_q], axis=1),
                   _pair_blockdiag([k.astype(bf16) for k in r_k]))
    r_o = _dot((r_sc * dmat2_ref[...]).astype(bf16), _pair_blockdiag([v.astype(bf16) for v in r_v]))
    for p in range(PAIRS):
        ps = slice(p * LANES, (p + 1) * LANES)
        rkt = (r_k[p] * rslab_ref[1, p]).T
        s_p = s_in[:, p].reshape(groups * HEAD_DIM, LANES)
        s_new = rslab_ref[2, p][0:1, :] * s_p
        q_s = []
        for side in range(2):
            hs = slice(side * HEAD_DIM, (side + 1) * HEAD_DIM)
            q_s.append(q_times_state(r_q[p][:, hs], s_p))
            s_new = s_new + state_increment(rkt[hs, :], half_lanes(r_v[p], side))
        s_out[:, p] = s_new.reshape(groups, HEAD_DIM, LANES)
        outs.append(_pick(r_o[:, ps]) + rslab_ref[0, p] * jnp.where(left, q_s[0], q_s[1]))

    x4 = jnp.concatenate(outs, axis=0)
    xc = x4 - _half_mean(x4)
    y4 = xc * lax.rsqrt(_half_mean(xc * xc) + NORM_EPS)
    for i in range(2 * PAIRS):
        gain = (mgain_ref if i < PAIRS else rgain_ref)[:, (i % PAIRS) * LANES:(i % PAIRS + 1) * LANES]
        zoff = (OFF_MZ if i < PAIRS else OFF_RZ) + (i % PAIRS) * LANES
        mix_sc[:, i * LANES:(i + 1) * LANES] = (y4[i * ROWS:(i + 1) * ROWS] * gain * _silu(proj(zoff))).astype(bf16)

    kcur = proj_sc[rows, OFF_AK:OFF_AK + A_KV_DIM]
    vcur = proj_sc[rows, OFF_AV:OFF_AV + A_KV_DIM]
    kcur16, vcur16 = kcur.astype(bf16), vcur.astype(bf16)

    a_q = {}
    for j in range(KV_GROUP):
        hq = _halves(proj(OFF_AQ + j * LANES) * QK_SCALE)
        a_q[j], a_q[KV_GROUP + j] = hq[:ROWS], hq[ROWS:]
    for h in range(A_HEADS):
        qb_sc[:, h * glen:(h + 1) * glen, :] = a_q[h].reshape(groups, glen, LANES)
    sc_all = _dot_nt(jnp.concatenate([a_q[h] for h in range(A_HEADS)], axis=0), kcur16)

    for b in range(groups):
        sp = _dot(qb_sc[b], k_in[b].astype(bf16))
        sp_sc[:, b * glen:(b + 1) * glen, :] = sp.reshape(A_HEADS, glen, WINDOW)

    vcur1 = jnp.concatenate([vcur16, ones16], axis=1)
    esink, o_cur = [], []
    for h in range(A_HEADS):
        sc = sc_all[h * ROWS:(h + 1) * ROWS] + biasc_ref[h]
        sp = sp_sc[h] + biasp_ref[h]
        sink = sinks_ref[layer, h]
        m = jnp.maximum(jnp.max(jnp.maximum(sc, sp), axis=1, keepdims=True), sink)
        esink.append(jnp.exp(sink - m))
        o_cur.append(_dot(jnp.exp(sc - m).astype(bf16), vcur1))
        pp_sc[:, h * glen:(h + 1) * glen, :] = jnp.exp(sp - m).reshape(groups, glen, WINDOW).astype(bf16)

    for b in range(groups):
        v1 = jnp.concatenate([v_in[b].astype(bf16), ones16], axis=0)
        ob = _dot_nt(pp_sc[b], v1)
        ob_sc[:, b * glen:(b + 1) * glen, :] = ob.reshape(A_HEADS, glen, 2 * LANES)
    norm = []
    for h in range(A_HEADS):
        acc = o_cur[h] + ob_sc[h]
        norm.append(acc[:, :LANES] / (acc[:, LANES:] + esink[h]))
    outs = [jnp.where(left, norm[j], norm[KV_GROUP + j]) for j in range(KV_GROUP)]
    kcur_t, vcur_t = kcur.T, vcur.T
    new = slice(WINDOW - glen, WINDOW)
    for b in range(groups):
        shift = (WINDOW - glen - b * glen) % LANES
        k_out[b, :, new] = pltpu.roll(kcur_t, shift, axis=1)[:, new]
        v_out[b, :, new] = pltpu.roll(vcur_t, shift, axis=1)[:, new]
    out_a = jnp.concatenate(outs, axis=1) * _silu(proj_sc[rows, OFF_AZ:OFF_AZ + A_DIM])
    mix_sc[:, M_DIM + R_DIM:M_DIM + R_DIM + A_DIM] = out_a.astype(bf16)

    y = xcur_sc[...] + _dot(mix_sc[...], wout_ref[...])

    @pl.when(layer != last_layer)
    def _():
        xall_sc[xrows, :] = y
        y_ref[...] = y

    @pl.when(layer == last_layer)
    def _():
        y_ref[...] = y * lax.rsqrt(jnp.mean(y * y, axis=1, keepdims=True) + NORM_EPS) * fgain_ref[...]


def _t5_bucket(dist):
    max_exact = N_BUCKETS // 2
    d = np.maximum(dist, 1).astype(np.float32)
    large = max_exact + (np.log(d / max_exact) / np.log(REL_MAX_DIST / max_exact)
                         * (N_BUCKETS - max_exact)).astype(np.int32)
    large = np.minimum(large, N_BUCKETS - 1)
    return np.where(dist < max_exact, dist, large).astype(np.int32)


def _static_tables(groups):
    glen = ROWS // groups
    r = np.arange(ROWS)
    grp, tau = r // glen, r % glen
    causal = (grp[:, None] == grp[None, :]) & (tau[None, :] <= tau[:, None])
    tril = causal.astype(np.float32)
    maskadd = np.where(causal, 0.0, -np.inf).astype(np.float32)
    log_g = np.log1p(-np.exp2(-5.0 - np.arange(R_HEADS, dtype=np.float64)))
    diff = (tau[:, None] - tau[None, :]).astype(np.float64)
    dmat = np.where(causal[None], np.exp(log_g[:, None, None] * np.maximum(diff, 0.0)[None]), 0.0)
    inter = np.exp(log_g[None, :] * (tau[:, None] + 1.0))
    tail = np.exp(log_g[None, :] * (glen - 1.0 - tau[:, None]))
    full = np.exp(log_g * glen)
    lane_head = np.arange(LANES) // HEAD_DIM
    rslab = np.zeros((3, PAIRS, ROWS, LANES), np.float64)
    for p in range(PAIRS):
        rslab[0, p] = inter[:, 2 * p + lane_head]
        rslab[1, p] = tail[:, 2 * p + lane_head]
        rslab[2, p] = full[2 * p + lane_head][None, :]
    selh = np.zeros((M_HEADS * ROWS, SEL_TERMS * LANES), np.float32)
    selp = np.zeros((PAIRS * LANES, SEL_TERMS * LANES), np.float32)
    for t in range(SEL_TERMS):
        for h in range(M_HEADS):
            selh[h * ROWS:(h + 1) * ROWS, t * LANES + h] = 1.0
        for p in range(PAIRS):
            for side in range(2):
                selp[p * LANES + side * HEAD_DIM:p * LANES + (side + 1) * HEAD_DIM, t * LANES + 2 * p + side] = 1.0
    return dict(tril3=jnp.asarray(np.concatenate([tril] * SPLIT_TERMS, axis=1), bf16),
                maskadd=maskadd,
                dmat2=np.concatenate(list(dmat.astype(np.float32).reshape(PAIRS, 2 * ROWS, ROWS)), axis=1),
                rslab=rslab.astype(np.float32),
                full=tuple(float(v) for v in full),
                selh=jnp.asarray(selh, bf16), selp=jnp.asarray(selp, bf16))


def _bias_vectors(rel_table):
    tb = jnp.transpose(rel_table[_t5_bucket(np.arange(WINDOW))]).astype(f32)
    ninf = jnp.full((A_HEADS, WINDOW), -jnp.inf, f32)
    rev = tb[:, :0:-1]
    return jnp.concatenate([tb[:, :1], ninf, rev, ninf[:, :1], rev, ninf], axis=1)


def _skew(u_row, rows):
    x = jnp.broadcast_to(u_row, (rows, 2 * WINDOW))
    return pltpu.roll(x, 0, 1, stride=1, stride_axis=0)[:, :WINDOW]


def _rope_tables(pos, signed=True):
    half = HEAD_DIM // 2
    inv = ROPE_BASE ** (-jnp.arange(half, dtype=f32) / half)
    ang = pos.astype(f32)[:, None] * inv[None, :]
    cos, sin = jnp.cos(ang), jnp.sin(ang)
    reps = LANES // HEAD_DIM
    cos_t = jnp.tile(jnp.concatenate([cos, cos], axis=1), (1, reps))
    sin_t = jnp.tile(jnp.concatenate([-sin if signed else sin, sin], axis=1), (1, reps))
    return cos_t, sin_t


def _const_spec(shape, nargs):
    zeros = (0,) * len(shape)
    if nargs == 1:
        return pl.BlockSpec(shape, lambda i: zeros)
    return pl.BlockSpec(shape, lambda i, j: zeros)


def _layer_spec(shape, layer, nargs):
    idx = (layer,) + (0,) * len(shape)
    if nargs == 1:
        return pl.BlockSpec((None,) + shape, lambda i: idx)
    return pl.BlockSpec((None,) + shape, lambda i, j: idx)


def _param_specs(layer, nargs):
    ls = functools.partial(_layer_spec, layer=layer, nargs=nargs)
    return [ls((1, D_MODEL)), ls((P_COLS, D_MODEL)), ls((1, LANES)), ls((1, M_DIM)), ls((1, R_DIM)),
            pl.BlockSpec(memory_space=pltpu.SMEM), ls((D_MODEL, D_MODEL)), _const_spec((1, D_MODEL), nargs)]


def _param_args(p, layer):
    return (p["norm_gain"], p["w_in"], p["gbias"], p["m_gain"], p["r_gain"], p["sinks"][layer], p["w_out"],
            p["fgain"])


def _prompt_layer(x, p, layer, tabs, ubias, rope, final):
    B, T, _ = x.shape
    tb = min(PROMPT_ROWS, T)
    chunks = tb // ROWS
    nt = T // tb
    cs = functools.partial(_const_spec, nargs=2)
    in_specs = [pl.BlockSpec((None, tb, D_MODEL), lambda b, t: (b, t, 0))] + _param_specs(layer, 2) + [
        cs((ROWS, SPLIT_TERMS * ROWS)), cs((ROWS, ROWS)), cs((2 * ROWS, PAIRS * ROWS)),
        cs((3, PAIRS, ROWS, LANES)), cs((A_HEADS, 4 * WINDOW)),
        cs((T // ROWS, LANES)), cs((T // ROWS, LANES)), cs((ROWS, LANES)), cs((ROWS, LANES)),
        cs((M_HEADS * ROWS, SEL_TERMS * LANES)), cs((PAIRS * LANES, SEL_TERMS * LANES)),
    ]
    out_shape = (
        jax.ShapeDtypeStruct((B, T, D_MODEL), f32),
        jax.ShapeDtypeStruct((B, M_HEADS, HEAD_DIM, HEAD_DIM), f32),
        jax.ShapeDtypeStruct((B, M_HEADS, HEAD_DIM), f32),
        jax.ShapeDtypeStruct((B, ROWS, LANES), f32),
        jax.ShapeDtypeStruct((B, R_HEADS, HEAD_DIM, HEAD_DIM), f32),
        jax.ShapeDtypeStruct((B, WINDOW, A_KV_DIM), f32),
        jax.ShapeDtypeStruct((B, WINDOW, A_KV_DIM), f32),
    )
    out_specs = (
        pl.BlockSpec((None, tb, D_MODEL), lambda b, t: (b, t, 0)),
        pl.BlockSpec((1, M_HEADS, HEAD_DIM, HEAD_DIM), lambda b, t: (b, 0, 0, 0)),
        pl.BlockSpec((1, M_HEADS, HEAD_DIM), lambda b, t: (b, 0, 0)),
        pl.BlockSpec((1, ROWS, LANES), lambda b, t: (b, 0, 0)),
        pl.BlockSpec((1, R_HEADS, HEAD_DIM, HEAD_DIM), lambda b, t: (b, 0, 0, 0)),
        pl.BlockSpec((1, WINDOW, A_KV_DIM), lambda b, t: (b, 0, 0)),
        pl.BlockSpec((1, WINDOW, A_KV_DIM), lambda b, t: (b, 0, 0)),
    )
    kern = functools.partial(_prompt_kernel, chunks=chunks, final=final)
    y, c, n, m, s, k, v = pl.pallas_call(
        kern, grid=(B, nt), in_specs=in_specs, out_specs=out_specs, out_shape=out_shape,
        scratch_shapes=[pltpu.VMEM((tb, P_COLS), f32), pltpu.VMEM((tb, D_MODEL), bf16),
                        pltpu.VMEM((PAIRS, ROWS, 2 * LANES), f32), pltpu.VMEM((PAIRS, ROWS, LANES), f32),
                        pltpu.VMEM((ROWS, LANES), f32), pltpu.VMEM((ROWS, A_KV_DIM), f32),
                        pltpu.VMEM((ROWS, A_KV_DIM), f32), pltpu.VMEM((A_HEADS * ROWS, 2 * WINDOW), f32)],
        compiler_params=pltpu.CompilerParams(dimension_semantics=("arbitrary", "arbitrary"),
                                             vmem_limit_bytes=VMEM_LIMIT_BYTES),
        name="prompt_layer",
    )(x, *_param_args(p, layer), tabs["tril3"], tabs["maskadd"], tabs["dmat2"], tabs["rslab"], ubias,
      *rope, tabs["selh"], tabs["selp"])
    k = k.reshape(B, WINDOW, A_KV_HEADS, HEAD_DIM)
    v = v.reshape(B, WINDOW, A_KV_HEADS, HEAD_DIM)
    return y, c, n, m[:, 0, :M_HEADS], s, k, v


def _sample_path(x, states, p, tabs, ubias, cos_t, sin_t):
    B, T, _ = x.shape
    groups = ROWS // T
    nb = B // groups
    pg = math.gcd(nb, DECODE_PROJ_GROUPS)
    c0, n0, m0, s0, k0, v0 = states
    depth = c0.shape[0]
    x2 = x.reshape(B * T, D_MODEL)
    n0t = jnp.transpose(n0, (0, 2, 1, 3))
    m0r = jnp.pad(jnp.repeat(m0, T, axis=1), ((0, 0), (0, 0), (0, LANES - M_HEADS)))
    k0r = jnp.transpose(k0, (0, 1, 3, 4, 2)).reshape(depth, B, A_KV_DIM, WINDOW)
    v0r = jnp.transpose(v0, (0, 1, 3, 4, 2)).reshape(depth, B, A_KV_DIM, WINDOW)

    def cs(shape):
        zeros = (0,) * len(shape)
        return pl.BlockSpec(shape, lambda l, i: zeros)

    def per_layer(shape, buffers=None):
        zeros = (0,) * len(shape)
        mode = {} if buffers is None else dict(pipeline_mode=pl.Buffered(buffers))
        return pl.BlockSpec((None,) + shape, lambda l, i: (l,) + zeros, **mode)

    st4 = pl.BlockSpec((None, groups, PAIRS, HEAD_DIM, LANES), lambda l, i: (l, i, 0, 0, 0))

    def to_pairs(st):
        st = st.reshape(depth, B, PAIRS, 2, HEAD_DIM, HEAD_DIM)
        return jnp.transpose(st, (0, 1, 2, 4, 3, 5)).reshape(depth, B, PAIRS, HEAD_DIM, LANES)

    def from_pairs(st):
        st = st.reshape(depth, B, PAIRS, HEAD_DIM, 2, HEAD_DIM)
        return jnp.transpose(st, (0, 1, 2, 4, 3, 5)).reshape(depth, B, M_HEADS, HEAD_DIM, HEAD_DIM)
    stn = pl.BlockSpec((None, M_HEADS, groups, HEAD_DIM), lambda l, i: (l, 0, i, 0))
    stm = pl.BlockSpec((None, ROWS, LANES), lambda l, i: (l, i, 0))
    stk = pl.BlockSpec((None, groups, A_KV_DIM, WINDOW), lambda l, i: (l, i, 0, 0))
    in_specs = [
        pl.BlockSpec((pg * ROWS, D_MODEL), lambda l, i: (i // pg, 0)),
        per_layer((1, D_MODEL)), per_layer((P_COLS, D_MODEL), 1), per_layer((1, LANES)), per_layer((1, M_DIM)),
        per_layer((1, R_DIM)), pl.BlockSpec(memory_space=pltpu.SMEM), per_layer((D_MODEL, D_MODEL), 1),
        cs((1, D_MODEL)),
        cs((ROWS, SPLIT_TERMS * ROWS)), cs((ROWS, ROWS)), cs((2 * ROWS, PAIRS * ROWS)),
        cs((3, PAIRS, ROWS, LANES)), cs((A_HEADS, 4 * WINDOW)),
        cs((ROWS, LANES)), cs((ROWS, LANES)),
        cs((M_HEADS * ROWS, SEL_TERMS * LANES)), cs((PAIRS * LANES, SEL_TERMS * LANES)),
        st4, stn, stm, st4, stk, stk,
    ]
    out_shape = (
        jax.ShapeDtypeStruct((B * T, D_MODEL), f32),
        jax.ShapeDtypeStruct((depth, B, PAIRS, HEAD_DIM, LANES), f32),
        jax.ShapeDtypeStruct((depth, M_HEADS, B, HEAD_DIM), f32),
        jax.ShapeDtypeStruct((depth, B * T, LANES), f32),
        jax.ShapeDtypeStruct((depth, B, PAIRS, HEAD_DIM, LANES), f32),
        jax.ShapeDtypeStruct((depth, B, A_KV_DIM, WINDOW), f32),
        jax.ShapeDtypeStruct((depth, B, A_KV_DIM, WINDOW), f32),
    )
    y_spec = pl.BlockSpec((ROWS, D_MODEL), lambda l, i: (jnp.where(l == depth - 1, i, 0), 0))
    out_specs = (y_spec, st4, stn, stm, st4, stk, stk)
    kern = functools.partial(_sample_kernel, groups=groups, proj_groups=pg)
    y, c, n, m, s, k, v = pl.pallas_call(
        kern, grid=(depth, nb), in_specs=in_specs, out_specs=out_specs, out_shape=out_shape,
        scratch_shapes=[pltpu.VMEM((pg * ROWS, P_COLS), f32), pltpu.VMEM((ROWS, D_MODEL), bf16),
                        pltpu.VMEM((ROWS, D_MODEL), f32), pltpu.VMEM((B * T, D_MODEL), f32),
                        pltpu.VMEM((groups, A_HEADS * T, A_KV_DIM), bf16),
                        pltpu.VMEM((A_HEADS, ROWS, WINDOW), f32),
                        pltpu.VMEM((groups, A_HEADS * T, WINDOW), bf16),
                        pltpu.VMEM((A_HEADS, ROWS, 2 * LANES), f32),
                        pltpu.VMEM((A_HEADS, ROWS, ROWS), f32), pltpu.VMEM((A_HEADS, ROWS, WINDOW), f32)],
        compiler_params=pltpu.CompilerParams(dimension_semantics=("arbitrary", "arbitrary"),
                                             vmem_limit_bytes=DECODE_VMEM_LIMIT_BYTES),
        name="sample_path",
    )(x2, p["norm_gain"], p["w_in"], p["gbias"], p["m_gain"], p["r_gain"], p["sinks"], p["w_out"], p["fgain"],
      tabs["tril3"], tabs["maskadd"], tabs["dmat2"], tabs["rslab"], ubias,
      cos_t, sin_t, tabs["selh"], tabs["selp"], to_pairs(c0), n0t, m0r, to_pairs(s0), k0r, v0r)
    y = y.reshape(B, T, D_MODEL)
    n = jnp.transpose(n, (0, 2, 1, 3))
    m = m.reshape(depth, B, T, LANES)[:, :, 0, :M_HEADS]
    k = jnp.transpose(k.reshape(depth, B, A_KV_HEADS, HEAD_DIM, WINDOW), (0, 1, 4, 2, 3))
    v = jnp.transpose(v.reshape(depth, B, A_KV_HEADS, HEAD_DIM, WINDOW), (0, 1, 4, 2, 3))
    return y, from_pairs(c), n, m, from_pairs(s), k, v


def _prepare_params(norm_gain, w_in, mlstm_gate_bias, mlstm_norm_gain, ret_norm_gain, attn_sinks, w_out,
                    final_norm_gain):
    depth = w_in.shape[0]
    w_t = jnp.swapaxes(w_in, 1, 2)
    split = OFF_G + N_GATES
    aq0 = split + 4 * R_DIM
    akv0 = aq0 + A_DIM
    az0 = akv0 + 2 * A_KV_DIM

    def by_head(w):
        w = w.reshape(depth, A_KV_HEADS, KV_GROUP, HEAD_DIM, D_MODEL)
        return jnp.transpose(w, (0, 2, 1, 3, 4)).reshape(depth, A_DIM, D_MODEL)

    w_in_p = jnp.concatenate(
        [w_t[:, :split], jnp.zeros((depth, GATE_PAD - N_GATES, D_MODEL), w_t.dtype), w_t[:, split:aq0],
         by_head(w_t[:, aq0:akv0]), w_t[:, akv0:az0], by_head(w_t[:, az0:])], axis=1).astype(bf16)
    wo16 = w_out.astype(bf16)
    a0 = M_DIM + R_DIM
    w_out_p = jnp.concatenate([wo16[:, :a0, :], by_head(wo16[:, a0:, :])], axis=1)
    gbias = jnp.pad(mlstm_gate_bias.reshape(depth, 1, N_GATES), ((0, 0), (0, 0), (0, LANES - N_GATES)))
    return dict(norm_gain=norm_gain.reshape(depth, 1, D_MODEL), w_in=w_in_p, gbias=gbias,
                m_gain=mlstm_norm_gain.reshape(depth, 1, M_DIM), r_gain=ret_norm_gain.reshape(depth, 1, R_DIM),
                sinks=attn_sinks, w_out=w_out_p, fgain=final_norm_gain.reshape(1, D_MODEL))


def kernel(x_prompt, x_sample, state_mlstm_C, state_mlstm_n, state_mlstm_m, state_ret_S, cache_win_k,
           cache_win_v, norm_gain, w_in, mlstm_gate_bias, mlstm_norm_gain, ret_norm_gain, attn_sinks,
           rel_bias_table, w_out, final_norm_gain):
    depth = w_in.shape[0]
    seq = x_prompt.shape[1]
    dec_seq = x_sample.shape[1]
    past_len = seq
    p = _prepare_params(norm_gain, w_in, mlstm_gate_bias, mlstm_norm_gain, ret_norm_gain, attn_sinks, w_out,
                        final_norm_gain)
    tabs_p = _static_tables(1)
    tabs_s = _static_tables(ROWS // dec_seq)
    ubias = _bias_vectors(rel_bias_table)
    rope_p = (*_rope_tables(jnp.arange(0, seq, ROWS, dtype=jnp.int32), signed=False),
              *_rope_tables(jnp.arange(ROWS, dtype=jnp.int32), signed=False))
    cos_s, sin_s = _rope_tables(past_len + (jnp.arange(ROWS, dtype=jnp.int32) % dec_seq))
    states = (state_mlstm_C, state_mlstm_n, state_mlstm_m, state_ret_S, cache_win_k, cache_win_v)

    xp = x_prompt
    p_states = []
    for layer in range(depth):
        xp, *sp = _prompt_layer(xp, p, layer, tabs_p, ubias, rope_p, layer == depth - 1)
        p_states.append(sp)
    outs_p = [jnp.stack([p_states[l][i] for l in range(depth)]) for i in range(6)]
    xs, *outs_s = _sample_path(x_sample, states, p, tabs_s, ubias, cos_s, sin_s)
    return (xp, xs, *outs_p, *outs_s)
```

```python
import functools
import math

import numpy as np
import jax
import jax.numpy as jnp
from jax import lax
from jax.experimental import pallas as pl
from jax.experimental.pallas import tpu as pltpu

D_MODEL = 1024
HEAD_DIM = 64
M_HEADS = 4
R_HEADS = 4
A_HEADS = 8
A_KV_HEADS = 2
KV_GROUP = A_HEADS // A_KV_HEADS
M_DIM = M_HEADS * HEAD_DIM
R_DIM = R_HEADS * HEAD_DIM
A_DIM = A_HEADS * HEAD_DIM
A_KV_DIM = A_KV_HEADS * HEAD_DIM
WINDOW = 128
N_BUCKETS = 32
REL_MAX_DIST = 128
ROPE_BASE = 10000.0
NORM_EPS = 1e-6
QK_SCALE = HEAD_DIM ** -0.5

LANES = 128
ROWS = 128
GATE_PAD = LANES
PAIRS = M_HEADS // 2
SPLIT_TERMS = 2
SEL_TERMS = 2

OFF_MQ = 0
OFF_MK = OFF_MQ + M_DIM
OFF_MV = OFF_MK + M_DIM
OFF_MO = OFF_MV + M_DIM
OFF_MZ = OFF_MO + M_DIM
OFF_G = OFF_MZ + M_DIM
OFF_RQ = OFF_G + GATE_PAD
OFF_RK = OFF_RQ + R_DIM
OFF_RV = OFF_RK + R_DIM
OFF_RZ = OFF_RV + R_DIM
OFF_AQ = OFF_RZ + R_DIM
OFF_AK = OFF_AQ + A_DIM
OFF_AV = OFF_AK + A_KV_DIM
OFF_AZ = OFF_AV + A_KV_DIM
P_COLS = OFF_AZ + A_DIM
N_GATES = 2 * M_HEADS
PROJ_COL_BLOCK = 512
SCHEDULE = "FBFBFB"
ATTN_HEAD_ORDER = tuple(h for j in range(KV_GROUP) for h in (j, KV_GROUP + j))

PROMPT_ROWS = 512
DECODE_PROJ_GROUPS = 4
VMEM_LIMIT_BYTES = 56 * 1024 * 1024
DECODE_VMEM_LIMIT_BYTES = 58 * 1024 * 1024

f32 = jnp.float32
bf16 = jnp.bfloat16


def _dot(a, b):
    return jnp.dot(a, b, preferred_element_type=f32)


def _dot_nt(a, b):
    return lax.dot_general(a, b, (((1,), (1,)), ((), ())), preferred_element_type=f32)


def _sigmoid(x):
    return 1.0 / (1.0 + jnp.exp(-x))


def _silu(x):
    return x * _sigmoid(x)


def _log_sigmoid(x):
    return jnp.minimum(x, 0.0) - jnp.log(1.0 + jnp.exp(-jnp.abs(x)))


def _split_parts(x, terms):
    parts, r = [], x
    for i in range(terms):
        p = r.astype(bf16)
        parts.append(p)
        if i + 1 < terms:
            r = r - p.astype(f32)
    return parts


def _split_terms(x, terms=SPLIT_TERMS):
    return jnp.concatenate(_split_parts(x, terms), axis=1)


def _exact_tril_dot(tril3, x):
    return _dot(tril3, jnp.concatenate(_split_parts(x, SPLIT_TERMS), axis=0))


def _rope(x, cos_t, sin_t, first_half):
    up = pltpu.roll(x, LANES - HEAD_DIM // 2, axis=1)
    down = pltpu.roll(x, HEAD_DIM // 2, axis=1)
    return x * cos_t + jnp.where(first_half, up, down) * sin_t


def _rms_project(x_ref, ngain_ref, win_ref, proj_sc):
    xf = x_ref[...]
    u = xf * lax.rsqrt(jnp.mean(xf * xf, axis=1, keepdims=True) + NORM_EPS) * ngain_ref[...]
    u16 = u.astype(bf16)
    for c0 in range(0, P_COLS, PROJ_COL_BLOCK):
        c1 = min(c0 + PROJ_COL_BLOCK, P_COLS)
        proj_sc[:, c0:c1] = _dot_nt(u16, win_ref[c0:c1, :])


def _prompt_kernel(x_ref, ngain_ref, win_ref, gbias_ref, mgain_ref, rgain_ref, sinks_ref, wout_ref,
                   fgain_ref, tril3_ref, maskadd_ref, dmat2_ref, rslab_ref, ubias_ref, cos_ref,
                   sin_ref, cosoff_ref, sinoff_ref, selh_ref, selp_ref,
                   y_ref, c_out, n_out, m_out, s_out, k_out, v_out,
                   proj_sc, mix_sc, cn_sc, sb_sc, m_sc, kp_sc, vp_sc, bias_sc, *, chunks, final):
    step = pl.program_id(1)
    last_step = pl.num_programs(1) - 1

    @pl.when(jnp.logical_and(pl.program_id(0) == 0, step == 0))
    def _():
        for blk in range(A_HEADS):
            u = ubias_ref[ATTN_HEAD_ORDER[blk]:ATTN_HEAD_ORDER[blk] + 1, :]
            bias_sc[blk * ROWS:(blk + 1) * ROWS, :] = jnp.concatenate(
                [_skew(u[:, :2 * WINDOW], ROWS), _skew(u[:, 2 * WINDOW:], ROWS)], axis=1)

    xf = x_ref[...]
    u16 = (xf * lax.rsqrt(jnp.mean(xf * xf, axis=1, keepdims=True) + NORM_EPS) * ngain_ref[...]).astype(bf16)
    col_blocks = [(c0, min(c0 + PROJ_COL_BLOCK, P_COLS)) for c0 in range(0, P_COLS, PROJ_COL_BLOCK)]
    half_rows = (chunks // 2) * ROWS if chunks > 1 else chunks * ROWS

    def project(r0, r1, c0, c1):
        proj_sc[r0:r1, c0:c1] = _dot_nt(u16[r0:r1], win_ref[c0:c1, :])

    for c0, c1 in col_blocks:
        project(0, half_rows, c0, c1)
    late_pieces = [(half_rows, chunks * ROWS, c0, c1) for c0, c1 in col_blocks] if half_rows < chunks * ROWS else []

    @pl.when(step == 0)
    def _():
        cn_sc[...] = jnp.zeros_like(cn_sc)
        sb_sc[...] = jnp.zeros_like(sb_sc)
        m_sc[...] = jnp.zeros_like(m_sc)
        kp_sc[...] = jnp.zeros_like(kp_sc)
        vp_sc[...] = jnp.zeros_like(vp_sc)

    lane = lax.broadcasted_iota(jnp.int32, (ROWS, LANES), 1)
    row = lax.broadcasted_iota(jnp.int32, (ROWS, LANES), 0)
    left = lane < HEAD_DIM
    first_half = (lane & (HEAD_DIM - 1)) < (HEAD_DIM // 2)
    head_col = lane < M_HEADS
    blockdiag = (row < HEAD_DIM) == left
    row2 = lax.broadcasted_iota(jnp.int32, (ROWS, 2 * LANES), 0)
    lane2w = lax.broadcasted_iota(jnp.int32, (ROWS, 2 * LANES), 1)
    left2 = (lane2w & (LANES - 1)) < HEAD_DIM
    blockdiag2 = (row2 < HEAD_DIM) == left2
    ones16 = jnp.ones((ROWS, LANES), bf16)
    halves, pick, pair_blockdiag, half_mean = _halves, _pick, _pair_blockdiag, _half_mean

    def chunk_body(ci):
        rows = slice(ci * ROWS, (ci + 1) * ROWS)

        def proj(off, width=LANES):
            return proj_sc[rows, off:off + width]

        gates = proj(OFF_G) + gbias_ref[...]
        bcum = _exact_tril_dot(tril3_ref[...], _log_sigmoid(gates))

        base = pl.ds(step * chunks + ci, 1)
        cos_a, sin_a = cos_ref[base, :], sin_ref[base, :]
        cos_b, sin_b = cosoff_ref[...], sinoff_ref[...]
        cos_t = cos_a * cos_b - sin_a * sin_b
        sin_t = sin_a * cos_b + cos_a * sin_b
        sin_t = jnp.where(first_half, -sin_t, sin_t)
        r_q = [_rope(proj(OFF_RQ + p * LANES), cos_t, sin_t, first_half) for p in range(PAIRS)]
        r_k = [_rope(proj(OFF_RK + p * LANES), cos_t, sin_t, first_half) * QK_SCALE for p in range(PAIRS)]
        r_vbd = pair_blockdiag([proj(OFF_RV + p * LANES).astype(bf16) for p in range(PAIRS)])
        r_sb = [sb_sc[p] for p in range(PAIRS)]
        r_sc = _dot_nt(jnp.concatenate([halves(q) for q in r_q], axis=1),
                       pair_blockdiag([k.astype(bf16) for k in r_k]))
        r_inter = _dot(jnp.concatenate([q.astype(bf16) for q in r_q], axis=1),
                       pair_blockdiag([sb.astype(bf16) for sb in r_sb]))
        r_upd = _dot(jnp.concatenate([(r_k[p] * rslab_ref[1, p]).T.astype(bf16) for p in range(PAIRS)], axis=1),
                     r_vbd)
        for p in range(PAIRS):
            sb_sc[p] = rslab_ref[2, p] * r_sb[p] + jnp.where(blockdiag, r_upd[:, p * LANES:(p + 1) * LANES], 0.0)
        yield

        kcur, vcur = proj(OFF_AK), proj(OFF_AV)
        kprev, vprev = kp_sc[...], vp_sc[...]
        kk16 = jnp.concatenate([kcur, kprev], axis=0).astype(bf16)
        vv16 = jnp.concatenate([jnp.concatenate([vcur.astype(bf16), ones16], axis=1),
                                jnp.concatenate([vprev.astype(bf16), ones16], axis=1)], axis=0)
        kp_sc[...] = kcur
        vp_sc[...] = vcur
        a_q = jnp.concatenate([halves(proj(OFF_AQ + j * LANES) * QK_SCALE) for j in range(KV_GROUP)], axis=0)
        a_s = _dot_nt(a_q, kk16) + bias_sc[...]
        if ci == 0:
            pen = jnp.where(step == 0, -jnp.inf, 0.0).astype(f32)
            a_s = a_s + jnp.where(lax.broadcasted_iota(jnp.int32, (1, 2 * ROWS), 1) >= ROWS, pen, 0.0)

        m_q = [proj(OFF_MQ + p * LANES) for p in range(PAIRS)]
        m_k = [proj(OFF_MK + p * LANES) * QK_SCALE for p in range(PAIRS)]
        m_qk = _dot_nt(jnp.concatenate([halves(q) for q in m_q], axis=1),
                       pair_blockdiag([k.astype(bf16) for k in m_k]))
        m_q16 = [q.astype(bf16) for q in m_q]
        m_v16 = [proj(OFF_MV + p * LANES).astype(bf16) for p in range(PAIRS)]
        yield

        zb = pltpu.roll(bcum, LANES - M_HEADS, axis=1)
        r_mat = jnp.where(head_col, gates - zb, 0.0)
        cm = r_mat
        sh = 1
        while sh < ROWS:
            cm = jnp.where(row >= sh, jnp.maximum(cm, pltpu.roll(cm, sh, axis=0)), cm)
            sh *= 2
        mprev = m_sc[...]
        mx = jnp.maximum(mprev, cm)
        gm = mprev - mx
        em = jnp.where(head_col, -(zb + mx), 0.0)
        mx_last = jnp.broadcast_to(mx[ROWS - 1:ROWS, :], (ROWS, LANES))
        m_sc[...] = jnp.where(head_col, jnp.broadcast_to((zb + mx)[ROWS - 1:ROWS, :], (ROWS, LANES)), 0.0)
        mx_b = _dot_nt(_split_terms(mx, SEL_TERMS), selh_ref[...])
        slabs = jnp.exp(_dot_nt(_split_terms(jnp.concatenate([gm, em, r_mat - mx_last], axis=0), SEL_TERMS),
                                selp_ref[...]))
        winter_b, emt_b, ws_b = slabs[:ROWS], slabs[ROWS:2 * ROWS], slabs[2 * ROWS:]
        r_t = r_mat.T
        yield

        outs = []
        r_acc = []
        r_o = _dot((r_sc * dmat2_ref[...]).astype(bf16), r_vbd)
        for p in range(PAIRS):
            ps = slice(p * LANES, (p + 1) * LANES)
            r_acc.append(pick(r_o[:, ps]) + rslab_ref[0, p] * r_inter[:, ps])

        a_out = []
        a_p = []
        for blk in range(A_HEADS):
            s = a_s[blk * ROWS:(blk + 1) * ROWS]
            sink = sinks_ref[ATTN_HEAD_ORDER[blk]]
            m = jnp.maximum(jnp.max(jnp.maximum(s[:, :ROWS], s[:, ROWS:]), axis=1, keepdims=True), sink)
            a_p.append(jnp.exp(s - m).astype(bf16))
            a_out.append(jnp.exp(sink - m))
        a_pv = _dot(jnp.concatenate(a_p, axis=0), vv16)
        yield

        maskadd = maskadd_ref[...]
        for p in range(PAIRS):
            ps = slice(p * LANES, (p + 1) * LANES)
            cn = cn_sc[p]
            w = jnp.concatenate(
                [jnp.exp((r_t[2 * p + side:2 * p + side + 1, :] + maskadd)
                         - mx_b[:, (2 * p + side) * ROWS:(2 * p + side + 1) * ROWS]) for side in range(2)],
                axis=0) * m_qk[:, ps]
            acc = (pick(_dot(w.astype(bf16), jnp.concatenate([m_v16[p], ones16], axis=1)))
                   + jnp.concatenate([winter_b[:, ps]] * 2, axis=1) * _dot(m_q16[p], cn.astype(bf16)))
            hh = acc[:, :LANES] / jnp.maximum(jnp.abs(acc[:, LANES:]), emt_b[:, ps])
            outs.append(_sigmoid(proj(OFF_MO + p * LANES)) * hh)
            kwt16 = (m_k[p] * ws_b[:, ps]).T.astype(bf16)
            dcn = _dot(kwt16, jnp.concatenate([m_v16[p], ones16], axis=1))
            decay = winter_b[ROWS - 1:ROWS, ps]
            cn_sc[p] = jnp.concatenate([decay, decay], axis=1) * cn + jnp.where(blockdiag2, dcn, 0.0)
        outs.extend(r_acc)
        yield

        x4 = jnp.concatenate(outs, axis=0)
        xc = x4 - half_mean(x4)
        y4 = xc * lax.rsqrt(half_mean(xc * xc) + NORM_EPS)
        for i in range(2 * PAIRS):
            gain = (mgain_ref if i < PAIRS else rgain_ref)[:, (i % PAIRS) * LANES:(i % PAIRS + 1) * LANES]
            zoff = (OFF_MZ if i < PAIRS else OFF_RZ) + (i % PAIRS) * LANES
            out = y4[i * ROWS:(i + 1) * ROWS] * gain * _silu(proj(zoff))
            mix_sc[rows, i * LANES:(i + 1) * LANES] = out.astype(bf16)
        for j in range(KV_GROUP):
            acc = pick(a_pv[2 * j * ROWS:(2 * j + 2) * ROWS])
            den = acc[:, LANES:] + jnp.where(left, a_out[2 * j], a_out[2 * j + 1])
            out = (acc[:, :LANES] / den) * _silu(proj(OFF_AZ + j * LANES))
            mix_sc[rows, M_DIM + R_DIM + j * LANES:M_DIM + R_DIM + (j + 1) * LANES] = out.astype(bf16)

    def out_project(r0, r1, c0, c1):
        y_ref[r0:r1, c0:c1] = x_ref[r0:r1, c0:c1] + _dot(mix_sc[r0:r1, :], wout_ref[:, c0:c1])

    out_blocks = [(c0, min(c0 + PROJ_COL_BLOCK, D_MODEL)) for c0 in range(0, D_MODEL, PROJ_COL_BLOCK)]
    early_out = [(0, half_rows, c0, c1) for c0, c1 in out_blocks] if half_rows < chunks * ROWS else []
    final_out = [(half_rows if early_out else 0, chunks * ROWS, c0, c1) for c0, c1 in out_blocks]

    def fill_mxu(ci):
        if late_pieces:
            project(*late_pieces.pop(0))
        elif early_out and (ci - 1) * ROWS >= half_rows:
            out_project(*early_out.pop(0))

    parts = [chunk_body(ci) for ci in range(chunks)]
    for ci in range(chunks + 1):
        if ci * ROWS >= half_rows:
            while late_pieces:
                project(*late_pieces.pop(0))
        for which in SCHEDULE:
            if which == "F" and ci < chunks:
                next(parts[ci])
                fill_mxu(ci)
            if which == "B" and ci > 0:
                next(parts[ci - 1], None)
                fill_mxu(ci)
    for piece in early_out + final_out:
        out_project(*piece)
    if final:
        y = y_ref[...]
        y_ref[...] = y * lax.rsqrt(jnp.mean(y * y, axis=1, keepdims=True) + NORM_EPS) * fgain_ref[...]

    @pl.when(step == last_step)
    def _():
        for p in range(PAIRS):
            cn = cn_sc[p]
            sb = sb_sc[p]
            n_t = cn[:, LANES:].T
            for side in range(2):
                h = 2 * p + side
                blk = slice(side * HEAD_DIM, (side + 1) * HEAD_DIM)
                c_out[0, h] = cn[blk, blk]
                s_out[0, h] = sb[blk, blk]
                n_out[0, h:h + 1, :] = n_t[side * HEAD_DIM:side * HEAD_DIM + 1, blk]
        m_out[0] = m_sc[...]
        k_out[0] = kp_sc[...]
        v_out[0] = vp_sc[...]


def _lane_is_left(shape):
    return (lax.broadcasted_iota(jnp.int32, shape, 1) & (LANES - 1)) < HEAD_DIM


def _halves(x):
    left = _lane_is_left(x.shape)
    return jnp.concatenate([jnp.where(left, x, 0.0), jnp.where(left, 0.0, x)], axis=0).astype(bf16)


def _pick(x):
    return jnp.where(_lane_is_left((ROWS, x.shape[1])), x[:ROWS], x[ROWS:])


def _pair_blockdiag(blocks):
    z = jnp.zeros_like(blocks[0])
    return jnp.concatenate(
        [jnp.concatenate([blk if j == i else z for j in range(len(blocks))], axis=1)
         for i, blk in enumerate(blocks)], axis=0)


def _half_mean(x):
    left = _lane_is_left(x.shape)
    s_left = jnp.sum(jnp.where(left, x, 0.0), axis=1, keepdims=True)
    s_right = jnp.sum(jnp.where(left, 0.0, x), axis=1, keepdims=True)
    return jnp.where(left, s_left, s_right) * (1.0 / HEAD_DIM)


def _group_last(x, groups):
    n = x.shape[1]
    glen = ROWS // groups
    x3 = x.reshape(groups, glen, n)
    return jnp.broadcast_to(x3[:, glen - 1:glen, :], (groups, glen, n)).reshape(ROWS, n)


def _sample_kernel(x_ref, ngain_ref, win_ref, gbias_ref, mgain_ref, rgain_ref, sinks_ref, wout_ref,
                   fgain_ref, tril3_ref, maskadd_ref, dmat2_ref, rslab_ref, ubias_ref, cos_ref, sin_ref,
                   selh_ref, selp_ref,
                   c_in, n_in, m_in, s_in, k_in, v_in,
                   y_ref, c_out, n_out, m_out, s_out, k_out, v_out,
                   proj_sc, mix_sc, xcur_sc, xall_sc, qb_sc, sp_sc, pp_sc, ob_sc, biasc_ref, biasp_ref, *, groups,
                   proj_groups, ret_full):
    glen = ROWS // groups
    glen_log2 = glen.bit_length() - 1
    hd_log2 = HEAD_DIM.bit_length() - 1
    layer = pl.program_id(0)
    step = pl.program_id(1)
    last_layer = pl.num_programs(0) - 1
    xrows = pl.ds(pl.multiple_of(step * ROWS, ROWS), ROWS)
    sub = lax.rem(step, proj_groups)
    rows = pl.ds(pl.multiple_of(sub * ROWS, ROWS), ROWS)
    wide = pl.ds(pl.multiple_of((step - sub) * ROWS, ROWS), proj_groups * ROWS)

    @pl.when(jnp.logical_and(sub == 0, layer == 0))
    def _():
        _rms_project(x_ref, ngain_ref, win_ref, proj_sc)

    @pl.when(jnp.logical_and(sub == 0, layer != 0))
    def _():
        _rms_project(xall_sc.at[wide, :], ngain_ref, win_ref, proj_sc)

    @pl.when(layer == 0)
    def _():
        xcur_sc[...] = x_ref[rows, :]

    @pl.when(layer != 0)
    def _():
        xcur_sc[...] = xall_sc[xrows, :]

    for b in range(groups):
        k_out[b] = pltpu.roll(k_in[b], WINDOW - glen, axis=1)
        v_out[b] = pltpu.roll(v_in[b], WINDOW - glen, axis=1)

    @pl.when(jnp.logical_and(step == 0, layer == 0))
    def _():
        for h in range(A_HEADS):
            u = ubias_ref[h:h + 1, :]
            biasc_ref[h] = _skew(u[:, :2 * WINDOW], ROWS) + maskadd_ref[...]
            biasp_ref[h] = jnp.concatenate([_skew(u[:, 2 * WINDOW:], glen)] * groups, axis=0)

    lane = lax.broadcasted_iota(jnp.int32, (ROWS, LANES), 1)
    first_half = (lane & (HEAD_DIM - 1)) < (HEAD_DIM // 2)

    r_i = lax.broadcasted_iota(jnp.int32, (ROWS, groups * HEAD_DIM), 0)
    c_i = lax.broadcasted_iota(jnp.int32, (ROWS, groups * HEAD_DIM), 1)
    blk = (r_i >> glen_log2) == (c_i >> hd_log2)
    r_t = lax.broadcasted_iota(jnp.int32, (groups * HEAD_DIM, ROWS), 0)
    c_t = lax.broadcasted_iota(jnp.int32, (groups * HEAD_DIM, ROWS), 1)
    blk_t = (r_t >> hd_log2) == (c_t >> glen_log2)

    def q_times_state(qh, st):
        qt = jnp.where(blk, jnp.concatenate([qh] * groups, axis=1), 0.0)
        return _dot(qt.astype(bf16), st.astype(bf16))

    def state_increment(kt_h, vh16):
        kt = jnp.where(blk_t, jnp.concatenate([kt_h] * groups, axis=0), 0.0)
        return _dot(kt.astype(bf16), vh16)

    def proj(off, width=LANES):
        return proj_sc[rows, off:off + width]

    def head_state_rows(slab, side):
        wide = slab[:, side * HEAD_DIM:(side + 1) * HEAD_DIM].reshape(groups, glen, HEAD_DIM)[:, 0:1, :]
        rows_ = jnp.broadcast_to(wide, (groups, HEAD_DIM, HEAD_DIM)).reshape(groups * HEAD_DIM, HEAD_DIM)
        return rows_, wide.reshape(groups, HEAD_DIM)

    row = lax.broadcasted_iota(jnp.int32, (ROWS, LANES), 0)
    tau = row & (glen - 1)
    head_col = lane < M_HEADS
    left = lane < HEAD_DIM
    ones16 = jnp.ones((ROWS, LANES), bf16)
    gates = proj(OFF_G) + gbias_ref[...]
    bcum = _exact_tril_dot(tril3_ref[...], _log_sigmoid(gates))
    zb = pltpu.roll(bcum, LANES - M_HEADS, axis=1)
    r_mat = jnp.where(head_col, gates - zb, 0.0)
    cm = r_mat
    sh = 1
    while sh < glen:
        cm = jnp.where(tau >= sh, jnp.maximum(cm, pltpu.roll(cm, sh, axis=0)), cm)
        sh *= 2
    mprev = m_in[...]
    mx = jnp.maximum(mprev, cm)
    gm = mprev - mx
    em = jnp.where(head_col, -(zb + mx), 0.0)
    mx_last = _group_last(mx, groups)
    m_out[...] = jnp.where(head_col, _group_last(zb + mx, groups), 0.0)
    mx_b = _dot_nt(_split_terms(mx, SEL_TERMS), selh_ref[...])
    slabs = jnp.exp(_dot_nt(_split_terms(jnp.concatenate([gm, em, r_mat - mx_last], axis=0), SEL_TERMS), selp_ref[...]))
    winter_b, emt_b, ws_b = slabs[:ROWS], slabs[ROWS:2 * ROWS], slabs[2 * ROWS:]
    decay_b = _group_last(winter_b, groups)
    r_t = r_mat.T
    maskadd = maskadd_ref[...]

    m_q = [proj(OFF_MQ + p * LANES) for p in range(PAIRS)]
    m_k = [proj(OFF_MK + p * LANES) * QK_SCALE for p in range(PAIRS)]
    m_v = [proj(OFF_MV + p * LANES) for p in range(PAIRS)]
    m_qk = _dot_nt(jnp.concatenate([_halves(q) for q in m_q], axis=1),
                   _pair_blockdiag([k.astype(bf16) for k in m_k]))
    outs = []
    for p in range(PAIRS):
        ps = slice(p * LANES, (p + 1) * LANES)
        w = jnp.concatenate(
            [jnp.exp((r_t[2 * p + side:2 * p + side + 1, :] + maskadd)
                     - mx_b[:, (2 * p + side) * ROWS:(2 * p + side + 1) * ROWS]) for side in range(2)],
            axis=0) * m_qk[:, ps]
        intra = _pick(_dot(w.astype(bf16), jnp.concatenate([m_v[p].astype(bf16), ones16], axis=1)))
        kw = m_k[p] * ws_b[:, ps]
        kwt = kw.T
        q_c, q_n = [], []
        for side in range(2):
            h = 2 * p + side
            hs = slice(side * HEAD_DIM, (side + 1) * HEAD_DIM)
            qh = m_q[p][:, hs]
            c_h = c_in[:, h].reshape(groups * HEAD_DIM, HEAD_DIM)
            n_g = n_in[h]
            n_rows = jnp.broadcast_to(n_g.reshape(groups, 1, HEAD_DIM),
                                      (groups, glen, HEAD_DIM)).reshape(ROWS, HEAD_DIM)
            q_c.append(q_times_state(qh, c_h))
            q_n.append(jnp.sum(qh * n_rows, axis=1, keepdims=True))
            dec_rows, dec_g = head_state_rows(decay_b[:, ps], side)
            c_new = dec_rows * c_h + state_increment(kwt[hs, :], m_v[p][:, hs].astype(bf16))
            c_out[:, h] = c_new.reshape(groups, HEAD_DIM, HEAD_DIM)
            n_out[h] = dec_g * n_g + jnp.sum(kw[:, hs].reshape(groups, glen, HEAD_DIM), axis=1)
        wb = winter_b[:, ps]
        num = intra[:, :LANES] + wb * jnp.concatenate(q_c, axis=1)
        nq = intra[:, LANES:] + wb * jnp.where(left, q_n[0], q_n[1])
        outs.append(_sigmoid(proj(OFF_MO + p * LANES)) * (num / jnp.maximum(jnp.abs(nq), emt_b[:, ps])))

    cos_t, sin_t = cos_ref[...], sin_ref[...]
    r_q = [_rope(proj(OFF_RQ + p * LANES), cos_t, sin_t, first_half) for p in range(PAIRS)]
    r_k = [_rope(proj(OFF_RK + p * LANES), cos_t, sin_t, first_half) * QK_SCALE for p in range(PAIRS)]
    r_v = [proj(OFF_RV + p * LANES) for p in range(PAIRS)]
    r_sc = _dot_nt(jnp.concatenate([_halves(q) for q in r_q], axis=1),
                   _pair_blockdiag([k.astype(bf16) for k in r_k]))
    r_o = _dot((r_sc * dmat2_ref[...]).astype(bf16), _pair_blockdiag([v.astype(bf16) for v in r_v]))
    for p in range(PAIRS):
        ps = slice(p * LANES, (p + 1) * LANES)
        rkt = (r_k[p] * rslab_ref[1, p]).T
        q_s = []
        for side in range(2):
            h = 2 * p + side
            hs = slice(side * HEAD_DIM, (side + 1) * HEAD_DIM)
            s_h = s_in[:, h].reshape(groups * HEAD_DIM, HEAD_DIM)
            q_s.append(q_times_state(r_q[p][:, hs], s_h))
            s_new = ret_full[h] * s_h + state_increment(rkt[hs, :], r_v[p][:, hs].astype(bf16))
            s_out[:, h] = s_new.reshape(groups, HEAD_DIM, HEAD_DIM)
        outs.append(_pick(r_o[:, ps]) + rslab_ref[0, p] * jnp.concatenate(q_s, axis=1))

    x4 = jnp.concatenate(outs, axis=0)
    xc = x4 - _half_mean(x4)
    y4 = xc * lax.rsqrt(_half_mean(xc * xc) + NORM_EPS)
    for i in range(2 * PAIRS):
        gain = (mgain_ref if i < PAIRS else rgain_ref)[:, (i % PAIRS) * LANES:(i % PAIRS + 1) * LANES]
        zoff = (OFF_MZ if i < PAIRS else OFF_RZ) + (i % PAIRS) * LANES
        mix_sc[:, i * LANES:(i + 1) * LANES] = (y4[i * ROWS:(i + 1) * ROWS] * gain * _silu(proj(zoff))).astype(bf16)

    kcur = proj_sc[rows, OFF_AK:OFF_AK + A_KV_DIM]
    vcur = proj_sc[rows, OFF_AV:OFF_AV + A_KV_DIM]
    kcur16, vcur16 = kcur.astype(bf16), vcur.astype(bf16)

    a_q = {}
    for j in range(KV_GROUP):
        hq = _halves(proj(OFF_AQ + j * LANES) * QK_SCALE)
        a_q[j], a_q[KV_GROUP + j] = hq[:ROWS], hq[ROWS:]
    for h in range(A_HEADS):
        qb_sc[:, h * glen:(h + 1) * glen, :] = a_q[h].reshape(groups, glen, LANES)
    sc_all = _dot_nt(jnp.concatenate([a_q[h] for h in range(A_HEADS)], axis=0), kcur16)

    for b in range(groups):
        sp = _dot(qb_sc[b], k_in[b].astype(bf16))
        sp_sc[:, b * glen:(b + 1) * glen, :] = sp.reshape(A_HEADS, glen, WINDOW)

    vcur1 = jnp.concatenate([vcur16, ones16], axis=1)
    esink, o_cur = [], []
    for h in range(A_HEADS):
        sc = sc_all[h * ROWS:(h + 1) * ROWS] + biasc_ref[h]
        sp = sp_sc[h] + biasp_ref[h]
        sink = sinks_ref[layer, h]
        m = jnp.maximum(jnp.max(jnp.maximum(sc, sp), axis=1, keepdims=True), sink)
        esink.append(jnp.exp(sink - m))
        o_cur.append(_dot(jnp.exp(sc - m).astype(bf16), vcur1))
        pp_sc[:, h * glen:(h + 1) * glen, :] = jnp.exp(sp - m).reshape(groups, glen, WINDOW).astype(bf16)

    for b in range(groups):
        v1 = jnp.concatenate([v_in[b].astype(bf16), ones16], axis=0)
        ob = _dot_nt(pp_sc[b], v1)
        ob_sc[:, b * glen:(b + 1) * glen, :] = ob.reshape(A_HEADS, glen, 2 * LANES)
    norm = []
    for h in range(A_HEADS):
        acc = o_cur[h] + ob_sc[h]
        norm.append(acc[:, :LANES] / (acc[:, LANES:] + esink[h]))
    outs = [jnp.where(left, norm[j], norm[KV_GROUP + j]) for j in range(KV_GROUP)]
    kcur_t, vcur_t = kcur.T, vcur.T
    new = slice(WINDOW - glen, WINDOW)
    for b in range(groups):
        shift = (WINDOW - glen - b * glen) % LANES
        k_out[b, :, new] = pltpu.roll(kcur_t, shift, axis=1)[:, new]
        v_out[b, :, new] = pltpu.roll(vcur_t, shift, axis=1)[:, new]
    out_a = jnp.concatenate(outs, axis=1) * _silu(proj_sc[rows, OFF_AZ:OFF_AZ + A_DIM])
    mix_sc[:, M_DIM + R_DIM:M_DIM + R_DIM + A_DIM] = out_a.astype(bf16)

    y = xcur_sc[...] + _dot(mix_sc[...], wout_ref[...])

    @pl.when(layer != last_layer)
    def _():
        xall_sc[xrows, :] = y
        y_ref[...] = y

    @pl.when(layer == last_layer)
    def _():
        y_ref[...] = y * lax.rsqrt(jnp.mean(y * y, axis=1, keepdims=True) + NORM_EPS) * fgain_ref[...]


def _t5_bucket(dist):
    max_exact = N_BUCKETS // 2
    d = np.maximum(dist, 1).astype(np.float32)
    large = max_exact + (np.log(d / max_exact) / np.log(REL_MAX_DIST / max_exact)
                         * (N_BUCKETS - max_exact)).astype(np.int32)
    large = np.minimum(large, N_BUCKETS - 1)
    return np.where(dist < max_exact, dist, large).astype(np.int32)


def _static_tables(groups):
    glen = ROWS // groups
    r = np.arange(ROWS)
    grp, tau = r // glen, r % glen
    causal = (grp[:, None] == grp[None, :]) & (tau[None, :] <= tau[:, None])
    tril = causal.astype(np.float32)
    maskadd = np.where(causal, 0.0, -np.inf).astype(np.float32)
    log_g = np.log1p(-np.exp2(-5.0 - np.arange(R_HEADS, dtype=np.float64)))
    diff = (tau[:, None] - tau[None, :]).astype(np.float64)
    dmat = np.where(causal[None], np.exp(log_g[:, None, None] * np.maximum(diff, 0.0)[None]), 0.0)
    inter = np.exp(log_g[None, :] * (tau[:, None] + 1.0))
    tail = np.exp(log_g[None, :] * (glen - 1.0 - tau[:, None]))
    full = np.exp(log_g * glen)
    lane_head = np.arange(LANES) // HEAD_DIM
    rslab = np.zeros((3, PAIRS, ROWS, LANES), np.float64)
    for p in range(PAIRS):
        rslab[0, p] = inter[:, 2 * p + lane_head]
        rslab[1, p] = tail[:, 2 * p + lane_head]
        rslab[2, p] = full[2 * p + lane_head][None, :]
    selh = np.zeros((M_HEADS * ROWS, SEL_TERMS * LANES), np.float32)
    selp = np.zeros((PAIRS * LANES, SEL_TERMS * LANES), np.float32)
    for t in range(SEL_TERMS):
        for h in range(M_HEADS):
            selh[h * ROWS:(h + 1) * ROWS, t * LANES + h] = 1.0
        for p in range(PAIRS):
            for side in range(2):
                selp[p * LANES + side * HEAD_DIM:p * LANES + (side + 1) * HEAD_DIM, t * LANES + 2 * p + side] = 1.0
    return dict(tril3=jnp.asarray(np.concatenate([tril] * SPLIT_TERMS, axis=1), bf16),
                maskadd=maskadd,
                dmat2=np.concatenate(list(dmat.astype(np.float32).reshape(PAIRS, 2 * ROWS, ROWS)), axis=1),
                rslab=rslab.astype(np.float32),
                full=tuple(float(v) for v in full),
                selh=jnp.asarray(selh, bf16), selp=jnp.asarray(selp, bf16))


def _bias_vectors(rel_table):
    tb = jnp.transpose(rel_table[_t5_bucket(np.arange(WINDOW))]).astype(f32)
    ninf = jnp.full((A_HEADS, WINDOW), -jnp.inf, f32)
    rev = tb[:, :0:-1]
    return jnp.concatenate([tb[:, :1], ninf, rev, ninf[:, :1], rev, ninf], axis=1)


def _skew(u_row, rows):
    x = jnp.broadcast_to(u_row, (rows, 2 * WINDOW))
    return pltpu.roll(x, 0, 1, stride=1, stride_axis=0)[:, :WINDOW]


def _rope_tables(pos, signed=True):
    half = HEAD_DIM // 2
    inv = ROPE_BASE ** (-jnp.arange(half, dtype=f32) / half)
    ang = pos.astype(f32)[:, None] * inv[None, :]
    cos, sin = jnp.cos(ang), jnp.sin(ang)
    reps = LANES // HEAD_DIM
    cos_t = jnp.tile(jnp.concatenate([cos, cos], axis=1), (1, reps))
    sin_t = jnp.tile(jnp.concatenate([-sin if signed else sin, sin], axis=1), (1, reps))
    return cos_t, sin_t


def _const_spec(shape, nargs):
    zeros = (0,) * len(shape)
    if nargs == 1:
        return pl.BlockSpec(shape, lambda i: zeros)
    return pl.BlockSpec(shape, lambda i, j: zeros)


def _layer_spec(shape, layer, nargs):
    idx = (layer,) + (0,) * len(shape)
    if nargs == 1:
        return pl.BlockSpec((None,) + shape, lambda i: idx)
    return pl.BlockSpec((None,) + shape, lambda i, j: idx)


def _param_specs(layer, nargs):
    ls = functools.partial(_layer_spec, layer=layer, nargs=nargs)
    return [ls((1, D_MODEL)), ls((P_COLS, D_MODEL)), ls((1, LANES)), ls((1, M_DIM)), ls((1, R_DIM)),
            pl.BlockSpec(memory_space=pltpu.SMEM), ls((D_MODEL, D_MODEL)), _const_spec((1, D_MODEL), nargs)]


def _param_args(p, layer):
    return (p["norm_gain"], p["w_in"], p["gbias"], p["m_gain"], p["r_gain"], p["sinks"][layer], p["w_out"],
            p["fgain"])


def _prompt_layer(x, p, layer, tabs, ubias, rope, final):
    B, T, _ = x.shape
    tb = min(PROMPT_ROWS, T)
    chunks = tb // ROWS
    nt = T // tb
    cs = functools.partial(_const_spec, nargs=2)
    in_specs = [pl.BlockSpec((None, tb, D_MODEL), lambda b, t: (b, t, 0))] + _param_specs(layer, 2) + [
        cs((ROWS, SPLIT_TERMS * ROWS)), cs((ROWS, ROWS)), cs((2 * ROWS, PAIRS * ROWS)),
        cs((3, PAIRS, ROWS, LANES)), cs((A_HEADS, 4 * WINDOW)),
        cs((T // ROWS, LANES)), cs((T // ROWS, LANES)), cs((ROWS, LANES)), cs((ROWS, LANES)),
        cs((M_HEADS * ROWS, SEL_TERMS * LANES)), cs((PAIRS * LANES, SEL_TERMS * LANES)),
    ]
    out_shape = (
        jax.ShapeDtypeStruct((B, T, D_MODEL), f32),
        jax.ShapeDtypeStruct((B, M_HEADS, HEAD_DIM, HEAD_DIM), f32),
        jax.ShapeDtypeStruct((B, M_HEADS, HEAD_DIM), f32),
        jax.ShapeDtypeStruct((B, ROWS, LANES), f32),
        jax.ShapeDtypeStruct((B, R_HEADS, HEAD_DIM, HEAD_DIM), f32),
        jax.ShapeDtypeStruct((B, WINDOW, A_KV_DIM), f32),
        jax.ShapeDtypeStruct((B, WINDOW, A_KV_DIM), f32),
    )
    out_specs = (
        pl.BlockSpec((None, tb, D_MODEL), lambda b, t: (b, t, 0)),
        pl.BlockSpec((1, M_HEADS, HEAD_DIM, HEAD_DIM), lambda b, t: (b, 0, 0, 0)),
        pl.BlockSpec((1, M_HEADS, HEAD_DIM), lambda b, t: (b, 0, 0)),
        pl.BlockSpec((1, ROWS, LANES), lambda b, t: (b, 0, 0)),
        pl.BlockSpec((1, R_HEADS, HEAD_DIM, HEAD_DIM), lambda b, t: (b, 0, 0, 0)),
        pl.BlockSpec((1, WINDOW, A_KV_DIM), lambda b, t: (b, 0, 0)),
        pl.BlockSpec((1, WINDOW, A_KV_DIM), lambda b, t: (b, 0, 0)),
    )
    kern = functools.partial(_prompt_kernel, chunks=chunks, final=final)
    y, c, n, m, s, k, v = pl.pallas_call(
        kern, grid=(B, nt), in_specs=in_specs, out_specs=out_specs, out_shape=out_shape,
        scratch_shapes=[pltpu.VMEM((tb, P_COLS), f32), pltpu.VMEM((tb, D_MODEL), bf16),
                        pltpu.VMEM((PAIRS, ROWS, 2 * LANES), f32), pltpu.VMEM((PAIRS, ROWS, LANES), f32),
                        pltpu.VMEM((ROWS, LANES), f32), pltpu.VMEM((ROWS, A_KV_DIM), f32),
                        pltpu.VMEM((ROWS, A_KV_DIM), f32), pltpu.VMEM((A_HEADS * ROWS, 2 * WINDOW), f32)],
        compiler_params=pltpu.CompilerParams(dimension_semantics=("arbitrary", "arbitrary"),
                                             vmem_limit_bytes=VMEM_LIMIT_BYTES),
        name="prompt_layer",
    )(x, *_param_args(p, layer), tabs["tril3"], tabs["maskadd"], tabs["dmat2"], tabs["rslab"], ubias,
      *rope, tabs["selh"], tabs["selp"])
    k = k.reshape(B, WINDOW, A_KV_HEADS, HEAD_DIM)
    v = v.reshape(B, WINDOW, A_KV_HEADS, HEAD_DIM)
    return y, c, n, m[:, 0, :M_HEADS], s, k, v


def _sample_path(x, states, p, tabs, ubias, cos_t, sin_t):
    B, T, _ = x.shape
    groups = ROWS // T
    nb = B // groups
    pg = math.gcd(nb, DECODE_PROJ_GROUPS)
    c0, n0, m0, s0, k0, v0 = states
    depth = c0.shape[0]
    x2 = x.reshape(B * T, D_MODEL)
    n0t = jnp.transpose(n0, (0, 2, 1, 3))
    m0r = jnp.pad(jnp.repeat(m0, T, axis=1), ((0, 0), (0, 0), (0, LANES - M_HEADS)))
    k0r = jnp.transpose(k0, (0, 1, 3, 4, 2)).reshape(depth, B, A_KV_DIM, WINDOW)
    v0r = jnp.transpose(v0, (0, 1, 3, 4, 2)).reshape(depth, B, A_KV_DIM, WINDOW)

    def cs(shape):
        zeros = (0,) * len(shape)
        return pl.BlockSpec(shape, lambda l, i: zeros)

    def per_layer(shape, buffers=None):
        zeros = (0,) * len(shape)
        mode = {} if buffers is None else dict(pipeline_mode=pl.Buffered(buffers))
        return pl.BlockSpec((None,) + shape, lambda l, i: (l,) + zeros, **mode)

    st4 = pl.BlockSpec((None, groups, M_HEADS, HEAD_DIM, HEAD_DIM), lambda l, i: (l, i, 0, 0, 0))
    stn = pl.BlockSpec((None, M_HEADS, groups, HEAD_DIM), lambda l, i: (l, 0, i, 0))
    stm = pl.BlockSpec((None, ROWS, LANES), lambda l, i: (l, i, 0))
    stk = pl.BlockSpec((None, groups, A_KV_DIM, WINDOW), lambda l, i: (l, i, 0, 0))
    in_specs = [
        pl.BlockSpec((pg * ROWS, D_MODEL), lambda l, i: (i // pg, 0)),
        per_layer((1, D_MODEL)), per_layer((P_COLS, D_MODEL), 1), per_layer((1, LANES)), per_layer((1, M_DIM)),
        per_layer((1, R_DIM)), pl.BlockSpec(memory_space=pltpu.SMEM), per_layer((D_MODEL, D_MODEL), 1),
        cs((1, D_MODEL)),
        cs((ROWS, SPLIT_TERMS * ROWS)), cs((ROWS, ROWS)), cs((2 * ROWS, PAIRS * ROWS)),
        cs((3, PAIRS, ROWS, LANES)), cs((A_HEADS, 4 * WINDOW)),
        cs((ROWS, LANES)), cs((ROWS, LANES)),
        cs((M_HEADS * ROWS, SEL_TERMS * LANES)), cs((PAIRS * LANES, SEL_TERMS * LANES)),
        st4, stn, stm, st4, stk, stk,
    ]
    out_shape = (
        jax.ShapeDtypeStruct((B * T, D_MODEL), f32),
        jax.ShapeDtypeStruct((depth, B, M_HEADS, HEAD_DIM, HEAD_DIM), f32),
        jax.ShapeDtypeStruct((depth, M_HEADS, B, HEAD_DIM), f32),
        jax.ShapeDtypeStruct((depth, B * T, LANES), f32),
        jax.ShapeDtypeStruct((depth, B, R_HEADS, HEAD_DIM, HEAD_DIM), f32),
        jax.ShapeDtypeStruct((depth, B, A_KV_DIM, WINDOW), f32),
        jax.ShapeDtypeStruct((depth, B, A_KV_DIM, WINDOW), f32),
    )
    y_spec = pl.BlockSpec((ROWS, D_MODEL), lambda l, i: (jnp.where(l == depth - 1, i, 0), 0))
    out_specs = (y_spec, st4, stn, stm, st4, stk, stk)
    kern = functools.partial(_sample_kernel, groups=groups, proj_groups=pg, ret_full=tabs["full"])
    y, c, n, m, s, k, v = pl.pallas_call(
        kern, grid=(depth, nb), in_specs=in_specs, out_specs=out_specs, out_shape=out_shape,
        scratch_shapes=[pltpu.VMEM((pg * ROWS, P_COLS), f32), pltpu.VMEM((ROWS, D_MODEL), bf16),
                        pltpu.VMEM((ROWS, D_MODEL), f32), pltpu.VMEM((B * T, D_MODEL), f32),
                        pltpu.VMEM((groups, A_HEADS * T, A_KV_DIM), bf16),
                        pltpu.VMEM((A_HEADS, ROWS, WINDOW), f32),
                        pltpu.VMEM((groups, A_HEADS * T, WINDOW), bf16),
                        pltpu.VMEM((A_HEADS, ROWS, 2 * LANES), f32),
                        pltpu.VMEM((A_HEADS, ROWS, ROWS), f32), pltpu.VMEM((A_HEADS, ROWS, WINDOW), f32)],
        compiler_params=pltpu.CompilerParams(dimension_semantics=("arbitrary", "arbitrary"),
                                             vmem_limit_bytes=DECODE_VMEM_LIMIT_BYTES),
        name="sample_path",
    )(x2, p["norm_gain"], p["w_in"], p["gbias"], p["m_gain"], p["r_gain"], p["sinks"], p["w_out"], p["fgain"],
      tabs["tril3"], tabs["maskadd"], tabs["dmat2"], tabs["rslab"], ubias,
      cos_t, sin_t, tabs["selh"], tabs["selp"], c0, n0t, m0r, s0, k0r, v0r)
    y = y.reshape(B, T, D_MODEL)
    n = jnp.transpose(n, (0, 2, 1, 3))
    m = m.reshape(depth, B, T, LANES)[:, :, 0, :M_HEADS]
    k = jnp.transpose(k.reshape(depth, B, A_KV_HEADS, HEAD_DIM, WINDOW), (0, 1, 4, 2, 3))
    v = jnp.transpose(v.reshape(depth, B, A_KV_HEADS, HEAD_DIM, WINDOW), (0, 1, 4, 2, 3))
    return y, c, n, m, s, k, v


def _prepare_params(norm_gain, w_in, mlstm_gate_bias, mlstm_norm_gain, ret_norm_gain, attn_sinks, w_out,
                    final_norm_gain):
    depth = w_in.shape[0]
    w_t = jnp.swapaxes(w_in, 1, 2)
    split = OFF_G + N_GATES
    aq0 = split + 4 * R_DIM
    akv0 = aq0 + A_DIM
    az0 = akv0 + 2 * A_KV_DIM

    def by_head(w):
        w = w.reshape(depth, A_KV_HEADS, KV_GROUP, HEAD_DIM, D_MODEL)
        return jnp.transpose(w, (0, 2, 1, 3, 4)).reshape(depth, A_DIM, D_MODEL)

    w_in_p = jnp.concatenate(
        [w_t[:, :split], jnp.zeros((depth, GATE_PAD - N_GATES, D_MODEL), w_t.dtype), w_t[:, split:aq0],
         by_head(w_t[:, aq0:akv0]), w_t[:, akv0:az0], by_head(w_t[:, az0:])], axis=1).astype(bf16)
    wo16 = w_out.astype(bf16)
    a0 = M_DIM + R_DIM
    w_out_p = jnp.concatenate([wo16[:, :a0, :], by_head(wo16[:, a0:, :])], axis=1)
    gbias = jnp.pad(mlstm_gate_bias.reshape(depth, 1, N_GATES), ((0, 0), (0, 0), (0, LANES - N_GATES)))
    return dict(norm_gain=norm_gain.reshape(depth, 1, D_MODEL), w_in=w_in_p, gbias=gbias,
                m_gain=mlstm_norm_gain.reshape(depth, 1, M_DIM), r_gain=ret_norm_gain.reshape(depth, 1, R_DIM),
                sinks=attn_sinks, w_out=w_out_p, fgain=final_norm_gain.reshape(1, D_MODEL))


def kernel(x_prompt, x_sample, state_mlstm_C, state_mlstm_n, state_mlstm_m, state_ret_S, cache_win_k,
           cache_win_v, norm_gain, w_in, mlstm_gate_bias, mlstm_norm_gain, ret_norm_gain, attn_sinks,
           rel_bias_table, w_out, final_norm_gain):
    depth = w_in.shape[0]
    seq = x_prompt.shape[1]
    dec_seq = x_sample.shape[1]
    past_len = seq
    p = _prepare_params(norm_gain, w_in, mlstm_gate_bias, mlstm_norm_gain, ret_norm_gain, attn_sinks, w_out,
                        final_norm_gain)
    tabs_p = _static_tables(1)
    tabs_s = _static_tables(ROWS // dec_seq)
    ubias = _bias_vectors(rel_bias_table)
    rope_p = (*_rope_tables(jnp.arange(0, seq, ROWS, dtype=jnp.int32), signed=False),
              *_rope_tables(jnp.arange(ROWS, dtype=jnp.int32), signed=False))
    cos_s, sin_s = _rope_tables(past_len + (jnp.arange(ROWS, dtype=jnp.int32) % dec_seq))
    states = (state_mlstm_C, state_mlstm_n, state_mlstm_m, state_ret_S, cache_win_k, cache_win_v)

    xp = x_prompt
    p_states = []
    for layer in range(depth):
        xp, *sp = _prompt_layer(xp, p, layer, tabs_p, ubias, rope_p, layer == depth - 1)
        p_states.append(sp)
    outs_p = [jnp.stack([p_states[l][i] for l in range(depth)]) for i in range(6)]
    xs, *outs_s = _sample_path(x_sample, states, p, tabs_s, ubias, cos_s, sin_s)
    return (xp, xs, *outs_p, *outs_s)
```

```python
import functools
import math

import numpy as np
import jax
import jax.numpy as jnp
from jax import lax
from jax.experimental import pallas as pl
from jax.experimental.pallas import tpu as pltpu

D_MODEL = 1024
HEAD_DIM = 64
M_HEADS = 4
R_HEADS = 4
A_HEADS = 8
A_KV_HEADS = 2
KV_GROUP = A_HEADS // A_KV_HEADS
M_DIM = M_HEADS * HEAD_DIM
R_DIM = R_HEADS * HEAD_DIM
A_DIM = A_HEADS * HEAD_DIM
A_KV_DIM = A_KV_HEADS * HEAD_DIM
WINDOW = 128
N_BUCKETS = 32
REL_MAX_DIST = 128
ROPE_BASE = 10000.0
NORM_EPS = 1e-6
QK_SCALE = HEAD_DIM ** -0.5

LANES = 128
ROWS = 128
GATE_PAD = LANES
PAIRS = M_HEADS // 2
SPLIT_TERMS = 3
SEL_TERMS = 2

OFF_MQ = 0
OFF_MK = OFF_MQ + M_DIM
OFF_MV = OFF_MK + M_DIM
OFF_MO = OFF_MV + M_DIM
OFF_MZ = OFF_MO + M_DIM
OFF_G = OFF_MZ + M_DIM
OFF_RQ = OFF_G + GATE_PAD
OFF_RK = OFF_RQ + R_DIM
OFF_RV = OFF_RK + R_DIM
OFF_RZ = OFF_RV + R_DIM
OFF_AQ = OFF_RZ + R_DIM
OFF_AK = OFF_AQ + A_DIM
OFF_AV = OFF_AK + A_KV_DIM
OFF_AZ = OFF_AV + A_KV_DIM
P_COLS = OFF_AZ + A_DIM
N_GATES = 2 * M_HEADS
PROJ_COL_BLOCK = 512
SCHEDULE = "FBFBFB"
ATTN_HEAD_ORDER = tuple(h for j in range(KV_GROUP) for h in (j, KV_GROUP + j))

PROMPT_ROWS = 512
DECODE_PROJ_GROUPS = 4
VMEM_LIMIT_BYTES = 56 * 1024 * 1024
DECODE_VMEM_LIMIT_BYTES = 58 * 1024 * 1024

f32 = jnp.float32
bf16 = jnp.bfloat16


def _dot(a, b):
    return jnp.dot(a, b, preferred_element_type=f32)


def _dot_nt(a, b):
    return lax.dot_general(a, b, (((1,), (1,)), ((), ())), preferred_element_type=f32)


def _sigmoid(x):
    return 1.0 / (1.0 + jnp.exp(-x))


def _silu(x):
    return x * _sigmoid(x)


def _log_sigmoid(x):
    return jnp.minimum(x, 0.0) - jnp.log(1.0 + jnp.exp(-jnp.abs(x)))


def _split_parts(x, terms):
    parts, r = [], x
    for i in range(terms):
        p = r.astype(bf16)
        parts.append(p)
        if i + 1 < terms:
            r = r - p.astype(f32)
    return parts


def _split_terms(x, terms=SPLIT_TERMS):
    return jnp.concatenate(_split_parts(x, terms), axis=1)


def _exact_tril_dot(tril3, x):
    return _dot(tril3, jnp.concatenate(_split_parts(x, SPLIT_TERMS), axis=0))


def _rope(x, cos_t, sin_t, first_half):
    up = pltpu.roll(x, LANES - HEAD_DIM // 2, axis=1)
    down = pltpu.roll(x, HEAD_DIM // 2, axis=1)
    return x * cos_t + jnp.where(first_half, up, down) * sin_t


def _rms_project(x_ref, ngain_ref, win_ref, proj_sc):
    xf = x_ref[...]
    u = xf * lax.rsqrt(jnp.mean(xf * xf, axis=1, keepdims=True) + NORM_EPS) * ngain_ref[...]
    u16 = u.astype(bf16)
    for c0 in range(0, P_COLS, PROJ_COL_BLOCK):
        c1 = min(c0 + PROJ_COL_BLOCK, P_COLS)
        proj_sc[:, c0:c1] = _dot_nt(u16, win_ref[c0:c1, :])


def _prompt_kernel(x_ref, ngain_ref, win_ref, gbias_ref, mgain_ref, rgain_ref, sinks_ref, wout_ref,
                   fgain_ref, tril3_ref, maskadd_ref, dmat2_ref, rslab_ref, ubias_ref, cos_ref,
                   sin_ref, cosoff_ref, sinoff_ref, selh_ref, selp_ref,
                   y_ref, c_out, n_out, m_out, s_out, k_out, v_out,
                   proj_sc, mix_sc, cn_sc, sb_sc, m_sc, kp_sc, vp_sc, bias_sc, *, chunks, final):
    step = pl.program_id(1)
    last_step = pl.num_programs(1) - 1

    @pl.when(jnp.logical_and(pl.program_id(0) == 0, step == 0))
    def _():
        for blk in range(A_HEADS):
            u = ubias_ref[ATTN_HEAD_ORDER[blk]:ATTN_HEAD_ORDER[blk] + 1, :]
            bias_sc[blk * ROWS:(blk + 1) * ROWS, :] = jnp.concatenate(
                [_skew(u[:, :2 * WINDOW], ROWS), _skew(u[:, 2 * WINDOW:], ROWS)], axis=1)

    xf = x_ref[...]
    u16 = (xf * lax.rsqrt(jnp.mean(xf * xf, axis=1, keepdims=True) + NORM_EPS) * ngain_ref[...]).astype(bf16)
    col_blocks = [(c0, min(c0 + PROJ_COL_BLOCK, P_COLS)) for c0 in range(0, P_COLS, PROJ_COL_BLOCK)]
    half_rows = (chunks // 2) * ROWS if chunks > 1 else chunks * ROWS

    def project(r0, r1, c0, c1):
        proj_sc[r0:r1, c0:c1] = _dot_nt(u16[r0:r1], win_ref[c0:c1, :])

    for c0, c1 in col_blocks:
        project(0, half_rows, c0, c1)
    late_pieces = [(half_rows, chunks * ROWS, c0, c1) for c0, c1 in col_blocks] if half_rows < chunks * ROWS else []

    @pl.when(step == 0)
    def _():
        cn_sc[...] = jnp.zeros_like(cn_sc)
        sb_sc[...] = jnp.zeros_like(sb_sc)
        m_sc[...] = jnp.zeros_like(m_sc)
        kp_sc[...] = jnp.zeros_like(kp_sc)
        vp_sc[...] = jnp.zeros_like(vp_sc)

    lane = lax.broadcasted_iota(jnp.int32, (ROWS, LANES), 1)
    row = lax.broadcasted_iota(jnp.int32, (ROWS, LANES), 0)
    left = lane < HEAD_DIM
    first_half = (lane & (HEAD_DIM - 1)) < (HEAD_DIM // 2)
    head_col = lane < M_HEADS
    blockdiag = (row < HEAD_DIM) == left
    row2 = lax.broadcasted_iota(jnp.int32, (ROWS, 2 * LANES), 0)
    lane2w = lax.broadcasted_iota(jnp.int32, (ROWS, 2 * LANES), 1)
    left2 = (lane2w & (LANES - 1)) < HEAD_DIM
    blockdiag2 = (row2 < HEAD_DIM) == left2
    ones16 = jnp.ones((ROWS, LANES), bf16)
    halves, pick, pair_blockdiag, half_mean = _halves, _pick, _pair_blockdiag, _half_mean

    def chunk_body(ci):
        rows = slice(ci * ROWS, (ci + 1) * ROWS)

        def proj(off, width=LANES):
            return proj_sc[rows, off:off + width]

        gates = proj(OFF_G) + gbias_ref[...]
        bcum = _exact_tril_dot(tril3_ref[...], _log_sigmoid(gates))

        base = pl.ds(step * chunks + ci, 1)
        cos_a, sin_a = cos_ref[base, :], sin_ref[base, :]
        cos_b, sin_b = cosoff_ref[...], sinoff_ref[...]
        cos_t = cos_a * cos_b - sin_a * sin_b
        sin_t = sin_a * cos_b + cos_a * sin_b
        sin_t = jnp.where(first_half, -sin_t, sin_t)
        r_q = [_rope(proj(OFF_RQ + p * LANES), cos_t, sin_t, first_half) for p in range(PAIRS)]
        r_k = [_rope(proj(OFF_RK + p * LANES), cos_t, sin_t, first_half) * QK_SCALE for p in range(PAIRS)]
        r_vbd = pair_blockdiag([proj(OFF_RV + p * LANES).astype(bf16) for p in range(PAIRS)])
        r_sb = [sb_sc[p] for p in range(PAIRS)]
        r_sc = _dot_nt(jnp.concatenate([halves(q) for q in r_q], axis=1),
                       pair_blockdiag([k.astype(bf16) for k in r_k]))
        r_inter = _dot(jnp.concatenate([q.astype(bf16) for q in r_q], axis=1),
                       pair_blockdiag([sb.astype(bf16) for sb in r_sb]))
        r_upd = _dot(jnp.concatenate([(r_k[p] * rslab_ref[1, p]).T.astype(bf16) for p in range(PAIRS)], axis=1),
                     r_vbd)
        for p in range(PAIRS):
            sb_sc[p] = rslab_ref[2, p] * r_sb[p] + jnp.where(blockdiag, r_upd[:, p * LANES:(p + 1) * LANES], 0.0)
        yield

        kcur, vcur = proj(OFF_AK), proj(OFF_AV)
        kprev, vprev = kp_sc[...], vp_sc[...]
        kk16 = jnp.concatenate([kcur, kprev], axis=0).astype(bf16)
        vv16 = jnp.concatenate([jnp.concatenate([vcur.astype(bf16), ones16], axis=1),
                                jnp.concatenate([vprev.astype(bf16), ones16], axis=1)], axis=0)
        kp_sc[...] = kcur
        vp_sc[...] = vcur
        a_q = jnp.concatenate([halves(proj(OFF_AQ + j * LANES) * QK_SCALE) for j in range(KV_GROUP)], axis=0)
        a_s = _dot_nt(a_q, kk16) + bias_sc[...]
        if ci == 0:
            pen = jnp.where(step == 0, -jnp.inf, 0.0).astype(f32)
            a_s = a_s + jnp.where(lax.broadcasted_iota(jnp.int32, (1, 2 * ROWS), 1) >= ROWS, pen, 0.0)

        m_q = [proj(OFF_MQ + p * LANES) for p in range(PAIRS)]
        m_k = [proj(OFF_MK + p * LANES) * QK_SCALE for p in range(PAIRS)]
        m_qk = _dot_nt(jnp.concatenate([halves(q) for q in m_q], axis=1),
                       pair_blockdiag([k.astype(bf16) for k in m_k]))
        m_q16 = [q.astype(bf16) for q in m_q]
        m_v16 = [proj(OFF_MV + p * LANES).astype(bf16) for p in range(PAIRS)]
        yield

        zb = pltpu.roll(bcum, LANES - M_HEADS, axis=1)
        r_mat = jnp.where(head_col, gates - zb, 0.0)
        cm = r_mat
        sh = 1
        while sh < ROWS:
            cm = jnp.where(row >= sh, jnp.maximum(cm, pltpu.roll(cm, sh, axis=0)), cm)
            sh *= 2
        mprev = m_sc[...]
        mx = jnp.maximum(mprev, cm)
        gm = mprev - mx
        em = jnp.where(head_col, -(zb + mx), 0.0)
        mx_last = jnp.broadcast_to(mx[ROWS - 1:ROWS, :], (ROWS, LANES))
        m_sc[...] = jnp.where(head_col, jnp.broadcast_to((zb + mx)[ROWS - 1:ROWS, :], (ROWS, LANES)), 0.0)
        mx_b = _dot_nt(_split_terms(mx, SEL_TERMS), selh_ref[...])
        slabs = jnp.exp(_dot_nt(_split_terms(jnp.concatenate([gm, em, r_mat - mx_last], axis=0), SEL_TERMS),
                                selp_ref[...]))
        winter_b, emt_b, ws_b = slabs[:ROWS], slabs[ROWS:2 * ROWS], slabs[2 * ROWS:]
        r_t = r_mat.T
        yield

        outs = []
        r_acc = []
        r_o = _dot((r_sc * dmat2_ref[...]).astype(bf16), r_vbd)
        for p in range(PAIRS):
            ps = slice(p * LANES, (p + 1) * LANES)
            r_acc.append(pick(r_o[:, ps]) + rslab_ref[0, p] * r_inter[:, ps])

        a_out = []
        a_p = []
        for blk in range(A_HEADS):
            s = a_s[blk * ROWS:(blk + 1) * ROWS]
            sink = sinks_ref[ATTN_HEAD_ORDER[blk]]
            m = jnp.maximum(jnp.max(jnp.maximum(s[:, :ROWS], s[:, ROWS:]), axis=1, keepdims=True), sink)
            a_p.append(jnp.exp(s - m).astype(bf16))
            a_out.append(jnp.exp(sink - m))
        a_pv = _dot(jnp.concatenate(a_p, axis=0), vv16)
        yield

        maskadd = maskadd_ref[...]
        for p in range(PAIRS):
            ps = slice(p * LANES, (p + 1) * LANES)
            cn = cn_sc[p]
            w = jnp.concatenate(
                [jnp.exp((r_t[2 * p + side:2 * p + side + 1, :] + maskadd)
                         - mx_b[:, (2 * p + side) * ROWS:(2 * p + side + 1) * ROWS]) for side in range(2)],
                axis=0) * m_qk[:, ps]
            acc = (pick(_dot(w.astype(bf16), jnp.concatenate([m_v16[p], ones16], axis=1)))
                   + jnp.concatenate([winter_b[:, ps]] * 2, axis=1) * _dot(m_q16[p], cn.astype(bf16)))
            hh = acc[:, :LANES] / jnp.maximum(jnp.abs(acc[:, LANES:]), emt_b[:, ps])
            outs.append(_sigmoid(proj(OFF_MO + p * LANES)) * hh)
            kwt16 = (m_k[p] * ws_b[:, ps]).T.astype(bf16)
            dcn = _dot(kwt16, jnp.concatenate([m_v16[p], ones16], axis=1))
            decay = winter_b[ROWS - 1:ROWS, ps]
            cn_sc[p] = jnp.concatenate([decay, decay], axis=1) * cn + jnp.where(blockdiag2, dcn, 0.0)
        outs.extend(r_acc)
        yield

        x4 = jnp.concatenate(outs, axis=0)
        xc = x4 - half_mean(x4)
        y4 = xc * lax.rsqrt(half_mean(xc * xc) + NORM_EPS)
        for i in range(2 * PAIRS):
            gain = (mgain_ref if i < PAIRS else rgain_ref)[:, (i % PAIRS) * LANES:(i % PAIRS + 1) * LANES]
            zoff = (OFF_MZ if i < PAIRS else OFF_RZ) + (i % PAIRS) * LANES
            out = y4[i * ROWS:(i + 1) * ROWS] * gain * _silu(proj(zoff))
            mix_sc[rows, i * LANES:(i + 1) * LANES] = out.astype(bf16)
        for j in range(KV_GROUP):
            acc = pick(a_pv[2 * j * ROWS:(2 * j + 2) * ROWS])
            den = acc[:, LANES:] + jnp.where(left, a_out[2 * j], a_out[2 * j + 1])
            out = (acc[:, :LANES] / den) * _silu(proj(OFF_AZ + j * LANES))
            mix_sc[rows, M_DIM + R_DIM + j * LANES:M_DIM + R_DIM + (j + 1) * LANES] = out.astype(bf16)

    def out_project(r0, r1, c0, c1):
        y_ref[r0:r1, c0:c1] = x_ref[r0:r1, c0:c1] + _dot(mix_sc[r0:r1, :], wout_ref[:, c0:c1])

    out_blocks = [(c0, min(c0 + PROJ_COL_BLOCK, D_MODEL)) for c0 in range(0, D_MODEL, PROJ_COL_BLOCK)]
    early_out = [(0, half_rows, c0, c1) for c0, c1 in out_blocks] if half_rows < chunks * ROWS else []
    final_out = [(half_rows if early_out else 0, chunks * ROWS, c0, c1) for c0, c1 in out_blocks]

    def fill_mxu(ci):
        if late_pieces:
            project(*late_pieces.pop(0))
        elif early_out and (ci - 1) * ROWS >= half_rows:
            out_project(*early_out.pop(0))

    parts = [chunk_body(ci) for ci in range(chunks)]
    for ci in range(chunks + 1):
        if ci * ROWS >= half_rows:
            while late_pieces:
                project(*late_pieces.pop(0))
        for which in SCHEDULE:
            if which == "F" and ci < chunks:
                next(parts[ci])
                fill_mxu(ci)
            if which == "B" and ci > 0:
                next(parts[ci - 1], None)
                fill_mxu(ci)
    for piece in early_out + final_out:
        out_project(*piece)
    if final:
        y = y_ref[...]
        y_ref[...] = y * lax.rsqrt(jnp.mean(y * y, axis=1, keepdims=True) + NORM_EPS) * fgain_ref[...]

    @pl.when(step == last_step)
    def _():
        for p in range(PAIRS):
            cn = cn_sc[p]
            sb = sb_sc[p]
            n_t = cn[:, LANES:].T
            for side in range(2):
                h = 2 * p + side
                blk = slice(side * HEAD_DIM, (side + 1) * HEAD_DIM)
                c_out[0, h] = cn[blk, blk]
                s_out[0, h] = sb[blk, blk]
                n_out[0, h:h + 1, :] = n_t[side * HEAD_DIM:side * HEAD_DIM + 1, blk]
        m_out[0] = m_sc[...]
        k_out[0] = kp_sc[...]
        v_out[0] = vp_sc[...]


def _lane_is_left(shape):
    return (lax.broadcasted_iota(jnp.int32, shape, 1) & (LANES - 1)) < HEAD_DIM


def _halves(x):
    left = _lane_is_left(x.shape)
    return jnp.concatenate([jnp.where(left, x, 0.0), jnp.where(left, 0.0, x)], axis=0).astype(bf16)


def _pick(x):
    return jnp.where(_lane_is_left((ROWS, x.shape[1])), x[:ROWS], x[ROWS:])


def _pair_blockdiag(blocks):
    z = jnp.zeros_like(blocks[0])
    return jnp.concatenate(
        [jnp.concatenate([blk if j == i else z for j in range(len(blocks))], axis=1)
         for i, blk in enumerate(blocks)], axis=0)


def _half_mean(x):
    left = _lane_is_left(x.shape)
    s_left = jnp.sum(jnp.where(left, x, 0.0), axis=1, keepdims=True)
    s_right = jnp.sum(jnp.where(left, 0.0, x), axis=1, keepdims=True)
    return jnp.where(left, s_left, s_right) * (1.0 / HEAD_DIM)


def _group_last(x, groups):
    n = x.shape[1]
    glen = ROWS // groups
    x3 = x.reshape(groups, glen, n)
    return jnp.broadcast_to(x3[:, glen - 1:glen, :], (groups, glen, n)).reshape(ROWS, n)


def _sample_kernel(x_ref, ngain_ref, win_ref, gbias_ref, mgain_ref, rgain_ref, sinks_ref, wout_ref,
                   fgain_ref, tril3_ref, maskadd_ref, dmat2_ref, rslab_ref, ubias_ref, cos_ref, sin_ref,
                   selh_ref, selp_ref,
                   c_in, n_in, m_in, s_in, k_in, v_in,
                   y_ref, c_out, n_out, m_out, s_out, k_out, v_out,
                   proj_sc, mix_sc, xcur_sc, xall_sc, qb_sc, sp_sc, pp_sc, ob_sc, biasc_ref, biasp_ref, *, groups,
                   proj_groups, ret_full):
    glen = ROWS // groups
    glen_log2 = glen.bit_length() - 1
    hd_log2 = HEAD_DIM.bit_length() - 1
    layer = pl.program_id(0)
    step = pl.program_id(1)
    last_layer = pl.num_programs(0) - 1
    xrows = pl.ds(pl.multiple_of(step * ROWS, ROWS), ROWS)
    sub = lax.rem(step, proj_groups)
    rows = pl.ds(pl.multiple_of(sub * ROWS, ROWS), ROWS)
    wide = pl.ds(pl.multiple_of((step - sub) * ROWS, ROWS), proj_groups * ROWS)

    @pl.when(jnp.logical_and(sub == 0, layer == 0))
    def _():
        _rms_project(x_ref, ngain_ref, win_ref, proj_sc)

    @pl.when(jnp.logical_and(sub == 0, layer != 0))
    def _():
        _rms_project(xall_sc.at[wide, :], ngain_ref, win_ref, proj_sc)

    @pl.when(layer == 0)
    def _():
        xcur_sc[...] = x_ref[rows, :]

    @pl.when(layer != 0)
    def _():
        xcur_sc[...] = xall_sc[xrows, :]

    for b in range(groups):
        k_out[b] = pltpu.roll(k_in[b], WINDOW - glen, axis=1)
        v_out[b] = pltpu.roll(v_in[b], WINDOW - glen, axis=1)

    @pl.when(jnp.logical_and(step == 0, layer == 0))
    def _():
        for h in range(A_HEADS):
            u = ubias_ref[h:h + 1, :]
            biasc_ref[h] = _skew(u[:, :2 * WINDOW], ROWS) + maskadd_ref[...]
            biasp_ref[h] = jnp.concatenate([_skew(u[:, 2 * WINDOW:], glen)] * groups, axis=0)

    lane = lax.broadcasted_iota(jnp.int32, (ROWS, LANES), 1)
    first_half = (lane & (HEAD_DIM - 1)) < (HEAD_DIM // 2)

    r_i = lax.broadcasted_iota(jnp.int32, (ROWS, groups * HEAD_DIM), 0)
    c_i = lax.broadcasted_iota(jnp.int32, (ROWS, groups * HEAD_DIM), 1)
    blk = (r_i >> glen_log2) == (c_i >> hd_log2)
    r_t = lax.broadcasted_iota(jnp.int32, (groups * HEAD_DIM, ROWS), 0)
    c_t = lax.broadcasted_iota(jnp.int32, (groups * HEAD_DIM, ROWS), 1)
    blk_t = (r_t >> hd_log2) == (c_t >> glen_log2)

    def q_times_state(qh, st):
        qt = jnp.where(blk, jnp.concatenate([qh] * groups, axis=1), 0.0)
        return _dot(qt.astype(bf16), st.astype(bf16))

    def state_increment(kt_h, vh16):
        kt = jnp.where(blk_t, jnp.concatenate([kt_h] * groups, axis=0), 0.0)
        return _dot(kt.astype(bf16), vh16)

    def proj(off, width=LANES):
        return proj_sc[rows, off:off + width]

    def head_state_rows(slab, side):
        wide = slab[:, side * HEAD_DIM:(side + 1) * HEAD_DIM].reshape(groups, glen, HEAD_DIM)[:, 0:1, :]
        rows_ = jnp.broadcast_to(wide, (groups, HEAD_DIM, HEAD_DIM)).reshape(groups * HEAD_DIM, HEAD_DIM)
        return rows_, wide.reshape(groups, HEAD_DIM)

    row = lax.broadcasted_iota(jnp.int32, (ROWS, LANES), 0)
    tau = row & (glen - 1)
    head_col = lane < M_HEADS
    left = lane < HEAD_DIM
    ones16 = jnp.ones((ROWS, LANES), bf16)
    gates = proj(OFF_G) + gbias_ref[...]
    bcum = _exact_tril_dot(tril3_ref[...], _log_sigmoid(gates))
    zb = pltpu.roll(bcum, LANES - M_HEADS, axis=1)
    r_mat = jnp.where(head_col, gates - zb, 0.0)
    cm = r_mat
    sh = 1
    while sh < glen:
        cm = jnp.where(tau >= sh, jnp.maximum(cm, pltpu.roll(cm, sh, axis=0)), cm)
        sh *= 2
    mprev = m_in[...]
    mx = jnp.maximum(mprev, cm)
    gm = mprev - mx
    em = jnp.where(head_col, -(zb + mx), 0.0)
    mx_last = _group_last(mx, groups)
    m_out[...] = jnp.where(head_col, _group_last(zb + mx, groups), 0.0)
    mx_b = _dot_nt(_split_terms(mx, SEL_TERMS), selh_ref[...])
    slabs = jnp.exp(_dot_nt(_split_terms(jnp.concatenate([gm, em, r_mat - mx_last], axis=0), SEL_TERMS), selp_ref[...]))
    winter_b, emt_b, ws_b = slabs[:ROWS], slabs[ROWS:2 * ROWS], slabs[2 * ROWS:]
    decay_b = _group_last(winter_b, groups)
    r_t = r_mat.T
    maskadd = maskadd_ref[...]

    m_q = [proj(OFF_MQ + p * LANES) for p in range(PAIRS)]
    m_k = [proj(OFF_MK + p * LANES) * QK_SCALE for p in range(PAIRS)]
    m_v = [proj(OFF_MV + p * LANES) for p in range(PAIRS)]
    m_qk = _dot_nt(jnp.concatenate([_halves(q) for q in m_q], axis=1),
                   _pair_blockdiag([k.astype(bf16) for k in m_k]))
    outs = []
    for p in range(PAIRS):
        ps = slice(p * LANES, (p + 1) * LANES)
        w = jnp.concatenate(
            [jnp.exp((r_t[2 * p + side:2 * p + side + 1, :] + maskadd)
                     - mx_b[:, (2 * p + side) * ROWS:(2 * p + side + 1) * ROWS]) for side in range(2)],
            axis=0) * m_qk[:, ps]
        intra = _pick(_dot(w.astype(bf16), jnp.concatenate([m_v[p].astype(bf16), ones16], axis=1)))
        kw = m_k[p] * ws_b[:, ps]
        kwt = kw.T
        q_c, q_n = [], []
        for side in range(2):
            h = 2 * p + side
            hs = slice(side * HEAD_DIM, (side + 1) * HEAD_DIM)
            qh = m_q[p][:, hs]
            c_h = c_in[:, h].reshape(groups * HEAD_DIM, HEAD_DIM)
            n_g = n_in[h]
            n_rows = jnp.broadcast_to(n_g.reshape(groups, 1, HEAD_DIM),
                                      (groups, glen, HEAD_DIM)).reshape(ROWS, HEAD_DIM)
            q_c.append(q_times_state(qh, c_h))
            q_n.append(jnp.sum(qh * n_rows, axis=1, keepdims=True))
            dec_rows, dec_g = head_state_rows(decay_b[:, ps], side)
            c_new = dec_rows * c_h + state_increment(kwt[hs, :], m_v[p][:, hs].astype(bf16))
            c_out[:, h] = c_new.reshape(groups, HEAD_DIM, HEAD_DIM)
            n_out[h] = dec_g * n_g + jnp.sum(kw[:, hs].reshape(groups, glen, HEAD_DIM), axis=1)
        wb = winter_b[:, ps]
        num = intra[:, :LANES] + wb * jnp.concatenate(q_c, axis=1)
        nq = intra[:, LANES:] + wb * jnp.where(left, q_n[0], q_n[1])
        outs.append(_sigmoid(proj(OFF_MO + p * LANES)) * (num / jnp.maximum(jnp.abs(nq), emt_b[:, ps])))

    cos_t, sin_t = cos_ref[...], sin_ref[...]
    r_q = [_rope(proj(OFF_RQ + p * LANES), cos_t, sin_t, first_half) for p in range(PAIRS)]
    r_k = [_rope(proj(OFF_RK + p * LANES), cos_t, sin_t, first_half) * QK_SCALE for p in range(PAIRS)]
    r_v = [proj(OFF_RV + p * LANES) for p in range(PAIRS)]
    r_sc = _dot_nt(jnp.concatenate([_halves(q) for q in r_q], axis=1),
                   _pair_blockdiag([k.astype(bf16) for k in r_k]))
    r_o = _dot((r_sc * dmat2_ref[...]).astype(bf16), _pair_blockdiag([v.astype(bf16) for v in r_v]))
    for p in range(PAIRS):
        ps = slice(p * LANES, (p + 1) * LANES)
        rkt = (r_k[p] * rslab_ref[1, p]).T
        q_s = []
        for side in range(2):
            h = 2 * p + side
            hs = slice(side * HEAD_DIM, (side + 1) * HEAD_DIM)
            s_h = s_in[:, h].reshape(groups * HEAD_DIM, HEAD_DIM)
            q_s.append(q_times_state(r_q[p][:, hs], s_h))
            s_new = ret_full[h] * s_h + state_increment(rkt[hs, :], r_v[p][:, hs].astype(bf16))
            s_out[:, h] = s_new.reshape(groups, HEAD_DIM, HEAD_DIM)
        outs.append(_pick(r_o[:, ps]) + rslab_ref[0, p] * jnp.concatenate(q_s, axis=1))

    x4 = jnp.concatenate(outs, axis=0)
    xc = x4 - _half_mean(x4)
    y4 = xc * lax.rsqrt(_half_mean(xc * xc) + NORM_EPS)
    for i in range(2 * PAIRS):
        gain = (mgain_ref if i < PAIRS else rgain_ref)[:, (i % PAIRS) * LANES:(i % PAIRS + 1) * LANES]
        zoff = (OFF_MZ if i < PAIRS else OFF_RZ) + (i % PAIRS) * LANES
        mix_sc[:, i * LANES:(i + 1) * LANES] = (y4[i * ROWS:(i + 1) * ROWS] * gain * _silu(proj(zoff))).astype(bf16)

    kcur = proj_sc[rows, OFF_AK:OFF_AK + A_KV_DIM]
    vcur = proj_sc[rows, OFF_AV:OFF_AV + A_KV_DIM]
    kcur16, vcur16 = kcur.astype(bf16), vcur.astype(bf16)

    a_q = {}
    for j in range(KV_GROUP):
        hq = _halves(proj(OFF_AQ + j * LANES) * QK_SCALE)
        a_q[j], a_q[KV_GROUP + j] = hq[:ROWS], hq[ROWS:]
    for h in range(A_HEADS):
        qb_sc[:, h * glen:(h + 1) * glen, :] = a_q[h].reshape(groups, glen, LANES)
    sc_all = _dot_nt(jnp.concatenate([a_q[h] for h in range(A_HEADS)], axis=0), kcur16)

    for b in range(groups):
        sp = _dot(qb_sc[b], k_in[b].astype(bf16))
        sp_sc[:, b * glen:(b + 1) * glen, :] = sp.reshape(A_HEADS, glen, WINDOW)

    vcur1 = jnp.concatenate([vcur16, ones16], axis=1)
    esink, o_cur = [], []
    for h in range(A_HEADS):
        sc = sc_all[h * ROWS:(h + 1) * ROWS] + biasc_ref[h]
        sp = sp_sc[h] + biasp_ref[h]
        sink = sinks_ref[layer, h]
        m = jnp.maximum(jnp.max(jnp.maximum(sc, sp), axis=1, keepdims=True), sink)
        esink.append(jnp.exp(sink - m))
        o_cur.append(_dot(jnp.exp(sc - m).astype(bf16), vcur1))
        pp_sc[:, h * glen:(h + 1) * glen, :] = jnp.exp(sp - m).reshape(groups, glen, WINDOW).astype(bf16)

    for b in range(groups):
        v1 = jnp.concatenate([v_in[b].astype(bf16), ones16], axis=0)
        ob = _dot_nt(pp_sc[b], v1)
        ob_sc[:, b * glen:(b + 1) * glen, :] = ob.reshape(A_HEADS, glen, 2 * LANES)
    norm = []
    for h in range(A_HEADS):
        acc = o_cur[h] + ob_sc[h]
        norm.append(acc[:, :LANES] / (acc[:, LANES:] + esink[h]))
    outs = [jnp.where(left, norm[j], norm[KV_GROUP + j]) for j in range(KV_GROUP)]
    kcur_t, vcur_t = kcur.T, vcur.T
    new = slice(WINDOW - glen, WINDOW)
    for b in range(groups):
        shift = (WINDOW - glen - b * glen) % LANES
        k_out[b, :, new] = pltpu.roll(kcur_t, shift, axis=1)[:, new]
        v_out[b, :, new] = pltpu.roll(vcur_t, shift, axis=1)[:, new]
    out_a = jnp.concatenate(outs, axis=1) * _silu(proj_sc[rows, OFF_AZ:OFF_AZ + A_DIM])
    mix_sc[:, M_DIM + R_DIM:M_DIM + R_DIM + A_DIM] = out_a.astype(bf16)

    y = xcur_sc[...] + _dot(mix_sc[...], wout_ref[...])

    @pl.when(layer != last_layer)
    def _():
        xall_sc[xrows, :] = y
        y_ref[...] = y

    @pl.when(layer == last_layer)
    def _():
        y_ref[...] = y * lax.rsqrt(jnp.mean(y * y, axis=1, keepdims=True) + NORM_EPS) * fgain_ref[...]


def _t5_bucket(dist):
    max_exact = N_BUCKETS // 2
    d = np.maximum(dist, 1).astype(np.float32)
    large = max_exact + (np.log(d / max_exact) / np.log(REL_MAX_DIST / max_exact)
                         * (N_BUCKETS - max_exact)).astype(np.int32)
    large = np.minimum(large, N_BUCKETS - 1)
    return np.where(dist < max_exact, dist, large).astype(np.int32)


def _static_tables(groups):
    glen = ROWS // groups
    r = np.arange(ROWS)
    grp, tau = r // glen, r % glen
    causal = (grp[:, None] == grp[None, :]) & (tau[None, :] <= tau[:, None])
    tril = causal.astype(np.float32)
    maskadd = np.where(causal, 0.0, -np.inf).astype(np.float32)
    log_g = np.log1p(-np.exp2(-5.0 - np.arange(R_HEADS, dtype=np.float64)))
    diff = (tau[:, None] - tau[None, :]).astype(np.float64)
    dmat = np.where(causal[None], np.exp(log_g[:, None, None] * np.maximum(diff, 0.0)[None]), 0.0)
    inter = np.exp(log_g[None, :] * (tau[:, None] + 1.0))
    tail = np.exp(log_g[None, :] * (glen - 1.0 - tau[:, None]))
    full = np.exp(log_g * glen)
    lane_head = np.arange(LANES) // HEAD_DIM
    rslab = np.zeros((3, PAIRS, ROWS, LANES), np.float64)
    for p in range(PAIRS):
        rslab[0, p] = inter[:, 2 * p + lane_head]
        rslab[1, p] = tail[:, 2 * p + lane_head]
        rslab[2, p] = full[2 * p + lane_head][None, :]
    selh = np.zeros((M_HEADS * ROWS, SEL_TERMS * LANES), np.float32)
    selp = np.zeros((PAIRS * LANES, SEL_TERMS * LANES), np.float32)
    for t in range(SEL_TERMS):
        for h in range(M_HEADS):
            selh[h * ROWS:(h + 1) * ROWS, t * LANES + h] = 1.0
        for p in range(PAIRS):
            for side in range(2):
                selp[p * LANES + side * HEAD_DIM:p * LANES + (side + 1) * HEAD_DIM, t * LANES + 2 * p + side] = 1.0
    return dict(tril3=jnp.asarray(np.concatenate([tril] * SPLIT_TERMS, axis=1), bf16),
                maskadd=maskadd,
                dmat2=np.concatenate(list(dmat.astype(np.float32).reshape(PAIRS, 2 * ROWS, ROWS)), axis=1),
                rslab=rslab.astype(np.float32),
                full=tuple(float(v) for v in full),
                selh=jnp.asarray(selh, bf16), selp=jnp.asarray(selp, bf16))


def _bias_vectors(rel_table):
    tb = jnp.transpose(rel_table[_t5_bucket(np.arange(WINDOW))]).astype(f32)
    ninf = jnp.full((A_HEADS, WINDOW), -jnp.inf, f32)
    rev = tb[:, :0:-1]
    return jnp.concatenate([tb[:, :1], ninf, rev, ninf[:, :1], rev, ninf], axis=1)


def _skew(u_row, rows):
    x = jnp.broadcast_to(u_row, (rows, 2 * WINDOW))
    return pltpu.roll(x, 0, 1, stride=1, stride_axis=0)[:, :WINDOW]


def _rope_tables(pos, signed=True):
    half = HEAD_DIM // 2
    inv = ROPE_BASE ** (-jnp.arange(half, dtype=f32) / half)
    ang = pos.astype(f32)[:, None] * inv[None, :]
    cos, sin = jnp.cos(ang), jnp.sin(ang)
    reps = LANES // HEAD_DIM
    cos_t = jnp.tile(jnp.concatenate([cos, cos], axis=1), (1, reps))
    sin_t = jnp.tile(jnp.concatenate([-sin if signed else sin, sin], axis=1), (1, reps))
    return cos_t, sin_t


def _const_spec(shape, nargs):
    zeros = (0,) * len(shape)
    if nargs == 1:
        return pl.BlockSpec(shape, lambda i: zeros)
    return pl.BlockSpec(shape, lambda i, j: zeros)


def _layer_spec(shape, layer, nargs):
    idx = (layer,) + (0,) * len(shape)
    if nargs == 1:
        return pl.BlockSpec((None,) + shape, lambda i: idx)
    return pl.BlockSpec((None,) + shape, lambda i, j: idx)


def _param_specs(layer, nargs):
    ls = functools.partial(_layer_spec, layer=layer, nargs=nargs)
    return [ls((1, D_MODEL)), ls((P_COLS, D_MODEL)), ls((1, LANES)), ls((1, M_DIM)), ls((1, R_DIM)),
            pl.BlockSpec(memory_space=pltpu.SMEM), ls((D_MODEL, D_MODEL)), _const_spec((1, D_MODEL), nargs)]


def _param_args(p, layer):
    return (p["norm_gain"], p["w_in"], p["gbias"], p["m_gain"], p["r_gain"], p["sinks"][layer], p["w_out"],
            p["fgain"])


def _prompt_layer(x, p, layer, tabs, ubias, rope, final):
    B, T, _ = x.shape
    tb = min(PROMPT_ROWS, T)
    chunks = tb // ROWS
    nt = T // tb
    cs = functools.partial(_const_spec, nargs=2)
    in_specs = [pl.BlockSpec((None, tb, D_MODEL), lambda b, t: (b, t, 0))] + _param_specs(layer, 2) + [
        cs((ROWS, SPLIT_TERMS * ROWS)), cs((ROWS, ROWS)), cs((2 * ROWS, PAIRS * ROWS)),
        cs((3, PAIRS, ROWS, LANES)), cs((A_HEADS, 4 * WINDOW)),
        cs((T // ROWS, LANES)), cs((T // ROWS, LANES)), cs((ROWS, LANES)), cs((ROWS, LANES)),
        cs((M_HEADS * ROWS, SEL_TERMS * LANES)), cs((PAIRS * LANES, SEL_TERMS * LANES)),
    ]
    out_shape = (
        jax.ShapeDtypeStruct((B, T, D_MODEL), f32),
        jax.ShapeDtypeStruct((B, M_HEADS, HEAD_DIM, HEAD_DIM), f32),
        jax.ShapeDtypeStruct((B, M_HEADS, HEAD_DIM), f32),
        jax.ShapeDtypeStruct((B, ROWS, LANES), f32),
        jax.ShapeDtypeStruct((B, R_HEADS, HEAD_DIM, HEAD_DIM), f32),
        jax.ShapeDtypeStruct((B, WINDOW, A_KV_DIM), f32),
        jax.ShapeDtypeStruct((B, WINDOW, A_KV_DIM), f32),
    )
    out_specs = (
        pl.BlockSpec((None, tb, D_MODEL), lambda b, t: (b, t, 0)),
        pl.BlockSpec((1, M_HEADS, HEAD_DIM, HEAD_DIM), lambda b, t: (b, 0, 0, 0)),
        pl.BlockSpec((1, M_HEADS, HEAD_DIM), lambda b, t: (b, 0, 0)),
        pl.BlockSpec((1, ROWS, LANES), lambda b, t: (b, 0, 0)),
        pl.BlockSpec((1, R_HEADS, HEAD_DIM, HEAD_DIM), lambda b, t: (b, 0, 0, 0)),
        pl.BlockSpec((1, WINDOW, A_KV_DIM), lambda b, t: (b, 0, 0)),
        pl.BlockSpec((1, WINDOW, A_KV_DIM), lambda b, t: (b, 0, 0)),
    )
    kern = functools.partial(_prompt_kernel, chunks=chunks, final=final)
    y, c, n, m, s, k, v = pl.pallas_call(
        kern, grid=(B, nt), in_specs=in_specs, out_specs=out_specs, out_shape=out_shape,
        scratch_shapes=[pltpu.VMEM((tb, P_COLS), f32), pltpu.VMEM((tb, D_MODEL), bf16),
                        pltpu.VMEM((PAIRS, ROWS, 2 * LANES), f32), pltpu.VMEM((PAIRS, ROWS, LANES), f32),
                        pltpu.VMEM((ROWS, LANES), f32), pltpu.VMEM((ROWS, A_KV_DIM), f32),
                        pltpu.VMEM((ROWS, A_KV_DIM), f32), pltpu.VMEM((A_HEADS * ROWS, 2 * WINDOW), f32)],
        compiler_params=pltpu.CompilerParams(dimension_semantics=("arbitrary", "arbitrary"),
                                             vmem_limit_bytes=VMEM_LIMIT_BYTES),
        name="prompt_layer",
    )(x, *_param_args(p, layer), tabs["tril3"], tabs["maskadd"], tabs["dmat2"], tabs["rslab"], ubias,
      *rope, tabs["selh"], tabs["selp"])
    k = k.reshape(B, WINDOW, A_KV_HEADS, HEAD_DIM)
    v = v.reshape(B, WINDOW, A_KV_HEADS, HEAD_DIM)
    return y, c, n, m[:, 0, :M_HEADS], s, k, v


def _batch_major_kernel(x_ref, o_ref):
    o_ref[...] = jnp.transpose(x_ref[...], (2, 0, 1))


def _batch_major(state):
    depth, B, heads, d, e = state.shape
    view = jnp.transpose(state, (0, 2, 3, 4, 1))
    return pl.pallas_call(
        _batch_major_kernel, grid=(depth, heads),
        in_specs=[pl.BlockSpec((None, None, d, e, B), lambda l, h: (l, h, 0, 0, 0))],
        out_specs=pl.BlockSpec((None, B, None, d, e), lambda l, h: (l, 0, h, 0, 0)),
        out_shape=jax.ShapeDtypeStruct(state.shape, state.dtype),
        compiler_params=pltpu.CompilerParams(dimension_semantics=("arbitrary", "arbitrary")),
        name="batch_major",
    )(view)


def _sample_path(x, states, p, tabs, ubias, cos_t, sin_t):
    B, T, _ = x.shape
    groups = ROWS // T
    nb = B // groups
    pg = math.gcd(nb, DECODE_PROJ_GROUPS)
    c0, n0, m0, s0, k0, v0 = states
    depth = c0.shape[0]
    c0 = _batch_major(c0)
    s0 = _batch_major(s0)
    x2 = x.reshape(B * T, D_MODEL)
    n0t = jnp.transpose(n0, (0, 2, 1, 3))
    m0r = jnp.pad(jnp.repeat(m0, T, axis=1), ((0, 0), (0, 0), (0, LANES - M_HEADS)))
    k0r = jnp.transpose(k0, (0, 1, 3, 4, 2)).reshape(depth, B, A_KV_DIM, WINDOW)
    v0r = jnp.transpose(v0, (0, 1, 3, 4, 2)).reshape(depth, B, A_KV_DIM, WINDOW)

    def cs(shape):
        zeros = (0,) * len(shape)
        return pl.BlockSpec(shape, lambda l, i: zeros)

    def per_layer(shape, buffers=None):
        zeros = (0,) * len(shape)
        mode = {} if buffers is None else dict(pipeline_mode=pl.Buffered(buffers))
        return pl.BlockSpec((None,) + shape, lambda l, i: (l,) + zeros, **mode)

    st4 = pl.BlockSpec((None, groups, M_HEADS, HEAD_DIM, HEAD_DIM), lambda l, i: (l, i, 0, 0, 0))
    stn = pl.BlockSpec((None, M_HEADS, groups, HEAD_DIM), lambda l, i: (l, 0, i, 0))
    stm = pl.BlockSpec((None, ROWS, LANES), lambda l, i: (l, i, 0))
    stk = pl.BlockSpec((None, groups, A_KV_DIM, WINDOW), lambda l, i: (l, i, 0, 0))
    in_specs = [
        pl.BlockSpec((pg * ROWS, D_MODEL), lambda l, i: (i // pg, 0)),
        per_layer((1, D_MODEL)), per_layer((P_COLS, D_MODEL), 1), per_layer((1, LANES)), per_layer((1, M_DIM)),
        per_layer((1, R_DIM)), pl.BlockSpec(memory_space=pltpu.SMEM), per_layer((D_MODEL, D_MODEL), 1),
        cs((1, D_MODEL)),
        cs((ROWS, SPLIT_TERMS * ROWS)), cs((ROWS, ROWS)), cs((2 * ROWS, PAIRS * ROWS)),
        cs((3, PAIRS, ROWS, LANES)), cs((A_HEADS, 4 * WINDOW)),
        cs((ROWS, LANES)), cs((ROWS, LANES)),
        cs((M_HEADS * ROWS, SEL_TERMS * LANES)), cs((PAIRS * LANES, SEL_TERMS * LANES)),
        st4, stn, stm, st4, stk, stk,
    ]
    out_shape = (
        jax.ShapeDtypeStruct((B * T, D_MODEL), f32),
        jax.ShapeDtypeStruct((depth, B, M_HEADS, HEAD_DIM, HEAD_DIM), f32),
        jax.ShapeDtypeStruct((depth, M_HEADS, B, HEAD_DIM), f32),
        jax.ShapeDtypeStruct((depth, B * T, LANES), f32),
        jax.ShapeDtypeStruct((depth, B, R_HEADS, HEAD_DIM, HEAD_DIM), f32),
        jax.ShapeDtypeStruct((depth, B, A_KV_DIM, WINDOW), f32),
        jax.ShapeDtypeStruct((depth, B, A_KV_DIM, WINDOW), f32),
    )
    y_spec = pl.BlockSpec((ROWS, D_MODEL), lambda l, i: (jnp.where(l == depth - 1, i, 0), 0))
    out_specs = (y_spec, st4, stn, stm, st4, stk, stk)
    kern = functools.partial(_sample_kernel, groups=groups, proj_groups=pg, ret_full=tabs["full"])
    y, c, n, m, s, k, v = pl.pallas_call(
        kern, grid=(depth, nb), in_specs=in_specs, out_specs=out_specs, out_shape=out_shape,
        scratch_shapes=[pltpu.VMEM((pg * ROWS, P_COLS), f32), pltpu.VMEM((ROWS, D_MODEL), bf16),
                        pltpu.VMEM((ROWS, D_MODEL), f32), pltpu.VMEM((B * T, D_MODEL), f32),
                        pltpu.VMEM((groups, A_HEADS * T, A_KV_DIM), bf16),
                        pltpu.VMEM((A_HEADS, ROWS, WINDOW), f32),
                        pltpu.VMEM((groups, A_HEADS * T, WINDOW), bf16),
                        pltpu.VMEM((A_HEADS, ROWS, 2 * LANES), f32),
                        pltpu.VMEM((A_HEADS, ROWS, ROWS), f32), pltpu.VMEM((A_HEADS, ROWS, WINDOW), f32)],
        compiler_params=pltpu.CompilerParams(dimension_semantics=("arbitrary", "arbitrary"),
                                             vmem_limit_bytes=DECODE_VMEM_LIMIT_BYTES),
        name="sample_path",
    )(x2, p["norm_gain"], p["w_in"], p["gbias"], p["m_gain"], p["r_gain"], p["sinks"], p["w_out"], p["fgain"],
      tabs["tril3"], tabs["maskadd"], tabs["dmat2"], tabs["rslab"], ubias,
      cos_t, sin_t, tabs["selh"], tabs["selp"], c0, n0t, m0r, s0, k0r, v0r)
    y = y.reshape(B, T, D_MODEL)
    n = jnp.transpose(n, (0, 2, 1, 3))
    m = m.reshape(depth, B, T, LANES)[:, :, 0, :M_HEADS]
    k = jnp.transpose(k.reshape(depth, B, A_KV_HEADS, HEAD_DIM, WINDOW), (0, 1, 4, 2, 3))
    v = jnp.transpose(v.reshape(depth, B, A_KV_HEADS, HEAD_DIM, WINDOW), (0, 1, 4, 2, 3))
    return y, c, n, m, s, k, v


def _prepare_params(norm_gain, w_in, mlstm_gate_bias, mlstm_norm_gain, ret_norm_gain, attn_sinks, w_out,
                    final_norm_gain):
    depth = w_in.shape[0]
    w_t = jnp.swapaxes(w_in, 1, 2)
    split = OFF_G + N_GATES
    aq0 = split + 4 * R_DIM
    akv0 = aq0 + A_DIM
    az0 = akv0 + 2 * A_KV_DIM

    def by_head(w):
        w = w.reshape(depth, A_KV_HEADS, KV_GROUP, HEAD_DIM, D_MODEL)
        return jnp.transpose(w, (0, 2, 1, 3, 4)).reshape(depth, A_DIM, D_MODEL)

    w_in_p = jnp.concatenate(
        [w_t[:, :split], jnp.zeros((depth, GATE_PAD - N_GATES, D_MODEL), w_t.dtype), w_t[:, split:aq0],
         by_head(w_t[:, aq0:akv0]), w_t[:, akv0:az0], by_head(w_t[:, az0:])], axis=1).astype(bf16)
    wo16 = w_out.astype(bf16)
    a0 = M_DIM + R_DIM
    w_out_p = jnp.concatenate([wo16[:, :a0, :], by_head(wo16[:, a0:, :])], axis=1)
    gbias = jnp.pad(mlstm_gate_bias.reshape(depth, 1, N_GATES), ((0, 0), (0, 0), (0, LANES - N_GATES)))
    return dict(norm_gain=norm_gain.reshape(depth, 1, D_MODEL), w_in=w_in_p, gbias=gbias,
                m_gain=mlstm_norm_gain.reshape(depth, 1, M_DIM), r_gain=ret_norm_gain.reshape(depth, 1, R_DIM),
                sinks=attn_sinks, w_out=w_out_p, fgain=final_norm_gain.reshape(1, D_MODEL))


def kernel(x_prompt, x_sample, state_mlstm_C, state_mlstm_n, state_mlstm_m, state_ret_S, cache_win_k,
           cache_win_v, norm_gain, w_in, mlstm_gate_bias, mlstm_norm_gain, ret_norm_gain, attn_sinks,
           rel_bias_table, w_out, final_norm_gain):
    depth = w_in.shape[0]
    seq = x_prompt.shape[1]
    dec_seq = x_sample.shape[1]
    past_len = seq
    p = _prepare_params(norm_gain, w_in, mlstm_gate_bias, mlstm_norm_gain, ret_norm_gain, attn_sinks, w_out,
                        final_norm_gain)
    tabs_p = _static_tables(1)
    tabs_s = _static_tables(ROWS // dec_seq)
    ubias = _bias_vectors(rel_bias_table)
    rope_p = (*_rope_tables(jnp.arange(0, seq, ROWS, dtype=jnp.int32), signed=False),
              *_rope_tables(jnp.arange(ROWS, dtype=jnp.int32), signed=False))
    cos_s, sin_s = _rope_tables(past_len + (jnp.arange(ROWS, dtype=jnp.int32) % dec_seq))
    states = (state_mlstm_C, state_mlstm_n, state_mlstm_m, state_ret_S, cache_win_k, cache_win_v)

    xp = x_prompt
    p_states = []
    for layer in range(depth):
        xp, *sp = _prompt_layer(xp, p, layer, tabs_p, ubias, rope_p, layer == depth - 1)
        p_states.append(sp)
    outs_p = [jnp.stack([p_states[l][i] for l in range(depth)]) for i in range(6)]
    xs, *outs_s = _sample_path(x_sample, states, p, tabs_s, ubias, cos_s, sin_s)
    return (xp, xs, *outs_p, *outs_s)
```

```python
import functools
import math

import numpy as np
import jax
import jax.numpy as jnp
from jax import lax
from jax.experimental import pallas as pl
from jax.experimental.pallas import tpu as pltpu

D_MODEL = 1024
HEAD_DIM = 64
M_HEADS = 4
R_HEADS = 4
A_HEADS = 8
A_KV_HEADS = 2
KV_GROUP = A_HEADS // A_KV_HEADS
M_DIM = M_HEADS * HEAD_DIM
R_DIM = R_HEADS * HEAD_DIM
A_DIM = A_HEADS * HEAD_DIM
A_KV_DIM = A_KV_HEADS * HEAD_DIM
WINDOW = 128
N_BUCKETS = 32
REL_MAX_DIST = 128
ROPE_BASE = 10000.0
NORM_EPS = 1e-6
QK_SCALE = HEAD_DIM ** -0.5

LANES = 128
ROWS = 128
GATE_PAD = LANES
PAIRS = M_HEADS // 2
SPLIT_TERMS = 3
SEL_TERMS = 2

OFF_MQ = 0
OFF_MK = OFF_MQ + M_DIM
OFF_MV = OFF_MK + M_DIM
OFF_MO = OFF_MV + M_DIM
OFF_MZ = OFF_MO + M_DIM
OFF_G = OFF_MZ + M_DIM
OFF_RQ = OFF_G + GATE_PAD
OFF_RK = OFF_RQ + R_DIM
OFF_RV = OFF_RK + R_DIM
OFF_RZ = OFF_RV + R_DIM
OFF_AQ = OFF_RZ + R_DIM
OFF_AK = OFF_AQ + A_DIM
OFF_AV = OFF_AK + A_KV_DIM
OFF_AZ = OFF_AV + A_KV_DIM
P_COLS = OFF_AZ + A_DIM
N_GATES = 2 * M_HEADS
PROJ_COL_BLOCK = 512
SCHEDULE = "FBFBFB"
ATTN_HEAD_ORDER = tuple(h for j in range(KV_GROUP) for h in (j, KV_GROUP + j))

PROMPT_ROWS = 512
DECODE_PROJ_GROUPS = 4
VMEM_LIMIT_BYTES = 56 * 1024 * 1024
DECODE_VMEM_LIMIT_BYTES = 58 * 1024 * 1024

f32 = jnp.float32
bf16 = jnp.bfloat16


def _dot(a, b):
    return jnp.dot(a, b, preferred_element_type=f32)


def _dot_nt(a, b):
    return lax.dot_general(a, b, (((1,), (1,)), ((), ())), preferred_element_type=f32)


def _sigmoid(x):
    return 1.0 / (1.0 + jnp.exp(-x))


def _silu(x):
    return x * _sigmoid(x)


def _log_sigmoid(x):
    return jnp.minimum(x, 0.0) - jnp.log(1.0 + jnp.exp(-jnp.abs(x)))


def _split_parts(x, terms):
    parts, r = [], x
    for i in range(terms):
        p = r.astype(bf16)
        parts.append(p)
        if i + 1 < terms:
            r = r - p.astype(f32)
    return parts


def _split_terms(x, terms=SPLIT_TERMS):
    return jnp.concatenate(_split_parts(x, terms), axis=1)


def _exact_tril_dot(tril3, x):
    return _dot(tril3, jnp.concatenate(_split_parts(x, SPLIT_TERMS), axis=0))


def _rope(x, cos_t, sin_t, first_half):
    up = pltpu.roll(x, LANES - HEAD_DIM // 2, axis=1)
    down = pltpu.roll(x, HEAD_DIM // 2, axis=1)
    return x * cos_t + jnp.where(first_half, up, down) * sin_t


def _rms_project(x_ref, ngain_ref, win_ref, proj_sc):
    xf = x_ref[...]
    u = xf * lax.rsqrt(jnp.mean(xf * xf, axis=1, keepdims=True) + NORM_EPS) * ngain_ref[...]
    u16 = u.astype(bf16)
    for c0 in range(0, P_COLS, PROJ_COL_BLOCK):
        c1 = min(c0 + PROJ_COL_BLOCK, P_COLS)
        proj_sc[:, c0:c1] = _dot_nt(u16, win_ref[c0:c1, :])


def _prompt_kernel(x_ref, ngain_ref, win_ref, gbias_ref, mgain_ref, rgain_ref, sinks_ref, wout_ref,
                   fgain_ref, tril3_ref, maskadd_ref, dmat2_ref, rslab_ref, ubias_ref, cos_ref,
                   sin_ref, cosoff_ref, sinoff_ref, selh_ref, selp_ref,
                   y_ref, c_out, n_out, m_out, s_out, k_out, v_out,
                   proj_sc, mix_sc, cn_sc, sb_sc, m_sc, kp_sc, vp_sc, bias_sc, *, chunks, final):
    step = pl.program_id(1)
    last_step = pl.num_programs(1) - 1

    @pl.when(jnp.logical_and(pl.program_id(0) == 0, step == 0))
    def _():
        for blk in range(A_HEADS):
            u = ubias_ref[ATTN_HEAD_ORDER[blk]:ATTN_HEAD_ORDER[blk] + 1, :]
            bias_sc[blk * ROWS:(blk + 1) * ROWS, :] = jnp.concatenate(
                [_skew(u[:, :2 * WINDOW], ROWS), _skew(u[:, 2 * WINDOW:], ROWS)], axis=1)

    xf = x_ref[...]
    u16 = (xf * lax.rsqrt(jnp.mean(xf * xf, axis=1, keepdims=True) + NORM_EPS) * ngain_ref[...]).astype(bf16)
    col_blocks = [(c0, min(c0 + PROJ_COL_BLOCK, P_COLS)) for c0 in range(0, P_COLS, PROJ_COL_BLOCK)]
    half_rows = (chunks // 2) * ROWS if chunks > 1 else chunks * ROWS

    def project(r0, r1, c0, c1):
        proj_sc[r0:r1, c0:c1] = _dot_nt(u16[r0:r1], win_ref[c0:c1, :])

    for c0, c1 in col_blocks:
        project(0, half_rows, c0, c1)
    late_pieces = [(half_rows, chunks * ROWS, c0, c1) for c0, c1 in col_blocks] if half_rows < chunks * ROWS else []

    @pl.when(step == 0)
    def _():
        cn_sc[...] = jnp.zeros_like(cn_sc)
        sb_sc[...] = jnp.zeros_like(sb_sc)
        m_sc[...] = jnp.zeros_like(m_sc)
        kp_sc[...] = jnp.zeros_like(kp_sc)
        vp_sc[...] = jnp.zeros_like(vp_sc)

    lane = lax.broadcasted_iota(jnp.int32, (ROWS, LANES), 1)
    row = lax.broadcasted_iota(jnp.int32, (ROWS, LANES), 0)
    left = lane < HEAD_DIM
    first_half = (lane & (HEAD_DIM - 1)) < (HEAD_DIM // 2)
    head_col = lane < M_HEADS
    blockdiag = (row < HEAD_DIM) == left
    row2 = lax.broadcasted_iota(jnp.int32, (ROWS, 2 * LANES), 0)
    lane2w = lax.broadcasted_iota(jnp.int32, (ROWS, 2 * LANES), 1)
    left2 = (lane2w & (LANES - 1)) < HEAD_DIM
    blockdiag2 = (row2 < HEAD_DIM) == left2
    ones16 = jnp.ones((ROWS, LANES), bf16)
    halves, pick, pair_blockdiag, half_mean = _halves, _pick, _pair_blockdiag, _half_mean

    def chunk_body(ci):
        rows = slice(ci * ROWS, (ci + 1) * ROWS)

        def proj(off, width=LANES):
            return proj_sc[rows, off:off + width]

        gates = proj(OFF_G) + gbias_ref[...]
        bcum = _exact_tril_dot(tril3_ref[...], _log_sigmoid(gates))

        base = pl.ds(step * chunks + ci, 1)
        cos_a, sin_a = cos_ref[base, :], sin_ref[base, :]
        cos_b, sin_b = cosoff_ref[...], sinoff_ref[...]
        cos_t = cos_a * cos_b - sin_a * sin_b
        sin_t = sin_a * cos_b + cos_a * sin_b
        sin_t = jnp.where(first_half, -sin_t, sin_t)
        r_q = [_rope(proj(OFF_RQ + p * LANES), cos_t, sin_t, first_half) for p in range(PAIRS)]
        r_k = [_rope(proj(OFF_RK + p * LANES), cos_t, sin_t, first_half) * QK_SCALE for p in range(PAIRS)]
        r_vbd = pair_blockdiag([proj(OFF_RV + p * LANES).astype(bf16) for p in range(PAIRS)])
        r_sb = [sb_sc[p] for p in range(PAIRS)]
        r_sc = _dot_nt(jnp.concatenate([halves(q) for q in r_q], axis=1),
                       pair_blockdiag([k.astype(bf16) for k in r_k]))
        r_inter = _dot(jnp.concatenate([q.astype(bf16) for q in r_q], axis=1),
                       pair_blockdiag([sb.astype(bf16) for sb in r_sb]))
        r_upd = _dot(jnp.concatenate([(r_k[p] * rslab_ref[1, p]).T.astype(bf16) for p in range(PAIRS)], axis=1),
                     r_vbd)
        for p in range(PAIRS):
            sb_sc[p] = rslab_ref[2, p] * r_sb[p] + jnp.where(blockdiag, r_upd[:, p * LANES:(p + 1) * LANES], 0.0)
        yield

        kcur, vcur = proj(OFF_AK), proj(OFF_AV)
        kprev, vprev = kp_sc[...], vp_sc[...]
        kk16 = jnp.concatenate([kcur, kprev], axis=0).astype(bf16)
        vv16 = jnp.concatenate([jnp.concatenate([vcur.astype(bf16), ones16], axis=1),
                                jnp.concatenate([vprev.astype(bf16), ones16], axis=1)], axis=0)
        kp_sc[...] = kcur
        vp_sc[...] = vcur
        a_q = jnp.concatenate([halves(proj(OFF_AQ + j * LANES) * QK_SCALE) for j in range(KV_GROUP)], axis=0)
        a_s = _dot_nt(a_q, kk16) + bias_sc[...]
        if ci == 0:
            pen = jnp.where(step == 0, -jnp.inf, 0.0).astype(f32)
            a_s = a_s + jnp.where(lax.broadcasted_iota(jnp.int32, (1, 2 * ROWS), 1) >= ROWS, pen, 0.0)

        m_q = [proj(OFF_MQ + p * LANES) for p in range(PAIRS)]
        m_k = [proj(OFF_MK + p * LANES) * QK_SCALE for p in range(PAIRS)]
        m_qk = _dot_nt(jnp.concatenate([halves(q) for q in m_q], axis=1),
                       pair_blockdiag([k.astype(bf16) for k in m_k]))
        m_q16 = [q.astype(bf16) for q in m_q]
        m_v16 = [proj(OFF_MV + p * LANES).astype(bf16) for p in range(PAIRS)]
        yield

        zb = pltpu.roll(bcum, LANES - M_HEADS, axis=1)
        r_mat = jnp.where(head_col, gates - zb, 0.0)
        cm = r_mat
        sh = 1
        while sh < ROWS:
            cm = jnp.where(row >= sh, jnp.maximum(cm, pltpu.roll(cm, sh, axis=0)), cm)
            sh *= 2
        mprev = m_sc[...]
        mx = jnp.maximum(mprev, cm)
        gm = mprev - mx
        em = jnp.where(head_col, -(zb + mx), 0.0)
        mx_last = jnp.broadcast_to(mx[ROWS - 1:ROWS, :], (ROWS, LANES))
        m_sc[...] = jnp.where(head_col, jnp.broadcast_to((zb + mx)[ROWS - 1:ROWS, :], (ROWS, LANES)), 0.0)
        mx_b = _dot_nt(_split_terms(mx, SEL_TERMS), selh_ref[...])
        slabs = jnp.exp(_dot_nt(_split_terms(jnp.concatenate([gm, em, r_mat - mx_last], axis=0), SEL_TERMS),
                                selp_ref[...]))
        winter_b, emt_b, ws_b = slabs[:ROWS], slabs[ROWS:2 * ROWS], slabs[2 * ROWS:]
        r_t = r_mat.T
        yield

        outs = []
        r_acc = []
        r_o = _dot((r_sc * dmat2_ref[...]).astype(bf16), r_vbd)
        for p in range(PAIRS):
            ps = slice(p * LANES, (p + 1) * LANES)
            r_acc.append(pick(r_o[:, ps]) + rslab_ref[0, p] * r_inter[:, ps])

        a_out = []
        a_p = []
        for blk in range(A_HEADS):
            s = a_s[blk * ROWS:(blk + 1) * ROWS]
            sink = sinks_ref[ATTN_HEAD_ORDER[blk]]
            m = jnp.maximum(jnp.max(jnp.maximum(s[:, :ROWS], s[:, ROWS:]), axis=1, keepdims=True), sink)
            a_p.append(jnp.exp(s - m).astype(bf16))
            a_out.append(jnp.exp(sink - m))
        a_pv = _dot(jnp.concatenate(a_p, axis=0), vv16)
        yield

        maskadd = maskadd_ref[...]
        for p in range(PAIRS):
            ps = slice(p * LANES, (p + 1) * LANES)
            cn = cn_sc[p]
            w = jnp.concatenate(
                [jnp.exp((r_t[2 * p + side:2 * p + side + 1, :] + maskadd)
                         - mx_b[:, (2 * p + side) * ROWS:(2 * p + side + 1) * ROWS]) for side in range(2)],
                axis=0) * m_qk[:, ps]
            acc = (pick(_dot(w.astype(bf16), jnp.concatenate([m_v16[p], ones16], axis=1)))
                   + jnp.concatenate([winter_b[:, ps]] * 2, axis=1) * _dot(m_q16[p], cn.astype(bf16)))
            hh = acc[:, :LANES] / jnp.maximum(jnp.abs(acc[:, LANES:]), emt_b[:, ps])
            outs.append(_sigmoid(proj(OFF_MO + p * LANES)) * hh)
            kwt16 = (m_k[p] * ws_b[:, ps]).T.astype(bf16)
            dcn = _dot(kwt16, jnp.concatenate([m_v16[p], ones16], axis=1))
            decay = winter_b[ROWS - 1:ROWS, ps]
            cn_sc[p] = jnp.concatenate([decay, decay], axis=1) * cn + jnp.where(blockdiag2, dcn, 0.0)
        outs.extend(r_acc)
        yield

        x4 = jnp.concatenate(outs, axis=0)
        xc = x4 - half_mean(x4)
        y4 = xc * lax.rsqrt(half_mean(xc * xc) + NORM_EPS)
        for i in range(2 * PAIRS):
            gain = (mgain_ref if i < PAIRS else rgain_ref)[:, (i % PAIRS) * LANES:(i % PAIRS + 1) * LANES]
            zoff = (OFF_MZ if i < PAIRS else OFF_RZ) + (i % PAIRS) * LANES
            out = y4[i * ROWS:(i + 1) * ROWS] * gain * _silu(proj(zoff))
            mix_sc[rows, i * LANES:(i + 1) * LANES] = out.astype(bf16)
        for j in range(KV_GROUP):
            acc = pick(a_pv[2 * j * ROWS:(2 * j + 2) * ROWS])
            den = acc[:, LANES:] + jnp.where(left, a_out[2 * j], a_out[2 * j + 1])
            out = (acc[:, :LANES] / den) * _silu(proj(OFF_AZ + j * LANES))
            mix_sc[rows, M_DIM + R_DIM + j * LANES:M_DIM + R_DIM + (j + 1) * LANES] = out.astype(bf16)

    def out_project(r0, r1, c0, c1):
        y_ref[r0:r1, c0:c1] = x_ref[r0:r1, c0:c1] + _dot(mix_sc[r0:r1, :], wout_ref[:, c0:c1])

    out_blocks = [(c0, min(c0 + PROJ_COL_BLOCK, D_MODEL)) for c0 in range(0, D_MODEL, PROJ_COL_BLOCK)]
    early_out = [(0, half_rows, c0, c1) for c0, c1 in out_blocks] if half_rows < chunks * ROWS else []
    final_out = [(half_rows if early_out else 0, chunks * ROWS, c0, c1) for c0, c1 in out_blocks]

    def fill_mxu(ci):
        if late_pieces:
            project(*late_pieces.pop(0))
        elif early_out and (ci - 1) * ROWS >= half_rows:
            out_project(*early_out.pop(0))

    parts = [chunk_body(ci) for ci in range(chunks)]
    for ci in range(chunks + 1):
        if ci * ROWS >= half_rows:
            while late_pieces:
                project(*late_pieces.pop(0))
        for which in SCHEDULE:
            if which == "F" and ci < chunks:
                next(parts[ci])
                fill_mxu(ci)
            if which == "B" and ci > 0:
                next(parts[ci - 1], None)
                fill_mxu(ci)
    for piece in early_out + final_out:
        out_project(*piece)
    if final:
        y = y_ref[...]
        y_ref[...] = y * lax.rsqrt(jnp.mean(y * y, axis=1, keepdims=True) + NORM_EPS) * fgain_ref[...]

    @pl.when(step == last_step)
    def _():
        for p in range(PAIRS):
            cn = cn_sc[p]
            sb = sb_sc[p]
            n_t = cn[:, LANES:].T
            for side in range(2):
                h = 2 * p + side
                blk = slice(side * HEAD_DIM, (side + 1) * HEAD_DIM)
                c_out[0, h] = cn[blk, blk]
                s_out[0, h] = sb[blk, blk]
                n_out[0, h:h + 1, :] = n_t[side * HEAD_DIM:side * HEAD_DIM + 1, blk]
        m_out[0] = m_sc[...]
        k_out[0] = kp_sc[...]
        v_out[0] = vp_sc[...]


def _lane_is_left(shape):
    return (lax.broadcasted_iota(jnp.int32, shape, 1) & (LANES - 1)) < HEAD_DIM


def _halves(x):
    left = _lane_is_left(x.shape)
    return jnp.concatenate([jnp.where(left, x, 0.0), jnp.where(left, 0.0, x)], axis=0).astype(bf16)


def _pick(x):
    return jnp.where(_lane_is_left((ROWS, x.shape[1])), x[:ROWS], x[ROWS:])


def _pair_blockdiag(blocks):
    z = jnp.zeros_like(blocks[0])
    return jnp.concatenate(
        [jnp.concatenate([blk if j == i else z for j in range(len(blocks))], axis=1)
         for i, blk in enumerate(blocks)], axis=0)


def _half_mean(x):
    left = _lane_is_left(x.shape)
    s_left = jnp.sum(jnp.where(left, x, 0.0), axis=1, keepdims=True)
    s_right = jnp.sum(jnp.where(left, 0.0, x), axis=1, keepdims=True)
    return jnp.where(left, s_left, s_right) * (1.0 / HEAD_DIM)


def _group_last(x, groups):
    n = x.shape[1]
    glen = ROWS // groups
    x3 = x.reshape(groups, glen, n)
    return jnp.broadcast_to(x3[:, glen - 1:glen, :], (groups, glen, n)).reshape(ROWS, n)


def _sample_kernel(x_ref, ngain_ref, win_ref, gbias_ref, mgain_ref, rgain_ref, sinks_ref, wout_ref,
                   fgain_ref, tril3_ref, maskadd_ref, dmat2_ref, rslab_ref, ubias_ref, cos_ref, sin_ref,
                   selh_ref, selp_ref,
                   c_in, n_in, m_in, s_in, k_in, v_in,
                   y_ref, c_out, n_out, m_out, s_out, k_out, v_out,
                   proj_sc, mix_sc, xcur_sc, xall_sc, qb_sc, sp_sc, pp_sc, ob_sc, biasc_ref, biasp_ref, *, groups,
                   proj_groups, ret_full):
    glen = ROWS // groups
    glen_log2 = glen.bit_length() - 1
    hd_log2 = HEAD_DIM.bit_length() - 1
    layer = pl.program_id(0)
    step = pl.program_id(1)
    last_layer = pl.num_programs(0) - 1
    xrows = pl.ds(pl.multiple_of(step * ROWS, ROWS), ROWS)
    sub = lax.rem(step, proj_groups)
    rows = pl.ds(pl.multiple_of(sub * ROWS, ROWS), ROWS)
    wide = pl.ds(pl.multiple_of((step - sub) * ROWS, ROWS), proj_groups * ROWS)

    @pl.when(jnp.logical_and(sub == 0, layer == 0))
    def _():
        _rms_project(x_ref, ngain_ref, win_ref, proj_sc)

    @pl.when(jnp.logical_and(sub == 0, layer != 0))
    def _():
        _rms_project(xall_sc.at[wide, :], ngain_ref, win_ref, proj_sc)

    @pl.when(layer == 0)
    def _():
        xcur_sc[...] = x_ref[rows, :]

    @pl.when(layer != 0)
    def _():
        xcur_sc[...] = xall_sc[xrows, :]

    for b in range(groups):
        k_out[b] = pltpu.roll(k_in[b], WINDOW - glen, axis=1)
        v_out[b] = pltpu.roll(v_in[b], WINDOW - glen, axis=1)

    @pl.when(jnp.logical_and(step == 0, layer == 0))
    def _():
        for h in range(A_HEADS):
            u = ubias_ref[h:h + 1, :]
            biasc_ref[h] = _skew(u[:, :2 * WINDOW], ROWS) + maskadd_ref[...]
            biasp_ref[h] = jnp.concatenate([_skew(u[:, 2 * WINDOW:], glen)] * groups, axis=0)

    lane = lax.broadcasted_iota(jnp.int32, (ROWS, LANES), 1)
    first_half = (lane & (HEAD_DIM - 1)) < (HEAD_DIM // 2)

    r_i = lax.broadcasted_iota(jnp.int32, (ROWS, groups * HEAD_DIM), 0)
    c_i = lax.broadcasted_iota(jnp.int32, (ROWS, groups * HEAD_DIM), 1)
    blk = (r_i >> glen_log2) == (c_i >> hd_log2)
    r_t = lax.broadcasted_iota(jnp.int32, (groups * HEAD_DIM, ROWS), 0)
    c_t = lax.broadcasted_iota(jnp.int32, (groups * HEAD_DIM, ROWS), 1)
    blk_t = (r_t >> hd_log2) == (c_t >> glen_log2)

    def q_times_state(qh, st):
        qt = jnp.where(blk, jnp.concatenate([qh] * groups, axis=1), 0.0)
        return _dot(qt.astype(bf16), st.astype(bf16))

    def state_increment(kt_h, vh16):
        kt = jnp.where(blk_t, jnp.concatenate([kt_h] * groups, axis=0), 0.0)
        return _dot(kt.astype(bf16), vh16)

    def proj(off, width=LANES):
        return proj_sc[rows, off:off + width]

    def head_state_rows(slab, side):
        wide = slab[:, side * HEAD_DIM:(side + 1) * HEAD_DIM].reshape(groups, glen, HEAD_DIM)[:, 0:1, :]
        rows_ = jnp.broadcast_to(wide, (groups, HEAD_DIM, HEAD_DIM)).reshape(groups * HEAD_DIM, HEAD_DIM)
        return rows_, wide.reshape(groups, HEAD_DIM)

    row = lax.broadcasted_iota(jnp.int32, (ROWS, LANES), 0)
    tau = row & (glen - 1)
    head_col = lane < M_HEADS
    left = lane < HEAD_DIM
    ones16 = jnp.ones((ROWS, LANES), bf16)
    gates = proj(OFF_G) + gbias_ref[...]
    bcum = _exact_tril_dot(tril3_ref[...], _log_sigmoid(gates))
    zb = pltpu.roll(bcum, LANES - M_HEADS, axis=1)
    r_mat = jnp.where(head_col, gates - zb, 0.0)
    cm = r_mat
    sh = 1
    while sh < glen:
        cm = jnp.where(tau >= sh, jnp.maximum(cm, pltpu.roll(cm, sh, axis=0)), cm)
        sh *= 2
    mprev = m_in[...]
    mx = jnp.maximum(mprev, cm)
    gm = mprev - mx
    em = jnp.where(head_col, -(zb + mx), 0.0)
    mx_last = _group_last(mx, groups)
    m_out[...] = jnp.where(head_col, _group_last(zb + mx, groups), 0.0)
    mx_b = _dot_nt(_split_terms(mx, SEL_TERMS), selh_ref[...])
    slabs = jnp.exp(_dot_nt(_split_terms(jnp.concatenate([gm, em, r_mat - mx_last], axis=0), SEL_TERMS), selp_ref[...]))
    winter_b, emt_b, ws_b = slabs[:ROWS], slabs[ROWS:2 * ROWS], slabs[2 * ROWS:]
    decay_b = _group_last(winter_b, groups)
    r_t = r_mat.T
    maskadd = maskadd_ref[...]

    m_q = [proj(OFF_MQ + p * LANES) for p in range(PAIRS)]
    m_k = [proj(OFF_MK + p * LANES) * QK_SCALE for p in range(PAIRS)]
    m_v = [proj(OFF_MV + p * LANES) for p in range(PAIRS)]
    m_qk = _dot_nt(jnp.concatenate([_halves(q) for q in m_q], axis=1),
                   _pair_blockdiag([k.astype(bf16) for k in m_k]))
    outs = []
    for p in range(PAIRS):
        ps = slice(p * LANES, (p + 1) * LANES)
        w = jnp.concatenate(
            [jnp.exp((r_t[2 * p + side:2 * p + side + 1, :] + maskadd)
                     - mx_b[:, (2 * p + side) * ROWS:(2 * p + side + 1) * ROWS]) for side in range(2)],
            axis=0) * m_qk[:, ps]
        intra = _pick(_dot(w.astype(bf16), jnp.concatenate([m_v[p].astype(bf16), ones16], axis=1)))
        kw = m_k[p] * ws_b[:, ps]
        kwt = kw.T
        q_c, q_n = [], []
        for side in range(2):
            h = 2 * p + side
            hs = slice(side * HEAD_DIM, (side + 1) * HEAD_DIM)
            qh = m_q[p][:, hs]
            c_h = c_in[:, h].reshape(groups * HEAD_DIM, HEAD_DIM)
            n_g = n_in[h]
            n_rows = jnp.broadcast_to(n_g.reshape(groups, 1, HEAD_DIM),
                                      (groups, glen, HEAD_DIM)).reshape(ROWS, HEAD_DIM)
            q_c.append(q_times_state(qh, c_h))
            q_n.append(jnp.sum(qh * n_rows, axis=1, keepdims=True))
            dec_rows, dec_g = head_state_rows(decay_b[:, ps], side)
            c_new = dec_rows * c_h + state_increment(kwt[hs, :], m_v[p][:, hs].astype(bf16))
            c_out[:, h] = c_new.reshape(groups, HEAD_DIM, HEAD_DIM)
            n_out[h] = dec_g * n_g + jnp.sum(kw[:, hs].reshape(groups, glen, HEAD_DIM), axis=1)
        wb = winter_b[:, ps]
        num = intra[:, :LANES] + wb * jnp.concatenate(q_c, axis=1)
        nq = intra[:, LANES:] + wb * jnp.where(left, q_n[0], q_n[1])
        outs.append(_sigmoid(proj(OFF_MO + p * LANES)) * (num / jnp.maximum(jnp.abs(nq), emt_b[:, ps])))

    cos_t, sin_t = cos_ref[...], sin_ref[...]
    r_q = [_rope(proj(OFF_RQ + p * LANES), cos_t, sin_t, first_half) for p in range(PAIRS)]
    r_k = [_rope(proj(OFF_RK + p * LANES), cos_t, sin_t, first_half) * QK_SCALE for p in range(PAIRS)]
    r_v = [proj(OFF_RV + p * LANES) for p in range(PAIRS)]
    r_sc = _dot_nt(jnp.concatenate([_halves(q) for q in r_q], axis=1),
                   _pair_blockdiag([k.astype(bf16) for k in r_k]))
    r_o = _dot((r_sc * dmat2_ref[...]).astype(bf16), _pair_blockdiag([v.astype(bf16) for v in r_v]))
    for p in range(PAIRS):
        ps = slice(p * LANES, (p + 1) * LANES)
        rkt = (r_k[p] * rslab_ref[1, p]).T
        q_s = []
        for side in range(2):
            h = 2 * p + side
            hs = slice(side * HEAD_DIM, (side + 1) * HEAD_DIM)
            s_h = s_in[:, h].reshape(groups * HEAD_DIM, HEAD_DIM)
            q_s.append(q_times_state(r_q[p][:, hs], s_h))
            s_new = ret_full[h] * s_h + state_increment(rkt[hs, :], r_v[p][:, hs].astype(bf16))
            s_out[:, h] = s_new.reshape(groups, HEAD_DIM, HEAD_DIM)
        outs.append(_pick(r_o[:, ps]) + rslab_ref[0, p] * jnp.concatenate(q_s, axis=1))

    x4 = jnp.concatenate(outs, axis=0)
    xc = x4 - _half_mean(x4)
    y4 = xc * lax.rsqrt(_half_mean(xc * xc) + NORM_EPS)
    for i in range(2 * PAIRS):
        gain = (mgain_ref if i < PAIRS else rgain_ref)[:, (i % PAIRS) * LANES:(i % PAIRS + 1) * LANES]
        zoff = (OFF_MZ if i < PAIRS else OFF_RZ) + (i % PAIRS) * LANES
        mix_sc[:, i * LANES:(i + 1) * LANES] = (y4[i * ROWS:(i + 1) * ROWS] * gain * _silu(proj(zoff))).astype(bf16)

    kcur = proj_sc[rows, OFF_AK:OFF_AK + A_KV_DIM]
    vcur = proj_sc[rows, OFF_AV:OFF_AV + A_KV_DIM]
    kcur16, vcur16 = kcur.astype(bf16), vcur.astype(bf16)

    a_q = {}
    for j in range(KV_GROUP):
        hq = _halves(proj(OFF_AQ + j * LANES) * QK_SCALE)
        a_q[j], a_q[KV_GROUP + j] = hq[:ROWS], hq[ROWS:]
    for h in range(A_HEADS):
        qb_sc[:, h * glen:(h + 1) * glen, :] = a_q[h].reshape(groups, glen, LANES)
    sc_all = _dot_nt(jnp.concatenate([a_q[h] for h in range(A_HEADS)], axis=0), kcur16)

    for b in range(groups):
        sp = _dot(qb_sc[b], k_in[b].astype(bf16))
        sp_sc[:, b * glen:(b + 1) * glen, :] = sp.reshape(A_HEADS, glen, WINDOW)

    vcur1 = jnp.concatenate([vcur16, ones16], axis=1)
    esink, o_cur = [], []
    for h in range(A_HEADS):
        sc = sc_all[h * ROWS:(h + 1) * ROWS] + biasc_ref[h]
        sp = sp_sc[h] + biasp_ref[h]
        sink = sinks_ref[layer, h]
        m = jnp.maximum(jnp.max(jnp.maximum(sc, sp), axis=1, keepdims=True), sink)
        esink.append(jnp.exp(sink - m))
        o_cur.append(_dot(jnp.exp(sc - m).astype(bf16), vcur1))
        pp_sc[:, h * glen:(h + 1) * glen, :] = jnp.exp(sp - m).reshape(groups, glen, WINDOW).astype(bf16)

    for b in range(groups):
        v1 = jnp.concatenate([v_in[b].astype(bf16), ones16], axis=0)
        ob = _dot_nt(pp_sc[b], v1)
        ob_sc[:, b * glen:(b + 1) * glen, :] = ob.reshape(A_HEADS, glen, 2 * LANES)
    norm = []
    for h in range(A_HEADS):
        acc = o_cur[h] + ob_sc[h]
        norm.append(acc[:, :LANES] / (acc[:, LANES:] + esink[h]))
    outs = [jnp.where(left, norm[j], norm[KV_GROUP + j]) for j in range(KV_GROUP)]
    kcur_t, vcur_t = kcur.T, vcur.T
    new = slice(WINDOW - glen, WINDOW)
    for b in range(groups):
        shift = (WINDOW - glen - b * glen) % LANES
        k_out[b, :, new] = pltpu.roll(kcur_t, shift, axis=1)[:, new]
        v_out[b, :, new] = pltpu.roll(vcur_t, shift, axis=1)[:, new]
    out_a = jnp.concatenate(outs, axis=1) * _silu(proj_sc[rows, OFF_AZ:OFF_AZ + A_DIM])
    mix_sc[:, M_DIM + R_DIM:M_DIM + R_DIM + A_DIM] = out_a.astype(bf16)

    y = xcur_sc[...] + _dot(mix_sc[...], wout_ref[...])

    @pl.when(layer != last_layer)
    def _():
        xall_sc[xrows, :] = y
        y_ref[...] = y

    @pl.when(layer == last_layer)
    def _():
        y_ref[...] = y * lax.rsqrt(jnp.mean(y * y, axis=1, keepdims=True) + NORM_EPS) * fgain_ref[...]


def _t5_bucket(dist):
    max_exact = N_BUCKETS // 2
    d = np.maximum(dist, 1).astype(np.float32)
    large = max_exact + (np.log(d / max_exact) / np.log(REL_MAX_DIST / max_exact)
                         * (N_BUCKETS - max_exact)).astype(np.int32)
    large = np.minimum(large, N_BUCKETS - 1)
    return np.where(dist < max_exact, dist, large).astype(np.int32)


def _static_tables(groups):
    glen = ROWS // groups
    r = np.arange(ROWS)
    grp, tau = r // glen, r % glen
    causal = (grp[:, None] == grp[None, :]) & (tau[None, :] <= tau[:, None])
    tril = causal.astype(np.float32)
    maskadd = np.where(causal, 0.0, -np.inf).astype(np.float32)
    log_g = np.log1p(-np.exp2(-5.0 - np.arange(R_HEADS, dtype=np.float64)))
    diff = (tau[:, None] - tau[None, :]).astype(np.float64)
    dmat = np.where(causal[None], np.exp(log_g[:, None, None] * np.maximum(diff, 0.0)[None]), 0.0)
    inter = np.exp(log_g[None, :] * (tau[:, None] + 1.0))
    tail = np.exp(log_g[None, :] * (glen - 1.0 - tau[:, None]))
    full = np.exp(log_g * glen)
    lane_head = np.arange(LANES) // HEAD_DIM
    rslab = np.zeros((3, PAIRS, ROWS, LANES), np.float64)
    for p in range(PAIRS):
        rslab[0, p] = inter[:, 2 * p + lane_head]
        rslab[1, p] = tail[:, 2 * p + lane_head]
        rslab[2, p] = full[2 * p + lane_head][None, :]
    selh = np.zeros((M_HEADS * ROWS, SEL_TERMS * LANES), np.float32)
    selp = np.zeros((PAIRS * LANES, SEL_TERMS * LANES), np.float32)
    for t in range(SEL_TERMS):
        for h in range(M_HEADS):
            selh[h * ROWS:(h + 1) * ROWS, t * LANES + h] = 1.0
        for p in range(PAIRS):
            for side in range(2):
                selp[p * LANES + side * HEAD_DIM:p * LANES + (side + 1) * HEAD_DIM, t * LANES + 2 * p + side] = 1.0
    return dict(tril3=jnp.asarray(np.concatenate([tril] * SPLIT_TERMS, axis=1), bf16),
                maskadd=maskadd,
                dmat2=np.concatenate(list(dmat.astype(np.float32).reshape(PAIRS, 2 * ROWS, ROWS)), axis=1),
                rslab=rslab.astype(np.float32),
                full=tuple(float(v) for v in full),
                selh=jnp.asarray(selh, bf16), selp=jnp.asarray(selp, bf16))


def _bias_vectors(rel_table):
    tb = jnp.transpose(rel_table[_t5_bucket(np.arange(WINDOW))]).astype(f32)
    ninf = jnp.full((A_HEADS, WINDOW), -jnp.inf, f32)
    rev = tb[:, :0:-1]
    return jnp.concatenate([tb[:, :1], ninf, rev, ninf[:, :1], rev, ninf], axis=1)


def _skew(u_row, rows):
    x = jnp.broadcast_to(u_row, (rows, 2 * WINDOW))
    return pltpu.roll(x, 0, 1, stride=1, stride_axis=0)[:, :WINDOW]


def _rope_tables(pos, signed=True):
    half = HEAD_DIM // 2
    inv = ROPE_BASE ** (-jnp.arange(half, dtype=f32) / half)
    ang = pos.astype(f32)[:, None] * inv[None, :]
    cos, sin = jnp.cos(ang), jnp.sin(ang)
    reps = LANES // HEAD_DIM
    cos_t = jnp.tile(jnp.concatenate([cos, cos], axis=1), (1, reps))
    sin_t = jnp.tile(jnp.concatenate([-sin if signed else sin, sin], axis=1), (1, reps))
    return cos_t, sin_t


def _const_spec(shape, nargs):
    zeros = (0,) * len(shape)
    if nargs == 1:
        return pl.BlockSpec(shape, lambda i: zeros)
    return pl.BlockSpec(shape, lambda i, j: zeros)


def _layer_spec(shape, layer, nargs):
    idx = (layer,) + (0,) * len(shape)
    if nargs == 1:
        return pl.BlockSpec((None,) + shape, lambda i: idx)
    return pl.BlockSpec((None,) + shape, lambda i, j: idx)


def _param_specs(layer, nargs):
    ls = functools.partial(_layer_spec, layer=layer, nargs=nargs)
    return [ls((1, D_MODEL)), ls((P_COLS, D_MODEL)), ls((1, LANES)), ls((1, M_DIM)), ls((1, R_DIM)),
            pl.BlockSpec(memory_space=pltpu.SMEM), ls((D_MODEL, D_MODEL)), _const_spec((1, D_MODEL), nargs)]


def _param_args(p, layer):
    return (p["norm_gain"], p["w_in"], p["gbias"], p["m_gain"], p["r_gain"], p["sinks"][layer], p["w_out"],
            p["fgain"])


def _prompt_layer(x, p, layer, tabs, ubias, rope, final):
    B, T, _ = x.shape
    tb = min(PROMPT_ROWS, T)
    chunks = tb // ROWS
    nt = T // tb
    cs = functools.partial(_const_spec, nargs=2)
    in_specs = [pl.BlockSpec((None, tb, D_MODEL), lambda b, t: (b, t, 0))] + _param_specs(layer, 2) + [
        cs((ROWS, SPLIT_TERMS * ROWS)), cs((ROWS, ROWS)), cs((2 * ROWS, PAIRS * ROWS)),
        cs((3, PAIRS, ROWS, LANES)), cs((A_HEADS, 4 * WINDOW)),
        cs((T // ROWS, LANES)), cs((T // ROWS, LANES)), cs((ROWS, LANES)), cs((ROWS, LANES)),
        cs((M_HEADS * ROWS, SEL_TERMS * LANES)), cs((PAIRS * LANES, SEL_TERMS * LANES)),
    ]
    out_shape = (
        jax.ShapeDtypeStruct((B, T, D_MODEL), f32),
        jax.ShapeDtypeStruct((B, M_HEADS, HEAD_DIM, HEAD_DIM), f32),
        jax.ShapeDtypeStruct((B, M_HEADS, HEAD_DIM), f32),
        jax.ShapeDtypeStruct((B, ROWS, LANES), f32),
        jax.ShapeDtypeStruct((B, R_HEADS, HEAD_DIM, HEAD_DIM), f32),
        jax.ShapeDtypeStruct((B, WINDOW, A_KV_DIM), f32),
        jax.ShapeDtypeStruct((B, WINDOW, A_KV_DIM), f32),
    )
    out_specs = (
        pl.BlockSpec((None, tb, D_MODEL), lambda b, t: (b, t, 0)),
        pl.BlockSpec((1, M_HEADS, HEAD_DIM, HEAD_DIM), lambda b, t: (b, 0, 0, 0)),
        pl.BlockSpec((1, M_HEADS, HEAD_DIM), lambda b, t: (b, 0, 0)),
        pl.BlockSpec((1, ROWS, LANES), lambda b, t: (b, 0, 0)),
        pl.BlockSpec((1, R_HEADS, HEAD_DIM, HEAD_DIM), lambda b, t: (b, 0, 0, 0)),
        pl.BlockSpec((1, WINDOW, A_KV_DIM), lambda b, t: (b, 0, 0)),
        pl.BlockSpec((1, WINDOW, A_KV_DIM), lambda b, t: (b, 0, 0)),
    )
    kern = functools.partial(_prompt_kernel, chunks=chunks, final=final)
    y, c, n, m, s, k, v = pl.pallas_call(
        kern, grid=(B, nt), in_specs=in_specs, out_specs=out_specs, out_shape=out_shape,
        scratch_shapes=[pltpu.VMEM((tb, P_COLS), f32), pltpu.VMEM((tb, D_MODEL), bf16),
                        pltpu.VMEM((PAIRS, ROWS, 2 * LANES), f32), pltpu.VMEM((PAIRS, ROWS, LANES), f32),
                        pltpu.VMEM((ROWS, LANES), f32), pltpu.VMEM((ROWS, A_KV_DIM), f32),
                        pltpu.VMEM((ROWS, A_KV_DIM), f32), pltpu.VMEM((A_HEADS * ROWS, 2 * WINDOW), f32)],
        compiler_params=pltpu.CompilerParams(dimension_semantics=("arbitrary", "arbitrary"),
                                             vmem_limit_bytes=VMEM_LIMIT_BYTES),
        name="prompt_layer",
    )(x, *_param_args(p, layer), tabs["tril3"], tabs["maskadd"], tabs["dmat2"], tabs["rslab"], ubias,
      *rope, tabs["selh"], tabs["selp"])
    k = k.reshape(B, WINDOW, A_KV_HEADS, HEAD_DIM)
    v = v.reshape(B, WINDOW, A_KV_HEADS, HEAD_DIM)
    return y, c, n, m[:, 0, :M_HEADS], s, k, v


def _batch_major_kernel(*refs):
    half = len(refs) // 2
    for x_ref, o_ref in zip(refs[:half], refs[half:]):
        o_ref[...] = jnp.transpose(x_ref[...], (2, 0, 1))


def _batch_major(*states):
    depth, B, heads, d, e = states[0].shape
    assert all(s.shape == states[0].shape for s in states)
    views = [jnp.transpose(s, (0, 2, 3, 4, 1)) for s in states]
    return pl.pallas_call(
        _batch_major_kernel, grid=(depth, heads),
        in_specs=[pl.BlockSpec((None, None, d, e, B), lambda l, h: (l, h, 0, 0, 0))] * len(states),
        out_specs=[pl.BlockSpec((None, B, None, d, e), lambda l, h: (l, 0, h, 0, 0))] * len(states),
        out_shape=[jax.ShapeDtypeStruct(s.shape, s.dtype) for s in states],
        compiler_params=pltpu.CompilerParams(dimension_semantics=("arbitrary", "arbitrary"),
                                             vmem_limit_bytes=VMEM_LIMIT_BYTES),
        name="batch_major",
    )(*views)


def _sample_path(x, states, p, tabs, ubias, cos_t, sin_t):
    B, T, _ = x.shape
    groups = ROWS // T
    nb = B // groups
    pg = math.gcd(nb, DECODE_PROJ_GROUPS)
    c0, n0, m0, s0, k0, v0 = states
    depth = c0.shape[0]
    c0, s0 = _batch_major(c0, s0)
    x2 = x.reshape(B * T, D_MODEL)
    n0t = jnp.transpose(n0, (0, 2, 1, 3))
    m0r = jnp.pad(jnp.repeat(m0, T, axis=1), ((0, 0), (0, 0), (0, LANES - M_HEADS)))
    k0r = jnp.transpose(k0, (0, 1, 3, 4, 2)).reshape(depth, B, A_KV_DIM, WINDOW)
    v0r = jnp.transpose(v0, (0, 1, 3, 4, 2)).reshape(depth, B, A_KV_DIM, WINDOW)

    def cs(shape):
        zeros = (0,) * len(shape)
        return pl.BlockSpec(shape, lambda l, i: zeros)

    def per_layer(shape, buffers=None):
        zeros = (0,) * len(shape)
        mode = {} if buffers is None else dict(pipeline_mode=pl.Buffered(buffers))
        return pl.BlockSpec((None,) + shape, lambda l, i: (l,) + zeros, **mode)

    st4 = pl.BlockSpec((None, groups, M_HEADS, HEAD_DIM, HEAD_DIM), lambda l, i: (l, i, 0, 0, 0))
    stn = pl.BlockSpec((None, M_HEADS, groups, HEAD_DIM), lambda l, i: (l, 0, i, 0))
    stm = pl.BlockSpec((None, ROWS, LANES), lambda l, i: (l, i, 0))
    stk = pl.BlockSpec((None, groups, A_KV_DIM, WINDOW), lambda l, i: (l, i, 0, 0))
    in_specs = [
        pl.BlockSpec((pg * ROWS, D_MODEL), lambda l, i: (i // pg, 0)),
        per_layer((1, D_MODEL)), per_layer((P_COLS, D_MODEL), 1), per_layer((1, LANES)), per_layer((1, M_DIM)),
        per_layer((1, R_DIM)), pl.BlockSpec(memory_space=pltpu.SMEM), per_layer((D_MODEL, D_MODEL), 1),
        cs((1, D_MODEL)),
        cs((ROWS, SPLIT_TERMS * ROWS)), cs((ROWS, ROWS)), cs((2 * ROWS, PAIRS * ROWS)),
        cs((3, PAIRS, ROWS, LANES)), cs((A_HEADS, 4 * WINDOW)),
        cs((ROWS, LANES)), cs((ROWS, LANES)),
        cs((M_HEADS * ROWS, SEL_TERMS * LANES)), cs((PAIRS * LANES, SEL_TERMS * LANES)),
        st4, stn, stm, st4, stk, stk,
    ]
    out_shape = (
        jax.ShapeDtypeStruct((B * T, D_MODEL), f32),
        jax.ShapeDtypeStruct((depth, B, M_HEADS, HEAD_DIM, HEAD_DIM), f32),
        jax.ShapeDtypeStruct((depth, M_HEADS, B, HEAD_DIM), f32),
        jax.ShapeDtypeStruct((depth, B * T, LANES), f32),
        jax.ShapeDtypeStruct((depth, B, R_HEADS, HEAD_DIM, HEAD_DIM), f32),
        jax.ShapeDtypeStruct((depth, B, A_KV_DIM, WINDOW), f32),
        jax.ShapeDtypeStruct((depth, B, A_KV_DIM, WINDOW), f32),
    )
    y_spec = pl.BlockSpec((ROWS, D_MODEL), lambda l, i: (jnp.where(l == depth - 1, i, 0), 0))
    out_specs = (y_spec, st4, stn, stm, st4, stk, stk)
    kern = functools.partial(_sample_kernel, groups=groups, proj_groups=pg, ret_full=tabs["full"])
    y, c, n, m, s, k, v = pl.pallas_call(
        kern, grid=(depth, nb), in_specs=in_specs, out_specs=out_specs, out_shape=out_shape,
        scratch_shapes=[pltpu.VMEM((pg * ROWS, P_COLS), f32), pltpu.VMEM((ROWS, D_MODEL), bf16),
                        pltpu.VMEM((ROWS, D_MODEL), f32), pltpu.VMEM((B * T, D_MODEL), f32),
                        pltpu.VMEM((groups, A_HEADS * T, A_KV_DIM), bf16),
                        pltpu.VMEM((A_HEADS, ROWS, WINDOW), f32),
                        pltpu.VMEM((groups, A_HEADS * T, WINDOW), bf16),
                        pltpu.VMEM((A_HEADS, ROWS, 2 * LANES), f32),
                        pltpu.VMEM((A_HEADS, ROWS, ROWS), f32), pltpu.VMEM((A_HEADS, ROWS, WINDOW), f32)],
        compiler_params=pltpu.CompilerParams(dimension_semantics=("arbitrary", "arbitrary"),
                                             vmem_limit_bytes=DECODE_VMEM_LIMIT_BYTES),
        name="sample_path",
    )(x2, p["norm_gain"], p["w_in"], p["gbias"], p["m_gain"], p["r_gain"], p["sinks"], p["w_out"], p["fgain"],
      tabs["tril3"], tabs["maskadd"], tabs["dmat2"], tabs["rslab"], ubias,
      cos_t, sin_t, tabs["selh"], tabs["selp"], c0, n0t, m0r, s0, k0r, v0r)
    y = y.reshape(B, T, D_MODEL)
    n = jnp.transpose(n, (0, 2, 1, 3))
    m = m.reshape(depth, B, T, LANES)[:, :, 0, :M_HEADS]
    k = jnp.transpose(k.reshape(depth, B, A_KV_HEADS, HEAD_DIM, WINDOW), (0, 1, 4, 2, 3))
    v = jnp.transpose(v.reshape(depth, B, A_KV_HEADS, HEAD_DIM, WINDOW), (0, 1, 4, 2, 3))
    return y, c, n, m, s, k, v


def _prepare_params(norm_gain, w_in, mlstm_gate_bias, mlstm_norm_gain, ret_norm_gain, attn_sinks, w_out,
                    final_norm_gain):
    depth = w_in.shape[0]
    w_t = jnp.swapaxes(w_in, 1, 2)
    split = OFF_G + N_GATES
    aq0 = split + 4 * R_DIM
    akv0 = aq0 + A_DIM
    az0 = akv0 + 2 * A_KV_DIM

    def by_head(w):
        w = w.reshape(depth, A_KV_HEADS, KV_GROUP, HEAD_DIM, D_MODEL)
        return jnp.transpose(w, (0, 2, 1, 3, 4)).reshape(depth, A_DIM, D_MODEL)

    w_in_p = jnp.concatenate(
        [w_t[:, :split], jnp.zeros((depth, GATE_PAD - N_GATES, D_MODEL), w_t.dtype), w_t[:, split:aq0],
         by_head(w_t[:, aq0:akv0]), w_t[:, akv0:az0], by_head(w_t[:, az0:])], axis=1).astype(bf16)
    wo16 = w_out.astype(bf16)
    a0 = M_DIM + R_DIM
    w_out_p = jnp.concatenate([wo16[:, :a0, :], by_head(wo16[:, a0:, :])], axis=1)
    gbias = jnp.pad(mlstm_gate_bias.reshape(depth, 1, N_GATES), ((0, 0), (0, 0), (0, LANES - N_GATES)))
    return dict(norm_gain=norm_gain.reshape(depth, 1, D_MODEL), w_in=w_in_p, gbias=gbias,
                m_gain=mlstm_norm_gain.reshape(depth, 1, M_DIM), r_gain=ret_norm_gain.reshape(depth, 1, R_DIM),
                sinks=attn_sinks, w_out=w_out_p, fgain=final_norm_gain.reshape(1, D_MODEL))


def kernel(x_prompt, x_sample, state_mlstm_C, state_mlstm_n, state_mlstm_m, state_ret_S, cache_win_k,
           cache_win_v, norm_gain, w_in, mlstm_gate_bias, mlstm_norm_gain, ret_norm_gain, attn_sinks,
           rel_bias_table, w_out, final_norm_gain):
    depth = w_in.shape[0]
    seq = x_prompt.shape[1]
    dec_seq = x_sample.shape[1]
    past_len = seq
    p = _prepare_params(norm_gain, w_in, mlstm_gate_bias, mlstm_norm_gain, ret_norm_gain, attn_sinks, w_out,
                        final_norm_gain)
    tabs_p = _static_tables(1)
    tabs_s = _static_tables(ROWS // dec_seq)
    ubias = _bias_vectors(rel_bias_table)
    rope_p = (*_rope_tables(jnp.arange(0, seq, ROWS, dtype=jnp.int32), signed=False),
              *_rope_tables(jnp.arange(ROWS, dtype=jnp.int32), signed=False))
    cos_s, sin_s = _rope_tables(past_len + (jnp.arange(ROWS, dtype=jnp.int32) % dec_seq))
    states = (state_mlstm_C, state_mlstm_n, state_mlstm_m, state_ret_S, cache_win_k, cache_win_v)

    xp = x_prompt
    p_states = []
    for layer in range(depth):
        xp, *sp = _prompt_layer(xp, p, layer, tabs_p, ubias, rope_p, layer == depth - 1)
        p_states.append(sp)
    outs_p = [jnp.stack([p_states[l][i] for l in range(depth)]) for i in range(6)]
    xs, *outs_s = _sample_path(x_sample, states, p, tabs_s, ubias, cos_s, sin_s)
    return (xp, xs, *outs_p, *outs_s)
```

```python
import functools
import math

import numpy as np
import jax
import jax.numpy as jnp
from jax import lax
from jax.experimental import pallas as pl
from jax.experimental.pallas import tpu as pltpu

D_MODEL = 1024
HEAD_DIM = 64
M_HEADS = 4
R_HEADS = 4
A_HEADS = 8
A_KV_HEADS = 2
KV_GROUP = A_HEADS // A_KV_HEADS
M_DIM = M_HEADS * HEAD_DIM
R_DIM = R_HEADS * HEAD_DIM
A_DIM = A_HEADS * HEAD_DIM
A_KV_DIM = A_KV_HEADS * HEAD_DIM
WINDOW = 128
N_BUCKETS = 32
REL_MAX_DIST = 128
ROPE_BASE = 10000.0
NORM_EPS = 1e-6
QK_SCALE = HEAD_DIM ** -0.5

LANES = 128
ROWS = 128
GATE_PAD = LANES
PAIRS = M_HEADS // 2
SPLIT_TERMS = 3
SEL_TERMS = 2

OFF_MQ = 0
OFF_MK = OFF_MQ + M_DIM
OFF_MV = OFF_MK + M_DIM
OFF_MO = OFF_MV + M_DIM
OFF_MZ = OFF_MO + M_DIM
OFF_G = OFF_MZ + M_DIM
OFF_RQ = OFF_G + GATE_PAD
OFF_RK = OFF_RQ + R_DIM
OFF_RV = OFF_RK + R_DIM
OFF_RZ = OFF_RV + R_DIM
OFF_AQ = OFF_RZ + R_DIM
OFF_AK = OFF_AQ + A_DIM
OFF_AV = OFF_AK + A_KV_DIM
OFF_AZ = OFF_AV + A_KV_DIM
P_COLS = OFF_AZ + A_DIM
N_GATES = 2 * M_HEADS
PROJ_COL_BLOCK = 512
SCHEDULE = "FBFBFB"
ATTN_HEAD_ORDER = tuple(h for j in range(KV_GROUP) for h in (j, KV_GROUP + j))

PROMPT_ROWS = 512
PACK_COL_BLOCK = 256
DECODE_PROJ_GROUPS = 4
VMEM_LIMIT_BYTES = 56 * 1024 * 1024
DECODE_VMEM_LIMIT_BYTES = 58 * 1024 * 1024

f32 = jnp.float32
bf16 = jnp.bfloat16


def _dot(a, b):
    return jnp.dot(a, b, preferred_element_type=f32)


def _dot_nt(a, b):
    return lax.dot_general(a, b, (((1,), (1,)), ((), ())), preferred_element_type=f32)


def _sigmoid(x):
    return 1.0 / (1.0 + jnp.exp(-x))


def _silu(x):
    return x * _sigmoid(x)


def _log_sigmoid(x):
    return jnp.minimum(x, 0.0) - jnp.log(1.0 + jnp.exp(-jnp.abs(x)))


def _split_parts(x, terms):
    parts, r = [], x
    for i in range(terms):
        p = r.astype(bf16)
        parts.append(p)
        if i + 1 < terms:
            r = r - p.astype(f32)
    return parts


def _split_terms(x, terms=SPLIT_TERMS):
    return jnp.concatenate(_split_parts(x, terms), axis=1)


def _exact_tril_dot(tril3, x):
    return _dot(tril3, jnp.concatenate(_split_parts(x, SPLIT_TERMS), axis=0))


def _rope(x, cos_t, sin_t, first_half):
    up = pltpu.roll(x, LANES - HEAD_DIM // 2, axis=1)
    down = pltpu.roll(x, HEAD_DIM // 2, axis=1)
    return x * cos_t + jnp.where(first_half, up, down) * sin_t


def _rms_project(x_ref, ngain_ref, win_ref, proj_sc):
    xf = x_ref[...]
    u = xf * lax.rsqrt(jnp.mean(xf * xf, axis=1, keepdims=True) + NORM_EPS) * ngain_ref[...]
    u16 = u.astype(bf16)
    for c0 in range(0, P_COLS, PROJ_COL_BLOCK):
        c1 = min(c0 + PROJ_COL_BLOCK, P_COLS)
        proj_sc[:, c0:c1] = _dot_nt(u16, win_ref[c0:c1, :])


def _prompt_kernel(x_ref, ngain_ref, win_ref, gbias_ref, mgain_ref, rgain_ref, sinks_ref, wout_ref,
                   fgain_ref, tril3_ref, maskadd_ref, dmat2_ref, rslab_ref, ubias_ref, cos_ref,
                   sin_ref, cosoff_ref, sinoff_ref, selh_ref, selp_ref,
                   y_ref, c_out, n_out, m_out, s_out, k_out, v_out,
                   proj_sc, mix_sc, cn_sc, sb_sc, m_sc, kp_sc, vp_sc, bias_sc, *, chunks, final):
    step = pl.program_id(1)
    last_step = pl.num_programs(1) - 1

    @pl.when(jnp.logical_and(pl.program_id(0) == 0, step == 0))
    def _():
        for blk in range(A_HEADS):
            u = ubias_ref[ATTN_HEAD_ORDER[blk]:ATTN_HEAD_ORDER[blk] + 1, :]
            bias_sc[blk * ROWS:(blk + 1) * ROWS, :] = jnp.concatenate(
                [_skew(u[:, :2 * WINDOW], ROWS), _skew(u[:, 2 * WINDOW:], ROWS)], axis=1)

    xf = x_ref[...]
    u16 = (xf * lax.rsqrt(jnp.mean(xf * xf, axis=1, keepdims=True) + NORM_EPS) * ngain_ref[...]).astype(bf16)
    col_blocks = [(c0, min(c0 + PROJ_COL_BLOCK, P_COLS)) for c0 in range(0, P_COLS, PROJ_COL_BLOCK)]
    half_rows = (chunks // 2) * ROWS if chunks > 1 else chunks * ROWS

    def project(r0, r1, c0, c1):
        proj_sc[r0:r1, c0:c1] = _dot_nt(u16[r0:r1], win_ref[c0:c1, :])

    for c0, c1 in col_blocks:
        project(0, half_rows, c0, c1)
    late_pieces = [(half_rows, chunks * ROWS, c0, c1) for c0, c1 in col_blocks] if half_rows < chunks * ROWS else []

    @pl.when(step == 0)
    def _():
        cn_sc[...] = jnp.zeros_like(cn_sc)
        sb_sc[...] = jnp.zeros_like(sb_sc)
        m_sc[...] = jnp.zeros_like(m_sc)
        kp_sc[...] = jnp.zeros_like(kp_sc)
        vp_sc[...] = jnp.zeros_like(vp_sc)

    lane = lax.broadcasted_iota(jnp.int32, (ROWS, LANES), 1)
    row = lax.broadcasted_iota(jnp.int32, (ROWS, LANES), 0)
    left = lane < HEAD_DIM
    first_half = (lane & (HEAD_DIM - 1)) < (HEAD_DIM // 2)
    head_col = lane < M_HEADS
    blockdiag = (row < HEAD_DIM) == left
    row2 = lax.broadcasted_iota(jnp.int32, (ROWS, 2 * LANES), 0)
    lane2w = lax.broadcasted_iota(jnp.int32, (ROWS, 2 * LANES), 1)
    left2 = (lane2w & (LANES - 1)) < HEAD_DIM
    blockdiag2 = (row2 < HEAD_DIM) == left2
    ones16 = jnp.ones((ROWS, LANES), bf16)
    halves, pick, pair_blockdiag, half_mean = _halves, _pick, _pair_blockdiag, _half_mean

    def chunk_body(ci):
        rows = slice(ci * ROWS, (ci + 1) * ROWS)

        def proj(off, width=LANES):
            return proj_sc[rows, off:off + width]

        gates = proj(OFF_G) + gbias_ref[...]
        bcum = _exact_tril_dot(tril3_ref[...], _log_sigmoid(gates))

        base = pl.ds(step * chunks + ci, 1)
        cos_a, sin_a = cos_ref[base, :], sin_ref[base, :]
        cos_b, sin_b = cosoff_ref[...], sinoff_ref[...]
        cos_t = cos_a * cos_b - sin_a * sin_b
        sin_t = sin_a * cos_b + cos_a * sin_b
        sin_t = jnp.where(first_half, -sin_t, sin_t)
        r_q = [_rope(proj(OFF_RQ + p * LANES), cos_t, sin_t, first_half) for p in range(PAIRS)]
        r_k = [_rope(proj(OFF_RK + p * LANES), cos_t, sin_t, first_half) * QK_SCALE for p in range(PAIRS)]
        r_vbd = pair_blockdiag([proj(OFF_RV + p * LANES).astype(bf16) for p in range(PAIRS)])
        r_sb = [sb_sc[p] for p in range(PAIRS)]
        r_sc = _dot_nt(jnp.concatenate([halves(q) for q in r_q], axis=1),
                       pair_blockdiag([k.astype(bf16) for k in r_k]))
        r_inter = _dot(jnp.concatenate([q.astype(bf16) for q in r_q], axis=1),
                       pair_blockdiag([sb.astype(bf16) for sb in r_sb]))
        r_upd = _dot(jnp.concatenate([(r_k[p] * rslab_ref[1, p]).T.astype(bf16) for p in range(PAIRS)], axis=1),
                     r_vbd)
        for p in range(PAIRS):
            sb_sc[p] = rslab_ref[2, p] * r_sb[p] + jnp.where(blockdiag, r_upd[:, p * LANES:(p + 1) * LANES], 0.0)
        yield

        kcur, vcur = proj(OFF_AK), proj(OFF_AV)
        kprev, vprev = kp_sc[...], vp_sc[...]
        kk16 = jnp.concatenate([kcur, kprev], axis=0).astype(bf16)
        vv16 = jnp.concatenate([jnp.concatenate([vcur.astype(bf16), ones16], axis=1),
                                jnp.concatenate([vprev.astype(bf16), ones16], axis=1)], axis=0)
        kp_sc[...] = kcur
        vp_sc[...] = vcur
        a_q = jnp.concatenate([halves(proj(OFF_AQ + j * LANES) * QK_SCALE) for j in range(KV_GROUP)], axis=0)
        a_s = _dot_nt(a_q, kk16) + bias_sc[...]
        if ci == 0:
            pen = jnp.where(step == 0, -jnp.inf, 0.0).astype(f32)
            a_s = a_s + jnp.where(lax.broadcasted_iota(jnp.int32, (1, 2 * ROWS), 1) >= ROWS, pen, 0.0)

        m_q = [proj(OFF_MQ + p * LANES) for p in range(PAIRS)]
        m_k = [proj(OFF_MK + p * LANES) * QK_SCALE for p in range(PAIRS)]
        m_qk = _dot_nt(jnp.concatenate([halves(q) for q in m_q], axis=1),
                       pair_blockdiag([k.astype(bf16) for k in m_k]))
        m_q16 = [q.astype(bf16) for q in m_q]
        m_v16 = [proj(OFF_MV + p * LANES).astype(bf16) for p in range(PAIRS)]
        yield

        zb = pltpu.roll(bcum, LANES - M_HEADS, axis=1)
        r_mat = jnp.where(head_col, gates - zb, 0.0)
        cm = r_mat
        sh = 1
        while sh < ROWS:
            cm = jnp.where(row >= sh, jnp.maximum(cm, pltpu.roll(cm, sh, axis=0)), cm)
            sh *= 2
        mprev = m_sc[...]
        mx = jnp.maximum(mprev, cm)
        gm = mprev - mx
        em = jnp.where(head_col, -(zb + mx), 0.0)
        mx_last = jnp.broadcast_to(mx[ROWS - 1:ROWS, :], (ROWS, LANES))
        m_sc[...] = jnp.where(head_col, jnp.broadcast_to((zb + mx)[ROWS - 1:ROWS, :], (ROWS, LANES)), 0.0)
        mx_b = _dot_nt(_split_terms(mx, SEL_TERMS), selh_ref[...])
        slabs = jnp.exp(_dot_nt(_split_terms(jnp.concatenate([gm, em, r_mat - mx_last], axis=0), SEL_TERMS),
                                selp_ref[...]))
        winter_b, emt_b, ws_b = slabs[:ROWS], slabs[ROWS:2 * ROWS], slabs[2 * ROWS:]
        r_t = r_mat.T
        yield

        outs = []
        r_acc = []
        r_o = _dot((r_sc * dmat2_ref[...]).astype(bf16), r_vbd)
        for p in range(PAIRS):
            ps = slice(p * LANES, (p + 1) * LANES)
            r_acc.append(pick(r_o[:, ps]) + rslab_ref[0, p] * r_inter[:, ps])

        a_out = []
        a_p = []
        for blk in range(A_HEADS):
            s = a_s[blk * ROWS:(blk + 1) * ROWS]
            sink = sinks_ref[ATTN_HEAD_ORDER[blk]]
            m = jnp.maximum(jnp.max(jnp.maximum(s[:, :ROWS], s[:, ROWS:]), axis=1, keepdims=True), sink)
            a_p.append(jnp.exp(s - m).astype(bf16))
            a_out.append(jnp.exp(sink - m))
        a_pv = _dot(jnp.concatenate(a_p, axis=0), vv16)
        yield

        maskadd = maskadd_ref[...]
        for p in range(PAIRS):
            ps = slice(p * LANES, (p + 1) * LANES)
            cn = cn_sc[p]
            w = jnp.concatenate(
                [jnp.exp((r_t[2 * p + side:2 * p + side + 1, :] + maskadd)
                         - mx_b[:, (2 * p + side) * ROWS:(2 * p + side + 1) * ROWS]) for side in range(2)],
                axis=0) * m_qk[:, ps]
            acc = (pick(_dot(w.astype(bf16), jnp.concatenate([m_v16[p], ones16], axis=1)))
                   + jnp.concatenate([winter_b[:, ps]] * 2, axis=1) * _dot(m_q16[p], cn.astype(bf16)))
            hh = acc[:, :LANES] / jnp.maximum(jnp.abs(acc[:, LANES:]), emt_b[:, ps])
            outs.append(_sigmoid(proj(OFF_MO + p * LANES)) * hh)
            kwt16 = (m_k[p] * ws_b[:, ps]).T.astype(bf16)
            dcn = _dot(kwt16, jnp.concatenate([m_v16[p], ones16], axis=1))
            decay = winter_b[ROWS - 1:ROWS, ps]
            cn_sc[p] = jnp.concatenate([decay, decay], axis=1) * cn + jnp.where(blockdiag2, dcn, 0.0)
        outs.extend(r_acc)
        yield

        x4 = jnp.concatenate(outs, axis=0)
        xc = x4 - half_mean(x4)
        y4 = xc * lax.rsqrt(half_mean(xc * xc) + NORM_EPS)
        for i in range(2 * PAIRS):
            gain = (mgain_ref if i < PAIRS else rgain_ref)[:, (i % PAIRS) * LANES:(i % PAIRS + 1) * LANES]
            zoff = (OFF_MZ if i < PAIRS else OFF_RZ) + (i % PAIRS) * LANES
            out = y4[i * ROWS:(i + 1) * ROWS] * gain * _silu(proj(zoff))
            mix_sc[rows, i * LANES:(i + 1) * LANES] = out.astype(bf16)
        for j in range(KV_GROUP):
            acc = pick(a_pv[2 * j * ROWS:(2 * j + 2) * ROWS])
            den = acc[:, LANES:] + jnp.where(left, a_out[2 * j], a_out[2 * j + 1])
            out = (acc[:, :LANES] / den) * _silu(proj(OFF_AZ + j * LANES))
            mix_sc[rows, M_DIM + R_DIM + j * LANES:M_DIM + R_DIM + (j + 1) * LANES] = out.astype(bf16)

    def out_project(r0, r1, c0, c1):
        y_ref[r0:r1, c0:c1] = x_ref[r0:r1, c0:c1] + _dot(mix_sc[r0:r1, :], wout_ref[:, c0:c1])

    out_blocks = [(c0, min(c0 + PROJ_COL_BLOCK, D_MODEL)) for c0 in range(0, D_MODEL, PROJ_COL_BLOCK)]
    early_out = [(0, half_rows, c0, c1) for c0, c1 in out_blocks] if half_rows < chunks * ROWS else []
    final_out = [(half_rows if early_out else 0, chunks * ROWS, c0, c1) for c0, c1 in out_blocks]

    def fill_mxu(ci):
        if late_pieces:
            project(*late_pieces.pop(0))
        elif early_out and (ci - 1) * ROWS >= half_rows:
            out_project(*early_out.pop(0))

    parts = [chunk_body(ci) for ci in range(chunks)]
    for ci in range(chunks + 1):
        if ci * ROWS >= half_rows:
            while late_pieces:
                project(*late_pieces.pop(0))
        for which in SCHEDULE:
            if which == "F" and ci < chunks:
                next(parts[ci])
                fill_mxu(ci)
            if which == "B" and ci > 0:
                next(parts[ci - 1], None)
                fill_mxu(ci)
    for piece in early_out + final_out:
        out_project(*piece)
    if final:
        y = y_ref[...]
        y_ref[...] = y * lax.rsqrt(jnp.mean(y * y, axis=1, keepdims=True) + NORM_EPS) * fgain_ref[...]

    @pl.when(step == last_step)
    def _():
        for p in range(PAIRS):
            cn = cn_sc[p]
            sb = sb_sc[p]
            n_t = cn[:, LANES:].T
            for side in range(2):
                h = 2 * p + side
                blk = slice(side * HEAD_DIM, (side + 1) * HEAD_DIM)
                c_out[0, h] = cn[blk, blk]
                s_out[0, h] = sb[blk, blk]
                n_out[0, h:h + 1, :] = n_t[side * HEAD_DIM:side * HEAD_DIM + 1, blk]
        m_out[0] = m_sc[...]
        k_out[0] = kp_sc[...]
        v_out[0] = vp_sc[...]


def _lane_is_left(shape):
    return (lax.broadcasted_iota(jnp.int32, shape, 1) & (LANES - 1)) < HEAD_DIM


def _halves(x):
    left = _lane_is_left(x.shape)
    return jnp.concatenate([jnp.where(left, x, 0.0), jnp.where(left, 0.0, x)], axis=0).astype(bf16)


def _pick(x):
    return jnp.where(_lane_is_left((ROWS, x.shape[1])), x[:ROWS], x[ROWS:])


def _pair_blockdiag(blocks):
    z = jnp.zeros_like(blocks[0])
    return jnp.concatenate(
        [jnp.concatenate([blk if j == i else z for j in range(len(blocks))], axis=1)
         for i, blk in enumerate(blocks)], axis=0)


def _half_mean(x):
    left = _lane_is_left(x.shape)
    s_left = jnp.sum(jnp.where(left, x, 0.0), axis=1, keepdims=True)
    s_right = jnp.sum(jnp.where(left, 0.0, x), axis=1, keepdims=True)
    return jnp.where(left, s_left, s_right) * (1.0 / HEAD_DIM)


def _group_last(x, groups):
    n = x.shape[1]
    glen = ROWS // groups
    x3 = x.reshape(groups, glen, n)
    return jnp.broadcast_to(x3[:, glen - 1:glen, :], (groups, glen, n)).reshape(ROWS, n)


def _sample_kernel(x_ref, ngain_ref, win_ref, gbias_ref, mgain_ref, rgain_ref, sinks_ref, wout_ref,
                   fgain_ref, tril3_ref, maskadd_ref, dmat2_ref, rslab_ref, ubias_ref, cos_ref, sin_ref,
                   selh_ref, selp_ref,
                   c_in, n_in, m_in, s_in, k_in, v_in,
                   y_ref, c_out, n_out, m_out, s_out, k_out, v_out,
                   proj_sc, mix_sc, xcur_sc, xall_sc, qb_sc, sp_sc, pp_sc, ob_sc, biasc_ref, biasp_ref, *, groups,
                   proj_groups, ret_full):
    glen = ROWS // groups
    glen_log2 = glen.bit_length() - 1
    hd_log2 = HEAD_DIM.bit_length() - 1
    layer = pl.program_id(0)
    step = pl.program_id(1)
    last_layer = pl.num_programs(0) - 1
    xrows = pl.ds(pl.multiple_of(step * ROWS, ROWS), ROWS)
    sub = lax.rem(step, proj_groups)
    rows = pl.ds(pl.multiple_of(sub * ROWS, ROWS), ROWS)
    wide = pl.ds(pl.multiple_of((step - sub) * ROWS, ROWS), proj_groups * ROWS)

    @pl.when(jnp.logical_and(sub == 0, layer == 0))
    def _():
        _rms_project(x_ref, ngain_ref, win_ref, proj_sc)

    @pl.when(jnp.logical_and(sub == 0, layer != 0))
    def _():
        _rms_project(xall_sc.at[wide, :], ngain_ref, win_ref, proj_sc)

    @pl.when(layer == 0)
    def _():
        xcur_sc[...] = x_ref[rows, :]

    @pl.when(layer != 0)
    def _():
        xcur_sc[...] = xall_sc[xrows, :]

    for b in range(groups):
        k_out[b] = pltpu.roll(k_in[b], WINDOW - glen, axis=1)
        v_out[b] = pltpu.roll(v_in[b], WINDOW - glen, axis=1)

    @pl.when(jnp.logical_and(step == 0, layer == 0))
    def _():
        for h in range(A_HEADS):
            u = ubias_ref[h:h + 1, :]
            biasc_ref[h] = _skew(u[:, :2 * WINDOW], ROWS) + maskadd_ref[...]
            biasp_ref[h] = jnp.concatenate([_skew(u[:, 2 * WINDOW:], glen)] * groups, axis=0)

    lane = lax.broadcasted_iota(jnp.int32, (ROWS, LANES), 1)
    first_half = (lane & (HEAD_DIM - 1)) < (HEAD_DIM // 2)

    r_i = lax.broadcasted_iota(jnp.int32, (ROWS, groups * HEAD_DIM), 0)
    c_i = lax.broadcasted_iota(jnp.int32, (ROWS, groups * HEAD_DIM), 1)
    blk = (r_i >> glen_log2) == (c_i >> hd_log2)
    r_t = lax.broadcasted_iota(jnp.int32, (groups * HEAD_DIM, ROWS), 0)
    c_t = lax.broadcasted_iota(jnp.int32, (groups * HEAD_DIM, ROWS), 1)
    blk_t = (r_t >> hd_log2) == (c_t >> glen_log2)

    def q_times_state(qh, st):
        qt = jnp.where(blk, jnp.concatenate([qh] * groups, axis=1), 0.0)
        return _dot(qt.astype(bf16), st.astype(bf16))

    def state_increment(kt_h, vh16):
        kt = jnp.where(blk_t, jnp.concatenate([kt_h] * groups, axis=0), 0.0)
        return _dot(kt.astype(bf16), vh16)

    def proj(off, width=LANES):
        return proj_sc[rows, off:off + width]

    def head_state_rows(slab, side):
        wide = slab[:, side * HEAD_DIM:(side + 1) * HEAD_DIM].reshape(groups, glen, HEAD_DIM)[:, 0:1, :]
        rows_ = jnp.broadcast_to(wide, (groups, HEAD_DIM, HEAD_DIM)).reshape(groups * HEAD_DIM, HEAD_DIM)
        return rows_, wide.reshape(groups, HEAD_DIM)

    row = lax.broadcasted_iota(jnp.int32, (ROWS, LANES), 0)
    tau = row & (glen - 1)
    head_col = lane < M_HEADS
    left = lane < HEAD_DIM
    ones16 = jnp.ones((ROWS, LANES), bf16)
    gates = proj(OFF_G) + gbias_ref[...]
    bcum = _exact_tril_dot(tril3_ref[...], _log_sigmoid(gates))
    zb = pltpu.roll(bcum, LANES - M_HEADS, axis=1)
    r_mat = jnp.where(head_col, gates - zb, 0.0)
    cm = r_mat
    sh = 1
    while sh < glen:
        cm = jnp.where(tau >= sh, jnp.maximum(cm, pltpu.roll(cm, sh, axis=0)), cm)
        sh *= 2
    mprev = m_in[...]
    mx = jnp.maximum(mprev, cm)
    gm = mprev - mx
    em = jnp.where(head_col, -(zb + mx), 0.0)
    mx_last = _group_last(mx, groups)
    m_out[...] = jnp.where(head_col, _group_last(zb + mx, groups), 0.0)
    mx_b = _dot_nt(_split_terms(mx, SEL_TERMS), selh_ref[...])
    slabs = jnp.exp(_dot_nt(_split_terms(jnp.concatenate([gm, em, r_mat - mx_last], axis=0), SEL_TERMS), selp_ref[...]))
    winter_b, emt_b, ws_b = slabs[:ROWS], slabs[ROWS:2 * ROWS], slabs[2 * ROWS:]
    decay_b = _group_last(winter_b, groups)
    r_t = r_mat.T
    maskadd = maskadd_ref[...]

    m_q = [proj(OFF_MQ + p * LANES) for p in range(PAIRS)]
    m_k = [proj(OFF_MK + p * LANES) * QK_SCALE for p in range(PAIRS)]
    m_v = [proj(OFF_MV + p * LANES) for p in range(PAIRS)]
    m_qk = _dot_nt(jnp.concatenate([_halves(q) for q in m_q], axis=1),
                   _pair_blockdiag([k.astype(bf16) for k in m_k]))
    outs = []
    for p in range(PAIRS):
        ps = slice(p * LANES, (p + 1) * LANES)
        w = jnp.concatenate(
            [jnp.exp((r_t[2 * p + side:2 * p + side + 1, :] + maskadd)
                     - mx_b[:, (2 * p + side) * ROWS:(2 * p + side + 1) * ROWS]) for side in range(2)],
            axis=0) * m_qk[:, ps]
        intra = _pick(_dot(w.astype(bf16), jnp.concatenate([m_v[p].astype(bf16), ones16], axis=1)))
        kw = m_k[p] * ws_b[:, ps]
        kwt = kw.T
        q_c, q_n = [], []
        for side in range(2):
            h = 2 * p + side
            hs = slice(side * HEAD_DIM, (side + 1) * HEAD_DIM)
            qh = m_q[p][:, hs]
            c_h = c_in[:, h].reshape(groups * HEAD_DIM, HEAD_DIM)
            n_g = n_in[h]
            n_rows = jnp.broadcast_to(n_g.reshape(groups, 1, HEAD_DIM),
                                      (groups, glen, HEAD_DIM)).reshape(ROWS, HEAD_DIM)
            q_c.append(q_times_state(qh, c_h))
            q_n.append(jnp.sum(qh * n_rows, axis=1, keepdims=True))
            dec_rows, dec_g = head_state_rows(decay_b[:, ps], side)
            c_new = dec_rows * c_h + state_increment(kwt[hs, :], m_v[p][:, hs].astype(bf16))
            c_out[:, h] = c_new.reshape(groups, HEAD_DIM, HEAD_DIM)
            n_out[h] = dec_g * n_g + jnp.sum(kw[:, hs].reshape(groups, glen, HEAD_DIM), axis=1)
        wb = winter_b[:, ps]
        num = intra[:, :LANES] + wb * jnp.concatenate(q_c, axis=1)
        nq = intra[:, LANES:] + wb * jnp.where(left, q_n[0], q_n[1])
        outs.append(_sigmoid(proj(OFF_MO + p * LANES)) * (num / jnp.maximum(jnp.abs(nq), emt_b[:, ps])))

    cos_t, sin_t = cos_ref[...], sin_ref[...]
    r_q = [_rope(proj(OFF_RQ + p * LANES), cos_t, sin_t, first_half) for p in range(PAIRS)]
    r_k = [_rope(proj(OFF_RK + p * LANES), cos_t, sin_t, first_half) * QK_SCALE for p in range(PAIRS)]
    r_v = [proj(OFF_RV + p * LANES) for p in range(PAIRS)]
    r_sc = _dot_nt(jnp.concatenate([_halves(q) for q in r_q], axis=1),
                   _pair_blockdiag([k.astype(bf16) for k in r_k]))
    r_o = _dot((r_sc * dmat2_ref[...]).astype(bf16), _pair_blockdiag([v.astype(bf16) for v in r_v]))
    for p in range(PAIRS):
        ps = slice(p * LANES, (p + 1) * LANES)
        rkt = (r_k[p] * rslab_ref[1, p]).T
        q_s = []
        for side in range(2):
            h = 2 * p + side
            hs = slice(side * HEAD_DIM, (side + 1) * HEAD_DIM)
            s_h = s_in[:, h].reshape(groups * HEAD_DIM, HEAD_DIM)
            q_s.append(q_times_state(r_q[p][:, hs], s_h))
            s_new = ret_full[h] * s_h + state_increment(rkt[hs, :], r_v[p][:, hs].astype(bf16))
            s_out[:, h] = s_new.reshape(groups, HEAD_DIM, HEAD_DIM)
        outs.append(_pick(r_o[:, ps]) + rslab_ref[0, p] * jnp.concatenate(q_s, axis=1))

    x4 = jnp.concatenate(outs, axis=0)
    xc = x4 - _half_mean(x4)
    y4 = xc * lax.rsqrt(_half_mean(xc * xc) + NORM_EPS)
    for i in range(2 * PAIRS):
        gain = (mgain_ref if i < PAIRS else rgain_ref)[:, (i % PAIRS) * LANES:(i % PAIRS + 1) * LANES]
        zoff = (OFF_MZ if i < PAIRS else OFF_RZ) + (i % PAIRS) * LANES
        mix_sc[:, i * LANES:(i + 1) * LANES] = (y4[i * ROWS:(i + 1) * ROWS] * gain * _silu(proj(zoff))).astype(bf16)

    kcur = proj_sc[rows, OFF_AK:OFF_AK + A_KV_DIM]
    vcur = proj_sc[rows, OFF_AV:OFF_AV + A_KV_DIM]
    kcur16, vcur16 = kcur.astype(bf16), vcur.astype(bf16)

    a_q = {}
    for j in range(KV_GROUP):
        hq = _halves(proj(OFF_AQ + j * LANES) * QK_SCALE)
        a_q[j], a_q[KV_GROUP + j] = hq[:ROWS], hq[ROWS:]
    for h in range(A_HEADS):
        qb_sc[:, h * glen:(h + 1) * glen, :] = a_q[h].reshape(groups, glen, LANES)
    sc_all = _dot_nt(jnp.concatenate([a_q[h] for h in range(A_HEADS)], axis=0), kcur16)

    for b in range(groups):
        sp = _dot(qb_sc[b], k_in[b].astype(bf16))
        sp_sc[:, b * glen:(b + 1) * glen, :] = sp.reshape(A_HEADS, glen, WINDOW)

    vcur1 = jnp.concatenate([vcur16, ones16], axis=1)
    esink, o_cur = [], []
    for h in range(A_HEADS):
        sc = sc_all[h * ROWS:(h + 1) * ROWS] + biasc_ref[h]
        sp = sp_sc[h] + biasp_ref[h]
        sink = sinks_ref[layer, h]
        m = jnp.maximum(jnp.max(jnp.maximum(sc, sp), axis=1, keepdims=True), sink)
        esink.append(jnp.exp(sink - m))
        o_cur.append(_dot(jnp.exp(sc - m).astype(bf16), vcur1))
        pp_sc[:, h * glen:(h + 1) * glen, :] = jnp.exp(sp - m).reshape(groups, glen, WINDOW).astype(bf16)

    for b in range(groups):
        v1 = jnp.concatenate([v_in[b].astype(bf16), ones16], axis=0)
        ob = _dot_nt(pp_sc[b], v1)
        ob_sc[:, b * glen:(b + 1) * glen, :] = ob.reshape(A_HEADS, glen, 2 * LANES)
    norm = []
    for h in range(A_HEADS):
        acc = o_cur[h] + ob_sc[h]
        norm.append(acc[:, :LANES] / (acc[:, LANES:] + esink[h]))
    outs = [jnp.where(left, norm[j], norm[KV_GROUP + j]) for j in range(KV_GROUP)]
    kcur_t, vcur_t = kcur.T, vcur.T
    new = slice(WINDOW - glen, WINDOW)
    for b in range(groups):
        shift = (WINDOW - glen - b * glen) % LANES
        k_out[b, :, new] = pltpu.roll(kcur_t, shift, axis=1)[:, new]
        v_out[b, :, new] = pltpu.roll(vcur_t, shift, axis=1)[:, new]
    out_a = jnp.concatenate(outs, axis=1) * _silu(proj_sc[rows, OFF_AZ:OFF_AZ + A_DIM])
    mix_sc[:, M_DIM + R_DIM:M_DIM + R_DIM + A_DIM] = out_a.astype(bf16)

    y = xcur_sc[...] + _dot(mix_sc[...], wout_ref[...])

    @pl.when(layer != last_layer)
    def _():
        xall_sc[xrows, :] = y
        y_ref[...] = y

    @pl.when(layer == last_layer)
    def _():
        y_ref[...] = y * lax.rsqrt(jnp.mean(y * y, axis=1, keepdims=True) + NORM_EPS) * fgain_ref[...]


def _t5_bucket(dist):
    max_exact = N_BUCKETS // 2
    d = np.maximum(dist, 1).astype(np.float32)
    large = max_exact + (np.log(d / max_exact) / np.log(REL_MAX_DIST / max_exact)
                         * (N_BUCKETS - max_exact)).astype(np.int32)
    large = np.minimum(large, N_BUCKETS - 1)
    return np.where(dist < max_exact, dist, large).astype(np.int32)


def _static_tables(groups):
    glen = ROWS // groups
    r = np.arange(ROWS)
    grp, tau = r // glen, r % glen
    causal = (grp[:, None] == grp[None, :]) & (tau[None, :] <= tau[:, None])
    tril = causal.astype(np.float32)
    maskadd = np.where(causal, 0.0, -np.inf).astype(np.float32)
    log_g = np.log1p(-np.exp2(-5.0 - np.arange(R_HEADS, dtype=np.float64)))
    diff = (tau[:, None] - tau[None, :]).astype(np.float64)
    dmat = np.where(causal[None], np.exp(log_g[:, None, None] * np.maximum(diff, 0.0)[None]), 0.0)
    inter = np.exp(log_g[None, :] * (tau[:, None] + 1.0))
    tail = np.exp(log_g[None, :] * (glen - 1.0 - tau[:, None]))
    full = np.exp(log_g * glen)
    lane_head = np.arange(LANES) // HEAD_DIM
    rslab = np.zeros((3, PAIRS, ROWS, LANES), np.float64)
    for p in range(PAIRS):
        rslab[0, p] = inter[:, 2 * p + lane_head]
        rslab[1, p] = tail[:, 2 * p + lane_head]
        rslab[2, p] = full[2 * p + lane_head][None, :]
    selh = np.zeros((M_HEADS * ROWS, SEL_TERMS * LANES), np.float32)
    selp = np.zeros((PAIRS * LANES, SEL_TERMS * LANES), np.float32)
    for t in range(SEL_TERMS):
        for h in range(M_HEADS):
            selh[h * ROWS:(h + 1) * ROWS, t * LANES + h] = 1.0
        for p in range(PAIRS):
            for side in range(2):
                selp[p * LANES + side * HEAD_DIM:p * LANES + (side + 1) * HEAD_DIM, t * LANES + 2 * p + side] = 1.0
    return dict(tril3=jnp.asarray(np.concatenate([tril] * SPLIT_TERMS, axis=1), bf16),
                maskadd=maskadd,
                dmat2=np.concatenate(list(dmat.astype(np.float32).reshape(PAIRS, 2 * ROWS, ROWS)), axis=1),
                rslab=rslab.astype(np.float32),
                full=tuple(float(v) for v in full),
                selh=jnp.asarray(selh, bf16), selp=jnp.asarray(selp, bf16))


def _bias_vectors(rel_table):
    tb = jnp.transpose(rel_table[_t5_bucket(np.arange(WINDOW))]).astype(f32)
    ninf = jnp.full((A_HEADS, WINDOW), -jnp.inf, f32)
    rev = tb[:, :0:-1]
    return jnp.concatenate([tb[:, :1], ninf, rev, ninf[:, :1], rev, ninf], axis=1)


def _skew(u_row, rows):
    x = jnp.broadcast_to(u_row, (rows, 2 * WINDOW))
    return pltpu.roll(x, 0, 1, stride=1, stride_axis=0)[:, :WINDOW]


def _rope_tables(pos, signed=True):
    half = HEAD_DIM // 2
    inv = ROPE_BASE ** (-jnp.arange(half, dtype=f32) / half)
    ang = pos.astype(f32)[:, None] * inv[None, :]
    cos, sin = jnp.cos(ang), jnp.sin(ang)
    reps = LANES // HEAD_DIM
    cos_t = jnp.tile(jnp.concatenate([cos, cos], axis=1), (1, reps))
    sin_t = jnp.tile(jnp.concatenate([-sin if signed else sin, sin], axis=1), (1, reps))
    return cos_t, sin_t


def _const_spec(shape, nargs):
    zeros = (0,) * len(shape)
    if nargs == 1:
        return pl.BlockSpec(shape, lambda i: zeros)
    return pl.BlockSpec(shape, lambda i, j: zeros)


def _layer_spec(shape, layer, nargs):
    idx = (layer,) + (0,) * len(shape)
    if nargs == 1:
        return pl.BlockSpec((None,) + shape, lambda i: idx)
    return pl.BlockSpec((None,) + shape, lambda i, j: idx)


def _param_specs(layer, nargs):
    ls = functools.partial(_layer_spec, layer=layer, nargs=nargs)
    return [ls((1, D_MODEL)), ls((P_COLS, D_MODEL)), ls((1, LANES)), ls((1, M_DIM)), ls((1, R_DIM)),
            pl.BlockSpec(memory_space=pltpu.SMEM), ls((D_MODEL, D_MODEL)), _const_spec((1, D_MODEL), nargs)]


def _param_args(p, layer):
    return (p["norm_gain"], p["w_in"], p["gbias"], p["m_gain"], p["r_gain"], p["sinks"][layer], p["w_out"],
            p["fgain"])


def _prompt_layer(x, p, layer, tabs, ubias, rope, final):
    B, T, _ = x.shape
    tb = min(PROMPT_ROWS, T)
    chunks = tb // ROWS
    nt = T // tb
    cs = functools.partial(_const_spec, nargs=2)
    in_specs = [pl.BlockSpec((None, tb, D_MODEL), lambda b, t: (b, t, 0))] + _param_specs(layer, 2) + [
        cs((ROWS, SPLIT_TERMS * ROWS)), cs((ROWS, ROWS)), cs((2 * ROWS, PAIRS * ROWS)),
        cs((3, PAIRS, ROWS, LANES)), cs((A_HEADS, 4 * WINDOW)),
        cs((T // ROWS, LANES)), cs((T // ROWS, LANES)), cs((ROWS, LANES)), cs((ROWS, LANES)),
        cs((M_HEADS * ROWS, SEL_TERMS * LANES)), cs((PAIRS * LANES, SEL_TERMS * LANES)),
    ]
    out_shape = (
        jax.ShapeDtypeStruct((B, T, D_MODEL), f32),
        jax.ShapeDtypeStruct((B, M_HEADS, HEAD_DIM, HEAD_DIM), f32),
        jax.ShapeDtypeStruct((B, M_HEADS, HEAD_DIM), f32),
        jax.ShapeDtypeStruct((B, ROWS, LANES), f32),
        jax.ShapeDtypeStruct((B, R_HEADS, HEAD_DIM, HEAD_DIM), f32),
        jax.ShapeDtypeStruct((B, WINDOW, A_KV_DIM), f32),
        jax.ShapeDtypeStruct((B, WINDOW, A_KV_DIM), f32),
    )
    out_specs = (
        pl.BlockSpec((None, tb, D_MODEL), lambda b, t: (b, t, 0)),
        pl.BlockSpec((1, M_HEADS, HEAD_DIM, HEAD_DIM), lambda b, t: (b, 0, 0, 0)),
        pl.BlockSpec((1, M_HEADS, HEAD_DIM), lambda b, t: (b, 0, 0)),
        pl.BlockSpec((1, ROWS, LANES), lambda b, t: (b, 0, 0)),
        pl.BlockSpec((1, R_HEADS, HEAD_DIM, HEAD_DIM), lambda b, t: (b, 0, 0, 0)),
        pl.BlockSpec((1, WINDOW, A_KV_DIM), lambda b, t: (b, 0, 0)),
        pl.BlockSpec((1, WINDOW, A_KV_DIM), lambda b, t: (b, 0, 0)),
    )
    kern = functools.partial(_prompt_kernel, chunks=chunks, final=final)
    y, c, n, m, s, k, v = pl.pallas_call(
        kern, grid=(B, nt), in_specs=in_specs, out_specs=out_specs, out_shape=out_shape,
        scratch_shapes=[pltpu.VMEM((tb, P_COLS), f32), pltpu.VMEM((tb, D_MODEL), bf16),
                        pltpu.VMEM((PAIRS, ROWS, 2 * LANES), f32), pltpu.VMEM((PAIRS, ROWS, LANES), f32),
                        pltpu.VMEM((ROWS, LANES), f32), pltpu.VMEM((ROWS, A_KV_DIM), f32),
                        pltpu.VMEM((ROWS, A_KV_DIM), f32), pltpu.VMEM((A_HEADS * ROWS, 2 * WINDOW), f32)],
        compiler_params=pltpu.CompilerParams(dimension_semantics=("arbitrary", "arbitrary"),
                                             vmem_limit_bytes=VMEM_LIMIT_BYTES),
        name="prompt_layer",
    )(x, *_param_args(p, layer), tabs["tril3"], tabs["maskadd"], tabs["dmat2"], tabs["rslab"], ubias,
      *rope, tabs["selh"], tabs["selp"])
    k = k.reshape(B, WINDOW, A_KV_HEADS, HEAD_DIM)
    v = v.reshape(B, WINDOW, A_KV_HEADS, HEAD_DIM)
    return y, c, n, m[:, 0, :M_HEADS], s, k, v


def _batch_major_kernel(*refs):
    half = len(refs) // 2
    for x_ref, o_ref in zip(refs[:half], refs[half:]):
        o_ref[...] = jnp.transpose(x_ref[...], (2, 0, 1))


def _batch_major(*states):
    depth, B, heads, d, e = states[0].shape
    assert all(s.shape == states[0].shape for s in states)
    views = [jnp.transpose(s, (0, 2, 3, 4, 1)) for s in states]
    return pl.pallas_call(
        _batch_major_kernel, grid=(depth, heads),
        in_specs=[pl.BlockSpec((None, None, d, e, B), lambda l, h: (l, h, 0, 0, 0))] * len(states),
        out_specs=[pl.BlockSpec((None, B, None, d, e), lambda l, h: (l, 0, h, 0, 0))] * len(states),
        out_shape=[jax.ShapeDtypeStruct(s.shape, s.dtype) for s in states],
        compiler_params=pltpu.CompilerParams(dimension_semantics=("arbitrary", "arbitrary"),
                                             vmem_limit_bytes=VMEM_LIMIT_BYTES),
        name="batch_major",
    )(*views)


def _sample_path(x, states, p, tabs, ubias, cos_t, sin_t):
    B, T, _ = x.shape
    groups = ROWS // T
    nb = B // groups
    pg = math.gcd(nb, DECODE_PROJ_GROUPS)
    c0, n0, m0, s0, k0, v0 = states
    depth = c0.shape[0]
    c0, s0 = _batch_major(c0, s0)
    x2 = x.reshape(B * T, D_MODEL)
    n0t = jnp.transpose(n0, (0, 2, 1, 3))
    m0r = jnp.pad(jnp.repeat(m0, T, axis=1), ((0, 0), (0, 0), (0, LANES - M_HEADS)))
    k0r = jnp.transpose(k0, (0, 1, 3, 4, 2)).reshape(depth, B, A_KV_DIM, WINDOW)
    v0r = jnp.transpose(v0, (0, 1, 3, 4, 2)).reshape(depth, B, A_KV_DIM, WINDOW)

    def cs(shape):
        zeros = (0,) * len(shape)
        return pl.BlockSpec(shape, lambda l, i: zeros)

    def per_layer(shape, buffers=None):
        zeros = (0,) * len(shape)
        mode = {} if buffers is None else dict(pipeline_mode=pl.Buffered(buffers))
        return pl.BlockSpec((None,) + shape, lambda l, i: (l,) + zeros, **mode)

    st4 = pl.BlockSpec((None, groups, M_HEADS, HEAD_DIM, HEAD_DIM), lambda l, i: (l, i, 0, 0, 0))
    stn = pl.BlockSpec((None, M_HEADS, groups, HEAD_DIM), lambda l, i: (l, 0, i, 0))
    stm = pl.BlockSpec((None, ROWS, LANES), lambda l, i: (l, i, 0))
    stk = pl.BlockSpec((None, groups, A_KV_DIM, WINDOW), lambda l, i: (l, i, 0, 0))
    in_specs = [
        pl.BlockSpec((pg * ROWS, D_MODEL), lambda l, i: (i // pg, 0)),
        per_layer((1, D_MODEL)), per_layer((P_COLS, D_MODEL), 1), per_layer((1, LANES)), per_layer((1, M_DIM)),
        per_layer((1, R_DIM)), pl.BlockSpec(memory_space=pltpu.SMEM), per_layer((D_MODEL, D_MODEL), 1),
        cs((1, D_MODEL)),
        cs((ROWS, SPLIT_TERMS * ROWS)), cs((ROWS, ROWS)), cs((2 * ROWS, PAIRS * ROWS)),
        cs((3, PAIRS, ROWS, LANES)), cs((A_HEADS, 4 * WINDOW)),
        cs((ROWS, LANES)), cs((ROWS, LANES)),
        cs((M_HEADS * ROWS, SEL_TERMS * LANES)), cs((PAIRS * LANES, SEL_TERMS * LANES)),
        st4, stn, stm, st4, stk, stk,
    ]
    out_shape = (
        jax.ShapeDtypeStruct((B * T, D_MODEL), f32),
        jax.ShapeDtypeStruct((depth, B, M_HEADS, HEAD_DIM, HEAD_DIM), f32),
        jax.ShapeDtypeStruct((depth, M_HEADS, B, HEAD_DIM), f32),
        jax.ShapeDtypeStruct((depth, B * T, LANES), f32),
        jax.ShapeDtypeStruct((depth, B, R_HEADS, HEAD_DIM, HEAD_DIM), f32),
        jax.ShapeDtypeStruct((depth, B, A_KV_DIM, WINDOW), f32),
        jax.ShapeDtypeStruct((depth, B, A_KV_DIM, WINDOW), f32),
    )
    y_spec = pl.BlockSpec((ROWS, D_MODEL), lambda l, i: (jnp.where(l == depth - 1, i, 0), 0))
    out_specs = (y_spec, st4, stn, stm, st4, stk, stk)
    kern = functools.partial(_sample_kernel, groups=groups, proj_groups=pg, ret_full=tabs["full"])
    y, c, n, m, s, k, v = pl.pallas_call(
        kern, grid=(depth, nb), in_specs=in_specs, out_specs=out_specs, out_shape=out_shape,
        scratch_shapes=[pltpu.VMEM((pg * ROWS, P_COLS), f32), pltpu.VMEM((ROWS, D_MODEL), bf16),
                        pltpu.VMEM((ROWS, D_MODEL), f32), pltpu.VMEM((B * T, D_MODEL), f32),
                        pltpu.VMEM((groups, A_HEADS * T, A_KV_DIM), bf16),
                        pltpu.VMEM((A_HEADS, ROWS, WINDOW), f32),
                        pltpu.VMEM((groups, A_HEADS * T, WINDOW), bf16),
                        pltpu.VMEM((A_HEADS, ROWS, 2 * LANES), f32),
                        pltpu.VMEM((A_HEADS, ROWS, ROWS), f32), pltpu.VMEM((A_HEADS, ROWS, WINDOW), f32)],
        compiler_params=pltpu.CompilerParams(dimension_semantics=("arbitrary", "arbitrary"),
                                             vmem_limit_bytes=DECODE_VMEM_LIMIT_BYTES),
        name="sample_path",
    )(x2, p["norm_gain"], p["w_in"], p["gbias"], p["m_gain"], p["r_gain"], p["sinks"], p["w_out"], p["fgain"],
      tabs["tril3"], tabs["maskadd"], tabs["dmat2"], tabs["rslab"], ubias,
      cos_t, sin_t, tabs["selh"], tabs["selp"], c0, n0t, m0r, s0, k0r, v0r)
    y = y.reshape(B, T, D_MODEL)
    n = jnp.transpose(n, (0, 2, 1, 3))
    m = m.reshape(depth, B, T, LANES)[:, :, 0, :M_HEADS]
    k = jnp.transpose(k.reshape(depth, B, A_KV_HEADS, HEAD_DIM, WINDOW), (0, 1, 4, 2, 3))
    v = jnp.transpose(v.reshape(depth, B, A_KV_HEADS, HEAD_DIM, WINDOW), (0, 1, 4, 2, 3))
    return y, c, n, m, s, k, v


def _prepare_params(norm_gain, w_in, mlstm_gate_bias, mlstm_norm_gain, ret_norm_gain, attn_sinks, w_out,
                    final_norm_gain):
    depth = w_in.shape[0]
    w_t = jnp.swapaxes(w_in, 1, 2)
    split = OFF_G + N_GATES
    aq0 = split + 4 * R_DIM
    akv0 = aq0 + A_DIM
    az0 = akv0 + 2 * A_KV_DIM

    def by_head(w):
        w = w.reshape(depth, A_KV_HEADS, KV_GROUP, HEAD_DIM, D_MODEL)
        return jnp.transpose(w, (0, 2, 1, 3, 4)).reshape(depth, A_DIM, D_MODEL)

    shift = GATE_PAD - N_GATES
    pieces = [(0, 0, OFF_G), (OFF_RQ, split, aq0 - split), (OFF_AK, akv0, az0 - akv0)]
    for dst0, src0 in ((OFF_AQ, aq0), (OFF_AZ, az0)):
        for slot, head in enumerate(ATTN_HEAD_ORDER):
            pieces.append((dst0 + slot * HEAD_DIM, src0 + head * HEAD_DIM, HEAD_DIM))
    assert OFF_AQ == aq0 + shift and OFF_AZ == az0 + shift and w_t.shape[1] == P_COLS - shift

    def pack(x_ref, o_ref):
        for dst, src, rows in pieces:
            o_ref[dst:dst + rows, :] = x_ref[src:src + rows, :].astype(bf16)
        gates = jnp.concatenate([x_ref[OFF_G:split, :], jnp.zeros((shift, x_ref.shape[1]), x_ref.dtype)], axis=0)
        o_ref[OFF_G:OFF_RQ, :] = gates.astype(bf16)

    w_in_p = pl.pallas_call(
        pack, grid=(depth, D_MODEL // PACK_COL_BLOCK),
        in_specs=[pl.BlockSpec((None, w_t.shape[1], PACK_COL_BLOCK), lambda l, c: (l, 0, c))],
        out_specs=pl.BlockSpec((None, P_COLS, PACK_COL_BLOCK), lambda l, c: (l, 0, c)),
        out_shape=jax.ShapeDtypeStruct((depth, P_COLS, D_MODEL), bf16),
        compiler_params=pltpu.CompilerParams(dimension_semantics=("arbitrary", "arbitrary")),
        name="pack_w_in",
    )(w_t)
    wo16 = w_out.astype(bf16)
    a0 = M_DIM + R_DIM
    w_out_p = jnp.concatenate([wo16[:, :a0, :], by_head(wo16[:, a0:, :])], axis=1)
    gbias = jnp.pad(mlstm_gate_bias.reshape(depth, 1, N_GATES), ((0, 0), (0, 0), (0, LANES - N_GATES)))
    return dict(norm_gain=norm_gain.reshape(depth, 1, D_MODEL), w_in=w_in_p, gbias=gbias,
                m_gain=mlstm_norm_gain.reshape(depth, 1, M_DIM), r_gain=ret_norm_gain.reshape(depth, 1, R_DIM),
                sinks=attn_sinks, w_out=w_out_p, fgain=final_norm_gain.reshape(1, D_MODEL))


def kernel(x_prompt, x_sample, state_mlstm_C, state_mlstm_n, state_mlstm_m, state_ret_S, cache_win_k,
           cache_win_v, norm_gain, w_in, mlstm_gate_bias, mlstm_norm_gain, ret_norm_gain, attn_sinks,
           rel_bias_table, w_out, final_norm_gain):
    depth = w_in.shape[0]
    seq = x_prompt.shape[1]
    dec_seq = x_sample.shape[1]
    past_len = seq
    p = _prepare_params(norm_gain, w_in, mlstm_gate_bias, mlstm_norm_gain, ret_norm_gain, attn_sinks, w_out,
                        final_norm_gain)
    tabs_p = _static_tables(1)
    tabs_s = _static_tables(ROWS // dec_seq)
    ubias = _bias_vectors(rel_bias_table)
    rope_p = (*_rope_tables(jnp.arange(0, seq, ROWS, dtype=jnp.int32), signed=False),
              *_rope_tables(jnp.arange(ROWS, dtype=jnp.int32), signed=False))
    cos_s, sin_s = _rope_tables(past_len + (jnp.arange(ROWS, dtype=jnp.int32) % dec_seq))
    states = (state_mlstm_C, state_mlstm_n, state_mlstm_m, state_ret_S, cache_win_k, cache_win_v)

    xp = x_prompt
    p_states = []
    for layer in range(depth):
        xp, *sp = _prompt_layer(xp, p, layer, tabs_p, ubias, rope_p, layer == depth - 1)
        p_states.append(sp)
    outs_p = [jnp.stack([p_states[l][i] for l in range(depth)]) for i in range(6)]
    xs, *outs_s = _sample_path(x_sample, states, p, tabs_s, ubias, cos_s, sin_s)
    return (xp, xs, *outs_p, *outs_s)
```

```python
import functools
import math

import numpy as np
import jax
import jax.numpy as jnp
from jax import lax
from jax.experimental import pallas as pl
from jax.experimental.pallas import tpu as pltpu

D_MODEL = 1024
HEAD_DIM = 64
M_HEADS = 4
R_HEADS = 4
A_HEADS = 8
A_KV_HEADS = 2
KV_GROUP = A_HEADS // A_KV_HEADS
M_DIM = M_HEADS * HEAD_DIM
R_DIM = R_HEADS * HEAD_DIM
A_DIM = A_HEADS * HEAD_DIM
A_KV_DIM = A_KV_HEADS * HEAD_DIM
WINDOW = 128
N_BUCKETS = 32
REL_MAX_DIST = 128
ROPE_BASE = 10000.0
NORM_EPS = 1e-6
QK_SCALE = HEAD_DIM ** -0.5

LANES = 128
ROWS = 128
GATE_PAD = LANES
PAIRS = M_HEADS // 2
SPLIT_TERMS = 3
SEL_TERMS = 2

OFF_MQ = 0
OFF_MK = OFF_MQ + M_DIM
OFF_MV = OFF_MK + M_DIM
OFF_MO = OFF_MV + M_DIM
OFF_MZ = OFF_MO + M_DIM
OFF_G = OFF_MZ + M_DIM
OFF_RQ = OFF_G + GATE_PAD
OFF_RK = OFF_RQ + R_DIM
OFF_RV = OFF_RK + R_DIM
OFF_RZ = OFF_RV + R_DIM
OFF_AQ = OFF_RZ + R_DIM
OFF_AK = OFF_AQ + A_DIM
OFF_AV = OFF_AK + A_KV_DIM
OFF_AZ = OFF_AV + A_KV_DIM
P_COLS = OFF_AZ + A_DIM
N_GATES = 2 * M_HEADS
PROJ_COL_BLOCK = 512
SCHEDULE = "FBFBFB"
ATTN_HEAD_ORDER = tuple(h for j in range(KV_GROUP) for h in (j, KV_GROUP + j))

PROMPT_ROWS = 512
PACK_COL_BLOCK = 256
DECODE_PROJ_GROUPS = 4
VMEM_LIMIT_BYTES = 56 * 1024 * 1024
DECODE_VMEM_LIMIT_BYTES = 58 * 1024 * 1024

f32 = jnp.float32
bf16 = jnp.bfloat16


def _dot(a, b):
    return jnp.dot(a, b, preferred_element_type=f32)


def _dot_nt(a, b):
    return lax.dot_general(a, b, (((1,), (1,)), ((), ())), preferred_element_type=f32)


def _sigmoid(x):
    return 1.0 / (1.0 + jnp.exp(-x))


def _silu(x):
    return x * _sigmoid(x)


def _log_sigmoid(x):
    return jnp.minimum(x, 0.0) - jnp.log(1.0 + jnp.exp(-jnp.abs(x)))


def _split_parts(x, terms):
    parts, r = [], x
    for i in range(terms):
        p = r.astype(bf16)
        parts.append(p)
        if i + 1 < terms:
            r = r - p.astype(f32)
    return parts


def _split_terms(x, terms=SPLIT_TERMS):
    return jnp.concatenate(_split_parts(x, terms), axis=1)


def _exact_tril_dot(tril3, x):
    return _dot(tril3, jnp.concatenate(_split_parts(x, SPLIT_TERMS), axis=0))


def _rope(x, cos_t, sin_t, first_half):
    up = pltpu.roll(x, LANES - HEAD_DIM // 2, axis=1)
    down = pltpu.roll(x, HEAD_DIM // 2, axis=1)
    return x * cos_t + jnp.where(first_half, up, down) * sin_t


def _rms_project(x_ref, ngain_ref, win_ref, proj_sc):
    xf = x_ref[...]
    u = xf * lax.rsqrt(jnp.mean(xf * xf, axis=1, keepdims=True) + NORM_EPS) * ngain_ref[...]
    u16 = u.astype(bf16)
    for c0 in range(0, P_COLS, PROJ_COL_BLOCK):
        c1 = min(c0 + PROJ_COL_BLOCK, P_COLS)
        proj_sc[:, c0:c1] = _dot_nt(u16, win_ref[c0:c1, :])


def _prompt_kernel(x_ref, ngain_ref, win_ref, gbias_ref, mgain_ref, rgain_ref, sinks_ref, wout_ref,
                   fgain_ref, tril3_ref, maskadd_ref, dmat2_ref, rslab_ref, ubias_ref, cos_ref,
                   sin_ref, cosoff_ref, sinoff_ref, selh_ref, selp_ref,
                   y_ref, c_out, n_out, m_out, s_out, k_out, v_out,
                   proj_sc, mix_sc, cn_sc, sb_sc, m_sc, kp_sc, vp_sc, bias_sc, *, chunks, final):
    step = pl.program_id(1)
    last_step = pl.num_programs(1) - 1

    @pl.when(jnp.logical_and(pl.program_id(0) == 0, step == 0))
    def _():
        for blk in range(A_HEADS):
            u = ubias_ref[ATTN_HEAD_ORDER[blk]:ATTN_HEAD_ORDER[blk] + 1, :]
            bias_sc[blk * ROWS:(blk + 1) * ROWS, :] = jnp.concatenate(
                [_skew(u[:, :2 * WINDOW], ROWS), _skew(u[:, 2 * WINDOW:], ROWS)], axis=1)

    xf = x_ref[...]
    u16 = (xf * lax.rsqrt(jnp.mean(xf * xf, axis=1, keepdims=True) + NORM_EPS) * ngain_ref[...]).astype(bf16)
    col_blocks = [(c0, min(c0 + PROJ_COL_BLOCK, P_COLS)) for c0 in range(0, P_COLS, PROJ_COL_BLOCK)]
    half_rows = (chunks // 2) * ROWS if chunks > 1 else chunks * ROWS

    def project(r0, r1, c0, c1):
        proj_sc[r0:r1, c0:c1] = _dot_nt(u16[r0:r1], win_ref[c0:c1, :])

    for c0, c1 in col_blocks:
        project(0, half_rows, c0, c1)
    late_pieces = [(half_rows, chunks * ROWS, c0, c1) for c0, c1 in col_blocks] if half_rows < chunks * ROWS else []

    @pl.when(step == 0)
    def _():
        cn_sc[...] = jnp.zeros_like(cn_sc)
        sb_sc[...] = jnp.zeros_like(sb_sc)
        m_sc[...] = jnp.zeros_like(m_sc)
        kp_sc[...] = jnp.zeros_like(kp_sc)
        vp_sc[...] = jnp.zeros_like(vp_sc)

    lane = lax.broadcasted_iota(jnp.int32, (ROWS, LANES), 1)
    row = lax.broadcasted_iota(jnp.int32, (ROWS, LANES), 0)
    left = lane < HEAD_DIM
    first_half = (lane & (HEAD_DIM - 1)) < (HEAD_DIM // 2)
    head_col = lane < M_HEADS
    blockdiag = (row < HEAD_DIM) == left
    row2 = lax.broadcasted_iota(jnp.int32, (ROWS, 2 * LANES), 0)
    lane2w = lax.broadcasted_iota(jnp.int32, (ROWS, 2 * LANES), 1)
    left2 = (lane2w & (LANES - 1)) < HEAD_DIM
    blockdiag2 = (row2 < HEAD_DIM) == left2
    ones16 = jnp.ones((ROWS, LANES), bf16)
    halves, pick, pair_blockdiag, half_mean = _halves, _pick, _pair_blockdiag, _half_mean

    def chunk_body(ci):
        rows = slice(ci * ROWS, (ci + 1) * ROWS)

        def proj(off, width=LANES):
            return proj_sc[rows, off:off + width]

        gates = proj(OFF_G) + gbias_ref[...]
        bcum = _exact_tril_dot(tril3_ref[...], _log_sigmoid(gates))

        base = pl.ds(step * chunks + ci, 1)
        cos_a, sin_a = cos_ref[base, :], sin_ref[base, :]
        cos_b, sin_b = cosoff_ref[...], sinoff_ref[...]
        cos_t = cos_a * cos_b - sin_a * sin_b
        sin_t = sin_a * cos_b + cos_a * sin_b
        sin_t = jnp.where(first_half, -sin_t, sin_t)
        r_q = [_rope(proj(OFF_RQ + p * LANES), cos_t, sin_t, first_half) for p in range(PAIRS)]
        r_k = [_rope(proj(OFF_RK + p * LANES), cos_t, sin_t, first_half) * QK_SCALE for p in range(PAIRS)]
        r_vbd = pair_blockdiag([proj(OFF_RV + p * LANES).astype(bf16) for p in range(PAIRS)])
        r_sb = [sb_sc[p] for p in range(PAIRS)]
        r_sc = _dot_nt(jnp.concatenate([halves(q) for q in r_q], axis=1),
                       pair_blockdiag([k.astype(bf16) for k in r_k]))
        r_inter = _dot(jnp.concatenate([q.astype(bf16) for q in r_q], axis=1),
                       pair_blockdiag([sb.astype(bf16) for sb in r_sb]))
        r_upd = _dot(jnp.concatenate([(r_k[p] * rslab_ref[1, p]).T.astype(bf16) for p in range(PAIRS)], axis=1),
                     r_vbd)
        for p in range(PAIRS):
            sb_sc[p] = rslab_ref[2, p] * r_sb[p] + jnp.where(blockdiag, r_upd[:, p * LANES:(p + 1) * LANES], 0.0)
        yield

        kcur, vcur = proj(OFF_AK), proj(OFF_AV)
        kprev, vprev = kp_sc[...], vp_sc[...]
        kk16 = jnp.concatenate([kcur, kprev], axis=0).astype(bf16)
        vv16 = jnp.concatenate([jnp.concatenate([vcur.astype(bf16), ones16], axis=1),
                                jnp.concatenate([vprev.astype(bf16), ones16], axis=1)], axis=0)
        kp_sc[...] = kcur
        vp_sc[...] = vcur
        a_q = jnp.concatenate([halves(proj(OFF_AQ + j * LANES) * QK_SCALE) for j in range(KV_GROUP)], axis=0)
        a_s = _dot_nt(a_q, kk16) + bias_sc[...]
        if ci == 0:
            pen = jnp.where(step == 0, -jnp.inf, 0.0).astype(f32)
            a_s = a_s + jnp.where(lax.broadcasted_iota(jnp.int32, (1, 2 * ROWS), 1) >= ROWS, pen, 0.0)

        m_q = [proj(OFF_MQ + p * LANES) for p in range(PAIRS)]
        m_k = [proj(OFF_MK + p * LANES) * QK_SCALE for p in range(PAIRS)]
        m_qk = _dot_nt(jnp.concatenate([halves(q) for q in m_q], axis=1),
                       pair_blockdiag([k.astype(bf16) for k in m_k]))
        m_q16 = [q.astype(bf16) for q in m_q]
        m_v16 = [proj(OFF_MV + p * LANES).astype(bf16) for p in range(PAIRS)]
        yield

        zb = pltpu.roll(bcum, LANES - M_HEADS, axis=1)
        r_mat = jnp.where(head_col, gates - zb, 0.0)
        cm = r_mat
        sh = 1
        while sh < ROWS:
            cm = jnp.where(row >= sh, jnp.maximum(cm, pltpu.roll(cm, sh, axis=0)), cm)
            sh *= 2
        mprev = m_sc[...]
        mx = jnp.maximum(mprev, cm)
        gm = mprev - mx
        em = jnp.where(head_col, -(zb + mx), 0.0)
        mx_last = jnp.broadcast_to(mx[ROWS - 1:ROWS, :], (ROWS, LANES))
        m_sc[...] = jnp.where(head_col, jnp.broadcast_to((zb + mx)[ROWS - 1:ROWS, :], (ROWS, LANES)), 0.0)
        mx_b = _dot_nt(_split_terms(mx, SEL_TERMS), selh_ref[...])
        slabs = jnp.exp(_dot_nt(_split_terms(jnp.concatenate([gm, em, r_mat - mx_last], axis=0), SEL_TERMS),
                                selp_ref[...]))
        winter_b, emt_b, ws_b = slabs[:ROWS], slabs[ROWS:2 * ROWS], slabs[2 * ROWS:]
        r_t = r_mat.T
        yield

        outs = []
        r_acc = []
        r_o = _dot((r_sc * dmat2_ref[...]).astype(bf16), r_vbd)
        for p in range(PAIRS):
            ps = slice(p * LANES, (p + 1) * LANES)
            r_acc.append(pick(r_o[:, ps]) + rslab_ref[0, p] * r_inter[:, ps])

        a_out = []
        a_p = []
        for blk in range(A_HEADS):
            s = a_s[blk * ROWS:(blk + 1) * ROWS]
            sink = sinks_ref[ATTN_HEAD_ORDER[blk]]
            m = jnp.maximum(jnp.max(jnp.maximum(s[:, :ROWS], s[:, ROWS:]), axis=1, keepdims=True), sink)
            a_p.append(jnp.exp(s - m).astype(bf16))
            a_out.append(jnp.exp(sink - m))
        a_pv = _dot(jnp.concatenate(a_p, axis=0), vv16)
        yield

        maskadd = maskadd_ref[...]
        for p in range(PAIRS):
            ps = slice(p * LANES, (p + 1) * LANES)
            cn = cn_sc[p]
            w = jnp.concatenate(
                [jnp.exp((r_t[2 * p + side:2 * p + side + 1, :] + maskadd)
                         - mx_b[:, (2 * p + side) * ROWS:(2 * p + side + 1) * ROWS]) for side in range(2)],
                axis=0) * m_qk[:, ps]
            acc = (pick(_dot(w.astype(bf16), jnp.concatenate([m_v16[p], ones16], axis=1)))
                   + jnp.concatenate([winter_b[:, ps]] * 2, axis=1) * _dot(m_q16[p], cn.astype(bf16)))
            hh = acc[:, :LANES] / jnp.maximum(jnp.abs(acc[:, LANES:]), emt_b[:, ps])
            outs.append(_sigmoid(proj(OFF_MO + p * LANES)) * hh)
            kwt16 = (m_k[p] * ws_b[:, ps]).T.astype(bf16)
            dcn = _dot(kwt16, jnp.concatenate([m_v16[p], ones16], axis=1))
            decay = winter_b[ROWS - 1:ROWS, ps]
            cn_sc[p] = jnp.concatenate([decay, decay], axis=1) * cn + jnp.where(blockdiag2, dcn, 0.0)
        outs.extend(r_acc)
        yield

        x4 = jnp.concatenate(outs, axis=0)
        xc = x4 - half_mean(x4)
        y4 = xc * lax.rsqrt(half_mean(xc * xc) + NORM_EPS)
        for i in range(2 * PAIRS):
            gain = (mgain_ref if i < PAIRS else rgain_ref)[:, (i % PAIRS) * LANES:(i % PAIRS + 1) * LANES]
            zoff = (OFF_MZ if i < PAIRS else OFF_RZ) + (i % PAIRS) * LANES
            out = y4[i * ROWS:(i + 1) * ROWS] * gain * _silu(proj(zoff))
            mix_sc[rows, i * LANES:(i + 1) * LANES] = out.astype(bf16)
        for j in range(KV_GROUP):
            acc = pick(a_pv[2 * j * ROWS:(2 * j + 2) * ROWS])
            den = acc[:, LANES:] + jnp.where(left, a_out[2 * j], a_out[2 * j + 1])
            out = (acc[:, :LANES] / den) * _silu(proj(OFF_AZ + j * LANES))
            mix_sc[rows, M_DIM + R_DIM + j * LANES:M_DIM + R_DIM + (j + 1) * LANES] = out.astype(bf16)

    def out_project(r0, r1, c0, c1):
        y_ref[r0:r1, c0:c1] = x_ref[r0:r1, c0:c1] + _dot(mix_sc[r0:r1, :], wout_ref[:, c0:c1])

    out_blocks = [(c0, min(c0 + PROJ_COL_BLOCK, D_MODEL)) for c0 in range(0, D_MODEL, PROJ_COL_BLOCK)]
    early_out = [(0, half_rows, c0, c1) for c0, c1 in out_blocks] if half_rows < chunks * ROWS else []
    final_out = [(half_rows if early_out else 0, chunks * ROWS, c0, c1) for c0, c1 in out_blocks]

    def fill_mxu(ci):
        if late_pieces:
            project(*late_pieces.pop(0))
        elif early_out and (ci - 1) * ROWS >= half_rows:
            out_project(*early_out.pop(0))

    parts = [chunk_body(ci) for ci in range(chunks)]
    for ci in range(chunks + 1):
        if ci * ROWS >= half_rows:
            while late_pieces:
                project(*late_pieces.pop(0))
        for which in SCHEDULE:
            if which == "F" and ci < chunks:
                next(parts[ci])
                fill_mxu(ci)
            if which == "B" and ci > 0:
                next(parts[ci - 1], None)
                fill_mxu(ci)
    for piece in early_out + final_out:
        out_project(*piece)
    if final:
        y = y_ref[...]
        y_ref[...] = y * lax.rsqrt(jnp.mean(y * y, axis=1, keepdims=True) + NORM_EPS) * fgain_ref[...]

    @pl.when(step == last_step)
    def _():
        for p in range(PAIRS):
            cn = cn_sc[p]
            sb = sb_sc[p]
            n_t = cn[:, LANES:].T
            for side in range(2):
                h = 2 * p + side
                blk = slice(side * HEAD_DIM, (side + 1) * HEAD_DIM)
                c_out[0, h] = cn[blk, blk]
                s_out[0, h] = sb[blk, blk]
                n_out[0, h:h + 1, :] = n_t[side * HEAD_DIM:side * HEAD_DIM + 1, blk]
        m_out[0] = m_sc[...]
        k_out[0] = kp_sc[...]
        v_out[0] = vp_sc[...]


def _lane_is_left(shape):
    return (lax.broadcasted_iota(jnp.int32, shape, 1) & (LANES - 1)) < HEAD_DIM


def _halves(x):
    left = _lane_is_left(x.shape)
    return jnp.concatenate([jnp.where(left, x, 0.0), jnp.where(left, 0.0, x)], axis=0).astype(bf16)


def _pick(x):
    return jnp.where(_lane_is_left((ROWS, x.shape[1])), x[:ROWS], x[ROWS:])


def _pair_blockdiag(blocks):
    z = jnp.zeros_like(blocks[0])
    return jnp.concatenate(
        [jnp.concatenate([blk if j == i else z for j in range(len(blocks))], axis=1)
         for i, blk in enumerate(blocks)], axis=0)


def _half_mean(x):
    left = _lane_is_left(x.shape)
    s_left = jnp.sum(jnp.where(left, x, 0.0), axis=1, keepdims=True)
    s_right = jnp.sum(jnp.where(left, 0.0, x), axis=1, keepdims=True)
    return jnp.where(left, s_left, s_right) * (1.0 / HEAD_DIM)


def _group_last(x, groups):
    n = x.shape[1]
    glen = ROWS // groups
    x3 = x.reshape(groups, glen, n)
    return jnp.broadcast_to(x3[:, glen - 1:glen, :], (groups, glen, n)).reshape(ROWS, n)


def _sample_kernel(x_ref, ngain_ref, win_ref, gbias_ref, mgain_ref, rgain_ref, sinks_ref, wout_ref,
                   fgain_ref, tril3_ref, maskadd_ref, dmat2_ref, rslab_ref, ubias_ref, cos_ref, sin_ref,
                   selh_ref, selp_ref,
                   c_in, n_in, m_in, s_in, k_in, v_in,
                   y_ref, c_out, n_out, m_out, s_out, k_out, v_out,
                   proj_sc, mix_sc, xcur_sc, xall_sc, qb_sc, sp_sc, pp_sc, ob_sc, biasc_ref, biasp_ref, *, groups,
                   proj_groups, ret_full):
    glen = ROWS // groups
    glen_log2 = glen.bit_length() - 1
    hd_log2 = HEAD_DIM.bit_length() - 1
    layer = pl.program_id(0)
    step = pl.program_id(1)
    last_layer = pl.num_programs(0) - 1
    xrows = pl.ds(pl.multiple_of(step * ROWS, ROWS), ROWS)
    sub = lax.rem(step, proj_groups)
    rows = pl.ds(pl.multiple_of(sub * ROWS, ROWS), ROWS)
    wide = pl.ds(pl.multiple_of((step - sub) * ROWS, ROWS), proj_groups * ROWS)

    @pl.when(jnp.logical_and(sub == 0, layer == 0))
    def _():
        _rms_project(x_ref, ngain_ref, win_ref, proj_sc)

    @pl.when(jnp.logical_and(sub == 0, layer != 0))
    def _():
        _rms_project(xall_sc.at[wide, :], ngain_ref, win_ref, proj_sc)

    @pl.when(layer == 0)
    def _():
        xcur_sc[...] = x_ref[rows, :]

    @pl.when(layer != 0)
    def _():
        xcur_sc[...] = xall_sc[xrows, :]

    for b in range(groups):
        k_out[b] = pltpu.roll(k_in[b], WINDOW - glen, axis=1)
        v_out[b] = pltpu.roll(v_in[b], WINDOW - glen, axis=1)

    @pl.when(jnp.logical_and(step == 0, layer == 0))
    def _():
        for h in range(A_HEADS):
            u = ubias_ref[h:h + 1, :]
            biasc_ref[h] = _skew(u[:, :2 * WINDOW], ROWS) + maskadd_ref[...]
            biasp_ref[h] = jnp.concatenate([_skew(u[:, 2 * WINDOW:], glen)] * groups, axis=0)

    lane = lax.broadcasted_iota(jnp.int32, (ROWS, LANES), 1)
    first_half = (lane & (HEAD_DIM - 1)) < (HEAD_DIM // 2)

    r_i = lax.broadcasted_iota(jnp.int32, (ROWS, groups * HEAD_DIM), 0)
    c_i = lax.broadcasted_iota(jnp.int32, (ROWS, groups * HEAD_DIM), 1)
    blk = (r_i >> glen_log2) == (c_i >> hd_log2)
    r_t = lax.broadcasted_iota(jnp.int32, (groups * HEAD_DIM, ROWS), 0)
    c_t = lax.broadcasted_iota(jnp.int32, (groups * HEAD_DIM, ROWS), 1)
    blk_t = (r_t >> hd_log2) == (c_t >> glen_log2)

    def q_times_state(qh, st):
        qt = jnp.where(blk, jnp.concatenate([qh] * groups, axis=1), 0.0)
        return _dot(qt.astype(bf16), st.astype(bf16))

    def state_increment(kt_h, vh16):
        kt = jnp.where(blk_t, jnp.concatenate([kt_h] * groups, axis=0), 0.0)
        return _dot(kt.astype(bf16), vh16)

    def proj(off, width=LANES):
        return proj_sc[rows, off:off + width]

    def head_state_rows(slab, side):
        wide = slab[:, side * HEAD_DIM:(side + 1) * HEAD_DIM].reshape(groups, glen, HEAD_DIM)[:, 0:1, :]
        rows_ = jnp.broadcast_to(wide, (groups, HEAD_DIM, HEAD_DIM)).reshape(groups * HEAD_DIM, HEAD_DIM)
        return rows_, wide.reshape(groups, HEAD_DIM)

    row = lax.broadcasted_iota(jnp.int32, (ROWS, LANES), 0)
    tau = row & (glen - 1)
    head_col = lane < M_HEADS
    left = lane < HEAD_DIM
    ones16 = jnp.ones((ROWS, LANES), bf16)
    gates = proj(OFF_G) + gbias_ref[...]
    bcum = _exact_tril_dot(tril3_ref[...], _log_sigmoid(gates))
    zb = pltpu.roll(bcum, LANES - M_HEADS, axis=1)
    r_mat = jnp.where(head_col, gates - zb, 0.0)
    cm = r_mat
    sh = 1
    while sh < glen:
        cm = jnp.where(tau >= sh, jnp.maximum(cm, pltpu.roll(cm, sh, axis=0)), cm)
        sh *= 2
    mprev = m_in[...]
    mx = jnp.maximum(mprev, cm)
    gm = mprev - mx
    em = jnp.where(head_col, -(zb + mx), 0.0)
    mx_last = _group_last(mx, groups)
    m_out[...] = jnp.where(head_col, _group_last(zb + mx, groups), 0.0)
    mx_b = _dot_nt(_split_terms(mx, SEL_TERMS), selh_ref[...])
    slabs = jnp.exp(_dot_nt(_split_terms(jnp.concatenate([gm, em, r_mat - mx_last], axis=0), SEL_TERMS), selp_ref[...]))
    winter_b, emt_b, ws_b = slabs[:ROWS], slabs[ROWS:2 * ROWS], slabs[2 * ROWS:]
    decay_b = _group_last(winter_b, groups)
    r_t = r_mat.T
    maskadd = maskadd_ref[...]

    m_q = [proj(OFF_MQ + p * LANES) for p in range(PAIRS)]
    m_k = [proj(OFF_MK + p * LANES) * QK_SCALE for p in range(PAIRS)]
    m_v = [proj(OFF_MV + p * LANES) for p in range(PAIRS)]
    m_qk = _dot_nt(jnp.concatenate([_halves(q) for q in m_q], axis=1),
                   _pair_blockdiag([k.astype(bf16) for k in m_k]))
    outs = []
    for p in range(PAIRS):
        ps = slice(p * LANES, (p + 1) * LANES)
        w = jnp.concatenate(
            [jnp.exp((r_t[2 * p + side:2 * p + side + 1, :] + maskadd)
                     - mx_b[:, (2 * p + side) * ROWS:(2 * p + side + 1) * ROWS]) for side in range(2)],
            axis=0) * m_qk[:, ps]
        intra = _pick(_dot(w.astype(bf16), jnp.concatenate([m_v[p].astype(bf16), ones16], axis=1)))
        kw = m_k[p] * ws_b[:, ps]
        kwt = kw.T
        q_c, q_n = [], []
        for side in range(2):
            h = 2 * p + side
            hs = slice(side * HEAD_DIM, (side + 1) * HEAD_DIM)
            qh = m_q[p][:, hs]
            c_h = c_in[:, h].reshape(groups * HEAD_DIM, HEAD_DIM)
            n_g = n_in[h]
            n_rows = jnp.broadcast_to(n_g.reshape(groups, 1, HEAD_DIM),
                                      (groups, glen, HEAD_DIM)).reshape(ROWS, HEAD_DIM)
            q_c.append(q_times_state(qh, c_h))
            q_n.append(jnp.sum(qh * n_rows, axis=1, keepdims=True))
            dec_rows, dec_g = head_state_rows(decay_b[:, ps], side)
            c_new = dec_rows * c_h + state_increment(kwt[hs, :], m_v[p][:, hs].astype(bf16))
            c_out[:, h] = c_new.reshape(groups, HEAD_DIM, HEAD_DIM)
            n_out[h] = dec_g * n_g + jnp.sum(kw[:, hs].reshape(groups, glen, HEAD_DIM), axis=1)
        wb = winter_b[:, ps]
        num = intra[:, :LANES] + wb * jnp.concatenate(q_c, axis=1)
        nq = intra[:, LANES:] + wb * jnp.where(left, q_n[0], q_n[1])
        outs.append(_sigmoid(proj(OFF_MO + p * LANES)) * (num / jnp.maximum(jnp.abs(nq), emt_b[:, ps])))

    cos_t, sin_t = cos_ref[...], sin_ref[...]
    r_q = [_rope(proj(OFF_RQ + p * LANES), cos_t, sin_t, first_half) for p in range(PAIRS)]
    r_k = [_rope(proj(OFF_RK + p * LANES), cos_t, sin_t, first_half) * QK_SCALE for p in range(PAIRS)]
    r_v = [proj(OFF_RV + p * LANES) for p in range(PAIRS)]
    r_sc = _dot_nt(jnp.concatenate([_halves(q) for q in r_q], axis=1),
                   _pair_blockdiag([k.astype(bf16) for k in r_k]))
    r_o = _dot((r_sc * dmat2_ref[...]).astype(bf16), _pair_blockdiag([v.astype(bf16) for v in r_v]))
    for p in range(PAIRS):
        ps = slice(p * LANES, (p + 1) * LANES)
        rkt = (r_k[p] * rslab_ref[1, p]).T
        q_s = []
        for side in range(2):
            h = 2 * p + side
            hs = slice(side * HEAD_DIM, (side + 1) * HEAD_DIM)
            s_h = s_in[:, h].reshape(groups * HEAD_DIM, HEAD_DIM)
            q_s.append(q_times_state(r_q[p][:, hs], s_h))
            s_new = ret_full[h] * s_h + state_increment(rkt[hs, :], r_v[p][:, hs].astype(bf16))
            s_out[:, h] = s_new.reshape(groups, HEAD_DIM, HEAD_DIM)
        outs.append(_pick(r_o[:, ps]) + rslab_ref[0, p] * jnp.concatenate(q_s, axis=1))

    x4 = jnp.concatenate(outs, axis=0)
    xc = x4 - _half_mean(x4)
    y4 = xc * lax.rsqrt(_half_mean(xc * xc) + NORM_EPS)
    for i in range(2 * PAIRS):
        gain = (mgain_ref if i < PAIRS else rgain_ref)[:, (i % PAIRS) * LANES:(i % PAIRS + 1) * LANES]
        zoff = (OFF_MZ if i < PAIRS else OFF_RZ) + (i % PAIRS) * LANES
        mix_sc[:, i * LANES:(i + 1) * LANES] = (y4[i * ROWS:(i + 1) * ROWS] * gain * _silu(proj(zoff))).astype(bf16)

    kcur = proj_sc[rows, OFF_AK:OFF_AK + A_KV_DIM]
    vcur = proj_sc[rows, OFF_AV:OFF_AV + A_KV_DIM]
    kcur16, vcur16 = kcur.astype(bf16), vcur.astype(bf16)

    a_q = {}
    for j in range(KV_GROUP):
        hq = _halves(proj(OFF_AQ + j * LANES) * QK_SCALE)
        a_q[j], a_q[KV_GROUP + j] = hq[:ROWS], hq[ROWS:]
    for h in range(A_HEADS):
        qb_sc[:, h * glen:(h + 1) * glen, :] = a_q[h].reshape(groups, glen, LANES)
    sc_all = _dot_nt(jnp.concatenate([a_q[h] for h in range(A_HEADS)], axis=0), kcur16)

    for b in range(groups):
        sp = _dot(qb_sc[b], k_in[b].astype(bf16))
        sp_sc[:, b * glen:(b + 1) * glen, :] = sp.reshape(A_HEADS, glen, WINDOW)

    vcur1 = jnp.concatenate([vcur16, ones16], axis=1)
    esink, o_cur = [], []
    for h in range(A_HEADS):
        sc = sc_all[h * ROWS:(h + 1) * ROWS] + biasc_ref[h]
        sp = sp_sc[h] + biasp_ref[h]
        sink = sinks_ref[layer, h]
        m = jnp.maximum(jnp.max(jnp.maximum(sc, sp), axis=1, keepdims=True), sink)
        esink.append(jnp.exp(sink - m))
        o_cur.append(_dot(jnp.exp(sc - m).astype(bf16), vcur1))
        pp_sc[:, h * glen:(h + 1) * glen, :] = jnp.exp(sp - m).reshape(groups, glen, WINDOW).astype(bf16)

    for b in range(groups):
        v1 = jnp.concatenate([v_in[b].astype(bf16), ones16], axis=0)
        ob = _dot_nt(pp_sc[b], v1)
        ob_sc[:, b * glen:(b + 1) * glen, :] = ob.reshape(A_HEADS, glen, 2 * LANES)
    norm = []
    for h in range(A_HEADS):
        acc = o_cur[h] + ob_sc[h]
        norm.append(acc[:, :LANES] / (acc[:, LANES:] + esink[h]))
    outs = [jnp.where(left, norm[j], norm[KV_GROUP + j]) for j in range(KV_GROUP)]
    kcur_t, vcur_t = kcur.T, vcur.T
    new = slice(WINDOW - glen, WINDOW)
    for b in range(groups):
        shift = (WINDOW - glen - b * glen) % LANES
        k_out[b, :, new] = pltpu.roll(kcur_t, shift, axis=1)[:, new]
        v_out[b, :, new] = pltpu.roll(vcur_t, shift, axis=1)[:, new]
    out_a = jnp.concatenate(outs, axis=1) * _silu(proj_sc[rows, OFF_AZ:OFF_AZ + A_DIM])
    mix_sc[:, M_DIM + R_DIM:M_DIM + R_DIM + A_DIM] = out_a.astype(bf16)

    y = xcur_sc[...] + _dot(mix_sc[...], wout_ref[...])

    @pl.when(layer != last_layer)
    def _():
        xall_sc[xrows, :] = y
        y_ref[...] = y

    @pl.when(layer == last_layer)
    def _():
        y_ref[...] = y * lax.rsqrt(jnp.mean(y * y, axis=1, keepdims=True) + NORM_EPS) * fgain_ref[...]


def _t5_bucket(dist):
    max_exact = N_BUCKETS // 2
    d = np.maximum(dist, 1).astype(np.float32)
    large = max_exact + (np.log(d / max_exact) / np.log(REL_MAX_DIST / max_exact)
                         * (N_BUCKETS - max_exact)).astype(np.int32)
    large = np.minimum(large, N_BUCKETS - 1)
    return np.where(dist < max_exact, dist, large).astype(np.int32)


def _static_tables(groups):
    glen = ROWS // groups
    r = np.arange(ROWS)
    grp, tau = r // glen, r % glen
    causal = (grp[:, None] == grp[None, :]) & (tau[None, :] <= tau[:, None])
    tril = causal.astype(np.float32)
    maskadd = np.where(causal, 0.0, -np.inf).astype(np.float32)
    log_g = np.log1p(-np.exp2(-5.0 - np.arange(R_HEADS, dtype=np.float64)))
    diff = (tau[:, None] - tau[None, :]).astype(np.float64)
    dmat = np.where(causal[None], np.exp(log_g[:, None, None] * np.maximum(diff, 0.0)[None]), 0.0)
    inter = np.exp(log_g[None, :] * (tau[:, None] + 1.0))
    tail = np.exp(log_g[None, :] * (glen - 1.0 - tau[:, None]))
    full = np.exp(log_g * glen)
    lane_head = np.arange(LANES) // HEAD_DIM
    rslab = np.zeros((3, PAIRS, ROWS, LANES), np.float64)
    for p in range(PAIRS):
        rslab[0, p] = inter[:, 2 * p + lane_head]
        rslab[1, p] = tail[:, 2 * p + lane_head]
        rslab[2, p] = full[2 * p + lane_head][None, :]
    selh = np.zeros((M_HEADS * ROWS, SEL_TERMS * LANES), np.float32)
    selp = np.zeros((PAIRS * LANES, SEL_TERMS * LANES), np.float32)
    for t in range(SEL_TERMS):
        for h in range(M_HEADS):
            selh[h * ROWS:(h + 1) * ROWS, t * LANES + h] = 1.0
        for p in range(PAIRS):
            for side in range(2):
                selp[p * LANES + side * HEAD_DIM:p * LANES + (side + 1) * HEAD_DIM, t * LANES + 2 * p + side] = 1.0
    return dict(tril3=jnp.asarray(np.concatenate([tril] * SPLIT_TERMS, axis=1), bf16),
                maskadd=maskadd,
                dmat2=np.concatenate(list(dmat.astype(np.float32).reshape(PAIRS, 2 * ROWS, ROWS)), axis=1),
                rslab=rslab.astype(np.float32),
                full=tuple(float(v) for v in full),
                selh=jnp.asarray(selh, bf16), selp=jnp.asarray(selp, bf16))


def _bias_vectors(rel_table):
    tb = jnp.transpose(rel_table[_t5_bucket(np.arange(WINDOW))]).astype(f32)
    ninf = jnp.full((A_HEADS, WINDOW), -jnp.inf, f32)
    rev = tb[:, :0:-1]
    return jnp.concatenate([tb[:, :1], ninf, rev, ninf[:, :1], rev, ninf], axis=1)


def _skew(u_row, rows):
    x = jnp.broadcast_to(u_row, (rows, 2 * WINDOW))
    return pltpu.roll(x, 0, 1, stride=1, stride_axis=0)[:, :WINDOW]


def _rope_tables(pos, signed=True):
    half = HEAD_DIM // 2
    inv = ROPE_BASE ** (-jnp.arange(half, dtype=f32) / half)
    ang = pos.astype(f32)[:, None] * inv[None, :]
    cos, sin = jnp.cos(ang), jnp.sin(ang)
    reps = LANES // HEAD_DIM
    cos_t = jnp.tile(jnp.concatenate([cos, cos], axis=1), (1, reps))
    sin_t = jnp.tile(jnp.concatenate([-sin if signed else sin, sin], axis=1), (1, reps))
    return cos_t, sin_t


def _const_spec(shape, nargs):
    zeros = (0,) * len(shape)
    if nargs == 1:
        return pl.BlockSpec(shape, lambda i: zeros)
    return pl.BlockSpec(shape, lambda i, j: zeros)


def _layer_spec(shape, layer, nargs):
    idx = (layer,) + (0,) * len(shape)
    if nargs == 1:
        return pl.BlockSpec((None,) + shape, lambda i: idx)
    return pl.BlockSpec((None,) + shape, lambda i, j: idx)


def _param_specs(layer, nargs):
    ls = functools.partial(_layer_spec, layer=layer, nargs=nargs)
    return [ls((1, D_MODEL)), ls((P_COLS, D_MODEL)), ls((1, LANES)), ls((1, M_DIM)), ls((1, R_DIM)),
            pl.BlockSpec(memory_space=pltpu.SMEM), ls((D_MODEL, D_MODEL)), _const_spec((1, D_MODEL), nargs)]


def _param_args(p, layer):
    return (p["norm_gain"], p["w_in"], p["gbias"], p["m_gain"], p["r_gain"], p["sinks"][layer], p["w_out"],
            p["fgain"])


def _prompt_layer(x, p, layer, tabs, ubias, rope, final):
    B, T, _ = x.shape
    tb = min(PROMPT_ROWS, T)
    chunks = tb // ROWS
    nt = T // tb
    cs = functools.partial(_const_spec, nargs=2)
    in_specs = [pl.BlockSpec((None, tb, D_MODEL), lambda b, t: (b, t, 0))] + _param_specs(layer, 2) + [
        cs((ROWS, SPLIT_TERMS * ROWS)), cs((ROWS, ROWS)), cs((2 * ROWS, PAIRS * ROWS)),
        cs((3, PAIRS, ROWS, LANES)), cs((A_HEADS, 4 * WINDOW)),
        cs((T // ROWS, LANES)), cs((T // ROWS, LANES)), cs((ROWS, LANES)), cs((ROWS, LANES)),
        cs((M_HEADS * ROWS, SEL_TERMS * LANES)), cs((PAIRS * LANES, SEL_TERMS * LANES)),
    ]
    out_shape = (
        jax.ShapeDtypeStruct((B, T, D_MODEL), f32),
        jax.ShapeDtypeStruct((B, M_HEADS, HEAD_DIM, HEAD_DIM), f32),
        jax.ShapeDtypeStruct((B, M_HEADS, HEAD_DIM), f32),
        jax.ShapeDtypeStruct((B, ROWS, LANES), f32),
        jax.ShapeDtypeStruct((B, R_HEADS, HEAD_DIM, HEAD_DIM), f32),
        jax.ShapeDtypeStruct((B, WINDOW, A_KV_DIM), f32),
        jax.ShapeDtypeStruct((B, WINDOW, A_KV_DIM), f32),
    )
    out_specs = (
        pl.BlockSpec((None, tb, D_MODEL), lambda b, t: (b, t, 0)),
        pl.BlockSpec((1, M_HEADS, HEAD_DIM, HEAD_DIM), lambda b, t: (b, 0, 0, 0)),
        pl.BlockSpec((1, M_HEADS, HEAD_DIM), lambda b, t: (b, 0, 0)),
        pl.BlockSpec((1, ROWS, LANES), lambda b, t: (b, 0, 0)),
        pl.BlockSpec((1, R_HEADS, HEAD_DIM, HEAD_DIM), lambda b, t: (b, 0, 0, 0)),
        pl.BlockSpec((1, WINDOW, A_KV_DIM), lambda b, t: (b, 0, 0)),
        pl.BlockSpec((1, WINDOW, A_KV_DIM), lambda b, t: (b, 0, 0)),
    )
    kern = functools.partial(_prompt_kernel, chunks=chunks, final=final)
    y, c, n, m, s, k, v = pl.pallas_call(
        kern, grid=(B, nt), in_specs=in_specs, out_specs=out_specs, out_shape=out_shape,
        scratch_shapes=[pltpu.VMEM((tb, P_COLS), f32), pltpu.VMEM((tb, D_MODEL), bf16),
                        pltpu.VMEM((PAIRS, ROWS, 2 * LANES), f32), pltpu.VMEM((PAIRS, ROWS, LANES), f32),
                        pltpu.VMEM((ROWS, LANES), f32), pltpu.VMEM((ROWS, A_KV_DIM), f32),
                        pltpu.VMEM((ROWS, A_KV_DIM), f32), pltpu.VMEM((A_HEADS * ROWS, 2 * WINDOW), f32)],
        compiler_params=pltpu.CompilerParams(dimension_semantics=("arbitrary", "arbitrary"),
                                             vmem_limit_bytes=VMEM_LIMIT_BYTES),
        name="prompt_layer",
    )(x, *_param_args(p, layer), tabs["tril3"], tabs["maskadd"], tabs["dmat2"], tabs["rslab"], ubias,
      *rope, tabs["selh"], tabs["selp"])
    k = k.reshape(B, WINDOW, A_KV_HEADS, HEAD_DIM)
    v = v.reshape(B, WINDOW, A_KV_HEADS, HEAD_DIM)
    return y, c, n, m[:, 0, :M_HEADS], s, k, v


def _batch_major_kernel(*refs):
    half = len(refs) // 2
    for x_ref, o_ref in zip(refs[:half], refs[half:]):
        o_ref[...] = jnp.transpose(x_ref[...], (2, 0, 1))


def _batch_major(*states):
    depth, B, heads, d, e = states[0].shape
    assert all(s.shape == states[0].shape for s in states)
    views = [jnp.transpose(s, (0, 2, 3, 4, 1)) for s in states]
    return pl.pallas_call(
        _batch_major_kernel, grid=(depth, heads),
        in_specs=[pl.BlockSpec((None, None, d, e, B), lambda l, h: (l, h, 0, 0, 0))] * len(states),
        out_specs=[pl.BlockSpec((None, B, None, d, e), lambda l, h: (l, 0, h, 0, 0))] * len(states),
        out_shape=[jax.ShapeDtypeStruct(s.shape, s.dtype) for s in states],
        compiler_params=pltpu.CompilerParams(dimension_semantics=("arbitrary", "arbitrary"),
                                             vmem_limit_bytes=VMEM_LIMIT_BYTES),
        name="batch_major",
    )(*views)


def _sample_path(x, states, p, tabs, ubias, cos_t, sin_t):
    B, T, _ = x.shape
    groups = ROWS // T
    nb = B // groups
    pg = math.gcd(nb, DECODE_PROJ_GROUPS)
    c0, n0, m0, s0, k0, v0 = states
    depth = c0.shape[0]
    c0, s0 = _batch_major(c0, s0)
    x2 = x.reshape(B * T, D_MODEL)
    n0t = jnp.transpose(n0, (0, 2, 1, 3))
    m0r = jnp.pad(jnp.repeat(m0, T, axis=1), ((0, 0), (0, 0), (0, LANES - M_HEADS)))
    k0r = jnp.transpose(k0, (0, 1, 3, 4, 2)).reshape(depth, B, A_KV_DIM, WINDOW)
    v0r = jnp.transpose(v0, (0, 1, 3, 4, 2)).reshape(depth, B, A_KV_DIM, WINDOW)

    def cs(shape):
        zeros = (0,) * len(shape)
        return pl.BlockSpec(shape, lambda l, i: zeros)

    def per_layer(shape, buffers=None):
        zeros = (0,) * len(shape)
        mode = {} if buffers is None else dict(pipeline_mode=pl.Buffered(buffers))
        return pl.BlockSpec((None,) + shape, lambda l, i: (l,) + zeros, **mode)

    st4 = pl.BlockSpec((None, groups, M_HEADS, HEAD_DIM, HEAD_DIM), lambda l, i: (l, i, 0, 0, 0))
    stn = pl.BlockSpec((None, M_HEADS, groups, HEAD_DIM), lambda l, i: (l, 0, i, 0))
    stm = pl.BlockSpec((None, ROWS, LANES), lambda l, i: (l, i, 0))
    stk = pl.BlockSpec((None, groups, A_KV_DIM, WINDOW), lambda l, i: (l, i, 0, 0))
    in_specs = [
        pl.BlockSpec((pg * ROWS, D_MODEL), lambda l, i: (i // pg, 0)),
        per_layer((1, D_MODEL)), per_layer((P_COLS, D_MODEL), 1), per_layer((1, LANES)), per_layer((1, M_DIM)),
        per_layer((1, R_DIM)), pl.BlockSpec(memory_space=pltpu.SMEM), per_layer((D_MODEL, D_MODEL), 1),
        cs((1, D_MODEL)),
        cs((ROWS, SPLIT_TERMS * ROWS)), cs((ROWS, ROWS)), cs((2 * ROWS, PAIRS * ROWS)),
        cs((3, PAIRS, ROWS, LANES)), cs((A_HEADS, 4 * WINDOW)),
        cs((ROWS, LANES)), cs((ROWS, LANES)),
        cs((M_HEADS * ROWS, SEL_TERMS * LANES)), cs((PAIRS * LANES, SEL_TERMS * LANES)),
        st4, stn, stm, st4, stk, stk,
    ]
    out_shape = (
        jax.ShapeDtypeStruct((B * T, D_MODEL), f32),
        jax.ShapeDtypeStruct((depth, B, M_HEADS, HEAD_DIM, HEAD_DIM), f32),
        jax.ShapeDtypeStruct((depth, M_HEADS, B, HEAD_DIM), f32),
        jax.ShapeDtypeStruct((depth, B * T, LANES), f32),
        jax.ShapeDtypeStruct((depth, B, R_HEADS, HEAD_DIM, HEAD_DIM), f32),
        jax.ShapeDtypeStruct((depth, B, A_KV_DIM, WINDOW), f32),
        jax.ShapeDtypeStruct((depth, B, A_KV_DIM, WINDOW), f32),
    )
    y_spec = pl.BlockSpec((ROWS, D_MODEL), lambda l, i: (jnp.where(l == depth - 1, i, 0), 0))
    out_specs = (y_spec, st4, stn, stm, st4, stk, stk)
    kern = functools.partial(_sample_kernel, groups=groups, proj_groups=pg, ret_full=tabs["full"])
    y, c, n, m, s, k, v = pl.pallas_call(
        kern, grid=(depth, nb), in_specs=in_specs, out_specs=out_specs, out_shape=out_shape,
        scratch_shapes=[pltpu.VMEM((pg * ROWS, P_COLS), f32), pltpu.VMEM((ROWS, D_MODEL), bf16),
                        pltpu.VMEM((ROWS, D_MODEL), f32), pltpu.VMEM((B * T, D_MODEL), f32),
                        pltpu.VMEM((groups, A_HEADS * T, A_KV_DIM), bf16),
                        pltpu.VMEM((A_HEADS, ROWS, WINDOW), f32),
                        pltpu.VMEM((groups, A_HEADS * T, WINDOW), bf16),
                        pltpu.VMEM((A_HEADS, ROWS, 2 * LANES), f32),
                        pltpu.VMEM((A_HEADS, ROWS, ROWS), f32), pltpu.VMEM((A_HEADS, ROWS, WINDOW), f32)],
        compiler_params=pltpu.CompilerParams(dimension_semantics=("arbitrary", "arbitrary"),
                                             vmem_limit_bytes=DECODE_VMEM_LIMIT_BYTES),
        name="sample_path",
    )(x2, p["norm_gain"], p["w_in"], p["gbias"], p["m_gain"], p["r_gain"], p["sinks"], p["w_out"], p["fgain"],
      tabs["tril3"], tabs["maskadd"], tabs["dmat2"], tabs["rslab"], ubias,
      cos_t, sin_t, tabs["selh"], tabs["selp"], c0, n0t, m0r, s0, k0r, v0r)
    y = y.reshape(B, T, D_MODEL)
    n = jnp.transpose(n, (0, 2, 1, 3))
    m = m.reshape(depth, B, T, LANES)[:, :, 0, :M_HEADS]
    k = jnp.transpose(k.reshape(depth, B, A_KV_HEADS, HEAD_DIM, WINDOW), (0, 1, 4, 2, 3))
    v = jnp.transpose(v.reshape(depth, B, A_KV_HEADS, HEAD_DIM, WINDOW), (0, 1, 4, 2, 3))
    return y, c, n, m, s, k, v


def _prepare_params(norm_gain, w_in, mlstm_gate_bias, mlstm_norm_gain, ret_norm_gain, attn_sinks, w_out,
                    final_norm_gain):
    depth = w_in.shape[0]
    w_t = jnp.swapaxes(w_in, 1, 2)
    split = OFF_G + N_GATES
    aq0 = split + 4 * R_DIM
    akv0 = aq0 + A_DIM
    az0 = akv0 + 2 * A_KV_DIM

    shift = GATE_PAD - N_GATES
    pieces = [(0, 0, OFF_G), (OFF_RQ, split, aq0 - split), (OFF_AK, akv0, az0 - akv0)]
    for dst0, src0 in ((OFF_AQ, aq0), (OFF_AZ, az0)):
        for slot, head in enumerate(ATTN_HEAD_ORDER):
            pieces.append((dst0 + slot * HEAD_DIM, src0 + head * HEAD_DIM, HEAD_DIM))
    assert OFF_AQ == aq0 + shift and OFF_AZ == az0 + shift and w_t.shape[1] == P_COLS - shift

    def pack(x_ref, o_ref):
        for dst, src, rows in pieces:
            o_ref[dst:dst + rows, :] = x_ref[src:src + rows, :].astype(bf16)
        gates = jnp.concatenate([x_ref[OFF_G:split, :], jnp.zeros((shift, x_ref.shape[1]), x_ref.dtype)], axis=0)
        o_ref[OFF_G:OFF_RQ, :] = gates.astype(bf16)

    w_in_p = pl.pallas_call(
        pack, grid=(depth, D_MODEL // PACK_COL_BLOCK),
        in_specs=[pl.BlockSpec((None, w_t.shape[1], PACK_COL_BLOCK), lambda l, c: (l, 0, c))],
        out_specs=pl.BlockSpec((None, P_COLS, PACK_COL_BLOCK), lambda l, c: (l, 0, c)),
        out_shape=jax.ShapeDtypeStruct((depth, P_COLS, D_MODEL), bf16),
        compiler_params=pltpu.CompilerParams(dimension_semantics=("arbitrary", "arbitrary")),
        name="pack_w_in",
    )(w_t)
    a0 = M_DIM + R_DIM
    out_pieces = [(0, 0, a0)] + [(a0 + slot * HEAD_DIM, a0 + head * HEAD_DIM, HEAD_DIM)
                                 for slot, head in enumerate(ATTN_HEAD_ORDER)]

    def pack_out(x_ref, o_ref):
        for dst, src, rows in out_pieces:
            o_ref[dst:dst + rows, :] = x_ref[src:src + rows, :].astype(bf16)

    w_out_p = pl.pallas_call(
        pack_out, grid=(depth, D_MODEL // PACK_COL_BLOCK),
        in_specs=[pl.BlockSpec((None, D_MODEL, PACK_COL_BLOCK), lambda l, c: (l, 0, c))],
        out_specs=pl.BlockSpec((None, D_MODEL, PACK_COL_BLOCK), lambda l, c: (l, 0, c)),
        out_shape=jax.ShapeDtypeStruct((depth, D_MODEL, D_MODEL), bf16),
        compiler_params=pltpu.CompilerParams(dimension_semantics=("arbitrary", "arbitrary")),
        name="pack_w_out",
    )(w_out)
    gbias = jnp.pad(mlstm_gate_bias.reshape(depth, 1, N_GATES), ((0, 0), (0, 0), (0, LANES - N_GATES)))
    return dict(norm_gain=norm_gain.reshape(depth, 1, D_MODEL), w_in=w_in_p, gbias=gbias,
                m_gain=mlstm_norm_gain.reshape(depth, 1, M_DIM), r_gain=ret_norm_gain.reshape(depth, 1, R_DIM),
                sinks=attn_sinks, w_out=w_out_p, fgain=final_norm_gain.reshape(1, D_MODEL))


def kernel(x_prompt, x_sample, state_mlstm_C, state_mlstm_n, state_mlstm_m, state_ret_S, cache_win_k,
           cache_win_v, norm_gain, w_in, mlstm_gate_bias, mlstm_norm_gain, ret_norm_gain, attn_sinks,
           rel_bias_table, w_out, final_norm_gain):
    depth = w_in.shape[0]
    seq = x_prompt.shape[1]
    dec_seq = x_sample.shape[1]
    past_len = seq
    p = _prepare_params(norm_gain, w_in, mlstm_gate_bias, mlstm_norm_gain, ret_norm_gain, attn_sinks, w_out,
                        final_norm_gain)
    tabs_p = _static_tables(1)
    tabs_s = _static_tables(ROWS // dec_seq)
    ubias = _bias_vectors(rel_bias_table)
    rope_p = (*_rope_tables(jnp.arange(0, seq, ROWS, dtype=jnp.int32), signed=False),
              *_rope_tables(jnp.arange(ROWS, dtype=jnp.int32), signed=False))
    cos_s, sin_s = _rope_tables(past_len + (jnp.arange(ROWS, dtype=jnp.int32) % dec_seq))
    states = (state_mlstm_C, state_mlstm_n, state_mlstm_m, state_ret_S, cache_win_k, cache_win_v)

    xp = x_prompt
    p_states = []
    for layer in range(depth):
        xp, *sp = _prompt_layer(xp, p, layer, tabs_p, ubias, rope_p, layer == depth - 1)
        p_states.append(sp)
    outs_p = [jnp.stack([p_states[l][i] for l in range(depth)]) for i in range(6)]
    xs, *outs_s = _sample_path(x_sample, states, p, tabs_s, ubias, cos_s, sin_s)
    return (xp, xs, *outs_p, *outs_s)
```
